```python
import jax, jax.numpy as jnp
from jax import lax
import numpy as np

D_MODEL = 1024
BATCH = 8
SEQ = 4096
DEPTH = 2

N_A = DEPTH // 2
N_B = DEPTH - N_A
N_META = 16
POOL_WINDOWS = (2, 4, 8, 16)
N_POOL_GROUPS = len(POOL_WINDOWS)
POOL_GROUP_DIM = D_MODEL // N_POOL_GROUPS
MAX_WINDOW = max(POOL_WINDOWS)
N_HEADS = 8
QK_NOPE_DIM = 128
QK_ROPE_DIM = 64
V_HEAD_DIM = 128
Q_LORA_RANK = 384
KV_LORA_RANK = 256
ROPE_THETA = 10000.0
ATTN_SCALE = (QK_NOPE_DIM + QK_ROPE_DIM) ** -0.5
Q_BLOCK = 128
N_EXPERT_GROUPS = 4
EXPERTS_PER_GROUP = 8
N_EXPERTS = N_EXPERT_GROUPS * EXPERTS_PER_GROUP
TOP_K_IN_GROUP = 2
D_EXPERT = 256
MOE_GROUP_ROWS = 256
RMS_EPS = 1e-6
NEG_INF = -1e30

kernel_name = "yoco_pool_mla_hier_moe"


def rmsnorm(x, g):
    xf = x.astype(jnp.float32)
    y = xf * lax.rsqrt(jnp.mean(xf * xf, axis=-1, keepdims=True) + RMS_EPS)
    return (y * g.astype(jnp.float32)).astype(x.dtype)


def rope_tables(length):
    pos = jnp.arange(length, dtype=jnp.float32)
    inv_freq = ROPE_THETA ** (-jnp.arange(0, QK_ROPE_DIM, 2, dtype=jnp.float32) / QK_ROPE_DIM)
    ang = pos[:, None] * inv_freq[None, :]
    return jnp.cos(ang), jnp.sin(ang)


def apply_rope(x, cos, sin):
    xf = x.astype(jnp.float32)
    half = xf.shape[-1] // 2
    x1, x2 = xf[..., :half], xf[..., half:]
    return jnp.concatenate([x1 * cos - x2 * sin, x1 * sin + x2 * cos], axis=-1).astype(x.dtype)


def pool_mixer(h, w_grp, scale):
    L_ = h.shape[1]
    hf = h.astype(jnp.float32)
    cs = jnp.pad(jnp.cumsum(hf, axis=1), ((0, 0), (MAX_WINDOW, 0), (0, 0)))
    t = jnp.arange(L_)
    outs = []
    for g, w in enumerate(POOL_WINDOWS):
        sl = slice(g * POOL_GROUP_DIM, (g + 1) * POOL_GROUP_DIM)
        win_sum = cs[:, MAX_WINDOW:, sl] - cs[:, MAX_WINDOW - w:MAX_WINDOW - w + L_, sl]
        cnt = jnp.minimum(t + 1, w).astype(jnp.float32)[None, :, None]
        pooled = (win_sum / cnt - hf[:, :, sl]).astype(h.dtype)
        outs.append(jnp.einsum('blc,cd->bld', pooled, w_grp[g]))
    return jnp.concatenate(outs, axis=-1) * scale


def shared_latent_kv(h, kv_norm, w_dkv, kv_lat_norm, w_uk, w_uv, cos, sin):
    c = jnp.einsum('bld,dr->blr', rmsnorm(h, kv_norm), w_dkv)
    c_kv = rmsnorm(c[..., :KV_LORA_RANK], kv_lat_norm)
    k_rope = apply_rope(c[..., KV_LORA_RANK:], cos, sin)
    k_nope = jnp.einsum('blr,rhd->blhd', c_kv, w_uk)
    v = jnp.einsum('blr,rhd->blhd', c_kv, w_uv)
    return k_nope, k_rope, v


def causal_block_attention(q_nope, q_rope, k_nope, k_rope, v):
    B_, L_, H_, _ = q_nope.shape
    nb = -(-L_ // Q_BLOCK)
    Lp = nb * Q_BLOCK
    pad = Lp - L_
    padf = lambda a: jnp.pad(a, [(0, 0), (0, pad)] + [(0, 0)] * (a.ndim - 2))
    qn, qr, kn, kr, vv = padf(q_nope), padf(q_rope), padf(k_nope), padf(k_rope), padf(v)
    to_blocks = lambda a: jnp.moveaxis(a.reshape((B_, nb, Q_BLOCK) + a.shape[2:]), 1, 0)
    kpos = jnp.arange(Lp)

    def one_block(args):
        qn_b, qr_b, b_idx = args
        s = (jnp.einsum('bqhd,bkhd->bhqk', qn_b, kn, preferred_element_type=jnp.float32)
             + jnp.einsum('bqhr,bkr->bhqk', qr_b, kr, preferred_element_type=jnp.float32)) * ATTN_SCALE
        qpos = b_idx * Q_BLOCK + jnp.arange(Q_BLOCK)
        s = jnp.where(kpos[None, :] <= qpos[:, None], s, NEG_INF)
        p = jax.nn.softmax(s, axis=-1)
        return jnp.einsum('bhqk,bkhd->bqhd', p.astype(vv.dtype), vv)

    o = lax.map(one_block, (to_blocks(qn), to_blocks(qr), jnp.arange(nb)))
    return jnp.moveaxis(o, 0, 1).reshape(B_, Lp, H_, V_HEAD_DIM)[:, :L_]


def mla_mixer(hn, w_dq, q_norm, w_uq, w_o, k_nope, k_rope, v, cos, sin):
    B_, L_, _ = hn.shape
    cq = rmsnorm(jnp.einsum('bld,dr->blr', hn, w_dq), q_norm)
    q = jnp.einsum('blr,rhd->blhd', cq, w_uq)
    q_nope = q[..., :QK_NOPE_DIM]
    q_rope = apply_rope(q[..., QK_NOPE_DIM:], cos[:, None, :], sin[:, None, :])
    o = causal_block_attention(q_nope, q_rope, k_nope, k_rope, v)
    return jnp.einsum('blf,fd->bld', o.reshape(B_, L_, N_HEADS * V_HEAD_DIM), w_o)


def hierarchical_moe(h, wr_g, br_g, wr_e, br_e, w_gate, w_up, w_down):
    B_, L_, D_ = h.shape
    xt = h.reshape(-1, D_)
    n_tok = xt.shape[0]
    xf = xt.astype(jnp.float32)
    p_g = jax.nn.softmax(xf @ wr_g.astype(jnp.float32) + br_g.astype(jnp.float32), axis=-1)
    g_sel = jnp.argmax(p_g, axis=-1)
    w_g = jnp.max(p_g, axis=-1)
    logit_e = (xf @ wr_e.astype(jnp.float32) + br_e.astype(jnp.float32)).reshape(n_tok, N_EXPERT_GROUPS, EXPERTS_PER_GROUP)
    sel = jnp.take_along_axis(logit_e, g_sel[:, None, None], axis=1)[:, 0]
    top_v, top_i = lax.top_k(sel, TOP_K_IN_GROUP)
    gate = w_g[:, None] * jax.nn.softmax(top_v, axis=-1)
    eid = (g_sel[:, None] * EXPERTS_PER_GROUP + top_i).reshape(-1).astype(jnp.int32)
    tid = jnp.repeat(jnp.arange(n_tok, dtype=jnp.int32), TOP_K_IN_GROUP)
    wt = gate.reshape(-1)
    n_assign = n_tok * TOP_K_IN_GROUP
    order = jnp.argsort(eid)
    e_sorted = eid[order]
    counts = jnp.bincount(eid, length=N_EXPERTS)
    starts = jnp.cumsum(counts) - counts
    padded = (counts + MOE_GROUP_ROWS - 1) // MOE_GROUP_ROWS * MOE_GROUP_ROWS
    pends = jnp.cumsum(padded)
    pstarts = pends - padded
    dest = pstarts[e_sorted] + (jnp.arange(n_assign) - starts[e_sorted])
    n_blocks = -(-(n_assign + N_EXPERTS * (MOE_GROUP_ROWS - 1)) // MOE_GROUP_ROWS)
    n_rows = n_blocks * MOE_GROUP_ROWS
    row_tok = jnp.full((n_rows,), n_tok, jnp.int32).at[dest].set(tid[order])
    row_w = jnp.zeros((n_rows,), h.dtype).at[dest].set(wt[order].astype(h.dtype))
    block_e = jnp.minimum(jnp.searchsorted(pends, jnp.arange(n_blocks) * MOE_GROUP_ROWS, side='right'), N_EXPERTS - 1)
    x_rows = jnp.concatenate([xt, jnp.zeros((1, D_), xt.dtype)], axis=0)[row_tok]
    x_rows = x_rows.reshape(n_blocks, MOE_GROUP_ROWS, D_)

    def expert_block(args):
        e, xb = args
        return (jax.nn.silu(xb @ w_gate[e]) * (xb @ w_up[e])) @ w_down[e]

    y = lax.map(expert_block, (block_e, x_rows)).reshape(n_rows, D_)
    out = jnp.zeros((n_tok + 1, D_), h.dtype).at[row_tok].add(y * row_w[:, None])[:n_tok]
    return out.reshape(B_, L_, D_)


def setup_inputs(seed: int = 0) -> dict:
    key = jax.random.key(seed)
    ks = jax.random.split(key, 32)
    f32 = jnp.float32
    nrm = lambda k, shape, fan: jax.random.normal(k, shape, f32) * (fan ** -0.5)
    gain = lambda k, shape: 1.0 + 0.05 * jax.random.normal(k, shape, f32)
    return {
        "x": jax.random.normal(ks[0], (BATCH, SEQ, D_MODEL), f32),
        "meta_tokens": jax.random.normal(ks[1], (N_META, D_MODEL), f32),
        "a_norm": gain(ks[2], (N_A, D_MODEL)),
        "a_w": nrm(ks[3], (N_A, N_POOL_GROUPS, POOL_GROUP_DIM, POOL_GROUP_DIM), POOL_GROUP_DIM),
        "a_scale": gain(ks[4], (N_A, D_MODEL)),
        "b_norm": gain(ks[5], (N_B, D_MODEL)),
        "b_w_dq": nrm(ks[6], (N_B, D_MODEL, Q_LORA_RANK), D_MODEL),
        "b_q_norm": gain(ks[7], (N_B, Q_LORA_RANK)),
        "b_w_uq": nrm(ks[8], (N_B, Q_LORA_RANK, N_HEADS, QK_NOPE_DIM + QK_ROPE_DIM), Q_LORA_RANK),
        "b_w_o": nrm(ks[9], (N_B, N_HEADS * V_HEAD_DIM, D_MODEL), N_HEADS * V_HEAD_DIM),
        "kv_norm": gain(ks[10], (D_MODEL,)),
        "w_dkv": nrm(ks[11], (D_MODEL, KV_LORA_RANK + QK_ROPE_DIM), D_MODEL),
        "kv_lat_norm": gain(ks[12], (KV_LORA_RANK,)),
        "w_uk": nrm(ks[13], (KV_LORA_RANK, N_HEADS, QK_NOPE_DIM), KV_LORA_RANK),
        "w_uv": nrm(ks[14], (KV_LORA_RANK, N_HEADS, V_HEAD_DIM), KV_LORA_RANK),
        "ffn_norm": gain(ks[15], (DEPTH, D_MODEL)),
        "router_g": nrm(ks[16], (DEPTH, D_MODEL, N_EXPERT_GROUPS), D_MODEL),
        "router_g_bias": 0.01 * jax.random.normal(ks[17], (DEPTH, N_EXPERT_GROUPS), f32),
        "router_e": nrm(ks[18], (DEPTH, D_MODEL, N_EXPERTS), D_MODEL),
        "router_e_bias": 0.01 * jax.random.normal(ks[19], (DEPTH, N_EXPERTS), f32),
        "w_gate": nrm(ks[20], (DEPTH, N_EXPERTS, D_MODEL, D_EXPERT), D_MODEL),
        "w_up": nrm(ks[21], (DEPTH, N_EXPERTS, D_MODEL, D_EXPERT), D_MODEL),
        "w_down": nrm(ks[22], (DEPTH, N_EXPERTS, D_EXPERT, D_MODEL), D_EXPERT),
        "final_norm": gain(ks[23], (D_MODEL,)),
    }


def reference(x, meta_tokens, a_norm, a_w, a_scale, b_norm, b_w_dq, b_q_norm, b_w_uq, b_w_o,
              kv_norm, w_dkv, kv_lat_norm, w_uk, w_uv, ffn_norm, router_g, router_g_bias,
              router_e, router_e_bias, w_gate, w_up, w_down, final_norm):
    B_, S_, D_ = x.shape
    h = jnp.concatenate([jnp.broadcast_to(meta_tokens[None].astype(x.dtype), (B_, N_META, D_)), x], axis=1)
    L_ = S_ + N_META
    cos, sin = rope_tables(L_)
    k_nope = k_rope = v = None
    for layer in range(DEPTH):
        if layer < N_A:
            i = layer
            h = h + pool_mixer(rmsnorm(h, a_norm[i]), a_w[i], a_scale[i])
        else:
            i = layer - N_A
            h = h + mla_mixer(rmsnorm(h, b_norm[i]), b_w_dq[i], b_q_norm[i], b_w_uq[i], b_w_o[i],
                              k_nope, k_rope, v, cos, sin)
        h = h + hierarchical_moe(rmsnorm(h, ffn_norm[layer]), router_g[layer], router_g_bias[layer],
                                 router_e[layer], router_e_bias[layer], w_gate[layer], w_up[layer], w_down[layer])
        if layer == N_A - 1:
            k_nope, k_rope, v = shared_latent_kv(h, kv_norm, w_dkv, kv_lat_norm, w_uk, w_uv, cos, sin)
    return rmsnorm(h, final_norm)[:, N_META:]
```

```python
import functools

import numpy as np
import jax
import jax.numpy as jnp
from jax import lax
from jax.experimental import pallas as pl
from jax.experimental.pallas import tpu as pltpu

N_META = 16
POOL_WINDOWS = (2, 4, 8, 16)
N_HEADS = 8
QK_NOPE_DIM = 128
QK_ROPE_DIM = 64
QK_DIM = QK_NOPE_DIM + QK_ROPE_DIM
V_HEAD_DIM = 128
KV_LORA_RANK = 256
ROPE_THETA = 10000.0
ATTN_SCALE = QK_DIM ** -0.5
N_EXPERT_GROUPS = 4
EXPERTS_PER_GROUP = 8
N_EXPERTS = N_EXPERT_GROUPS * EXPERTS_PER_GROUP
RMS_EPS = 1e-6
NEG_INF = -1e30

TOKEN_TILE = 256
EXPERT_ROWS = 256
ATTN_Q_TILE = 256
ATTN_K_TILE = 256
ROUTER_ROWS = 8 + N_EXPERTS
VMEM_LIMIT_BYTES = 48 * 1024 * 1024

_F32 = jnp.float32
_BF16 = jnp.bfloat16
_NT_DIMS = (((1,), (1,)), ((), ()))


def _params(n_grid_dims=1, **kw):
    return pltpu.CompilerParams(dimension_semantics=("arbitrary",) * n_grid_dims,
                                vmem_limit_bytes=VMEM_LIMIT_BYTES, **kw)


def _rms(x, g):
    ms = jnp.mean(x * x, axis=-1, keepdims=True)
    return x * lax.rsqrt(ms + RMS_EPS) * g


def _split_bf16(x):
    hi = x.astype(_BF16)
    lo = (x - hi.astype(_F32)).astype(_BF16)
    return hi, lo


def _dot(a, b):
    return jnp.dot(a, b, preferred_element_type=_F32)


def _route(xn, valid, wrt_ref, br_ref, triu_ref, base_ref, ri_ref, rf_ref, cnt_ref):
    tm = xn.shape[0]
    logits = lax.dot_general(wrt_ref[...], xn, _NT_DIMS, precision=lax.Precision.HIGHEST,
                             preferred_element_type=_F32) + br_ref[...]
    lg = logits[0:N_EXPERT_GROUPS]
    eg = jnp.exp(lg - jnp.max(lg, axis=0, keepdims=True))
    pg = eg / jnp.sum(eg, axis=0, keepdims=True)
    w_g = jnp.max(pg, axis=0, keepdims=True)
    ig = lax.broadcasted_iota(jnp.int32, pg.shape, 0).astype(_F32)
    g_sel = jnp.min(jnp.where(pg == w_g, ig, float(N_EXPERT_GROUPS)), axis=0, keepdims=True)

    sel = logits[8:8 + EXPERTS_PER_GROUP]
    for g in range(1, N_EXPERT_GROUPS):
        sel = jnp.where(g_sel == float(g), logits[8 + g * EXPERTS_PER_GROUP:8 + (g + 1) * EXPERTS_PER_GROUP], sel)
    ie = lax.broadcasted_iota(jnp.int32, sel.shape, 0).astype(_F32)
    v1 = jnp.max(sel, axis=0, keepdims=True)
    i1 = jnp.min(jnp.where(sel == v1, ie, float(EXPERTS_PER_GROUP)), axis=0, keepdims=True)
    rest = jnp.where(ie == i1, -jnp.inf, sel)
    v2 = jnp.max(rest, axis=0, keepdims=True)
    i2 = jnp.min(jnp.where(rest == v2, ie, float(EXPERTS_PER_GROUP)), axis=0, keepdims=True)
    e2 = jnp.exp(v2 - v1)
    den = 1.0 + e2
    validf = valid.astype(_F32)
    gate0 = w_g * (1.0 / den) * validf
    gate1 = w_g * (e2 / den) * validf
    eid0 = g_sel * float(EXPERTS_PER_GROUP) + i1
    eid1 = g_sel * float(EXPERTS_PER_GROUP) + i2

    iall = lax.broadcasted_iota(jnp.int32, (N_EXPERTS, tm), 0).astype(_F32)
    oh0 = jnp.where(iall == eid0, validf, 0.0)
    oh1 = jnp.where(iall == eid1, validf, 0.0)
    both = oh0 + oh1
    before = _dot(both.astype(_BF16), triu_ref[...]) + base_ref[...]
    rank0 = jnp.sum(oh0 * before, axis=0, keepdims=True)
    rank1 = jnp.sum(oh1 * before, axis=0, keepdims=True)
    base_ref[...] = base_ref[...] + jnp.sum(both, axis=1, keepdims=True)

    ri_ref[...] = jnp.zeros(ri_ref.shape, ri_ref.dtype)
    rf_ref[...] = jnp.zeros(rf_ref.shape, rf_ref.dtype)
    ri_ref[0, 0:1, :] = eid0.astype(jnp.int32)
    ri_ref[0, 1:2, :] = eid1.astype(jnp.int32)
    ri_ref[0, 2:3, :] = rank0.astype(jnp.int32)
    ri_ref[0, 3:4, :] = rank1.astype(jnp.int32)
    rf_ref[0, 0:1, :] = gate0
    rf_ref[0, 1:2, :] = gate1
    cnt_ref[...] = jnp.broadcast_to(base_ref[...], cnt_ref.shape).astype(jnp.int32)


def _mixer_kernel(x_ref, xh_ref, mp_ref, meta_ref, anorm_ref, aw_ref, ascale_ref, fnorm_ref,
                  wrt_ref, br_ref, pm_ref, ph_ref, triu_ref,
                  h_ref, xn_ref, ri_ref, rf_ref, cnt_ref, base_ref, *, n_real_tiles, tiles_per_batch):
    i = pl.program_id(0)
    tm = x_ref.shape[0]
    gd = x_ref.shape[1] // len(POOL_WINDOWS)
    is_meta = i == n_real_tiles
    first = (i % tiles_per_batch) == 0

    @pl.when(i == 0)
    def _():
        base_ref[...] = jnp.zeros(base_ref.shape, base_ref.dtype)

    h = jnp.where(is_meta, mp_ref[...], x_ref[...])
    halo = jnp.where(is_meta, 0.0, jnp.where(first, meta_ref[...], xh_ref[...]))
    hn = _rms(h, anorm_ref[...])
    hh = _rms(halo, anorm_ref[...])
    hn_hi, hn_lo = _split_bf16(hn)
    hh_hi, hh_lo = _split_bf16(hh)
    row = lax.broadcasted_iota(jnp.int32, (tm, 1), 0)
    for g, w in enumerate(POOL_WINDOWS):
        sl = slice(g * gd, (g + 1) * gd)
        win = (_dot(pm_ref[g], hn_hi[:, sl]) + _dot(pm_ref[g], hn_lo[:, sl])
               + _dot(ph_ref[g], hh_hi[:, sl]) + _dot(ph_ref[g], hh_lo[:, sl]))
        cnt = jnp.where(is_meta, jnp.minimum(row + 1, w), w).astype(_F32)
        pooled = win * (1.0 / cnt) - hn[:, sl]
        mix = _dot(pooled.astype(_BF16), aw_ref[g])
        h_ref[:, sl] = h[:, sl] + mix * ascale_ref[:, sl]

    xn = _rms(h_ref[...], fnorm_ref[...])
    xn_ref[...] = xn
    lane = lax.broadcasted_iota(jnp.int32, (1, tm), 1)
    valid = jnp.logical_or(jnp.logical_not(is_meta), lane < N_META)
    _route(xn, valid, wrt_ref, br_ref, triu_ref, base_ref, ri_ref, rf_ref, cnt_ref)


def _dispatch_kernel(dest_hbm, xn_hbm, xs_hbm, idx_smem, isem, rsem, *, tm, n_real_tiles):
    i = pl.program_id(0)
    icp = pltpu.make_async_copy(dest_hbm.at[i], idx_smem, isem)
    icp.start()
    icp.wait()
    n_rows = jnp.where(i < n_real_tiles, tm, N_META)

    def row_copy(t, k):
        d = idx_smem[0, k * tm + t]
        return pltpu.make_async_copy(xn_hbm.at[pl.ds(i * tm + t, 1), :], xs_hbm.at[pl.ds(d, 1), :], rsem)

    def issue(t, c):
        row_copy(t, 0).start()
        row_copy(t, 1).start()
        return c

    def drain(t, c):
        row_copy(t, 0).wait()
        row_copy(t, 1).wait()
        return c

    lax.fori_loop(0, n_rows, issue, 0)
    lax.fori_loop(0, n_rows, drain, 0)


def _expert_kernel(be_ref, bv_ref, nu_ref, xs_ref, wg_ref, wu_ref, wd_ref, y_ref):
    i = pl.program_id(0)

    @pl.when(i < nu_ref[0])
    def _():
        rows = lax.broadcasted_iota(jnp.int32, (xs_ref.shape[0], 1), 0)
        x = jnp.where(rows < bv_ref[i], xs_ref[...], 0.0).astype(_BF16)
        g = _dot(x, wg_ref[0])
        u = _dot(x, wu_ref[0])
        a = g * (1.0 / (1.0 + jnp.exp(-g))) * u
        y_ref[...] = _dot(a.astype(_BF16), wd_ref[0])

    @pl.when(i >= nu_ref[0])
    def _():
        y_ref[...] = jnp.zeros(y_ref.shape, y_ref.dtype)


def _combine_kernel(dest_hbm, rf_ref, h_ref, y_hbm, eye_ref, fnorm_ref, out_ref,
                    idx_smem, ybuf, isem, rsem, *, tm, final_norm):
    i = pl.program_id(0)
    icp = pltpu.make_async_copy(dest_hbm.at[i], idx_smem, isem)
    icp.start()
    icp.wait()

    def row_copy(t, k):
        d = idx_smem[0, k * tm + t]
        return pltpu.make_async_copy(y_hbm.at[pl.ds(d, 1), :], ybuf.at[pl.ds(k * tm + t, 1), :], rsem)

    def issue(t, c):
        row_copy(t, 0).start()
        row_copy(t, 1).start()
        return c

    def drain(t, c):
        row_copy(t, 0).wait()
        row_copy(t, 1).wait()
        return c

    lax.fori_loop(0, tm, issue, 0)
    gt = lax.dot_general(eye_ref[...], rf_ref[0], _NT_DIMS, precision=lax.Precision.HIGHEST,
                         preferred_element_type=_F32)
    lax.fori_loop(0, tm, drain, 0)
    out = h_ref[...] + (ybuf[0:tm, :] * gt[:, 0:1] + ybuf[tm:2 * tm, :] * gt[:, 1:2])
    if final_norm:
        out = _rms(out, fnorm_ref[...])
    out_ref[...] = out


def _rope128(x, cos_t, sin_t):
    lane = lax.broadcasted_iota(jnp.int32, (1, 128), 1)
    first_half = (lane % QK_ROPE_DIM) < (QK_ROPE_DIM // 2)
    swapped = jnp.where(first_half, pltpu.roll(x, 128 - QK_ROPE_DIM // 2, axis=1),
                        pltpu.roll(x, QK_ROPE_DIM // 2, axis=1))
    return x * cos_t + swapped * sin_t


def _proj_kernel(h_ref, cos_ref, sin_ref, kvn_ref, wdkv_ref, kvlat_ref, wuk_ref, wuv_ref,
                 bnorm_ref, wdq_ref, qnorm_ref, wuq_ref, q_ref, k_ref, v_ref):
    h = h_ref[...]
    cos_t = cos_ref[...]
    sin_t = sin_ref[...]
    c = _dot(_rms(h, kvn_ref[...]).astype(_BF16), wdkv_ref[...])
    ckv = _rms(c[:, :KV_LORA_RANK], kvlat_ref[...]).astype(_BF16)
    kr = _rope128(c[:, KV_LORA_RANK:KV_LORA_RANK + 128], cos_t, sin_t)[:, :QK_ROPE_DIM].astype(_BF16)
    kn = _dot(ckv, wuk_ref[...])
    vv = _dot(ckv, wuv_ref[...])
    cq = _rms(_dot(_rms(h, bnorm_ref[...]).astype(_BF16), wdq_ref[...]), qnorm_ref[...]).astype(_BF16)
    q = _dot(cq, wuq_ref[...])
    rope0 = N_HEADS * QK_NOPE_DIM
    for hd in range(N_HEADS):
        k_ref[hd, :, 0:QK_NOPE_DIM] = kn[:, hd * QK_NOPE_DIM:(hd + 1) * QK_NOPE_DIM].astype(_BF16)
        k_ref[hd, :, QK_NOPE_DIM:QK_DIM] = kr
        v_ref[hd] = vv[:, hd * V_HEAD_DIM:(hd + 1) * V_HEAD_DIM].astype(_BF16)
        q_ref[hd, :, 0:QK_NOPE_DIM] = (q[:, hd * QK_NOPE_DIM:(hd + 1) * QK_NOPE_DIM] * ATTN_SCALE).astype(_BF16)
        qr = _rope128(q[:, rope0 + hd * 128:rope0 + (hd + 1) * 128], cos_t, sin_t)[:, :QK_ROPE_DIM]
        q_ref[hd, :, QK_NOPE_DIM:QK_DIM] = (qr * ATTN_SCALE).astype(_BF16)


def _attn_kernel(q_ref, k_ref, v_ref, km_ref, vm_ref, o_ref, *, tq, tk):
    j = pl.program_id(2)
    q = q_ref[0]
    s0 = lax.dot_general(q, km_ref[0], _NT_DIMS, preferred_element_type=_F32)
    m = jnp.max(s0, axis=1, keepdims=True)
    p = jnp.exp(s0 - m)
    l = jnp.sum(p, axis=1, keepdims=True)
    acc = _dot(p.astype(_BF16), vm_ref[0])

    def step(kb, carry, masked):
        m, l, acc = carry
        start = pl.multiple_of(kb * tk, tk)
        s = lax.dot_general(q, k_ref[0, pl.ds(start, tk), :], _NT_DIMS, preferred_element_type=_F32)
        if masked:
            rows = j * tq + lax.broadcasted_iota(jnp.int32, (tq, tk), 0)
            cols = kb * tk + lax.broadcasted_iota(jnp.int32, (tq, tk), 1)
            s = jnp.where(cols <= rows, s, NEG_INF)
        m_new = jnp.maximum(m, jnp.max(s, axis=1, keepdims=True))
        alpha = jnp.exp(m - m_new)
        p = jnp.exp(s - m_new)
        l = alpha * l + jnp.sum(p, axis=1, keepdims=True)
        acc = alpha * acc + _dot(p.astype(_BF16), v_ref[0, pl.ds(start, tk), :])
        return m_new, l, acc

    n_full = (j * tq) // tk
    carry = lax.fori_loop(0, n_full, functools.partial(step, masked=False), (m, l, acc))
    for d in range(tq // tk):
        carry = step(n_full + d, carry, True)
    m, l, acc = carry
    o_ref[...] = (acc / l).astype(o_ref.dtype)


def _oproj_kernel(o_ref, h_ref, wo_ref, fnorm_ref, wrt_ref, br_ref, triu_ref,
                  h_out_ref, xn_ref, ri_ref, rf_ref, cnt_ref, base_ref):
    i = pl.program_id(0)

    @pl.when(i == 0)
    def _():
        base_ref[...] = jnp.zeros(base_ref.shape, base_ref.dtype)

    h = h_ref[...] + _dot(o_ref[...], wo_ref[...])
    h_out_ref[...] = h
    xn = _rms(h, fnorm_ref[...])
    xn_ref[...] = xn
    valid = jnp.ones((1, h.shape[0]), jnp.bool_)
    _route(xn, valid, wrt_ref, br_ref, triu_ref, base_ref, ri_ref, rf_ref, cnt_ref)


def _full(shape):
    nd = len(shape)
    return pl.BlockSpec(shape, lambda *_: (0,) * nd)


def _router_operands(router_g, router_g_bias, router_e, router_e_bias):
    d = router_g.shape[0]
    wrt = jnp.concatenate([router_g.T, jnp.zeros((8 - N_EXPERT_GROUPS, d), _F32), router_e.T], axis=0)
    br = jnp.concatenate([router_g_bias, jnp.zeros((8 - N_EXPERT_GROUPS,), _F32), router_e_bias])[:, None]
    return wrt.astype(_F32), br.astype(_F32)


def _route_out(n_tiles, tm):
    shapes = [jax.ShapeDtypeStruct((n_tiles, 8, tm), jnp.int32),
              jax.ShapeDtypeStruct((n_tiles, 8, tm), _F32),
              jax.ShapeDtypeStruct((N_EXPERTS, 128), jnp.int32)]
    specs = [pl.BlockSpec((1, 8, tm), lambda i: (i, 0, 0)),
             pl.BlockSpec((1, 8, tm), lambda i: (i, 0, 0)),
             pl.BlockSpec((N_EXPERTS, 128), lambda i: (0, 0))]
    return shapes, specs


def _moe(h, xn, ri, rf, counts, w_gate, w_up, w_down, n_tiles, n_real_tiles, out_tiles, final_norm_w):
    tm, tr = TOKEN_TILE, EXPERT_ROWS
    d = h.shape[1]
    n_valid = n_real_tiles * tm + (n_tiles - n_real_tiles) * N_META
    n_blocks = -(-(2 * n_valid + N_EXPERTS * (tr - 1)) // tr)
    n_rows = n_blocks * tr

    counts = counts[:, 0]
    padded = (counts + tr - 1) // tr * tr
    pends = jnp.cumsum(padded)
    pstarts = pends - padded
    n_used = (pends[-1] // tr).astype(jnp.int32).reshape(1)
    blk0 = jnp.arange(n_blocks, dtype=jnp.int32) * tr
    block_e = jnp.minimum(jnp.searchsorted(pends, blk0, side='right'), N_EXPERTS - 1).astype(jnp.int32)
    block_valid = jnp.clip(counts[block_e] - (blk0 - pstarts[block_e]), 0, tr).astype(jnp.int32)
    eid = ri[:, 0:2, :]
    dest = (pstarts[eid] + ri[:, 2:4, :]).astype(jnp.int32).reshape(n_tiles, 1, 2 * tm)

    xs = pl.pallas_call(
        functools.partial(_dispatch_kernel, tm=tm, n_real_tiles=n_real_tiles),
        grid=(n_tiles,),
        in_specs=[pl.BlockSpec(memory_space=pl.ANY), pl.BlockSpec(memory_space=pl.ANY)],
        out_specs=pl.BlockSpec(memory_space=pl.ANY),
        out_shape=jax.ShapeDtypeStruct((n_rows, d), _F32),
        scratch_shapes=[pltpu.SMEM((1, 2 * tm), jnp.int32), pltpu.SemaphoreType.DMA, pltpu.SemaphoreType.DMA],
        compiler_params=_params(1, has_side_effects=True),
        name="moe_dispatch",
    )(dest, xn)

    f = w_gate.shape[2]
    last = lambda i, be, bv, nu: jnp.minimum(i, nu[0] - 1)
    y = pl.pallas_call(
        _expert_kernel,
        grid_spec=pltpu.PrefetchScalarGridSpec(
            num_scalar_prefetch=3,
            grid=(n_blocks,),
            in_specs=[pl.BlockSpec((tr, d), lambda i, be, bv, nu: (last(i, be, bv, nu), 0)),
                      pl.BlockSpec((1, d, f), lambda i, be, bv, nu: (be[last(i, be, bv, nu)], 0, 0)),
                      pl.BlockSpec((1, d, f), lambda i, be, bv, nu: (be[last(i, be, bv, nu)], 0, 0)),
                      pl.BlockSpec((1, f, d), lambda i, be, bv, nu: (be[last(i, be, bv, nu)], 0, 0))],
            out_specs=pl.BlockSpec((tr, d), lambda i, be, bv, nu: (i, 0))),
        out_shape=jax.ShapeDtypeStruct((n_rows, d), _F32),
        compiler_params=_params(1),
        name="moe_experts",
    )(block_e, block_valid, n_used, xs, w_gate.astype(_BF16), w_up.astype(_BF16), w_down.astype(_BF16))

    eye = jnp.asarray(np.eye(tm, dtype=np.float32))
    fin = final_norm_w is not None
    fnorm = (final_norm_w if fin else jnp.ones((d,), _F32)).reshape(1, d)
    out = pl.pallas_call(
        functools.partial(_combine_kernel, tm=tm, final_norm=fin),
        grid=(out_tiles,),
        in_specs=[pl.BlockSpec(memory_space=pl.ANY),
                  pl.BlockSpec((1, 8, tm), lambda i: (i, 0, 0)),
                  pl.BlockSpec((tm, d), lambda i: (i, 0)),
                  pl.BlockSpec(memory_space=pl.ANY),
                  _full((tm, tm)), _full((1, d))],
        out_specs=pl.BlockSpec((tm, d), lambda i: (i, 0)),
        out_shape=jax.ShapeDtypeStruct((out_tiles * tm, d), _F32),
        scratch_shapes=[pltpu.SMEM((1, 2 * tm), jnp.int32), pltpu.VMEM((2 * tm, d), _F32),
                        pltpu.SemaphoreType.DMA, pltpu.SemaphoreType.DMA],
        compiler_params=_params(1),
        name="moe_combine",
    )(dest, rf, h, y, eye, fnorm)
    return out


def kernel(x, meta_tokens, a_norm, a_w, a_scale, b_norm, b_w_dq, b_q_norm, b_w_uq, b_w_o, kv_norm, w_dkv,
           kv_lat_norm, w_uk, w_uv, ffn_norm, router_g, router_g_bias, router_e, router_e_bias, w_gate, w_up,
           w_down, final_norm):
    bsz, seq, d = x.shape
    tm = TOKEN_TILE
    assert seq % tm == 0 and seq % ATTN_Q_TILE == 0 and ATTN_Q_TILE % ATTN_K_TILE == 0
    assert d % (128 * len(POOL_WINDOWS)) == 0 and N_META == max(POOL_WINDOWS) and N_META <= tm
    n_tok = bsz * seq
    n_real_tiles = n_tok // tm
    tiles_per_batch = seq // tm
    n_tiles = n_real_tiles + 1
    gd = d // len(POOL_WINDOWS)
    row = lambda v: v.reshape(1, -1).astype(_F32)

    x2 = x.reshape(n_tok, d)
    meta_pad = jnp.concatenate([meta_tokens, jnp.zeros((tm - N_META, d), x.dtype)], axis=0)

    r = np.arange(tm)[:, None]
    cidx = np.arange(tm)[None, :]
    pm = np.stack([((r - cidx >= 0) & (r - cidx < w)) for w in POOL_WINDOWS]).astype(np.float32)
    hc = np.arange(N_META)[None, :]
    ph = np.stack([(r + N_META - hc < w) for w in POOL_WINDOWS]).astype(np.float32)
    triu = (r < cidx).astype(np.float32)
    pm, ph, triu = (jnp.asarray(a, dtype=_BF16) for a in (pm, ph, triu))

    wrt0, br0 = _router_operands(router_g[0], router_g_bias[0], router_e[0], router_e_bias[0])
    route_shapes, route_specs = _route_out(n_tiles, tm)
    tile_or_last = lambda i: (jnp.minimum(i, n_real_tiles - 1), 0)
    halo_blocks = tm // N_META
    h1, xn1, ri1, rf1, cnt1 = pl.pallas_call(
        functools.partial(_mixer_kernel, n_real_tiles=n_real_tiles, tiles_per_batch=tiles_per_batch),
        grid=(n_tiles,),
        in_specs=[pl.BlockSpec((tm, d), tile_or_last),
                  pl.BlockSpec((N_META, d), lambda i: (jnp.clip(i * halo_blocks - 1, 0, n_tok // N_META - 1), 0)),
                  _full((tm, d)), _full((N_META, d)), _full((1, d)),
                  _full((len(POOL_WINDOWS), gd, gd)), _full((1, d)), _full((1, d)),
                  _full((ROUTER_ROWS, d)), _full((ROUTER_ROWS, 1)),
                  _full(pm.shape), _full(ph.shape), _full((tm, tm))],
        out_specs=[pl.BlockSpec((tm, d), lambda i: (i, 0)), pl.BlockSpec((tm, d), lambda i: (i, 0))] + route_specs,
        out_shape=[jax.ShapeDtypeStruct((n_tiles * tm, d), _F32),
                   jax.ShapeDtypeStruct((n_tiles * tm, d), _F32)] + route_shapes,
        scratch_shapes=[pltpu.VMEM((N_EXPERTS, 1), _F32)],
        compiler_params=_params(1),
        name="pool_mixer_router",
    )(x2, x2, meta_pad, meta_tokens, row(a_norm[0]), a_w[0].astype(_BF16), row(a_scale[0]), row(ffn_norm[0]),
      wrt0, br0, pm, ph, triu)

    h2 = _moe(h1, xn1, ri1, rf1, cnt1, w_gate[0], w_up[0], w_down[0], n_tiles, n_real_tiles, n_tiles, None)

    pos = jnp.concatenate([jnp.arange(seq, dtype=_F32) + N_META, jnp.arange(tm, dtype=_F32)])
    inv_freq = ROPE_THETA ** (-jnp.arange(0, QK_ROPE_DIM, 2, dtype=_F32) / QK_ROPE_DIM)
    ang = pos[:, None] * inv_freq[None, :]
    cos_t = jnp.tile(jnp.cos(ang), (1, 4))
    sin_t = jnp.tile(jnp.concatenate([-jnp.sin(ang), jnp.sin(ang)], axis=1), (1, 2))

    wdkv = jnp.concatenate([w_dkv, w_dkv[:, KV_LORA_RANK:]], axis=1).astype(_BF16)
    wuk = w_uk.reshape(KV_LORA_RANK, N_HEADS * QK_NOPE_DIM).astype(_BF16)
    wuv = w_uv.reshape(KV_LORA_RANK, N_HEADS * V_HEAD_DIM).astype(_BF16)
    wuq = b_w_uq[0]
    q_rank = wuq.shape[0]
    wuq_rope = wuq[:, :, QK_NOPE_DIM:]
    wuq = jnp.concatenate([wuq[:, :, :QK_NOPE_DIM].reshape(q_rank, -1),
                           jnp.concatenate([wuq_rope, wuq_rope], axis=2).reshape(q_rank, -1)], axis=1).astype(_BF16)
    wdq = b_w_dq[0].astype(_BF16)
    pos_tile = lambda i: (jnp.where(i < n_real_tiles, i % tiles_per_batch, tiles_per_batch), 0)
    head_tile = lambda i: (0, i, 0)
    q, k, v = pl.pallas_call(
        _proj_kernel,
        grid=(n_tiles,),
        in_specs=[pl.BlockSpec((tm, d), lambda i: (i, 0)),
                  pl.BlockSpec((tm, 128), pos_tile), pl.BlockSpec((tm, 128), pos_tile),
                  _full((1, d)), _full(wdkv.shape), _full((1, KV_LORA_RANK)), _full(wuk.shape), _full(wuv.shape),
                  _full((1, d)), _full(wdq.shape), _full((1, q_rank)), _full(wuq.shape)],
        out_specs=[pl.BlockSpec((N_HEADS, tm, QK_DIM), head_tile),
                   pl.BlockSpec((N_HEADS, tm, QK_DIM), head_tile),
                   pl.BlockSpec((N_HEADS, tm, V_HEAD_DIM), head_tile)],
        out_shape=[jax.ShapeDtypeStruct((N_HEADS, n_tiles * tm, QK_DIM), _BF16),
                   jax.ShapeDtypeStruct((N_HEADS, n_tiles * tm, QK_DIM), _BF16),
                   jax.ShapeDtypeStruct((N_HEADS, n_tiles * tm, V_HEAD_DIM), _BF16)],
        compiler_params=_params(1),
        name="latent_qkv",
    )(h2, cos_t, sin_t, row(kv_norm), wdkv, row(kv_lat_norm), wuk, wuv,
      row(b_norm[0]), wdq, row(b_q_norm[0]), wuq)

    tq, tk = ATTN_Q_TILE, ATTN_K_TILE
    n_q = seq // tq
    meta_block = n_tok // N_META
    o = pl.pallas_call(
        functools.partial(_attn_kernel, tq=tq, tk=tk),
        grid=(bsz, N_HEADS, n_q),
        in_specs=[pl.BlockSpec((1, tq, QK_DIM), lambda b, hd, j: (hd, b * n_q + j, 0)),
                  pl.BlockSpec((1, seq, QK_DIM), lambda b, hd, j: (hd, b, 0)),
                  pl.BlockSpec((1, seq, V_HEAD_DIM), lambda b, hd, j: (hd, b, 0)),
                  pl.BlockSpec((1, N_META, QK_DIM), lambda b, hd, j: (hd, meta_block, 0)),
                  pl.BlockSpec((1, N_META, V_HEAD_DIM), lambda b, hd, j: (hd, meta_block, 0))],
        out_specs=pl.BlockSpec((tq, V_HEAD_DIM), lambda b, hd, j: (b * n_q + j, hd)),
        out_shape=jax.ShapeDtypeStruct((n_tok, N_HEADS * V_HEAD_DIM), _BF16),
        compiler_params=_params(3),
        name="causal_attention",
    )(q, k, v, k, v)

    wrt1, br1 = _router_operands(router_g[1], router_g_bias[1], router_e[1], router_e_bias[1])
    route_shapes, route_specs = _route_out(n_real_tiles, tm)
    h3, xn2, ri2, rf2, cnt2 = pl.pallas_call(
        _oproj_kernel,
        grid=(n_real_tiles,),
        in_specs=[pl.BlockSpec((tm, N_HEADS * V_HEAD_DIM), lambda i: (i, 0)),
                  pl.BlockSpec((tm, d), lambda i: (i, 0)),
                  _full((N_HEADS * V_HEAD_DIM, d)), _full((1, d)),
                  _full((ROUTER_ROWS, d)), _full((ROUTER_ROWS, 1)), _full((tm, tm))],
        out_specs=[pl.BlockSpec((tm, d), lambda i: (i, 0)), pl.BlockSpec((tm, d), lambda i: (i, 0))] + route_specs,
        out_shape=[jax.ShapeDtypeStruct((n_tok, d), _F32), jax.ShapeDtypeStruct((n_tok, d), _F32)] + route_shapes,
        scratch_shapes=[pltpu.VMEM((N_EXPERTS, 1), _F32)],
        compiler_params=_params(1),
        name="attn_out_router",
    )(o, h2, b_w_o[0].astype(_BF16), row(ffn_norm[1]), wrt1, br1, triu)

    out = _moe(h3, xn2, ri2, rf2, cnt2, w_gate[1], w_up[1], w_down[1],
               n_real_tiles, n_real_tiles, n_real_tiles, final_norm)
    return out.reshape(bsz, seq, d)
```

```python
import functools

import numpy as np
import jax
import jax.numpy as jnp
from jax import lax
from jax.experimental import pallas as pl
from jax.experimental.pallas import tpu as pltpu

N_META = 16
POOL_WINDOWS = (2, 4, 8, 16)
N_HEADS = 8
QK_NOPE_DIM = 128
QK_ROPE_DIM = 64
QK_DIM = QK_NOPE_DIM + QK_ROPE_DIM
V_HEAD_DIM = 128
KV_LORA_RANK = 256
ROPE_THETA = 10000.0
ATTN_SCALE = QK_DIM ** -0.5
Q_SCALE = ATTN_SCALE * 1.4426950408889634
N_EXPERT_GROUPS = 4
EXPERTS_PER_GROUP = 8
N_EXPERTS = N_EXPERT_GROUPS * EXPERTS_PER_GROUP
RMS_EPS = 1e-6
NEG_INF = -1e30

TOKEN_TILE = 256
EXPERT_ROWS = 256
ATTN_Q_TILE = 512
ATTN_K_TILE = 512
ROUTER_ROWS = 8 + N_EXPERTS
VMEM_LIMIT_BYTES = 48 * 1024 * 1024

_F32 = jnp.float32
_BF16 = jnp.bfloat16
_NT_DIMS = (((1,), (1,)), ((), ()))


def _params(n_grid_dims=1, **kw):
    return pltpu.CompilerParams(dimension_semantics=("arbitrary",) * n_grid_dims,
                                vmem_limit_bytes=VMEM_LIMIT_BYTES, **kw)


def _rms(x, g):
    ms = jnp.mean(x * x, axis=-1, keepdims=True)
    return x * lax.rsqrt(ms + RMS_EPS) * g


def _split_bf16(x):
    hi = x.astype(_BF16)
    lo = (x - hi.astype(_F32)).astype(_BF16)
    return hi, lo


def _dot(a, b):
    return jnp.dot(a, b, preferred_element_type=_F32)


def _route(xn, valid, wrt_ref, br_ref, triu_ref, base_ref, ri_ref, rf_ref, cnt_ref):
    tm = xn.shape[0]
    logits = lax.dot_general(wrt_ref[...], xn, _NT_DIMS, precision=lax.Precision.HIGHEST,
                             preferred_element_type=_F32) + br_ref[...]
    lg = logits[0:N_EXPERT_GROUPS]
    eg = jnp.exp(lg - jnp.max(lg, axis=0, keepdims=True))
    pg = eg / jnp.sum(eg, axis=0, keepdims=True)
    w_g = jnp.max(pg, axis=0, keepdims=True)
    ig = lax.broadcasted_iota(jnp.int32, pg.shape, 0).astype(_F32)
    g_sel = jnp.min(jnp.where(pg == w_g, ig, float(N_EXPERT_GROUPS)), axis=0, keepdims=True)

    sel = logits[8:8 + EXPERTS_PER_GROUP]
    for g in range(1, N_EXPERT_GROUPS):
        sel = jnp.where(g_sel == float(g), logits[8 + g * EXPERTS_PER_GROUP:8 + (g + 1) * EXPERTS_PER_GROUP], sel)
    ie = lax.broadcasted_iota(jnp.int32, sel.shape, 0).astype(_F32)
    v1 = jnp.max(sel, axis=0, keepdims=True)
    i1 = jnp.min(jnp.where(sel == v1, ie, float(EXPERTS_PER_GROUP)), axis=0, keepdims=True)
    rest = jnp.where(ie == i1, -jnp.inf, sel)
    v2 = jnp.max(rest, axis=0, keepdims=True)
    i2 = jnp.min(jnp.where(rest == v2, ie, float(EXPERTS_PER_GROUP)), axis=0, keepdims=True)
    e2 = jnp.exp(v2 - v1)
    den = 1.0 + e2
    validf = valid.astype(_F32)
    gate0 = w_g * (1.0 / den) * validf
    gate1 = w_g * (e2 / den) * validf
    eid0 = g_sel * float(EXPERTS_PER_GROUP) + i1
    eid1 = g_sel * float(EXPERTS_PER_GROUP) + i2

    iall = lax.broadcasted_iota(jnp.int32, (N_EXPERTS, tm), 0).astype(_F32)
    oh0 = jnp.where(iall == eid0, validf, 0.0)
    oh1 = jnp.where(iall == eid1, validf, 0.0)
    both = oh0 + oh1
    before = _dot(both.astype(_BF16), triu_ref[...]) + base_ref[...]
    rank0 = jnp.sum(oh0 * before, axis=0, keepdims=True)
    rank1 = jnp.sum(oh1 * before, axis=0, keepdims=True)
    base_ref[...] = base_ref[...] + jnp.sum(both, axis=1, keepdims=True)

    ri_ref[...] = jnp.zeros(ri_ref.shape, ri_ref.dtype)
    rf_ref[...] = jnp.zeros(rf_ref.shape, rf_ref.dtype)
    ri_ref[0, 0:1, :] = eid0.astype(jnp.int32)
    ri_ref[0, 1:2, :] = eid1.astype(jnp.int32)
    ri_ref[0, 2:3, :] = rank0.astype(jnp.int32)
    ri_ref[0, 3:4, :] = rank1.astype(jnp.int32)
    rf_ref[0, 0:1, :] = gate0
    rf_ref[0, 1:2, :] = gate1
    cnt_ref[...] = jnp.broadcast_to(base_ref[...], cnt_ref.shape).astype(jnp.int32)


def _mixer_kernel(x_ref, xh_ref, mp_ref, meta_ref, anorm_ref, aw_ref, ascale_ref, fnorm_ref,
                  wrt_ref, br_ref, pm_ref, ph_ref, triu_ref,
                  h_ref, xn_ref, ri_ref, rf_ref, cnt_ref, base_ref, *, n_real_tiles, tiles_per_batch):
    i = pl.program_id(0)
    tm = x_ref.shape[0]
    gd = x_ref.shape[1] // len(POOL_WINDOWS)
    is_meta = i == n_real_tiles
    first = (i % tiles_per_batch) == 0

    @pl.when(i == 0)
    def _():
        base_ref[...] = jnp.zeros(base_ref.shape, base_ref.dtype)

    h = jnp.where(is_meta, mp_ref[...], x_ref[...])
    halo = jnp.where(is_meta, 0.0, jnp.where(first, meta_ref[...], xh_ref[...]))
    hn = _rms(h, anorm_ref[...])
    hh = _rms(halo, anorm_ref[...])
    hn_hi, hn_lo = _split_bf16(hn)
    hh_hi, hh_lo = _split_bf16(hh)
    row = lax.broadcasted_iota(jnp.int32, (tm, 1), 0)
    for g, w in enumerate(POOL_WINDOWS):
        sl = slice(g * gd, (g + 1) * gd)
        win = (_dot(pm_ref[g], hn_hi[:, sl]) + _dot(pm_ref[g], hn_lo[:, sl])
               + _dot(ph_ref[g], hh_hi[:, sl]) + _dot(ph_ref[g], hh_lo[:, sl]))
        cnt = jnp.where(is_meta, jnp.minimum(row + 1, w), w).astype(_F32)
        pooled = win * (1.0 / cnt) - hn[:, sl]
        mix = _dot(pooled.astype(_BF16), aw_ref[g])
        h_ref[:, sl] = h[:, sl] + mix * ascale_ref[:, sl]

    xn = _rms(h_ref[...], fnorm_ref[...])
    xn_ref[...] = xn
    lane = lax.broadcasted_iota(jnp.int32, (1, tm), 1)
    valid = jnp.logical_or(jnp.logical_not(is_meta), lane < N_META)
    _route(xn, valid, wrt_ref, br_ref, triu_ref, base_ref, ri_ref, rf_ref, cnt_ref)


def _dispatch_kernel(dest_hbm, xn_ref, xs_hbm, idx_smem, isem, rsem, *, tm, n_real_tiles):
    i = pl.program_id(0)
    icp = pltpu.make_async_copy(dest_hbm.at[i], idx_smem, isem)
    icp.start()
    icp.wait()

    def scatter(n_rows):
        def issue(t, c):
            for k in range(2):
                d = idx_smem[0, k * tm + t]
                pltpu.make_async_copy(xn_ref.at[pl.ds(t, 1), :], xs_hbm.at[pl.ds(d, 1), :], rsem).start()
            return c

        lax.fori_loop(0, n_rows, issue, 0, unroll=min(8, n_rows))
        for k in range(2):
            pltpu.make_async_copy(xn_ref.at[pl.ds(0, n_rows), :], xs_hbm.at[pl.ds(0, n_rows), :], rsem).wait()

    @pl.when(i < n_real_tiles)
    def _():
        scatter(tm)

    @pl.when(i >= n_real_tiles)
    def _():
        scatter(N_META)


def _expert_kernel(be_ref, bv_ref, nu_ref, xs_ref, wg_ref, wu_ref, wd_ref, y_ref):
    i = pl.program_id(0)

    @pl.when(i < nu_ref[0])
    def _():
        rows = lax.broadcasted_iota(jnp.int32, (xs_ref.shape[0], 1), 0)
        x = jnp.where(rows < bv_ref[i], xs_ref[...], 0.0).astype(_BF16)
        g = _dot(x, wg_ref[0])
        u = _dot(x, wu_ref[0])
        a = g * (1.0 / (1.0 + jnp.exp(-g))) * u
        y_ref[...] = _dot(a.astype(_BF16), wd_ref[0])

    @pl.when(i >= nu_ref[0])
    def _():
        y_ref[...] = jnp.zeros(y_ref.shape, y_ref.dtype)


def _combine_kernel(dest_hbm, rf_ref, h_ref, y_hbm, eye_ref, fnorm_ref, out_ref,
                    idx_smem, ybuf, isem, rsem, *, tm, final_norm):
    i = pl.program_id(0)
    icp = pltpu.make_async_copy(dest_hbm.at[i], idx_smem, isem)
    icp.start()
    icp.wait()

    def issue(t, c):
        for k in range(2):
            d = idx_smem[0, k * tm + t]
            pltpu.make_async_copy(y_hbm.at[pl.ds(d, 1), :], ybuf.at[pl.ds(k * tm + t, 1), :], rsem).start()
        return c

    lax.fori_loop(0, tm, issue, 0, unroll=8)
    gt = lax.dot_general(eye_ref[...], rf_ref[0], _NT_DIMS, precision=lax.Precision.HIGHEST,
                         preferred_element_type=_F32)
    pltpu.make_async_copy(y_hbm.at[pl.ds(0, 2 * tm), :], ybuf, rsem).wait()
    out = h_ref[...] + (ybuf[0:tm, :] * gt[:, 0:1] + ybuf[tm:2 * tm, :] * gt[:, 1:2])
    if final_norm:
        out = _rms(out, fnorm_ref[...])
    out_ref[...] = out


def _rope128(x, cos_t, sin_t):
    lane = lax.broadcasted_iota(jnp.int32, (1, 128), 1)
    first_half = (lane % QK_ROPE_DIM) < (QK_ROPE_DIM // 2)
    swapped = jnp.where(first_half, pltpu.roll(x, 128 - QK_ROPE_DIM // 2, axis=1),
                        pltpu.roll(x, QK_ROPE_DIM // 2, axis=1))
    return x * cos_t + swapped * sin_t


def _proj_kernel(h_ref, cos_ref, sin_ref, kvn_ref, wdkv_ref, kvlat_ref, wuk_ref, wuv_ref,
                 bnorm_ref, wdq_ref, qnorm_ref, wuq_ref, q_ref, k_ref, v_ref):
    h = h_ref[...]
    cos_t = cos_ref[...]
    sin_t = sin_ref[...]
    c = _dot(_rms(h, kvn_ref[...]).astype(_BF16), wdkv_ref[...])
    ckv = _rms(c[:, :KV_LORA_RANK], kvlat_ref[...]).astype(_BF16)
    kr = _rope128(c[:, KV_LORA_RANK:KV_LORA_RANK + 128], cos_t, sin_t)[:, :QK_ROPE_DIM].astype(_BF16)
    kn = _dot(ckv, wuk_ref[...])
    vv = _dot(ckv, wuv_ref[...])
    cq = _rms(_dot(_rms(h, bnorm_ref[...]).astype(_BF16), wdq_ref[...]), qnorm_ref[...]).astype(_BF16)
    q = _dot(cq, wuq_ref[...])
    rope0 = N_HEADS * QK_NOPE_DIM
    for hd in range(N_HEADS):
        k_ref[hd, :, 0:QK_NOPE_DIM] = kn[:, hd * QK_NOPE_DIM:(hd + 1) * QK_NOPE_DIM].astype(_BF16)
        k_ref[hd, :, QK_NOPE_DIM:QK_DIM] = kr
        v_ref[hd] = vv[:, hd * V_HEAD_DIM:(hd + 1) * V_HEAD_DIM].astype(_BF16)
        q_ref[hd, :, 0:QK_NOPE_DIM] = (q[:, hd * QK_NOPE_DIM:(hd + 1) * QK_NOPE_DIM] * Q_SCALE).astype(_BF16)
        qr = _rope128(q[:, rope0 + hd * 128:rope0 + (hd + 1) * 128], cos_t, sin_t)[:, :QK_ROPE_DIM]
        q_ref[hd, :, QK_NOPE_DIM:QK_DIM] = (qr * Q_SCALE).astype(_BF16)


def _attn_kernel(q_ref, k_ref, v_ref, km_ref, vm_ref, o_ref, *, tq, tk):
    j = pl.program_id(2)
    q = q_ref[0]
    s0 = lax.dot_general(q, km_ref[0], _NT_DIMS, preferred_element_type=_F32)
    m = jnp.max(s0, axis=1, keepdims=True)
    p = jnp.exp2(s0 - m)
    l = jnp.sum(p, axis=1, keepdims=True)
    acc = _dot(p.astype(_BF16), vm_ref[0])

    def step(kb, carry, masked):
        m, l, acc = carry
        start = pl.multiple_of(kb * tk, tk)
        s = lax.dot_general(q, k_ref[0, pl.ds(start, tk), :], _NT_DIMS, preferred_element_type=_F32)
        if masked:
            rows = j * tq + lax.broadcasted_iota(jnp.int32, (tq, tk), 0)
            cols = kb * tk + lax.broadcasted_iota(jnp.int32, (tq, tk), 1)
            s = jnp.where(cols <= rows, s, NEG_INF)
        m_new = jnp.maximum(m, jnp.max(s, axis=1, keepdims=True))
        alpha = jnp.exp2(m - m_new)
        p = jnp.exp2(s - m_new)
        l = alpha * l + jnp.sum(p, axis=1, keepdims=True)
        acc = alpha * acc + _dot(p.astype(_BF16), v_ref[0, pl.ds(start, tk), :])
        return m_new, l, acc

    n_full = (j * tq) // tk
    carry = lax.fori_loop(0, n_full, functools.partial(step, masked=False), (m, l, acc))
    for d in range(tq // tk):
        carry = step(n_full + d, carry, True)
    m, l, acc = carry
    o_ref[...] = (acc / l).astype(o_ref.dtype)


def _oproj_kernel(o_ref, h_ref, wo_ref, fnorm_ref, wrt_ref, br_ref, triu_ref,
                  h_out_ref, xn_ref, ri_ref, rf_ref, cnt_ref, base_ref):
    i = pl.program_id(0)

    @pl.when(i == 0)
    def _():
        base_ref[...] = jnp.zeros(base_ref.shape, base_ref.dtype)

    h = h_ref[...] + _dot(o_ref[...], wo_ref[...])
    h_out_ref[...] = h
    xn = _rms(h, fnorm_ref[...])
    xn_ref[...] = xn
    valid = lax.broadcasted_iota(jnp.int32, (1, h.shape[0]), 1) >= 0
    _route(xn, valid, wrt_ref, br_ref, triu_ref, base_ref, ri_ref, rf_ref, cnt_ref)


def _full(shape):
    nd = len(shape)
    return pl.BlockSpec(shape, lambda *_: (0,) * nd)


def _router_operands(router_g, router_g_bias, router_e, router_e_bias):
    d = router_g.shape[0]
    wrt = jnp.concatenate([router_g.T, jnp.zeros((8 - N_EXPERT_GROUPS, d), _F32), router_e.T], axis=0)
    br = jnp.concatenate([router_g_bias, jnp.zeros((8 - N_EXPERT_GROUPS,), _F32), router_e_bias])[:, None]
    return wrt.astype(_F32), br.astype(_F32)


def _route_out(n_tiles, tm):
    shapes = [jax.ShapeDtypeStruct((n_tiles, 8, tm), jnp.int32),
              jax.ShapeDtypeStruct((n_tiles, 8, tm), _F32),
              jax.ShapeDtypeStruct((N_EXPERTS, 128), jnp.int32)]
    specs = [pl.BlockSpec((1, 8, tm), lambda i: (i, 0, 0)),
             pl.BlockSpec((1, 8, tm), lambda i: (i, 0, 0)),
             pl.BlockSpec((N_EXPERTS, 128), lambda i: (0, 0))]
    return shapes, specs


def _moe(h, xn, ri, rf, counts, w_gate, w_up, w_down, n_tiles, n_real_tiles, out_tiles, final_norm_w):
    tm, tr = TOKEN_TILE, EXPERT_ROWS
    d = h.shape[1]
    n_valid = n_real_tiles * tm + (n_tiles - n_real_tiles) * N_META
    n_blocks = -(-(2 * n_valid + N_EXPERTS * (tr - 1)) // tr)
    n_rows = n_blocks * tr

    counts = counts[:, 0]
    padded = (counts + tr - 1) // tr * tr
    pends = jnp.cumsum(padded)
    pstarts = pends - padded
    n_used = (pends[-1] // tr).astype(jnp.int32).reshape(1)
    blk0 = jnp.arange(n_blocks, dtype=jnp.int32) * tr
    block_e = jnp.minimum(jnp.sum(blk0[:, None] >= pends[None, :], axis=1), N_EXPERTS - 1).astype(jnp.int32)
    experts = jnp.arange(N_EXPERTS, dtype=jnp.int32)
    block_oh = block_e[:, None] == experts[None, :]
    block_cnt = jnp.sum(jnp.where(block_oh, counts[None, :], 0), axis=1)
    block_start = jnp.sum(jnp.where(block_oh, pstarts[None, :], 0), axis=1)
    block_valid = jnp.clip(block_cnt - (blk0 - block_start), 0, tr).astype(jnp.int32)
    eid = ri[:, 0:2, :]
    slot0 = jnp.sum(jnp.where(eid[..., None] == experts, pstarts, 0), axis=-1)
    dest = (slot0 + ri[:, 2:4, :]).astype(jnp.int32).reshape(n_tiles, 1, 2 * tm)

    xs = pl.pallas_call(
        functools.partial(_dispatch_kernel, tm=tm, n_real_tiles=n_real_tiles),
        grid=(n_tiles,),
        in_specs=[pl.BlockSpec(memory_space=pl.ANY), pl.BlockSpec((tm, d), lambda i: (i, 0))],
        out_specs=pl.BlockSpec(memory_space=pl.ANY),
        out_shape=jax.ShapeDtypeStruct((n_rows, d), _F32),
        scratch_shapes=[pltpu.SMEM((1, 2 * tm), jnp.int32), pltpu.SemaphoreType.DMA, pltpu.SemaphoreType.DMA],
        compiler_params=_params(1, has_side_effects=True, disable_bounds_checks=True),
        name="moe_dispatch",
    )(dest, xn)

    f = w_gate.shape[2]
    last = lambda i, be, bv, nu: jnp.minimum(i, nu[0] - 1)
    y = pl.pallas_call(
        _expert_kernel,
        grid_spec=pltpu.PrefetchScalarGridSpec(
            num_scalar_prefetch=3,
            grid=(n_blocks,),
            in_specs=[pl.BlockSpec((tr, d), lambda i, be, bv, nu: (last(i, be, bv, nu), 0)),
                      pl.BlockSpec((1, d, f), lambda i, be, bv, nu: (be[last(i, be, bv, nu)], 0, 0)),
                      pl.BlockSpec((1, d, f), lambda i, be, bv, nu: (be[last(i, be, bv, nu)], 0, 0)),
                      pl.BlockSpec((1, f, d), lambda i, be, bv, nu: (be[last(i, be, bv, nu)], 0, 0))],
            out_specs=pl.BlockSpec((tr, d), lambda i, be, bv, nu: (i, 0))),
        out_shape=jax.ShapeDtypeStruct((n_rows, d), _F32),
        compiler_params=_params(1),
        name="moe_experts",
    )(block_e, block_valid, n_used, xs, w_gate.astype(_BF16), w_up.astype(_BF16), w_down.astype(_BF16))

    eye = jnp.asarray(np.eye(tm, dtype=np.float32))
    fin = final_norm_w is not None
    fnorm = (final_norm_w if fin else jnp.ones((d,), _F32)).reshape(1, d)
    out = pl.pallas_call(
        functools.partial(_combine_kernel, tm=tm, final_norm=fin),
        grid=(out_tiles,),
        in_specs=[pl.BlockSpec(memory_space=pl.ANY),
                  pl.BlockSpec((1, 8, tm), lambda i: (i, 0, 0)),
                  pl.BlockSpec((tm, d), lambda i: (i, 0)),
                  pl.BlockSpec(memory_space=pl.ANY),
                  _full((tm, tm)), _full((1, d))],
        out_specs=pl.BlockSpec((tm, d), lambda i: (i, 0)),
        out_shape=jax.ShapeDtypeStruct((out_tiles * tm, d), _F32),
        scratch_shapes=[pltpu.SMEM((1, 2 * tm), jnp.int32), pltpu.VMEM((2 * tm, d), _F32),
                        pltpu.SemaphoreType.DMA, pltpu.SemaphoreType.DMA],
        compiler_params=_params(1, disable_bounds_checks=True),
        name="moe_combine",
    )(dest, rf, h, y, eye, fnorm)
    return out


def kernel(x, meta_tokens, a_norm, a_w, a_scale, b_norm, b_w_dq, b_q_norm, b_w_uq, b_w_o, kv_norm, w_dkv,
           kv_lat_norm, w_uk, w_uv, ffn_norm, router_g, router_g_bias, router_e, router_e_bias, w_gate, w_up,
           w_down, final_norm):
    bsz, seq, d = x.shape
    tm = TOKEN_TILE
    assert seq % tm == 0 and seq % ATTN_Q_TILE == 0 and ATTN_Q_TILE % ATTN_K_TILE == 0
    assert d % (128 * len(POOL_WINDOWS)) == 0 and N_META == max(POOL_WINDOWS) and N_META <= tm
    n_tok = bsz * seq
    n_real_tiles = n_tok // tm
    tiles_per_batch = seq // tm
    n_tiles = n_real_tiles + 1
    gd = d // len(POOL_WINDOWS)
    row = lambda v: v.reshape(1, -1).astype(_F32)

    x2 = x.reshape(n_tok, d)
    meta_pad = jnp.concatenate([meta_tokens, jnp.zeros((tm - N_META, d), x.dtype)], axis=0)

    r = np.arange(tm)[:, None]
    cidx = np.arange(tm)[None, :]
    pm = np.stack([((r - cidx >= 0) & (r - cidx < w)) for w in POOL_WINDOWS]).astype(np.float32)
    hc = np.arange(N_META)[None, :]
    ph = np.stack([(r + N_META - hc < w) for w in POOL_WINDOWS]).astype(np.float32)
    triu = (r < cidx).astype(np.float32)
    pm, ph, triu = (jnp.asarray(a, dtype=_BF16) for a in (pm, ph, triu))

    wrt0, br0 = _router_operands(router_g[0], router_g_bias[0], router_e[0], router_e_bias[0])
    route_shapes, route_specs = _route_out(n_tiles, tm)
    tile_or_last = lambda i: (jnp.minimum(i, n_real_tiles - 1), 0)
    halo_blocks = tm // N_META
    h1, xn1, ri1, rf1, cnt1 = pl.pallas_call(
        functools.partial(_mixer_kernel, n_real_tiles=n_real_tiles, tiles_per_batch=tiles_per_batch),
        grid=(n_tiles,),
        in_specs=[pl.BlockSpec((tm, d), tile_or_last),
                  pl.BlockSpec((N_META, d), lambda i: (jnp.clip(i * halo_blocks - 1, 0, n_tok // N_META - 1), 0)),
                  _full((tm, d)), _full((N_META, d)), _full((1, d)),
                  _full((len(POOL_WINDOWS), gd, gd)), _full((1, d)), _full((1, d)),
                  _full((ROUTER_ROWS, d)), _full((ROUTER_ROWS, 1)),
                  _full(pm.shape), _full(ph.shape), _full((tm, tm))],
        out_specs=[pl.BlockSpec((tm, d), lambda i: (i, 0)), pl.BlockSpec((tm, d), lambda i: (i, 0))] + route_specs,
        out_shape=[jax.ShapeDtypeStruct((n_tiles * tm, d), _F32),
                   jax.ShapeDtypeStruct((n_tiles * tm, d), _F32)] + route_shapes,
        scratch_shapes=[pltpu.VMEM((N_EXPERTS, 1), _F32)],
        compiler_params=_params(1),
        name="pool_mixer_router",
    )(x2, x2, meta_pad, meta_tokens, row(a_norm[0]), a_w[0].astype(_BF16), row(a_scale[0]), row(ffn_norm[0]),
      wrt0, br0, pm, ph, triu)

    h2 = _moe(h1, xn1, ri1, rf1, cnt1, w_gate[0], w_up[0], w_down[0], n_tiles, n_real_tiles, n_tiles, None)

    pos = jnp.concatenate([jnp.arange(seq, dtype=_F32) + N_META, jnp.arange(tm, dtype=_F32)])
    inv_freq = ROPE_THETA ** (-jnp.arange(0, QK_ROPE_DIM, 2, dtype=_F32) / QK_ROPE_DIM)
    ang = pos[:, None] * inv_freq[None, :]
    cos_t = jnp.tile(jnp.cos(ang), (1, 4))
    sin_t = jnp.tile(jnp.concatenate([-jnp.sin(ang), jnp.sin(ang)], axis=1), (1, 2))

    wdkv = jnp.concatenate([w_dkv, w_dkv[:, KV_LORA_RANK:]], axis=1).astype(_BF16)
    wuk = w_uk.reshape(KV_LORA_RANK, N_HEADS * QK_NOPE_DIM).astype(_BF16)
    wuv = w_uv.reshape(KV_LORA_RANK, N_HEADS * V_HEAD_DIM).astype(_BF16)
    wuq = b_w_uq[0]
    q_rank = wuq.shape[0]
    wuq_rope = wuq[:, :, QK_NOPE_DIM:]
    wuq = jnp.concatenate([wuq[:, :, :QK_NOPE_DIM].reshape(q_rank, -1),
                           jnp.concatenate([wuq_rope, wuq_rope], axis=2).reshape(q_rank, -1)], axis=1).astype(_BF16)
    wdq = b_w_dq[0].astype(_BF16)
    pos_tile = lambda i: (jnp.where(i < n_real_tiles, i % tiles_per_batch, tiles_per_batch), 0)
    head_tile = lambda i: (0, i, 0)
    q, k, v = pl.pallas_call(
        _proj_kernel,
        grid=(n_tiles,),
        in_specs=[pl.BlockSpec((tm, d), lambda i: (i, 0)),
                  pl.BlockSpec((tm, 128), pos_tile), pl.BlockSpec((tm, 128), pos_tile),
                  _full((1, d)), _full(wdkv.shape), _full((1, KV_LORA_RANK)), _full(wuk.shape), _full(wuv.shape),
                  _full((1, d)), _full(wdq.shape), _full((1, q_rank)), _full(wuq.shape)],
        out_specs=[pl.BlockSpec((N_HEADS, tm, QK_DIM), head_tile),
                   pl.BlockSpec((N_HEADS, tm, QK_DIM), head_tile),
                   pl.BlockSpec((N_HEADS, tm, V_HEAD_DIM), head_tile)],
        out_shape=[jax.ShapeDtypeStruct((N_HEADS, n_tiles * tm, QK_DIM), _BF16),
                   jax.ShapeDtypeStruct((N_HEADS, n_tiles * tm, QK_DIM), _BF16),
                   jax.ShapeDtypeStruct((N_HEADS, n_tiles * tm, V_HEAD_DIM), _BF16)],
        compiler_params=_params(1),
        name="latent_qkv",
    )(h2, cos_t, sin_t, row(kv_norm), wdkv, row(kv_lat_norm), wuk, wuv,
      row(b_norm[0]), wdq, row(b_q_norm[0]), wuq)

    tq, tk = ATTN_Q_TILE, ATTN_K_TILE
    n_q = seq // tq
    meta_block = n_tok // N_META
    o = pl.pallas_call(
        functools.partial(_attn_kernel, tq=tq, tk=tk),
        grid=(bsz, N_HEADS, n_q),
        in_specs=[pl.BlockSpec((1, tq, QK_DIM), lambda b, hd, j: (hd, b * n_q + j, 0)),
                  pl.BlockSpec((1, seq, QK_DIM), lambda b, hd, j: (hd, b, 0)),
                  pl.BlockSpec((1, seq, V_HEAD_DIM), lambda b, hd, j: (hd, b, 0)),
                  pl.BlockSpec((1, N_META, QK_DIM), lambda b, hd, j: (hd, meta_block, 0)),
                  pl.BlockSpec((1, N_META, V_HEAD_DIM), lambda b, hd, j: (hd, meta_block, 0))],
        out_specs=pl.BlockSpec((tq, V_HEAD_DIM), lambda b, hd, j: (b * n_q + j, hd)),
        out_shape=jax.ShapeDtypeStruct((n_tok, N_HEADS * V_HEAD_DIM), _BF16),
        compiler_params=_params(3),
        name="causal_attention",
    )(q, k, v, k, v)

    wrt1, br1 = _router_operands(router_g[1], router_g_bias[1], router_e[1], router_e_bias[1])
    route_shapes, route_specs = _route_out(n_real_tiles, tm)
    h3, xn2, ri2, rf2, cnt2 = pl.pallas_call(
        _oproj_kernel,
        grid=(n_real_tiles,),
        in_specs=[pl.BlockSpec((tm, N_HEADS * V_HEAD_DIM), lambda i: (i, 0)),
                  pl.BlockSpec((tm, d), lambda i: (i, 0)),
                  _full((N_HEADS * V_HEAD_DIM, d)), _full((1, d)),
                  _full((ROUTER_ROWS, d)), _full((ROUTER_ROWS, 1)), _full((tm, tm))],
        out_specs=[pl.BlockSpec((tm, d), lambda i: (i, 0)), pl.BlockSpec((tm, d), lambda i: (i, 0))] + route_specs,
        out_shape=[jax.ShapeDtypeStruct((n_tok, d), _F32), jax.ShapeDtypeStruct((n_tok, d), _F32)] + route_shapes,
        scratch_shapes=[pltpu.VMEM((N_EXPERTS, 1), _F32)],
        compiler_params=_params(1),
        name="attn_out_router",
    )(o, h2, b_w_o[0].astype(_BF16), row(ffn_norm[1]), wrt1, br1, triu)

    out = _moe(h3, xn2, ri2, rf2, cnt2, w_gate[1], w_up[1], w_down[1],
               n_real_tiles, n_real_tiles, n_real_tiles, final_norm)
    return out.reshape(bsz, seq, d)
```

```python
import functools

import numpy as np
import jax
import jax.numpy as jnp
from jax import lax
from jax.experimental import pallas as pl
from jax.experimental.pallas import tpu as pltpu

N_META = 16
POOL_WINDOWS = (2, 4, 8, 16)
N_HEADS = 8
QK_NOPE_DIM = 128
QK_ROPE_DIM = 64
QK_DIM = QK_NOPE_DIM + QK_ROPE_DIM
V_HEAD_DIM = 128
KV_LORA_RANK = 256
ROPE_THETA = 10000.0
ATTN_SCALE = QK_DIM ** -0.5
Q_SCALE = ATTN_SCALE * 1.4426950408889634
N_EXPERT_GROUPS = 4
EXPERTS_PER_GROUP = 8
N_EXPERTS = N_EXPERT_GROUPS * EXPERTS_PER_GROUP
RMS_EPS = 1e-6
NEG_INF = -1e30

TOKEN_TILE = 256
EXPERT_ROWS = 256
ATTN_Q_TILE = 512
ATTN_K_TILE = ATTN_Q_TILE // 2
ATTN_HEADS_PER_STEP = 4
ROUTER_ROWS = 8 + N_EXPERTS
ROUTER_LO_ROW = 48
ROUTER_SPLIT_ROWS = 2 * ROUTER_LO_ROW
VMEM_LIMIT_BYTES = 48 * 1024 * 1024

_F32 = jnp.float32
_BF16 = jnp.bfloat16
_NT_DIMS = (((1,), (1,)), ((), ()))


def _params(n_grid_dims=1, **kw):
    return pltpu.CompilerParams(dimension_semantics=("arbitrary",) * n_grid_dims,
                                vmem_limit_bytes=VMEM_LIMIT_BYTES, **kw)


def _rms(x, g):
    ms = jnp.mean(x * x, axis=-1, keepdims=True)
    return x * lax.rsqrt(ms + RMS_EPS) * g


def _split_bf16(x):
    hi = x.astype(_BF16)
    lo = (x - hi.astype(_F32)).astype(_BF16)
    return hi, lo


def _dot(a, b):
    return jnp.dot(a, b, preferred_element_type=_F32)


def _route_init(wrt_ref, wsplit_ref, base_ref):
    base_ref[...] = jnp.zeros(base_ref.shape, base_ref.dtype)
    w_hi, w_lo = _split_bf16(wrt_ref[...])
    wsplit_ref[...] = jnp.zeros(wsplit_ref.shape, wsplit_ref.dtype)
    wsplit_ref[0:ROUTER_ROWS, :] = w_hi
    wsplit_ref[ROUTER_LO_ROW:ROUTER_LO_ROW + ROUTER_ROWS, :] = w_lo


def _route(xn, valid, wsplit_ref, br_ref, triu_ref, base_ref, ri_ref, rf_ref, cnt_ref):
    tm = xn.shape[0]
    x_hi, x_lo = _split_bf16(xn)
    by_hi = lax.dot_general(wsplit_ref[...], x_hi, _NT_DIMS, preferred_element_type=_F32)
    by_lo = lax.dot_general(wsplit_ref[0:ROUTER_LO_ROW, :], x_lo, _NT_DIMS, preferred_element_type=_F32)
    logits = (by_hi[0:ROUTER_ROWS] + by_hi[ROUTER_LO_ROW:ROUTER_LO_ROW + ROUTER_ROWS]
              + by_lo[0:ROUTER_ROWS] + br_ref[...])
    lg = logits[0:N_EXPERT_GROUPS]
    eg = jnp.exp(lg - jnp.max(lg, axis=0, keepdims=True))
    pg = eg / jnp.sum(eg, axis=0, keepdims=True)
    w_g = jnp.max(pg, axis=0, keepdims=True)
    ig = lax.broadcasted_iota(jnp.int32, pg.shape, 0).astype(_F32)
    g_sel = jnp.min(jnp.where(pg == w_g, ig, float(N_EXPERT_GROUPS)), axis=0, keepdims=True)

    sel = logits[8:8 + EXPERTS_PER_GROUP]
    for g in range(1, N_EXPERT_GROUPS):
        sel = jnp.where(g_sel == float(g), logits[8 + g * EXPERTS_PER_GROUP:8 + (g + 1) * EXPERTS_PER_GROUP], sel)
    ie = lax.broadcasted_iota(jnp.int32, sel.shape, 0).astype(_F32)
    v1 = jnp.max(sel, axis=0, keepdims=True)
    i1 = jnp.min(jnp.where(sel == v1, ie, float(EXPERTS_PER_GROUP)), axis=0, keepdims=True)
    rest = jnp.where(ie == i1, -jnp.inf, sel)
    v2 = jnp.max(rest, axis=0, keepdims=True)
    i2 = jnp.min(jnp.where(rest == v2, ie, float(EXPERTS_PER_GROUP)), axis=0, keepdims=True)
    e2 = jnp.exp(v2 - v1)
    den = 1.0 + e2
    validf = valid.astype(_F32)
    gate0 = w_g * (1.0 / den) * validf
    gate1 = w_g * (e2 / den) * validf
    eid0 = g_sel * float(EXPERTS_PER_GROUP) + i1
    eid1 = g_sel * float(EXPERTS_PER_GROUP) + i2

    iall = lax.broadcasted_iota(jnp.int32, (N_EXPERTS, tm), 0).astype(_F32)
    oh0 = jnp.where(iall == eid0, validf, 0.0)
    oh1 = jnp.where(iall == eid1, validf, 0.0)
    both = oh0 + oh1
    before = _dot(both.astype(_BF16), triu_ref[...]) + base_ref[...]
    rank0 = jnp.sum(oh0 * before, axis=0, keepdims=True)
    rank1 = jnp.sum(oh1 * before, axis=0, keepdims=True)
    base_ref[...] = base_ref[...] + jnp.sum(both, axis=1, keepdims=True)

    ri_ref[...] = jnp.zeros(ri_ref.shape, ri_ref.dtype)
    rf_ref[...] = jnp.zeros(rf_ref.shape, rf_ref.dtype)
    ri_ref[0, 0:1, :] = eid0.astype(jnp.int32)
    ri_ref[0, 1:2, :] = eid1.astype(jnp.int32)
    ri_ref[0, 2:3, :] = rank0.astype(jnp.int32)
    ri_ref[0, 3:4, :] = rank1.astype(jnp.int32)
    rf_ref[0, 0:1, :] = gate0
    rf_ref[0, 1:2, :] = gate1
    cnt_ref[...] = jnp.broadcast_to(base_ref[...], cnt_ref.shape).astype(jnp.int32)


def _mixer_kernel(x_ref, xh_ref, mp_ref, meta_ref, anorm_ref, aw_ref, ascale_ref, fnorm_ref,
                  wrt_ref, br_ref, pm_ref, ph_ref, triu_ref,
                  h_ref, xn_ref, ri_ref, rf_ref, cnt_ref, base_ref, wsplit_ref, *, n_real_tiles, tiles_per_batch):
    i = pl.program_id(0)
    tm = x_ref.shape[0]
    gd = x_ref.shape[1] // len(POOL_WINDOWS)
    is_meta = i == n_real_tiles
    first = (i % tiles_per_batch) == 0

    @pl.when(i == 0)
    def _():
        _route_init(wrt_ref, wsplit_ref, base_ref)

    h = jnp.where(is_meta, mp_ref[...], x_ref[...])
    halo = jnp.where(is_meta, 0.0, jnp.where(first, meta_ref[...], xh_ref[...]))
    hn = _rms(h, anorm_ref[...])
    hh = _rms(halo, anorm_ref[...])
    hn_hi, hn_lo = _split_bf16(hn)
    hh_hi, hh_lo = _split_bf16(hh)
    row = lax.broadcasted_iota(jnp.int32, (tm, 1), 0)
    for g, w in enumerate(POOL_WINDOWS):
        sl = slice(g * gd, (g + 1) * gd)
        win = (_dot(pm_ref[g], hn_hi[:, sl]) + _dot(pm_ref[g], hn_lo[:, sl])
               + _dot(ph_ref[g], hh_hi[:, sl]) + _dot(ph_ref[g], hh_lo[:, sl]))
        cnt = jnp.where(is_meta, jnp.minimum(row + 1, w), w).astype(_F32)
        pooled = win * (1.0 / cnt) - hn[:, sl]
        mix = _dot(pooled.astype(_BF16), aw_ref[g])
        h_ref[:, sl] = h[:, sl] + mix * ascale_ref[:, sl]

    xn = _rms(h_ref[...], fnorm_ref[...])
    xn_ref[...] = xn
    lane = lax.broadcasted_iota(jnp.int32, (1, tm), 1)
    valid = jnp.logical_or(jnp.logical_not(is_meta), lane < N_META)
    _route(xn, valid, wsplit_ref, br_ref, triu_ref, base_ref, ri_ref, rf_ref, cnt_ref)


def _dispatch_kernel(dest_hbm, xn_ref, xs_hbm, idx_smem, isem, rsem, *, tm, n_real_tiles):
    i = pl.program_id(0)
    icp = pltpu.make_async_copy(dest_hbm.at[i], idx_smem, isem)
    icp.start()
    icp.wait()

    def scatter(n_rows):
        def issue(t, c):
            for k in range(2):
                d = idx_smem[0, k * tm + t]
                pltpu.make_async_copy(xn_ref.at[pl.ds(t, 1), :], xs_hbm.at[pl.ds(d, 1), :], rsem).start()
            return c

        lax.fori_loop(0, n_rows, issue, 0, unroll=min(8, n_rows))
        for k in range(2):
            pltpu.make_async_copy(xn_ref.at[pl.ds(0, n_rows), :], xs_hbm.at[pl.ds(0, n_rows), :], rsem).wait()

    @pl.when(i < n_real_tiles)
    def _():
        scatter(tm)

    @pl.when(i >= n_real_tiles)
    def _():
        scatter(N_META)


def _expert_kernel(be_ref, bv_ref, nu_ref, xs_ref, wg_ref, wu_ref, wd_ref, y_ref, wgb_ref, wub_ref, wdb_ref):
    i = pl.program_id(0)
    active = i < nu_ref[0]
    new_expert = jnp.logical_or(i == 0, be_ref[i] != be_ref[jnp.maximum(i - 1, 0)])

    @pl.when(jnp.logical_and(active, new_expert))
    def _():
        wgb_ref[...] = wg_ref[0].astype(_BF16)
        wub_ref[...] = wu_ref[0].astype(_BF16)
        wdb_ref[...] = wd_ref[0].astype(_BF16)

    @pl.when(active)
    def _():
        rows = lax.broadcasted_iota(jnp.int32, (xs_ref.shape[0], 1), 0)
        x = jnp.where(rows < bv_ref[i], xs_ref[...], 0.0).astype(_BF16)
        g = _dot(x, wgb_ref[...])
        u = _dot(x, wub_ref[...])
        a = g * (1.0 / (1.0 + jnp.exp(-g))) * u
        y_ref[...] = _dot(a.astype(_BF16), wdb_ref[...])

    @pl.when(i >= nu_ref[0])
    def _():
        y_ref[...] = jnp.zeros(y_ref.shape, y_ref.dtype)


def _combine_kernel(dest_hbm, rf_ref, h_ref, y_hbm, eye_ref, fnorm_ref, out_ref,
                    idx_smem, ybuf, isem, rsem, *, tm, final_norm):
    i = pl.program_id(0)
    icp = pltpu.make_async_copy(dest_hbm.at[i], idx_smem, isem)
    icp.start()
    icp.wait()

    def issue(t, c):
        for k in range(2):
            d = idx_smem[0, k * tm + t]
            pltpu.make_async_copy(y_hbm.at[pl.ds(d, 1), :], ybuf.at[pl.ds(k * tm + t, 1), :], rsem).start()
        return c

    lax.fori_loop(0, tm, issue, 0, unroll=8)
    gt = lax.dot_general(eye_ref[...], rf_ref[0], _NT_DIMS, precision=lax.Precision.HIGHEST,
                         preferred_element_type=_F32)
    pltpu.make_async_copy(y_hbm.at[pl.ds(0, 2 * tm), :], ybuf, rsem).wait()
    out = h_ref[...] + (ybuf[0:tm, :] * gt[:, 0:1] + ybuf[tm:2 * tm, :] * gt[:, 1:2])
    if final_norm:
        out = _rms(out, fnorm_ref[...])
    out_ref[...] = out


def _rope128(x, cos_t, sin_t):
    lane = lax.broadcasted_iota(jnp.int32, (1, 128), 1)
    first_half = (lane % QK_ROPE_DIM) < (QK_ROPE_DIM // 2)
    swapped = jnp.where(first_half, pltpu.roll(x, 128 - QK_ROPE_DIM // 2, axis=1),
                        pltpu.roll(x, QK_ROPE_DIM // 2, axis=1))
    return x * cos_t + swapped * sin_t


def _proj_kernel(h_ref, cos_ref, sin_ref, kvn_ref, wdkv_ref, kvlat_ref, wuk_ref, wuv_ref,
                 bnorm_ref, wdq_ref, qnorm_ref, wuq_ref, q_ref, k_ref, v_ref):
    h = h_ref[...]
    cos_t = cos_ref[...]
    sin_t = sin_ref[...]
    c = _dot(_rms(h, kvn_ref[...]).astype(_BF16), wdkv_ref[...])
    ckv = _rms(c[:, :KV_LORA_RANK], kvlat_ref[...]).astype(_BF16)
    kr = _rope128(c[:, KV_LORA_RANK:KV_LORA_RANK + 128], cos_t, sin_t)[:, :QK_ROPE_DIM].astype(_BF16)
    kn = _dot(ckv, wuk_ref[...])
    vv = _dot(ckv, wuv_ref[...])
    cq = _rms(_dot(_rms(h, bnorm_ref[...]).astype(_BF16), wdq_ref[...]), qnorm_ref[...]).astype(_BF16)
    q = _dot(cq, wuq_ref[...])
    rope0 = N_HEADS * QK_NOPE_DIM
    for hd in range(N_HEADS):
        k_ref[hd, :, 0:QK_NOPE_DIM] = kn[:, hd * QK_NOPE_DIM:(hd + 1) * QK_NOPE_DIM].astype(_BF16)
        k_ref[hd, :, QK_NOPE_DIM:QK_DIM] = kr
        v_ref[hd, 0] = vv[:, hd * V_HEAD_DIM:(hd + 1) * V_HEAD_DIM].T.astype(_BF16)
        q_ref[hd, :, 0:QK_NOPE_DIM] = (q[:, hd * QK_NOPE_DIM:(hd + 1) * QK_NOPE_DIM] * Q_SCALE).astype(_BF16)
        qr = _rope128(q[:, rope0 + hd * 128:rope0 + (hd + 1) * 128], cos_t, sin_t)[:, :QK_ROPE_DIM]
        q_ref[hd, :, QK_NOPE_DIM:QK_DIM] = (qr * Q_SCALE).astype(_BF16)


def _attn_kernel(q_ref, k_ref, vt_ref, km_ref, vmt_ref, o_ref, sa_ref, sb_ref, acc_ref, *, tq, tk):
    j = pl.program_id(2)
    heads = range(q_ref.shape[0])
    vd = acc_ref.shape[1]

    def scores(kb, s_ref):
        start = pl.multiple_of(kb * tk, tk)
        for hd in heads:
            s_ref[hd] = lax.dot_general(k_ref[hd, pl.ds(start, tk), :], q_ref[hd], _NT_DIMS,
                                        preferred_element_type=_F32)

    def update(kb, s_ref, carry, masked):
        out = []
        for hd in heads:
            m, l = carry[hd]
            s = s_ref[hd]
            if masked:
                keys = kb * tk + lax.broadcasted_iota(jnp.int32, (tk, tq), 0)
                qpos = j * tq + lax.broadcasted_iota(jnp.int32, (tk, tq), 1)
                s = jnp.where(keys <= qpos, s, NEG_INF)
            m_new = jnp.maximum(m, jnp.max(s, axis=0, keepdims=True))
            alpha = jnp.exp2(m - m_new)
            p = jnp.exp2(s - m_new)
            l = alpha * l + jnp.sum(p, axis=0, keepdims=True)
            acc_ref[hd] = acc_ref[hd] * alpha + _dot(vt_ref[hd, kb], p.astype(_BF16))
            out.append((m_new, l))
        return tuple(out)

    scores(0, sa_ref)
    carry = []
    for hd in heads:
        s0 = lax.dot_general(km_ref[hd], q_ref[hd], _NT_DIMS, preferred_element_type=_F32)
        m = jnp.max(s0, axis=0, keepdims=True)
        p0 = jnp.exp2(s0 - m)
        carry.append((m, jnp.sum(p0, axis=0, keepdims=True)))
        acc_ref[hd] = _dot(vmt_ref[hd, 0, :, 0:N_META], p0.astype(_BF16))
    carry = tuple(carry)

    def pair(kp, carry):
        scores(2 * kp + 1, sb_ref)
        carry = update(2 * kp, sa_ref, carry, False)
        scores(2 * kp + 2, sa_ref)
        return update(2 * kp + 1, sb_ref, carry, False)

    carry = lax.fori_loop(0, j, pair, carry)
    scores(2 * j + 1, sb_ref)
    carry = update(2 * j, sa_ref, carry, True)
    carry = update(2 * j + 1, sb_ref, carry, True)
    for hd in heads:
        o_ref[:, hd * vd:(hd + 1) * vd] = (acc_ref[hd] / carry[hd][1]).T.astype(o_ref.dtype)


def _oproj_kernel(o_ref, h_ref, wo_ref, fnorm_ref, wrt_ref, br_ref, triu_ref,
                  h_out_ref, xn_ref, ri_ref, rf_ref, cnt_ref, base_ref, wsplit_ref):
    i = pl.program_id(0)

    @pl.when(i == 0)
    def _():
        _route_init(wrt_ref, wsplit_ref, base_ref)

    h = h_ref[...] + _dot(o_ref[...], wo_ref[...])
    h_out_ref[...] = h
    xn = _rms(h, fnorm_ref[...])
    xn_ref[...] = xn
    valid = lax.broadcasted_iota(jnp.int32, (1, h.shape[0]), 1) >= 0
    _route(xn, valid, wsplit_ref, br_ref, triu_ref, base_ref, ri_ref, rf_ref, cnt_ref)


def _full(shape):
    nd = len(shape)
    return pl.BlockSpec(shape, lambda *_: (0,) * nd)


def _router_operands(router_g, router_g_bias, router_e, router_e_bias):
    d = router_g.shape[0]
    wrt = jnp.concatenate([router_g.T, jnp.zeros((8 - N_EXPERT_GROUPS, d), _F32), router_e.T], axis=0)
    br = jnp.concatenate([router_g_bias, jnp.zeros((8 - N_EXPERT_GROUPS,), _F32), router_e_bias])[:, None]
    return wrt.astype(_F32), br.astype(_F32)


def _route_out(n_tiles, tm):
    shapes = [jax.ShapeDtypeStruct((n_tiles, 8, tm), jnp.int32),
              jax.ShapeDtypeStruct((n_tiles, 8, tm), _F32),
              jax.ShapeDtypeStruct((N_EXPERTS, 128), jnp.int32)]
    specs = [pl.BlockSpec((1, 8, tm), lambda i: (i, 0, 0)),
             pl.BlockSpec((1, 8, tm), lambda i: (i, 0, 0)),
             pl.BlockSpec((N_EXPERTS, 128), lambda i: (0, 0))]
    return shapes, specs


def _moe(h, xn, ri, rf, counts, w_gate, w_up, w_down, n_tiles, n_real_tiles, out_tiles, final_norm_w):
    tm, tr = TOKEN_TILE, EXPERT_ROWS
    d = h.shape[1]
    n_valid = n_real_tiles * tm + (n_tiles - n_real_tiles) * N_META
    n_blocks = -(-(2 * n_valid + N_EXPERTS * (tr - 1)) // tr)
    n_rows = n_blocks * tr

    counts = counts[:, 0]
    padded = (counts + tr - 1) // tr * tr
    pends = jnp.cumsum(padded)
    pstarts = pends - padded
    n_used = (pends[-1] // tr).astype(jnp.int32).reshape(1)
    blk0 = jnp.arange(n_blocks, dtype=jnp.int32) * tr
    block_e = jnp.minimum(jnp.sum(blk0[:, None] >= pends[None, :], axis=1), N_EXPERTS - 1).astype(jnp.int32)
    experts = jnp.arange(N_EXPERTS, dtype=jnp.int32)
    block_oh = block_e[:, None] == experts[None, :]
    block_cnt = jnp.sum(jnp.where(block_oh, counts[None, :], 0), axis=1)
    block_start = jnp.sum(jnp.where(block_oh, pstarts[None, :], 0), axis=1)
    block_valid = jnp.clip(block_cnt - (blk0 - block_start), 0, tr).astype(jnp.int32)
    eid = ri[:, 0:2, :]
    slot0 = jnp.sum(jnp.where(eid[..., None] == experts, pstarts, 0), axis=-1)
    dest = (slot0 + ri[:, 2:4, :]).astype(jnp.int32).reshape(n_tiles, 1, 2 * tm)

    xs = pl.pallas_call(
        functools.partial(_dispatch_kernel, tm=tm, n_real_tiles=n_real_tiles),
        grid=(n_tiles,),
        in_specs=[pl.BlockSpec(memory_space=pl.ANY), pl.BlockSpec((tm, d), lambda i: (i, 0))],
        out_specs=pl.BlockSpec(memory_space=pl.ANY),
        out_shape=jax.ShapeDtypeStruct((n_rows, d), _F32),
        scratch_shapes=[pltpu.SMEM((1, 2 * tm), jnp.int32), pltpu.SemaphoreType.DMA, pltpu.SemaphoreType.DMA],
        compiler_params=_params(1, has_side_effects=True, disable_bounds_checks=True),
        name="moe_dispatch",
    )(dest, xn)

    f = w_gate.shape[2]
    last = lambda i, be, bv, nu: jnp.minimum(i, nu[0] - 1)
    y = pl.pallas_call(
        _expert_kernel,
        grid_spec=pltpu.PrefetchScalarGridSpec(
            num_scalar_prefetch=3,
            grid=(n_blocks,),
            in_specs=[pl.BlockSpec((tr, d), lambda i, be, bv, nu: (last(i, be, bv, nu), 0)),
                      pl.BlockSpec((1, d, f), lambda i, be, bv, nu: (be[last(i, be, bv, nu)], 0, 0)),
                      pl.BlockSpec((1, d, f), lambda i, be, bv, nu: (be[last(i, be, bv, nu)], 0, 0)),
                      pl.BlockSpec((1, f, d), lambda i, be, bv, nu: (be[last(i, be, bv, nu)], 0, 0))],
            out_specs=pl.BlockSpec((tr, d), lambda i, be, bv, nu: (i, 0)),
            scratch_shapes=[pltpu.VMEM((d, f), _BF16), pltpu.VMEM((d, f), _BF16), pltpu.VMEM((f, d), _BF16)]),
        out_shape=jax.ShapeDtypeStruct((n_rows, d), _F32),
        compiler_params=_params(1),
        name="moe_experts",
    )(block_e, block_valid, n_used, xs, w_gate, w_up, w_down)

    eye = jnp.asarray(np.eye(tm, dtype=np.float32))
    fin = final_norm_w is not None
    fnorm = (final_norm_w if fin else jnp.ones((d,), _F32)).reshape(1, d)
    out = pl.pallas_call(
        functools.partial(_combine_kernel, tm=tm, final_norm=fin),
        grid=(out_tiles,),
        in_specs=[pl.BlockSpec(memory_space=pl.ANY),
                  pl.BlockSpec((1, 8, tm), lambda i: (i, 0, 0)),
                  pl.BlockSpec((tm, d), lambda i: (i, 0)),
                  pl.BlockSpec(memory_space=pl.ANY),
                  _full((tm, tm)), _full((1, d))],
        out_specs=pl.BlockSpec((tm, d), lambda i: (i, 0)),
        out_shape=jax.ShapeDtypeStruct((out_tiles * tm, d), _F32),
        scratch_shapes=[pltpu.SMEM((1, 2 * tm), jnp.int32), pltpu.VMEM((2 * tm, d), _F32),
                        pltpu.SemaphoreType.DMA, pltpu.SemaphoreType.DMA],
        compiler_params=_params(1, disable_bounds_checks=True),
        name="moe_combine",
    )(dest, rf, h, y, eye, fnorm)
    return out


def kernel(x, meta_tokens, a_norm, a_w, a_scale, b_norm, b_w_dq, b_q_norm, b_w_uq, b_w_o, kv_norm, w_dkv,
           kv_lat_norm, w_uk, w_uv, ffn_norm, router_g, router_g_bias, router_e, router_e_bias, w_gate, w_up,
           w_down, final_norm):
    bsz, seq, d = x.shape
    tm = TOKEN_TILE
    assert seq % tm == 0 and seq % ATTN_Q_TILE == 0 and ATTN_Q_TILE == 2 * ATTN_K_TILE and ATTN_K_TILE == tm
    assert d % (128 * len(POOL_WINDOWS)) == 0 and N_META == max(POOL_WINDOWS) and N_META <= tm
    n_tok = bsz * seq
    n_real_tiles = n_tok // tm
    tiles_per_batch = seq // tm
    n_tiles = n_real_tiles + 1
    gd = d // len(POOL_WINDOWS)
    row = lambda v: v.reshape(1, -1).astype(_F32)

    x2 = x.reshape(n_tok, d)
    meta_pad = jnp.concatenate([meta_tokens, jnp.zeros((tm - N_META, d), x.dtype)], axis=0)

    r = np.arange(tm)[:, None]
    cidx = np.arange(tm)[None, :]
    pm = np.stack([((r - cidx >= 0) & (r - cidx < w)) for w in POOL_WINDOWS]).astype(np.float32)
    hc = np.arange(N_META)[None, :]
    ph = np.stack([(r + N_META - hc < w) for w in POOL_WINDOWS]).astype(np.float32)
    triu = (r < cidx).astype(np.float32)
    pm, ph, triu = (jnp.asarray(a, dtype=_BF16) for a in (pm, ph, triu))

    wrt0, br0 = _router_operands(router_g[0], router_g_bias[0], router_e[0], router_e_bias[0])
    route_shapes, route_specs = _route_out(n_tiles, tm)
    tile_or_last = lambda i: (jnp.minimum(i, n_real_tiles - 1), 0)
    halo_blocks = tm // N_META
    h1, xn1, ri1, rf1, cnt1 = pl.pallas_call(
        functools.partial(_mixer_kernel, n_real_tiles=n_real_tiles, tiles_per_batch=tiles_per_batch),
        grid=(n_tiles,),
        in_specs=[pl.BlockSpec((tm, d), tile_or_last),
                  pl.BlockSpec((N_META, d), lambda i: (jnp.clip(i * halo_blocks - 1, 0, n_tok // N_META - 1), 0)),
                  _full((tm, d)), _full((N_META, d)), _full((1, d)),
                  _full((len(POOL_WINDOWS), gd, gd)), _full((1, d)), _full((1, d)),
                  _full((ROUTER_ROWS, d)), _full((ROUTER_ROWS, 1)),
                  _full(pm.shape), _full(ph.shape), _full((tm, tm))],
        out_specs=[pl.BlockSpec((tm, d), lambda i: (i, 0)), pl.BlockSpec((tm, d), lambda i: (i, 0))] + route_specs,
        out_shape=[jax.ShapeDtypeStruct((n_tiles * tm, d), _F32),
                   jax.ShapeDtypeStruct((n_tiles * tm, d), _F32)] + route_shapes,
        scratch_shapes=[pltpu.VMEM((N_EXPERTS, 1), _F32), pltpu.VMEM((ROUTER_SPLIT_ROWS, d), _BF16)],
        compiler_params=_params(1),
        name="pool_mixer_router",
    )(x2, x2, meta_pad, meta_tokens, row(a_norm[0]), a_w[0].astype(_BF16), row(a_scale[0]), row(ffn_norm[0]),
      wrt0, br0, pm, ph, triu)

    h2 = _moe(h1, xn1, ri1, rf1, cnt1, w_gate[0], w_up[0], w_down[0], n_tiles, n_real_tiles, n_tiles, None)

    pos = jnp.concatenate([jnp.arange(seq, dtype=_F32) + N_META, jnp.arange(tm, dtype=_F32)])
    inv_freq = ROPE_THETA ** (-jnp.arange(0, QK_ROPE_DIM, 2, dtype=_F32) / QK_ROPE_DIM)
    ang = pos[:, None] * inv_freq[None, :]
    cos_t = jnp.tile(jnp.cos(ang), (1, 4))
    sin_t = jnp.tile(jnp.concatenate([-jnp.sin(ang), jnp.sin(ang)], axis=1), (1, 2))

    wdkv = jnp.concatenate([w_dkv, w_dkv[:, KV_LORA_RANK:]], axis=1).astype(_BF16)
    wuk = w_uk.reshape(KV_LORA_RANK, N_HEADS * QK_NOPE_DIM).astype(_BF16)
    wuv = w_uv.reshape(KV_LORA_RANK, N_HEADS * V_HEAD_DIM).astype(_BF16)
    wuq = b_w_uq[0]
    q_rank = wuq.shape[0]
    wuq_rope = wuq[:, :, QK_NOPE_DIM:]
    wuq = jnp.concatenate([wuq[:, :, :QK_NOPE_DIM].reshape(q_rank, -1),
                           jnp.concatenate([wuq_rope, wuq_rope], axis=2).reshape(q_rank, -1)], axis=1).astype(_BF16)
    wdq = b_w_dq[0].astype(_BF16)
    pos_tile = lambda i: (jnp.where(i < n_real_tiles, i % tiles_per_batch, tiles_per_batch), 0)
    head_tile = lambda i: (0, i, 0)
    q, k, v = pl.pallas_call(
        _proj_kernel,
        grid=(n_tiles,),
        in_specs=[pl.BlockSpec((tm, d), lambda i: (i, 0)),
                  pl.BlockSpec((tm, 128), pos_tile), pl.BlockSpec((tm, 128), pos_tile),
                  _full((1, d)), _full(wdkv.shape), _full((1, KV_LORA_RANK)), _full(wuk.shape), _full(wuv.shape),
                  _full((1, d)), _full(wdq.shape), _full((1, q_rank)), _full(wuq.shape)],
        out_specs=[pl.BlockSpec((N_HEADS, tm, QK_DIM), head_tile),
                   pl.BlockSpec((N_HEADS, tm, QK_DIM), head_tile),
                   pl.BlockSpec((N_HEADS, 1, V_HEAD_DIM, tm), lambda i: (0, i, 0, 0))],
        out_shape=[jax.ShapeDtypeStruct((N_HEADS, n_tiles * tm, QK_DIM), _BF16),
                   jax.ShapeDtypeStruct((N_HEADS, n_tiles * tm, QK_DIM), _BF16),
                   jax.ShapeDtypeStruct((N_HEADS, n_tiles, V_HEAD_DIM, tm), _BF16)],
        compiler_params=_params(1),
        name="latent_qkv",
    )(h2, cos_t, sin_t, row(kv_norm), wdkv, row(kv_lat_norm), wuk, wuv,
      row(b_norm[0]), wdq, row(b_q_norm[0]), wuq)

    tq, tk, hp = ATTN_Q_TILE, ATTN_K_TILE, ATTN_HEADS_PER_STEP
    n_q = seq // tq
    meta_block = n_tok // N_META
    o = pl.pallas_call(
        functools.partial(_attn_kernel, tq=tq, tk=tk),
        grid=(bsz, N_HEADS // hp, n_q),
        in_specs=[pl.BlockSpec((hp, tq, QK_DIM), lambda b, hg, j: (hg, b * n_q + j, 0)),
                  pl.BlockSpec((hp, seq, QK_DIM), lambda b, hg, j: (hg, b, 0)),
                  pl.BlockSpec((hp, tiles_per_batch, V_HEAD_DIM, tm), lambda b, hg, j: (hg, b, 0, 0)),
                  pl.BlockSpec((hp, N_META, QK_DIM), lambda b, hg, j: (hg, meta_block, 0)),
                  pl.BlockSpec((hp, 1, V_HEAD_DIM, tm), lambda b, hg, j: (hg, n_real_tiles, 0, 0))],
        out_specs=pl.BlockSpec((tq, hp * V_HEAD_DIM), lambda b, hg, j: (b * n_q + j, hg)),
        out_shape=jax.ShapeDtypeStruct((n_tok, N_HEADS * V_HEAD_DIM), _BF16),
        scratch_shapes=[pltpu.VMEM((hp, tk, tq), _F32), pltpu.VMEM((hp, tk, tq), _F32),
                        pltpu.VMEM((hp, V_HEAD_DIM, tq), _F32)],
        compiler_params=_params(3),
        name="causal_attention",
    )(q, k, v, k, v)

    wrt1, br1 = _router_operands(router_g[1], router_g_bias[1], router_e[1], router_e_bias[1])
    route_shapes, route_specs = _route_out(n_real_tiles, tm)
    h3, xn2, ri2, rf2, cnt2 = pl.pallas_call(
        _oproj_kernel,
        grid=(n_real_tiles,),
        in_specs=[pl.BlockSpec((tm, N_HEADS * V_HEAD_DIM), lambda i: (i, 0)),
                  pl.BlockSpec((tm, d), lambda i: (i, 0)),
                  _full((N_HEADS * V_HEAD_DIM, d)), _full((1, d)),
                  _full((ROUTER_ROWS, d)), _full((ROUTER_ROWS, 1)), _full((tm, tm))],
        out_specs=[pl.BlockSpec((tm, d), lambda i: (i, 0)), pl.BlockSpec((tm, d), lambda i: (i, 0))] + route_specs,
        out_shape=[jax.ShapeDtypeStruct((n_tok, d), _F32), jax.ShapeDtypeStruct((n_tok, d), _F32)] + route_shapes,
        scratch_shapes=[pltpu.VMEM((N_EXPERTS, 1), _F32), pltpu.VMEM((ROUTER_SPLIT_ROWS, d), _BF16)],
        compiler_params=_params(1),
        name="attn_out_router",
    )(o, h2, b_w_o[0].astype(_BF16), row(ffn_norm[1]), wrt1, br1, triu)

    out = _moe(h3, xn2, ri2, rf2, cnt2, w_gate[1], w_up[1], w_down[1],
               n_real_tiles, n_real_tiles, n_real_tiles, final_norm)
    return out.reshape(bsz, seq, d)
```

```python
import functools

import numpy as np
import jax
import jax.numpy as jnp
from jax import lax
from jax.experimental import pallas as pl
from jax.experimental.pallas import tpu as pltpu

N_META = 16
POOL_WINDOWS = (2, 4, 8, 16)
N_HEADS = 8
QK_NOPE_DIM = 128
QK_ROPE_DIM = 64
QK_DIM = QK_NOPE_DIM + QK_ROPE_DIM
V_HEAD_DIM = 128
KV_LORA_RANK = 256
ROPE_THETA = 10000.0
ATTN_SCALE = QK_DIM ** -0.5
Q_SCALE = ATTN_SCALE * 1.4426950408889634
N_EXPERT_GROUPS = 4
EXPERTS_PER_GROUP = 8
N_EXPERTS = N_EXPERT_GROUPS * EXPERTS_PER_GROUP
RMS_EPS = 1e-6
NEG_INF = -1e30

TOKEN_TILE = 256
EXPERT_ROWS = 256
ATTN_Q_TILE = 512
ATTN_K_TILE = ATTN_Q_TILE // 2
ATTN_HEADS_PER_STEP = 4
ROUTER_ROWS = 8 + N_EXPERTS
ROUTER_LO_ROW = 48
ROUTER_SPLIT_ROWS = 2 * ROUTER_LO_ROW
VMEM_LIMIT_BYTES = 48 * 1024 * 1024

LANES = 128
_F32 = jnp.float32
_BF16 = jnp.bfloat16
_NT_DIMS = (((1,), (1,)), ((), ()))


def _params(n_grid_dims=1, **kw):
    return pltpu.CompilerParams(dimension_semantics=("arbitrary",) * n_grid_dims,
                                vmem_limit_bytes=VMEM_LIMIT_BYTES, **kw)


def _rms(x, g):
    ms = jnp.mean(x * x, axis=-1, keepdims=True)
    return x * lax.rsqrt(ms + RMS_EPS) * g


def _split_bf16(x):
    hi = x.astype(_BF16)
    lo = (x - hi.astype(_F32)).astype(_BF16)
    return hi, lo


def _dot(a, b):
    return jnp.dot(a, b, preferred_element_type=_F32)


def _store_token_tiles(ref, x):
    n, d = x.shape
    parts = d // LANES
    for s in range(parts):
        ref[pl.ds(s, n, stride=parts), :] = x[:, s * LANES:(s + 1) * LANES]


def _load_token_tiles(ref, first_token, n, parts):
    return jnp.concatenate([ref[pl.ds(first_token * parts + s, n, stride=parts), :] for s in range(parts)], axis=1)


def _route_init(wrt_ref, wsplit_ref, base_ref):
    base_ref[...] = jnp.zeros(base_ref.shape, base_ref.dtype)
    w_hi, w_lo = _split_bf16(wrt_ref[...])
    wsplit_ref[...] = jnp.zeros(wsplit_ref.shape, wsplit_ref.dtype)
    wsplit_ref[0:ROUTER_ROWS, :] = w_hi
    wsplit_ref[ROUTER_LO_ROW:ROUTER_LO_ROW + ROUTER_ROWS, :] = w_lo


def _route(xn, valid, wsplit_ref, br_ref, triu_ref, base_ref, ri_ref, rf_ref, cnt_ref):
    tm = xn.shape[0]
    x_hi, x_lo = _split_bf16(xn)
    by_hi = lax.dot_general(wsplit_ref[...], x_hi, _NT_DIMS, preferred_element_type=_F32)
    by_lo = lax.dot_general(wsplit_ref[0:ROUTER_LO_ROW, :], x_lo, _NT_DIMS, preferred_element_type=_F32)
    logits = (by_hi[0:ROUTER_ROWS] + by_hi[ROUTER_LO_ROW:ROUTER_LO_ROW + ROUTER_ROWS]
              + by_lo[0:ROUTER_ROWS] + br_ref[...])
    lg = logits[0:N_EXPERT_GROUPS]
    eg = jnp.exp(lg - jnp.max(lg, axis=0, keepdims=True))
    pg = eg / jnp.sum(eg, axis=0, keepdims=True)
    w_g = jnp.max(pg, axis=0, keepdims=True)
    ig = lax.broadcasted_iota(jnp.int32, pg.shape, 0).astype(_F32)
    g_sel = jnp.min(jnp.where(pg == w_g, ig, float(N_EXPERT_GROUPS)), axis=0, keepdims=True)

    sel = logits[8:8 + EXPERTS_PER_GROUP]
    for g in range(1, N_EXPERT_GROUPS):
        sel = jnp.where(g_sel == float(g), logits[8 + g * EXPERTS_PER_GROUP:8 + (g + 1) * EXPERTS_PER_GROUP], sel)
    ie = lax.broadcasted_iota(jnp.int32, sel.shape, 0).astype(_F32)
    v1 = jnp.max(sel, axis=0, keepdims=True)
    i1 = jnp.min(jnp.where(sel == v1, ie, float(EXPERTS_PER_GROUP)), axis=0, keepdims=True)
    rest = jnp.where(ie == i1, -jnp.inf, sel)
    v2 = jnp.max(rest, axis=0, keepdims=True)
    i2 = jnp.min(jnp.where(rest == v2, ie, float(EXPERTS_PER_GROUP)), axis=0, keepdims=True)
    e2 = jnp.exp(v2 - v1)
    den = 1.0 + e2
    validf = valid.astype(_F32)
    gate0 = w_g * (1.0 / den) * validf
    gate1 = w_g * (e2 / den) * validf
    eid0 = g_sel * float(EXPERTS_PER_GROUP) + i1
    eid1 = g_sel * float(EXPERTS_PER_GROUP) + i2

    iall = lax.broadcasted_iota(jnp.int32, (N_EXPERTS, tm), 0).astype(_F32)
    oh0 = jnp.where(iall == eid0, validf, 0.0)
    oh1 = jnp.where(iall == eid1, validf, 0.0)
    both = oh0 + oh1
    before = _dot(both.astype(_BF16), triu_ref[...]) + base_ref[...]
    rank0 = jnp.sum(oh0 * before, axis=0, keepdims=True)
    rank1 = jnp.sum(oh1 * before, axis=0, keepdims=True)
    base_ref[...] = base_ref[...] + jnp.sum(both, axis=1, keepdims=True)

    ri_ref[...] = jnp.zeros(ri_ref.shape, ri_ref.dtype)
    rf_ref[...] = jnp.zeros(rf_ref.shape, rf_ref.dtype)
    ri_ref[0, 0:1, :] = eid0.astype(jnp.int32)
    ri_ref[0, 1:2, :] = eid1.astype(jnp.int32)
    ri_ref[0, 2:3, :] = rank0.astype(jnp.int32)
    ri_ref[0, 3:4, :] = rank1.astype(jnp.int32)
    rf_ref[0, 0:1, :] = gate0
    rf_ref[0, 1:2, :] = gate1
    cnt_ref[...] = jnp.broadcast_to(base_ref[...], cnt_ref.shape).astype(jnp.int32)


def _mixer_kernel(x_ref, xh_ref, mp_ref, meta_ref, anorm_ref, aw_ref, ascale_ref, fnorm_ref,
                  wrt_ref, br_ref, pm_ref, ph_ref, triu_ref,
                  h_ref, xn_ref, ri_ref, rf_ref, cnt_ref, base_ref, wsplit_ref, *, n_real_tiles, tiles_per_batch):
    i = pl.program_id(0)
    tm = x_ref.shape[0]
    gd = x_ref.shape[1] // len(POOL_WINDOWS)
    is_meta = i == n_real_tiles
    first = (i % tiles_per_batch) == 0

    @pl.when(i == 0)
    def _():
        _route_init(wrt_ref, wsplit_ref, base_ref)

    h = jnp.where(is_meta, mp_ref[...], x_ref[...])
    halo = jnp.where(is_meta, 0.0, jnp.where(first, meta_ref[...], xh_ref[...]))
    hn = _rms(h, anorm_ref[...])
    hh = _rms(halo, anorm_ref[...])
    hn_hi, hn_lo = _split_bf16(hn)
    hh_hi, hh_lo = _split_bf16(hh)
    row = lax.broadcasted_iota(jnp.int32, (tm, 1), 0)
    for g, w in enumerate(POOL_WINDOWS):
        sl = slice(g * gd, (g + 1) * gd)
        win = (_dot(pm_ref[g], hn_hi[:, sl]) + _dot(pm_ref[g], hn_lo[:, sl])
               + _dot(ph_ref[g], hh_hi[:, sl]) + _dot(ph_ref[g], hh_lo[:, sl]))
        cnt = jnp.where(is_meta, jnp.minimum(row + 1, w), w).astype(_F32)
        pooled = win * (1.0 / cnt) - hn[:, sl]
        mix = _dot(pooled.astype(_BF16), aw_ref[g])
        h_ref[:, sl] = h[:, sl] + mix * ascale_ref[:, sl]

    xn = _rms(h_ref[...], fnorm_ref[...])
    _store_token_tiles(xn_ref, xn)
    lane = lax.broadcasted_iota(jnp.int32, (1, tm), 1)
    valid = jnp.logical_or(jnp.logical_not(is_meta), lane < N_META)
    _route(xn, valid, wsplit_ref, br_ref, triu_ref, base_ref, ri_ref, rf_ref, cnt_ref)


def _dispatch_kernel(dest_hbm, xn_ref, xs_hbm, idx_smem, isem, rsem, *, tm, n_real_tiles):
    i = pl.program_id(0)
    icp = pltpu.make_async_copy(dest_hbm.at[i], idx_smem, isem)
    icp.start()
    icp.wait()

    parts = xn_ref.shape[0] // tm

    def scatter(n_rows):
        def issue(t, c):
            src = xn_ref.at[pl.ds(pl.multiple_of(t * parts, parts), parts), :]
            for k in range(2):
                d = idx_smem[0, k * tm + t]
                pltpu.make_async_copy(src, xs_hbm.at[pl.ds(pl.multiple_of(d * parts, parts), parts), :], rsem).start()
            return c

        lax.fori_loop(0, n_rows, issue, 0, unroll=min(8, n_rows))
        for k in range(2):
            pltpu.make_async_copy(xn_ref.at[pl.ds(0, n_rows * parts), :],
                                  xs_hbm.at[pl.ds(0, n_rows * parts), :], rsem).wait()

    @pl.when(i < n_real_tiles)
    def _():
        scatter(tm)

    @pl.when(i >= n_real_tiles)
    def _():
        scatter(N_META)


def _expert_kernel(be_ref, bv_ref, nu_ref, xs_ref, wg_ref, wu_ref, wd_ref, y_ref, wgb_ref, wub_ref, wdb_ref):
    i = pl.program_id(0)
    active = i < nu_ref[0]
    new_expert = jnp.logical_or(i == 0, be_ref[i] != be_ref[jnp.maximum(i - 1, 0)])

    @pl.when(jnp.logical_and(active, new_expert))
    def _():
        wgb_ref[...] = wg_ref[0].astype(_BF16)
        wub_ref[...] = wu_ref[0].astype(_BF16)
        wdb_ref[...] = wd_ref[0].astype(_BF16)

    @pl.when(active)
    def _():
        parts = wgb_ref.shape[0] // LANES
        tr = xs_ref.shape[0] // parts
        rows = lax.broadcasted_iota(jnp.int32, (tr, 1), 0)
        x = jnp.where(rows < bv_ref[i], _load_token_tiles(xs_ref, 0, tr, parts), 0.0).astype(_BF16)
        g = _dot(x, wgb_ref[...])
        u = _dot(x, wub_ref[...])
        a = g * (1.0 / (1.0 + jnp.exp(-g))) * u
        _store_token_tiles(y_ref, _dot(a.astype(_BF16), wdb_ref[...]))

    @pl.when(i >= nu_ref[0])
    def _():
        y_ref[...] = jnp.zeros(y_ref.shape, y_ref.dtype)


def _combine_kernel(dest_hbm, rf_ref, h_ref, y_hbm, eye_ref, fnorm_ref, out_ref,
                    idx_smem, ybuf, isem, rsem, *, tm, final_norm):
    i = pl.program_id(0)
    icp = pltpu.make_async_copy(dest_hbm.at[i], idx_smem, isem)
    icp.start()
    icp.wait()

    parts = ybuf.shape[0] // (2 * tm)

    def issue(t, c):
        for k in range(2):
            d = idx_smem[0, k * tm + t]
            pltpu.make_async_copy(y_hbm.at[pl.ds(pl.multiple_of(d * parts, parts), parts), :],
                                  ybuf.at[pl.ds(pl.multiple_of((k * tm + t) * parts, parts), parts), :], rsem).start()
        return c

    lax.fori_loop(0, tm, issue, 0, unroll=8)
    gt = lax.dot_general(eye_ref[...], rf_ref[0], _NT_DIMS, precision=lax.Precision.HIGHEST,
                         preferred_element_type=_F32)
    pltpu.make_async_copy(y_hbm.at[pl.ds(0, 2 * tm * parts), :], ybuf, rsem).wait()
    out = h_ref[...] + (_load_token_tiles(ybuf, 0, tm, parts) * gt[:, 0:1]
                        + _load_token_tiles(ybuf, tm, tm, parts) * gt[:, 1:2])
    if final_norm:
        out = _rms(out, fnorm_ref[...])
    out_ref[...] = out


def _rope128(x, cos_t, sin_t):
    lane = lax.broadcasted_iota(jnp.int32, (1, 128), 1)
    first_half = (lane % QK_ROPE_DIM) < (QK_ROPE_DIM // 2)
    swapped = jnp.where(first_half, pltpu.roll(x, 128 - QK_ROPE_DIM // 2, axis=1),
                        pltpu.roll(x, QK_ROPE_DIM // 2, axis=1))
    return x * cos_t + swapped * sin_t


def _proj_kernel(h_ref, cos_ref, sin_ref, kvn_ref, wdkv_ref, kvlat_ref, wuk_ref, wuv_ref,
                 bnorm_ref, wdq_ref, qnorm_ref, wuq_ref, q_ref, k_ref, v_ref):
    h = h_ref[...]
    cos_t = cos_ref[...]
    sin_t = sin_ref[...]
    c = _dot(_rms(h, kvn_ref[...]).astype(_BF16), wdkv_ref[...])
    ckv = _rms(c[:, :KV_LORA_RANK], kvlat_ref[...]).astype(_BF16)
    kr = _rope128(c[:, KV_LORA_RANK:KV_LORA_RANK + 128], cos_t, sin_t)[:, :QK_ROPE_DIM].astype(_BF16)
    kn = _dot(ckv, wuk_ref[...])
    vv = _dot(ckv, wuv_ref[...])
    cq = _rms(_dot(_rms(h, bnorm_ref[...]).astype(_BF16), wdq_ref[...]), qnorm_ref[...]).astype(_BF16)
    q = _dot(cq, wuq_ref[...])
    rope0 = N_HEADS * QK_NOPE_DIM
    for hd in range(N_HEADS):
        k_ref[hd, :, 0:QK_NOPE_DIM] = kn[:, hd * QK_NOPE_DIM:(hd + 1) * QK_NOPE_DIM].astype(_BF16)
        k_ref[hd, :, QK_NOPE_DIM:QK_DIM] = kr
        v_ref[hd, 0] = vv[:, hd * V_HEAD_DIM:(hd + 1) * V_HEAD_DIM].T.astype(_BF16)
        q_ref[hd, :, 0:QK_NOPE_DIM] = (q[:, hd * QK_NOPE_DIM:(hd + 1) * QK_NOPE_DIM] * Q_SCALE).astype(_BF16)
        qr = _rope128(q[:, rope0 + hd * 128:rope0 + (hd + 1) * 128], cos_t, sin_t)[:, :QK_ROPE_DIM]
        q_ref[hd, :, QK_NOPE_DIM:QK_DIM] = (qr * Q_SCALE).astype(_BF16)


def _attn_kernel(q_ref, k_ref, vt_ref, km_ref, vmt_ref, o_ref, sa_ref, sb_ref, acc_ref, *, tq, tk):
    j = pl.program_id(2)
    heads = range(q_ref.shape[0])
    vd = acc_ref.shape[1]

    def scores(kb, s_ref):
        start = pl.multiple_of(kb * tk, tk)
        for hd in heads:
            s_ref[hd] = lax.dot_general(k_ref[hd, pl.ds(start, tk), :], q_ref[hd], _NT_DIMS,
                                        preferred_element_type=_F32)

    def update(kb, s_ref, carry, masked):
        out = []
        for hd in heads:
            m, l = carry[hd]
            s = s_ref[hd]
            if masked:
                keys = kb * tk + lax.broadcasted_iota(jnp.int32, (tk, tq), 0)
                qpos = j * tq + lax.broadcasted_iota(jnp.int32, (tk, tq), 1)
                s = jnp.where(keys <= qpos, s, NEG_INF)
            m_new = jnp.maximum(m, jnp.max(s, axis=0, keepdims=True))
            alpha = jnp.exp2(m - m_new)
            p = jnp.exp2(s - m_new)
            l = alpha * l + jnp.sum(p, axis=0, keepdims=True)
            acc_ref[hd] = acc_ref[hd] * alpha + _dot(vt_ref[hd, kb], p.astype(_BF16))
            out.append((m_new, l))
        return tuple(out)

    scores(0, sa_ref)
    carry = []
    for hd in heads:
        s0 = lax.dot_general(km_ref[hd], q_ref[hd], _NT_DIMS, preferred_element_type=_F32)
        m = jnp.max(s0, axis=0, keepdims=True)
        p0 = jnp.exp2(s0 - m)
        carry.append((m, jnp.sum(p0, axis=0, keepdims=True)))
        acc_ref[hd] = _dot(vmt_ref[hd, 0, :, 0:N_META], p0.astype(_BF16))
    carry = tuple(carry)

    def pair(kp, carry):
        scores(2 * kp + 1, sb_ref)
        carry = update(2 * kp, sa_ref, carry, False)
        scores(2 * kp + 2, sa_ref)
        return update(2 * kp + 1, sb_ref, carry, False)

    carry = lax.fori_loop(0, j, pair, carry)
    scores(2 * j + 1, sb_ref)
    carry = update(2 * j, sa_ref, carry, True)
    carry = update(2 * j + 1, sb_ref, carry, True)
    for hd in heads:
        o_ref[:, hd * vd:(hd + 1) * vd] = (acc_ref[hd] / carry[hd][1]).T.astype(o_ref.dtype)


def _oproj_kernel(o_ref, h_ref, wo_ref, fnorm_ref, wrt_ref, br_ref, triu_ref,
                  h_out_ref, xn_ref, ri_ref, rf_ref, cnt_ref, base_ref, wsplit_ref):
    i = pl.program_id(0)

    @pl.when(i == 0)
    def _():
        _route_init(wrt_ref, wsplit_ref, base_ref)

    h = h_ref[...] + _dot(o_ref[...], wo_ref[...])
    h_out_ref[...] = h
    xn = _rms(h, fnorm_ref[...])
    _store_token_tiles(xn_ref, xn)
    valid = lax.broadcasted_iota(jnp.int32, (1, h.shape[0]), 1) >= 0
    _route(xn, valid, wsplit_ref, br_ref, triu_ref, base_ref, ri_ref, rf_ref, cnt_ref)


def _full(shape):
    nd = len(shape)
    return pl.BlockSpec(shape, lambda *_: (0,) * nd)


def _router_operands(router_g, router_g_bias, router_e, router_e_bias):
    d = router_g.shape[0]
    wrt = jnp.concatenate([router_g.T, jnp.zeros((8 - N_EXPERT_GROUPS, d), _F32), router_e.T], axis=0)
    br = jnp.concatenate([router_g_bias, jnp.zeros((8 - N_EXPERT_GROUPS,), _F32), router_e_bias])[:, None]
    return wrt.astype(_F32), br.astype(_F32)


def _route_out(n_tiles, tm):
    shapes = [jax.ShapeDtypeStruct((n_tiles, 8, tm), jnp.int32),
              jax.ShapeDtypeStruct((n_tiles, 8, tm), _F32),
              jax.ShapeDtypeStruct((N_EXPERTS, 128), jnp.int32)]
    specs = [pl.BlockSpec((1, 8, tm), lambda i: (i, 0, 0)),
             pl.BlockSpec((1, 8, tm), lambda i: (i, 0, 0)),
             pl.BlockSpec((N_EXPERTS, 128), lambda i: (0, 0))]
    return shapes, specs


def _moe(h, xn, ri, rf, counts, w_gate, w_up, w_down, n_tiles, n_real_tiles, out_tiles, final_norm_w):
    tm, tr = TOKEN_TILE, EXPERT_ROWS
    d = h.shape[1]
    parts = d // LANES
    n_valid = n_real_tiles * tm + (n_tiles - n_real_tiles) * N_META
    n_blocks = -(-(2 * n_valid + N_EXPERTS * (tr - 1)) // tr)
    n_rows = n_blocks * tr

    counts = counts[:, 0]
    padded = (counts + tr - 1) // tr * tr
    pends = jnp.cumsum(padded)
    pstarts = pends - padded
    n_used = (pends[-1] // tr).astype(jnp.int32).reshape(1)
    blk0 = jnp.arange(n_blocks, dtype=jnp.int32) * tr
    block_e = jnp.minimum(jnp.sum(blk0[:, None] >= pends[None, :], axis=1), N_EXPERTS - 1).astype(jnp.int32)
    experts = jnp.arange(N_EXPERTS, dtype=jnp.int32)
    block_oh = block_e[:, None] == experts[None, :]
    block_cnt = jnp.sum(jnp.where(block_oh, counts[None, :], 0), axis=1)
    block_start = jnp.sum(jnp.where(block_oh, pstarts[None, :], 0), axis=1)
    block_valid = jnp.clip(block_cnt - (blk0 - block_start), 0, tr).astype(jnp.int32)
    eid = ri[:, 0:2, :]
    slot0 = jnp.sum(jnp.where(eid[..., None] == experts, pstarts, 0), axis=-1)
    dest = (slot0 + ri[:, 2:4, :]).astype(jnp.int32).reshape(n_tiles, 1, 2 * tm)

    xs = pl.pallas_call(
        functools.partial(_dispatch_kernel, tm=tm, n_real_tiles=n_real_tiles),
        grid=(n_tiles,),
        in_specs=[pl.BlockSpec(memory_space=pl.ANY), pl.BlockSpec((tm * parts, LANES), lambda i: (i, 0))],
        out_specs=pl.BlockSpec(memory_space=pl.ANY),
        out_shape=jax.ShapeDtypeStruct((n_rows * parts, LANES), _F32),
        scratch_shapes=[pltpu.SMEM((1, 2 * tm), jnp.int32), pltpu.SemaphoreType.DMA, pltpu.SemaphoreType.DMA],
        compiler_params=_params(1, has_side_effects=True, disable_bounds_checks=True),
        name="moe_dispatch",
    )(dest, xn)

    f = w_gate.shape[2]
    last = lambda i, be, bv, nu: jnp.minimum(i, nu[0] - 1)
    y = pl.pallas_call(
        _expert_kernel,
        grid_spec=pltpu.PrefetchScalarGridSpec(
            num_scalar_prefetch=3,
            grid=(n_blocks,),
            in_specs=[pl.BlockSpec((tr * parts, LANES), lambda i, be, bv, nu: (last(i, be, bv, nu), 0)),
                      pl.BlockSpec((1, d, f), lambda i, be, bv, nu: (be[last(i, be, bv, nu)], 0, 0)),
                      pl.BlockSpec((1, d, f), lambda i, be, bv, nu: (be[last(i, be, bv, nu)], 0, 0)),
                      pl.BlockSpec((1, f, d), lambda i, be, bv, nu: (be[last(i, be, bv, nu)], 0, 0))],
            out_specs=pl.BlockSpec((tr * parts, LANES), lambda i, be, bv, nu: (i, 0)),
            scratch_shapes=[pltpu.VMEM((d, f), _BF16), pltpu.VMEM((d, f), _BF16), pltpu.VMEM((f, d), _BF16)]),
        out_shape=jax.ShapeDtypeStruct((n_rows * parts, LANES), _F32),
        compiler_params=_params(1),
        name="moe_experts",
    )(block_e, block_valid, n_used, xs, w_gate, w_up, w_down)

    eye = jnp.asarray(np.eye(tm, dtype=np.float32))
    fin = final_norm_w is not None
    fnorm = (final_norm_w if fin else jnp.ones((d,), _F32)).reshape(1, d)
    out = pl.pallas_call(
        functools.partial(_combine_kernel, tm=tm, final_norm=fin),
        grid=(out_tiles,),
        in_specs=[pl.BlockSpec(memory_space=pl.ANY),
                  pl.BlockSpec((1, 8, tm), lambda i: (i, 0, 0)),
                  pl.BlockSpec((tm, d), lambda i: (i, 0)),
                  pl.BlockSpec(memory_space=pl.ANY),
                  _full((tm, tm)), _full((1, d))],
        out_specs=pl.BlockSpec((tm, d), lambda i: (i, 0)),
        out_shape=jax.ShapeDtypeStruct((out_tiles * tm, d), _F32),
        scratch_shapes=[pltpu.SMEM((1, 2 * tm), jnp.int32), pltpu.VMEM((2 * tm * parts, LANES), _F32),
                        pltpu.SemaphoreType.DMA, pltpu.SemaphoreType.DMA],
        compiler_params=_params(1, disable_bounds_checks=True),
        name="moe_combine",
    )(dest, rf, h, y, eye, fnorm)
    return out


def kernel(x, meta_tokens, a_norm, a_w, a_scale, b_norm, b_w_dq, b_q_norm, b_w_uq, b_w_o, kv_norm, w_dkv,
           kv_lat_norm, w_uk, w_uv, ffn_norm, router_g, router_g_bias, router_e, router_e_bias, w_gate, w_up,
           w_down, final_norm):
    bsz, seq, d = x.shape
    tm = TOKEN_TILE
    assert seq % tm == 0 and seq % ATTN_Q_TILE == 0 and ATTN_Q_TILE == 2 * ATTN_K_TILE and ATTN_K_TILE == tm
    assert d % (LANES * len(POOL_WINDOWS)) == 0 and N_META == max(POOL_WINDOWS) and N_META <= tm
    parts = d // LANES
    n_tok = bsz * seq
    n_real_tiles = n_tok // tm
    tiles_per_batch = seq // tm
    n_tiles = n_real_tiles + 1
    gd = d // len(POOL_WINDOWS)
    row = lambda v: v.reshape(1, -1).astype(_F32)

    x2 = x.reshape(n_tok, d)
    meta_pad = jnp.concatenate([meta_tokens, jnp.zeros((tm - N_META, d), x.dtype)], axis=0)

    r = np.arange(tm)[:, None]
    cidx = np.arange(tm)[None, :]
    pm = np.stack([((r - cidx >= 0) & (r - cidx < w)) for w in POOL_WINDOWS]).astype(np.float32)
    hc = np.arange(N_META)[None, :]
    ph = np.stack([(r + N_META - hc < w) for w in POOL_WINDOWS]).astype(np.float32)
    triu = (r < cidx).astype(np.float32)
    pm, ph, triu = (jnp.asarray(a, dtype=_BF16) for a in (pm, ph, triu))

    wrt0, br0 = _router_operands(router_g[0], router_g_bias[0], router_e[0], router_e_bias[0])
    route_shapes, route_specs = _route_out(n_tiles, tm)
    tile_or_last = lambda i: (jnp.minimum(i, n_real_tiles - 1), 0)
    halo_blocks = tm // N_META
    h1, xn1, ri1, rf1, cnt1 = pl.pallas_call(
        functools.partial(_mixer_kernel, n_real_tiles=n_real_tiles, tiles_per_batch=tiles_per_batch),
        grid=(n_tiles,),
        in_specs=[pl.BlockSpec((tm, d), tile_or_last),
                  pl.BlockSpec((N_META, d), lambda i: (jnp.clip(i * halo_blocks - 1, 0, n_tok // N_META - 1), 0)),
                  _full((tm, d)), _full((N_META, d)), _full((1, d)),
                  _full((len(POOL_WINDOWS), gd, gd)), _full((1, d)), _full((1, d)),
                  _full((ROUTER_ROWS, d)), _full((ROUTER_ROWS, 1)),
                  _full(pm.shape), _full(ph.shape), _full((tm, tm))],
        out_specs=[pl.BlockSpec((tm, d), lambda i: (i, 0)),
                   pl.BlockSpec((tm * parts, LANES), lambda i: (i, 0))] + route_specs,
        out_shape=[jax.ShapeDtypeStruct((n_tiles * tm, d), _F32),
                   jax.ShapeDtypeStruct((n_tiles * tm * parts, LANES), _F32)] + route_shapes,
        scratch_shapes=[pltpu.VMEM((N_EXPERTS, 1), _F32), pltpu.VMEM((ROUTER_SPLIT_ROWS, d), _BF16)],
        compiler_params=_params(1),
        name="pool_mixer_router",
    )(x2, x2, meta_pad, meta_tokens, row(a_norm[0]), a_w[0].astype(_BF16), row(a_scale[0]), row(ffn_norm[0]),
      wrt0, br0, pm, ph, triu)

    h2 = _moe(h1, xn1, ri1, rf1, cnt1, w_gate[0], w_up[0], w_down[0], n_tiles, n_real_tiles, n_tiles, None)

    pos = jnp.concatenate([jnp.arange(seq, dtype=_F32) + N_META, jnp.arange(tm, dtype=_F32)])
    inv_freq = ROPE_THETA ** (-jnp.arange(0, QK_ROPE_DIM, 2, dtype=_F32) / QK_ROPE_DIM)
    ang = pos[:, None] * inv_freq[None, :]
    cos_t = jnp.tile(jnp.cos(ang), (1, 4))
    sin_t = jnp.tile(jnp.concatenate([-jnp.sin(ang), jnp.sin(ang)], axis=1), (1, 2))

    wdkv = jnp.concatenate([w_dkv, w_dkv[:, KV_LORA_RANK:]], axis=1).astype(_BF16)
    wuk = w_uk.reshape(KV_LORA_RANK, N_HEADS * QK_NOPE_DIM).astype(_BF16)
    wuv = w_uv.reshape(KV_LORA_RANK, N_HEADS * V_HEAD_DIM).astype(_BF16)
    wuq = b_w_uq[0]
    q_rank = wuq.shape[0]
    wuq_rope = wuq[:, :, QK_NOPE_DIM:]
    wuq = jnp.concatenate([wuq[:, :, :QK_NOPE_DIM].reshape(q_rank, -1),
                           jnp.concatenate([wuq_rope, wuq_rope], axis=2).reshape(q_rank, -1)], axis=1).astype(_BF16)
    wdq = b_w_dq[0].astype(_BF16)
    pos_tile = lambda i: (jnp.where(i < n_real_tiles, i % tiles_per_batch, tiles_per_batch), 0)
    head_tile = lambda i: (0, i, 0)
    q, k, v = pl.pallas_call(
        _proj_kernel,
        grid=(n_tiles,),
        in_specs=[pl.BlockSpec((tm, d), lambda i: (i, 0)),
                  pl.BlockSpec((tm, 128), pos_tile), pl.BlockSpec((tm, 128), pos_tile),
                  _full((1, d)), _full(wdkv.shape), _full((1, KV_LORA_RANK)), _full(wuk.shape), _full(wuv.shape),
                  _full((1, d)), _full(wdq.shape), _full((1, q_rank)), _full(wuq.shape)],
        out_specs=[pl.BlockSpec((N_HEADS, tm, QK_DIM), head_tile),
                   pl.BlockSpec((N_HEADS, tm, QK_DIM), head_tile),
                   pl.BlockSpec((N_HEADS, 1, V_HEAD_DIM, tm), lambda i: (0, i, 0, 0))],
        out_shape=[jax.ShapeDtypeStruct((N_HEADS, n_tiles * tm, QK_DIM), _BF16),
                   jax.ShapeDtypeStruct((N_HEADS, n_tiles * tm, QK_DIM), _BF16),
                   jax.ShapeDtypeStruct((N_HEADS, n_tiles, V_HEAD_DIM, tm), _BF16)],
        compiler_params=_params(1),
        name="latent_qkv",
    )(h2, cos_t, sin_t, row(kv_norm), wdkv, row(kv_lat_norm), wuk, wuv,
      row(b_norm[0]), wdq, row(b_q_norm[0]), wuq)

    tq, tk, hp = ATTN_Q_TILE, ATTN_K_TILE, ATTN_HEADS_PER_STEP
    n_q = seq // tq
    meta_block = n_tok // N_META
    o = pl.pallas_call(
        functools.partial(_attn_kernel, tq=tq, tk=tk),
        grid=(bsz, N_HEADS // hp, n_q),
        in_specs=[pl.BlockSpec((hp, tq, QK_DIM), lambda b, hg, j: (hg, b * n_q + j, 0)),
                  pl.BlockSpec((hp, seq, QK_DIM), lambda b, hg, j: (hg, b, 0)),
                  pl.BlockSpec((hp, tiles_per_batch, V_HEAD_DIM, tm), lambda b, hg, j: (hg, b, 0, 0)),
                  pl.BlockSpec((hp, N_META, QK_DIM), lambda b, hg, j: (hg, meta_block, 0)),
                  pl.BlockSpec((hp, 1, V_HEAD_DIM, tm), lambda b, hg, j: (hg, n_real_tiles, 0, 0))],
        out_specs=pl.BlockSpec((tq, hp * V_HEAD_DIM), lambda b, hg, j: (b * n_q + j, hg)),
        out_shape=jax.ShapeDtypeStruct((n_tok, N_HEADS * V_HEAD_DIM), _BF16),
        scratch_shapes=[pltpu.VMEM((hp, tk, tq), _F32), pltpu.VMEM((hp, tk, tq), _F32),
                        pltpu.VMEM((hp, V_HEAD_DIM, tq), _F32)],
        compiler_params=_params(3),
        name="causal_attention",
    )(q, k, v, k, v)

    wrt1, br1 = _router_operands(router_g[1], router_g_bias[1], router_e[1], router_e_bias[1])
    route_shapes, route_specs = _route_out(n_real_tiles, tm)
    h3, xn2, ri2, rf2, cnt2 = pl.pallas_call(
        _oproj_kernel,
        grid=(n_real_tiles,),
        in_specs=[pl.BlockSpec((tm, N_HEADS * V_HEAD_DIM), lambda i: (i, 0)),
                  pl.BlockSpec((tm, d), lambda i: (i, 0)),
                  _full((N_HEADS * V_HEAD_DIM, d)), _full((1, d)),
                  _full((ROUTER_ROWS, d)), _full((ROUTER_ROWS, 1)), _full((tm, tm))],
        out_specs=[pl.BlockSpec((tm, d), lambda i: (i, 0)),
                   pl.BlockSpec((tm * parts, LANES), lambda i: (i, 0))] + route_specs,
        out_shape=[jax.ShapeDtypeStruct((n_tok, d), _F32),
                   jax.ShapeDtypeStruct((n_tok * parts, LANES), _F32)] + route_shapes,
        scratch_shapes=[pltpu.VMEM((N_EXPERTS, 1), _F32), pltpu.VMEM((ROUTER_SPLIT_ROWS, d), _BF16)],
        compiler_params=_params(1),
        name="attn_out_router",
    )(o, h2, b_w_o[0].astype(_BF16), row(ffn_norm[1]), wrt1, br1, triu)

    out = _moe(h3, xn2, ri2, rf2, cnt2, w_gate[1], w_up[1], w_down[1],
               n_real_tiles, n_real_tiles, n_real_tiles, final_norm)
    return out.reshape(bsz, seq, d)
```

```python
import functools

import numpy as np
import jax
import jax.numpy as jnp
from jax import lax
from jax.experimental import pallas as pl
from jax.experimental.pallas import tpu as pltpu

N_META = 16
POOL_WINDOWS = (2, 4, 8, 16)
N_HEADS = 8
QK_NOPE_DIM = 128
QK_ROPE_DIM = 64
QK_DIM = QK_NOPE_DIM + QK_ROPE_DIM
V_HEAD_DIM = 128
KV_LORA_RANK = 256
ROPE_THETA = 10000.0
ATTN_SCALE = QK_DIM ** -0.5
Q_SCALE = ATTN_SCALE * 1.4426950408889634
N_EXPERT_GROUPS = 4
EXPERTS_PER_GROUP = 8
N_EXPERTS = N_EXPERT_GROUPS * EXPERTS_PER_GROUP
RMS_EPS = 1e-6
NEG_INF = -1e30

TOKEN_TILE = 256
EXPERT_ROWS = 256
ATTN_Q_TILE = 512
ATTN_K_TILE = ATTN_Q_TILE // 2
ATTN_HEADS_PER_STEP = 4
ROUTER_ROWS = 8 + N_EXPERTS
ROUTER_LO_ROW = 48
ROUTER_SPLIT_ROWS = 2 * ROUTER_LO_ROW
VMEM_LIMIT_BYTES = 48 * 1024 * 1024

LANES = 128
_F32 = jnp.float32
_BF16 = jnp.bfloat16
_NT_DIMS = (((1,), (1,)), ((), ()))


def _params(n_grid_dims=1, **kw):
    return pltpu.CompilerParams(dimension_semantics=("arbitrary",) * n_grid_dims,
                                vmem_limit_bytes=VMEM_LIMIT_BYTES, **kw)


def _rms(x, g):
    ms = jnp.mean(x * x, axis=-1, keepdims=True)
    return x * lax.rsqrt(ms + RMS_EPS) * g


def _split_bf16(x):
    hi = x.astype(_BF16)
    lo = (x - hi.astype(_F32)).astype(_BF16)
    return hi, lo


def _dot(a, b):
    return jnp.dot(a, b, preferred_element_type=_F32)


def _store_token_tiles(ref, x):
    n, d = x.shape
    parts = d // LANES
    for s in range(parts):
        ref[pl.ds(s, n, stride=parts), :] = x[:, s * LANES:(s + 1) * LANES]


def _load_token_tiles(ref, first_token, n, parts):
    return jnp.concatenate([ref[pl.ds(first_token * parts + s, n, stride=parts), :] for s in range(parts)], axis=1)


def _route_init(wrt_ref, wsplit_ref, base_ref):
    base_ref[...] = jnp.zeros(base_ref.shape, base_ref.dtype)
    w_hi, w_lo = _split_bf16(wrt_ref[...])
    wsplit_ref[...] = jnp.zeros(wsplit_ref.shape, wsplit_ref.dtype)
    wsplit_ref[0:ROUTER_ROWS, :] = w_hi
    wsplit_ref[ROUTER_LO_ROW:ROUTER_LO_ROW + ROUTER_ROWS, :] = w_lo


def _route(xn, valid, wsplit_ref, br_ref, triu_ref, base_ref, ri_ref, rf_ref, cnt_ref):
    tm = xn.shape[0]
    x_hi, x_lo = _split_bf16(xn)
    by_hi = lax.dot_general(wsplit_ref[...], x_hi, _NT_DIMS, preferred_element_type=_F32)
    by_lo = lax.dot_general(wsplit_ref[0:ROUTER_LO_ROW, :], x_lo, _NT_DIMS, preferred_element_type=_F32)
    logits = (by_hi[0:ROUTER_ROWS] + by_hi[ROUTER_LO_ROW:ROUTER_LO_ROW + ROUTER_ROWS]
              + by_lo[0:ROUTER_ROWS] + br_ref[...])
    lg = logits[0:N_EXPERT_GROUPS]
    eg = jnp.exp(lg - jnp.max(lg, axis=0, keepdims=True))
    pg = eg / jnp.sum(eg, axis=0, keepdims=True)
    w_g = jnp.max(pg, axis=0, keepdims=True)
    ig = lax.broadcasted_iota(jnp.int32, pg.shape, 0).astype(_F32)
    g_sel = jnp.min(jnp.where(pg == w_g, ig, float(N_EXPERT_GROUPS)), axis=0, keepdims=True)

    sel = logits[8:8 + EXPERTS_PER_GROUP]
    for g in range(1, N_EXPERT_GROUPS):
        sel = jnp.where(g_sel == float(g), logits[8 + g * EXPERTS_PER_GROUP:8 + (g + 1) * EXPERTS_PER_GROUP], sel)
    ie = lax.broadcasted_iota(jnp.int32, sel.shape, 0).astype(_F32)
    v1 = jnp.max(sel, axis=0, keepdims=True)
    i1 = jnp.min(jnp.where(sel == v1, ie, float(EXPERTS_PER_GROUP)), axis=0, keepdims=True)
    rest = jnp.where(ie == i1, -jnp.inf, sel)
    v2 = jnp.max(rest, axis=0, keepdims=True)
    i2 = jnp.min(jnp.where(rest == v2, ie, float(EXPERTS_PER_GROUP)), axis=0, keepdims=True)
    e2 = jnp.exp(v2 - v1)
    den = 1.0 + e2
    validf = valid.astype(_F32)
    gate0 = w_g * (1.0 / den) * validf
    gate1 = w_g * (e2 / den) * validf
    eid0 = g_sel * float(EXPERTS_PER_GROUP) + i1
    eid1 = g_sel * float(EXPERTS_PER_GROUP) + i2

    iall = lax.broadcasted_iota(jnp.int32, (N_EXPERTS, tm), 0).astype(_F32)
    oh0 = jnp.where(iall == eid0, validf, 0.0)
    oh1 = jnp.where(iall == eid1, validf, 0.0)
    both = oh0 + oh1
    before = _dot(both.astype(_BF16), triu_ref[...]) + base_ref[...]
    rank0 = jnp.sum(oh0 * before, axis=0, keepdims=True)
    rank1 = jnp.sum(oh1 * before, axis=0, keepdims=True)
    base_ref[...] = base_ref[...] + jnp.sum(both, axis=1, keepdims=True)

    ri_ref[...] = jnp.zeros(ri_ref.shape, ri_ref.dtype)
    rf_ref[...] = jnp.zeros(rf_ref.shape, rf_ref.dtype)
    ri_ref[0, 0:1, :] = eid0.astype(jnp.int32)
    ri_ref[0, 1:2, :] = eid1.astype(jnp.int32)
    ri_ref[0, 2:3, :] = rank0.astype(jnp.int32)
    ri_ref[0, 3:4, :] = rank1.astype(jnp.int32)
    rf_ref[0, 0:1, :] = gate0
    rf_ref[0, 1:2, :] = gate1
    cnt_ref[...] = jnp.broadcast_to(base_ref[...], cnt_ref.shape).astype(jnp.int32)


def _mixer_kernel(x_ref, xh_ref, mp_ref, meta_ref, anorm_ref, aw_ref, ascale_ref, fnorm_ref,
                  wrt_ref, br_ref, pm_ref, ph_ref, triu_ref,
                  h_ref, xn_ref, ri_ref, rf_ref, cnt_ref, base_ref, wsplit_ref, *, n_real_tiles, tiles_per_batch):
    i = pl.program_id(0)
    tm = x_ref.shape[0]
    gd = x_ref.shape[1] // len(POOL_WINDOWS)
    is_meta = i == n_real_tiles
    first = (i % tiles_per_batch) == 0

    @pl.when(i == 0)
    def _():
        _route_init(wrt_ref, wsplit_ref, base_ref)

    h = jnp.where(is_meta, mp_ref[...], x_ref[...])
    halo = jnp.where(is_meta, 0.0, jnp.where(first, meta_ref[...], xh_ref[...]))
    hn = _rms(h, anorm_ref[...])
    hh = _rms(halo, anorm_ref[...])
    hn_hi, hn_lo = _split_bf16(hn)
    hh_hi, hh_lo = _split_bf16(hh)
    row = lax.broadcasted_iota(jnp.int32, (tm, 1), 0)
    for g, w in enumerate(POOL_WINDOWS):
        sl = slice(g * gd, (g + 1) * gd)
        win = _dot(pm_ref[g], hn_hi[:, sl]) + _dot(pm_ref[g], hn_lo[:, sl])
        top = win[:N_META] + _dot(ph_ref[g], hh_hi[:, sl]) + _dot(ph_ref[g], hh_lo[:, sl])
        win = jnp.concatenate([top, win[N_META:]], axis=0)
        cnt = jnp.where(is_meta, jnp.minimum(row + 1, w), w).astype(_F32)
        pooled = win * (1.0 / cnt) - hn[:, sl]
        mix = _dot(pooled.astype(_BF16), aw_ref[g])
        h_ref[:, sl] = h[:, sl] + mix * ascale_ref[:, sl]

    xn = _rms(h_ref[...], fnorm_ref[...])
    _store_token_tiles(xn_ref, xn)
    lane = lax.broadcasted_iota(jnp.int32, (1, tm), 1)
    valid = jnp.logical_or(jnp.logical_not(is_meta), lane < N_META)
    _route(xn, valid, wsplit_ref, br_ref, triu_ref, base_ref, ri_ref, rf_ref, cnt_ref)


def _dispatch_kernel(dest_hbm, xn_ref, xs_hbm, idx_smem, isem, rsem, *, tm, n_real_tiles):
    i = pl.program_id(0)
    icp = pltpu.make_async_copy(dest_hbm.at[i], idx_smem, isem)
    icp.start()
    icp.wait()

    parts = xn_ref.shape[0] // tm

    def scatter(n_rows):
        def issue(t, c):
            src = xn_ref.at[pl.ds(pl.multiple_of(t * parts, parts), parts), :]
            for k in range(2):
                d = idx_smem[0, k * tm + t]
                pltpu.make_async_copy(src, xs_hbm.at[pl.ds(pl.multiple_of(d * parts, parts), parts), :],
                                      rsem).start(priority=k)
            return c

        lax.fori_loop(0, n_rows, issue, 0, unroll=min(8, n_rows))
        for k in range(2):
            pltpu.make_async_copy(xn_ref.at[pl.ds(0, n_rows * parts), :],
                                  xs_hbm.at[pl.ds(0, n_rows * parts), :], rsem).wait()

    @pl.when(i < n_real_tiles)
    def _():
        scatter(tm)

    @pl.when(i >= n_real_tiles)
    def _():
        scatter(N_META)


def _expert_kernel(be_ref, bv_ref, nu_ref, xs_ref, wg_ref, wu_ref, wd_ref, y_ref, wgb_ref, wub_ref, wdb_ref):
    i = pl.program_id(0)
    active = i < nu_ref[0]
    new_expert = jnp.logical_or(i == 0, be_ref[i] != be_ref[jnp.maximum(i - 1, 0)])

    @pl.when(jnp.logical_and(active, new_expert))
    def _():
        wgb_ref[...] = wg_ref[0, 0].astype(_BF16)
        wub_ref[...] = wu_ref[0, 0].astype(_BF16)
        wdb_ref[...] = wd_ref[0, 0].astype(_BF16)

    @pl.when(active)
    def _():
        parts = wgb_ref.shape[0] // LANES
        tr = xs_ref.shape[0] // parts
        rows = lax.broadcasted_iota(jnp.int32, (tr, 1), 0)
        x = jnp.where(rows < bv_ref[i], _load_token_tiles(xs_ref, 0, tr, parts), 0.0).astype(_BF16)
        g = _dot(x, wgb_ref[...])
        u = _dot(x, wub_ref[...])
        a = g * (1.0 / (1.0 + jnp.exp(-g))) * u
        _store_token_tiles(y_ref, _dot(a.astype(_BF16), wdb_ref[...]))

    @pl.when(i >= nu_ref[0])
    def _():
        y_ref[...] = jnp.zeros(y_ref.shape, y_ref.dtype)


def _combine_kernel(dest_hbm, rf_ref, h_ref, y_hbm, eye_ref, fnorm_ref, out_ref,
                    idx_smem, ybuf, isem, rsem, *, tm, final_norm):
    i = pl.program_id(0)
    icp = pltpu.make_async_copy(dest_hbm.at[i], idx_smem, isem)
    icp.start()
    icp.wait()

    parts = ybuf.shape[0] // (2 * tm)

    def issue(t, c):
        for k in range(2):
            d = idx_smem[0, k * tm + t]
            pltpu.make_async_copy(y_hbm.at[pl.ds(pl.multiple_of(d * parts, parts), parts), :],
                                  ybuf.at[pl.ds(pl.multiple_of((k * tm + t) * parts, parts), parts), :],
                                  rsem).start(priority=k)
        return c

    lax.fori_loop(0, tm, issue, 0, unroll=8)
    gt = lax.dot_general(eye_ref[...], rf_ref[0], _NT_DIMS, precision=lax.Precision.HIGHEST,
                         preferred_element_type=_F32)
    pltpu.make_async_copy(y_hbm.at[pl.ds(0, 2 * tm * parts), :], ybuf, rsem).wait()
    out = h_ref[...] + (_load_token_tiles(ybuf, 0, tm, parts) * gt[:, 0:1]
                        + _load_token_tiles(ybuf, tm, tm, parts) * gt[:, 1:2])
    if final_norm:
        out = _rms(out, fnorm_ref[...])
    out_ref[...] = out


def _rope128(x, cos_t, sin_t):
    lane = lax.broadcasted_iota(jnp.int32, (1, 128), 1)
    first_half = (lane % QK_ROPE_DIM) < (QK_ROPE_DIM // 2)
    swapped = jnp.where(first_half, pltpu.roll(x, 128 - QK_ROPE_DIM // 2, axis=1),
                        pltpu.roll(x, QK_ROPE_DIM // 2, axis=1))
    return x * cos_t + swapped * sin_t


def _proj_kernel(h_ref, cos_ref, sin_ref, kvn_ref, wdkv_ref, kvlat_ref, wuk_ref, wuv_ref,
                 bnorm_ref, wdq_ref, qnorm_ref, wuq_ref, q_ref, k_ref, v_ref):
    h = h_ref[...]
    cos_t = cos_ref[...]
    sin_t = sin_ref[...]
    c = _dot(_rms(h, kvn_ref[...]).astype(_BF16), wdkv_ref[...])
    ckv = _rms(c[:, :KV_LORA_RANK], kvlat_ref[...]).astype(_BF16)
    kr = _rope128(c[:, KV_LORA_RANK:KV_LORA_RANK + 128], cos_t, sin_t)[:, :QK_ROPE_DIM].astype(_BF16)
    kn = _dot(ckv, wuk_ref[...])
    vv = _dot(ckv, wuv_ref[...])
    cq = _rms(_dot(_rms(h, bnorm_ref[...]).astype(_BF16), wdq_ref[...]), qnorm_ref[...]).astype(_BF16)
    q = _dot(cq, wuq_ref[...])
    rope0 = N_HEADS * QK_NOPE_DIM
    for hd in range(N_HEADS):
        k_ref[hd, :, 0:QK_NOPE_DIM] = kn[:, hd * QK_NOPE_DIM:(hd + 1) * QK_NOPE_DIM].astype(_BF16)
        k_ref[hd, :, QK_NOPE_DIM:QK_DIM] = kr
        v_ref[hd, 0] = vv[:, hd * V_HEAD_DIM:(hd + 1) * V_HEAD_DIM].T.astype(_BF16)
        q_ref[hd, :, 0:QK_NOPE_DIM] = (q[:, hd * QK_NOPE_DIM:(hd + 1) * QK_NOPE_DIM] * Q_SCALE).astype(_BF16)
        qr = _rope128(q[:, rope0 + hd * 128:rope0 + (hd + 1) * 128], cos_t, sin_t)[:, :QK_ROPE_DIM]
        q_ref[hd, :, QK_NOPE_DIM:QK_DIM] = (qr * Q_SCALE).astype(_BF16)


def _attn_kernel(q_ref, k_ref, vt_ref, km_ref, vmt_ref, o_ref, sa_ref, sb_ref, acc_ref, *, tq, tk):
    j = pl.program_id(2)
    heads = range(q_ref.shape[0])
    vd = acc_ref.shape[1]

    def scores(kb, s_ref):
        start = pl.multiple_of(kb * tk, tk)
        for hd in heads:
            s_ref[hd] = lax.dot_general(k_ref[hd, pl.ds(start, tk), :], q_ref[hd], _NT_DIMS,
                                        preferred_element_type=_F32)

    def update(kb, s_ref, carry, masked):
        out = []
        for hd in heads:
            m, l = carry[hd]
            s = s_ref[hd]
            if masked:
                keys = kb * tk + lax.broadcasted_iota(jnp.int32, (tk, tq), 0)
                qpos = j * tq + lax.broadcasted_iota(jnp.int32, (tk, tq), 1)
                s = jnp.where(keys <= qpos, s, NEG_INF)
            m_new = jnp.maximum(m, jnp.max(s, axis=0, keepdims=True))
            alpha = jnp.exp2(m - m_new)
            p = jnp.exp2(s - m_new)
            l = alpha * l + jnp.sum(p, axis=0, keepdims=True)
            acc_ref[hd] = acc_ref[hd] * alpha + _dot(vt_ref[hd, kb], p.astype(_BF16))
            out.append((m_new, l))
        return tuple(out)

    scores(0, sa_ref)
    carry = []
    for hd in heads:
        s0 = lax.dot_general(km_ref[hd], q_ref[hd], _NT_DIMS, preferred_element_type=_F32)
        m = jnp.max(s0, axis=0, keepdims=True)
        p0 = jnp.exp2(s0 - m)
        carry.append((m, jnp.sum(p0, axis=0, keepdims=True)))
        acc_ref[hd] = _dot(vmt_ref[hd, 0, :, 0:N_META], p0.astype(_BF16))
    carry = tuple(carry)

    def pair(kp, carry):
        scores(2 * kp + 1, sb_ref)
        carry = update(2 * kp, sa_ref, carry, False)
        scores(2 * kp + 2, sa_ref)
        return update(2 * kp + 1, sb_ref, carry, False)

    carry = lax.fori_loop(0, j, pair, carry)
    scores(2 * j + 1, sb_ref)
    carry = update(2 * j, sa_ref, carry, True)
    carry = update(2 * j + 1, sb_ref, carry, True)
    for hd in heads:
        o_ref[:, hd * vd:(hd + 1) * vd] = (acc_ref[hd] / carry[hd][1]).T.astype(o_ref.dtype)


def _oproj_kernel(o_ref, h_ref, wo_ref, fnorm_ref, wrt_ref, br_ref, triu_ref,
                  h_out_ref, xn_ref, ri_ref, rf_ref, cnt_ref, base_ref, wsplit_ref):
    i = pl.program_id(0)

    @pl.when(i == 0)
    def _():
        _route_init(wrt_ref, wsplit_ref, base_ref)

    h = h_ref[...] + _dot(o_ref[...], wo_ref[...])
    h_out_ref[...] = h
    xn = _rms(h, fnorm_ref[...])
    _store_token_tiles(xn_ref, xn)
    valid = lax.broadcasted_iota(jnp.int32, (1, h.shape[0]), 1) >= 0
    _route(xn, valid, wsplit_ref, br_ref, triu_ref, base_ref, ri_ref, rf_ref, cnt_ref)


def _full(shape):
    nd = len(shape)
    return pl.BlockSpec(shape, lambda *_: (0,) * nd)


def _router_operands(router_g, router_g_bias, router_e, router_e_bias):
    d = router_g.shape[0]
    wrt = jnp.concatenate([router_g.T, jnp.zeros((8 - N_EXPERT_GROUPS, d), _F32), router_e.T], axis=0)
    br = jnp.concatenate([router_g_bias, jnp.zeros((8 - N_EXPERT_GROUPS,), _F32), router_e_bias])[:, None]
    return wrt.astype(_F32), br.astype(_F32)


def _route_out(n_tiles, tm):
    shapes = [jax.ShapeDtypeStruct((n_tiles, 8, tm), jnp.int32),
              jax.ShapeDtypeStruct((n_tiles, 8, tm), _F32),
              jax.ShapeDtypeStruct((N_EXPERTS, 128), jnp.int32)]
    specs = [pl.BlockSpec((1, 8, tm), lambda i: (i, 0, 0)),
             pl.BlockSpec((1, 8, tm), lambda i: (i, 0, 0)),
             pl.BlockSpec((N_EXPERTS, 128), lambda i: (0, 0))]
    return shapes, specs


def _moe(h, xn, ri, rf, counts, w_gate, w_up, w_down, layer, n_tiles, n_real_tiles, out_tiles, final_norm_w):
    tm, tr = TOKEN_TILE, EXPERT_ROWS
    d = h.shape[1]
    parts = d // LANES
    n_valid = n_real_tiles * tm + (n_tiles - n_real_tiles) * N_META
    n_blocks = -(-(2 * n_valid + N_EXPERTS * (tr - 1)) // tr)
    n_rows = n_blocks * tr

    counts = counts[:, 0]
    padded = (counts + tr - 1) // tr * tr
    pends = jnp.cumsum(padded)
    pstarts = pends - padded
    n_used = (pends[-1] // tr).astype(jnp.int32).reshape(1)
    blk0 = jnp.arange(n_blocks, dtype=jnp.int32) * tr
    block_e = jnp.minimum(jnp.sum(blk0[:, None] >= pends[None, :], axis=1), N_EXPERTS - 1).astype(jnp.int32)
    experts = jnp.arange(N_EXPERTS, dtype=jnp.int32)
    block_oh = block_e[:, None] == experts[None, :]
    block_cnt = jnp.sum(jnp.where(block_oh, counts[None, :], 0), axis=1)
    block_start = jnp.sum(jnp.where(block_oh, pstarts[None, :], 0), axis=1)
    block_valid = jnp.clip(block_cnt - (blk0 - block_start), 0, tr).astype(jnp.int32)
    eid = ri[:, 0:2, :]
    slot0 = jnp.sum(jnp.where(eid[..., None] == experts, pstarts, 0), axis=-1)
    dest = (slot0 + ri[:, 2:4, :]).astype(jnp.int32).reshape(n_tiles, 1, 2 * tm)

    xs = pl.pallas_call(
        functools.partial(_dispatch_kernel, tm=tm, n_real_tiles=n_real_tiles),
        grid=(n_tiles,),
        in_specs=[pl.BlockSpec(memory_space=pl.ANY), pl.BlockSpec((tm * parts, LANES), lambda i: (i, 0))],
        out_specs=pl.BlockSpec(memory_space=pl.ANY),
        out_shape=jax.ShapeDtypeStruct((n_rows * parts, LANES), _F32),
        scratch_shapes=[pltpu.SMEM((1, 2 * tm), jnp.int32), pltpu.SemaphoreType.DMA, pltpu.SemaphoreType.DMA],
        compiler_params=_params(1, has_side_effects=True, disable_bounds_checks=True),
        name="moe_dispatch",
    )(dest, xn)

    f = w_gate.shape[3]
    expert_block = lambda i, be, bv, nu: (layer, be[last(i, be, bv, nu)], 0, 0)
    last = lambda i, be, bv, nu: jnp.minimum(i, nu[0] - 1)
    y = pl.pallas_call(
        _expert_kernel,
        grid_spec=pltpu.PrefetchScalarGridSpec(
            num_scalar_prefetch=3,
            grid=(n_blocks,),
            in_specs=[pl.BlockSpec((tr * parts, LANES), lambda i, be, bv, nu: (last(i, be, bv, nu), 0)),
                      pl.BlockSpec((1, 1, d, f), expert_block),
                      pl.BlockSpec((1, 1, d, f), expert_block),
                      pl.BlockSpec((1, 1, f, d), expert_block)],
            out_specs=pl.BlockSpec((tr * parts, LANES), lambda i, be, bv, nu: (i, 0)),
            scratch_shapes=[pltpu.VMEM((d, f), _BF16), pltpu.VMEM((d, f), _BF16), pltpu.VMEM((f, d), _BF16)]),
        out_shape=jax.ShapeDtypeStruct((n_rows * parts, LANES), _F32),
        compiler_params=_params(1),
        name="moe_experts",
    )(block_e, block_valid, n_used, xs, w_gate, w_up, w_down)

    eye = jnp.asarray(np.eye(tm, dtype=np.float32))
    fin = final_norm_w is not None
    fnorm = (final_norm_w if fin else jnp.ones((d,), _F32)).reshape(1, d)
    out = pl.pallas_call(
        functools.partial(_combine_kernel, tm=tm, final_norm=fin),
        grid=(out_tiles,),
        in_specs=[pl.BlockSpec(memory_space=pl.ANY),
                  pl.BlockSpec((1, 8, tm), lambda i: (i, 0, 0)),
                  pl.BlockSpec((tm, d), lambda i: (i, 0)),
                  pl.BlockSpec(memory_space=pl.ANY),
                  _full((tm, tm)), _full((1, d))],
        out_specs=pl.BlockSpec((tm, d), lambda i: (i, 0)),
        out_shape=jax.ShapeDtypeStruct((out_tiles * tm, d), _F32),
        scratch_shapes=[pltpu.SMEM((1, 2 * tm), jnp.int32), pltpu.VMEM((2 * tm * parts, LANES), _F32),
                        pltpu.SemaphoreType.DMA, pltpu.SemaphoreType.DMA],
        compiler_params=_params(1, disable_bounds_checks=True),
        name="moe_combine",
    )(dest, rf, h, y, eye, fnorm)
    return out


def kernel(x, meta_tokens, a_norm, a_w, a_scale, b_norm, b_w_dq, b_q_norm, b_w_uq, b_w_o, kv_norm, w_dkv,
           kv_lat_norm, w_uk, w_uv, ffn_norm, router_g, router_g_bias, router_e, router_e_bias, w_gate, w_up,
           w_down, final_norm):
    bsz, seq, d = x.shape
    tm = TOKEN_TILE
    assert seq % tm == 0 and seq % ATTN_Q_TILE == 0 and ATTN_Q_TILE == 2 * ATTN_K_TILE and ATTN_K_TILE == tm
    assert d % (LANES * len(POOL_WINDOWS)) == 0 and N_META == max(POOL_WINDOWS) and N_META <= tm
    parts = d // LANES
    n_tok = bsz * seq
    n_real_tiles = n_tok // tm
    tiles_per_batch = seq // tm
    n_tiles = n_real_tiles + 1
    gd = d // len(POOL_WINDOWS)
    row = lambda v: v.reshape(1, -1).astype(_F32)

    x2 = x.reshape(n_tok, d)
    meta_pad = jnp.concatenate([meta_tokens, jnp.zeros((tm - N_META, d), x.dtype)], axis=0)

    r = np.arange(tm)[:, None]
    cidx = np.arange(tm)[None, :]
    pm = np.stack([((r - cidx >= 0) & (r - cidx < w)) for w in POOL_WINDOWS]).astype(np.float32)
    hc = np.arange(N_META)[None, :]
    ph = np.stack([(r[:N_META] + N_META - hc < w) for w in POOL_WINDOWS]).astype(np.float32)
    triu = (r < cidx).astype(np.float32)
    pm, ph, triu = (jnp.asarray(a, dtype=_BF16) for a in (pm, ph, triu))

    wrt0, br0 = _router_operands(router_g[0], router_g_bias[0], router_e[0], router_e_bias[0])
    route_shapes, route_specs = _route_out(n_tiles, tm)
    tile_or_last = lambda i: (jnp.minimum(i, n_real_tiles - 1), 0)
    halo_blocks = tm // N_META
    h1, xn1, ri1, rf1, cnt1 = pl.pallas_call(
        functools.partial(_mixer_kernel, n_real_tiles=n_real_tiles, tiles_per_batch=tiles_per_batch),
        grid=(n_tiles,),
        in_specs=[pl.BlockSpec((tm, d), tile_or_last),
                  pl.BlockSpec((N_META, d), lambda i: (jnp.clip(i * halo_blocks - 1, 0, n_tok // N_META - 1), 0)),
                  _full((tm, d)), _full((N_META, d)), _full((1, d)),
                  _full((len(POOL_WINDOWS), gd, gd)), _full((1, d)), _full((1, d)),
                  _full((ROUTER_ROWS, d)), _full((ROUTER_ROWS, 1)),
                  _full(pm.shape), _full(ph.shape), _full((tm, tm))],
        out_specs=[pl.BlockSpec((tm, d), lambda i: (i, 0)),
                   pl.BlockSpec((tm * parts, LANES), lambda i: (i, 0))] + route_specs,
        out_shape=[jax.ShapeDtypeStruct((n_tiles * tm, d), _F32),
                   jax.ShapeDtypeStruct((n_tiles * tm * parts, LANES), _F32)] + route_shapes,
        scratch_shapes=[pltpu.VMEM((N_EXPERTS, 1), _F32), pltpu.VMEM((ROUTER_SPLIT_ROWS, d), _BF16)],
        compiler_params=_params(1),
        name="pool_mixer_router",
    )(x2, x2, meta_pad, meta_tokens, row(a_norm[0]), a_w[0].astype(_BF16), row(a_scale[0]), row(ffn_norm[0]),
      wrt0, br0, pm, ph, triu)

    h2 = _moe(h1, xn1, ri1, rf1, cnt1, w_gate, w_up, w_down, 0, n_tiles, n_real_tiles, n_tiles, None)

    pos = jnp.concatenate([jnp.arange(seq, dtype=_F32) + N_META, jnp.arange(tm, dtype=_F32)])
    inv_freq = ROPE_THETA ** (-jnp.arange(0, QK_ROPE_DIM, 2, dtype=_F32) / QK_ROPE_DIM)
    ang = pos[:, None] * inv_freq[None, :]
    cos_t = jnp.tile(jnp.cos(ang), (1, 4))
    sin_t = jnp.tile(jnp.concatenate([-jnp.sin(ang), jnp.sin(ang)], axis=1), (1, 2))

    wdkv = jnp.concatenate([w_dkv, w_dkv[:, KV_LORA_RANK:]], axis=1).astype(_BF16)
    wuk = w_uk.reshape(KV_LORA_RANK, N_HEADS * QK_NOPE_DIM).astype(_BF16)
    wuv = w_uv.reshape(KV_LORA_RANK, N_HEADS * V_HEAD_DIM).astype(_BF16)
    wuq = b_w_uq[0]
    q_rank = wuq.shape[0]
    wuq_rope = wuq[:, :, QK_NOPE_DIM:]
    wuq = jnp.concatenate([wuq[:, :, :QK_NOPE_DIM].reshape(q_rank, -1),
                           jnp.concatenate([wuq_rope, wuq_rope], axis=2).reshape(q_rank, -1)], axis=1).astype(_BF16)
    wdq = b_w_dq[0].astype(_BF16)
    pos_tile = lambda i: (jnp.where(i < n_real_tiles, i % tiles_per_batch, tiles_per_batch), 0)
    head_tile = lambda i: (0, i, 0)
    q, k, v = pl.pallas_call(
        _proj_kernel,
        grid=(n_tiles,),
        in_specs=[pl.BlockSpec((tm, d), lambda i: (i, 0)),
                  pl.BlockSpec((tm, 128), pos_tile), pl.BlockSpec((tm, 128), pos_tile),
                  _full((1, d)), _full(wdkv.shape), _full((1, KV_LORA_RANK)), _full(wuk.shape), _full(wuv.shape),
                  _full((1, d)), _full(wdq.shape), _full((1, q_rank)), _full(wuq.shape)],
        out_specs=[pl.BlockSpec((N_HEADS, tm, QK_DIM), head_tile),
                   pl.BlockSpec((N_HEADS, tm, QK_DIM), head_tile),
                   pl.BlockSpec((N_HEADS, 1, V_HEAD_DIM, tm), lambda i: (0, i, 0, 0))],
        out_shape=[jax.ShapeDtypeStruct((N_HEADS, n_tiles * tm, QK_DIM), _BF16),
                   jax.ShapeDtypeStruct((N_HEADS, n_tiles * tm, QK_DIM), _BF16),
                   jax.ShapeDtypeStruct((N_HEADS, n_tiles, V_HEAD_DIM, tm), _BF16)],
        compiler_params=_params(1),
        name="latent_qkv",
    )(h2, cos_t, sin_t, row(kv_norm), wdkv, row(kv_lat_norm), wuk, wuv,
      row(b_norm[0]), wdq, row(b_q_norm[0]), wuq)

    tq, tk, hp = ATTN_Q_TILE, ATTN_K_TILE, ATTN_HEADS_PER_STEP
    n_q = seq // tq
    meta_block = n_tok // N_META
    o = pl.pallas_call(
        functools.partial(_attn_kernel, tq=tq, tk=tk),
        grid=(bsz, N_HEADS // hp, n_q),
        in_specs=[pl.BlockSpec((hp, tq, QK_DIM), lambda b, hg, j: (hg, b * n_q + j, 0)),
                  pl.BlockSpec((hp, seq, QK_DIM), lambda b, hg, j: (hg, b, 0)),
                  pl.BlockSpec((hp, tiles_per_batch, V_HEAD_DIM, tm), lambda b, hg, j: (hg, b, 0, 0)),
                  pl.BlockSpec((hp, N_META, QK_DIM), lambda b, hg, j: (hg, meta_block, 0)),
                  pl.BlockSpec((hp, 1, V_HEAD_DIM, tm), lambda b, hg, j: (hg, n_real_tiles, 0, 0))],
        out_specs=pl.BlockSpec((tq, hp * V_HEAD_DIM), lambda b, hg, j: (b * n_q + j, hg)),
        out_shape=jax.ShapeDtypeStruct((n_tok, N_HEADS * V_HEAD_DIM), _BF16),
        scratch_shapes=[pltpu.VMEM((hp, tk, tq), _F32), pltpu.VMEM((hp, tk, tq), _F32),
                        pltpu.VMEM((hp, V_HEAD_DIM, tq), _F32)],
        compiler_params=_params(3),
        name="causal_attention",
    )(q, k, v, k, v)

    wrt1, br1 = _router_operands(router_g[1], router_g_bias[1], router_e[1], router_e_bias[1])
    route_shapes, route_specs = _route_out(n_real_tiles, tm)
    h3, xn2, ri2, rf2, cnt2 = pl.pallas_call(
        _oproj_kernel,
        grid=(n_real_tiles,),
        in_specs=[pl.BlockSpec((tm, N_HEADS * V_HEAD_DIM), lambda i: (i, 0)),
                  pl.BlockSpec((tm, d), lambda i: (i, 0)),
                  _full((N_HEADS * V_HEAD_DIM, d)), _full((1, d)),
                  _full((ROUTER_ROWS, d)), _full((ROUTER_ROWS, 1)), _full((tm, tm))],
        out_specs=[pl.BlockSpec((tm, d), lambda i: (i, 0)),
                   pl.BlockSpec((tm * parts, LANES), lambda i: (i, 0))] + route_specs,
        out_shape=[jax.ShapeDtypeStruct((n_tok, d), _F32),
                   jax.ShapeDtypeStruct((n_tok * parts, LANES), _F32)] + route_shapes,
        scratch_shapes=[pltpu.VMEM((N_EXPERTS, 1), _F32), pltpu.VMEM((ROUTER_SPLIT_ROWS, d), _BF16)],
        compiler_params=_params(1),
        name="attn_out_router",
    )(o, h2, b_w_o[0].astype(_BF16), row(ffn_norm[1]), wrt1, br1, triu)

    out = _moe(h3, xn2, ri2, rf2, cnt2, w_gate, w_up, w_down, 1,
               n_real_tiles, n_real_tiles, n_real_tiles, final_norm)
    return out.reshape(bsz, seq, d)
```

```python
import functools

import numpy as np
import jax
import jax.numpy as jnp
from jax import lax
from jax.experimental import pallas as pl
from jax.experimental.pallas import tpu as pltpu

N_META = 16
POOL_WINDOWS = (2, 4, 8, 16)
N_HEADS = 8
QK_NOPE_DIM = 128
QK_ROPE_DIM = 64
QK_DIM = QK_NOPE_DIM + QK_ROPE_DIM
V_HEAD_DIM = 128
KV_LORA_RANK = 256
ROPE_THETA = 10000.0
ATTN_SCALE = QK_DIM ** -0.5
Q_SCALE = ATTN_SCALE * 1.4426950408889634
N_EXPERT_GROUPS = 4
EXPERTS_PER_GROUP = 8
N_EXPERTS = N_EXPERT_GROUPS * EXPERTS_PER_GROUP
RMS_EPS = 1e-6
NEG_INF = -1e30

TOKEN_TILE = 256
EXPERT_ROWS = 256
ATTN_Q_TILE = 512
ATTN_K_TILE = ATTN_Q_TILE // 2
ATTN_HEADS_PER_STEP = 4
DMA_UNROLL = 8
DMA_THREADS = 2
ROUTER_ROWS = 8 + N_EXPERTS
ROUTER_LO_ROW = 48
ROUTER_SPLIT_ROWS = 2 * ROUTER_LO_ROW
VMEM_LIMIT_BYTES = 48 * 1024 * 1024

LANES = 128
_F32 = jnp.float32
_BF16 = jnp.bfloat16
_NT_DIMS = (((1,), (1,)), ((), ()))


def _params(n_grid_dims=1, **kw):
    return pltpu.CompilerParams(dimension_semantics=("arbitrary",) * n_grid_dims,
                                vmem_limit_bytes=VMEM_LIMIT_BYTES, **kw)


def _rms(x, g):
    ms = jnp.mean(x * x, axis=-1, keepdims=True)
    return x * lax.rsqrt(ms + RMS_EPS) * g


def _split_bf16(x):
    hi = x.astype(_BF16)
    lo = (x - hi.astype(_F32)).astype(_BF16)
    return hi, lo


def _dot(a, b):
    return jnp.dot(a, b, preferred_element_type=_F32)


def _store_token_tiles(ref, x):
    n, d = x.shape
    parts = d // LANES
    for s in range(parts):
        ref[pl.ds(s, n, stride=parts), :] = x[:, s * LANES:(s + 1) * LANES]


def _load_token_tiles(ref, first_token, n, parts):
    return jnp.concatenate([ref[pl.ds(first_token * parts + s, n, stride=parts), :] for s in range(parts)], axis=1)


def _route_init(wrt_ref, wsplit_ref, base_ref):
    base_ref[...] = jnp.zeros(base_ref.shape, base_ref.dtype)
    w_hi, w_lo = _split_bf16(wrt_ref[...])
    wsplit_ref[...] = jnp.zeros(wsplit_ref.shape, wsplit_ref.dtype)
    wsplit_ref[0:ROUTER_ROWS, :] = w_hi
    wsplit_ref[ROUTER_LO_ROW:ROUTER_LO_ROW + ROUTER_ROWS, :] = w_lo


def _route(xn, valid, wsplit_ref, br_ref, triu_ref, base_ref, ri_ref, rf_ref, cnt_ref):
    tm = xn.shape[0]
    x_hi, x_lo = _split_bf16(xn)
    by_hi = lax.dot_general(wsplit_ref[...], x_hi, _NT_DIMS, preferred_element_type=_F32)
    by_lo = lax.dot_general(wsplit_ref[0:ROUTER_LO_ROW, :], x_lo, _NT_DIMS, preferred_element_type=_F32)
    logits = (by_hi[0:ROUTER_ROWS] + by_hi[ROUTER_LO_ROW:ROUTER_LO_ROW + ROUTER_ROWS]
              + by_lo[0:ROUTER_ROWS] + br_ref[...])
    lg = logits[0:N_EXPERT_GROUPS]
    eg = jnp.exp(lg - jnp.max(lg, axis=0, keepdims=True))
    pg = eg / jnp.sum(eg, axis=0, keepdims=True)
    w_g = jnp.max(pg, axis=0, keepdims=True)
    ig = lax.broadcasted_iota(jnp.int32, pg.shape, 0).astype(_F32)
    g_sel = jnp.min(jnp.where(pg == w_g, ig, float(N_EXPERT_GROUPS)), axis=0, keepdims=True)

    sel = logits[8:8 + EXPERTS_PER_GROUP]
    for g in range(1, N_EXPERT_GROUPS):
        sel = jnp.where(g_sel == float(g), logits[8 + g * EXPERTS_PER_GROUP:8 + (g + 1) * EXPERTS_PER_GROUP], sel)
    ie = lax.broadcasted_iota(jnp.int32, sel.shape, 0).astype(_F32)
    v1 = jnp.max(sel, axis=0, keepdims=True)
    i1 = jnp.min(jnp.where(sel == v1, ie, float(EXPERTS_PER_GROUP)), axis=0, keepdims=True)
    rest = jnp.where(ie == i1, -jnp.inf, sel)
    v2 = jnp.max(rest, axis=0, keepdims=True)
    i2 = jnp.min(jnp.where(rest == v2, ie, float(EXPERTS_PER_GROUP)), axis=0, keepdims=True)
    e2 = jnp.exp(v2 - v1)
    den = 1.0 + e2
    validf = valid.astype(_F32)
    gate0 = w_g * (1.0 / den) * validf
    gate1 = w_g * (e2 / den) * validf
    eid0 = g_sel * float(EXPERTS_PER_GROUP) + i1
    eid1 = g_sel * float(EXPERTS_PER_GROUP) + i2

    iall = lax.broadcasted_iota(jnp.int32, (N_EXPERTS, tm), 0).astype(_F32)
    oh0 = jnp.where(iall == eid0, validf, 0.0)
    oh1 = jnp.where(iall == eid1, validf, 0.0)
    both = oh0 + oh1
    before = _dot(both.astype(_BF16), triu_ref[...]) + base_ref[...]
    rank0 = jnp.sum(oh0 * before, axis=0, keepdims=True)
    rank1 = jnp.sum(oh1 * before, axis=0, keepdims=True)
    base_ref[...] = base_ref[...] + jnp.sum(both, axis=1, keepdims=True)

    ri_ref[...] = jnp.zeros(ri_ref.shape, ri_ref.dtype)
    rf_ref[...] = jnp.zeros(rf_ref.shape, rf_ref.dtype)
    ri_ref[0, 0:1, :] = eid0.astype(jnp.int32)
    ri_ref[0, 1:2, :] = eid1.astype(jnp.int32)
    ri_ref[0, 2:3, :] = rank0.astype(jnp.int32)
    ri_ref[0, 3:4, :] = rank1.astype(jnp.int32)
    rf_ref[0, 0:1, :] = gate0
    rf_ref[0, 1:2, :] = gate1
    cnt_ref[...] = jnp.broadcast_to(base_ref[...], cnt_ref.shape).astype(jnp.int32)


def _mixer_kernel(x_ref, xh_ref, mp_ref, meta_ref, anorm_ref, aw_ref, ascale_ref, fnorm_ref,
                  wrt_ref, br_ref, pm_ref, ph_ref, triu_ref,
                  h_ref, xn_ref, ri_ref, rf_ref, cnt_ref, base_ref, wsplit_ref, *, n_real_tiles, tiles_per_batch):
    i = pl.program_id(0)
    tm = x_ref.shape[0]
    gd = x_ref.shape[1] // len(POOL_WINDOWS)
    is_meta = i == n_real_tiles
    first = (i % tiles_per_batch) == 0

    @pl.when(i == 0)
    def _():
        _route_init(wrt_ref, wsplit_ref, base_ref)

    h = jnp.where(is_meta, mp_ref[...], x_ref[...])
    halo = jnp.where(is_meta, 0.0, jnp.where(first, meta_ref[...], xh_ref[...]))
    hn = _rms(h, anorm_ref[...])
    hh = _rms(halo, anorm_ref[...])
    hn_hi, hn_lo = _split_bf16(hn)
    hh_hi, hh_lo = _split_bf16(hh)
    row = lax.broadcasted_iota(jnp.int32, (tm, 1), 0)
    for g, w in enumerate(POOL_WINDOWS):
        sl = slice(g * gd, (g + 1) * gd)
        win = _dot(pm_ref[g], hn_hi[:, sl]) + _dot(pm_ref[g], hn_lo[:, sl])
        top = win[:N_META] + _dot(ph_ref[g], hh_hi[:, sl]) + _dot(ph_ref[g], hh_lo[:, sl])
        win = jnp.concatenate([top, win[N_META:]], axis=0)
        cnt = jnp.where(is_meta, jnp.minimum(row + 1, w), w).astype(_F32)
        pooled = win * (1.0 / cnt) - hn[:, sl]
        mix = _dot(pooled.astype(_BF16), aw_ref[g])
        h_ref[:, sl] = h[:, sl] + mix * ascale_ref[:, sl]

    xn = _rms(h_ref[...], fnorm_ref[...])
    _store_token_tiles(xn_ref, xn)
    lane = lax.broadcasted_iota(jnp.int32, (1, tm), 1)
    valid = jnp.logical_or(jnp.logical_not(is_meta), lane < N_META)
    _route(xn, valid, wsplit_ref, br_ref, triu_ref, base_ref, ri_ref, rf_ref, cnt_ref)


def _dispatch_kernel(dest_hbm, xn_hbm, xs_hbm, idx_smem, xbuf, isem, lsem, ssem, *, tm, n_tiles, n_real_tiles):
    i = pl.program_id(0)
    parts = xbuf.shape[1] // tm
    buf = i % 3

    def idx_copy(tile):
        return pltpu.make_async_copy(dest_hbm.at[tile], idx_smem.at[pl.ds(tile % 2, 1)], isem.at[tile % 2])

    def load(tile):
        rows = tm * parts
        return pltpu.make_async_copy(xn_hbm.at[pl.ds(pl.multiple_of(tile * rows, rows), rows), :],
                                     xbuf.at[tile % 3], lsem.at[tile % 3])

    def wait_scatters(tile):
        def wait_rows(n_tok):
            for k in range(2):
                pltpu.make_async_copy(xbuf.at[tile % 3, pl.ds(0, n_tok * parts), :],
                                      xs_hbm.at[pl.ds(0, n_tok * parts), :], ssem.at[tile % 3]).wait()

        @pl.when(tile < n_real_tiles)
        def _():
            wait_rows(tm)

        @pl.when(tile >= n_real_tiles)
        def _():
            wait_rows(N_META)

    def scatter(n_tok):
        def issue(g, c):
            for u in range(min(DMA_UNROLL, n_tok)):
                t = g * min(DMA_UNROLL, n_tok) + u
                src = xbuf.at[buf, pl.ds(pl.multiple_of(t * parts, parts), parts), :]
                for k in range(2):
                    d = idx_smem[i % 2, k * tm + t]
                    pltpu.make_async_copy(src, xs_hbm.at[pl.ds(pl.multiple_of(d * parts, parts), parts), :],
                                          ssem.at[buf]).start(priority=(2 * u + k) % DMA_THREADS)
            return c

        lax.fori_loop(0, n_tok // min(DMA_UNROLL, n_tok), issue, 0)

    @pl.when(i == 0)
    def _():
        idx_copy(0).start()
        load(0).start()

    @pl.when(i >= 2)
    def _():
        wait_scatters(i - 2)

    @pl.when(i + 1 < n_tiles)
    def _():
        idx_copy(i + 1).start()
        load(i + 1).start()

    idx_copy(i).wait()
    load(i).wait()

    @pl.when(i < n_real_tiles)
    def _():
        scatter(tm)

    @pl.when(i >= n_real_tiles)
    def _():
        scatter(N_META)

    @pl.when(i == n_tiles - 1)
    def _():
        if n_tiles >= 2:
            wait_scatters(i - 1)
        wait_scatters(i)


def _expert_kernel(be_ref, bv_ref, nu_ref, xs_ref, wg_ref, wu_ref, wd_ref, y_ref, wgb_ref, wub_ref, wdb_ref):
    i = pl.program_id(0)
    active = i < nu_ref[0]
    new_expert = jnp.logical_or(i == 0, be_ref[i] != be_ref[jnp.maximum(i - 1, 0)])

    @pl.when(jnp.logical_and(active, new_expert))
    def _():
        wgb_ref[...] = wg_ref[0, 0].astype(_BF16)
        wub_ref[...] = wu_ref[0, 0].astype(_BF16)
        wdb_ref[...] = wd_ref[0, 0].astype(_BF16)

    @pl.when(active)
    def _():
        parts = wgb_ref.shape[0] // LANES
        tr = xs_ref.shape[0] // parts
        rows = lax.broadcasted_iota(jnp.int32, (tr, 1), 0)
        x = jnp.where(rows < bv_ref[i], _load_token_tiles(xs_ref, 0, tr, parts), 0.0).astype(_BF16)
        g = _dot(x, wgb_ref[...])
        u = _dot(x, wub_ref[...])
        a = g * (1.0 / (1.0 + jnp.exp(-g))) * u
        _store_token_tiles(y_ref, _dot(a.astype(_BF16), wdb_ref[...]))

    @pl.when(i >= nu_ref[0])
    def _():
        y_ref[...] = jnp.zeros(y_ref.shape, y_ref.dtype)


def _combine_kernel(dest_hbm, rf_ref, h_ref, y_hbm, eye_ref, fnorm_ref, out_ref,
                    idx_smem, ybuf, isem, rsem, *, tm, n_tiles, final_norm):
    i = pl.program_id(0)
    parts = ybuf.shape[1] // (2 * tm)
    slot = i % 2

    def idx_copy(tile):
        return pltpu.make_async_copy(dest_hbm.at[tile], idx_smem.at[pl.ds(tile % 2, 1)], isem.at[tile % 2])

    def gather(tile):
        def issue(g, c):
            for u in range(DMA_UNROLL):
                t = g * DMA_UNROLL + u
                for k in range(2):
                    d = idx_smem[tile % 2, k * tm + t]
                    pltpu.make_async_copy(
                        y_hbm.at[pl.ds(pl.multiple_of(d * parts, parts), parts), :],
                        ybuf.at[tile % 2, pl.ds(pl.multiple_of((k * tm + t) * parts, parts), parts), :],
                        rsem.at[tile % 2]).start(priority=(2 * u + k) % DMA_THREADS)
            return c

        lax.fori_loop(0, tm // DMA_UNROLL, issue, 0)

    @pl.when(i == 0)
    def _():
        idx_copy(0).start()
        idx_copy(0).wait()
        gather(0)
        if n_tiles >= 2:
            idx_copy(1).start()

    @pl.when(i + 1 < n_tiles)
    def _():
        idx_copy(i + 1).wait()
        gather(i + 1)

    @pl.when(i + 2 < n_tiles)
    def _():
        idx_copy(i + 2).start()

    gt = lax.dot_general(eye_ref[...], rf_ref[0], _NT_DIMS, precision=lax.Precision.HIGHEST,
                         preferred_element_type=_F32)
    pltpu.make_async_copy(y_hbm.at[pl.ds(0, 2 * tm * parts), :], ybuf.at[slot], rsem.at[slot]).wait()
    yb = ybuf.at[slot]
    out = h_ref[...] + (_load_token_tiles(yb, 0, tm, parts) * gt[:, 0:1]
                        + _load_token_tiles(yb, tm, tm, parts) * gt[:, 1:2])
    if final_norm:
        out = _rms(out, fnorm_ref[...])
    out_ref[...] = out


def _rope128(x, cos_t, sin_t):
    lane = lax.broadcasted_iota(jnp.int32, (1, 128), 1)
    first_half = (lane % QK_ROPE_DIM) < (QK_ROPE_DIM // 2)
    swapped = jnp.where(first_half, pltpu.roll(x, 128 - QK_ROPE_DIM // 2, axis=1),
                        pltpu.roll(x, QK_ROPE_DIM // 2, axis=1))
    return x * cos_t + swapped * sin_t


def _proj_kernel(h_ref, cos_ref, sin_ref, kvn_ref, wdkv_ref, kvlat_ref, wuk_ref, wuv_ref,
                 bnorm_ref, wdq_ref, qnorm_ref, wuq_ref, q_ref, k_ref, v_ref):
    h = h_ref[...]
    cos_t = cos_ref[...]
    sin_t = sin_ref[...]
    c = _dot(_rms(h, kvn_ref[...]).astype(_BF16), wdkv_ref[...])
    ckv = _rms(c[:, :KV_LORA_RANK], kvlat_ref[...]).astype(_BF16)
    kr = _rope128(c[:, KV_LORA_RANK:KV_LORA_RANK + 128], cos_t, sin_t)[:, :QK_ROPE_DIM].astype(_BF16)
    kn = _dot(ckv, wuk_ref[...])
    vv = _dot(ckv, wuv_ref[...])
    cq = _rms(_dot(_rms(h, bnorm_ref[...]).astype(_BF16), wdq_ref[...]), qnorm_ref[...]).astype(_BF16)
    q = _dot(cq, wuq_ref[...])
    rope0 = N_HEADS * QK_NOPE_DIM
    for hd in range(N_HEADS):
        k_ref[hd, :, 0:QK_NOPE_DIM] = kn[:, hd * QK_NOPE_DIM:(hd + 1) * QK_NOPE_DIM].astype(_BF16)
        k_ref[hd, :, QK_NOPE_DIM:QK_DIM] = kr
        v_ref[hd, 0] = vv[:, hd * V_HEAD_DIM:(hd + 1) * V_HEAD_DIM].T.astype(_BF16)
        q_ref[hd, :, 0:QK_NOPE_DIM] = (q[:, hd * QK_NOPE_DIM:(hd + 1) * QK_NOPE_DIM] * Q_SCALE).astype(_BF16)
        qr = _rope128(q[:, rope0 + hd * 128:rope0 + (hd + 1) * 128], cos_t, sin_t)[:, :QK_ROPE_DIM]
        q_ref[hd, :, QK_NOPE_DIM:QK_DIM] = (qr * Q_SCALE).astype(_BF16)


def _attn_kernel(q_ref, k_ref, vt_ref, km_ref, vmt_ref, o_ref, sa_ref, sb_ref, acc_ref, *, tq, tk):
    j = pl.program_id(2)
    heads = range(q_ref.shape[0])
    vd = acc_ref.shape[1]

    def scores(kb, s_ref):
        start = pl.multiple_of(kb * tk, tk)
        for hd in heads:
            s_ref[hd] = lax.dot_general(k_ref[hd, pl.ds(start, tk), :], q_ref[hd], _NT_DIMS,
                                        preferred_element_type=_F32)

    def update(kb, s_ref, carry, masked):
        out = []
        for hd in heads:
            m, l = carry[hd]
            s = s_ref[hd]
            if masked:
                keys = kb * tk + lax.broadcasted_iota(jnp.int32, (tk, tq), 0)
                qpos = j * tq + lax.broadcasted_iota(jnp.int32, (tk, tq), 1)
                s = jnp.where(keys <= qpos, s, NEG_INF)
            m_new = jnp.maximum(m, jnp.max(s, axis=0, keepdims=True))
            alpha = jnp.exp2(m - m_new)
            p = jnp.exp2(s - m_new)
            l = alpha * l + jnp.sum(p, axis=0, keepdims=True)
            acc_ref[hd] = acc_ref[hd] * alpha + _dot(vt_ref[hd, kb], p.astype(_BF16))
            out.append((m_new, l))
        return tuple(out)

    scores(0, sa_ref)
    carry = []
    for hd in heads:
        s0 = lax.dot_general(km_ref[hd], q_ref[hd], _NT_DIMS, preferred_element_type=_F32)
        m = jnp.max(s0, axis=0, keepdims=True)
        p0 = jnp.exp2(s0 - m)
        carry.append((m, jnp.sum(p0, axis=0, keepdims=True)))
        acc_ref[hd] = _dot(vmt_ref[hd, 0, :, 0:N_META], p0.astype(_BF16))
    carry = tuple(carry)

    def pair(kp, carry):
        scores(2 * kp + 1, sb_ref)
        carry = update(2 * kp, sa_ref, carry, False)
        scores(2 * kp + 2, sa_ref)
        return update(2 * kp + 1, sb_ref, carry, False)

    carry = lax.fori_loop(0, j, pair, carry)
    scores(2 * j + 1, sb_ref)
    carry = update(2 * j, sa_ref, carry, True)
    carry = update(2 * j + 1, sb_ref, carry, True)
    for hd in heads:
        o_ref[:, hd * vd:(hd + 1) * vd] = (acc_ref[hd] / carry[hd][1]).T.astype(o_ref.dtype)


def _oproj_kernel(o_ref, h_ref, wo_ref, fnorm_ref, wrt_ref, br_ref, triu_ref,
                  h_out_ref, xn_ref, ri_ref, rf_ref, cnt_ref, base_ref, wsplit_ref):
    i = pl.program_id(0)

    @pl.when(i == 0)
    def _():
        _route_init(wrt_ref, wsplit_ref, base_ref)

    h = h_ref[...] + _dot(o_ref[...], wo_ref[...])
    h_out_ref[...] = h
    xn = _rms(h, fnorm_ref[...])
    _store_token_tiles(xn_ref, xn)
    valid = lax.broadcasted_iota(jnp.int32, (1, h.shape[0]), 1) >= 0
    _route(xn, valid, wsplit_ref, br_ref, triu_ref, base_ref, ri_ref, rf_ref, cnt_ref)


def _full(shape):
    nd = len(shape)
    return pl.BlockSpec(shape, lambda *_: (0,) * nd)


def _router_operands(router_g, router_g_bias, router_e, router_e_bias):
    d = router_g.shape[0]
    wrt = jnp.concatenate([router_g.T, jnp.zeros((8 - N_EXPERT_GROUPS, d), _F32), router_e.T], axis=0)
    br = jnp.concatenate([router_g_bias, jnp.zeros((8 - N_EXPERT_GROUPS,), _F32), router_e_bias])[:, None]
    return wrt.astype(_F32), br.astype(_F32)


def _route_out(n_tiles, tm):
    shapes = [jax.ShapeDtypeStruct((n_tiles, 8, tm), jnp.int32),
              jax.ShapeDtypeStruct((n_tiles, 8, tm), _F32),
              jax.ShapeDtypeStruct((N_EXPERTS, 128), jnp.int32)]
    specs = [pl.BlockSpec((1, 8, tm), lambda i: (i, 0, 0)),
             pl.BlockSpec((1, 8, tm), lambda i: (i, 0, 0)),
             pl.BlockSpec((N_EXPERTS, 128), lambda i: (0, 0))]
    return shapes, specs


def _moe(h, xn, ri, rf, counts, w_gate, w_up, w_down, layer, n_tiles, n_real_tiles, out_tiles, final_norm_w):
    tm, tr = TOKEN_TILE, EXPERT_ROWS
    d = h.shape[1]
    parts = d // LANES
    n_valid = n_real_tiles * tm + (n_tiles - n_real_tiles) * N_META
    n_blocks = -(-(2 * n_valid + N_EXPERTS * (tr - 1)) // tr)
    n_rows = n_blocks * tr

    counts = counts[:, 0]
    padded = (counts + tr - 1) // tr * tr
    pends = jnp.cumsum(padded)
    pstarts = pends - padded
    n_used = (pends[-1] // tr).astype(jnp.int32).reshape(1)
    blk0 = jnp.arange(n_blocks, dtype=jnp.int32) * tr
    block_e = jnp.minimum(jnp.sum(blk0[:, None] >= pends[None, :], axis=1), N_EXPERTS - 1).astype(jnp.int32)
    experts = jnp.arange(N_EXPERTS, dtype=jnp.int32)
    block_oh = block_e[:, None] == experts[None, :]
    block_cnt = jnp.sum(jnp.where(block_oh, counts[None, :], 0), axis=1)
    block_start = jnp.sum(jnp.where(block_oh, pstarts[None, :], 0), axis=1)
    block_valid = jnp.clip(block_cnt - (blk0 - block_start), 0, tr).astype(jnp.int32)
    eid = ri[:, 0:2, :]
    slot0 = jnp.sum(jnp.where(eid[..., None] == experts, pstarts, 0), axis=-1)
    dest = (slot0 + ri[:, 2:4, :]).astype(jnp.int32).reshape(n_tiles, 1, 2 * tm)

    xs = pl.pallas_call(
        functools.partial(_dispatch_kernel, tm=tm, n_tiles=n_tiles, n_real_tiles=n_real_tiles),
        grid=(n_tiles,),
        in_specs=[pl.BlockSpec(memory_space=pl.ANY), pl.BlockSpec(memory_space=pl.ANY)],
        out_specs=pl.BlockSpec(memory_space=pl.ANY),
        out_shape=jax.ShapeDtypeStruct((n_rows * parts, LANES), _F32),
        scratch_shapes=[pltpu.SMEM((2, 2 * tm), jnp.int32), pltpu.VMEM((3, tm * parts, LANES), _F32),
                        pltpu.SemaphoreType.DMA((2,)), pltpu.SemaphoreType.DMA((3,)),
                        pltpu.SemaphoreType.DMA((3,))],
        compiler_params=_params(1, has_side_effects=True, disable_bounds_checks=True),
        name="moe_dispatch",
    )(dest, xn)

    f = w_gate.shape[3]
    expert_block = lambda i, be, bv, nu: (layer, be[last(i, be, bv, nu)], 0, 0)
    last = lambda i, be, bv, nu: jnp.minimum(i, nu[0] - 1)
    y = pl.pallas_call(
        _expert_kernel,
        grid_spec=pltpu.PrefetchScalarGridSpec(
            num_scalar_prefetch=3,
            grid=(n_blocks,),
            in_specs=[pl.BlockSpec((tr * parts, LANES), lambda i, be, bv, nu: (last(i, be, bv, nu), 0)),
                      pl.BlockSpec((1, 1, d, f), expert_block),
                      pl.BlockSpec((1, 1, d, f), expert_block),
                      pl.BlockSpec((1, 1, f, d), expert_block)],
            out_specs=pl.BlockSpec((tr * parts, LANES), lambda i, be, bv, nu: (i, 0)),
            scratch_shapes=[pltpu.VMEM((d, f), _BF16), pltpu.VMEM((d, f), _BF16), pltpu.VMEM((f, d), _BF16)]),
        out_shape=jax.ShapeDtypeStruct((n_rows * parts, LANES), _F32),
        compiler_params=_params(1),
        name="moe_experts",
    )(block_e, block_valid, n_used, xs, w_gate, w_up, w_down)

    eye = jnp.asarray(np.eye(tm, dtype=np.float32))
    fin = final_norm_w is not None
    fnorm = (final_norm_w if fin else jnp.ones((d,), _F32)).reshape(1, d)
    out = pl.pallas_call(
        functools.partial(_combine_kernel, tm=tm, n_tiles=out_tiles, final_norm=fin),
        grid=(out_tiles,),
        in_specs=[pl.BlockSpec(memory_space=pl.ANY),
                  pl.BlockSpec((1, 8, tm), lambda i: (i, 0, 0)),
                  pl.BlockSpec((tm, d), lambda i: (i, 0)),
                  pl.BlockSpec(memory_space=pl.ANY),
                  _full((tm, tm)), _full((1, d))],
        out_specs=pl.BlockSpec((tm, d), lambda i: (i, 0)),
        out_shape=jax.ShapeDtypeStruct((out_tiles * tm, d), _F32),
        scratch_shapes=[pltpu.SMEM((2, 2 * tm), jnp.int32), pltpu.VMEM((2, 2 * tm * parts, LANES), _F32),
                        pltpu.SemaphoreType.DMA((2,)), pltpu.SemaphoreType.DMA((2,))],
        compiler_params=_params(1, disable_bounds_checks=True),
        name="moe_combine",
    )(dest, rf, h, y, eye, fnorm)
    return out


def kernel(x, meta_tokens, a_norm, a_w, a_scale, b_norm, b_w_dq, b_q_norm, b_w_uq, b_w_o, kv_norm, w_dkv,
           kv_lat_norm, w_uk, w_uv, ffn_norm, router_g, router_g_bias, router_e, router_e_bias, w_gate, w_up,
           w_down, final_norm):
    bsz, seq, d = x.shape
    tm = TOKEN_TILE
    assert seq % tm == 0 and seq % ATTN_Q_TILE == 0 and ATTN_Q_TILE == 2 * ATTN_K_TILE and ATTN_K_TILE == tm
    assert d % (LANES * len(POOL_WINDOWS)) == 0 and N_META == max(POOL_WINDOWS) and N_META <= tm
    parts = d // LANES
    n_tok = bsz * seq
    n_real_tiles = n_tok // tm
    tiles_per_batch = seq // tm
    n_tiles = n_real_tiles + 1
    gd = d // len(POOL_WINDOWS)
    row = lambda v: v.reshape(1, -1).astype(_F32)

    x2 = x.reshape(n_tok, d)
    meta_pad = jnp.concatenate([meta_tokens, jnp.zeros((tm - N_META, d), x.dtype)], axis=0)

    r = np.arange(tm)[:, None]
    cidx = np.arange(tm)[None, :]
    pm = np.stack([((r - cidx >= 0) & (r - cidx < w)) for w in POOL_WINDOWS]).astype(np.float32)
    hc = np.arange(N_META)[None, :]
    ph = np.stack([(r[:N_META] + N_META - hc < w) for w in POOL_WINDOWS]).astype(np.float32)
    triu = (r < cidx).astype(np.float32)
    pm, ph, triu = (jnp.asarray(a, dtype=_BF16) for a in (pm, ph, triu))

    wrt0, br0 = _router_operands(router_g[0], router_g_bias[0], router_e[0], router_e_bias[0])
    route_shapes, route_specs = _route_out(n_tiles, tm)
    tile_or_last = lambda i: (jnp.minimum(i, n_real_tiles - 1), 0)
    halo_blocks = tm // N_META
    h1, xn1, ri1, rf1, cnt1 = pl.pallas_call(
        functools.partial(_mixer_kernel, n_real_tiles=n_real_tiles, tiles_per_batch=tiles_per_batch),
        grid=(n_tiles,),
        in_specs=[pl.BlockSpec((tm, d), tile_or_last),
                  pl.BlockSpec((N_META, d), lambda i: (jnp.clip(i * halo_blocks - 1, 0, n_tok // N_META - 1), 0)),
                  _full((tm, d)), _full((N_META, d)), _full((1, d)),
                  _full((len(POOL_WINDOWS), gd, gd)), _full((1, d)), _full((1, d)),
                  _full((ROUTER_ROWS, d)), _full((ROUTER_ROWS, 1)),
                  _full(pm.shape), _full(ph.shape), _full((tm, tm))],
        out_specs=[pl.BlockSpec((tm, d), lambda i: (i, 0)),
                   pl.BlockSpec((tm * parts, LANES), lambda i: (i, 0))] + route_specs,
        out_shape=[jax.ShapeDtypeStruct((n_tiles * tm, d), _F32),
                   jax.ShapeDtypeStruct((n_tiles * tm * parts, LANES), _F32)] + route_shapes,
        scratch_shapes=[pltpu.VMEM((N_EXPERTS, 1), _F32), pltpu.VMEM((ROUTER_SPLIT_ROWS, d), _BF16)],
        compiler_params=_params(1),
        name="pool_mixer_router",
    )(x2, x2, meta_pad, meta_tokens, row(a_norm[0]), a_w[0].astype(_BF16), row(a_scale[0]), row(ffn_norm[0]),
      wrt0, br0, pm, ph, triu)

    h2 = _moe(h1, xn1, ri1, rf1, cnt1, w_gate, w_up, w_down, 0, n_tiles, n_real_tiles, n_tiles, None)

    pos = jnp.concatenate([jnp.arange(seq, dtype=_F32) + N_META, jnp.arange(tm, dtype=_F32)])
    inv_freq = ROPE_THETA ** (-jnp.arange(0, QK_ROPE_DIM, 2, dtype=_F32) / QK_ROPE_DIM)
    ang = pos[:, None] * inv_freq[None, :]
    cos_t = jnp.tile(jnp.cos(ang), (1, 4))
    sin_t = jnp.tile(jnp.concatenate([-jnp.sin(ang), jnp.sin(ang)], axis=1), (1, 2))

    wdkv = jnp.concatenate([w_dkv, w_dkv[:, KV_LORA_RANK:]], axis=1).astype(_BF16)
    wuk = w_uk.reshape(KV_LORA_RANK, N_HEADS * QK_NOPE_DIM).astype(_BF16)
    wuv = w_uv.reshape(KV_LORA_RANK, N_HEADS * V_HEAD_DIM).astype(_BF16)
    wuq = b_w_uq[0]
    q_rank = wuq.shape[0]
    wuq_rope = wuq[:, :, QK_NOPE_DIM:]
    wuq = jnp.concatenate([wuq[:, :, :QK_NOPE_DIM].reshape(q_rank, -1),
                           jnp.concatenate([wuq_rope, wuq_rope], axis=2).reshape(q_rank, -1)], axis=1).astype(_BF16)
    wdq = b_w_dq[0].astype(_BF16)
    pos_tile = lambda i: (jnp.where(i < n_real_tiles, i % tiles_per_batch, tiles_per_batch), 0)
    head_tile = lambda i: (0, i, 0)
    q, k, v = pl.pallas_call(
        _proj_kernel,
        grid=(n_tiles,),
        in_specs=[pl.BlockSpec((tm, d), lambda i: (i, 0)),
                  pl.BlockSpec((tm, 128), pos_tile), pl.BlockSpec((tm, 128), pos_tile),
                  _full((1, d)), _full(wdkv.shape), _full((1, KV_LORA_RANK)), _full(wuk.shape), _full(wuv.shape),
                  _full((1, d)), _full(wdq.shape), _full((1, q_rank)), _full(wuq.shape)],
        out_specs=[pl.BlockSpec((N_HEADS, tm, QK_DIM), head_tile),
                   pl.BlockSpec((N_HEADS, tm, QK_DIM), head_tile),
                   pl.BlockSpec((N_HEADS, 1, V_HEAD_DIM, tm), lambda i: (0, i, 0, 0))],
        out_shape=[jax.ShapeDtypeStruct((N_HEADS, n_tiles * tm, QK_DIM), _BF16),
                   jax.ShapeDtypeStruct((N_HEADS, n_tiles * tm, QK_DIM), _BF16),
                   jax.ShapeDtypeStruct((N_HEADS, n_tiles, V_HEAD_DIM, tm), _BF16)],
        compiler_params=_params(1),
        name="latent_qkv",
    )(h2, cos_t, sin_t, row(kv_norm), wdkv, row(kv_lat_norm), wuk, wuv,
      row(b_norm[0]), wdq, row(b_q_norm[0]), wuq)

    tq, tk, hp = ATTN_Q_TILE, ATTN_K_TILE, ATTN_HEADS_PER_STEP
    n_q = seq // tq
    meta_block = n_tok // N_META
    o = pl.pallas_call(
        functools.partial(_attn_kernel, tq=tq, tk=tk),
        grid=(bsz, N_HEADS // hp, n_q),
        in_specs=[pl.BlockSpec((hp, tq, QK_DIM), lambda b, hg, j: (hg, b * n_q + j, 0)),
                  pl.BlockSpec((hp, seq, QK_DIM), lambda b, hg, j: (hg, b, 0)),
                  pl.BlockSpec((hp, tiles_per_batch, V_HEAD_DIM, tm), lambda b, hg, j: (hg, b, 0, 0)),
                  pl.BlockSpec((hp, N_META, QK_DIM), lambda b, hg, j: (hg, meta_block, 0)),
                  pl.BlockSpec((hp, 1, V_HEAD_DIM, tm), lambda b, hg, j: (hg, n_real_tiles, 0, 0))],
        out_specs=pl.BlockSpec((tq, hp * V_HEAD_DIM), lambda b, hg, j: (b * n_q + j, hg)),
        out_shape=jax.ShapeDtypeStruct((n_tok, N_HEADS * V_HEAD_DIM), _BF16),
        scratch_shapes=[pltpu.VMEM((hp, tk, tq), _F32), pltpu.VMEM((hp, tk, tq), _F32),
                        pltpu.VMEM((hp, V_HEAD_DIM, tq), _F32)],
        compiler_params=_params(3),
        name="causal_attention",
    )(q, k, v, k, v)

    wrt1, br1 = _router_operands(router_g[1], router_g_bias[1], router_e[1], router_e_bias[1])
    route_shapes, route_specs = _route_out(n_real_tiles, tm)
    h3, xn2, ri2, rf2, cnt2 = pl.pallas_call(
        _oproj_kernel,
        grid=(n_real_tiles,),
        in_specs=[pl.BlockSpec((tm, N_HEADS * V_HEAD_DIM), lambda i: (i, 0)),
                  pl.BlockSpec((tm, d), lambda i: (i, 0)),
                  _full((N_HEADS * V_HEAD_DIM, d)), _full((1, d)),
                  _full((ROUTER_ROWS, d)), _full((ROUTER_ROWS, 1)), _full((tm, tm))],
        out_specs=[pl.BlockSpec((tm, d), lambda i: (i, 0)),
                   pl.BlockSpec((tm * parts, LANES), lambda i: (i, 0))] + route_specs,
        out_shape=[jax.ShapeDtypeStruct((n_tok, d), _F32),
                   jax.ShapeDtypeStruct((n_tok * parts, LANES), _F32)] + route_shapes,
        scratch_shapes=[pltpu.VMEM((N_EXPERTS, 1), _F32), pltpu.VMEM((ROUTER_SPLIT_ROWS, d), _BF16)],
        compiler_params=_params(1),
        name="attn_out_router",
    )(o, h2, b_w_o[0].astype(_BF16), row(ffn_norm[1]), wrt1, br1, triu)

    out = _moe(h3, xn2, ri2, rf2, cnt2, w_gate, w_up, w_down, 1,
               n_real_tiles, n_real_tiles, n_real_tiles, final_norm)
    return out.reshape(bsz, seq, d)
```

```python
import functools

import numpy as np
import jax
import jax.numpy as jnp
from jax import lax
from jax.experimental import pallas as pl
from jax.experimental.pallas import tpu as pltpu

N_META = 16
POOL_WINDOWS = (2, 4, 8, 16)
N_HEADS = 8
QK_NOPE_DIM = 128
QK_ROPE_DIM = 64
QK_DIM = QK_NOPE_DIM + QK_ROPE_DIM
V_HEAD_DIM = 128
KV_LORA_RANK = 256
ROPE_THETA = 10000.0
ATTN_SCALE = QK_DIM ** -0.5
Q_SCALE = ATTN_SCALE * 1.4426950408889634
N_EXPERT_GROUPS = 4
EXPERTS_PER_GROUP = 8
N_EXPERTS = N_EXPERT_GROUPS * EXPERTS_PER_GROUP
RMS_EPS = 1e-6
NEG_INF = -1e30

TOKEN_TILE = 256
EXPERT_ROWS = 256
ATTN_Q_TILE = 512
ATTN_K_TILE = ATTN_Q_TILE // 2
ATTN_HEADS_PER_STEP = 4
DMA_UNROLL = 8
DMA_THREADS = 2
ROUTER_ROWS = 8 + N_EXPERTS
ROUTER_LO_ROW = 48
ROUTER_SPLIT_ROWS = 2 * ROUTER_LO_ROW
VMEM_LIMIT_BYTES = 48 * 1024 * 1024

LANES = 128
_F32 = jnp.float32
_BF16 = jnp.bfloat16
_NT_DIMS = (((1,), (1,)), ((), ()))


def _params(n_grid_dims=1, **kw):
    return pltpu.CompilerParams(dimension_semantics=("arbitrary",) * n_grid_dims,
                                vmem_limit_bytes=VMEM_LIMIT_BYTES, **kw)


def _rms(x, g):
    ms = jnp.mean(x * x, axis=-1, keepdims=True)
    return x * lax.rsqrt(ms + RMS_EPS) * g


def _split_bf16(x):
    hi = x.astype(_BF16)
    lo = (x - hi.astype(_F32)).astype(_BF16)
    return hi, lo


def _dot(a, b):
    return jnp.dot(a, b, preferred_element_type=_F32)


def _store_token_tiles(ref, x):
    n, d = x.shape
    parts = d // LANES
    for s in range(parts):
        ref[pl.ds(s, n, stride=parts), :] = x[:, s * LANES:(s + 1) * LANES]


def _load_token_tiles(ref, first_token, n, parts):
    return jnp.concatenate([ref[pl.ds(first_token * parts + s, n, stride=parts), :] for s in range(parts)], axis=1)


_HIGH_HALF = 0xFFFF0000


def _pack_rows(x):
    n, d = x.shape
    words = []
    for s in range(d // (2 * LANES)):
        lo = x[:, 2 * s * LANES:(2 * s + 1) * LANES].astype(_BF16).astype(_F32)
        hi = x[:, (2 * s + 1) * LANES:(2 * s + 2) * LANES].astype(_BF16).astype(_F32)
        words.append(lax.shift_right_logical(lax.bitcast_convert_type(lo, jnp.uint32), jnp.uint32(16))
                     | (lax.bitcast_convert_type(hi, jnp.uint32) & jnp.uint32(_HIGH_HALF)))
    return jnp.concatenate(words, axis=1)


def _unpack_rows(w):
    cols = []
    for s in range(w.shape[1] // LANES):
        ws = w[:, s * LANES:(s + 1) * LANES]
        cols.append(lax.bitcast_convert_type(lax.shift_left(ws, jnp.uint32(16)), _F32))
        cols.append(lax.bitcast_convert_type(ws & jnp.uint32(_HIGH_HALF), _F32))
    return jnp.concatenate(cols, axis=1)


def _route_init(wrt_ref, wsplit_ref, base_ref):
    base_ref[...] = jnp.zeros(base_ref.shape, base_ref.dtype)
    w_hi, w_lo = _split_bf16(wrt_ref[...])
    wsplit_ref[...] = jnp.zeros(wsplit_ref.shape, wsplit_ref.dtype)
    wsplit_ref[0:ROUTER_ROWS, :] = w_hi
    wsplit_ref[ROUTER_LO_ROW:ROUTER_LO_ROW + ROUTER_ROWS, :] = w_lo


def _route(xn, valid, wsplit_ref, br_ref, triu_ref, base_ref, ri_ref, rf_ref, cnt_ref):
    tm = xn.shape[0]
    x_hi, x_lo = _split_bf16(xn)
    by_hi = lax.dot_general(wsplit_ref[...], x_hi, _NT_DIMS, preferred_element_type=_F32)
    by_lo = lax.dot_general(wsplit_ref[0:ROUTER_LO_ROW, :], x_lo, _NT_DIMS, preferred_element_type=_F32)
    logits = (by_hi[0:ROUTER_ROWS] + by_hi[ROUTER_LO_ROW:ROUTER_LO_ROW + ROUTER_ROWS]
              + by_lo[0:ROUTER_ROWS] + br_ref[...])
    lg = logits[0:N_EXPERT_GROUPS]
    eg = jnp.exp(lg - jnp.max(lg, axis=0, keepdims=True))
    pg = eg / jnp.sum(eg, axis=0, keepdims=True)
    w_g = jnp.max(pg, axis=0, keepdims=True)
    ig = lax.broadcasted_iota(jnp.int32, pg.shape, 0).astype(_F32)
    g_sel = jnp.min(jnp.where(pg == w_g, ig, float(N_EXPERT_GROUPS)), axis=0, keepdims=True)

    sel = logits[8:8 + EXPERTS_PER_GROUP]
    for g in range(1, N_EXPERT_GROUPS):
        sel = jnp.where(g_sel == float(g), logits[8 + g * EXPERTS_PER_GROUP:8 + (g + 1) * EXPERTS_PER_GROUP], sel)
    ie = lax.broadcasted_iota(jnp.int32, sel.shape, 0).astype(_F32)
    v1 = jnp.max(sel, axis=0, keepdims=True)
    i1 = jnp.min(jnp.where(sel == v1, ie, float(EXPERTS_PER_GROUP)), axis=0, keepdims=True)
    rest = jnp.where(ie == i1, -jnp.inf, sel)
    v2 = jnp.max(rest, axis=0, keepdims=True)
    i2 = jnp.min(jnp.where(rest == v2, ie, float(EXPERTS_PER_GROUP)), axis=0, keepdims=True)
    e2 = jnp.exp(v2 - v1)
    den = 1.0 + e2
    validf = valid.astype(_F32)
    gate0 = w_g * (1.0 / den) * validf
    gate1 = w_g * (e2 / den) * validf
    eid0 = g_sel * float(EXPERTS_PER_GROUP) + i1
    eid1 = g_sel * float(EXPERTS_PER_GROUP) + i2

    iall = lax.broadcasted_iota(jnp.int32, (N_EXPERTS, tm), 0).astype(_F32)
    oh0 = jnp.where(iall == eid0, validf, 0.0)
    oh1 = jnp.where(iall == eid1, validf, 0.0)
    both = oh0 + oh1
    before = _dot(both.astype(_BF16), triu_ref[...]) + base_ref[...]
    rank0 = jnp.sum(oh0 * before, axis=0, keepdims=True)
    rank1 = jnp.sum(oh1 * before, axis=0, keepdims=True)
    base_ref[...] = base_ref[...] + jnp.sum(both, axis=1, keepdims=True)

    ri_ref[...] = jnp.zeros(ri_ref.shape, ri_ref.dtype)
    rf_ref[...] = jnp.zeros(rf_ref.shape, rf_ref.dtype)
    ri_ref[0, 0:1, :] = eid0.astype(jnp.int32)
    ri_ref[0, 1:2, :] = eid1.astype(jnp.int32)
    ri_ref[0, 2:3, :] = rank0.astype(jnp.int32)
    ri_ref[0, 3:4, :] = rank1.astype(jnp.int32)
    rf_ref[0, 0:1, :] = gate0
    rf_ref[0, 1:2, :] = gate1
    cnt_ref[...] = jnp.broadcast_to(base_ref[...], cnt_ref.shape).astype(jnp.int32)


def _mixer_kernel(x_ref, xh_ref, mp_ref, meta_ref, anorm_ref, aw_ref, ascale_ref, fnorm_ref,
                  wrt_ref, br_ref, pm_ref, ph_ref, triu_ref,
                  h_ref, xn_ref, ri_ref, rf_ref, cnt_ref, base_ref, wsplit_ref, *, n_real_tiles, tiles_per_batch):
    i = pl.program_id(0)
    tm = x_ref.shape[0]
    gd = x_ref.shape[1] // len(POOL_WINDOWS)
    is_meta = i == n_real_tiles
    first = (i % tiles_per_batch) == 0

    @pl.when(i == 0)
    def _():
        _route_init(wrt_ref, wsplit_ref, base_ref)

    h = jnp.where(is_meta, mp_ref[...], x_ref[...])
    halo = jnp.where(is_meta, 0.0, jnp.where(first, meta_ref[...], xh_ref[...]))
    hn = _rms(h, anorm_ref[...])
    hh = _rms(halo, anorm_ref[...])
    hn_hi, hn_lo = _split_bf16(hn)
    hh_hi, hh_lo = _split_bf16(hh)
    row = lax.broadcasted_iota(jnp.int32, (tm, 1), 0)
    for g, w in enumerate(POOL_WINDOWS):
        sl = slice(g * gd, (g + 1) * gd)
        win = _dot(pm_ref[g], hn_hi[:, sl]) + _dot(pm_ref[g], hn_lo[:, sl])
        top = win[:N_META] + _dot(ph_ref[g], hh_hi[:, sl]) + _dot(ph_ref[g], hh_lo[:, sl])
        win = jnp.concatenate([top, win[N_META:]], axis=0)
        cnt = jnp.where(is_meta, jnp.minimum(row + 1, w), w).astype(_F32)
        pooled = win * (1.0 / cnt) - hn[:, sl]
        mix = _dot(pooled.astype(_BF16), aw_ref[g])
        h_ref[:, sl] = h[:, sl] + mix * ascale_ref[:, sl]

    xn = _rms(h_ref[...], fnorm_ref[...])
    _store_token_tiles(xn_ref, _pack_rows(xn))
    lane = lax.broadcasted_iota(jnp.int32, (1, tm), 1)
    valid = jnp.logical_or(jnp.logical_not(is_meta), lane < N_META)
    _route(xn, valid, wsplit_ref, br_ref, triu_ref, base_ref, ri_ref, rf_ref, cnt_ref)


def _dispatch_kernel(dest_hbm, xn_hbm, xs_hbm, idx_smem, xbuf, isem, lsem, ssem, *, tm, n_tiles, n_real_tiles):
    i = pl.program_id(0)
    parts = xbuf.shape[1] // tm
    buf = i % 3

    def idx_copy(tile):
        return pltpu.make_async_copy(dest_hbm.at[tile], idx_smem.at[pl.ds(tile % 2, 1)], isem.at[tile % 2])

    def load(tile):
        rows = tm * parts
        return pltpu.make_async_copy(xn_hbm.at[pl.ds(pl.multiple_of(tile * rows, rows), rows), :],
                                     xbuf.at[tile % 3], lsem.at[tile % 3])

    def wait_scatters(tile):
        def wait_rows(n_tok):
            for k in range(2):
                pltpu.make_async_copy(xbuf.at[tile % 3, pl.ds(0, n_tok * parts), :],
                                      xs_hbm.at[pl.ds(0, n_tok * parts), :], ssem.at[tile % 3]).wait()

        @pl.when(tile < n_real_tiles)
        def _():
            wait_rows(tm)

        @pl.when(tile >= n_real_tiles)
        def _():
            wait_rows(N_META)

    def scatter(n_tok):
        def issue(g, c):
            for u in range(min(DMA_UNROLL, n_tok)):
                t = g * min(DMA_UNROLL, n_tok) + u
                src = xbuf.at[buf, pl.ds(pl.multiple_of(t * parts, parts), parts), :]
                for k in range(2):
                    d = idx_smem[i % 2, k * tm + t]
                    pltpu.make_async_copy(src, xs_hbm.at[pl.ds(pl.multiple_of(d * parts, parts), parts), :],
                                          ssem.at[buf]).start(priority=(2 * u + k) % DMA_THREADS)
            return c

        lax.fori_loop(0, n_tok // min(DMA_UNROLL, n_tok), issue, 0)

    @pl.when(i == 0)
    def _():
        idx_copy(0).start()
        load(0).start()

    @pl.when(i >= 2)
    def _():
        wait_scatters(i - 2)

    @pl.when(i + 1 < n_tiles)
    def _():
        idx_copy(i + 1).start()
        load(i + 1).start()

    idx_copy(i).wait()
    load(i).wait()

    @pl.when(i < n_real_tiles)
    def _():
        scatter(tm)

    @pl.when(i >= n_real_tiles)
    def _():
        scatter(N_META)

    @pl.when(i == n_tiles - 1)
    def _():
        if n_tiles >= 2:
            wait_scatters(i - 1)
        wait_scatters(i)


def _expert_kernel(be_ref, bv_ref, nu_ref, xs_ref, wg_ref, wu_ref, wd_ref, y_ref, wgb_ref, wub_ref, wdb_ref):
    i = pl.program_id(0)
    active = i < nu_ref[0]
    new_expert = jnp.logical_or(i == 0, be_ref[i] != be_ref[jnp.maximum(i - 1, 0)])

    @pl.when(jnp.logical_and(active, new_expert))
    def _():
        wgb_ref[...] = wg_ref[0, 0].astype(_BF16)
        wub_ref[...] = wu_ref[0, 0].astype(_BF16)
        wdb_ref[...] = wd_ref[0, 0].astype(_BF16)

    @pl.when(active)
    def _():
        parts = wgb_ref.shape[0] // (2 * LANES)
        tr = xs_ref.shape[0] // parts
        rows = lax.broadcasted_iota(jnp.int32, (tr, 1), 0)
        x = jnp.where(rows < bv_ref[i], _unpack_rows(_load_token_tiles(xs_ref, 0, tr, parts)), 0.0).astype(_BF16)
        g = _dot(x, wgb_ref[...])
        u = _dot(x, wub_ref[...])
        a = g * (1.0 / (1.0 + jnp.exp(-g))) * u
        _store_token_tiles(y_ref, _pack_rows(_dot(a.astype(_BF16), wdb_ref[...])))

    @pl.when(i >= nu_ref[0])
    def _():
        y_ref[...] = jnp.zeros(y_ref.shape, y_ref.dtype)


def _combine_kernel(dest_hbm, rf_ref, h_ref, y_hbm, eye_ref, fnorm_ref, out_ref,
                    idx_smem, ybuf, isem, rsem, *, tm, n_tiles, final_norm):
    i = pl.program_id(0)
    parts = ybuf.shape[1] // (2 * tm)
    slot = i % 2

    def idx_copy(tile):
        return pltpu.make_async_copy(dest_hbm.at[tile], idx_smem.at[pl.ds(tile % 2, 1)], isem.at[tile % 2])

    def gather(tile):
        def issue(g, c):
            for u in range(DMA_UNROLL):
                t = g * DMA_UNROLL + u
                for k in range(2):
                    d = idx_smem[tile % 2, k * tm + t]
                    pltpu.make_async_copy(
                        y_hbm.at[pl.ds(pl.multiple_of(d * parts, parts), parts), :],
                        ybuf.at[tile % 2, pl.ds(pl.multiple_of((k * tm + t) * parts, parts), parts), :],
                        rsem.at[tile % 2]).start(priority=(2 * u + k) % DMA_THREADS)
            return c

        lax.fori_loop(0, tm // DMA_UNROLL, issue, 0)

    @pl.when(i == 0)
    def _():
        idx_copy(0).start()
        idx_copy(0).wait()
        gather(0)
        if n_tiles >= 2:
            idx_copy(1).start()

    @pl.when(i + 1 < n_tiles)
    def _():
        idx_copy(i + 1).wait()
        gather(i + 1)

    @pl.when(i + 2 < n_tiles)
    def _():
        idx_copy(i + 2).start()

    gt = lax.dot_general(eye_ref[...], rf_ref[0], _NT_DIMS, precision=lax.Precision.HIGHEST,
                         preferred_element_type=_F32)
    pltpu.make_async_copy(y_hbm.at[pl.ds(0, 2 * tm * parts), :], ybuf.at[slot], rsem.at[slot]).wait()
    yb = ybuf.at[slot]
    out = h_ref[...] + (_unpack_rows(_load_token_tiles(yb, 0, tm, parts)) * gt[:, 0:1]
                        + _unpack_rows(_load_token_tiles(yb, tm, tm, parts)) * gt[:, 1:2])
    if final_norm:
        out = _rms(out, fnorm_ref[...])
    out_ref[...] = out


def _rope128(x, cos_t, sin_t):
    lane = lax.broadcasted_iota(jnp.int32, (1, 128), 1)
    first_half = (lane % QK_ROPE_DIM) < (QK_ROPE_DIM // 2)
    swapped = jnp.where(first_half, pltpu.roll(x, 128 - QK_ROPE_DIM // 2, axis=1),
                        pltpu.roll(x, QK_ROPE_DIM // 2, axis=1))
    return x * cos_t + swapped * sin_t


def _proj_kernel(h_ref, cos_ref, sin_ref, kvn_ref, wdkv_ref, kvlat_ref, wuk_ref, wuv_ref,
                 bnorm_ref, wdq_ref, qnorm_ref, wuq_ref, q_ref, k_ref, v_ref):
    h = h_ref[...]
    cos_t = cos_ref[...]
    sin_t = sin_ref[...]
    c = _dot(_rms(h, kvn_ref[...]).astype(_BF16), wdkv_ref[...])
    ckv = _rms(c[:, :KV_LORA_RANK], kvlat_ref[...]).astype(_BF16)
    kr = _rope128(c[:, KV_LORA_RANK:KV_LORA_RANK + 128], cos_t, sin_t)[:, :QK_ROPE_DIM].astype(_BF16)
    kn = _dot(ckv, wuk_ref[...])
    vv = _dot(ckv, wuv_ref[...])
    cq = _rms(_dot(_rms(h, bnorm_ref[...]).astype(_BF16), wdq_ref[...]), qnorm_ref[...]).astype(_BF16)
    q = _dot(cq, wuq_ref[...])
    rope0 = N_HEADS * QK_NOPE_DIM
    for hd in range(N_HEADS):
        k_ref[hd, :, 0:QK_NOPE_DIM] = kn[:, hd * QK_NOPE_DIM:(hd + 1) * QK_NOPE_DIM].astype(_BF16)
        k_ref[hd, :, QK_NOPE_DIM:QK_DIM] = kr
        v_ref[hd, 0] = vv[:, hd * V_HEAD_DIM:(hd + 1) * V_HEAD_DIM].T.astype(_BF16)
        q_ref[hd, :, 0:QK_NOPE_DIM] = (q[:, hd * QK_NOPE_DIM:(hd + 1) * QK_NOPE_DIM] * Q_SCALE).astype(_BF16)
        qr = _rope128(q[:, rope0 + hd * 128:rope0 + (hd + 1) * 128], cos_t, sin_t)[:, :QK_ROPE_DIM]
        q_ref[hd, :, QK_NOPE_DIM:QK_DIM] = (qr * Q_SCALE).astype(_BF16)


def _attn_kernel(q_ref, k_ref, vt_ref, km_ref, vmt_ref, o_ref, sa_ref, sb_ref, acc_ref, *, tq, tk):
    j = pl.program_id(2)
    heads = range(q_ref.shape[0])
    vd = acc_ref.shape[1]

    def scores(kb, s_ref):
        start = pl.multiple_of(kb * tk, tk)
        for hd in heads:
            s_ref[hd] = lax.dot_general(k_ref[hd, pl.ds(start, tk), :], q_ref[hd], _NT_DIMS,
                                        preferred_element_type=_F32)

    def update(kb, s_ref, carry, masked):
        out = []
        for hd in heads:
            m, l = carry[hd]
            s = s_ref[hd]
            if masked:
                keys = kb * tk + lax.broadcasted_iota(jnp.int32, (tk, tq), 0)
                qpos = j * tq + lax.broadcasted_iota(jnp.int32, (tk, tq), 1)
                s = jnp.where(keys <= qpos, s, NEG_INF)
            m_new = jnp.maximum(m, jnp.max(s, axis=0, keepdims=True))
            alpha = jnp.exp2(m - m_new)
            p = jnp.exp2(s - m_new)
            l = alpha * l + jnp.sum(p, axis=0, keepdims=True)
            acc_ref[hd] = acc_ref[hd] * alpha + _dot(vt_ref[hd, kb], p.astype(_BF16))
            out.append((m_new, l))
        return tuple(out)

    scores(0, sa_ref)
    carry = []
    for hd in heads:
        s0 = lax.dot_general(km_ref[hd], q_ref[hd], _NT_DIMS, preferred_element_type=_F32)
        m = jnp.max(s0, axis=0, keepdims=True)
        p0 = jnp.exp2(s0 - m)
        carry.append((m, jnp.sum(p0, axis=0, keepdims=True)))
        acc_ref[hd] = _dot(vmt_ref[hd, 0, :, 0:N_META], p0.astype(_BF16))
    carry = tuple(carry)

    def pair(kp, carry):
        scores(2 * kp + 1, sb_ref)
        carry = update(2 * kp, sa_ref, carry, False)
        scores(2 * kp + 2, sa_ref)
        return update(2 * kp + 1, sb_ref, carry, False)

    carry = lax.fori_loop(0, j, pair, carry)
    scores(2 * j + 1, sb_ref)
    carry = update(2 * j, sa_ref, carry, True)
    carry = update(2 * j + 1, sb_ref, carry, True)
    for hd in heads:
        o_ref[:, hd * vd:(hd + 1) * vd] = (acc_ref[hd] / carry[hd][1]).T.astype(o_ref.dtype)


def _oproj_kernel(o_ref, h_ref, wo_ref, fnorm_ref, wrt_ref, br_ref, triu_ref,
                  h_out_ref, xn_ref, ri_ref, rf_ref, cnt_ref, base_ref, wsplit_ref):
    i = pl.program_id(0)

    @pl.when(i == 0)
    def _():
        _route_init(wrt_ref, wsplit_ref, base_ref)

    h = h_ref[...] + _dot(o_ref[...], wo_ref[...])
    h_out_ref[...] = h
    xn = _rms(h, fnorm_ref[...])
    _store_token_tiles(xn_ref, _pack_rows(xn))
    valid = lax.broadcasted_iota(jnp.int32, (1, h.shape[0]), 1) >= 0
    _route(xn, valid, wsplit_ref, br_ref, triu_ref, base_ref, ri_ref, rf_ref, cnt_ref)


def _full(shape):
    nd = len(shape)
    return pl.BlockSpec(shape, lambda *_: (0,) * nd)


def _router_operands(router_g, router_g_bias, router_e, router_e_bias):
    d = router_g.shape[0]
    wrt = jnp.concatenate([router_g.T, jnp.zeros((8 - N_EXPERT_GROUPS, d), _F32), router_e.T], axis=0)
    br = jnp.concatenate([router_g_bias, jnp.zeros((8 - N_EXPERT_GROUPS,), _F32), router_e_bias])[:, None]
    return wrt.astype(_F32), br.astype(_F32)


def _route_out(n_tiles, tm):
    shapes = [jax.ShapeDtypeStruct((n_tiles, 8, tm), jnp.int32),
              jax.ShapeDtypeStruct((n_tiles, 8, tm), _F32),
              jax.ShapeDtypeStruct((N_EXPERTS, 128), jnp.int32)]
    specs = [pl.BlockSpec((1, 8, tm), lambda i: (i, 0, 0)),
             pl.BlockSpec((1, 8, tm), lambda i: (i, 0, 0)),
             pl.BlockSpec((N_EXPERTS, 128), lambda i: (0, 0))]
    return shapes, specs


def _moe(h, xn, ri, rf, counts, w_gate, w_up, w_down, layer, n_tiles, n_real_tiles, out_tiles, final_norm_w):
    tm, tr = TOKEN_TILE, EXPERT_ROWS
    d = h.shape[1]
    parts = d // (2 * LANES)
    n_valid = n_real_tiles * tm + (n_tiles - n_real_tiles) * N_META
    n_blocks = -(-(2 * n_valid + N_EXPERTS * (tr - 1)) // tr)
    n_rows = n_blocks * tr

    counts = counts[:, 0]
    padded = (counts + tr - 1) // tr * tr
    pends = jnp.cumsum(padded)
    pstarts = pends - padded
    n_used = (pends[-1] // tr).astype(jnp.int32).reshape(1)
    blk0 = jnp.arange(n_blocks, dtype=jnp.int32) * tr
    block_e = jnp.minimum(jnp.sum(blk0[:, None] >= pends[None, :], axis=1), N_EXPERTS - 1).astype(jnp.int32)
    experts = jnp.arange(N_EXPERTS, dtype=jnp.int32)
    block_oh = block_e[:, None] == experts[None, :]
    block_cnt = jnp.sum(jnp.where(block_oh, counts[None, :], 0), axis=1)
    block_start = jnp.sum(jnp.where(block_oh, pstarts[None, :], 0), axis=1)
    block_valid = jnp.clip(block_cnt - (blk0 - block_start), 0, tr).astype(jnp.int32)
    eid = ri[:, 0:2, :]
    slot0 = jnp.sum(jnp.where(eid[..., None] == experts, pstarts, 0), axis=-1)
    dest = (slot0 + ri[:, 2:4, :]).astype(jnp.int32).reshape(n_tiles, 1, 2 * tm)

    xs = pl.pallas_call(
        functools.partial(_dispatch_kernel, tm=tm, n_tiles=n_tiles, n_real_tiles=n_real_tiles),
        grid=(n_tiles,),
        in_specs=[pl.BlockSpec(memory_space=pl.ANY), pl.BlockSpec(memory_space=pl.ANY)],
        out_specs=pl.BlockSpec(memory_space=pl.ANY),
        out_shape=jax.ShapeDtypeStruct((n_rows * parts, LANES), jnp.uint32),
        scratch_shapes=[pltpu.SMEM((2, 2 * tm), jnp.int32), pltpu.VMEM((3, tm * parts, LANES), jnp.uint32),
                        pltpu.SemaphoreType.DMA((2,)), pltpu.SemaphoreType.DMA((3,)),
                        pltpu.SemaphoreType.DMA((3,))],
        compiler_params=_params(1, has_side_effects=True, disable_bounds_checks=True),
        name="moe_dispatch",
    )(dest, xn)

    f = w_gate.shape[3]
    expert_block = lambda i, be, bv, nu: (layer, be[last(i, be, bv, nu)], 0, 0)
    last = lambda i, be, bv, nu: jnp.minimum(i, nu[0] - 1)
    y = pl.pallas_call(
        _expert_kernel,
        grid_spec=pltpu.PrefetchScalarGridSpec(
            num_scalar_prefetch=3,
            grid=(n_blocks,),
            in_specs=[pl.BlockSpec((tr * parts, LANES), lambda i, be, bv, nu: (last(i, be, bv, nu), 0)),
                      pl.BlockSpec((1, 1, d, f), expert_block),
                      pl.BlockSpec((1, 1, d, f), expert_block),
                      pl.BlockSpec((1, 1, f, d), expert_block)],
            out_specs=pl.BlockSpec((tr * parts, LANES), lambda i, be, bv, nu: (i, 0)),
            scratch_shapes=[pltpu.VMEM((d, f), _BF16), pltpu.VMEM((d, f), _BF16), pltpu.VMEM((f, d), _BF16)]),
        out_shape=jax.ShapeDtypeStruct((n_rows * parts, LANES), jnp.uint32),
        compiler_params=_params(1),
        name="moe_experts",
    )(block_e, block_valid, n_used, xs, w_gate, w_up, w_down)

    eye = jnp.asarray(np.eye(tm, dtype=np.float32))
    fin = final_norm_w is not None
    fnorm = (final_norm_w if fin else jnp.ones((d,), _F32)).reshape(1, d)
    out = pl.pallas_call(
        functools.partial(_combine_kernel, tm=tm, n_tiles=out_tiles, final_norm=fin),
        grid=(out_tiles,),
        in_specs=[pl.BlockSpec(memory_space=pl.ANY),
                  pl.BlockSpec((1, 8, tm), lambda i: (i, 0, 0)),
                  pl.BlockSpec((tm, d), lambda i: (i, 0)),
                  pl.BlockSpec(memory_space=pl.ANY),
                  _full((tm, tm)), _full((1, d))],
        out_specs=pl.BlockSpec((tm, d), lambda i: (i, 0)),
        out_shape=jax.ShapeDtypeStruct((out_tiles * tm, d), _F32),
        scratch_shapes=[pltpu.SMEM((2, 2 * tm), jnp.int32), pltpu.VMEM((2, 2 * tm * parts, LANES), jnp.uint32),
                        pltpu.SemaphoreType.DMA((2,)), pltpu.SemaphoreType.DMA((2,))],
        compiler_params=_params(1, disable_bounds_checks=True),
        name="moe_combine",
    )(dest, rf, h, y, eye, fnorm)
    return out


def kernel(x, meta_tokens, a_norm, a_w, a_scale, b_norm, b_w_dq, b_q_norm, b_w_uq, b_w_o, kv_norm, w_dkv,
           kv_lat_norm, w_uk, w_uv, ffn_norm, router_g, router_g_bias, router_e, router_e_bias, w_gate, w_up,
           w_down, final_norm):
    bsz, seq, d = x.shape
    tm = TOKEN_TILE
    assert seq % tm == 0 and seq % ATTN_Q_TILE == 0 and ATTN_Q_TILE == 2 * ATTN_K_TILE and ATTN_K_TILE == tm
    assert d % (LANES * len(POOL_WINDOWS)) == 0 and N_META == max(POOL_WINDOWS) and N_META <= tm
    parts = d // (2 * LANES)
    n_tok = bsz * seq
    n_real_tiles = n_tok // tm
    tiles_per_batch = seq // tm
    n_tiles = n_real_tiles + 1
    gd = d // len(POOL_WINDOWS)
    row = lambda v: v.reshape(1, -1).astype(_F32)

    x2 = x.reshape(n_tok, d)
    meta_pad = jnp.concatenate([meta_tokens, jnp.zeros((tm - N_META, d), x.dtype)], axis=0)

    r = np.arange(tm)[:, None]
    cidx = np.arange(tm)[None, :]
    pm = np.stack([((r - cidx >= 0) & (r - cidx < w)) for w in POOL_WINDOWS]).astype(np.float32)
    hc = np.arange(N_META)[None, :]
    ph = np.stack([(r[:N_META] + N_META - hc < w) for w in POOL_WINDOWS]).astype(np.float32)
    triu = (r < cidx).astype(np.float32)
    pm, ph, triu = (jnp.asarray(a, dtype=_BF16) for a in (pm, ph, triu))

    wrt0, br0 = _router_operands(router_g[0], router_g_bias[0], router_e[0], router_e_bias[0])
    route_shapes, route_specs = _route_out(n_tiles, tm)
    tile_or_last = lambda i: (jnp.minimum(i, n_real_tiles - 1), 0)
    halo_blocks = tm // N_META
    h1, xn1, ri1, rf1, cnt1 = pl.pallas_call(
        functools.partial(_mixer_kernel, n_real_tiles=n_real_tiles, tiles_per_batch=tiles_per_batch),
        grid=(n_tiles,),
        in_specs=[pl.BlockSpec((tm, d), tile_or_last),
                  pl.BlockSpec((N_META, d), lambda i: (jnp.clip(i * halo_blocks - 1, 0, n_tok // N_META - 1), 0)),
                  _full((tm, d)), _full((N_META, d)), _full((1, d)),
                  _full((len(POOL_WINDOWS), gd, gd)), _full((1, d)), _full((1, d)),
                  _full((ROUTER_ROWS, d)), _full((ROUTER_ROWS, 1)),
                  _full(pm.shape), _full(ph.shape), _full((tm, tm))],
        out_specs=[pl.BlockSpec((tm, d), lambda i: (i, 0)),
                   pl.BlockSpec((tm * parts, LANES), lambda i: (i, 0))] + route_specs,
        out_shape=[jax.ShapeDtypeStruct((n_tiles * tm, d), _F32),
                   jax.ShapeDtypeStruct((n_tiles * tm * parts, LANES), jnp.uint32)] + route_shapes,
        scratch_shapes=[pltpu.VMEM((N_EXPERTS, 1), _F32), pltpu.VMEM((ROUTER_SPLIT_ROWS, d), _BF16)],
        compiler_params=_params(1),
        name="pool_mixer_router",
    )(x2, x2, meta_pad, meta_tokens, row(a_norm[0]), a_w[0].astype(_BF16), row(a_scale[0]), row(ffn_norm[0]),
      wrt0, br0, pm, ph, triu)

    h2 = _moe(h1, xn1, ri1, rf1, cnt1, w_gate, w_up, w_down, 0, n_tiles, n_real_tiles, n_tiles, None)

    pos = jnp.concatenate([jnp.arange(seq, dtype=_F32) + N_META, jnp.arange(tm, dtype=_F32)])
    inv_freq = ROPE_THETA ** (-jnp.arange(0, QK_ROPE_DIM, 2, dtype=_F32) / QK_ROPE_DIM)
    ang = pos[:, None] * inv_freq[None, :]
    cos_t = jnp.tile(jnp.cos(ang), (1, 4))
    sin_t = jnp.tile(jnp.concatenate([-jnp.sin(ang), jnp.sin(ang)], axis=1), (1, 2))

    wdkv = jnp.concatenate([w_dkv, w_dkv[:, KV_LORA_RANK:]], axis=1).astype(_BF16)
    wuk = w_uk.reshape(KV_LORA_RANK, N_HEADS * QK_NOPE_DIM).astype(_BF16)
    wuv = w_uv.reshape(KV_LORA_RANK, N_HEADS * V_HEAD_DIM).astype(_BF16)
    wuq = b_w_uq[0]
    q_rank = wuq.shape[0]
    wuq_rope = wuq[:, :, QK_NOPE_DIM:]
    wuq = jnp.concatenate([wuq[:, :, :QK_NOPE_DIM].reshape(q_rank, -1),
                           jnp.concatenate([wuq_rope, wuq_rope], axis=2).reshape(q_rank, -1)], axis=1).astype(_BF16)
    wdq = b_w_dq[0].astype(_BF16)
    pos_tile = lambda i: (jnp.where(i < n_real_tiles, i % tiles_per_batch, tiles_per_batch), 0)
    head_tile = lambda i: (0, i, 0)
    q, k, v = pl.pallas_call(
        _proj_kernel,
        grid=(n_tiles,),
        in_specs=[pl.BlockSpec((tm, d), lambda i: (i, 0)),
                  pl.BlockSpec((tm, 128), pos_tile), pl.BlockSpec((tm, 128), pos_tile),
                  _full((1, d)), _full(wdkv.shape), _full((1, KV_LORA_RANK)), _full(wuk.shape), _full(wuv.shape),
                  _full((1, d)), _full(wdq.shape), _full((1, q_rank)), _full(wuq.shape)],
        out_specs=[pl.BlockSpec((N_HEADS, tm, QK_DIM), head_tile),
                   pl.BlockSpec((N_HEADS, tm, QK_DIM), head_tile),
                   pl.BlockSpec((N_HEADS, 1, V_HEAD_DIM, tm), lambda i: (0, i, 0, 0))],
        out_shape=[jax.ShapeDtypeStruct((N_HEADS, n_tiles * tm, QK_DIM), _BF16),
                   jax.ShapeDtypeStruct((N_HEADS, n_tiles * tm, QK_DIM), _BF16),
                   jax.ShapeDtypeStruct((N_HEADS, n_tiles, V_HEAD_DIM, tm), _BF16)],
        compiler_params=_params(1),
        name="latent_qkv",
    )(h2, cos_t, sin_t, row(kv_norm), wdkv, row(kv_lat_norm), wuk, wuv,
      row(b_norm[0]), wdq, row(b_q_norm[0]), wuq)

    tq, tk, hp = ATTN_Q_TILE, ATTN_K_TILE, ATTN_HEADS_PER_STEP
    n_q = seq // tq
    meta_block = n_tok // N_META
    o = pl.pallas_call(
        functools.partial(_attn_kernel, tq=tq, tk=tk),
        grid=(bsz, N_HEADS // hp, n_q),
        in_specs=[pl.BlockSpec((hp, tq, QK_DIM), lambda b, hg, j: (hg, b * n_q + j, 0)),
                  pl.BlockSpec((hp, seq, QK_DIM), lambda b, hg, j: (hg, b, 0)),
                  pl.BlockSpec((hp, tiles_per_batch, V_HEAD_DIM, tm), lambda b, hg, j: (hg, b, 0, 0)),
                  pl.BlockSpec((hp, N_META, QK_DIM), lambda b, hg, j: (hg, meta_block, 0)),
                  pl.BlockSpec((hp, 1, V_HEAD_DIM, tm), lambda b, hg, j: (hg, n_real_tiles, 0, 0))],
        out_specs=pl.BlockSpec((tq, hp * V_HEAD_DIM), lambda b, hg, j: (b * n_q + j, hg)),
        out_shape=jax.ShapeDtypeStruct((n_tok, N_HEADS * V_HEAD_DIM), _BF16),
        scratch_shapes=[pltpu.VMEM((hp, tk, tq), _F32), pltpu.VMEM((hp, tk, tq), _F32),
                        pltpu.VMEM((hp, V_HEAD_DIM, tq), _F32)],
        compiler_params=_params(3),
        name="causal_attention",
    )(q, k, v, k, v)

    wrt1, br1 = _router_operands(router_g[1], router_g_bias[1], router_e[1], router_e_bias[1])
    route_shapes, route_specs = _route_out(n_real_tiles, tm)
    h3, xn2, ri2, rf2, cnt2 = pl.pallas_call(
        _oproj_kernel,
        grid=(n_real_tiles,),
        in_specs=[pl.BlockSpec((tm, N_HEADS * V_HEAD_DIM), lambda i: (i, 0)),
                  pl.BlockSpec((tm, d), lambda i: (i, 0)),
                  _full((N_HEADS * V_HEAD_DIM, d)), _full((1, d)),
                  _full((ROUTER_ROWS, d)), _full((ROUTER_ROWS, 1)), _full((tm, tm))],
        out_specs=[pl.BlockSpec((tm, d), lambda i: (i, 0)),
                   pl.BlockSpec((tm * parts, LANES), lambda i: (i, 0))] + route_specs,
        out_shape=[jax.ShapeDtypeStruct((n_tok, d), _F32),
                   jax.ShapeDtypeStruct((n_tok * parts, LANES), jnp.uint32)] + route_shapes,
        scratch_shapes=[pltpu.VMEM((N_EXPERTS, 1), _F32), pltpu.VMEM((ROUTER_SPLIT_ROWS, d), _BF16)],
        compiler_params=_params(1),
        name="attn_out_router",
    )(o, h2, b_w_o[0].astype(_BF16), row(ffn_norm[1]), wrt1, br1, triu)

    out = _moe(h3, xn2, ri2, rf2, cnt2, w_gate, w_up, w_down, 1,
               n_real_tiles, n_real_tiles, n_real_tiles, final_norm)
    return out.reshape(bsz, seq, d)
```

```python
import functools

import numpy as np
import jax
import jax.numpy as jnp
from jax import lax
from jax.experimental import pallas as pl
from jax.experimental.pallas import tpu as pltpu

N_META = 16
POOL_WINDOWS = (2, 4, 8, 16)
N_HEADS = 8
QK_NOPE_DIM = 128
QK_ROPE_DIM = 64
QK_DIM = QK_NOPE_DIM + QK_ROPE_DIM
V_HEAD_DIM = 128
KV_LORA_RANK = 256
ROPE_THETA = 10000.0
ATTN_SCALE = QK_DIM ** -0.5
Q_SCALE = ATTN_SCALE * 1.4426950408889634
N_EXPERT_GROUPS = 4
EXPERTS_PER_GROUP = 8
N_EXPERTS = N_EXPERT_GROUPS * EXPERTS_PER_GROUP
RMS_EPS = 1e-6
NEG_INF = -1e30

TOKEN_TILE = 256
EXPERT_ROWS = 256
ATTN_Q_TILE = 512
ATTN_K_TILE = ATTN_Q_TILE // 2
ATTN_HEADS_PER_STEP = 4
DMA_UNROLL = 8
DMA_THREADS = 2
ROUTER_ROWS = 8 + N_EXPERTS
ROUTER_LO_ROW = 48
ROUTER_SPLIT_ROWS = 2 * ROUTER_LO_ROW
VMEM_LIMIT_BYTES = 48 * 1024 * 1024

LANES = 128
_F32 = jnp.float32
_BF16 = jnp.bfloat16
_NT_DIMS = (((1,), (1,)), ((), ()))


def _params(n_grid_dims=1, **kw):
    return pltpu.CompilerParams(dimension_semantics=("arbitrary",) * n_grid_dims,
                                vmem_limit_bytes=VMEM_LIMIT_BYTES, **kw)


def _rms(x, g):
    ms = jnp.mean(x * x, axis=-1, keepdims=True)
    return x * lax.rsqrt(ms + RMS_EPS) * g


def _split_bf16(x):
    hi = x.astype(_BF16)
    lo = (x - hi.astype(_F32)).astype(_BF16)
    return hi, lo


def _dot(a, b):
    return jnp.dot(a, b, preferred_element_type=_F32)


def _store_token_tiles(ref, x):
    n, d = x.shape
    parts = d // LANES
    for s in range(parts):
        ref[pl.ds(s, n, stride=parts), :] = x[:, s * LANES:(s + 1) * LANES]


def _load_token_tiles(ref, first_token, n, parts):
    return jnp.concatenate([ref[pl.ds(first_token * parts + s, n, stride=parts), :] for s in range(parts)], axis=1)


_HIGH_HALF = 0xFFFF0000


def _pack_rows(x):
    n, d = x.shape
    words = []
    for s in range(d // (2 * LANES)):
        lo = x[:, 2 * s * LANES:(2 * s + 1) * LANES].astype(_BF16).astype(_F32)
        hi = x[:, (2 * s + 1) * LANES:(2 * s + 2) * LANES].astype(_BF16).astype(_F32)
        words.append(lax.shift_right_logical(lax.bitcast_convert_type(lo, jnp.uint32), jnp.uint32(16))
                     | (lax.bitcast_convert_type(hi, jnp.uint32) & jnp.uint32(_HIGH_HALF)))
    return jnp.concatenate(words, axis=1)


def _unpack_rows(w):
    cols = []
    for s in range(w.shape[1] // LANES):
        ws = w[:, s * LANES:(s + 1) * LANES]
        cols.append(lax.bitcast_convert_type(lax.shift_left(ws, jnp.uint32(16)), _F32))
        cols.append(lax.bitcast_convert_type(ws & jnp.uint32(_HIGH_HALF), _F32))
    return jnp.concatenate(cols, axis=1)


def _route_init(wrt_ref, wsplit_ref, base_ref):
    base_ref[...] = jnp.zeros(base_ref.shape, base_ref.dtype)
    w_hi, w_lo = _split_bf16(wrt_ref[...])
    wsplit_ref[...] = jnp.zeros(wsplit_ref.shape, wsplit_ref.dtype)
    wsplit_ref[0:ROUTER_ROWS, :] = w_hi
    wsplit_ref[ROUTER_LO_ROW:ROUTER_LO_ROW + ROUTER_ROWS, :] = w_lo


def _route(xn, valid, wsplit_ref, br_ref, triu_ref, base_ref, ri_ref, rf_ref, cnt_ref):
    tm = xn.shape[0]
    x_hi, x_lo = _split_bf16(xn)
    by_hi = lax.dot_general(wsplit_ref[...], x_hi, _NT_DIMS, preferred_element_type=_F32)
    by_lo = lax.dot_general(wsplit_ref[0:ROUTER_LO_ROW, :], x_lo, _NT_DIMS, preferred_element_type=_F32)
    logits = (by_hi[0:ROUTER_ROWS] + by_hi[ROUTER_LO_ROW:ROUTER_LO_ROW + ROUTER_ROWS]
              + by_lo[0:ROUTER_ROWS] + br_ref[...])
    lg = logits[0:N_EXPERT_GROUPS]
    eg = jnp.exp(lg - jnp.max(lg, axis=0, keepdims=True))
    pg = eg / jnp.sum(eg, axis=0, keepdims=True)
    w_g = jnp.max(pg, axis=0, keepdims=True)
    ig = lax.broadcasted_iota(jnp.int32, pg.shape, 0).astype(_F32)
    g_sel = jnp.min(jnp.where(pg == w_g, ig, float(N_EXPERT_GROUPS)), axis=0, keepdims=True)

    sel = logits[8:8 + EXPERTS_PER_GROUP]
    for g in range(1, N_EXPERT_GROUPS):
        sel = jnp.where(g_sel == float(g), logits[8 + g * EXPERTS_PER_GROUP:8 + (g + 1) * EXPERTS_PER_GROUP], sel)
    ie = lax.broadcasted_iota(jnp.int32, sel.shape, 0).astype(_F32)
    v1 = jnp.max(sel, axis=0, keepdims=True)
    i1 = jnp.min(jnp.where(sel == v1, ie, float(EXPERTS_PER_GROUP)), axis=0, keepdims=True)
    rest = jnp.where(ie == i1, -jnp.inf, sel)
    v2 = jnp.max(rest, axis=0, keepdims=True)
    i2 = jnp.min(jnp.where(rest == v2, ie, float(EXPERTS_PER_GROUP)), axis=0, keepdims=True)
    e2 = jnp.exp(v2 - v1)
    den = 1.0 + e2
    validf = valid.astype(_F32)
    gate0 = w_g * (1.0 / den) * validf
    gate1 = w_g * (e2 / den) * validf
    eid0 = g_sel * float(EXPERTS_PER_GROUP) + i1
    eid1 = g_sel * float(EXPERTS_PER_GROUP) + i2

    iall = lax.broadcasted_iota(jnp.int32, (N_EXPERTS, tm), 0).astype(_F32)
    oh0 = jnp.where(iall == eid0, validf, 0.0)
    oh1 = jnp.where(iall == eid1, validf, 0.0)
    both = oh0 + oh1
    before = _dot(both.astype(_BF16), triu_ref[...]) + base_ref[...]
    rank0 = jnp.sum(oh0 * before, axis=0, keepdims=True)
    rank1 = jnp.sum(oh1 * before, axis=0, keepdims=True)
    base_ref[...] = base_ref[...] + jnp.sum(both, axis=1, keepdims=True)

    ri_ref[...] = jnp.zeros(ri_ref.shape, ri_ref.dtype)
    rf_ref[...] = jnp.zeros(rf_ref.shape, rf_ref.dtype)
    ri_ref[0, 0:1, :] = eid0.astype(jnp.int32)
    ri_ref[0, 1:2, :] = eid1.astype(jnp.int32)
    ri_ref[0, 2:3, :] = rank0.astype(jnp.int32)
    ri_ref[0, 3:4, :] = rank1.astype(jnp.int32)
    rf_ref[0, 0:1, :] = gate0
    rf_ref[0, 1:2, :] = gate1
    cnt_ref[...] = jnp.broadcast_to(base_ref[...], cnt_ref.shape).astype(jnp.int32)


def _mixer_kernel(x_ref, xh_ref, mp_ref, meta_ref, anorm_ref, aw_ref, ascale_ref, fnorm_ref,
                  wrt_ref, br_ref, pm_ref, ph_ref, triu_ref,
                  h_ref, xn_ref, ri_ref, rf_ref, cnt_ref, base_ref, wsplit_ref, *, n_real_tiles, tiles_per_batch):
    i = pl.program_id(0)
    tm = x_ref.shape[0]
    gd = x_ref.shape[1] // len(POOL_WINDOWS)
    is_meta = i == n_real_tiles
    first = (i % tiles_per_batch) == 0

    @pl.when(i == 0)
    def _():
        _route_init(wrt_ref, wsplit_ref, base_ref)

    h = jnp.where(is_meta, mp_ref[...], x_ref[...])
    halo = jnp.where(is_meta, 0.0, jnp.where(first, meta_ref[...], xh_ref[...]))
    hn = _rms(h, anorm_ref[...])
    hh = _rms(halo, anorm_ref[...])
    hn_hi, hn_lo = _split_bf16(hn)
    hh_hi, hh_lo = _split_bf16(hh)
    row = lax.broadcasted_iota(jnp.int32, (tm, 1), 0)
    for g, w in enumerate(POOL_WINDOWS):
        sl = slice(g * gd, (g + 1) * gd)
        win = _dot(pm_ref[g], hn_hi[:, sl]) + _dot(pm_ref[g], hn_lo[:, sl])
        top = win[:N_META] + _dot(ph_ref[g], hh_hi[:, sl]) + _dot(ph_ref[g], hh_lo[:, sl])
        win = jnp.concatenate([top, win[N_META:]], axis=0)
        cnt = jnp.where(is_meta, jnp.minimum(row + 1, w), w).astype(_F32)
        pooled = win * (1.0 / cnt) - hn[:, sl]
        mix = _dot(pooled.astype(_BF16), aw_ref[g])
        h_ref[:, sl] = h[:, sl] + mix * ascale_ref[:, sl]

    xn = _rms(h_ref[...], fnorm_ref[...])
    _store_token_tiles(xn_ref, _pack_rows(xn))
    lane = lax.broadcasted_iota(jnp.int32, (1, tm), 1)
    valid = jnp.logical_or(jnp.logical_not(is_meta), lane < N_META)
    _route(xn, valid, wsplit_ref, br_ref, triu_ref, base_ref, ri_ref, rf_ref, cnt_ref)


def _dispatch_kernel(dest_hbm, xn_hbm, xs_hbm, idx_smem, xbuf, isem, lsem, ssem, *, tm, n_tiles, n_real_tiles):
    i = pl.program_id(0)
    parts = xbuf.shape[1] // tm
    buf = i % 3

    def idx_copy(tile):
        return pltpu.make_async_copy(dest_hbm.at[tile], idx_smem.at[pl.ds(tile % 2, 1)], isem.at[tile % 2])

    def load(tile):
        rows = tm * parts
        return pltpu.make_async_copy(xn_hbm.at[pl.ds(pl.multiple_of(tile * rows, rows), rows), :],
                                     xbuf.at[tile % 3], lsem.at[tile % 3])

    def wait_scatters(tile):
        def wait_rows(n_tok):
            for k in range(2):
                pltpu.make_async_copy(xbuf.at[tile % 3, pl.ds(0, n_tok * parts), :],
                                      xs_hbm.at[pl.ds(0, n_tok * parts), :], ssem.at[tile % 3]).wait()

        @pl.when(tile < n_real_tiles)
        def _():
            wait_rows(tm)

        @pl.when(tile >= n_real_tiles)
        def _():
            wait_rows(N_META)

    def scatter(n_tok):
        def issue(g, c):
            for u in range(min(DMA_UNROLL, n_tok)):
                t = g * min(DMA_UNROLL, n_tok) + u
                src = xbuf.at[buf, pl.ds(pl.multiple_of(t * parts, parts), parts), :]
                for k in range(2):
                    d = idx_smem[i % 2, k * tm + t]
                    pltpu.make_async_copy(src, xs_hbm.at[pl.ds(pl.multiple_of(d * parts, parts), parts), :],
                                          ssem.at[buf]).start(priority=(2 * u + k) % DMA_THREADS)
            return c

        lax.fori_loop(0, n_tok // min(DMA_UNROLL, n_tok), issue, 0)

    @pl.when(i == 0)
    def _():
        idx_copy(0).start()
        load(0).start()

    @pl.when(i >= 2)
    def _():
        wait_scatters(i - 2)

    @pl.when(i + 1 < n_tiles)
    def _():
        idx_copy(i + 1).start()
        load(i + 1).start()

    idx_copy(i).wait()
    load(i).wait()

    @pl.when(i < n_real_tiles)
    def _():
        scatter(tm)

    @pl.when(i >= n_real_tiles)
    def _():
        scatter(N_META)

    @pl.when(i == n_tiles - 1)
    def _():
        if n_tiles >= 2:
            wait_scatters(i - 1)
        wait_scatters(i)


def _expert_kernel(be_ref, bv_ref, nu_ref, xs_ref, wg_ref, wu_ref, wd_ref, y_ref, wgb_ref, wub_ref, wdb_ref):
    i = pl.program_id(0)
    active = i < nu_ref[0]
    new_expert = jnp.logical_or(i == 0, be_ref[i] != be_ref[jnp.maximum(i - 1, 0)])

    @pl.when(jnp.logical_and(active, new_expert))
    def _():
        wgb_ref[...] = wg_ref[0, 0].astype(_BF16)
        wub_ref[...] = wu_ref[0, 0].astype(_BF16)
        wdb_ref[...] = wd_ref[0, 0].astype(_BF16)

    @pl.when(active)
    def _():
        parts = wgb_ref.shape[0] // (2 * LANES)
        tr = xs_ref.shape[0] // parts
        rows = lax.broadcasted_iota(jnp.int32, (tr, 1), 0)
        x = jnp.where(rows < bv_ref[i], _unpack_rows(_load_token_tiles(xs_ref, 0, tr, parts)), 0.0).astype(_BF16)
        g = _dot(x, wgb_ref[...])
        u = _dot(x, wub_ref[...])
        a = g * (1.0 / (1.0 + jnp.exp(-g))) * u
        _store_token_tiles(y_ref, _pack_rows(_dot(a.astype(_BF16), wdb_ref[...])))

    @pl.when(i >= nu_ref[0])
    def _():
        y_ref[...] = jnp.zeros(y_ref.shape, y_ref.dtype)


def _combined_tile(dest_hbm, rf_ref, h_ref, y_hbm, eye_ref, idx_smem, ybuf, isem, rsem, *, tm, n_tiles):
    i = pl.program_id(0)
    parts = ybuf.shape[1] // (2 * tm)
    slot = i % 2

    def idx_copy(tile):
        return pltpu.make_async_copy(dest_hbm.at[tile], idx_smem.at[pl.ds(tile % 2, 1)], isem.at[tile % 2])

    def gather(tile):
        def issue(g, c):
            for u in range(DMA_UNROLL):
                t = g * DMA_UNROLL + u
                for k in range(2):
                    d = idx_smem[tile % 2, k * tm + t]
                    pltpu.make_async_copy(
                        y_hbm.at[pl.ds(pl.multiple_of(d * parts, parts), parts), :],
                        ybuf.at[tile % 2, pl.ds(pl.multiple_of((k * tm + t) * parts, parts), parts), :],
                        rsem.at[tile % 2]).start(priority=(2 * u + k) % DMA_THREADS)
            return c

        lax.fori_loop(0, tm // DMA_UNROLL, issue, 0)

    @pl.when(i == 0)
    def _():
        idx_copy(0).start()
        idx_copy(0).wait()
        gather(0)
        if n_tiles >= 2:
            idx_copy(1).start()

    @pl.when(i + 1 < n_tiles)
    def _():
        idx_copy(i + 1).wait()
        gather(i + 1)

    @pl.when(i + 2 < n_tiles)
    def _():
        idx_copy(i + 2).start()

    gt = lax.dot_general(eye_ref[...], rf_ref[0], _NT_DIMS, precision=lax.Precision.HIGHEST,
                         preferred_element_type=_F32)
    pltpu.make_async_copy(y_hbm.at[pl.ds(0, 2 * tm * parts), :], ybuf.at[slot], rsem.at[slot]).wait()
    yb = ybuf.at[slot]
    return h_ref[...] + (_unpack_rows(_load_token_tiles(yb, 0, tm, parts)) * gt[:, 0:1]
                         + _unpack_rows(_load_token_tiles(yb, tm, tm, parts)) * gt[:, 1:2])


def _combine_norm_kernel(dest_hbm, rf_ref, h_ref, y_hbm, eye_ref, fnorm_ref, out_ref,
                         idx_smem, ybuf, isem, rsem, *, tm, n_tiles):
    h = _combined_tile(dest_hbm, rf_ref, h_ref, y_hbm, eye_ref, idx_smem, ybuf, isem, rsem, tm=tm, n_tiles=n_tiles)
    out_ref[...] = _rms(h, fnorm_ref[...])


def _combine_proj_kernel(dest_hbm, rf_ref, h_ref, y_hbm, eye_ref, cos_ref, sin_ref, kvn_ref, wdkv_ref, kvlat_ref,
                         wuk_ref, wuv_ref, bnorm_ref, wdq_ref, qnorm_ref, wuq_ref,
                         h_out_ref, q_ref, k_ref, v_ref, idx_smem, ybuf, isem, rsem, *, tm, n_tiles):
    h = _combined_tile(dest_hbm, rf_ref, h_ref, y_hbm, eye_ref, idx_smem, ybuf, isem, rsem, tm=tm, n_tiles=n_tiles)
    h_out_ref[...] = h
    _project(h, cos_ref, sin_ref, kvn_ref, wdkv_ref, kvlat_ref, wuk_ref, wuv_ref,
             bnorm_ref, wdq_ref, qnorm_ref, wuq_ref, q_ref, k_ref, v_ref)


def _rope128(x, cos_t, sin_t):
    lane = lax.broadcasted_iota(jnp.int32, (1, 128), 1)
    first_half = (lane % QK_ROPE_DIM) < (QK_ROPE_DIM // 2)
    swapped = jnp.where(first_half, pltpu.roll(x, 128 - QK_ROPE_DIM // 2, axis=1),
                        pltpu.roll(x, QK_ROPE_DIM // 2, axis=1))
    return x * cos_t + swapped * sin_t


def _project(h, cos_ref, sin_ref, kvn_ref, wdkv_ref, kvlat_ref, wuk_ref, wuv_ref,
             bnorm_ref, wdq_ref, qnorm_ref, wuq_ref, q_ref, k_ref, v_ref):
    cos_t = cos_ref[...]
    sin_t = sin_ref[...]
    c = _dot(_rms(h, kvn_ref[...]).astype(_BF16), wdkv_ref[...])
    ckv = _rms(c[:, :KV_LORA_RANK], kvlat_ref[...]).astype(_BF16)
    kr = _rope128(c[:, KV_LORA_RANK:KV_LORA_RANK + 128], cos_t, sin_t)[:, :QK_ROPE_DIM].astype(_BF16)
    kn = _dot(ckv, wuk_ref[...])
    vv = _dot(ckv, wuv_ref[...])
    cq = _rms(_dot(_rms(h, bnorm_ref[...]).astype(_BF16), wdq_ref[...]), qnorm_ref[...]).astype(_BF16)
    q = _dot(cq, wuq_ref[...])
    rope0 = N_HEADS * QK_NOPE_DIM
    for hd in range(N_HEADS):
        k_ref[hd, :, 0:QK_NOPE_DIM] = kn[:, hd * QK_NOPE_DIM:(hd + 1) * QK_NOPE_DIM].astype(_BF16)
        k_ref[hd, :, QK_NOPE_DIM:QK_DIM] = kr
        v_ref[hd, 0] = vv[:, hd * V_HEAD_DIM:(hd + 1) * V_HEAD_DIM].T.astype(_BF16)
        q_ref[hd, :, 0:QK_NOPE_DIM] = (q[:, hd * QK_NOPE_DIM:(hd + 1) * QK_NOPE_DIM] * Q_SCALE).astype(_BF16)
        qr = _rope128(q[:, rope0 + hd * 128:rope0 + (hd + 1) * 128], cos_t, sin_t)[:, :QK_ROPE_DIM]
        q_ref[hd, :, QK_NOPE_DIM:QK_DIM] = (qr * Q_SCALE).astype(_BF16)


def _attn_kernel(q_ref, k_ref, vt_ref, km_ref, vmt_ref, o_ref, sa_ref, sb_ref, acc_ref, *, tq, tk):
    j = pl.program_id(2)
    heads = range(q_ref.shape[0])
    vd = acc_ref.shape[1]

    def scores(kb, s_ref):
        start = pl.multiple_of(kb * tk, tk)
        for hd in heads:
            s_ref[hd] = lax.dot_general(k_ref[hd, pl.ds(start, tk), :], q_ref[hd], _NT_DIMS,
                                        preferred_element_type=_F32)

    def update(kb, s_ref, carry, masked):
        out = []
        for hd in heads:
            m, l = carry[hd]
            s = s_ref[hd]
            if masked:
                keys = kb * tk + lax.broadcasted_iota(jnp.int32, (tk, tq), 0)
                qpos = j * tq + lax.broadcasted_iota(jnp.int32, (tk, tq), 1)
                s = jnp.where(keys <= qpos, s, NEG_INF)
            m_new = jnp.maximum(m, jnp.max(s, axis=0, keepdims=True))
            alpha = jnp.exp2(m - m_new)
            p = jnp.exp2(s - m_new)
            l = alpha * l + jnp.sum(p, axis=0, keepdims=True)
            acc_ref[hd] = acc_ref[hd] * alpha + _dot(vt_ref[hd, kb], p.astype(_BF16))
            out.append((m_new, l))
        return tuple(out)

    scores(0, sa_ref)
    carry = []
    for hd in heads:
        s0 = lax.dot_general(km_ref[hd], q_ref[hd], _NT_DIMS, preferred_element_type=_F32)
        m = jnp.max(s0, axis=0, keepdims=True)
        p0 = jnp.exp2(s0 - m)
        carry.append((m, jnp.sum(p0, axis=0, keepdims=True)))
        acc_ref[hd] = _dot(vmt_ref[hd, 0, :, 0:N_META], p0.astype(_BF16))
    carry = tuple(carry)

    def pair(kp, carry):
        scores(2 * kp + 1, sb_ref)
        carry = update(2 * kp, sa_ref, carry, False)
        scores(2 * kp + 2, sa_ref)
        return update(2 * kp + 1, sb_ref, carry, False)

    carry = lax.fori_loop(0, j, pair, carry)
    scores(2 * j + 1, sb_ref)
    carry = update(2 * j, sa_ref, carry, True)
    carry = update(2 * j + 1, sb_ref, carry, True)
    for hd in heads:
        o_ref[:, hd * vd:(hd + 1) * vd] = (acc_ref[hd] / carry[hd][1]).T.astype(o_ref.dtype)


def _oproj_kernel(o_ref, h_ref, wo_ref, fnorm_ref, wrt_ref, br_ref, triu_ref,
                  h_out_ref, xn_ref, ri_ref, rf_ref, cnt_ref, base_ref, wsplit_ref):
    i = pl.program_id(0)

    @pl.when(i == 0)
    def _():
        _route_init(wrt_ref, wsplit_ref, base_ref)

    h = h_ref[...] + _dot(o_ref[...], wo_ref[...])
    h_out_ref[...] = h
    xn = _rms(h, fnorm_ref[...])
    _store_token_tiles(xn_ref, _pack_rows(xn))
    valid = lax.broadcasted_iota(jnp.int32, (1, h.shape[0]), 1) >= 0
    _route(xn, valid, wsplit_ref, br_ref, triu_ref, base_ref, ri_ref, rf_ref, cnt_ref)


def _full(shape):
    nd = len(shape)
    return pl.BlockSpec(shape, lambda *_: (0,) * nd)


def _router_operands(router_g, router_g_bias, router_e, router_e_bias):
    d = router_g.shape[0]
    wrt = jnp.concatenate([router_g.T, jnp.zeros((8 - N_EXPERT_GROUPS, d), _F32), router_e.T], axis=0)
    br = jnp.concatenate([router_g_bias, jnp.zeros((8 - N_EXPERT_GROUPS,), _F32), router_e_bias])[:, None]
    return wrt.astype(_F32), br.astype(_F32)


def _route_out(n_tiles, tm):
    shapes = [jax.ShapeDtypeStruct((n_tiles, 8, tm), jnp.int32),
              jax.ShapeDtypeStruct((n_tiles, 8, tm), _F32),
              jax.ShapeDtypeStruct((N_EXPERTS, 128), jnp.int32)]
    specs = [pl.BlockSpec((1, 8, tm), lambda i: (i, 0, 0)),
             pl.BlockSpec((1, 8, tm), lambda i: (i, 0, 0)),
             pl.BlockSpec((N_EXPERTS, 128), lambda i: (0, 0))]
    return shapes, specs


def _moe_experts(xn, ri, counts, w_gate, w_up, w_down, layer, n_tiles, n_real_tiles, d):
    tm, tr = TOKEN_TILE, EXPERT_ROWS
    parts = d // (2 * LANES)
    n_valid = n_real_tiles * tm + (n_tiles - n_real_tiles) * N_META
    n_blocks = -(-(2 * n_valid + N_EXPERTS * (tr - 1)) // tr)
    n_rows = n_blocks * tr

    counts = counts[:, 0]
    padded = (counts + tr - 1) // tr * tr
    pends = jnp.cumsum(padded)
    pstarts = pends - padded
    n_used = (pends[-1] // tr).astype(jnp.int32).reshape(1)
    blk0 = jnp.arange(n_blocks, dtype=jnp.int32) * tr
    block_e = jnp.minimum(jnp.sum(blk0[:, None] >= pends[None, :], axis=1), N_EXPERTS - 1).astype(jnp.int32)
    experts = jnp.arange(N_EXPERTS, dtype=jnp.int32)
    block_oh = block_e[:, None] == experts[None, :]
    block_cnt = jnp.sum(jnp.where(block_oh, counts[None, :], 0), axis=1)
    block_start = jnp.sum(jnp.where(block_oh, pstarts[None, :], 0), axis=1)
    block_valid = jnp.clip(block_cnt - (blk0 - block_start), 0, tr).astype(jnp.int32)
    eid = ri[:, 0:2, :]
    slot0 = jnp.sum(jnp.where(eid[..., None] == experts, pstarts, 0), axis=-1)
    dest = (slot0 + ri[:, 2:4, :]).astype(jnp.int32).reshape(n_tiles, 1, 2 * tm)

    xs = pl.pallas_call(
        functools.partial(_dispatch_kernel, tm=tm, n_tiles=n_tiles, n_real_tiles=n_real_tiles),
        grid=(n_tiles,),
        in_specs=[pl.BlockSpec(memory_space=pl.ANY), pl.BlockSpec(memory_space=pl.ANY)],
        out_specs=pl.BlockSpec(memory_space=pl.ANY),
        out_shape=jax.ShapeDtypeStruct((n_rows * parts, LANES), jnp.uint32),
        scratch_shapes=[pltpu.SMEM((2, 2 * tm), jnp.int32), pltpu.VMEM((3, tm * parts, LANES), jnp.uint32),
                        pltpu.SemaphoreType.DMA((2,)), pltpu.SemaphoreType.DMA((3,)),
                        pltpu.SemaphoreType.DMA((3,))],
        compiler_params=_params(1, has_side_effects=True, disable_bounds_checks=True),
        name="moe_dispatch",
    )(dest, xn)

    f = w_gate.shape[3]
    expert_block = lambda i, be, bv, nu: (layer, be[last(i, be, bv, nu)], 0, 0)
    last = lambda i, be, bv, nu: jnp.minimum(i, nu[0] - 1)
    y = pl.pallas_call(
        _expert_kernel,
        grid_spec=pltpu.PrefetchScalarGridSpec(
            num_scalar_prefetch=3,
            grid=(n_blocks,),
            in_specs=[pl.BlockSpec((tr * parts, LANES), lambda i, be, bv, nu: (last(i, be, bv, nu), 0)),
                      pl.BlockSpec((1, 1, d, f), expert_block),
                      pl.BlockSpec((1, 1, d, f), expert_block),
                      pl.BlockSpec((1, 1, f, d), expert_block)],
            out_specs=pl.BlockSpec((tr * parts, LANES), lambda i, be, bv, nu: (i, 0)),
            scratch_shapes=[pltpu.VMEM((d, f), _BF16), pltpu.VMEM((d, f), _BF16), pltpu.VMEM((f, d), _BF16)]),
        out_shape=jax.ShapeDtypeStruct((n_rows * parts, LANES), jnp.uint32),
        compiler_params=_params(1),
        name="moe_experts",
    )(block_e, block_valid, n_used, xs, w_gate, w_up, w_down)

    return dest, y


def _combine_operands(dest, rf, h, y, d):
    tm = TOKEN_TILE
    parts = d // (2 * LANES)
    eye = jnp.asarray(np.eye(tm, dtype=np.float32))
    in_specs = [pl.BlockSpec(memory_space=pl.ANY),
                pl.BlockSpec((1, 8, tm), lambda i: (i, 0, 0)),
                pl.BlockSpec((tm, d), lambda i: (i, 0)),
                pl.BlockSpec(memory_space=pl.ANY),
                _full((tm, tm))]
    scratch = [pltpu.SMEM((2, 2 * tm), jnp.int32), pltpu.VMEM((2, 2 * tm * parts, LANES), jnp.uint32),
               pltpu.SemaphoreType.DMA((2,)), pltpu.SemaphoreType.DMA((2,))]
    return (dest, rf, h, y, eye), in_specs, scratch


def kernel(x, meta_tokens, a_norm, a_w, a_scale, b_norm, b_w_dq, b_q_norm, b_w_uq, b_w_o, kv_norm, w_dkv,
           kv_lat_norm, w_uk, w_uv, ffn_norm, router_g, router_g_bias, router_e, router_e_bias, w_gate, w_up,
           w_down, final_norm):
    bsz, seq, d = x.shape
    tm = TOKEN_TILE
    assert seq % tm == 0 and seq % ATTN_Q_TILE == 0 and ATTN_Q_TILE == 2 * ATTN_K_TILE and ATTN_K_TILE == tm
    assert d % (LANES * len(POOL_WINDOWS)) == 0 and N_META == max(POOL_WINDOWS) and N_META <= tm
    parts = d // (2 * LANES)
    n_tok = bsz * seq
    n_real_tiles = n_tok // tm
    tiles_per_batch = seq // tm
    n_tiles = n_real_tiles + 1
    gd = d // len(POOL_WINDOWS)
    row = lambda v: v.reshape(1, -1).astype(_F32)

    x2 = x.reshape(n_tok, d)
    meta_pad = jnp.concatenate([meta_tokens, jnp.zeros((tm - N_META, d), x.dtype)], axis=0)

    r = np.arange(tm)[:, None]
    cidx = np.arange(tm)[None, :]
    pm = np.stack([((r - cidx >= 0) & (r - cidx < w)) for w in POOL_WINDOWS]).astype(np.float32)
    hc = np.arange(N_META)[None, :]
    ph = np.stack([(r[:N_META] + N_META - hc < w) for w in POOL_WINDOWS]).astype(np.float32)
    triu = (r < cidx).astype(np.float32)
    pm, ph, triu = (jnp.asarray(a, dtype=_BF16) for a in (pm, ph, triu))

    wrt0, br0 = _router_operands(router_g[0], router_g_bias[0], router_e[0], router_e_bias[0])
    route_shapes, route_specs = _route_out(n_tiles, tm)
    tile_or_last = lambda i: (jnp.minimum(i, n_real_tiles - 1), 0)
    halo_blocks = tm // N_META
    h1, xn1, ri1, rf1, cnt1 = pl.pallas_call(
        functools.partial(_mixer_kernel, n_real_tiles=n_real_tiles, tiles_per_batch=tiles_per_batch),
        grid=(n_tiles,),
        in_specs=[pl.BlockSpec((tm, d), tile_or_last),
                  pl.BlockSpec((N_META, d), lambda i: (jnp.clip(i * halo_blocks - 1, 0, n_tok // N_META - 1), 0)),
                  _full((tm, d)), _full((N_META, d)), _full((1, d)),
                  _full((len(POOL_WINDOWS), gd, gd)), _full((1, d)), _full((1, d)),
                  _full((ROUTER_ROWS, d)), _full((ROUTER_ROWS, 1)),
                  _full(pm.shape), _full(ph.shape), _full((tm, tm))],
        out_specs=[pl.BlockSpec((tm, d), lambda i: (i, 0)),
                   pl.BlockSpec((tm * parts, LANES), lambda i: (i, 0))] + route_specs,
        out_shape=[jax.ShapeDtypeStruct((n_tiles * tm, d), _F32),
                   jax.ShapeDtypeStruct((n_tiles * tm * parts, LANES), jnp.uint32)] + route_shapes,
        scratch_shapes=[pltpu.VMEM((N_EXPERTS, 1), _F32), pltpu.VMEM((ROUTER_SPLIT_ROWS, d), _BF16)],
        compiler_params=_params(1),
        name="pool_mixer_router",
    )(x2, x2, meta_pad, meta_tokens, row(a_norm[0]), a_w[0].astype(_BF16), row(a_scale[0]), row(ffn_norm[0]),
      wrt0, br0, pm, ph, triu)

    dest1, y1 = _moe_experts(xn1, ri1, cnt1, w_gate, w_up, w_down, 0, n_tiles, n_real_tiles, d)

    pos = jnp.concatenate([jnp.arange(seq, dtype=_F32) + N_META, jnp.arange(tm, dtype=_F32)])
    inv_freq = ROPE_THETA ** (-jnp.arange(0, QK_ROPE_DIM, 2, dtype=_F32) / QK_ROPE_DIM)
    ang = pos[:, None] * inv_freq[None, :]
    cos_t = jnp.tile(jnp.cos(ang), (1, 4))
    sin_t = jnp.tile(jnp.concatenate([-jnp.sin(ang), jnp.sin(ang)], axis=1), (1, 2))

    wdkv = jnp.concatenate([w_dkv, w_dkv[:, KV_LORA_RANK:]], axis=1).astype(_BF16)
    wuk = w_uk.reshape(KV_LORA_RANK, N_HEADS * QK_NOPE_DIM).astype(_BF16)
    wuv = w_uv.reshape(KV_LORA_RANK, N_HEADS * V_HEAD_DIM).astype(_BF16)
    wuq = b_w_uq[0]
    q_rank = wuq.shape[0]
    wuq_rope = wuq[:, :, QK_NOPE_DIM:]
    wuq = jnp.concatenate([wuq[:, :, :QK_NOPE_DIM].reshape(q_rank, -1),
                           jnp.concatenate([wuq_rope, wuq_rope], axis=2).reshape(q_rank, -1)], axis=1).astype(_BF16)
    wdq = b_w_dq[0].astype(_BF16)
    pos_tile = lambda i: (jnp.where(i < n_real_tiles, i % tiles_per_batch, tiles_per_batch), 0)
    head_tile = lambda i: (0, i, 0)
    operands, in_specs, scratch = _combine_operands(dest1, rf1, h1, y1, d)
    h2, q, k, v = pl.pallas_call(
        functools.partial(_combine_proj_kernel, tm=tm, n_tiles=n_tiles),
        grid=(n_tiles,),
        in_specs=in_specs + [
            pl.BlockSpec((tm, 128), pos_tile), pl.BlockSpec((tm, 128), pos_tile),
            _full((1, d)), _full(wdkv.shape), _full((1, KV_LORA_RANK)), _full(wuk.shape), _full(wuv.shape),
            _full((1, d)), _full(wdq.shape), _full((1, q_rank)), _full(wuq.shape)],
        out_specs=[pl.BlockSpec((tm, d), lambda i: (i, 0)),
                   pl.BlockSpec((N_HEADS, tm, QK_DIM), head_tile),
                   pl.BlockSpec((N_HEADS, tm, QK_DIM), head_tile),
                   pl.BlockSpec((N_HEADS, 1, V_HEAD_DIM, tm), lambda i: (0, i, 0, 0))],
        out_shape=[jax.ShapeDtypeStruct((n_tiles * tm, d), _F32),
                   jax.ShapeDtypeStruct((N_HEADS, n_tiles * tm, QK_DIM), _BF16),
                   jax.ShapeDtypeStruct((N_HEADS, n_tiles * tm, QK_DIM), _BF16),
                   jax.ShapeDtypeStruct((N_HEADS, n_tiles, V_HEAD_DIM, tm), _BF16)],
        scratch_shapes=scratch,
        compiler_params=_params(1, disable_bounds_checks=True),
        name="moe_combine_latent_qkv",
    )(*operands, cos_t, sin_t, row(kv_norm), wdkv, row(kv_lat_norm), wuk, wuv,
      row(b_norm[0]), wdq, row(b_q_norm[0]), wuq)

    tq, tk, hp = ATTN_Q_TILE, ATTN_K_TILE, ATTN_HEADS_PER_STEP
    n_q = seq // tq
    meta_block = n_tok // N_META
    o = pl.pallas_call(
        functools.partial(_attn_kernel, tq=tq, tk=tk),
        grid=(bsz, N_HEADS // hp, n_q),
        in_specs=[pl.BlockSpec((hp, tq, QK_DIM), lambda b, hg, j: (hg, b * n_q + j, 0)),
                  pl.BlockSpec((hp, seq, QK_DIM), lambda b, hg, j: (hg, b, 0)),
                  pl.BlockSpec((hp, tiles_per_batch, V_HEAD_DIM, tm), lambda b, hg, j: (hg, b, 0, 0)),
                  pl.BlockSpec((hp, N_META, QK_DIM), lambda b, hg, j: (hg, meta_block, 0)),
                  pl.BlockSpec((hp, 1, V_HEAD_DIM, tm), lambda b, hg, j: (hg, n_real_tiles, 0, 0))],
        out_specs=pl.BlockSpec((tq, hp * V_HEAD_DIM), lambda b, hg, j: (b * n_q + j, hg)),
        out_shape=jax.ShapeDtypeStruct((n_tok, N_HEADS * V_HEAD_DIM), _BF16),
        scratch_shapes=[pltpu.VMEM((hp, tk, tq), _F32), pltpu.VMEM((hp, tk, tq), _F32),
                        pltpu.VMEM((hp, V_HEAD_DIM, tq), _F32)],
        compiler_params=_params(3),
        name="causal_attention",
    )(q, k, v, k, v)

    wrt1, br1 = _router_operands(router_g[1], router_g_bias[1], router_e[1], router_e_bias[1])
    route_shapes, route_specs = _route_out(n_real_tiles, tm)
    h3, xn2, ri2, rf2, cnt2 = pl.pallas_call(
        _oproj_kernel,
        grid=(n_real_tiles,),
        in_specs=[pl.BlockSpec((tm, N_HEADS * V_HEAD_DIM), lambda i: (i, 0)),
                  pl.BlockSpec((tm, d), lambda i: (i, 0)),
                  _full((N_HEADS * V_HEAD_DIM, d)), _full((1, d)),
                  _full((ROUTER_ROWS, d)), _full((ROUTER_ROWS, 1)), _full((tm, tm))],
        out_specs=[pl.BlockSpec((tm, d), lambda i: (i, 0)),
                   pl.BlockSpec((tm * parts, LANES), lambda i: (i, 0))] + route_specs,
        out_shape=[jax.ShapeDtypeStruct((n_tok, d), _F32),
                   jax.ShapeDtypeStruct((n_tok * parts, LANES), jnp.uint32)] + route_shapes,
        scratch_shapes=[pltpu.VMEM((N_EXPERTS, 1), _F32), pltpu.VMEM((ROUTER_SPLIT_ROWS, d), _BF16)],
        compiler_params=_params(1),
        name="attn_out_router",
    )(o, h2, b_w_o[0].astype(_BF16), row(ffn_norm[1]), wrt1, br1, triu)

    dest2, y2 = _moe_experts(xn2, ri2, cnt2, w_gate, w_up, w_down, 1, n_real_tiles, n_real_tiles, d)
    operands, in_specs, scratch = _combine_operands(dest2, rf2, h3, y2, d)
    out = pl.pallas_call(
        functools.partial(_combine_norm_kernel, tm=tm, n_tiles=n_real_tiles),
        grid=(n_real_tiles,),
        in_specs=in_specs + [_full((1, d))],
        out_specs=pl.BlockSpec((tm, d), lambda i: (i, 0)),
        out_shape=jax.ShapeDtypeStruct((n_tok, d), _F32),
        scratch_shapes=scratch,
        compiler_params=_params(1, disable_bounds_checks=True),
        name="moe_combine_final_norm",
    )(*operands, row(final_norm))
    return out.reshape(bsz, seq, d)
```

```python
import functools

import numpy as np
import jax
import jax.numpy as jnp
from jax import lax
from jax.experimental import pallas as pl
from jax.experimental.pallas import tpu as pltpu

N_META = 16
POOL_WINDOWS = (2, 4, 8, 16)
N_HEADS = 8
QK_NOPE_DIM = 128
QK_ROPE_DIM = 64
QK_DIM = QK_NOPE_DIM + QK_ROPE_DIM
V_HEAD_DIM = 128
KV_LORA_RANK = 256
ROPE_THETA = 10000.0
ATTN_SCALE = QK_DIM ** -0.5
Q_SCALE = ATTN_SCALE * 1.4426950408889634
N_EXPERT_GROUPS = 4
EXPERTS_PER_GROUP = 8
N_EXPERTS = N_EXPERT_GROUPS * EXPERTS_PER_GROUP
RMS_EPS = 1e-6
NEG_INF = -1e30

TOKEN_TILE = 256
EXPERT_ROWS = 256
ATTN_Q_TILE = 512
ATTN_K_TILE = ATTN_Q_TILE // 2
ATTN_HEADS_PER_STEP = 4
DMA_UNROLL = 8
DMA_THREADS = 2
ROUTER_ROWS = 8 + N_EXPERTS
ROUTER_LO_ROW = 48
ROUTER_SPLIT_ROWS = 2 * ROUTER_LO_ROW
VMEM_LIMIT_BYTES = 48 * 1024 * 1024

LANES = 128
_F32 = jnp.float32
_BF16 = jnp.bfloat16
_NT_DIMS = (((1,), (1,)), ((), ()))


def _params(n_grid_dims=1, **kw):
    return pltpu.CompilerParams(dimension_semantics=("arbitrary",) * n_grid_dims,
                                vmem_limit_bytes=VMEM_LIMIT_BYTES, **kw)


def _rms(x, g):
    ms = jnp.mean(x * x, axis=-1, keepdims=True)
    return x * lax.rsqrt(ms + RMS_EPS) * g


def _split_bf16(x):
    hi = x.astype(_BF16)
    lo = (x - hi.astype(_F32)).astype(_BF16)
    return hi, lo


def _dot(a, b):
    return jnp.dot(a, b, preferred_element_type=_F32)


def _store_token_tiles(ref, x):
    n, d = x.shape
    parts = d // LANES
    for s in range(parts):
        ref[pl.ds(s, n, stride=parts), :] = x[:, s * LANES:(s + 1) * LANES]


def _load_token_tiles(ref, first_token, n, parts):
    return jnp.concatenate([ref[pl.ds(first_token * parts + s, n, stride=parts), :] for s in range(parts)], axis=1)


_HIGH_HALF = 0xFFFF0000


def _pack_rows(x):
    n, d = x.shape
    words = []
    for s in range(d // (2 * LANES)):
        lo = x[:, 2 * s * LANES:(2 * s + 1) * LANES].astype(_BF16).astype(_F32)
        hi = x[:, (2 * s + 1) * LANES:(2 * s + 2) * LANES].astype(_BF16).astype(_F32)
        words.append(lax.shift_right_logical(lax.bitcast_convert_type(lo, jnp.uint32), jnp.uint32(16))
                     | (lax.bitcast_convert_type(hi, jnp.uint32) & jnp.uint32(_HIGH_HALF)))
    return jnp.concatenate(words, axis=1)


def _unpack_rows(w):
    cols = []
    for s in range(w.shape[1] // LANES):
        ws = w[:, s * LANES:(s + 1) * LANES]
        cols.append(lax.bitcast_convert_type(lax.shift_left(ws, jnp.uint32(16)), _F32))
        cols.append(lax.bitcast_convert_type(ws & jnp.uint32(_HIGH_HALF), _F32))
    return jnp.concatenate(cols, axis=1)


def _route_init(wrt_ref, wsplit_ref, base_ref):
    base_ref[...] = jnp.zeros(base_ref.shape, base_ref.dtype)
    w_hi, w_lo = _split_bf16(wrt_ref[...])
    wsplit_ref[...] = jnp.zeros(wsplit_ref.shape, wsplit_ref.dtype)
    wsplit_ref[0:ROUTER_ROWS, :] = w_hi
    wsplit_ref[ROUTER_LO_ROW:ROUTER_LO_ROW + ROUTER_ROWS, :] = w_lo


def _route(xn, valid, wsplit_ref, br_ref, triu_ref, base_ref, ri_ref, rf_ref, cnt_ref):
    tm = xn.shape[0]
    x_hi, x_lo = _split_bf16(xn)
    by_hi = lax.dot_general(wsplit_ref[...], x_hi, _NT_DIMS, preferred_element_type=_F32)
    by_lo = lax.dot_general(wsplit_ref[0:ROUTER_LO_ROW, :], x_lo, _NT_DIMS, preferred_element_type=_F32)
    logits = (by_hi[0:ROUTER_ROWS] + by_hi[ROUTER_LO_ROW:ROUTER_LO_ROW + ROUTER_ROWS]
              + by_lo[0:ROUTER_ROWS] + br_ref[...])
    lg = logits[0:N_EXPERT_GROUPS]
    eg = jnp.exp(lg - jnp.max(lg, axis=0, keepdims=True))
    pg = eg / jnp.sum(eg, axis=0, keepdims=True)
    w_g = jnp.max(pg, axis=0, keepdims=True)
    ig = lax.broadcasted_iota(jnp.int32, pg.shape, 0).astype(_F32)
    g_sel = jnp.min(jnp.where(pg == w_g, ig, float(N_EXPERT_GROUPS)), axis=0, keepdims=True)

    sel = logits[8:8 + EXPERTS_PER_GROUP]
    for g in range(1, N_EXPERT_GROUPS):
        sel = jnp.where(g_sel == float(g), logits[8 + g * EXPERTS_PER_GROUP:8 + (g + 1) * EXPERTS_PER_GROUP], sel)
    ie = lax.broadcasted_iota(jnp.int32, sel.shape, 0).astype(_F32)
    v1 = jnp.max(sel, axis=0, keepdims=True)
    i1 = jnp.min(jnp.where(sel == v1, ie, float(EXPERTS_PER_GROUP)), axis=0, keepdims=True)
    rest = jnp.where(ie == i1, -jnp.inf, sel)
    v2 = jnp.max(rest, axis=0, keepdims=True)
    i2 = jnp.min(jnp.where(rest == v2, ie, float(EXPERTS_PER_GROUP)), axis=0, keepdims=True)
    e2 = jnp.exp(v2 - v1)
    den = 1.0 + e2
    validf = valid.astype(_F32)
    gate0 = w_g * (1.0 / den) * validf
    gate1 = w_g * (e2 / den) * validf
    eid0 = g_sel * float(EXPERTS_PER_GROUP) + i1
    eid1 = g_sel * float(EXPERTS_PER_GROUP) + i2

    iall = lax.broadcasted_iota(jnp.int32, (N_EXPERTS, tm), 0).astype(_F32)
    oh0 = jnp.where(iall == eid0, validf, 0.0)
    oh1 = jnp.where(iall == eid1, validf, 0.0)
    both = oh0 + oh1
    before = _dot(both.astype(_BF16), triu_ref[...]) + base_ref[...]
    rank0 = jnp.sum(oh0 * before, axis=0, keepdims=True)
    rank1 = jnp.sum(oh1 * before, axis=0, keepdims=True)
    base_ref[...] = base_ref[...] + jnp.sum(both, axis=1, keepdims=True)

    ri_ref[...] = jnp.zeros(ri_ref.shape, ri_ref.dtype)
    rf_ref[...] = jnp.zeros(rf_ref.shape, rf_ref.dtype)
    ri_ref[0, 0:1, :] = eid0.astype(jnp.int32)
    ri_ref[0, 1:2, :] = eid1.astype(jnp.int32)
    ri_ref[0, 2:3, :] = rank0.astype(jnp.int32)
    ri_ref[0, 3:4, :] = rank1.astype(jnp.int32)
    rf_ref[0, 0:1, :] = gate0
    rf_ref[0, 1:2, :] = gate1
    cnt_ref[...] = jnp.broadcast_to(base_ref[...], cnt_ref.shape).astype(jnp.int32)


def _mixer_kernel(x_ref, xh_ref, mp_ref, meta_ref, anorm_ref, aw_ref, ascale_ref, fnorm_ref,
                  wrt_ref, br_ref, pm_ref, ph_ref, triu_ref,
                  h_ref, xn_ref, ri_ref, rf_ref, cnt_ref, base_ref, wsplit_ref, *, n_real_tiles, tiles_per_batch):
    i = pl.program_id(0)
    tm = x_ref.shape[0]
    gd = x_ref.shape[1] // len(POOL_WINDOWS)
    is_meta = i == n_real_tiles
    first = (i % tiles_per_batch) == 0

    @pl.when(i == 0)
    def _():
        _route_init(wrt_ref, wsplit_ref, base_ref)

    h = jnp.where(is_meta, mp_ref[...], x_ref[...])
    halo = jnp.where(is_meta, 0.0, jnp.where(first, meta_ref[...], xh_ref[...]))
    hn = _rms(h, anorm_ref[...])
    hh = _rms(halo, anorm_ref[...])
    hn_hi, hn_lo = _split_bf16(hn)
    hh_hi, hh_lo = _split_bf16(hh)
    row = lax.broadcasted_iota(jnp.int32, (tm, 1), 0)
    for g, w in enumerate(POOL_WINDOWS):
        sl = slice(g * gd, (g + 1) * gd)
        win = _dot(pm_ref[g], hn_hi[:, sl]) + _dot(pm_ref[g], hn_lo[:, sl])
        top = win[:N_META] + _dot(ph_ref[g], hh_hi[:, sl]) + _dot(ph_ref[g], hh_lo[:, sl])
        win = jnp.concatenate([top, win[N_META:]], axis=0)
        cnt = jnp.where(is_meta, jnp.minimum(row + 1, w), w).astype(_F32)
        pooled = win * (1.0 / cnt) - hn[:, sl]
        mix = _dot(pooled.astype(_BF16), aw_ref[g])
        h_ref[:, sl] = h[:, sl] + mix * ascale_ref[:, sl]

    xn = _rms(h_ref[...], fnorm_ref[...])
    _store_token_tiles(xn_ref, _pack_rows(xn))
    lane = lax.broadcasted_iota(jnp.int32, (1, tm), 1)
    valid = jnp.logical_or(jnp.logical_not(is_meta), lane < N_META)
    _route(xn, valid, wsplit_ref, br_ref, triu_ref, base_ref, ri_ref, rf_ref, cnt_ref)


def _dispatch_kernel(dest_hbm, xn_hbm, xs_hbm, idx_smem, xbuf, isem, lsem, ssem, *, tm, n_tiles, n_real_tiles):
    i = pl.program_id(0)
    parts = xbuf.shape[1] // tm
    buf = i % 3

    def idx_copy(tile):
        return pltpu.make_async_copy(dest_hbm.at[tile], idx_smem.at[pl.ds(tile % 2, 1)], isem.at[tile % 2])

    def load(tile):
        rows = tm * parts
        return pltpu.make_async_copy(xn_hbm.at[pl.ds(pl.multiple_of(tile * rows, rows), rows), :],
                                     xbuf.at[tile % 3], lsem.at[tile % 3])

    def wait_scatters(tile):
        def wait_rows(n_tok):
            for k in range(2):
                pltpu.make_async_copy(xbuf.at[tile % 3, pl.ds(0, n_tok * parts), :],
                                      xs_hbm.at[pl.ds(0, n_tok * parts), :], ssem.at[tile % 3]).wait()

        @pl.when(tile < n_real_tiles)
        def _():
            wait_rows(tm)

        @pl.when(tile >= n_real_tiles)
        def _():
            wait_rows(N_META)

    def scatter(n_tok):
        def issue(g, c):
            for u in range(min(DMA_UNROLL, n_tok)):
                t = g * min(DMA_UNROLL, n_tok) + u
                src = xbuf.at[buf, pl.ds(pl.multiple_of(t * parts, parts), parts), :]
                for k in range(2):
                    d = idx_smem[i % 2, k * tm + t]
                    pltpu.make_async_copy(src, xs_hbm.at[pl.ds(pl.multiple_of(d * parts, parts), parts), :],
                                          ssem.at[buf]).start(priority=(2 * u + k) % DMA_THREADS)
            return c

        lax.fori_loop(0, n_tok // min(DMA_UNROLL, n_tok), issue, 0)

    @pl.when(i == 0)
    def _():
        idx_copy(0).start()
        load(0).start()

    @pl.when(i >= 2)
    def _():
        wait_scatters(i - 2)

    @pl.when(i + 1 < n_tiles)
    def _():
        idx_copy(i + 1).start()
        load(i + 1).start()

    idx_copy(i).wait()
    load(i).wait()

    @pl.when(i < n_real_tiles)
    def _():
        scatter(tm)

    @pl.when(i >= n_real_tiles)
    def _():
        scatter(N_META)

    @pl.when(i == n_tiles - 1)
    def _():
        if n_tiles >= 2:
            wait_scatters(i - 1)
        wait_scatters(i)


def _expert_kernel(be_ref, bv_ref, nu_ref, xs_ref, wg_ref, wu_ref, wd_ref, y_ref, wgb_ref, wub_ref, wdb_ref):
    i = pl.program_id(0)
    active = i < nu_ref[0]
    new_expert = jnp.logical_or(i == 0, be_ref[i] != be_ref[jnp.maximum(i - 1, 0)])

    @pl.when(jnp.logical_and(active, new_expert))
    def _():
        wgb_ref[...] = wg_ref[0, 0].astype(_BF16)
        wub_ref[...] = wu_ref[0, 0].astype(_BF16)
        wdb_ref[...] = wd_ref[0, 0].astype(_BF16)

    @pl.when(active)
    def _():
        parts = wgb_ref.shape[0] // (2 * LANES)
        tr = xs_ref.shape[0] // parts
        rows = lax.broadcasted_iota(jnp.int32, (tr, 1), 0)
        x = jnp.where(rows < bv_ref[i], _unpack_rows(_load_token_tiles(xs_ref, 0, tr, parts)), 0.0).astype(_BF16)
        g = _dot(x, wgb_ref[...])
        u = _dot(x, wub_ref[...])
        a = g * (1.0 / (1.0 + jnp.exp(-g))) * u
        _store_token_tiles(y_ref, _pack_rows(_dot(a.astype(_BF16), wdb_ref[...])))

    @pl.when(i >= nu_ref[0])
    def _():
        y_ref[...] = jnp.zeros(y_ref.shape, y_ref.dtype)


def _combined_tile(dest_hbm, rf_ref, h_ref, y_hbm, eye_ref, idx_smem, ybuf, isem, rsem, *, tm, n_tiles):
    i = pl.program_id(0)
    parts = ybuf.shape[1] // (2 * tm)
    slot = i % 2

    def idx_copy(tile):
        return pltpu.make_async_copy(dest_hbm.at[tile], idx_smem.at[pl.ds(tile % 2, 1)], isem.at[tile % 2])

    def row_copy(idx_slot, buf_slot, t, k, u):
        d = idx_smem[idx_slot, k * tm + t]
        return pltpu.make_async_copy(
            y_hbm.at[pl.ds(pl.multiple_of(d * parts, parts), parts), :],
            ybuf.at[buf_slot, pl.ds(pl.multiple_of((k * tm + t) * parts, parts), parts), :],
            rsem.at[buf_slot]).start(priority=(2 * u + k) % DMA_THREADS)

    def wait_rows(buf_slot):
        pltpu.make_async_copy(y_hbm.at[pl.ds(0, 2 * tm * parts), :], ybuf.at[buf_slot], rsem.at[buf_slot]).wait()

    @pl.when(i == 0)
    def _():
        idx_copy(0).start()
        idx_copy(0).wait()

        def issue(g, c):
            for u in range(DMA_UNROLL):
                for k in range(2):
                    row_copy(0, 0, g * DMA_UNROLL + u, k, u)
            return c

        lax.fori_loop(0, tm // DMA_UNROLL, issue, 0)
        if n_tiles >= 2:
            idx_copy(1).start()

    @pl.when(i + 1 < n_tiles)
    def _():
        idx_copy(i + 1).wait()

    @pl.when(i + 2 < n_tiles)
    def _():
        idx_copy(i + 2).start()

    nxt = jnp.minimum(i + 1, n_tiles - 1)
    for t in range(tm):
        for k in range(2):
            row_copy(nxt % 2, 1 - slot, t, k, t)

    gt = lax.dot_general(eye_ref[...], rf_ref[0], _NT_DIMS, precision=lax.Precision.HIGHEST,
                         preferred_element_type=_F32)
    wait_rows(slot)
    yb = ybuf.at[slot]
    out = h_ref[...] + (_unpack_rows(_load_token_tiles(yb, 0, tm, parts)) * gt[:, 0:1]
                        + _unpack_rows(_load_token_tiles(yb, tm, tm, parts)) * gt[:, 1:2])

    def drain():
        @pl.when(i == n_tiles - 1)
        def _():
            wait_rows(1 - slot)

    return out, drain


def _combine_norm_kernel(dest_hbm, rf_ref, h_ref, y_hbm, eye_ref, fnorm_ref, out_ref,
                         idx_smem, ybuf, isem, rsem, *, tm, n_tiles):
    h, drain = _combined_tile(dest_hbm, rf_ref, h_ref, y_hbm, eye_ref, idx_smem, ybuf, isem, rsem,
                              tm=tm, n_tiles=n_tiles)
    out_ref[...] = _rms(h, fnorm_ref[...])
    drain()


def _combine_proj_kernel(dest_hbm, rf_ref, h_ref, y_hbm, eye_ref, cos_ref, sin_ref, kvn_ref, wdkv_ref, kvlat_ref,
                         wuk_ref, wuv_ref, bnorm_ref, wdq_ref, qnorm_ref, wuq_ref,
                         h_out_ref, q_ref, k_ref, v_ref, idx_smem, ybuf, isem, rsem, *, tm, n_tiles):
    h, drain = _combined_tile(dest_hbm, rf_ref, h_ref, y_hbm, eye_ref, idx_smem, ybuf, isem, rsem,
                              tm=tm, n_tiles=n_tiles)
    h_out_ref[...] = h
    _project(h, cos_ref, sin_ref, kvn_ref, wdkv_ref, kvlat_ref, wuk_ref, wuv_ref,
             bnorm_ref, wdq_ref, qnorm_ref, wuq_ref, q_ref, k_ref, v_ref)
    drain()


def _rope128(x, cos_t, sin_t):
    lane = lax.broadcasted_iota(jnp.int32, (1, 128), 1)
    first_half = (lane % QK_ROPE_DIM) < (QK_ROPE_DIM // 2)
    swapped = jnp.where(first_half, pltpu.roll(x, 128 - QK_ROPE_DIM // 2, axis=1),
                        pltpu.roll(x, QK_ROPE_DIM // 2, axis=1))
    return x * cos_t + swapped * sin_t


def _project(h, cos_ref, sin_ref, kvn_ref, wdkv_ref, kvlat_ref, wuk_ref, wuv_ref,
             bnorm_ref, wdq_ref, qnorm_ref, wuq_ref, q_ref, k_ref, v_ref):
    cos_t = cos_ref[...]
    sin_t = sin_ref[...]
    c = _dot(_rms(h, kvn_ref[...]).astype(_BF16), wdkv_ref[...])
    ckv = _rms(c[:, :KV_LORA_RANK], kvlat_ref[...]).astype(_BF16)
    kr = _rope128(c[:, KV_LORA_RANK:KV_LORA_RANK + 128], cos_t, sin_t)[:, :QK_ROPE_DIM].astype(_BF16)
    kn = _dot(ckv, wuk_ref[...])
    vv = _dot(ckv, wuv_ref[...])
    cq = _rms(_dot(_rms(h, bnorm_ref[...]).astype(_BF16), wdq_ref[...]), qnorm_ref[...]).astype(_BF16)
    q = _dot(cq, wuq_ref[...])
    rope0 = N_HEADS * QK_NOPE_DIM
    for hd in range(N_HEADS):
        k_ref[hd, :, 0:QK_NOPE_DIM] = kn[:, hd * QK_NOPE_DIM:(hd + 1) * QK_NOPE_DIM].astype(_BF16)
        k_ref[hd, :, QK_NOPE_DIM:QK_DIM] = kr
        v_ref[hd, 0] = vv[:, hd * V_HEAD_DIM:(hd + 1) * V_HEAD_DIM].T.astype(_BF16)
        q_ref[hd, :, 0:QK_NOPE_DIM] = (q[:, hd * QK_NOPE_DIM:(hd + 1) * QK_NOPE_DIM] * Q_SCALE).astype(_BF16)
        qr = _rope128(q[:, rope0 + hd * 128:rope0 + (hd + 1) * 128], cos_t, sin_t)[:, :QK_ROPE_DIM]
        q_ref[hd, :, QK_NOPE_DIM:QK_DIM] = (qr * Q_SCALE).astype(_BF16)


def _attn_kernel(q_ref, k_ref, vt_ref, km_ref, vmt_ref, o_ref, sa_ref, sb_ref, acc_ref, *, tq, tk):
    j = pl.program_id(2)
    heads = range(q_ref.shape[0])
    vd = acc_ref.shape[1]

    def scores(kb, s_ref):
        start = pl.multiple_of(kb * tk, tk)
        for hd in heads:
            s_ref[hd] = lax.dot_general(k_ref[hd, pl.ds(start, tk), :], q_ref[hd], _NT_DIMS,
                                        preferred_element_type=_F32)

    def update(kb, s_ref, carry, masked):
        out = []
        for hd in heads:
            m, l = carry[hd]
            s = s_ref[hd]
            if masked:
                keys = kb * tk + lax.broadcasted_iota(jnp.int32, (tk, tq), 0)
                qpos = j * tq + lax.broadcasted_iota(jnp.int32, (tk, tq), 1)
                s = jnp.where(keys <= qpos, s, NEG_INF)
            m_new = jnp.maximum(m, jnp.max(s, axis=0, keepdims=True))
            alpha = jnp.exp2(m - m_new)
            p = jnp.exp2(s - m_new)
            l = alpha * l + jnp.sum(p, axis=0, keepdims=True)
            acc_ref[hd] = acc_ref[hd] * alpha + _dot(vt_ref[hd, kb], p.astype(_BF16))
            out.append((m_new, l))
        return tuple(out)

    scores(0, sa_ref)
    carry = []
    for hd in heads:
        s0 = lax.dot_general(km_ref[hd], q_ref[hd], _NT_DIMS, preferred_element_type=_F32)
        m = jnp.max(s0, axis=0, keepdims=True)
        p0 = jnp.exp2(s0 - m)
        carry.append((m, jnp.sum(p0, axis=0, keepdims=True)))
        acc_ref[hd] = _dot(vmt_ref[hd, 0, :, 0:N_META], p0.astype(_BF16))
    carry = tuple(carry)

    def pair(kp, carry):
        scores(2 * kp + 1, sb_ref)
        carry = update(2 * kp, sa_ref, carry, False)
        scores(2 * kp + 2, sa_ref)
        return update(2 * kp + 1, sb_ref, carry, False)

    carry = lax.fori_loop(0, j, pair, carry)
    scores(2 * j + 1, sb_ref)
    carry = update(2 * j, sa_ref, carry, True)
    carry = update(2 * j + 1, sb_ref, carry, True)
    for hd in heads:
        o_ref[:, hd * vd:(hd + 1) * vd] = (acc_ref[hd] / carry[hd][1]).T.astype(o_ref.dtype)


def _oproj_kernel(o_ref, h_ref, wo_ref, fnorm_ref, wrt_ref, br_ref, triu_ref,
                  h_out_ref, xn_ref, ri_ref, rf_ref, cnt_ref, base_ref, wsplit_ref):
    i = pl.program_id(0)

    @pl.when(i == 0)
    def _():
        _route_init(wrt_ref, wsplit_ref, base_ref)

    h = h_ref[...] + _dot(o_ref[...], wo_ref[...])
    h_out_ref[...] = h
    xn = _rms(h, fnorm_ref[...])
    _store_token_tiles(xn_ref, _pack_rows(xn))
    valid = lax.broadcasted_iota(jnp.int32, (1, h.shape[0]), 1) >= 0
    _route(xn, valid, wsplit_ref, br_ref, triu_ref, base_ref, ri_ref, rf_ref, cnt_ref)


def _full(shape):
    nd = len(shape)
    return pl.BlockSpec(shape, lambda *_: (0,) * nd)


def _router_operands(router_g, router_g_bias, router_e, router_e_bias):
    d = router_g.shape[0]
    wrt = jnp.concatenate([router_g.T, jnp.zeros((8 - N_EXPERT_GROUPS, d), _F32), router_e.T], axis=0)
    br = jnp.concatenate([router_g_bias, jnp.zeros((8 - N_EXPERT_GROUPS,), _F32), router_e_bias])[:, None]
    return wrt.astype(_F32), br.astype(_F32)


def _route_out(n_tiles, tm):
    shapes = [jax.ShapeDtypeStruct((n_tiles, 8, tm), jnp.int32),
              jax.ShapeDtypeStruct((n_tiles, 8, tm), _F32),
              jax.ShapeDtypeStruct((N_EXPERTS, 128), jnp.int32)]
    specs = [pl.BlockSpec((1, 8, tm), lambda i: (i, 0, 0)),
             pl.BlockSpec((1, 8, tm), lambda i: (i, 0, 0)),
             pl.BlockSpec((N_EXPERTS, 128), lambda i: (0, 0))]
    return shapes, specs


def _moe_experts(xn, ri, counts, w_gate, w_up, w_down, layer, n_tiles, n_real_tiles, d):
    tm, tr = TOKEN_TILE, EXPERT_ROWS
    parts = d // (2 * LANES)
    n_valid = n_real_tiles * tm + (n_tiles - n_real_tiles) * N_META
    n_blocks = -(-(2 * n_valid + N_EXPERTS * (tr - 1)) // tr)
    n_rows = n_blocks * tr

    counts = counts[:, 0]
    padded = (counts + tr - 1) // tr * tr
    pends = jnp.cumsum(padded)
    pstarts = pends - padded
    n_used = (pends[-1] // tr).astype(jnp.int32).reshape(1)
    blk0 = jnp.arange(n_blocks, dtype=jnp.int32) * tr
    block_e = jnp.minimum(jnp.sum(blk0[:, None] >= pends[None, :], axis=1), N_EXPERTS - 1).astype(jnp.int32)
    experts = jnp.arange(N_EXPERTS, dtype=jnp.int32)
    block_oh = block_e[:, None] == experts[None, :]
    block_cnt = jnp.sum(jnp.where(block_oh, counts[None, :], 0), axis=1)
    block_start = jnp.sum(jnp.where(block_oh, pstarts[None, :], 0), axis=1)
    block_valid = jnp.clip(block_cnt - (blk0 - block_start), 0, tr).astype(jnp.int32)
    eid = ri[:, 0:2, :]
    slot0 = jnp.sum(jnp.where(eid[..., None] == experts, pstarts, 0), axis=-1)
    dest = (slot0 + ri[:, 2:4, :]).astype(jnp.int32).reshape(n_tiles, 1, 2 * tm)

    xs = pl.pallas_call(
        functools.partial(_dispatch_kernel, tm=tm, n_tiles=n_tiles, n_real_tiles=n_real_tiles),
        grid=(n_tiles,),
        in_specs=[pl.BlockSpec(memory_space=pl.ANY), pl.BlockSpec(memory_space=pl.ANY)],
        out_specs=pl.BlockSpec(memory_space=pl.ANY),
        out_shape=jax.ShapeDtypeStruct((n_rows * parts, LANES), jnp.uint32),
        scratch_shapes=[pltpu.SMEM((2, 2 * tm), jnp.int32), pltpu.VMEM((3, tm * parts, LANES), jnp.uint32),
                        pltpu.SemaphoreType.DMA((2,)), pltpu.SemaphoreType.DMA((3,)),
                        pltpu.SemaphoreType.DMA((3,))],
        compiler_params=_params(1, has_side_effects=True, disable_bounds_checks=True),
        name="moe_dispatch",
    )(dest, xn)

    f = w_gate.shape[3]
    expert_block = lambda i, be, bv, nu: (layer, be[last(i, be, bv, nu)], 0, 0)
    last = lambda i, be, bv, nu: jnp.minimum(i, nu[0] - 1)
    y = pl.pallas_call(
        _expert_kernel,
        grid_spec=pltpu.PrefetchScalarGridSpec(
            num_scalar_prefetch=3,
            grid=(n_blocks,),
            in_specs=[pl.BlockSpec((tr * parts, LANES), lambda i, be, bv, nu: (last(i, be, bv, nu), 0)),
                      pl.BlockSpec((1, 1, d, f), expert_block),
                      pl.BlockSpec((1, 1, d, f), expert_block),
                      pl.BlockSpec((1, 1, f, d), expert_block)],
            out_specs=pl.BlockSpec((tr * parts, LANES), lambda i, be, bv, nu: (i, 0)),
            scratch_shapes=[pltpu.VMEM((d, f), _BF16), pltpu.VMEM((d, f), _BF16), pltpu.VMEM((f, d), _BF16)]),
        out_shape=jax.ShapeDtypeStruct((n_rows * parts, LANES), jnp.uint32),
        compiler_params=_params(1),
        name="moe_experts",
    )(block_e, block_valid, n_used, xs, w_gate, w_up, w_down)

    return dest, y


def _combine_operands(dest, rf, h, y, d):
    tm = TOKEN_TILE
    parts = d // (2 * LANES)
    eye = jnp.asarray(np.eye(tm, dtype=np.float32))
    in_specs = [pl.BlockSpec(memory_space=pl.ANY),
                pl.BlockSpec((1, 8, tm), lambda i: (i, 0, 0)),
                pl.BlockSpec((tm, d), lambda i: (i, 0)),
                pl.BlockSpec(memory_space=pl.ANY),
                _full((tm, tm))]
    scratch = [pltpu.SMEM((2, 2 * tm), jnp.int32), pltpu.VMEM((2, 2 * tm * parts, LANES), jnp.uint32),
               pltpu.SemaphoreType.DMA((2,)), pltpu.SemaphoreType.DMA((2,))]
    return (dest, rf, h, y, eye), in_specs, scratch


def kernel(x, meta_tokens, a_norm, a_w, a_scale, b_norm, b_w_dq, b_q_norm, b_w_uq, b_w_o, kv_norm, w_dkv,
           kv_lat_norm, w_uk, w_uv, ffn_norm, router_g, router_g_bias, router_e, router_e_bias, w_gate, w_up,
           w_down, final_norm):
    bsz, seq, d = x.shape
    tm = TOKEN_TILE
    assert seq % tm == 0 and seq % ATTN_Q_TILE == 0 and ATTN_Q_TILE == 2 * ATTN_K_TILE and ATTN_K_TILE == tm
    assert d % (LANES * len(POOL_WINDOWS)) == 0 and N_META == max(POOL_WINDOWS) and N_META <= tm
    parts = d // (2 * LANES)
    n_tok = bsz * seq
    n_real_tiles = n_tok // tm
    tiles_per_batch = seq // tm
    n_tiles = n_real_tiles + 1
    gd = d // len(POOL_WINDOWS)
    row = lambda v: v.reshape(1, -1).astype(_F32)

    x2 = x.reshape(n_tok, d)
    meta_pad = jnp.concatenate([meta_tokens, jnp.zeros((tm - N_META, d), x.dtype)], axis=0)

    r = np.arange(tm)[:, None]
    cidx = np.arange(tm)[None, :]
    pm = np.stack([((r - cidx >= 0) & (r - cidx < w)) for w in POOL_WINDOWS]).astype(np.float32)
    hc = np.arange(N_META)[None, :]
    ph = np.stack([(r[:N_META] + N_META - hc < w) for w in POOL_WINDOWS]).astype(np.float32)
    triu = (r < cidx).astype(np.float32)
    pm, ph, triu = (jnp.asarray(a, dtype=_BF16) for a in (pm, ph, triu))

    wrt0, br0 = _router_operands(router_g[0], router_g_bias[0], router_e[0], router_e_bias[0])
    route_shapes, route_specs = _route_out(n_tiles, tm)
    tile_or_last = lambda i: (jnp.minimum(i, n_real_tiles - 1), 0)
    halo_blocks = tm // N_META
    h1, xn1, ri1, rf1, cnt1 = pl.pallas_call(
        functools.partial(_mixer_kernel, n_real_tiles=n_real_tiles, tiles_per_batch=tiles_per_batch),
        grid=(n_tiles,),
        in_specs=[pl.BlockSpec((tm, d), tile_or_last),
                  pl.BlockSpec((N_META, d), lambda i: (jnp.clip(i * halo_blocks - 1, 0, n_tok // N_META - 1), 0)),
                  _full((tm, d)), _full((N_META, d)), _full((1, d)),
                  _full((len(POOL_WINDOWS), gd, gd)), _full((1, d)), _full((1, d)),
                  _full((ROUTER_ROWS, d)), _full((ROUTER_ROWS, 1)),
                  _full(pm.shape), _full(ph.shape), _full((tm, tm))],
        out_specs=[pl.BlockSpec((tm, d), lambda i: (i, 0)),
                   pl.BlockSpec((tm * parts, LANES), lambda i: (i, 0))] + route_specs,
        out_shape=[jax.ShapeDtypeStruct((n_tiles * tm, d), _F32),
                   jax.ShapeDtypeStruct((n_tiles * tm * parts, LANES), jnp.uint32)] + route_shapes,
        scratch_shapes=[pltpu.VMEM((N_EXPERTS, 1), _F32), pltpu.VMEM((ROUTER_SPLIT_ROWS, d), _BF16)],
        compiler_params=_params(1),
        name="pool_mixer_router",
    )(x2, x2, meta_pad, meta_tokens, row(a_norm[0]), a_w[0].astype(_BF16), row(a_scale[0]), row(ffn_norm[0]),
      wrt0, br0, pm, ph, triu)

    dest1, y1 = _moe_experts(xn1, ri1, cnt1, w_gate, w_up, w_down, 0, n_tiles, n_real_tiles, d)

    pos = jnp.concatenate([jnp.arange(seq, dtype=_F32) + N_META, jnp.arange(tm, dtype=_F32)])
    inv_freq = ROPE_THETA ** (-jnp.arange(0, QK_ROPE_DIM, 2, dtype=_F32) / QK_ROPE_DIM)
    ang = pos[:, None] * inv_freq[None, :]
    cos_t = jnp.tile(jnp.cos(ang), (1, 4))
    sin_t = jnp.tile(jnp.concatenate([-jnp.sin(ang), jnp.sin(ang)], axis=1), (1, 2))

    wdkv = jnp.concatenate([w_dkv, w_dkv[:, KV_LORA_RANK:]], axis=1).astype(_BF16)
    wuk = w_uk.reshape(KV_LORA_RANK, N_HEADS * QK_NOPE_DIM).astype(_BF16)
    wuv = w_uv.reshape(KV_LORA_RANK, N_HEADS * V_HEAD_DIM).astype(_BF16)
    wuq = b_w_uq[0]
    q_rank = wuq.shape[0]
    wuq_rope = wuq[:, :, QK_NOPE_DIM:]
    wuq = jnp.concatenate([wuq[:, :, :QK_NOPE_DIM].reshape(q_rank, -1),
                           jnp.concatenate([wuq_rope, wuq_rope], axis=2).reshape(q_rank, -1)], axis=1).astype(_BF16)
    wdq = b_w_dq[0].astype(_BF16)
    pos_tile = lambda i: (jnp.where(i < n_real_tiles, i % tiles_per_batch, tiles_per_batch), 0)
    head_tile = lambda i: (0, i, 0)
    operands, in_specs, scratch = _combine_operands(dest1, rf1, h1, y1, d)
    h2, q, k, v = pl.pallas_call(
        functools.partial(_combine_proj_kernel, tm=tm, n_tiles=n_tiles),
        grid=(n_tiles,),
        in_specs=in_specs + [
            pl.BlockSpec((tm, 128), pos_tile), pl.BlockSpec((tm, 128), pos_tile),
            _full((1, d)), _full(wdkv.shape), _full((1, KV_LORA_RANK)), _full(wuk.shape), _full(wuv.shape),
            _full((1, d)), _full(wdq.shape), _full((1, q_rank)), _full(wuq.shape)],
        out_specs=[pl.BlockSpec((tm, d), lambda i: (i, 0)),
                   pl.BlockSpec((N_HEADS, tm, QK_DIM), head_tile),
                   pl.BlockSpec((N_HEADS, tm, QK_DIM), head_tile),
                   pl.BlockSpec((N_HEADS, 1, V_HEAD_DIM, tm), lambda i: (0, i, 0, 0))],
        out_shape=[jax.ShapeDtypeStruct((n_tiles * tm, d), _F32),
                   jax.ShapeDtypeStruct((N_HEADS, n_tiles * tm, QK_DIM), _BF16),
                   jax.ShapeDtypeStruct((N_HEADS, n_tiles * tm, QK_DIM), _BF16),
                   jax.ShapeDtypeStruct((N_HEADS, n_tiles, V_HEAD_DIM, tm), _BF16)],
        scratch_shapes=scratch,
        compiler_params=_params(1, disable_bounds_checks=True),
        name="moe_combine_latent_qkv",
    )(*operands, cos_t, sin_t, row(kv_norm), wdkv, row(kv_lat_norm), wuk, wuv,
      row(b_norm[0]), wdq, row(b_q_norm[0]), wuq)

    tq, tk, hp = ATTN_Q_TILE, ATTN_K_TILE, ATTN_HEADS_PER_STEP
    n_q = seq // tq
    meta_block = n_tok // N_META
    o = pl.pallas_call(
        functools.partial(_attn_kernel, tq=tq, tk=tk),
        grid=(bsz, N_HEADS // hp, n_q),
        in_specs=[pl.BlockSpec((hp, tq, QK_DIM), lambda b, hg, j: (hg, b * n_q + j, 0)),
                  pl.BlockSpec((hp, seq, QK_DIM), lambda b, hg, j: (hg, b, 0)),
                  pl.BlockSpec((hp, tiles_per_batch, V_HEAD_DIM, tm), lambda b, hg, j: (hg, b, 0, 0)),
                  pl.BlockSpec((hp, N_META, QK_DIM), lambda b, hg, j: (hg, meta_block, 0)),
                  pl.BlockSpec((hp, 1, V_HEAD_DIM, tm), lambda b, hg, j: (hg, n_real_tiles, 0, 0))],
        out_specs=pl.BlockSpec((tq, hp * V_HEAD_DIM), lambda b, hg, j: (b * n_q + j, hg)),
        out_shape=jax.ShapeDtypeStruct((n_tok, N_HEADS * V_HEAD_DIM), _BF16),
        scratch_shapes=[pltpu.VMEM((hp, tk, tq), _F32), pltpu.VMEM((hp, tk, tq), _F32),
                        pltpu.VMEM((hp, V_HEAD_DIM, tq), _F32)],
        compiler_params=_params(3),
        name="causal_attention",
    )(q, k, v, k, v)

    wrt1, br1 = _router_operands(router_g[1], router_g_bias[1], router_e[1], router_e_bias[1])
    route_shapes, route_specs = _route_out(n_real_tiles, tm)
    h3, xn2, ri2, rf2, cnt2 = pl.pallas_call(
        _oproj_kernel,
        grid=(n_real_tiles,),
        in_specs=[pl.BlockSpec((tm, N_HEADS * V_HEAD_DIM), lambda i: (i, 0)),
                  pl.BlockSpec((tm, d), lambda i: (i, 0)),
                  _full((N_HEADS * V_HEAD_DIM, d)), _full((1, d)),
                  _full((ROUTER_ROWS, d)), _full((ROUTER_ROWS, 1)), _full((tm, tm))],
        out_specs=[pl.BlockSpec((tm, d), lambda i: (i, 0)),
                   pl.BlockSpec((tm * parts, LANES), lambda i: (i, 0))] + route_specs,
        out_shape=[jax.ShapeDtypeStruct((n_tok, d), _F32),
                   jax.ShapeDtypeStruct((n_tok * parts, LANES), jnp.uint32)] + route_shapes,
        scratch_shapes=[pltpu.VMEM((N_EXPERTS, 1), _F32), pltpu.VMEM((ROUTER_SPLIT_ROWS, d), _BF16)],
        compiler_params=_params(1),
        name="attn_out_router",
    )(o, h2, b_w_o[0].astype(_BF16), row(ffn_norm[1]), wrt1, br1, triu)

    dest2, y2 = _moe_experts(xn2, ri2, cnt2, w_gate, w_up, w_down, 1, n_real_tiles, n_real_tiles, d)
    operands, in_specs, scratch = _combine_operands(dest2, rf2, h3, y2, d)
    out = pl.pallas_call(
        functools.partial(_combine_norm_kernel, tm=tm, n_tiles=n_real_tiles),
        grid=(n_real_tiles,),
        in_specs=in_specs + [_full((1, d))],
        out_specs=pl.BlockSpec((tm, d), lambda i: (i, 0)),
        out_shape=jax.ShapeDtypeStruct((n_tok, d), _F32),
        scratch_shapes=scratch,
        compiler_params=_params(1, disable_bounds_checks=True),
        name="moe_combine_final_norm",
    )(*operands, row(final_norm))
    return out.reshape(bsz, seq, d)
```

```python
import functools

import numpy as np
import jax
import jax.numpy as jnp
from jax import lax
from jax.experimental import pallas as pl
from jax.experimental.pallas import tpu as pltpu

N_META = 16
POOL_WINDOWS = (2, 4, 8, 16)
N_HEADS = 8
QK_NOPE_DIM = 128
QK_ROPE_DIM = 64
QK_DIM = QK_NOPE_DIM + QK_ROPE_DIM
V_HEAD_DIM = 128
KV_LORA_RANK = 256
ROPE_THETA = 10000.0
ATTN_SCALE = QK_DIM ** -0.5
Q_SCALE = ATTN_SCALE * 1.4426950408889634
N_EXPERT_GROUPS = 4
EXPERTS_PER_GROUP = 8
N_EXPERTS = N_EXPERT_GROUPS * EXPERTS_PER_GROUP
RMS_EPS = 1e-6
NEG_INF = -1e30

TOKEN_TILE = 256
EXPERT_ROWS = 256
EXPERT_BLOCKS_PER_STEP = 2
ATTN_Q_TILE = 512
ATTN_K_TILE = ATTN_Q_TILE // 2
ATTN_HEADS_PER_STEP = 4
DMA_UNROLL = 8
DMA_THREADS = 2
ROUTER_ROWS = 8 + N_EXPERTS
ROUTER_LO_ROW = 48
ROUTER_SPLIT_ROWS = 2 * ROUTER_LO_ROW
VMEM_LIMIT_BYTES = 48 * 1024 * 1024

LANES = 128
_F32 = jnp.float32
_BF16 = jnp.bfloat16
_NT_DIMS = (((1,), (1,)), ((), ()))


def _params(n_grid_dims=1, **kw):
    return pltpu.CompilerParams(dimension_semantics=("arbitrary",) * n_grid_dims,
                                vmem_limit_bytes=VMEM_LIMIT_BYTES, **kw)


def _rms(x, g):
    ms = jnp.mean(x * x, axis=-1, keepdims=True)
    return x * lax.rsqrt(ms + RMS_EPS) * g


def _split_bf16(x):
    hi = x.astype(_BF16)
    lo = (x - hi.astype(_F32)).astype(_BF16)
    return hi, lo


def _dot(a, b):
    return jnp.dot(a, b, preferred_element_type=_F32)


def _store_token_tiles(ref, x):
    n, d = x.shape
    parts = d // LANES
    for s in range(parts):
        ref[pl.ds(s, n, stride=parts), :] = x[:, s * LANES:(s + 1) * LANES]


def _load_token_tiles(ref, first_token, n, parts):
    return jnp.concatenate([ref[pl.ds(first_token * parts + s, n, stride=parts), :] for s in range(parts)], axis=1)


_HIGH_HALF = 0xFFFF0000


def _pack_rows(x):
    n, d = x.shape
    words = []
    for s in range(d // (2 * LANES)):
        lo = x[:, 2 * s * LANES:(2 * s + 1) * LANES].astype(_BF16).astype(_F32)
        hi = x[:, (2 * s + 1) * LANES:(2 * s + 2) * LANES].astype(_BF16).astype(_F32)
        words.append(lax.shift_right_logical(lax.bitcast_convert_type(lo, jnp.uint32), jnp.uint32(16))
                     | (lax.bitcast_convert_type(hi, jnp.uint32) & jnp.uint32(_HIGH_HALF)))
    return jnp.concatenate(words, axis=1)


def _unpack_rows(w):
    cols = []
    for s in range(w.shape[1] // LANES):
        ws = w[:, s * LANES:(s + 1) * LANES]
        cols.append(lax.bitcast_convert_type(lax.shift_left(ws, jnp.uint32(16)), _F32))
        cols.append(lax.bitcast_convert_type(ws & jnp.uint32(_HIGH_HALF), _F32))
    return jnp.concatenate(cols, axis=1)


def _route_init(wrt_ref, wsplit_ref, base_ref):
    base_ref[...] = jnp.zeros(base_ref.shape, base_ref.dtype)
    w_hi, w_lo = _split_bf16(wrt_ref[...])
    wsplit_ref[...] = jnp.zeros(wsplit_ref.shape, wsplit_ref.dtype)
    wsplit_ref[0:ROUTER_ROWS, :] = w_hi
    wsplit_ref[ROUTER_LO_ROW:ROUTER_LO_ROW + ROUTER_ROWS, :] = w_lo


def _route(xn, valid, wsplit_ref, br_ref, triu_ref, base_ref, ri_ref, rf_ref, cnt_ref):
    tm = xn.shape[0]
    x_hi, x_lo = _split_bf16(xn)
    by_hi = lax.dot_general(wsplit_ref[...], x_hi, _NT_DIMS, preferred_element_type=_F32)
    by_lo = lax.dot_general(wsplit_ref[0:ROUTER_LO_ROW, :], x_lo, _NT_DIMS, preferred_element_type=_F32)
    logits = (by_hi[0:ROUTER_ROWS] + by_hi[ROUTER_LO_ROW:ROUTER_LO_ROW + ROUTER_ROWS]
              + by_lo[0:ROUTER_ROWS] + br_ref[...])
    lg = logits[0:N_EXPERT_GROUPS]
    eg = jnp.exp(lg - jnp.max(lg, axis=0, keepdims=True))
    pg = eg / jnp.sum(eg, axis=0, keepdims=True)
    w_g = jnp.max(pg, axis=0, keepdims=True)
    ig = lax.broadcasted_iota(jnp.int32, pg.shape, 0).astype(_F32)
    g_sel = jnp.min(jnp.where(pg == w_g, ig, float(N_EXPERT_GROUPS)), axis=0, keepdims=True)

    sel = logits[8:8 + EXPERTS_PER_GROUP]
    for g in range(1, N_EXPERT_GROUPS):
        sel = jnp.where(g_sel == float(g), logits[8 + g * EXPERTS_PER_GROUP:8 + (g + 1) * EXPERTS_PER_GROUP], sel)
    ie = lax.broadcasted_iota(jnp.int32, sel.shape, 0).astype(_F32)
    v1 = jnp.max(sel, axis=0, keepdims=True)
    i1 = jnp.min(jnp.where(sel == v1, ie, float(EXPERTS_PER_GROUP)), axis=0, keepdims=True)
    rest = jnp.where(ie == i1, -jnp.inf, sel)
    v2 = jnp.max(rest, axis=0, keepdims=True)
    i2 = jnp.min(jnp.where(rest == v2, ie, float(EXPERTS_PER_GROUP)), axis=0, keepdims=True)
    e2 = jnp.exp(v2 - v1)
    den = 1.0 + e2
    validf = valid.astype(_F32)
    gate0 = w_g * (1.0 / den) * validf
    gate1 = w_g * (e2 / den) * validf
    eid0 = g_sel * float(EXPERTS_PER_GROUP) + i1
    eid1 = g_sel * float(EXPERTS_PER_GROUP) + i2

    iall = lax.broadcasted_iota(jnp.int32, (N_EXPERTS, tm), 0).astype(_F32)
    oh0 = jnp.where(iall == eid0, validf, 0.0)
    oh1 = jnp.where(iall == eid1, validf, 0.0)
    both = oh0 + oh1
    before = _dot(both.astype(_BF16), triu_ref[...]) + base_ref[...]
    rank0 = jnp.sum(oh0 * before, axis=0, keepdims=True)
    rank1 = jnp.sum(oh1 * before, axis=0, keepdims=True)
    base_ref[...] = base_ref[...] + jnp.sum(both, axis=1, keepdims=True)

    ri_ref[...] = jnp.zeros(ri_ref.shape, ri_ref.dtype)
    rf_ref[...] = jnp.zeros(rf_ref.shape, rf_ref.dtype)
    ri_ref[0, 0:1, :] = eid0.astype(jnp.int32)
    ri_ref[0, 1:2, :] = eid1.astype(jnp.int32)
    ri_ref[0, 2:3, :] = rank0.astype(jnp.int32)
    ri_ref[0, 3:4, :] = rank1.astype(jnp.int32)
    rf_ref[0, 0:1, :] = gate0
    rf_ref[0, 1:2, :] = gate1
    cnt_ref[...] = jnp.broadcast_to(base_ref[...], cnt_ref.shape).astype(jnp.int32)


def _mixer_kernel(x_ref, xh_ref, mp_ref, meta_ref, anorm_ref, aw_ref, ascale_ref, fnorm_ref,
                  wrt_ref, br_ref, pm_ref, ph_ref, triu_ref,
                  h_ref, xn_ref, ri_ref, rf_ref, cnt_ref, base_ref, wsplit_ref, *, n_real_tiles, tiles_per_batch):
    i = pl.program_id(0)
    tm = x_ref.shape[0]
    gd = x_ref.shape[1] // len(POOL_WINDOWS)
    is_meta = i == n_real_tiles
    first = (i % tiles_per_batch) == 0

    @pl.when(i == 0)
    def _():
        _route_init(wrt_ref, wsplit_ref, base_ref)

    h = jnp.where(is_meta, mp_ref[...], x_ref[...])
    halo = jnp.where(is_meta, 0.0, jnp.where(first, meta_ref[...], xh_ref[...]))
    hn = _rms(h, anorm_ref[...])
    hh = _rms(halo, anorm_ref[...])
    hn_hi, hn_lo = _split_bf16(hn)
    hh_hi, hh_lo = _split_bf16(hh)
    row = lax.broadcasted_iota(jnp.int32, (tm, 1), 0)
    for g, w in enumerate(POOL_WINDOWS):
        sl = slice(g * gd, (g + 1) * gd)
        win = _dot(pm_ref[g], hn_hi[:, sl]) + _dot(pm_ref[g], hn_lo[:, sl])
        top = win[:N_META] + _dot(ph_ref[g], hh_hi[:, sl]) + _dot(ph_ref[g], hh_lo[:, sl])
        win = jnp.concatenate([top, win[N_META:]], axis=0)
        cnt = jnp.where(is_meta, jnp.minimum(row + 1, w), w).astype(_F32)
        pooled = win * (1.0 / cnt) - hn[:, sl]
        mix = _dot(pooled.astype(_BF16), aw_ref[g])
        h_ref[:, sl] = h[:, sl] + mix * ascale_ref[:, sl]

    xn = _rms(h_ref[...], fnorm_ref[...])
    _store_token_tiles(xn_ref, _pack_rows(xn))
    lane = lax.broadcasted_iota(jnp.int32, (1, tm), 1)
    valid = jnp.logical_or(jnp.logical_not(is_meta), lane < N_META)
    _route(xn, valid, wsplit_ref, br_ref, triu_ref, base_ref, ri_ref, rf_ref, cnt_ref)


def _dispatch_kernel(dest_hbm, xn_hbm, xs_hbm, idx_smem, xbuf, isem, lsem, ssem, *, tm, n_tiles, n_real_tiles):
    i = pl.program_id(0)
    parts = xbuf.shape[1] // tm
    buf = i % 3

    def idx_copy(tile):
        return pltpu.make_async_copy(dest_hbm.at[tile], idx_smem.at[pl.ds(tile % 2, 1)], isem.at[tile % 2])

    def load(tile):
        rows = tm * parts
        return pltpu.make_async_copy(xn_hbm.at[pl.ds(pl.multiple_of(tile * rows, rows), rows), :],
                                     xbuf.at[tile % 3], lsem.at[tile % 3])

    def wait_scatters(tile):
        def wait_rows(n_tok):
            for k in range(2):
                pltpu.make_async_copy(xbuf.at[tile % 3, pl.ds(0, n_tok * parts), :],
                                      xs_hbm.at[pl.ds(0, n_tok * parts), :], ssem.at[tile % 3]).wait()

        @pl.when(tile < n_real_tiles)
        def _():
            wait_rows(tm)

        @pl.when(tile >= n_real_tiles)
        def _():
            wait_rows(N_META)

    def scatter(n_tok):
        def issue(g, c):
            for u in range(min(DMA_UNROLL, n_tok)):
                t = g * min(DMA_UNROLL, n_tok) + u
                src = xbuf.at[buf, pl.ds(pl.multiple_of(t * parts, parts), parts), :]
                for k in range(2):
                    d = idx_smem[i % 2, k * tm + t]
                    pltpu.make_async_copy(src, xs_hbm.at[pl.ds(pl.multiple_of(d * parts, parts), parts), :],
                                          ssem.at[buf]).start(priority=(2 * u + k) % DMA_THREADS)
            return c

        lax.fori_loop(0, n_tok // min(DMA_UNROLL, n_tok), issue, 0)

    @pl.when(i == 0)
    def _():
        idx_copy(0).start()
        load(0).start()

    @pl.when(i >= 2)
    def _():
        wait_scatters(i - 2)

    @pl.when(i + 1 < n_tiles)
    def _():
        idx_copy(i + 1).start()
        load(i + 1).start()

    idx_copy(i).wait()
    load(i).wait()

    @pl.when(i < n_real_tiles)
    def _():
        scatter(tm)

    @pl.when(i >= n_real_tiles)
    def _():
        scatter(N_META)

    @pl.when(i == n_tiles - 1)
    def _():
        if n_tiles >= 2:
            wait_scatters(i - 1)
        wait_scatters(i)


def _expert_kernel(be_ref, bv_ref, nu_ref, xs_ref, *refs):
    nb = EXPERT_BLOCKS_PER_STEP
    w_refs, y_ref, wb_refs = refs[:3 * nb], refs[3 * nb], refs[3 * nb + 1:]
    i = pl.program_id(0)
    n_used = nu_ref[0]
    parts = wb_refs[0].shape[0] // (2 * LANES)
    tr = xs_ref.shape[0] // (nb * parts)

    for c in range(nb):
        blk = i * nb + c
        changed = be_ref[blk] != be_ref[jnp.maximum(blk - nb, 0)]

        @pl.when(jnp.logical_or(i == 0, jnp.logical_and(blk < n_used, changed)))
        def _(c=c):
            for w_ref, wb_ref in zip(w_refs[3 * c:3 * c + 3], wb_refs[3 * c:3 * c + 3]):
                wb_ref[...] = w_ref[0, 0].astype(_BF16)

    @pl.when(i * nb < n_used)
    def _():
        rows = lax.broadcasted_iota(jnp.int32, (tr, 1), 0)
        for c in range(nb):
            blk = i * nb + c
            wgb_ref, wub_ref, wdb_ref = wb_refs[3 * c:3 * c + 3]
            x = jnp.where(rows < bv_ref[blk], _unpack_rows(_load_token_tiles(xs_ref, c * tr, tr, parts)), 0.0)
            x = x.astype(_BF16)
            g = _dot(x, wgb_ref[...])
            u = _dot(x, wub_ref[...])
            a = g * (1.0 / (1.0 + jnp.exp(-g))) * u
            y = jnp.where(blk < n_used, _dot(a.astype(_BF16), wdb_ref[...]), 0.0)
            _store_token_tiles(y_ref.at[pl.ds(c * tr * parts, tr * parts), :], _pack_rows(y))

    @pl.when(i * nb >= n_used)
    def _():
        y_ref[...] = jnp.zeros(y_ref.shape, y_ref.dtype)


def _combined_tile(dest_hbm, rf_ref, h_ref, y_hbm, eye_ref, idx_smem, ybuf, isem, rsem, *, tm, n_tiles):
    i = pl.program_id(0)
    parts = ybuf.shape[1] // (2 * tm)
    slot = i % 2

    def idx_copy(tile):
        return pltpu.make_async_copy(dest_hbm.at[tile], idx_smem.at[pl.ds(tile % 2, 1)], isem.at[tile % 2])

    def row_copy(idx_slot, buf_slot, t, k, u):
        d = idx_smem[idx_slot, k * tm + t]
        return pltpu.make_async_copy(
            y_hbm.at[pl.ds(pl.multiple_of(d * parts, parts), parts), :],
            ybuf.at[buf_slot, pl.ds(pl.multiple_of((k * tm + t) * parts, parts), parts), :],
            rsem.at[buf_slot]).start(priority=(2 * u + k) % DMA_THREADS)

    def wait_rows(buf_slot):
        pltpu.make_async_copy(y_hbm.at[pl.ds(0, 2 * tm * parts), :], ybuf.at[buf_slot], rsem.at[buf_slot]).wait()

    @pl.when(i == 0)
    def _():
        idx_copy(0).start()
        idx_copy(0).wait()

        def issue(g, c):
            for u in range(DMA_UNROLL):
                for k in range(2):
                    row_copy(0, 0, g * DMA_UNROLL + u, k, u)
            return c

        lax.fori_loop(0, tm // DMA_UNROLL, issue, 0)
        if n_tiles >= 2:
            idx_copy(1).start()

    @pl.when(i + 1 < n_tiles)
    def _():
        idx_copy(i + 1).wait()

    @pl.when(i + 2 < n_tiles)
    def _():
        idx_copy(i + 2).start()

    nxt = jnp.minimum(i + 1, n_tiles - 1)
    for t in range(tm):
        for k in range(2):
            row_copy(nxt % 2, 1 - slot, t, k, t)

    gt = lax.dot_general(eye_ref[...], rf_ref[0], _NT_DIMS, precision=lax.Precision.HIGHEST,
                         preferred_element_type=_F32)
    wait_rows(slot)
    yb = ybuf.at[slot]
    out = h_ref[...] + (_unpack_rows(_load_token_tiles(yb, 0, tm, parts)) * gt[:, 0:1]
                        + _unpack_rows(_load_token_tiles(yb, tm, tm, parts)) * gt[:, 1:2])

    def drain():
        @pl.when(i == n_tiles - 1)
        def _():
            wait_rows(1 - slot)

    return out, drain


def _combine_norm_kernel(dest_hbm, rf_ref, h_ref, y_hbm, eye_ref, fnorm_ref, out_ref,
                         idx_smem, ybuf, isem, rsem, *, tm, n_tiles):
    h, drain = _combined_tile(dest_hbm, rf_ref, h_ref, y_hbm, eye_ref, idx_smem, ybuf, isem, rsem,
                              tm=tm, n_tiles=n_tiles)
    out_ref[...] = _rms(h, fnorm_ref[...])
    drain()


def _combine_proj_kernel(dest_hbm, rf_ref, h_ref, y_hbm, eye_ref, cos_ref, sin_ref, kvn_ref, wdkv_ref, kvlat_ref,
                         wuk_ref, wuv_ref, bnorm_ref, wdq_ref, qnorm_ref, wuq_ref,
                         h_out_ref, q_ref, k_ref, v_ref, idx_smem, ybuf, isem, rsem, *, tm, n_tiles):
    h, drain = _combined_tile(dest_hbm, rf_ref, h_ref, y_hbm, eye_ref, idx_smem, ybuf, isem, rsem,
                              tm=tm, n_tiles=n_tiles)
    h_out_ref[...] = h
    _project(h, cos_ref, sin_ref, kvn_ref, wdkv_ref, kvlat_ref, wuk_ref, wuv_ref,
             bnorm_ref, wdq_ref, qnorm_ref, wuq_ref, q_ref, k_ref, v_ref)
    drain()


def _rope128(x, cos_t, sin_t):
    lane = lax.broadcasted_iota(jnp.int32, (1, 128), 1)
    first_half = (lane % QK_ROPE_DIM) < (QK_ROPE_DIM // 2)
    swapped = jnp.where(first_half, pltpu.roll(x, 128 - QK_ROPE_DIM // 2, axis=1),
                        pltpu.roll(x, QK_ROPE_DIM // 2, axis=1))
    return x * cos_t + swapped * sin_t


def _project(h, cos_ref, sin_ref, kvn_ref, wdkv_ref, kvlat_ref, wuk_ref, wuv_ref,
             bnorm_ref, wdq_ref, qnorm_ref, wuq_ref, q_ref, k_ref, v_ref):
    cos_t = cos_ref[...]
    sin_t = sin_ref[...]
    c = _dot(_rms(h, kvn_ref[...]).astype(_BF16), wdkv_ref[...])
    ckv = _rms(c[:, :KV_LORA_RANK], kvlat_ref[...]).astype(_BF16)
    kr = _rope128(c[:, KV_LORA_RANK:KV_LORA_RANK + 128], cos_t, sin_t)[:, :QK_ROPE_DIM].astype(_BF16)
    kn = _dot(ckv, wuk_ref[...])
    vv = _dot(ckv, wuv_ref[...])
    cq = _rms(_dot(_rms(h, bnorm_ref[...]).astype(_BF16), wdq_ref[...]), qnorm_ref[...]).astype(_BF16)
    q = _dot(cq, wuq_ref[...])
    rope0 = N_HEADS * QK_NOPE_DIM
    for hd in range(N_HEADS):
        k_ref[hd, :, 0:QK_NOPE_DIM] = kn[:, hd * QK_NOPE_DIM:(hd + 1) * QK_NOPE_DIM].astype(_BF16)
        k_ref[hd, :, QK_NOPE_DIM:QK_DIM] = kr
        v_ref[hd, 0] = vv[:, hd * V_HEAD_DIM:(hd + 1) * V_HEAD_DIM].T.astype(_BF16)
        q_ref[hd, :, 0:QK_NOPE_DIM] = (q[:, hd * QK_NOPE_DIM:(hd + 1) * QK_NOPE_DIM] * Q_SCALE).astype(_BF16)
        qr = _rope128(q[:, rope0 + hd * 128:rope0 + (hd + 1) * 128], cos_t, sin_t)[:, :QK_ROPE_DIM]
        q_ref[hd, :, QK_NOPE_DIM:QK_DIM] = (qr * Q_SCALE).astype(_BF16)


def _attn_kernel(q_ref, k_ref, vt_ref, km_ref, vmt_ref, o_ref, sa_ref, sb_ref, acc_ref, *, tq, tk):
    j = pl.program_id(2)
    heads = range(q_ref.shape[0])
    vd = acc_ref.shape[1]

    def scores(kb, s_ref):
        start = pl.multiple_of(kb * tk, tk)
        for hd in heads:
            s_ref[hd] = lax.dot_general(k_ref[hd, pl.ds(start, tk), :], q_ref[hd], _NT_DIMS,
                                        preferred_element_type=_F32)

    def update(kb, s_ref, carry, masked):
        out = []
        for hd in heads:
            m, l = carry[hd]
            s = s_ref[hd]
            if masked:
                keys = kb * tk + lax.broadcasted_iota(jnp.int32, (tk, tq), 0)
                qpos = j * tq + lax.broadcasted_iota(jnp.int32, (tk, tq), 1)
                s = jnp.where(keys <= qpos, s, NEG_INF)
            m_new = jnp.maximum(m, jnp.max(s, axis=0, keepdims=True))
            alpha = jnp.exp2(m - m_new)
            p = jnp.exp2(s - m_new)
            l = alpha * l + jnp.sum(p, axis=0, keepdims=True)
            acc_ref[hd] = acc_ref[hd] * alpha + _dot(vt_ref[hd, kb], p.astype(_BF16))
            out.append((m_new, l))
        return tuple(out)

    scores(0, sa_ref)
    carry = []
    for hd in heads:
        s0 = lax.dot_general(km_ref[hd], q_ref[hd], _NT_DIMS, preferred_element_type=_F32)
        m = jnp.max(s0, axis=0, keepdims=True)
        p0 = jnp.exp2(s0 - m)
        carry.append((m, jnp.sum(p0, axis=0, keepdims=True)))
        acc_ref[hd] = _dot(vmt_ref[hd, 0, :, 0:N_META], p0.astype(_BF16))
    carry = tuple(carry)

    def pair(kp, carry):
        scores(2 * kp + 1, sb_ref)
        carry = update(2 * kp, sa_ref, carry, False)
        scores(2 * kp + 2, sa_ref)
        return update(2 * kp + 1, sb_ref, carry, False)

    carry = lax.fori_loop(0, j, pair, carry)
    scores(2 * j + 1, sb_ref)
    carry = update(2 * j, sa_ref, carry, True)
    carry = update(2 * j + 1, sb_ref, carry, True)
    for hd in heads:
        o_ref[:, hd * vd:(hd + 1) * vd] = (acc_ref[hd] / carry[hd][1]).T.astype(o_ref.dtype)


def _oproj_kernel(o_ref, h_ref, wo_ref, fnorm_ref, wrt_ref, br_ref, triu_ref,
                  h_out_ref, xn_ref, ri_ref, rf_ref, cnt_ref, base_ref, wsplit_ref):
    i = pl.program_id(0)

    @pl.when(i == 0)
    def _():
        _route_init(wrt_ref, wsplit_ref, base_ref)

    h = h_ref[...] + _dot(o_ref[...], wo_ref[...])
    h_out_ref[...] = h
    xn = _rms(h, fnorm_ref[...])
    _store_token_tiles(xn_ref, _pack_rows(xn))
    valid = lax.broadcasted_iota(jnp.int32, (1, h.shape[0]), 1) >= 0
    _route(xn, valid, wsplit_ref, br_ref, triu_ref, base_ref, ri_ref, rf_ref, cnt_ref)


def _full(shape):
    nd = len(shape)
    return pl.BlockSpec(shape, lambda *_: (0,) * nd)


def _router_operands(router_g, router_g_bias, router_e, router_e_bias):
    d = router_g.shape[0]
    wrt = jnp.concatenate([router_g.T, jnp.zeros((8 - N_EXPERT_GROUPS, d), _F32), router_e.T], axis=0)
    br = jnp.concatenate([router_g_bias, jnp.zeros((8 - N_EXPERT_GROUPS,), _F32), router_e_bias])[:, None]
    return wrt.astype(_F32), br.astype(_F32)


def _route_out(n_tiles, tm):
    shapes = [jax.ShapeDtypeStruct((n_tiles, 8, tm), jnp.int32),
              jax.ShapeDtypeStruct((n_tiles, 8, tm), _F32),
              jax.ShapeDtypeStruct((N_EXPERTS, 128), jnp.int32)]
    specs = [pl.BlockSpec((1, 8, tm), lambda i: (i, 0, 0)),
             pl.BlockSpec((1, 8, tm), lambda i: (i, 0, 0)),
             pl.BlockSpec((N_EXPERTS, 128), lambda i: (0, 0))]
    return shapes, specs


def _moe_experts(xn, ri, counts, w_gate, w_up, w_down, layer, n_tiles, n_real_tiles, d):
    tm, tr = TOKEN_TILE, EXPERT_ROWS
    parts = d // (2 * LANES)
    n_valid = n_real_tiles * tm + (n_tiles - n_real_tiles) * N_META
    n_blocks = -(-(2 * n_valid + N_EXPERTS * (tr - 1)) // tr)
    n_blocks = -(-n_blocks // EXPERT_BLOCKS_PER_STEP) * EXPERT_BLOCKS_PER_STEP
    n_rows = n_blocks * tr

    counts = counts[:, 0]
    padded = (counts + tr - 1) // tr * tr
    pends = jnp.cumsum(padded)
    pstarts = pends - padded
    n_used = (pends[-1] // tr).astype(jnp.int32).reshape(1)
    blk0 = jnp.arange(n_blocks, dtype=jnp.int32) * tr
    block_e = jnp.minimum(jnp.sum(blk0[:, None] >= pends[None, :], axis=1), N_EXPERTS - 1).astype(jnp.int32)
    experts = jnp.arange(N_EXPERTS, dtype=jnp.int32)
    block_oh = block_e[:, None] == experts[None, :]
    block_cnt = jnp.sum(jnp.where(block_oh, counts[None, :], 0), axis=1)
    block_start = jnp.sum(jnp.where(block_oh, pstarts[None, :], 0), axis=1)
    block_valid = jnp.clip(block_cnt - (blk0 - block_start), 0, tr).astype(jnp.int32)
    eid = ri[:, 0:2, :]
    slot0 = jnp.sum(jnp.where(eid[..., None] == experts, pstarts, 0), axis=-1)
    dest = (slot0 + ri[:, 2:4, :]).astype(jnp.int32).reshape(n_tiles, 1, 2 * tm)

    xs = pl.pallas_call(
        functools.partial(_dispatch_kernel, tm=tm, n_tiles=n_tiles, n_real_tiles=n_real_tiles),
        grid=(n_tiles,),
        in_specs=[pl.BlockSpec(memory_space=pl.ANY), pl.BlockSpec(memory_space=pl.ANY)],
        out_specs=pl.BlockSpec(memory_space=pl.ANY),
        out_shape=jax.ShapeDtypeStruct((n_rows * parts, LANES), jnp.uint32),
        scratch_shapes=[pltpu.SMEM((2, 2 * tm), jnp.int32), pltpu.VMEM((3, tm * parts, LANES), jnp.uint32),
                        pltpu.SemaphoreType.DMA((2,)), pltpu.SemaphoreType.DMA((3,)),
                        pltpu.SemaphoreType.DMA((3,))],
        compiler_params=_params(1, has_side_effects=True, disable_bounds_checks=True),
        name="moe_dispatch",
    )(dest, xn)

    f = w_gate.shape[3]
    nb = EXPERT_BLOCKS_PER_STEP
    step_rows = nb * tr * parts

    def expert_block(c):
        return lambda i, be, bv, nu: (layer, be[jnp.minimum(i * nb + c, nu[0] - 1)], 0, 0)

    weight_specs, weight_scratch = [], []
    for c in range(nb):
        weight_specs += [pl.BlockSpec((1, 1, d, f), expert_block(c)), pl.BlockSpec((1, 1, d, f), expert_block(c)),
                         pl.BlockSpec((1, 1, f, d), expert_block(c))]
        weight_scratch += [pltpu.VMEM((d, f), _BF16), pltpu.VMEM((d, f), _BF16), pltpu.VMEM((f, d), _BF16)]
    y = pl.pallas_call(
        _expert_kernel,
        grid_spec=pltpu.PrefetchScalarGridSpec(
            num_scalar_prefetch=3,
            grid=(n_blocks // nb,),
            in_specs=[pl.BlockSpec((step_rows, LANES),
                                   lambda i, be, bv, nu: (jnp.minimum(i, (nu[0] - 1) // nb), 0))] + weight_specs,
            out_specs=pl.BlockSpec((step_rows, LANES), lambda i, be, bv, nu: (i, 0)),
            scratch_shapes=weight_scratch),
        out_shape=jax.ShapeDtypeStruct((n_rows * parts, LANES), jnp.uint32),
        compiler_params=_params(1),
        name="moe_experts",
    )(block_e, block_valid, n_used, xs, *([w_gate, w_up, w_down] * nb))

    return dest, y


def _combine_operands(dest, rf, h, y, d):
    tm = TOKEN_TILE
    parts = d // (2 * LANES)
    eye = jnp.asarray(np.eye(tm, dtype=np.float32))
    in_specs = [pl.BlockSpec(memory_space=pl.ANY),
                pl.BlockSpec((1, 8, tm), lambda i: (i, 0, 0)),
                pl.BlockSpec((tm, d), lambda i: (i, 0)),
                pl.BlockSpec(memory_space=pl.ANY),
                _full((tm, tm))]
    scratch = [pltpu.SMEM((2, 2 * tm), jnp.int32), pltpu.VMEM((2, 2 * tm * parts, LANES), jnp.uint32),
               pltpu.SemaphoreType.DMA((2,)), pltpu.SemaphoreType.DMA((2,))]
    return (dest, rf, h, y, eye), in_specs, scratch


def kernel(x, meta_tokens, a_norm, a_w, a_scale, b_norm, b_w_dq, b_q_norm, b_w_uq, b_w_o, kv_norm, w_dkv,
           kv_lat_norm, w_uk, w_uv, ffn_norm, router_g, router_g_bias, router_e, router_e_bias, w_gate, w_up,
           w_down, final_norm):
    bsz, seq, d = x.shape
    tm = TOKEN_TILE
    assert seq % tm == 0 and seq % ATTN_Q_TILE == 0 and ATTN_Q_TILE == 2 * ATTN_K_TILE and ATTN_K_TILE == tm
    assert d % (LANES * len(POOL_WINDOWS)) == 0 and N_META == max(POOL_WINDOWS) and N_META <= tm
    parts = d // (2 * LANES)
    n_tok = bsz * seq
    n_real_tiles = n_tok // tm
    tiles_per_batch = seq // tm
    n_tiles = n_real_tiles + 1
    gd = d // len(POOL_WINDOWS)
    row = lambda v: v.reshape(1, -1).astype(_F32)

    x2 = x.reshape(n_tok, d)
    meta_pad = jnp.concatenate([meta_tokens, jnp.zeros((tm - N_META, d), x.dtype)], axis=0)

    r = np.arange(tm)[:, None]
    cidx = np.arange(tm)[None, :]
    pm = np.stack([((r - cidx >= 0) & (r - cidx < w)) for w in POOL_WINDOWS]).astype(np.float32)
    hc = np.arange(N_META)[None, :]
    ph = np.stack([(r[:N_META] + N_META - hc < w) for w in POOL_WINDOWS]).astype(np.float32)
    triu = (r < cidx).astype(np.float32)
    pm, ph, triu = (jnp.asarray(a, dtype=_BF16) for a in (pm, ph, triu))

    wrt0, br0 = _router_operands(router_g[0], router_g_bias[0], router_e[0], router_e_bias[0])
    route_shapes, route_specs = _route_out(n_tiles, tm)
    tile_or_last = lambda i: (jnp.minimum(i, n_real_tiles - 1), 0)
    halo_blocks = tm // N_META
    h1, xn1, ri1, rf1, cnt1 = pl.pallas_call(
        functools.partial(_mixer_kernel, n_real_tiles=n_real_tiles, tiles_per_batch=tiles_per_batch),
        grid=(n_tiles,),
        in_specs=[pl.BlockSpec((tm, d), tile_or_last),
                  pl.BlockSpec((N_META, d), lambda i: (jnp.clip(i * halo_blocks - 1, 0, n_tok // N_META - 1), 0)),
                  _full((tm, d)), _full((N_META, d)), _full((1, d)),
                  _full((len(POOL_WINDOWS), gd, gd)), _full((1, d)), _full((1, d)),
                  _full((ROUTER_ROWS, d)), _full((ROUTER_ROWS, 1)),
                  _full(pm.shape), _full(ph.shape), _full((tm, tm))],
        out_specs=[pl.BlockSpec((tm, d), lambda i: (i, 0)),
                   pl.BlockSpec((tm * parts, LANES), lambda i: (i, 0))] + route_specs,
        out_shape=[jax.ShapeDtypeStruct((n_tiles * tm, d), _F32),
                   jax.ShapeDtypeStruct((n_tiles * tm * parts, LANES), jnp.uint32)] + route_shapes,
        scratch_shapes=[pltpu.VMEM((N_EXPERTS, 1), _F32), pltpu.VMEM((ROUTER_SPLIT_ROWS, d), _BF16)],
        compiler_params=_params(1),
        name="pool_mixer_router",
    )(x2, x2, meta_pad, meta_tokens, row(a_norm[0]), a_w[0].astype(_BF16), row(a_scale[0]), row(ffn_norm[0]),
      wrt0, br0, pm, ph, triu)

    dest1, y1 = _moe_experts(xn1, ri1, cnt1, w_gate, w_up, w_down, 0, n_tiles, n_real_tiles, d)

    pos = jnp.concatenate([jnp.arange(seq, dtype=_F32) + N_META, jnp.arange(tm, dtype=_F32)])
    inv_freq = ROPE_THETA ** (-jnp.arange(0, QK_ROPE_DIM, 2, dtype=_F32) / QK_ROPE_DIM)
    ang = pos[:, None] * inv_freq[None, :]
    cos_t = jnp.tile(jnp.cos(ang), (1, 4))
    sin_t = jnp.tile(jnp.concatenate([-jnp.sin(ang), jnp.sin(ang)], axis=1), (1, 2))

    wdkv = jnp.concatenate([w_dkv, w_dkv[:, KV_LORA_RANK:]], axis=1).astype(_BF16)
    wuk = w_uk.reshape(KV_LORA_RANK, N_HEADS * QK_NOPE_DIM).astype(_BF16)
    wuv = w_uv.reshape(KV_LORA_RANK, N_HEADS * V_HEAD_DIM).astype(_BF16)
    wuq = b_w_uq[0]
    q_rank = wuq.shape[0]
    wuq_rope = wuq[:, :, QK_NOPE_DIM:]
    wuq = jnp.concatenate([wuq[:, :, :QK_NOPE_DIM].reshape(q_rank, -1),
                           jnp.concatenate([wuq_rope, wuq_rope], axis=2).reshape(q_rank, -1)], axis=1).astype(_BF16)
    wdq = b_w_dq[0].astype(_BF16)
    pos_tile = lambda i: (jnp.where(i < n_real_tiles, i % tiles_per_batch, tiles_per_batch), 0)
    head_tile = lambda i: (0, i, 0)
    operands, in_specs, scratch = _combine_operands(dest1, rf1, h1, y1, d)
    h2, q, k, v = pl.pallas_call(
        functools.partial(_combine_proj_kernel, tm=tm, n_tiles=n_tiles),
        grid=(n_tiles,),
        in_specs=in_specs + [
            pl.BlockSpec((tm, 128), pos_tile), pl.BlockSpec((tm, 128), pos_tile),
            _full((1, d)), _full(wdkv.shape), _full((1, KV_LORA_RANK)), _full(wuk.shape), _full(wuv.shape),
            _full((1, d)), _full(wdq.shape), _full((1, q_rank)), _full(wuq.shape)],
        out_specs=[pl.BlockSpec((tm, d), lambda i: (i, 0)),
                   pl.BlockSpec((N_HEADS, tm, QK_DIM), head_tile),
                   pl.BlockSpec((N_HEADS, tm, QK_DIM), head_tile),
                   pl.BlockSpec((N_HEADS, 1, V_HEAD_DIM, tm), lambda i: (0, i, 0, 0))],
        out_shape=[jax.ShapeDtypeStruct((n_tiles * tm, d), _F32),
                   jax.ShapeDtypeStruct((N_HEADS, n_tiles * tm, QK_DIM), _BF16),
                   jax.ShapeDtypeStruct((N_HEADS, n_tiles * tm, QK_DIM), _BF16),
                   jax.ShapeDtypeStruct((N_HEADS, n_tiles, V_HEAD_DIM, tm), _BF16)],
        scratch_shapes=scratch,
        compiler_params=_params(1, disable_bounds_checks=True),
        name="moe_combine_latent_qkv",
    )(*operands, cos_t, sin_t, row(kv_norm), wdkv, row(kv_lat_norm), wuk, wuv,
      row(b_norm[0]), wdq, row(b_q_norm[0]), wuq)

    tq, tk, hp = ATTN_Q_TILE, ATTN_K_TILE, ATTN_HEADS_PER_STEP
    n_q = seq // tq
    meta_block = n_tok // N_META
    o = pl.pallas_call(
        functools.partial(_attn_kernel, tq=tq, tk=tk),
        grid=(bsz, N_HEADS // hp, n_q),
        in_specs=[pl.BlockSpec((hp, tq, QK_DIM), lambda b, hg, j: (hg, b * n_q + j, 0)),
                  pl.BlockSpec((hp, seq, QK_DIM), lambda b, hg, j: (hg, b, 0)),
                  pl.BlockSpec((hp, tiles_per_batch, V_HEAD_DIM, tm), lambda b, hg, j: (hg, b, 0, 0)),
                  pl.BlockSpec((hp, N_META, QK_DIM), lambda b, hg, j: (hg, meta_block, 0)),
                  pl.BlockSpec((hp, 1, V_HEAD_DIM, tm), lambda b, hg, j: (hg, n_real_tiles, 0, 0))],
        out_specs=pl.BlockSpec((tq, hp * V_HEAD_DIM), lambda b, hg, j: (b * n_q + j, hg)),
        out_shape=jax.ShapeDtypeStruct((n_tok, N_HEADS * V_HEAD_DIM), _BF16),
        scratch_shapes=[pltpu.VMEM((hp, tk, tq), _F32), pltpu.VMEM((hp, tk, tq), _F32),
                        pltpu.VMEM((hp, V_HEAD_DIM, tq), _F32)],
        compiler_params=_params(3),
        name="causal_attention",
    )(q, k, v, k, v)

    wrt1, br1 = _router_operands(router_g[1], router_g_bias[1], router_e[1], router_e_bias[1])
    route_shapes, route_specs = _route_out(n_real_tiles, tm)
    h3, xn2, ri2, rf2, cnt2 = pl.pallas_call(
        _oproj_kernel,
        grid=(n_real_tiles,),
        in_specs=[pl.BlockSpec((tm, N_HEADS * V_HEAD_DIM), lambda i: (i, 0)),
                  pl.BlockSpec((tm, d), lambda i: (i, 0)),
                  _full((N_HEADS * V_HEAD_DIM, d)), _full((1, d)),
                  _full((ROUTER_ROWS, d)), _full((ROUTER_ROWS, 1)), _full((tm, tm))],
        out_specs=[pl.BlockSpec((tm, d), lambda i: (i, 0)),
                   pl.BlockSpec((tm * parts, LANES), lambda i: (i, 0))] + route_specs,
        out_shape=[jax.ShapeDtypeStruct((n_tok, d), _F32),
                   jax.ShapeDtypeStruct((n_tok * parts, LANES), jnp.uint32)] + route_shapes,
        scratch_shapes=[pltpu.VMEM((N_EXPERTS, 1), _F32), pltpu.VMEM((ROUTER_SPLIT_ROWS, d), _BF16)],
        compiler_params=_params(1),
        name="attn_out_router",
    )(o, h2, b_w_o[0].astype(_BF16), row(ffn_norm[1]), wrt1, br1, triu)

    dest2, y2 = _moe_experts(xn2, ri2, cnt2, w_gate, w_up, w_down, 1, n_real_tiles, n_real_tiles, d)
    operands, in_specs, scratch = _combine_operands(dest2, rf2, h3, y2, d)
    out = pl.pallas_call(
        functools.partial(_combine_norm_kernel, tm=tm, n_tiles=n_real_tiles),
        grid=(n_real_tiles,),
        in_specs=in_specs + [_full((1, d))],
        out_specs=pl.BlockSpec((tm, d), lambda i: (i, 0)),
        out_shape=jax.ShapeDtypeStruct((n_tok, d), _F32),
        scratch_shapes=scratch,
        compiler_params=_params(1, disable_bounds_checks=True),
        name="moe_combine_final_norm",
    )(*operands, row(final_norm))
    return out.reshape(bsz, seq, d)
```

```python
import functools

import numpy as np
import jax
import jax.numpy as jnp
from jax import lax
from jax.experimental import pallas as pl
from jax.experimental.pallas import tpu as pltpu

N_META = 16
POOL_WINDOWS = (2, 4, 8, 16)
N_HEADS = 8
QK_NOPE_DIM = 128
QK_ROPE_DIM = 64
QK_DIM = QK_NOPE_DIM + QK_ROPE_DIM
V_HEAD_DIM = 128
KV_LORA_RANK = 256
ROPE_THETA = 10000.0
ATTN_SCALE = QK_DIM ** -0.5
Q_SCALE = ATTN_SCALE * 1.4426950408889634
N_EXPERT_GROUPS = 4
EXPERTS_PER_GROUP = 8
N_EXPERTS = N_EXPERT_GROUPS * EXPERTS_PER_GROUP
RMS_EPS = 1e-6
NEG_INF = -1e30

TOKEN_TILE = 512
POOL_BLOCK = 256
EXPERT_ROWS = 256
EXPERT_BLOCKS_PER_STEP = 2
ATTN_Q_TILE = 512
ATTN_K_TILE = ATTN_Q_TILE // 2
ATTN_HEADS_PER_STEP = 4
DMA_UNROLL = 8
DMA_THREADS = 2
ROUTER_ROWS = 8 + N_EXPERTS
ROUTER_LO_ROW = 48
ROUTER_SPLIT_ROWS = 2 * ROUTER_LO_ROW
VMEM_LIMIT_BYTES = 48 * 1024 * 1024

LANES = 128
_F32 = jnp.float32
_BF16 = jnp.bfloat16
_NT_DIMS = (((1,), (1,)), ((), ()))


def _params(n_grid_dims=1, **kw):
    return pltpu.CompilerParams(dimension_semantics=("arbitrary",) * n_grid_dims,
                                vmem_limit_bytes=VMEM_LIMIT_BYTES, **kw)


def _rms(x, g):
    ms = jnp.mean(x * x, axis=-1, keepdims=True)
    return x * lax.rsqrt(ms + RMS_EPS) * g


def _split_bf16(x):
    hi = x.astype(_BF16)
    lo = (x - hi.astype(_F32)).astype(_BF16)
    return hi, lo


def _dot(a, b):
    return jnp.dot(a, b, preferred_element_type=_F32)


def _store_token_tiles(ref, x):
    n, d = x.shape
    parts = d // LANES
    for s in range(parts):
        ref[pl.ds(s, n, stride=parts), :] = x[:, s * LANES:(s + 1) * LANES]


def _load_token_tiles(ref, first_token, n, parts):
    return jnp.concatenate([ref[pl.ds(first_token * parts + s, n, stride=parts), :] for s in range(parts)], axis=1)


_HIGH_HALF = 0xFFFF0000


def _pack_rows(x):
    n, d = x.shape
    words = []
    for s in range(d // (2 * LANES)):
        lo = x[:, 2 * s * LANES:(2 * s + 1) * LANES].astype(_BF16).astype(_F32)
        hi = x[:, (2 * s + 1) * LANES:(2 * s + 2) * LANES].astype(_BF16).astype(_F32)
        words.append(lax.shift_right_logical(lax.bitcast_convert_type(lo, jnp.uint32), jnp.uint32(16))
                     | (lax.bitcast_convert_type(hi, jnp.uint32) & jnp.uint32(_HIGH_HALF)))
    return jnp.concatenate(words, axis=1)


def _unpack_rows(w):
    cols = []
    for s in range(w.shape[1] // LANES):
        ws = w[:, s * LANES:(s + 1) * LANES]
        cols.append(lax.bitcast_convert_type(lax.shift_left(ws, jnp.uint32(16)), _F32))
        cols.append(lax.bitcast_convert_type(ws & jnp.uint32(_HIGH_HALF), _F32))
    return jnp.concatenate(cols, axis=1)


def _route_init(wrt_ref, wsplit_ref, base_ref):
    base_ref[...] = jnp.zeros(base_ref.shape, base_ref.dtype)
    w_hi, w_lo = _split_bf16(wrt_ref[...])
    wsplit_ref[...] = jnp.zeros(wsplit_ref.shape, wsplit_ref.dtype)
    wsplit_ref[0:ROUTER_ROWS, :] = w_hi
    wsplit_ref[ROUTER_LO_ROW:ROUTER_LO_ROW + ROUTER_ROWS, :] = w_lo


def _route(xn, valid, wsplit_ref, br_ref, triu_ref, base_ref, ri_ref, rf_ref, cnt_ref):
    tm = xn.shape[0]
    x_hi, x_lo = _split_bf16(xn)
    by_hi = lax.dot_general(wsplit_ref[...], x_hi, _NT_DIMS, preferred_element_type=_F32)
    by_lo = lax.dot_general(wsplit_ref[0:ROUTER_LO_ROW, :], x_lo, _NT_DIMS, preferred_element_type=_F32)
    logits = (by_hi[0:ROUTER_ROWS] + by_hi[ROUTER_LO_ROW:ROUTER_LO_ROW + ROUTER_ROWS]
              + by_lo[0:ROUTER_ROWS] + br_ref[...])
    lg = logits[0:N_EXPERT_GROUPS]
    eg = jnp.exp(lg - jnp.max(lg, axis=0, keepdims=True))
    pg = eg / jnp.sum(eg, axis=0, keepdims=True)
    w_g = jnp.max(pg, axis=0, keepdims=True)
    ig = lax.broadcasted_iota(jnp.int32, pg.shape, 0).astype(_F32)
    g_sel = jnp.min(jnp.where(pg == w_g, ig, float(N_EXPERT_GROUPS)), axis=0, keepdims=True)

    sel = logits[8:8 + EXPERTS_PER_GROUP]
    for g in range(1, N_EXPERT_GROUPS):
        sel = jnp.where(g_sel == float(g), logits[8 + g * EXPERTS_PER_GROUP:8 + (g + 1) * EXPERTS_PER_GROUP], sel)
    ie = lax.broadcasted_iota(jnp.int32, sel.shape, 0).astype(_F32)
    v1 = jnp.max(sel, axis=0, keepdims=True)
    i1 = jnp.min(jnp.where(sel == v1, ie, float(EXPERTS_PER_GROUP)), axis=0, keepdims=True)
    rest = jnp.where(ie == i1, -jnp.inf, sel)
    v2 = jnp.max(rest, axis=0, keepdims=True)
    i2 = jnp.min(jnp.where(rest == v2, ie, float(EXPERTS_PER_GROUP)), axis=0, keepdims=True)
    e2 = jnp.exp(v2 - v1)
    den = 1.0 + e2
    validf = valid.astype(_F32)
    gate0 = w_g * (1.0 / den) * validf
    gate1 = w_g * (e2 / den) * validf
    eid0 = g_sel * float(EXPERTS_PER_GROUP) + i1
    eid1 = g_sel * float(EXPERTS_PER_GROUP) + i2

    iall = lax.broadcasted_iota(jnp.int32, (N_EXPERTS, tm), 0).astype(_F32)
    oh0 = jnp.where(iall == eid0, validf, 0.0)
    oh1 = jnp.where(iall == eid1, validf, 0.0)
    both = oh0 + oh1
    before = _dot(both.astype(_BF16), triu_ref[...]) + base_ref[...]
    rank0 = jnp.sum(oh0 * before, axis=0, keepdims=True)
    rank1 = jnp.sum(oh1 * before, axis=0, keepdims=True)
    base_ref[...] = base_ref[...] + jnp.sum(both, axis=1, keepdims=True)

    ri_ref[...] = jnp.zeros(ri_ref.shape, ri_ref.dtype)
    rf_ref[...] = jnp.zeros(rf_ref.shape, rf_ref.dtype)
    ri_ref[0, 0:1, :] = eid0.astype(jnp.int32)
    ri_ref[0, 1:2, :] = eid1.astype(jnp.int32)
    ri_ref[0, 2:3, :] = rank0.astype(jnp.int32)
    ri_ref[0, 3:4, :] = rank1.astype(jnp.int32)
    rf_ref[0, 0:1, :] = gate0
    rf_ref[0, 1:2, :] = gate1
    cnt_ref[...] = jnp.broadcast_to(base_ref[...], cnt_ref.shape).astype(jnp.int32)


def _mixer_kernel(x_ref, xh_ref, mp_ref, meta_ref, anorm_ref, aw_ref, ascale_ref, fnorm_ref,
                  wrt_ref, br_ref, pm_ref, ph_ref, triu_ref,
                  h_ref, xn_ref, ri_ref, rf_ref, cnt_ref, base_ref, wsplit_ref, *, n_real_tiles, tiles_per_batch):
    i = pl.program_id(0)
    tm = x_ref.shape[0]
    gd = x_ref.shape[1] // len(POOL_WINDOWS)
    is_meta = i == n_real_tiles
    first = (i % tiles_per_batch) == 0

    @pl.when(i == 0)
    def _():
        _route_init(wrt_ref, wsplit_ref, base_ref)

    h = jnp.where(is_meta, mp_ref[...], x_ref[...])
    halo = jnp.where(is_meta, 0.0, jnp.where(first, meta_ref[...], xh_ref[...]))
    hn = _rms(h, anorm_ref[...])
    hh = _rms(halo, anorm_ref[...])
    hn_hi, hn_lo = _split_bf16(hn)
    hh_hi, hh_lo = _split_bf16(hh)
    pb = pm_ref.shape[1]
    for c in range(tm // pb):
        rows = slice(c * pb, (c + 1) * pb)
        row = c * pb + lax.broadcasted_iota(jnp.int32, (pb, 1), 0)
        for g, w in enumerate(POOL_WINDOWS):
            sl = slice(g * gd, (g + 1) * gd)
            win = _dot(pm_ref[g], hn_hi[rows, sl]) + _dot(pm_ref[g], hn_lo[rows, sl])
            if c == 0:
                before_hi, before_lo = hh_hi[:, sl], hh_lo[:, sl]
            else:
                before_hi, before_lo = hn_hi[c * pb - N_META:c * pb, sl], hn_lo[c * pb - N_META:c * pb, sl]
            top = win[:N_META] + _dot(ph_ref[g], before_hi) + _dot(ph_ref[g], before_lo)
            win = jnp.concatenate([top, win[N_META:]], axis=0)
            cnt = jnp.where(is_meta, jnp.minimum(row + 1, w), w).astype(_F32)
            pooled = win * (1.0 / cnt) - hn[rows, sl]
            mix = _dot(pooled.astype(_BF16), aw_ref[g])
            h_ref[rows, sl] = h[rows, sl] + mix * ascale_ref[:, sl]

    xn = _rms(h_ref[...], fnorm_ref[...])
    _store_token_tiles(xn_ref, _pack_rows(xn))
    lane = lax.broadcasted_iota(jnp.int32, (1, tm), 1)
    valid = jnp.logical_or(jnp.logical_not(is_meta), lane < N_META)
    _route(xn, valid, wsplit_ref, br_ref, triu_ref, base_ref, ri_ref, rf_ref, cnt_ref)


def _dispatch_kernel(dest_hbm, xn_hbm, xs_hbm, idx_smem, xbuf, isem, lsem, ssem, *, tm, n_tiles, n_real_tiles):
    i = pl.program_id(0)
    parts = xbuf.shape[1] // tm
    buf = i % 3

    def idx_copy(tile):
        return pltpu.make_async_copy(dest_hbm.at[tile], idx_smem.at[pl.ds(tile % 2, 1)], isem.at[tile % 2])

    def load(tile):
        rows = tm * parts
        return pltpu.make_async_copy(xn_hbm.at[pl.ds(pl.multiple_of(tile * rows, rows), rows), :],
                                     xbuf.at[tile % 3], lsem.at[tile % 3])

    def wait_scatters(tile):
        def wait_rows(n_tok):
            for k in range(2):
                pltpu.make_async_copy(xbuf.at[tile % 3, pl.ds(0, n_tok * parts), :],
                                      xs_hbm.at[pl.ds(0, n_tok * parts), :], ssem.at[tile % 3]).wait()

        @pl.when(tile < n_real_tiles)
        def _():
            wait_rows(tm)

        @pl.when(tile >= n_real_tiles)
        def _():
            wait_rows(N_META)

    def scatter(n_tok):
        def issue(g, c):
            for u in range(min(DMA_UNROLL, n_tok)):
                t = g * min(DMA_UNROLL, n_tok) + u
                src = xbuf.at[buf, pl.ds(pl.multiple_of(t * parts, parts), parts), :]
                for k in range(2):
                    d = idx_smem[i % 2, k * tm + t]
                    pltpu.make_async_copy(src, xs_hbm.at[pl.ds(pl.multiple_of(d * parts, parts), parts), :],
                                          ssem.at[buf]).start(priority=(2 * u + k) % DMA_THREADS)
            return c

        lax.fori_loop(0, n_tok // min(DMA_UNROLL, n_tok), issue, 0)

    @pl.when(i == 0)
    def _():
        idx_copy(0).start()
        load(0).start()

    @pl.when(i >= 2)
    def _():
        wait_scatters(i - 2)

    @pl.when(i + 1 < n_tiles)
    def _():
        idx_copy(i + 1).start()
        load(i + 1).start()

    idx_copy(i).wait()
    load(i).wait()

    @pl.when(i < n_real_tiles)
    def _():
        scatter(tm)

    @pl.when(i >= n_real_tiles)
    def _():
        scatter(N_META)

    @pl.when(i == n_tiles - 1)
    def _():
        if n_tiles >= 2:
            wait_scatters(i - 1)
        wait_scatters(i)


def _expert_kernel(be_ref, bv_ref, nu_ref, xs_ref, *refs):
    nb = EXPERT_BLOCKS_PER_STEP
    w_refs, y_ref, wb_refs = refs[:3 * nb], refs[3 * nb], refs[3 * nb + 1:]
    i = pl.program_id(0)
    n_used = nu_ref[0]
    parts = wb_refs[0].shape[0] // (2 * LANES)
    tr = xs_ref.shape[0] // (nb * parts)

    for c in range(nb):
        blk = i * nb + c
        changed = be_ref[blk] != be_ref[jnp.maximum(blk - nb, 0)]

        @pl.when(jnp.logical_or(i == 0, jnp.logical_and(blk < n_used, changed)))
        def _(c=c):
            for w_ref, wb_ref in zip(w_refs[3 * c:3 * c + 3], wb_refs[3 * c:3 * c + 3]):
                wb_ref[...] = w_ref[0, 0].astype(_BF16)

    @pl.when(i * nb < n_used)
    def _():
        rows = lax.broadcasted_iota(jnp.int32, (tr, 1), 0)
        for c in range(nb):
            blk = i * nb + c
            wgb_ref, wub_ref, wdb_ref = wb_refs[3 * c:3 * c + 3]
            x = jnp.where(rows < bv_ref[blk], _unpack_rows(_load_token_tiles(xs_ref, c * tr, tr, parts)), 0.0)
            x = x.astype(_BF16)
            g = _dot(x, wgb_ref[...])
            u = _dot(x, wub_ref[...])
            a = g * (1.0 / (1.0 + jnp.exp(-g))) * u
            y = jnp.where(blk < n_used, _dot(a.astype(_BF16), wdb_ref[...]), 0.0)
            _store_token_tiles(y_ref.at[pl.ds(c * tr * parts, tr * parts), :], _pack_rows(y))

    @pl.when(i * nb >= n_used)
    def _():
        y_ref[...] = jnp.zeros(y_ref.shape, y_ref.dtype)


def _combined_tile(dest_hbm, rf_ref, h_ref, y_hbm, eye_ref, idx_smem, ybuf, isem, rsem, *, tm, n_tiles):
    i = pl.program_id(0)
    parts = ybuf.shape[1] // (2 * tm)
    slot = i % 2

    def idx_copy(tile):
        return pltpu.make_async_copy(dest_hbm.at[tile], idx_smem.at[pl.ds(tile % 2, 1)], isem.at[tile % 2])

    def row_copy(idx_slot, buf_slot, t, k, u):
        d = idx_smem[idx_slot, k * tm + t]
        return pltpu.make_async_copy(
            y_hbm.at[pl.ds(pl.multiple_of(d * parts, parts), parts), :],
            ybuf.at[buf_slot, pl.ds(pl.multiple_of((k * tm + t) * parts, parts), parts), :],
            rsem.at[buf_slot]).start(priority=(2 * u + k) % DMA_THREADS)

    def wait_rows(buf_slot):
        pltpu.make_async_copy(y_hbm.at[pl.ds(0, 2 * tm * parts), :], ybuf.at[buf_slot], rsem.at[buf_slot]).wait()

    @pl.when(i == 0)
    def _():
        idx_copy(0).start()
        idx_copy(0).wait()

        def issue(g, c):
            for u in range(DMA_UNROLL):
                for k in range(2):
                    row_copy(0, 0, g * DMA_UNROLL + u, k, u)
            return c

        lax.fori_loop(0, tm // DMA_UNROLL, issue, 0)
        if n_tiles >= 2:
            idx_copy(1).start()

    @pl.when(i + 1 < n_tiles)
    def _():
        idx_copy(i + 1).wait()

    @pl.when(i + 2 < n_tiles)
    def _():
        idx_copy(i + 2).start()

    nxt = jnp.minimum(i + 1, n_tiles - 1)
    for t in range(tm):
        for k in range(2):
            row_copy(nxt % 2, 1 - slot, t, k, t)

    gt = lax.dot_general(eye_ref[...], rf_ref[0], _NT_DIMS, precision=lax.Precision.HIGHEST,
                         preferred_element_type=_F32)
    wait_rows(slot)
    yb = ybuf.at[slot]
    out = h_ref[...] + (_unpack_rows(_load_token_tiles(yb, 0, tm, parts)) * gt[:, 0:1]
                        + _unpack_rows(_load_token_tiles(yb, tm, tm, parts)) * gt[:, 1:2])

    def drain():
        @pl.when(i == n_tiles - 1)
        def _():
            wait_rows(1 - slot)

    return out, drain


def _combine_norm_kernel(dest_hbm, rf_ref, h_ref, y_hbm, eye_ref, fnorm_ref, out_ref,
                         idx_smem, ybuf, isem, rsem, *, tm, n_tiles):
    h, drain = _combined_tile(dest_hbm, rf_ref, h_ref, y_hbm, eye_ref, idx_smem, ybuf, isem, rsem,
                              tm=tm, n_tiles=n_tiles)
    out_ref[...] = _rms(h, fnorm_ref[...])
    drain()


def _combine_proj_kernel(dest_hbm, rf_ref, h_ref, y_hbm, eye_ref, cos_ref, sin_ref, kvn_ref, wdkv_ref, kvlat_ref,
                         wuk_ref, wuv_ref, bnorm_ref, wdq_ref, qnorm_ref, wuq_ref,
                         h_out_ref, q_ref, k_ref, v_ref, idx_smem, ybuf, isem, rsem, *, tm, n_tiles):
    h, drain = _combined_tile(dest_hbm, rf_ref, h_ref, y_hbm, eye_ref, idx_smem, ybuf, isem, rsem,
                              tm=tm, n_tiles=n_tiles)
    h_out_ref[...] = h
    _project(h, cos_ref, sin_ref, kvn_ref, wdkv_ref, kvlat_ref, wuk_ref, wuv_ref,
             bnorm_ref, wdq_ref, qnorm_ref, wuq_ref, q_ref, k_ref, v_ref)
    drain()


def _rope128(x, cos_t, sin_t):
    lane = lax.broadcasted_iota(jnp.int32, (1, 128), 1)
    first_half = (lane % QK_ROPE_DIM) < (QK_ROPE_DIM // 2)
    swapped = jnp.where(first_half, pltpu.roll(x, 128 - QK_ROPE_DIM // 2, axis=1),
                        pltpu.roll(x, QK_ROPE_DIM // 2, axis=1))
    return x * cos_t + swapped * sin_t


def _project(h, cos_ref, sin_ref, kvn_ref, wdkv_ref, kvlat_ref, wuk_ref, wuv_ref,
             bnorm_ref, wdq_ref, qnorm_ref, wuq_ref, q_ref, k_ref, v_ref):
    cos_t = cos_ref[...]
    sin_t = sin_ref[...]
    c = _dot(_rms(h, kvn_ref[...]).astype(_BF16), wdkv_ref[...])
    ckv = _rms(c[:, :KV_LORA_RANK], kvlat_ref[...]).astype(_BF16)
    kr = _rope128(c[:, KV_LORA_RANK:KV_LORA_RANK + 128], cos_t, sin_t)[:, :QK_ROPE_DIM].astype(_BF16)
    kn = _dot(ckv, wuk_ref[...])
    vv = _dot(ckv, wuv_ref[...])
    cq = _rms(_dot(_rms(h, bnorm_ref[...]).astype(_BF16), wdq_ref[...]), qnorm_ref[...]).astype(_BF16)
    q = _dot(cq, wuq_ref[...])
    rope0 = N_HEADS * QK_NOPE_DIM
    for hd in range(N_HEADS):
        k_ref[hd, :, 0:QK_NOPE_DIM] = kn[:, hd * QK_NOPE_DIM:(hd + 1) * QK_NOPE_DIM].astype(_BF16)
        k_ref[hd, :, QK_NOPE_DIM:QK_DIM] = kr
        for c in range(v_ref.shape[1]):
            tk = v_ref.shape[3]
            v_ref[hd, c] = vv[c * tk:(c + 1) * tk, hd * V_HEAD_DIM:(hd + 1) * V_HEAD_DIM].T.astype(_BF16)
        q_ref[hd, :, 0:QK_NOPE_DIM] = (q[:, hd * QK_NOPE_DIM:(hd + 1) * QK_NOPE_DIM] * Q_SCALE).astype(_BF16)
        qr = _rope128(q[:, rope0 + hd * 128:rope0 + (hd + 1) * 128], cos_t, sin_t)[:, :QK_ROPE_DIM]
        q_ref[hd, :, QK_NOPE_DIM:QK_DIM] = (qr * Q_SCALE).astype(_BF16)


def _attn_kernel(q_ref, k_ref, vt_ref, km_ref, vmt_ref, o_ref, sa_ref, sb_ref, acc_ref, *, tq, tk):
    j = pl.program_id(2)
    heads = range(q_ref.shape[0])
    vd = acc_ref.shape[1]

    def scores(kb, s_ref):
        start = pl.multiple_of(kb * tk, tk)
        for hd in heads:
            s_ref[hd] = lax.dot_general(k_ref[hd, pl.ds(start, tk), :], q_ref[hd], _NT_DIMS,
                                        preferred_element_type=_F32)

    def update(kb, s_ref, carry, masked):
        out = []
        for hd in heads:
            m, l = carry[hd]
            s = s_ref[hd]
            if masked:
                keys = kb * tk + lax.broadcasted_iota(jnp.int32, (tk, tq), 0)
                qpos = j * tq + lax.broadcasted_iota(jnp.int32, (tk, tq), 1)
                s = jnp.where(keys <= qpos, s, NEG_INF)
            m_new = jnp.maximum(m, jnp.max(s, axis=0, keepdims=True))
            alpha = jnp.exp2(m - m_new)
            p = jnp.exp2(s - m_new)
            l = alpha * l + jnp.sum(p, axis=0, keepdims=True)
            acc_ref[hd] = acc_ref[hd] * alpha + _dot(vt_ref[hd, kb], p.astype(_BF16))
            out.append((m_new, l))
        return tuple(out)

    scores(0, sa_ref)
    carry = []
    for hd in heads:
        s0 = lax.dot_general(km_ref[hd], q_ref[hd], _NT_DIMS, preferred_element_type=_F32)
        m = jnp.max(s0, axis=0, keepdims=True)
        p0 = jnp.exp2(s0 - m)
        carry.append((m, jnp.sum(p0, axis=0, keepdims=True)))
        acc_ref[hd] = _dot(vmt_ref[hd, 0, :, 0:N_META], p0.astype(_BF16))
    carry = tuple(carry)

    def pair(kp, carry):
        scores(2 * kp + 1, sb_ref)
        carry = update(2 * kp, sa_ref, carry, False)
        scores(2 * kp + 2, sa_ref)
        return update(2 * kp + 1, sb_ref, carry, False)

    carry = lax.fori_loop(0, j, pair, carry)
    scores(2 * j + 1, sb_ref)
    carry = update(2 * j, sa_ref, carry, True)
    carry = update(2 * j + 1, sb_ref, carry, True)
    for hd in heads:
        o_ref[:, hd * vd:(hd + 1) * vd] = (acc_ref[hd] / carry[hd][1]).T.astype(o_ref.dtype)


def _oproj_kernel(o_ref, h_ref, wo_ref, fnorm_ref, wrt_ref, br_ref, triu_ref,
                  h_out_ref, xn_ref, ri_ref, rf_ref, cnt_ref, base_ref, wsplit_ref):
    i = pl.program_id(0)

    @pl.when(i == 0)
    def _():
        _route_init(wrt_ref, wsplit_ref, base_ref)

    h = h_ref[...] + _dot(o_ref[...], wo_ref[...])
    h_out_ref[...] = h
    xn = _rms(h, fnorm_ref[...])
    _store_token_tiles(xn_ref, _pack_rows(xn))
    valid = lax.broadcasted_iota(jnp.int32, (1, h.shape[0]), 1) >= 0
    _route(xn, valid, wsplit_ref, br_ref, triu_ref, base_ref, ri_ref, rf_ref, cnt_ref)


def _full(shape):
    nd = len(shape)
    return pl.BlockSpec(shape, lambda *_: (0,) * nd)


def _router_operands(router_g, router_g_bias, router_e, router_e_bias):
    d = router_g.shape[0]
    wrt = jnp.concatenate([router_g.T, jnp.zeros((8 - N_EXPERT_GROUPS, d), _F32), router_e.T], axis=0)
    br = jnp.concatenate([router_g_bias, jnp.zeros((8 - N_EXPERT_GROUPS,), _F32), router_e_bias])[:, None]
    return wrt.astype(_F32), br.astype(_F32)


def _route_out(n_tiles, tm):
    shapes = [jax.ShapeDtypeStruct((n_tiles, 8, tm), jnp.int32),
              jax.ShapeDtypeStruct((n_tiles, 8, tm), _F32),
              jax.ShapeDtypeStruct((N_EXPERTS, 128), jnp.int32)]
    specs = [pl.BlockSpec((1, 8, tm), lambda i: (i, 0, 0)),
             pl.BlockSpec((1, 8, tm), lambda i: (i, 0, 0)),
             pl.BlockSpec((N_EXPERTS, 128), lambda i: (0, 0))]
    return shapes, specs


def _moe_experts(xn, ri, counts, w_gate, w_up, w_down, layer, n_tiles, n_real_tiles, d):
    tm, tr = TOKEN_TILE, EXPERT_ROWS
    parts = d // (2 * LANES)
    n_valid = n_real_tiles * tm + (n_tiles - n_real_tiles) * N_META
    n_blocks = -(-(2 * n_valid + N_EXPERTS * (tr - 1)) // tr)
    n_blocks = -(-n_blocks // EXPERT_BLOCKS_PER_STEP) * EXPERT_BLOCKS_PER_STEP
    n_rows = n_blocks * tr

    counts = counts[:, 0]
    padded = (counts + tr - 1) // tr * tr
    pends = jnp.cumsum(padded)
    pstarts = pends - padded
    n_used = (pends[-1] // tr).astype(jnp.int32).reshape(1)
    blk0 = jnp.arange(n_blocks, dtype=jnp.int32) * tr
    block_e = jnp.minimum(jnp.sum(blk0[:, None] >= pends[None, :], axis=1), N_EXPERTS - 1).astype(jnp.int32)
    experts = jnp.arange(N_EXPERTS, dtype=jnp.int32)
    block_oh = block_e[:, None] == experts[None, :]
    block_cnt = jnp.sum(jnp.where(block_oh, counts[None, :], 0), axis=1)
    block_start = jnp.sum(jnp.where(block_oh, pstarts[None, :], 0), axis=1)
    block_valid = jnp.clip(block_cnt - (blk0 - block_start), 0, tr).astype(jnp.int32)
    eid = ri[:, 0:2, :]
    slot0 = jnp.sum(jnp.where(eid[..., None] == experts, pstarts, 0), axis=-1)
    dest = (slot0 + ri[:, 2:4, :]).astype(jnp.int32).reshape(n_tiles, 1, 2 * tm)

    xs = pl.pallas_call(
        functools.partial(_dispatch_kernel, tm=tm, n_tiles=n_tiles, n_real_tiles=n_real_tiles),
        grid=(n_tiles,),
        in_specs=[pl.BlockSpec(memory_space=pl.ANY), pl.BlockSpec(memory_space=pl.ANY)],
        out_specs=pl.BlockSpec(memory_space=pl.ANY),
        out_shape=jax.ShapeDtypeStruct((n_rows * parts, LANES), jnp.uint32),
        scratch_shapes=[pltpu.SMEM((2, 2 * tm), jnp.int32), pltpu.VMEM((3, tm * parts, LANES), jnp.uint32),
                        pltpu.SemaphoreType.DMA((2,)), pltpu.SemaphoreType.DMA((3,)),
                        pltpu.SemaphoreType.DMA((3,))],
        compiler_params=_params(1, has_side_effects=True, disable_bounds_checks=True),
        name="moe_dispatch",
    )(dest, xn)

    f = w_gate.shape[3]
    nb = EXPERT_BLOCKS_PER_STEP
    step_rows = nb * tr * parts

    def expert_block(c):
        return lambda i, be, bv, nu: (layer, be[jnp.minimum(i * nb + c, nu[0] - 1)], 0, 0)

    weight_specs, weight_scratch = [], []
    for c in range(nb):
        weight_specs += [pl.BlockSpec((1, 1, d, f), expert_block(c)), pl.BlockSpec((1, 1, d, f), expert_block(c)),
                         pl.BlockSpec((1, 1, f, d), expert_block(c))]
        weight_scratch += [pltpu.VMEM((d, f), _BF16), pltpu.VMEM((d, f), _BF16), pltpu.VMEM((f, d), _BF16)]
    y = pl.pallas_call(
        _expert_kernel,
        grid_spec=pltpu.PrefetchScalarGridSpec(
            num_scalar_prefetch=3,
            grid=(n_blocks // nb,),
            in_specs=[pl.BlockSpec((step_rows, LANES),
                                   lambda i, be, bv, nu: (jnp.minimum(i, (nu[0] - 1) // nb), 0))] + weight_specs,
            out_specs=pl.BlockSpec((step_rows, LANES), lambda i, be, bv, nu: (i, 0)),
            scratch_shapes=weight_scratch),
        out_shape=jax.ShapeDtypeStruct((n_rows * parts, LANES), jnp.uint32),
        compiler_params=_params(1),
        name="moe_experts",
    )(block_e, block_valid, n_used, xs, *([w_gate, w_up, w_down] * nb))

    return dest, y


def _combine_operands(dest, rf, h, y, d):
    tm = TOKEN_TILE
    parts = d // (2 * LANES)
    eye = jnp.asarray(np.eye(tm, dtype=np.float32))
    in_specs = [pl.BlockSpec(memory_space=pl.ANY),
                pl.BlockSpec((1, 8, tm), lambda i: (i, 0, 0)),
                pl.BlockSpec((tm, d), lambda i: (i, 0)),
                pl.BlockSpec(memory_space=pl.ANY),
                _full((tm, tm))]
    scratch = [pltpu.SMEM((2, 2 * tm), jnp.int32), pltpu.VMEM((2, 2 * tm * parts, LANES), jnp.uint32),
               pltpu.SemaphoreType.DMA((2,)), pltpu.SemaphoreType.DMA((2,))]
    return (dest, rf, h, y, eye), in_specs, scratch


def kernel(x, meta_tokens, a_norm, a_w, a_scale, b_norm, b_w_dq, b_q_norm, b_w_uq, b_w_o, kv_norm, w_dkv,
           kv_lat_norm, w_uk, w_uv, ffn_norm, router_g, router_g_bias, router_e, router_e_bias, w_gate, w_up,
           w_down, final_norm):
    bsz, seq, d = x.shape
    tm = TOKEN_TILE
    assert seq % tm == 0 and seq % ATTN_Q_TILE == 0 and ATTN_Q_TILE == 2 * ATTN_K_TILE and tm % ATTN_K_TILE == 0
    kt = tm // ATTN_K_TILE
    assert d % (LANES * len(POOL_WINDOWS)) == 0 and N_META == max(POOL_WINDOWS) and N_META <= tm
    parts = d // (2 * LANES)
    n_tok = bsz * seq
    n_real_tiles = n_tok // tm
    tiles_per_batch = seq // tm
    n_tiles = n_real_tiles + 1
    gd = d // len(POOL_WINDOWS)
    row = lambda v: v.reshape(1, -1).astype(_F32)

    x2 = x.reshape(n_tok, d)
    meta_pad = jnp.concatenate([meta_tokens, jnp.zeros((tm - N_META, d), x.dtype)], axis=0)

    r = np.arange(tm)[:, None]
    cidx = np.arange(tm)[None, :]
    pb = min(POOL_BLOCK, tm)
    pm = np.stack([((r[:pb] - cidx[:, :pb] >= 0) & (r[:pb] - cidx[:, :pb] < w)) for w in POOL_WINDOWS]).astype(np.float32)
    hc = np.arange(N_META)[None, :]
    ph = np.stack([(r[:N_META] + N_META - hc < w) for w in POOL_WINDOWS]).astype(np.float32)
    triu = (r < cidx).astype(np.float32)
    pm, ph, triu = (jnp.asarray(a, dtype=_BF16) for a in (pm, ph, triu))

    wrt0, br0 = _router_operands(router_g[0], router_g_bias[0], router_e[0], router_e_bias[0])
    route_shapes, route_specs = _route_out(n_tiles, tm)
    tile_or_last = lambda i: (jnp.minimum(i, n_real_tiles - 1), 0)
    halo_blocks = tm // N_META
    h1, xn1, ri1, rf1, cnt1 = pl.pallas_call(
        functools.partial(_mixer_kernel, n_real_tiles=n_real_tiles, tiles_per_batch=tiles_per_batch),
        grid=(n_tiles,),
        in_specs=[pl.BlockSpec((tm, d), tile_or_last),
                  pl.BlockSpec((N_META, d), lambda i: (jnp.clip(i * halo_blocks - 1, 0, n_tok // N_META - 1), 0)),
                  _full((tm, d)), _full((N_META, d)), _full((1, d)),
                  _full((len(POOL_WINDOWS), gd, gd)), _full((1, d)), _full((1, d)),
                  _full((ROUTER_ROWS, d)), _full((ROUTER_ROWS, 1)),
                  _full(pm.shape), _full(ph.shape), _full((tm, tm))],
        out_specs=[pl.BlockSpec((tm, d), lambda i: (i, 0)),
                   pl.BlockSpec((tm * parts, LANES), lambda i: (i, 0))] + route_specs,
        out_shape=[jax.ShapeDtypeStruct((n_tiles * tm, d), _F32),
                   jax.ShapeDtypeStruct((n_tiles * tm * parts, LANES), jnp.uint32)] + route_shapes,
        scratch_shapes=[pltpu.VMEM((N_EXPERTS, 1), _F32), pltpu.VMEM((ROUTER_SPLIT_ROWS, d), _BF16)],
        compiler_params=_params(1),
        name="pool_mixer_router",
    )(x2, x2, meta_pad, meta_tokens, row(a_norm[0]), a_w[0].astype(_BF16), row(a_scale[0]), row(ffn_norm[0]),
      wrt0, br0, pm, ph, triu)

    dest1, y1 = _moe_experts(xn1, ri1, cnt1, w_gate, w_up, w_down, 0, n_tiles, n_real_tiles, d)

    pos = jnp.concatenate([jnp.arange(seq, dtype=_F32) + N_META, jnp.arange(tm, dtype=_F32)])
    inv_freq = ROPE_THETA ** (-jnp.arange(0, QK_ROPE_DIM, 2, dtype=_F32) / QK_ROPE_DIM)
    ang = pos[:, None] * inv_freq[None, :]
    cos_t = jnp.tile(jnp.cos(ang), (1, 4))
    sin_t = jnp.tile(jnp.concatenate([-jnp.sin(ang), jnp.sin(ang)], axis=1), (1, 2))

    wdkv = jnp.concatenate([w_dkv, w_dkv[:, KV_LORA_RANK:]], axis=1).astype(_BF16)
    wuk = w_uk.reshape(KV_LORA_RANK, N_HEADS * QK_NOPE_DIM).astype(_BF16)
    wuv = w_uv.reshape(KV_LORA_RANK, N_HEADS * V_HEAD_DIM).astype(_BF16)
    wuq = b_w_uq[0]
    q_rank = wuq.shape[0]
    wuq_rope = wuq[:, :, QK_NOPE_DIM:]
    wuq = jnp.concatenate([wuq[:, :, :QK_NOPE_DIM].reshape(q_rank, -1),
                           jnp.concatenate([wuq_rope, wuq_rope], axis=2).reshape(q_rank, -1)], axis=1).astype(_BF16)
    wdq = b_w_dq[0].astype(_BF16)
    pos_tile = lambda i: (jnp.where(i < n_real_tiles, i % tiles_per_batch, tiles_per_batch), 0)
    head_tile = lambda i: (0, i, 0)
    operands, in_specs, scratch = _combine_operands(dest1, rf1, h1, y1, d)
    h2, q, k, v = pl.pallas_call(
        functools.partial(_combine_proj_kernel, tm=tm, n_tiles=n_tiles),
        grid=(n_tiles,),
        in_specs=in_specs + [
            pl.BlockSpec((tm, 128), pos_tile), pl.BlockSpec((tm, 128), pos_tile),
            _full((1, d)), _full(wdkv.shape), _full((1, KV_LORA_RANK)), _full(wuk.shape), _full(wuv.shape),
            _full((1, d)), _full(wdq.shape), _full((1, q_rank)), _full(wuq.shape)],
        out_specs=[pl.BlockSpec((tm, d), lambda i: (i, 0)),
                   pl.BlockSpec((N_HEADS, tm, QK_DIM), head_tile),
                   pl.BlockSpec((N_HEADS, tm, QK_DIM), head_tile),
                   pl.BlockSpec((N_HEADS, kt, V_HEAD_DIM, ATTN_K_TILE), lambda i: (0, i, 0, 0))],
        out_shape=[jax.ShapeDtypeStruct((n_tiles * tm, d), _F32),
                   jax.ShapeDtypeStruct((N_HEADS, n_tiles * tm, QK_DIM), _BF16),
                   jax.ShapeDtypeStruct((N_HEADS, n_tiles * tm, QK_DIM), _BF16),
                   jax.ShapeDtypeStruct((N_HEADS, n_tiles * kt, V_HEAD_DIM, ATTN_K_TILE), _BF16)],
        scratch_shapes=scratch,
        compiler_params=_params(1, disable_bounds_checks=True),
        name="moe_combine_latent_qkv",
    )(*operands, cos_t, sin_t, row(kv_norm), wdkv, row(kv_lat_norm), wuk, wuv,
      row(b_norm[0]), wdq, row(b_q_norm[0]), wuq)

    tq, tk, hp = ATTN_Q_TILE, ATTN_K_TILE, ATTN_HEADS_PER_STEP
    n_q = seq // tq
    meta_block = n_tok // N_META
    o = pl.pallas_call(
        functools.partial(_attn_kernel, tq=tq, tk=tk),
        grid=(bsz, N_HEADS // hp, n_q),
        in_specs=[pl.BlockSpec((hp, tq, QK_DIM), lambda b, hg, j: (hg, b * n_q + j, 0)),
                  pl.BlockSpec((hp, seq, QK_DIM), lambda b, hg, j: (hg, b, 0)),
                  pl.BlockSpec((hp, seq // tk, V_HEAD_DIM, tk), lambda b, hg, j: (hg, b, 0, 0)),
                  pl.BlockSpec((hp, N_META, QK_DIM), lambda b, hg, j: (hg, meta_block, 0)),
                  pl.BlockSpec((hp, 1, V_HEAD_DIM, tk), lambda b, hg, j: (hg, n_real_tiles * kt, 0, 0))],
        out_specs=pl.BlockSpec((tq, hp * V_HEAD_DIM), lambda b, hg, j: (b * n_q + j, hg)),
        out_shape=jax.ShapeDtypeStruct((n_tok, N_HEADS * V_HEAD_DIM), _BF16),
        scratch_shapes=[pltpu.VMEM((hp, tk, tq), _F32), pltpu.VMEM((hp, tk, tq), _F32),
                        pltpu.VMEM((hp, V_HEAD_DIM, tq), _F32)],
        compiler_params=_params(3),
        name="causal_attention",
    )(q, k, v, k, v)

    wrt1, br1 = _router_operands(router_g[1], router_g_bias[1], router_e[1], router_e_bias[1])
    route_shapes, route_specs = _route_out(n_real_tiles, tm)
    h3, xn2, ri2, rf2, cnt2 = pl.pallas_call(
        _oproj_kernel,
        grid=(n_real_tiles,),
        in_specs=[pl.BlockSpec((tm, N_HEADS * V_HEAD_DIM), lambda i: (i, 0)),
                  pl.BlockSpec((tm, d), lambda i: (i, 0)),
                  _full((N_HEADS * V_HEAD_DIM, d)), _full((1, d)),
                  _full((ROUTER_ROWS, d)), _full((ROUTER_ROWS, 1)), _full((tm, tm))],
        out_specs=[pl.BlockSpec((tm, d), lambda i: (i, 0)),
                   pl.BlockSpec((tm * parts, LANES), lambda i: (i, 0))] + route_specs,
        out_shape=[jax.ShapeDtypeStruct((n_tok, d), _F32),
                   jax.ShapeDtypeStruct((n_tok * parts, LANES), jnp.uint32)] + route_shapes,
        scratch_shapes=[pltpu.VMEM((N_EXPERTS, 1), _F32), pltpu.VMEM((ROUTER_SPLIT_ROWS, d), _BF16)],
        compiler_params=_params(1),
        name="attn_out_router",
    )(o, h2, b_w_o[0].astype(_BF16), row(ffn_norm[1]), wrt1, br1, triu)

    dest2, y2 = _moe_experts(xn2, ri2, cnt2, w_gate, w_up, w_down, 1, n_real_tiles, n_real_tiles, d)
    operands, in_specs, scratch = _combine_operands(dest2, rf2, h3, y2, d)
    out = pl.pallas_call(
        functools.partial(_combine_norm_kernel, tm=tm, n_tiles=n_real_tiles),
        grid=(n_real_tiles,),
        in_specs=in_specs + [_full((1, d))],
        out_specs=pl.BlockSpec((tm, d), lambda i: (i, 0)),
        out_shape=jax.ShapeDtypeStruct((n_tok, d), _F32),
        scratch_shapes=scratch,
        compiler_params=_params(1, disable_bounds_checks=True),
        name="moe_combine_final_norm",
    )(*operands, row(final_norm))
    return out.reshape(bsz, seq, d)
```

```python
import functools

import numpy as np
import jax
import jax.numpy as jnp
from jax import lax
from jax.experimental import pallas as pl
from jax.experimental.pallas import tpu as pltpu

N_META = 16
POOL_WINDOWS = (2, 4, 8, 16)
N_HEADS = 8
QK_NOPE_DIM = 128
QK_ROPE_DIM = 64
QK_DIM = QK_NOPE_DIM + QK_ROPE_DIM
V_HEAD_DIM = 128
KV_LORA_RANK = 256
ROPE_THETA = 10000.0
ATTN_SCALE = QK_DIM ** -0.5
Q_SCALE = ATTN_SCALE * 1.4426950408889634
N_EXPERT_GROUPS = 4
EXPERTS_PER_GROUP = 8
N_EXPERTS = N_EXPERT_GROUPS * EXPERTS_PER_GROUP
RMS_EPS = 1e-6
NEG_INF = -1e30

TOKEN_TILE = 512
POOL_BLOCK = 256
EXPERT_ROWS = 256
EXPERT_BLOCKS_PER_STEP = 2
ATTN_Q_TILE = 512
ATTN_K_TILE = ATTN_Q_TILE // 2
ATTN_HEADS_PER_STEP = 4
DMA_UNROLL = 8
DMA_THREADS = 2
ROUTER_ROWS = 8 + N_EXPERTS
ROUTER_LO_ROW = 48
ROUTER_SPLIT_ROWS = 2 * ROUTER_LO_ROW
VMEM_LIMIT_BYTES = 48 * 1024 * 1024

LANES = 128
_F32 = jnp.float32
_BF16 = jnp.bfloat16
_NT_DIMS = (((1,), (1,)), ((), ()))


def _params(n_grid_dims=1, **kw):
    return pltpu.CompilerParams(dimension_semantics=("arbitrary",) * n_grid_dims,
                                vmem_limit_bytes=VMEM_LIMIT_BYTES, **kw)


def _rms(x, g):
    ms = jnp.mean(x * x, axis=-1, keepdims=True)
    return x * lax.rsqrt(ms + RMS_EPS) * g


def _split_bf16(x):
    hi = x.astype(_BF16)
    lo = (x - hi.astype(_F32)).astype(_BF16)
    return hi, lo


def _dot(a, b):
    return jnp.dot(a, b, preferred_element_type=_F32)


def _store_token_tiles(ref, x):
    n, d = x.shape
    parts = d // LANES
    for s in range(parts):
        ref[pl.ds(s, n, stride=parts), :] = x[:, s * LANES:(s + 1) * LANES]


def _load_token_tiles(ref, first_token, n, parts):
    return jnp.concatenate([ref[pl.ds(first_token * parts + s, n, stride=parts), :] for s in range(parts)], axis=1)


_HIGH_HALF = 0xFFFF0000


def _pack_rows(x):
    n, d = x.shape
    words = []
    for s in range(d // (2 * LANES)):
        lo = x[:, 2 * s * LANES:(2 * s + 1) * LANES].astype(_BF16).astype(_F32)
        hi = x[:, (2 * s + 1) * LANES:(2 * s + 2) * LANES].astype(_BF16).astype(_F32)
        words.append(lax.shift_right_logical(lax.bitcast_convert_type(lo, jnp.uint32), jnp.uint32(16))
                     | (lax.bitcast_convert_type(hi, jnp.uint32) & jnp.uint32(_HIGH_HALF)))
    return jnp.concatenate(words, axis=1)


def _unpack_rows(w):
    cols = []
    for s in range(w.shape[1] // LANES):
        ws = w[:, s * LANES:(s + 1) * LANES]
        cols.append(lax.bitcast_convert_type(lax.shift_left(ws, jnp.uint32(16)), _F32))
        cols.append(lax.bitcast_convert_type(ws & jnp.uint32(_HIGH_HALF), _F32))
    return jnp.concatenate(cols, axis=1)


def _route_init(wrt_ref, wsplit_ref, base_ref):
    base_ref[...] = jnp.zeros(base_ref.shape, base_ref.dtype)
    w_hi, w_lo = _split_bf16(wrt_ref[...])
    wsplit_ref[...] = jnp.zeros(wsplit_ref.shape, wsplit_ref.dtype)
    wsplit_ref[0:ROUTER_ROWS, :] = w_hi
    wsplit_ref[ROUTER_LO_ROW:ROUTER_LO_ROW + ROUTER_ROWS, :] = w_lo


def _route(xn, valid, wsplit_ref, br_ref, triu_ref, base_ref, ri_ref, rf_ref, cnt_ref):
    tm = xn.shape[0]
    x_hi, x_lo = _split_bf16(xn)
    by_hi = lax.dot_general(wsplit_ref[...], x_hi, _NT_DIMS, preferred_element_type=_F32)
    by_lo = lax.dot_general(wsplit_ref[0:ROUTER_LO_ROW, :], x_lo, _NT_DIMS, preferred_element_type=_F32)
    logits = (by_hi[0:ROUTER_ROWS] + by_hi[ROUTER_LO_ROW:ROUTER_LO_ROW + ROUTER_ROWS]
              + by_lo[0:ROUTER_ROWS] + br_ref[...])
    lg = logits[0:N_EXPERT_GROUPS]
    eg = jnp.exp(lg - jnp.max(lg, axis=0, keepdims=True))
    pg = eg / jnp.sum(eg, axis=0, keepdims=True)
    w_g = jnp.max(pg, axis=0, keepdims=True)
    ig = lax.broadcasted_iota(jnp.int32, pg.shape, 0).astype(_F32)
    g_sel = jnp.min(jnp.where(pg == w_g, ig, float(N_EXPERT_GROUPS)), axis=0, keepdims=True)

    sel = logits[8:8 + EXPERTS_PER_GROUP]
    for g in range(1, N_EXPERT_GROUPS):
        sel = jnp.where(g_sel == float(g), logits[8 + g * EXPERTS_PER_GROUP:8 + (g + 1) * EXPERTS_PER_GROUP], sel)
    ie = lax.broadcasted_iota(jnp.int32, sel.shape, 0).astype(_F32)
    v1 = jnp.max(sel, axis=0, keepdims=True)
    i1 = jnp.min(jnp.where(sel == v1, ie, float(EXPERTS_PER_GROUP)), axis=0, keepdims=True)
    rest = jnp.where(ie == i1, -jnp.inf, sel)
    v2 = jnp.max(rest, axis=0, keepdims=True)
    i2 = jnp.min(jnp.where(rest == v2, ie, float(EXPERTS_PER_GROUP)), axis=0, keepdims=True)
    e2 = jnp.exp(v2 - v1)
    den = 1.0 + e2
    validf = valid.astype(_F32)
    gate0 = w_g * (1.0 / den) * validf
    gate1 = w_g * (e2 / den) * validf
    eid0 = g_sel * float(EXPERTS_PER_GROUP) + i1
    eid1 = g_sel * float(EXPERTS_PER_GROUP) + i2

    iall = lax.broadcasted_iota(jnp.int32, (N_EXPERTS, tm), 0).astype(_F32)
    oh0 = jnp.where(iall == eid0, validf, 0.0)
    oh1 = jnp.where(iall == eid1, validf, 0.0)
    both = oh0 + oh1
    before = _dot(both.astype(_BF16), triu_ref[...]) + base_ref[...]
    rank0 = jnp.sum(oh0 * before, axis=0, keepdims=True)
    rank1 = jnp.sum(oh1 * before, axis=0, keepdims=True)
    base_ref[...] = base_ref[...] + jnp.sum(both, axis=1, keepdims=True)

    ri_ref[...] = jnp.zeros(ri_ref.shape, ri_ref.dtype)
    rf_ref[...] = jnp.zeros(rf_ref.shape, rf_ref.dtype)
    ri_ref[0, 0:1, :] = eid0.astype(jnp.int32)
    ri_ref[0, 1:2, :] = eid1.astype(jnp.int32)
    ri_ref[0, 2:3, :] = rank0.astype(jnp.int32)
    ri_ref[0, 3:4, :] = rank1.astype(jnp.int32)
    rf_ref[0, 0:1, :] = gate0
    rf_ref[0, 1:2, :] = gate1
    cnt_ref[...] = jnp.broadcast_to(base_ref[...], cnt_ref.shape).astype(jnp.int32)


def _mixer_kernel(x_ref, xh_ref, mp_ref, meta_ref, anorm_ref, aw_ref, ascale_ref, fnorm_ref,
                  wrt_ref, br_ref, pm_ref, ph_ref, triu_ref,
                  h_ref, xn_ref, ri_ref, rf_ref, cnt_ref, base_ref, wsplit_ref, *, n_real_tiles, tiles_per_batch):
    i = pl.program_id(0)
    tm = x_ref.shape[0]
    gd = x_ref.shape[1] // len(POOL_WINDOWS)
    is_meta = i == n_real_tiles
    first = (i % tiles_per_batch) == 0

    @pl.when(i == 0)
    def _():
        _route_init(wrt_ref, wsplit_ref, base_ref)

    h = jnp.where(is_meta, mp_ref[...], x_ref[...])
    halo = jnp.where(is_meta, 0.0, jnp.where(first, meta_ref[...], xh_ref[...]))
    hn = _rms(h, anorm_ref[...])
    hh = _rms(halo, anorm_ref[...])
    hn_hi, hn_lo = _split_bf16(hn)
    hh_hi, hh_lo = _split_bf16(hh)
    pb = pm_ref.shape[1]
    for c in range(tm // pb):
        rows = slice(c * pb, (c + 1) * pb)
        row = c * pb + lax.broadcasted_iota(jnp.int32, (pb, 1), 0)
        for g, w in enumerate(POOL_WINDOWS):
            sl = slice(g * gd, (g + 1) * gd)
            win = _dot(pm_ref[g], hn_hi[rows, sl]) + _dot(pm_ref[g], hn_lo[rows, sl])
            if c == 0:
                before_hi, before_lo = hh_hi[:, sl], hh_lo[:, sl]
            else:
                before_hi, before_lo = hn_hi[c * pb - N_META:c * pb, sl], hn_lo[c * pb - N_META:c * pb, sl]
            top = win[:N_META] + _dot(ph_ref[g], before_hi) + _dot(ph_ref[g], before_lo)
            win = jnp.concatenate([top, win[N_META:]], axis=0)
            cnt = jnp.where(is_meta, jnp.minimum(row + 1, w), w).astype(_F32)
            pooled = win * (1.0 / cnt) - hn[rows, sl]
            mix = _dot(pooled.astype(_BF16), aw_ref[g])
            h_ref[rows, sl] = h[rows, sl] + mix * ascale_ref[:, sl]

    xn = _rms(h_ref[...], fnorm_ref[...])
    _store_token_tiles(xn_ref, _pack_rows(xn))
    lane = lax.broadcasted_iota(jnp.int32, (1, tm), 1)
    valid = jnp.logical_or(jnp.logical_not(is_meta), lane < N_META)
    _route(xn, valid, wsplit_ref, br_ref, triu_ref, base_ref, ri_ref, rf_ref, cnt_ref)


def _dispatch_kernel(dest_hbm, xn_hbm, xs_hbm, idx_smem, xbuf, isem, lsem, ssem, *, tm, n_tiles, n_real_tiles):
    i = pl.program_id(0)
    parts = xbuf.shape[1] // tm
    buf = i % 3

    def idx_copy(tile):
        return pltpu.make_async_copy(dest_hbm.at[tile], idx_smem.at[pl.ds(tile % 2, 1)], isem.at[tile % 2])

    def load(tile):
        rows = tm * parts
        return pltpu.make_async_copy(xn_hbm.at[pl.ds(pl.multiple_of(tile * rows, rows), rows), :],
                                     xbuf.at[tile % 3], lsem.at[tile % 3])

    def wait_scatters(tile):
        def wait_rows(n_tok):
            for k in range(2):
                pltpu.make_async_copy(xbuf.at[tile % 3, pl.ds(0, n_tok * parts), :],
                                      xs_hbm.at[pl.ds(0, n_tok * parts), :], ssem.at[tile % 3]).wait()

        @pl.when(tile < n_real_tiles)
        def _():
            wait_rows(tm)

        @pl.when(tile >= n_real_tiles)
        def _():
            wait_rows(N_META)

    def scatter(n_tok):
        def issue(g, c):
            for u in range(min(DMA_UNROLL, n_tok)):
                t = g * min(DMA_UNROLL, n_tok) + u
                src = xbuf.at[buf, pl.ds(pl.multiple_of(t * parts, parts), parts), :]
                for k in range(2):
                    d = idx_smem[i % 2, k * tm + t]
                    pltpu.make_async_copy(src, xs_hbm.at[pl.ds(pl.multiple_of(d * parts, parts), parts), :],
                                          ssem.at[buf]).start(priority=(2 * u + k) % DMA_THREADS)
            return c

        lax.fori_loop(0, n_tok // min(DMA_UNROLL, n_tok), issue, 0)

    @pl.when(i == 0)
    def _():
        idx_copy(0).start()
        load(0).start()

    @pl.when(i >= 2)
    def _():
        wait_scatters(i - 2)

    @pl.when(i + 1 < n_tiles)
    def _():
        idx_copy(i + 1).start()
        load(i + 1).start()

    idx_copy(i).wait()
    load(i).wait()

    @pl.when(i < n_real_tiles)
    def _():
        scatter(tm)

    @pl.when(i >= n_real_tiles)
    def _():
        scatter(N_META)

    @pl.when(i == n_tiles - 1)
    def _():
        if n_tiles >= 2:
            wait_scatters(i - 1)
        wait_scatters(i)


def _expert_kernel(be_ref, bv_ref, nu_ref, xs_ref, *refs):
    nb = EXPERT_BLOCKS_PER_STEP
    w_refs, y_ref, wb_refs = refs[:3 * nb], refs[3 * nb], refs[3 * nb + 1:]
    i = pl.program_id(0)
    n_used = nu_ref[0]
    parts = wb_refs[0].shape[0] // (2 * LANES)
    tr = xs_ref.shape[0] // (nb * parts)

    for c in range(nb):
        blk = i * nb + c
        changed = be_ref[blk] != be_ref[jnp.maximum(blk - nb, 0)]

        @pl.when(jnp.logical_or(i == 0, jnp.logical_and(blk < n_used, changed)))
        def _(c=c):
            for w_ref, wb_ref in zip(w_refs[3 * c:3 * c + 3], wb_refs[3 * c:3 * c + 3]):
                wb_ref[...] = w_ref[0, 0].astype(_BF16)

    @pl.when(i * nb < n_used)
    def _():
        rows = lax.broadcasted_iota(jnp.int32, (tr, 1), 0)
        for c in range(nb):
            blk = i * nb + c
            wgb_ref, wub_ref, wdb_ref = wb_refs[3 * c:3 * c + 3]
            x = jnp.where(rows < bv_ref[blk], _unpack_rows(_load_token_tiles(xs_ref, c * tr, tr, parts)), 0.0)
            x = x.astype(_BF16)
            g = _dot(x, wgb_ref[...])
            u = _dot(x, wub_ref[...])
            a = g * (1.0 / (1.0 + jnp.exp(-g))) * u
            y = jnp.where(blk < n_used, _dot(a.astype(_BF16), wdb_ref[...]), 0.0)
            _store_token_tiles(y_ref.at[pl.ds(c * tr * parts, tr * parts), :], _pack_rows(y))

    @pl.when(i * nb >= n_used)
    def _():
        y_ref[...] = jnp.zeros(y_ref.shape, y_ref.dtype)


def _combined_tile(dest_hbm, rf_ref, h_ref, y_hbm, idx_smem, ybuf, isem, rsem, *, tm, n_tiles):
    i = pl.program_id(0)
    parts = ybuf.shape[1] // (2 * tm)
    slot = i % 2

    def idx_copy(tile):
        return pltpu.make_async_copy(dest_hbm.at[tile], idx_smem.at[pl.ds(tile % 2, 1)], isem.at[tile % 2])

    def row_copy(idx_slot, buf_slot, t, k, u):
        d = idx_smem[idx_slot, k * tm + t]
        return pltpu.make_async_copy(
            y_hbm.at[pl.ds(pl.multiple_of(d * parts, parts), parts), :],
            ybuf.at[buf_slot, pl.ds(pl.multiple_of((k * tm + t) * parts, parts), parts), :],
            rsem.at[buf_slot]).start(priority=(2 * u + k) % DMA_THREADS)

    def wait_rows(buf_slot):
        pltpu.make_async_copy(y_hbm.at[pl.ds(0, 2 * tm * parts), :], ybuf.at[buf_slot], rsem.at[buf_slot]).wait()

    @pl.when(i == 0)
    def _():
        idx_copy(0).start()
        idx_copy(0).wait()

        def issue(g, c):
            for u in range(DMA_UNROLL):
                for k in range(2):
                    row_copy(0, 0, g * DMA_UNROLL + u, k, u)
            return c

        lax.fori_loop(0, tm // DMA_UNROLL, issue, 0)
        if n_tiles >= 2:
            idx_copy(1).start()

    @pl.when(i + 1 < n_tiles)
    def _():
        idx_copy(i + 1).wait()

    @pl.when(i + 2 < n_tiles)
    def _():
        idx_copy(i + 2).start()

    nxt = jnp.minimum(i + 1, n_tiles - 1)
    for t in range(tm):
        for k in range(2):
            row_copy(nxt % 2, 1 - slot, t, k, t)

    gates = rf_ref[0]
    gt = jnp.concatenate([gates, jnp.zeros((LANES - gates.shape[0], tm), _F32)], axis=0).T
    wait_rows(slot)
    yb = ybuf.at[slot]
    out = h_ref[...] + (_unpack_rows(_load_token_tiles(yb, 0, tm, parts)) * gt[:, 0:1]
                        + _unpack_rows(_load_token_tiles(yb, tm, tm, parts)) * gt[:, 1:2])

    def drain():
        @pl.when(i == n_tiles - 1)
        def _():
            wait_rows(1 - slot)

    return out, drain


def _combine_norm_kernel(dest_hbm, rf_ref, h_ref, y_hbm, fnorm_ref, out_ref,
                         idx_smem, ybuf, isem, rsem, *, tm, n_tiles):
    h, drain = _combined_tile(dest_hbm, rf_ref, h_ref, y_hbm, idx_smem, ybuf, isem, rsem,
                              tm=tm, n_tiles=n_tiles)
    out_ref[...] = _rms(h, fnorm_ref[...])
    drain()


def _combine_proj_kernel(dest_hbm, rf_ref, h_ref, y_hbm, cos_ref, sin_ref, kvn_ref, wdkv_ref, kvlat_ref,
                         wuk_ref, wuv_ref, bnorm_ref, wdq_ref, qnorm_ref, wuq_ref,
                         h_out_ref, q_ref, k_ref, v_ref, idx_smem, ybuf, isem, rsem, *, tm, n_tiles):
    h, drain = _combined_tile(dest_hbm, rf_ref, h_ref, y_hbm, idx_smem, ybuf, isem, rsem,
                              tm=tm, n_tiles=n_tiles)
    h_out_ref[...] = h
    _project(h, cos_ref, sin_ref, kvn_ref, wdkv_ref, kvlat_ref, wuk_ref, wuv_ref,
             bnorm_ref, wdq_ref, qnorm_ref, wuq_ref, q_ref, k_ref, v_ref)
    drain()


def _rope128(x, cos_t, sin_t):
    lane = lax.broadcasted_iota(jnp.int32, (1, 128), 1)
    first_half = (lane % QK_ROPE_DIM) < (QK_ROPE_DIM // 2)
    swapped = jnp.where(first_half, pltpu.roll(x, 128 - QK_ROPE_DIM // 2, axis=1),
                        pltpu.roll(x, QK_ROPE_DIM // 2, axis=1))
    return x * cos_t + swapped * sin_t


def _project(h, cos_ref, sin_ref, kvn_ref, wdkv_ref, kvlat_ref, wuk_ref, wuv_ref,
             bnorm_ref, wdq_ref, qnorm_ref, wuq_ref, q_ref, k_ref, v_ref):
    cos_t = cos_ref[...]
    sin_t = sin_ref[...]
    c = _dot(_rms(h, kvn_ref[...]).astype(_BF16), wdkv_ref[...])
    ckv = _rms(c[:, :KV_LORA_RANK], kvlat_ref[...]).astype(_BF16)
    kr = _rope128(c[:, KV_LORA_RANK:KV_LORA_RANK + 128], cos_t, sin_t)[:, :QK_ROPE_DIM].astype(_BF16)
    kn = _dot(ckv, wuk_ref[...])
    vv = _dot(ckv, wuv_ref[...])
    cq = _rms(_dot(_rms(h, bnorm_ref[...]).astype(_BF16), wdq_ref[...]), qnorm_ref[...]).astype(_BF16)
    q = _dot(cq, wuq_ref[...])
    rope0 = N_HEADS * QK_NOPE_DIM
    for hd in range(N_HEADS):
        k_ref[hd, :, 0:QK_NOPE_DIM] = kn[:, hd * QK_NOPE_DIM:(hd + 1) * QK_NOPE_DIM].astype(_BF16)
        k_ref[hd, :, QK_NOPE_DIM:QK_DIM] = kr
        for c in range(v_ref.shape[1]):
            tk = v_ref.shape[3]
            v_ref[hd, c] = vv[c * tk:(c + 1) * tk, hd * V_HEAD_DIM:(hd + 1) * V_HEAD_DIM].T.astype(_BF16)
        q_ref[hd, :, 0:QK_NOPE_DIM] = (q[:, hd * QK_NOPE_DIM:(hd + 1) * QK_NOPE_DIM] * Q_SCALE).astype(_BF16)
        qr = _rope128(q[:, rope0 + hd * 128:rope0 + (hd + 1) * 128], cos_t, sin_t)[:, :QK_ROPE_DIM]
        q_ref[hd, :, QK_NOPE_DIM:QK_DIM] = (qr * Q_SCALE).astype(_BF16)


def _attn_kernel(q_ref, k_ref, vt_ref, km_ref, vmt_ref, o_ref, sa_ref, sb_ref, acc_ref, *, tq, tk):
    j = pl.program_id(2)
    heads = range(q_ref.shape[0])
    vd = acc_ref.shape[1]

    def scores(kb, s_ref):
        start = pl.multiple_of(kb * tk, tk)
        for hd in heads:
            s_ref[hd] = lax.dot_general(k_ref[hd, pl.ds(start, tk), :], q_ref[hd], _NT_DIMS,
                                        preferred_element_type=_F32)

    def update(kb, s_ref, carry, masked):
        out = []
        for hd in heads:
            m, l = carry[hd]
            s = s_ref[hd]
            if masked:
                keys = kb * tk + lax.broadcasted_iota(jnp.int32, (tk, tq), 0)
                qpos = j * tq + lax.broadcasted_iota(jnp.int32, (tk, tq), 1)
                s = jnp.where(keys <= qpos, s, NEG_INF)
            m_new = jnp.maximum(m, jnp.max(s, axis=0, keepdims=True))
            alpha = jnp.exp2(m - m_new)
            p = jnp.exp2(s - m_new)
            l = alpha * l + jnp.sum(p, axis=0, keepdims=True)
            acc_ref[hd] = acc_ref[hd] * alpha + _dot(vt_ref[hd, kb], p.astype(_BF16))
            out.append((m_new, l))
        return tuple(out)

    scores(0, sa_ref)
    carry = []
    for hd in heads:
        s0 = lax.dot_general(km_ref[hd], q_ref[hd], _NT_DIMS, preferred_element_type=_F32)
        m = jnp.max(s0, axis=0, keepdims=True)
        p0 = jnp.exp2(s0 - m)
        carry.append((m, jnp.sum(p0, axis=0, keepdims=True)))
        acc_ref[hd] = _dot(vmt_ref[hd, 0, :, 0:N_META], p0.astype(_BF16))
    carry = tuple(carry)

    def pair(kp, carry):
        scores(2 * kp + 1, sb_ref)
        carry = update(2 * kp, sa_ref, carry, False)
        scores(2 * kp + 2, sa_ref)
        return update(2 * kp + 1, sb_ref, carry, False)

    carry = lax.fori_loop(0, j, pair, carry)
    scores(2 * j + 1, sb_ref)
    carry = update(2 * j, sa_ref, carry, True)
    carry = update(2 * j + 1, sb_ref, carry, True)
    for hd in heads:
        o_ref[:, hd * vd:(hd + 1) * vd] = (acc_ref[hd] / carry[hd][1]).T.astype(o_ref.dtype)


def _oproj_kernel(o_ref, h_ref, wo_ref, fnorm_ref, wrt_ref, br_ref, triu_ref,
                  h_out_ref, xn_ref, ri_ref, rf_ref, cnt_ref, base_ref, wsplit_ref):
    i = pl.program_id(0)

    @pl.when(i == 0)
    def _():
        _route_init(wrt_ref, wsplit_ref, base_ref)

    h = h_ref[...] + _dot(o_ref[...], wo_ref[...])
    h_out_ref[...] = h
    xn = _rms(h, fnorm_ref[...])
    _store_token_tiles(xn_ref, _pack_rows(xn))
    valid = lax.broadcasted_iota(jnp.int32, (1, h.shape[0]), 1) >= 0
    _route(xn, valid, wsplit_ref, br_ref, triu_ref, base_ref, ri_ref, rf_ref, cnt_ref)


def _full(shape):
    nd = len(shape)
    return pl.BlockSpec(shape, lambda *_: (0,) * nd)


def _router_operands(router_g, router_g_bias, router_e, router_e_bias):
    d = router_g.shape[0]
    wrt = jnp.concatenate([router_g.T, jnp.zeros((8 - N_EXPERT_GROUPS, d), _F32), router_e.T], axis=0)
    br = jnp.concatenate([router_g_bias, jnp.zeros((8 - N_EXPERT_GROUPS,), _F32), router_e_bias])[:, None]
    return wrt.astype(_F32), br.astype(_F32)


def _route_out(n_tiles, tm):
    shapes = [jax.ShapeDtypeStruct((n_tiles, 8, tm), jnp.int32),
              jax.ShapeDtypeStruct((n_tiles, 8, tm), _F32),
              jax.ShapeDtypeStruct((N_EXPERTS, 128), jnp.int32)]
    specs = [pl.BlockSpec((1, 8, tm), lambda i: (i, 0, 0)),
             pl.BlockSpec((1, 8, tm), lambda i: (i, 0, 0)),
             pl.BlockSpec((N_EXPERTS, 128), lambda i: (0, 0))]
    return shapes, specs


def _moe_experts(xn, ri, counts, w_gate, w_up, w_down, layer, n_tiles, n_real_tiles, d):
    tm, tr = TOKEN_TILE, EXPERT_ROWS
    parts = d // (2 * LANES)
    n_valid = n_real_tiles * tm + (n_tiles - n_real_tiles) * N_META
    n_blocks = -(-(2 * n_valid + N_EXPERTS * (tr - 1)) // tr)
    n_blocks = -(-n_blocks // EXPERT_BLOCKS_PER_STEP) * EXPERT_BLOCKS_PER_STEP
    n_rows = n_blocks * tr

    counts = counts[:, 0]
    padded = (counts + tr - 1) // tr * tr
    pends = jnp.cumsum(padded)
    pstarts = pends - padded
    n_used = (pends[-1] // tr).astype(jnp.int32).reshape(1)
    blk0 = jnp.arange(n_blocks, dtype=jnp.int32) * tr
    block_e = jnp.minimum(jnp.sum(blk0[:, None] >= pends[None, :], axis=1), N_EXPERTS - 1).astype(jnp.int32)
    experts = jnp.arange(N_EXPERTS, dtype=jnp.int32)
    block_oh = block_e[:, None] == experts[None, :]
    block_cnt = jnp.sum(jnp.where(block_oh, counts[None, :], 0), axis=1)
    block_start = jnp.sum(jnp.where(block_oh, pstarts[None, :], 0), axis=1)
    block_valid = jnp.clip(block_cnt - (blk0 - block_start), 0, tr).astype(jnp.int32)
    eid = ri[:, 0:2, :]
    slot0 = jnp.sum(jnp.where(eid[..., None] == experts, pstarts, 0), axis=-1)
    dest = (slot0 + ri[:, 2:4, :]).astype(jnp.int32).reshape(n_tiles, 1, 2 * tm)

    xs = pl.pallas_call(
        functools.partial(_dispatch_kernel, tm=tm, n_tiles=n_tiles, n_real_tiles=n_real_tiles),
        grid=(n_tiles,),
        in_specs=[pl.BlockSpec(memory_space=pl.ANY), pl.BlockSpec(memory_space=pl.ANY)],
        out_specs=pl.BlockSpec(memory_space=pl.ANY),
        out_shape=jax.ShapeDtypeStruct((n_rows * parts, LANES), jnp.uint32),
        scratch_shapes=[pltpu.SMEM((2, 2 * tm), jnp.int32), pltpu.VMEM((3, tm * parts, LANES), jnp.uint32),
                        pltpu.SemaphoreType.DMA((2,)), pltpu.SemaphoreType.DMA((3,)),
                        pltpu.SemaphoreType.DMA((3,))],
        compiler_params=_params(1, has_side_effects=True, disable_bounds_checks=True),
        name="moe_dispatch",
    )(dest, xn)

    f = w_gate.shape[3]
    nb = EXPERT_BLOCKS_PER_STEP
    step_rows = nb * tr * parts

    def expert_block(c):
        return lambda i, be, bv, nu: (layer, be[jnp.minimum(i * nb + c, nu[0] - 1)], 0, 0)

    weight_specs, weight_scratch = [], []
    for c in range(nb):
        weight_specs += [pl.BlockSpec((1, 1, d, f), expert_block(c)), pl.BlockSpec((1, 1, d, f), expert_block(c)),
                         pl.BlockSpec((1, 1, f, d), expert_block(c))]
        weight_scratch += [pltpu.VMEM((d, f), _BF16), pltpu.VMEM((d, f), _BF16), pltpu.VMEM((f, d), _BF16)]
    y = pl.pallas_call(
        _expert_kernel,
        grid_spec=pltpu.PrefetchScalarGridSpec(
            num_scalar_prefetch=3,
            grid=(n_blocks // nb,),
            in_specs=[pl.BlockSpec((step_rows, LANES),
                                   lambda i, be, bv, nu: (jnp.minimum(i, (nu[0] - 1) // nb), 0))] + weight_specs,
            out_specs=pl.BlockSpec((step_rows, LANES), lambda i, be, bv, nu: (i, 0)),
            scratch_shapes=weight_scratch),
        out_shape=jax.ShapeDtypeStruct((n_rows * parts, LANES), jnp.uint32),
        compiler_params=_params(1),
        name="moe_experts",
    )(block_e, block_valid, n_used, xs, *([w_gate, w_up, w_down] * nb))

    return dest, y


def _combine_operands(dest, rf, h, y, d):
    tm = TOKEN_TILE
    parts = d // (2 * LANES)
    in_specs = [pl.BlockSpec(memory_space=pl.ANY),
                pl.BlockSpec((1, 8, tm), lambda i: (i, 0, 0)),
                pl.BlockSpec((tm, d), lambda i: (i, 0)),
                pl.BlockSpec(memory_space=pl.ANY)]
    scratch = [pltpu.SMEM((2, 2 * tm), jnp.int32), pltpu.VMEM((2, 2 * tm * parts, LANES), jnp.uint32),
               pltpu.SemaphoreType.DMA((2,)), pltpu.SemaphoreType.DMA((2,))]
    return (dest, rf, h, y), in_specs, scratch


def kernel(x, meta_tokens, a_norm, a_w, a_scale, b_norm, b_w_dq, b_q_norm, b_w_uq, b_w_o, kv_norm, w_dkv,
           kv_lat_norm, w_uk, w_uv, ffn_norm, router_g, router_g_bias, router_e, router_e_bias, w_gate, w_up,
           w_down, final_norm):
    bsz, seq, d = x.shape
    tm = TOKEN_TILE
    assert seq % tm == 0 and seq % ATTN_Q_TILE == 0 and ATTN_Q_TILE == 2 * ATTN_K_TILE and tm % ATTN_K_TILE == 0
    kt = tm // ATTN_K_TILE
    assert d % (LANES * len(POOL_WINDOWS)) == 0 and N_META == max(POOL_WINDOWS) and N_META <= tm
    parts = d // (2 * LANES)
    n_tok = bsz * seq
    n_real_tiles = n_tok // tm
    tiles_per_batch = seq // tm
    n_tiles = n_real_tiles + 1
    gd = d // len(POOL_WINDOWS)
    row = lambda v: v.reshape(1, -1).astype(_F32)

    x2 = x.reshape(n_tok, d)
    meta_pad = jnp.concatenate([meta_tokens, jnp.zeros((tm - N_META, d), x.dtype)], axis=0)

    r = np.arange(tm)[:, None]
    cidx = np.arange(tm)[None, :]
    pb = min(POOL_BLOCK, tm)
    pm = np.stack([((r[:pb] - cidx[:, :pb] >= 0) & (r[:pb] - cidx[:, :pb] < w)) for w in POOL_WINDOWS]).astype(np.float32)
    hc = np.arange(N_META)[None, :]
    ph = np.stack([(r[:N_META] + N_META - hc < w) for w in POOL_WINDOWS]).astype(np.float32)
    triu = (r < cidx).astype(np.float32)
    pm, ph, triu = (jnp.asarray(a, dtype=_BF16) for a in (pm, ph, triu))

    wrt0, br0 = _router_operands(router_g[0], router_g_bias[0], router_e[0], router_e_bias[0])
    route_shapes, route_specs = _route_out(n_tiles, tm)
    tile_or_last = lambda i: (jnp.minimum(i, n_real_tiles - 1), 0)
    halo_blocks = tm // N_META
    h1, xn1, ri1, rf1, cnt1 = pl.pallas_call(
        functools.partial(_mixer_kernel, n_real_tiles=n_real_tiles, tiles_per_batch=tiles_per_batch),
        grid=(n_tiles,),
        in_specs=[pl.BlockSpec((tm, d), tile_or_last),
                  pl.BlockSpec((N_META, d), lambda i: (jnp.clip(i * halo_blocks - 1, 0, n_tok // N_META - 1), 0)),
                  _full((tm, d)), _full((N_META, d)), _full((1, d)),
                  _full((len(POOL_WINDOWS), gd, gd)), _full((1, d)), _full((1, d)),
                  _full((ROUTER_ROWS, d)), _full((ROUTER_ROWS, 1)),
                  _full(pm.shape), _full(ph.shape), _full((tm, tm))],
        out_specs=[pl.BlockSpec((tm, d), lambda i: (i, 0)),
                   pl.BlockSpec((tm * parts, LANES), lambda i: (i, 0))] + route_specs,
        out_shape=[jax.ShapeDtypeStruct((n_tiles * tm, d), _F32),
                   jax.ShapeDtypeStruct((n_tiles * tm * parts, LANES), jnp.uint32)] + route_shapes,
        scratch_shapes=[pltpu.VMEM((N_EXPERTS, 1), _F32), pltpu.VMEM((ROUTER_SPLIT_ROWS, d), _BF16)],
        compiler_params=_params(1),
        name="pool_mixer_router",
    )(x2, x2, meta_pad, meta_tokens, row(a_norm[0]), a_w[0].astype(_BF16), row(a_scale[0]), row(ffn_norm[0]),
      wrt0, br0, pm, ph, triu)

    dest1, y1 = _moe_experts(xn1, ri1, cnt1, w_gate, w_up, w_down, 0, n_tiles, n_real_tiles, d)

    pos = jnp.concatenate([jnp.arange(seq, dtype=_F32) + N_META, jnp.arange(tm, dtype=_F32)])
    inv_freq = ROPE_THETA ** (-jnp.arange(0, QK_ROPE_DIM, 2, dtype=_F32) / QK_ROPE_DIM)
    ang = pos[:, None] * inv_freq[None, :]
    cos_t = jnp.tile(jnp.cos(ang), (1, 4))
    sin_t = jnp.tile(jnp.concatenate([-jnp.sin(ang), jnp.sin(ang)], axis=1), (1, 2))

    wdkv = jnp.concatenate([w_dkv, w_dkv[:, KV_LORA_RANK:]], axis=1).astype(_BF16)
    wuk = w_uk.reshape(KV_LORA_RANK, N_HEADS * QK_NOPE_DIM).astype(_BF16)
    wuv = w_uv.reshape(KV_LORA_RANK, N_HEADS * V_HEAD_DIM).astype(_BF16)
    wuq = b_w_uq[0]
    q_rank = wuq.shape[0]
    wuq_rope = wuq[:, :, QK_NOPE_DIM:]
    wuq = jnp.concatenate([wuq[:, :, :QK_NOPE_DIM].reshape(q_rank, -1),
                           jnp.concatenate([wuq_rope, wuq_rope], axis=2).reshape(q_rank, -1)], axis=1).astype(_BF16)
    wdq = b_w_dq[0].astype(_BF16)
    pos_tile = lambda i: (jnp.where(i < n_real_tiles, i % tiles_per_batch, tiles_per_batch), 0)
    head_tile = lambda i: (0, i, 0)
    operands, in_specs, scratch = _combine_operands(dest1, rf1, h1, y1, d)
    h2, q, k, v = pl.pallas_call(
        functools.partial(_combine_proj_kernel, tm=tm, n_tiles=n_tiles),
        grid=(n_tiles,),
        in_specs=in_specs + [
            pl.BlockSpec((tm, 128), pos_tile), pl.BlockSpec((tm, 128), pos_tile),
            _full((1, d)), _full(wdkv.shape), _full((1, KV_LORA_RANK)), _full(wuk.shape), _full(wuv.shape),
            _full((1, d)), _full(wdq.shape), _full((1, q_rank)), _full(wuq.shape)],
        out_specs=[pl.BlockSpec((tm, d), lambda i: (i, 0)),
                   pl.BlockSpec((N_HEADS, tm, QK_DIM), head_tile),
                   pl.BlockSpec((N_HEADS, tm, QK_DIM), head_tile),
                   pl.BlockSpec((N_HEADS, kt, V_HEAD_DIM, ATTN_K_TILE), lambda i: (0, i, 0, 0))],
        out_shape=[jax.ShapeDtypeStruct((n_tiles * tm, d), _F32),
                   jax.ShapeDtypeStruct((N_HEADS, n_tiles * tm, QK_DIM), _BF16),
                   jax.ShapeDtypeStruct((N_HEADS, n_tiles * tm, QK_DIM), _BF16),
                   jax.ShapeDtypeStruct((N_HEADS, n_tiles * kt, V_HEAD_DIM, ATTN_K_TILE), _BF16)],
        scratch_shapes=scratch,
        compiler_params=_params(1, disable_bounds_checks=True),
        name="moe_combine_latent_qkv",
    )(*operands, cos_t, sin_t, row(kv_norm), wdkv, row(kv_lat_norm), wuk, wuv,
      row(b_norm[0]), wdq, row(b_q_norm[0]), wuq)

    tq, tk, hp = ATTN_Q_TILE, ATTN_K_TILE, ATTN_HEADS_PER_STEP
    n_q = seq // tq
    meta_block = n_tok // N_META
    o = pl.pallas_call(
        functools.partial(_attn_kernel, tq=tq, tk=tk),
        grid=(bsz, N_HEADS // hp, n_q),
        in_specs=[pl.BlockSpec((hp, tq, QK_DIM), lambda b, hg, j: (hg, b * n_q + j, 0)),
                  pl.BlockSpec((hp, seq, QK_DIM), lambda b, hg, j: (hg, b, 0)),
                  pl.BlockSpec((hp, seq // tk, V_HEAD_DIM, tk), lambda b, hg, j: (hg, b, 0, 0)),
                  pl.BlockSpec((hp, N_META, QK_DIM), lambda b, hg, j: (hg, meta_block, 0)),
                  pl.BlockSpec((hp, 1, V_HEAD_DIM, tk), lambda b, hg, j: (hg, n_real_tiles * kt, 0, 0))],
        out_specs=pl.BlockSpec((tq, hp * V_HEAD_DIM), lambda b, hg, j: (b * n_q + j, hg)),
        out_shape=jax.ShapeDtypeStruct((n_tok, N_HEADS * V_HEAD_DIM), _BF16),
        scratch_shapes=[pltpu.VMEM((hp, tk, tq), _F32), pltpu.VMEM((hp, tk, tq), _F32),
                        pltpu.VMEM((hp, V_HEAD_DIM, tq), _F32)],
        compiler_params=_params(3),
        name="causal_attention",
    )(q, k, v, k, v)

    wrt1, br1 = _router_operands(router_g[1], router_g_bias[1], router_e[1], router_e_bias[1])
    route_shapes, route_specs = _route_out(n_real_tiles, tm)
    h3, xn2, ri2, rf2, cnt2 = pl.pallas_call(
        _oproj_kernel,
        grid=(n_real_tiles,),
        in_specs=[pl.BlockSpec((tm, N_HEADS * V_HEAD_DIM), lambda i: (i, 0)),
                  pl.BlockSpec((tm, d), lambda i: (i, 0)),
                  _full((N_HEADS * V_HEAD_DIM, d)), _full((1, d)),
                  _full((ROUTER_ROWS, d)), _full((ROUTER_ROWS, 1)), _full((tm, tm))],
        out_specs=[pl.BlockSpec((tm, d), lambda i: (i, 0)),
                   pl.BlockSpec((tm * parts, LANES), lambda i: (i, 0))] + route_specs,
        out_shape=[jax.ShapeDtypeStruct((n_tok, d), _F32),
                   jax.ShapeDtypeStruct((n_tok * parts, LANES), jnp.uint32)] + route_shapes,
        scratch_shapes=[pltpu.VMEM((N_EXPERTS, 1), _F32), pltpu.VMEM((ROUTER_SPLIT_ROWS, d), _BF16)],
        compiler_params=_params(1),
        name="attn_out_router",
    )(o, h2, b_w_o[0].astype(_BF16), row(ffn_norm[1]), wrt1, br1, triu)

    dest2, y2 = _moe_experts(xn2, ri2, cnt2, w_gate, w_up, w_down, 1, n_real_tiles, n_real_tiles, d)
    operands, in_specs, scratch = _combine_operands(dest2, rf2, h3, y2, d)
    out = pl.pallas_call(
        functools.partial(_combine_norm_kernel, tm=tm, n_tiles=n_real_tiles),
        grid=(n_real_tiles,),
        in_specs=in_specs + [_full((1, d))],
        out_specs=pl.BlockSpec((tm, d), lambda i: (i, 0)),
        out_shape=jax.ShapeDtypeStruct((n_tok, d), _F32),
        scratch_shapes=scratch,
        compiler_params=_params(1, disable_bounds_checks=True),
        name="moe_combine_final_norm",
    )(*operands, row(final_norm))
    return out.reshape(bsz, seq, d)
```

```python
import functools

import numpy as np
import jax
import jax.numpy as jnp
from jax import lax
from jax.experimental import pallas as pl
from jax.experimental.pallas import tpu as pltpu

N_META = 16
POOL_WINDOWS = (2, 4, 8, 16)
N_HEADS = 8
QK_NOPE_DIM = 128
QK_ROPE_DIM = 64
QK_DIM = QK_NOPE_DIM + QK_ROPE_DIM
V_HEAD_DIM = 128
KV_LORA_RANK = 256
ROPE_THETA = 10000.0
ATTN_SCALE = QK_DIM ** -0.5
Q_SCALE = ATTN_SCALE * 1.4426950408889634
N_EXPERT_GROUPS = 4
EXPERTS_PER_GROUP = 8
N_EXPERTS = N_EXPERT_GROUPS * EXPERTS_PER_GROUP
RMS_EPS = 1e-6
NEG_INF = -1e30

TOKEN_TILE = 512
POOL_BLOCK = 256
EXPERT_ROWS = 256
EXPERT_BLOCKS_PER_STEP = 2
ATTN_Q_TILE = 512
ATTN_K_TILE = ATTN_Q_TILE // 2
ATTN_HEADS_PER_STEP = 4
DMA_UNROLL = 8
DMA_THREADS = 2
ROUTER_ROWS = 8 + N_EXPERTS
ROUTER_LO_ROW = 48
ROUTER_SPLIT_ROWS = 2 * ROUTER_LO_ROW
VMEM_LIMIT_BYTES = 48 * 1024 * 1024

LANES = 128
_F32 = jnp.float32
_BF16 = jnp.bfloat16
_NT_DIMS = (((1,), (1,)), ((), ()))


def _params(n_grid_dims=1, **kw):
    return pltpu.CompilerParams(dimension_semantics=("arbitrary",) * n_grid_dims,
                                vmem_limit_bytes=VMEM_LIMIT_BYTES, **kw)


def _rms(x, g):
    ms = jnp.mean(x * x, axis=-1, keepdims=True)
    return x * lax.rsqrt(ms + RMS_EPS) * g


def _split_bf16(x):
    hi = x.astype(_BF16)
    lo = (x - hi.astype(_F32)).astype(_BF16)
    return hi, lo


def _dot(a, b):
    return jnp.dot(a, b, preferred_element_type=_F32)


def _store_token_tiles(ref, x):
    n, d = x.shape
    parts = d // LANES
    for s in range(parts):
        ref[pl.ds(s, n, stride=parts), :] = x[:, s * LANES:(s + 1) * LANES]


def _load_token_tiles(ref, first_token, n, parts):
    return jnp.concatenate([ref[pl.ds(first_token * parts + s, n, stride=parts), :] for s in range(parts)], axis=1)


_HIGH_HALF = 0xFFFF0000


def _pack_rows(x):
    n, d = x.shape
    words = []
    for s in range(d // (2 * LANES)):
        lo = x[:, 2 * s * LANES:(2 * s + 1) * LANES].astype(_BF16).astype(_F32)
        hi = x[:, (2 * s + 1) * LANES:(2 * s + 2) * LANES].astype(_BF16).astype(_F32)
        words.append(lax.shift_right_logical(lax.bitcast_convert_type(lo, jnp.uint32), jnp.uint32(16))
                     | (lax.bitcast_convert_type(hi, jnp.uint32) & jnp.uint32(_HIGH_HALF)))
    return jnp.concatenate(words, axis=1)


def _unpack_rows(w):
    cols = []
    for s in range(w.shape[1] // LANES):
        ws = w[:, s * LANES:(s + 1) * LANES]
        cols.append(lax.bitcast_convert_type(lax.shift_left(ws, jnp.uint32(16)), _F32))
        cols.append(lax.bitcast_convert_type(ws & jnp.uint32(_HIGH_HALF), _F32))
    return jnp.concatenate(cols, axis=1)


def _route_init(wrt_ref, wsplit_ref, base_ref):
    base_ref[...] = jnp.zeros(base_ref.shape, base_ref.dtype)
    w_hi, w_lo = _split_bf16(wrt_ref[...])
    wsplit_ref[...] = jnp.zeros(wsplit_ref.shape, wsplit_ref.dtype)
    wsplit_ref[0:ROUTER_ROWS, :] = w_hi
    wsplit_ref[ROUTER_LO_ROW:ROUTER_LO_ROW + ROUTER_ROWS, :] = w_lo


def _route(xn, valid, wsplit_ref, br_ref, triu_ref, base_ref, ri_ref, rf_ref, cnt_ref):
    tm = xn.shape[0]
    x_hi, x_lo = _split_bf16(xn)
    by_hi = lax.dot_general(wsplit_ref[...], x_hi, _NT_DIMS, preferred_element_type=_F32)
    by_lo = lax.dot_general(wsplit_ref[0:ROUTER_LO_ROW, :], x_lo, _NT_DIMS, preferred_element_type=_F32)
    logits = (by_hi[0:ROUTER_ROWS] + by_hi[ROUTER_LO_ROW:ROUTER_LO_ROW + ROUTER_ROWS]
              + by_lo[0:ROUTER_ROWS] + br_ref[...])
    lg = logits[0:N_EXPERT_GROUPS]
    eg = jnp.exp(lg - jnp.max(lg, axis=0, keepdims=True))
    pg = eg / jnp.sum(eg, axis=0, keepdims=True)
    w_g = jnp.max(pg, axis=0, keepdims=True)
    ig = lax.broadcasted_iota(jnp.int32, pg.shape, 0).astype(_F32)
    g_sel = jnp.min(jnp.where(pg == w_g, ig, float(N_EXPERT_GROUPS)), axis=0, keepdims=True)

    sel = logits[8:8 + EXPERTS_PER_GROUP]
    for g in range(1, N_EXPERT_GROUPS):
        sel = jnp.where(g_sel == float(g), logits[8 + g * EXPERTS_PER_GROUP:8 + (g + 1) * EXPERTS_PER_GROUP], sel)
    ie = lax.broadcasted_iota(jnp.int32, sel.shape, 0).astype(_F32)
    v1 = jnp.max(sel, axis=0, keepdims=True)
    i1 = jnp.min(jnp.where(sel == v1, ie, float(EXPERTS_PER_GROUP)), axis=0, keepdims=True)
    rest = jnp.where(ie == i1, -jnp.inf, sel)
    v2 = jnp.max(rest, axis=0, keepdims=True)
    i2 = jnp.min(jnp.where(rest == v2, ie, float(EXPERTS_PER_GROUP)), axis=0, keepdims=True)
    e2 = jnp.exp(v2 - v1)
    den = 1.0 + e2
    validf = valid.astype(_F32)
    gate0 = w_g * (1.0 / den) * validf
    gate1 = w_g * (e2 / den) * validf
    eid0 = g_sel * float(EXPERTS_PER_GROUP) + i1
    eid1 = g_sel * float(EXPERTS_PER_GROUP) + i2

    iall = lax.broadcasted_iota(jnp.int32, (N_EXPERTS, tm), 0).astype(_F32)
    oh0 = jnp.where(iall == eid0, validf, 0.0)
    oh1 = jnp.where(iall == eid1, validf, 0.0)
    both = oh0 + oh1
    before = _dot(both.astype(_BF16), triu_ref[...]) + base_ref[...]
    rank0 = jnp.sum(oh0 * before, axis=0, keepdims=True)
    rank1 = jnp.sum(oh1 * before, axis=0, keepdims=True)
    base_ref[...] = base_ref[...] + jnp.sum(both, axis=1, keepdims=True)

    ri_ref[...] = jnp.zeros(ri_ref.shape, ri_ref.dtype)
    rf_ref[...] = jnp.zeros(rf_ref.shape, rf_ref.dtype)
    ri_ref[0, 0:1, :] = eid0.astype(jnp.int32)
    ri_ref[0, 1:2, :] = eid1.astype(jnp.int32)
    ri_ref[0, 2:3, :] = rank0.astype(jnp.int32)
    ri_ref[0, 3:4, :] = rank1.astype(jnp.int32)
    rf_ref[0, 0:1, :] = gate0
    rf_ref[0, 1:2, :] = gate1
    cnt_ref[...] = jnp.broadcast_to(base_ref[...], cnt_ref.shape).astype(jnp.int32)


def _mixer_kernel(x_ref, xh_ref, mp_ref, meta_ref, anorm_ref, aw_ref, ascale_ref, fnorm_ref,
                  wrt_ref, br_ref, pm_ref, ph_ref, triu_ref,
                  h_ref, xn_ref, ri_ref, rf_ref, cnt_ref, base_ref, wsplit_ref, *, n_real_tiles, tiles_per_batch):
    i = pl.program_id(0)
    tm = x_ref.shape[0]
    gd = x_ref.shape[1] // len(POOL_WINDOWS)
    is_meta = i == n_real_tiles
    first = (i % tiles_per_batch) == 0

    @pl.when(i == 0)
    def _():
        _route_init(wrt_ref, wsplit_ref, base_ref)

    h = jnp.where(is_meta, mp_ref[...], x_ref[...])
    halo = jnp.where(is_meta, 0.0, jnp.where(first, meta_ref[...], xh_ref[...]))
    hn = _rms(h, anorm_ref[...])
    hh = _rms(halo, anorm_ref[...])
    hn_hi, hn_lo = _split_bf16(hn)
    hh_hi, hh_lo = _split_bf16(hh)
    pb = pm_ref.shape[1]
    for c in range(tm // pb):
        rows = slice(c * pb, (c + 1) * pb)
        row = c * pb + lax.broadcasted_iota(jnp.int32, (pb, 1), 0)
        for g, w in enumerate(POOL_WINDOWS):
            sl = slice(g * gd, (g + 1) * gd)
            win = _dot(pm_ref[g], hn_hi[rows, sl]) + _dot(pm_ref[g], hn_lo[rows, sl])
            if c == 0:
                before_hi, before_lo = hh_hi[:, sl], hh_lo[:, sl]
            else:
                before_hi, before_lo = hn_hi[c * pb - N_META:c * pb, sl], hn_lo[c * pb - N_META:c * pb, sl]
            top = win[:N_META] + _dot(ph_ref[g], before_hi) + _dot(ph_ref[g], before_lo)
            win = jnp.concatenate([top, win[N_META:]], axis=0)
            cnt = jnp.where(is_meta, jnp.minimum(row + 1, w), w).astype(_F32)
            pooled = win * (1.0 / cnt) - hn[rows, sl]
            mix = _dot(pooled.astype(_BF16), aw_ref[g])
            h_ref[rows, sl] = h[rows, sl] + mix * ascale_ref[:, sl]

    xn = _rms(h_ref[...], fnorm_ref[...])
    _store_token_tiles(xn_ref, _pack_rows(xn))
    lane = lax.broadcasted_iota(jnp.int32, (1, tm), 1)
    valid = jnp.logical_or(jnp.logical_not(is_meta), lane < N_META)
    _route(xn, valid, wsplit_ref, br_ref, triu_ref, base_ref, ri_ref, rf_ref, cnt_ref)


def _index_copy(dest_hbm, idx_smem, isem, tile, n):
    return pltpu.make_async_copy(dest_hbm.at[pl.ds(pl.multiple_of(tile * n, n), n)],
                                 idx_smem.at[pl.ds(pl.multiple_of((tile % 2) * n, n), n)], isem.at[tile % 2])


def _dispatch_kernel(dest_hbm, xn_hbm, xs_hbm, idx_smem, xbuf, isem, lsem, ssem, *, tm, n_tiles, n_real_tiles):
    i = pl.program_id(0)
    parts = xbuf.shape[1] // tm
    buf = i % 3

    def idx_copy(tile):
        return _index_copy(dest_hbm, idx_smem, isem, tile, 2 * tm)

    def load(tile):
        rows = tm * parts
        return pltpu.make_async_copy(xn_hbm.at[pl.ds(pl.multiple_of(tile * rows, rows), rows), :],
                                     xbuf.at[tile % 3], lsem.at[tile % 3])

    def wait_scatters(tile):
        def wait_rows(n_tok):
            for k in range(2):
                pltpu.make_async_copy(xbuf.at[tile % 3, pl.ds(0, n_tok * parts), :],
                                      xs_hbm.at[pl.ds(0, n_tok * parts), :], ssem.at[tile % 3]).wait()

        @pl.when(tile < n_real_tiles)
        def _():
            wait_rows(tm)

        @pl.when(tile >= n_real_tiles)
        def _():
            wait_rows(N_META)

    def scatter(n_tok):
        def issue(g, c):
            for u in range(min(DMA_UNROLL, n_tok)):
                t = g * min(DMA_UNROLL, n_tok) + u
                src = xbuf.at[buf, pl.ds(pl.multiple_of(t * parts, parts), parts), :]
                for k in range(2):
                    d = idx_smem[(i % 2) * (2 * tm) + k * tm + t]
                    pltpu.make_async_copy(src, xs_hbm.at[pl.ds(pl.multiple_of(d * parts, parts), parts), :],
                                          ssem.at[buf]).start(priority=(2 * u + k) % DMA_THREADS)
            return c

        lax.fori_loop(0, n_tok // min(DMA_UNROLL, n_tok), issue, 0)

    @pl.when(i == 0)
    def _():
        idx_copy(0).start()
        load(0).start()

    @pl.when(i >= 2)
    def _():
        wait_scatters(i - 2)

    @pl.when(i + 1 < n_tiles)
    def _():
        idx_copy(i + 1).start()
        load(i + 1).start()

    idx_copy(i).wait()
    load(i).wait()

    @pl.when(i < n_real_tiles)
    def _():
        scatter(tm)

    @pl.when(i >= n_real_tiles)
    def _():
        scatter(N_META)

    @pl.when(i == n_tiles - 1)
    def _():
        if n_tiles >= 2:
            wait_scatters(i - 1)
        wait_scatters(i)


def _expert_kernel(be_ref, bv_ref, nu_ref, xs_ref, *refs):
    nb = EXPERT_BLOCKS_PER_STEP
    w_refs, y_ref, wb_refs = refs[:3 * nb], refs[3 * nb], refs[3 * nb + 1:]
    i = pl.program_id(0)
    n_used = nu_ref[0]
    parts = wb_refs[0].shape[0] // (2 * LANES)
    tr = xs_ref.shape[0] // (nb * parts)

    for c in range(nb):
        blk = i * nb + c
        changed = be_ref[blk] != be_ref[jnp.maximum(blk - nb, 0)]

        @pl.when(jnp.logical_or(i == 0, jnp.logical_and(blk < n_used, changed)))
        def _(c=c):
            for w_ref, wb_ref in zip(w_refs[3 * c:3 * c + 3], wb_refs[3 * c:3 * c + 3]):
                wb_ref[...] = w_ref[0, 0].astype(_BF16)

    @pl.when(i * nb < n_used)
    def _():
        rows = lax.broadcasted_iota(jnp.int32, (tr, 1), 0)
        for c in range(nb):
            blk = i * nb + c
            wgb_ref, wub_ref, wdb_ref = wb_refs[3 * c:3 * c + 3]
            x = jnp.where(rows < bv_ref[blk], _unpack_rows(_load_token_tiles(xs_ref, c * tr, tr, parts)), 0.0)
            x = x.astype(_BF16)
            g = _dot(x, wgb_ref[...])
            u = _dot(x, wub_ref[...])
            a = g * (1.0 / (1.0 + jnp.exp(-g))) * u
            y = jnp.where(blk < n_used, _dot(a.astype(_BF16), wdb_ref[...]), 0.0)
            _store_token_tiles(y_ref.at[pl.ds(c * tr * parts, tr * parts), :], _pack_rows(y))

    @pl.when(i * nb >= n_used)
    def _():
        y_ref[...] = jnp.zeros(y_ref.shape, y_ref.dtype)


def _combined_tile(dest_hbm, rf_ref, h_ref, y_hbm, idx_smem, ybufs, isem, rsem, *, tm, n_tiles, slot):
    i = pl.program_id(0)
    parts = ybufs[0].shape[0] // (2 * tm)

    def idx_copy(tile):
        return _index_copy(dest_hbm, idx_smem, isem, tile, 2 * tm)

    def row_copy(idx_slot, buf_slot, t, k, u):
        d = idx_smem[idx_slot * (2 * tm) + k * tm + t]
        row = (k * tm + t) * parts
        if not isinstance(t, int):
            row = pl.multiple_of(row, parts)
        return pltpu.make_async_copy(
            y_hbm.at[pl.ds(pl.multiple_of(d * parts, parts), parts), :],
            ybufs[buf_slot].at[pl.ds(row, parts), :],
            rsem.at[buf_slot]).start(priority=(2 * u + k) % DMA_THREADS)

    def wait_rows(buf_slot):
        pltpu.make_async_copy(y_hbm.at[pl.ds(0, 2 * tm * parts), :], ybufs[buf_slot], rsem.at[buf_slot]).wait()

    if slot == 0:
        @pl.when(i == 0)
        def _():
            idx_copy(0).start()
            idx_copy(0).wait()

            def issue(g, c):
                for u in range(DMA_UNROLL):
                    for k in range(2):
                        row_copy(0, 0, g * DMA_UNROLL + u, k, u)
                return c

            lax.fori_loop(0, tm // DMA_UNROLL, issue, 0)
            if n_tiles >= 2:
                idx_copy(1).start()

    @pl.when(i + 1 < n_tiles)
    def _():
        idx_copy(i + 1).wait()

    @pl.when(i + 2 < n_tiles)
    def _():
        idx_copy(i + 2).start()

    nxt_idx_slot = jnp.minimum(i + 1, n_tiles - 1) % 2

    def prefetch(part, n_parts):
        for t in range(part * tm // n_parts, (part + 1) * tm // n_parts):
            for k in range(2):
                row_copy(nxt_idx_slot, 1 - slot, t, k, t)

    gates = rf_ref[0]
    gt = jnp.concatenate([gates, jnp.zeros((LANES - gates.shape[0], tm), _F32)], axis=0).T
    wait_rows(slot)
    yb = ybufs[slot]

    def combined(r0, n):
        return h_ref[r0:r0 + n, :] + (_unpack_rows(_load_token_tiles(yb, r0, n, parts)) * gt[r0:r0 + n, 0:1]
                                      + _unpack_rows(_load_token_tiles(yb, tm + r0, n, parts)) * gt[r0:r0 + n, 1:2])

    def drain():
        @pl.when(i == n_tiles - 1)
        def _():
            wait_rows(1 - slot)

    return combined, prefetch, drain


def _for_each_parity(body):
    for slot in range(2):
        @pl.when(pl.program_id(0) % 2 == slot)
        def _(slot=slot):
            body(slot)


def _combine_norm_kernel(dest_hbm, rf_ref, h_ref, y_hbm, fnorm_ref, out_ref,
                         idx_smem, ybuf0, ybuf1, isem, rsem, *, tm, n_tiles):
    def body(slot):
        combined, prefetch, drain = _combined_tile(dest_hbm, rf_ref, h_ref, y_hbm, idx_smem, (ybuf0, ybuf1),
                                                   isem, rsem, tm=tm, n_tiles=n_tiles, slot=slot)
        n_chunks = 4
        n = tm // n_chunks
        for c in range(n_chunks):
            prefetch(c, n_chunks)
            out_ref[c * n:(c + 1) * n, :] = _rms(combined(c * n, n), fnorm_ref[...])
        drain()

    _for_each_parity(body)


def _combine_proj_kernel(dest_hbm, rf_ref, h_ref, y_hbm, cos_ref, sin_ref, kvn_ref, wdkv_ref, kvlat_ref,
                         wuk_ref, wuv_ref, bnorm_ref, wdq_ref, qnorm_ref, wuq_ref,
                         h_out_ref, q_ref, k_ref, v_ref, idx_smem, ybuf0, ybuf1, isem, rsem, *, tm, n_tiles):
    def body(slot):
        combined, prefetch, drain = _combined_tile(dest_hbm, rf_ref, h_ref, y_hbm, idx_smem, (ybuf0, ybuf1),
                                                   isem, rsem, tm=tm, n_tiles=n_tiles, slot=slot)
        n = v_ref.shape[3]
        n_chunks = tm // n
        for c in range(n_chunks):
            prefetch(c, n_chunks)
            h = combined(c * n, n)
            h_out_ref[c * n:(c + 1) * n, :] = h
            _project(h, c, cos_ref, sin_ref, kvn_ref, wdkv_ref, kvlat_ref, wuk_ref, wuv_ref,
                     bnorm_ref, wdq_ref, qnorm_ref, wuq_ref, q_ref, k_ref, v_ref)
        drain()

    _for_each_parity(body)


def _rope128(x, cos_t, sin_t):
    lane = lax.broadcasted_iota(jnp.int32, (1, 128), 1)
    first_half = (lane % QK_ROPE_DIM) < (QK_ROPE_DIM // 2)
    swapped = jnp.where(first_half, pltpu.roll(x, 128 - QK_ROPE_DIM // 2, axis=1),
                        pltpu.roll(x, QK_ROPE_DIM // 2, axis=1))
    return x * cos_t + swapped * sin_t


def _project(h, chunk, cos_ref, sin_ref, kvn_ref, wdkv_ref, kvlat_ref, wuk_ref, wuv_ref,
             bnorm_ref, wdq_ref, qnorm_ref, wuq_ref, q_ref, k_ref, v_ref):
    n = h.shape[0]
    rows = slice(chunk * n, (chunk + 1) * n)
    cos_t = cos_ref[rows, :]
    sin_t = sin_ref[rows, :]
    c = _dot(_rms(h, kvn_ref[...]).astype(_BF16), wdkv_ref[...])
    ckv = _rms(c[:, :KV_LORA_RANK], kvlat_ref[...]).astype(_BF16)
    kr = _rope128(c[:, KV_LORA_RANK:KV_LORA_RANK + 128], cos_t, sin_t)[:, :QK_ROPE_DIM].astype(_BF16)
    kn = _dot(ckv, wuk_ref[...])
    vv = _dot(ckv, wuv_ref[...])
    cq = _rms(_dot(_rms(h, bnorm_ref[...]).astype(_BF16), wdq_ref[...]), qnorm_ref[...]).astype(_BF16)
    q = _dot(cq, wuq_ref[...])
    rope0 = N_HEADS * QK_NOPE_DIM
    for hd in range(N_HEADS):
        k_ref[hd, rows, 0:QK_NOPE_DIM] = kn[:, hd * QK_NOPE_DIM:(hd + 1) * QK_NOPE_DIM].astype(_BF16)
        k_ref[hd, rows, QK_NOPE_DIM:QK_DIM] = kr
        v_ref[hd, chunk] = vv[:, hd * V_HEAD_DIM:(hd + 1) * V_HEAD_DIM].T.astype(_BF16)
        q_ref[hd, rows, 0:QK_NOPE_DIM] = (q[:, hd * QK_NOPE_DIM:(hd + 1) * QK_NOPE_DIM] * Q_SCALE).astype(_BF16)
        qr = _rope128(q[:, rope0 + hd * 128:rope0 + (hd + 1) * 128], cos_t, sin_t)[:, :QK_ROPE_DIM]
        q_ref[hd, rows, QK_NOPE_DIM:QK_DIM] = (qr * Q_SCALE).astype(_BF16)


def _attn_kernel(q_ref, k_ref, vt_ref, km_ref, vmt_ref, o_ref, sa_ref, sb_ref, acc_ref, *, tq, tk):
    j = pl.program_id(2)
    heads = range(q_ref.shape[0])
    vd = acc_ref.shape[1]

    def scores(kb, s_ref):
        start = pl.multiple_of(kb * tk, tk)
        for hd in heads:
            s_ref[hd] = lax.dot_general(k_ref[hd, pl.ds(start, tk), :], q_ref[hd], _NT_DIMS,
                                        preferred_element_type=_F32)

    def update(kb, s_ref, carry, masked):
        out = []
        for hd in heads:
            m, l = carry[hd]
            s = s_ref[hd]
            if masked:
                keys = kb * tk + lax.broadcasted_iota(jnp.int32, (tk, tq), 0)
                qpos = j * tq + lax.broadcasted_iota(jnp.int32, (tk, tq), 1)
                s = jnp.where(keys <= qpos, s, NEG_INF)
            m_new = jnp.maximum(m, jnp.max(s, axis=0, keepdims=True))
            alpha = jnp.exp2(m - m_new)
            p = jnp.exp2(s - m_new)
            l = alpha * l + jnp.sum(p, axis=0, keepdims=True)
            acc_ref[hd] = acc_ref[hd] * alpha + _dot(vt_ref[hd, kb], p.astype(_BF16))
            out.append((m_new, l))
        return tuple(out)

    scores(0, sa_ref)
    carry = []
    for hd in heads:
        s0 = lax.dot_general(km_ref[hd], q_ref[hd], _NT_DIMS, preferred_element_type=_F32)
        m = jnp.max(s0, axis=0, keepdims=True)
        p0 = jnp.exp2(s0 - m)
        carry.append((m, jnp.sum(p0, axis=0, keepdims=True)))
        acc_ref[hd] = _dot(vmt_ref[hd, 0, :, 0:N_META], p0.astype(_BF16))
    carry = tuple(carry)

    def pair(kp, carry):
        scores(2 * kp + 1, sb_ref)
        carry = update(2 * kp, sa_ref, carry, False)
        scores(2 * kp + 2, sa_ref)
        return update(2 * kp + 1, sb_ref, carry, False)

    carry = lax.fori_loop(0, j, pair, carry)
    scores(2 * j + 1, sb_ref)
    carry = update(2 * j, sa_ref, carry, True)
    carry = update(2 * j + 1, sb_ref, carry, True)
    for hd in heads:
        o_ref[:, hd * vd:(hd + 1) * vd] = (acc_ref[hd] / carry[hd][1]).T.astype(o_ref.dtype)


def _oproj_kernel(o_ref, h_ref, wo_ref, fnorm_ref, wrt_ref, br_ref, triu_ref,
                  h_out_ref, xn_ref, ri_ref, rf_ref, cnt_ref, base_ref, wsplit_ref):
    i = pl.program_id(0)

    @pl.when(i == 0)
    def _():
        _route_init(wrt_ref, wsplit_ref, base_ref)

    h = h_ref[...] + _dot(o_ref[...], wo_ref[...])
    h_out_ref[...] = h
    xn = _rms(h, fnorm_ref[...])
    _store_token_tiles(xn_ref, _pack_rows(xn))
    valid = lax.broadcasted_iota(jnp.int32, (1, h.shape[0]), 1) >= 0
    _route(xn, valid, wsplit_ref, br_ref, triu_ref, base_ref, ri_ref, rf_ref, cnt_ref)


def _full(shape):
    nd = len(shape)
    return pl.BlockSpec(shape, lambda *_: (0,) * nd)


def _router_operands(router_g, router_g_bias, router_e, router_e_bias):
    d = router_g.shape[0]
    wrt = jnp.concatenate([router_g.T, jnp.zeros((8 - N_EXPERT_GROUPS, d), _F32), router_e.T], axis=0)
    br = jnp.concatenate([router_g_bias, jnp.zeros((8 - N_EXPERT_GROUPS,), _F32), router_e_bias])[:, None]
    return wrt.astype(_F32), br.astype(_F32)


def _route_out(n_tiles, tm):
    shapes = [jax.ShapeDtypeStruct((n_tiles, 8, tm), jnp.int32),
              jax.ShapeDtypeStruct((n_tiles, 8, tm), _F32),
              jax.ShapeDtypeStruct((N_EXPERTS, 128), jnp.int32)]
    specs = [pl.BlockSpec((1, 8, tm), lambda i: (i, 0, 0)),
             pl.BlockSpec((1, 8, tm), lambda i: (i, 0, 0)),
             pl.BlockSpec((N_EXPERTS, 128), lambda i: (0, 0))]
    return shapes, specs


def _moe_experts(xn, ri, counts, w_gate, w_up, w_down, layer, n_tiles, n_real_tiles, d):
    tm, tr = TOKEN_TILE, EXPERT_ROWS
    parts = d // (2 * LANES)
    n_valid = n_real_tiles * tm + (n_tiles - n_real_tiles) * N_META
    n_blocks = -(-(2 * n_valid + N_EXPERTS * (tr - 1)) // tr)
    n_blocks = -(-n_blocks // EXPERT_BLOCKS_PER_STEP) * EXPERT_BLOCKS_PER_STEP
    n_rows = n_blocks * tr

    counts = counts[:, 0]
    padded = (counts + tr - 1) // tr * tr
    pends = jnp.cumsum(padded)
    pstarts = pends - padded
    n_used = (pends[-1] // tr).astype(jnp.int32).reshape(1)
    blk0 = jnp.arange(n_blocks, dtype=jnp.int32) * tr
    block_e = jnp.minimum(jnp.sum(blk0[:, None] >= pends[None, :], axis=1), N_EXPERTS - 1).astype(jnp.int32)
    experts = jnp.arange(N_EXPERTS, dtype=jnp.int32)
    block_oh = block_e[:, None] == experts[None, :]
    block_cnt = jnp.sum(jnp.where(block_oh, counts[None, :], 0), axis=1)
    block_start = jnp.sum(jnp.where(block_oh, pstarts[None, :], 0), axis=1)
    block_valid = jnp.clip(block_cnt - (blk0 - block_start), 0, tr).astype(jnp.int32)
    eid = ri[:, 0:2, :]
    slot0 = jnp.sum(jnp.where(eid[..., None] == experts, pstarts, 0), axis=-1)
    dest = (slot0 + ri[:, 2:4, :]).astype(jnp.int32).reshape(n_tiles * 2 * tm)

    xs = pl.pallas_call(
        functools.partial(_dispatch_kernel, tm=tm, n_tiles=n_tiles, n_real_tiles=n_real_tiles),
        grid=(n_tiles,),
        in_specs=[pl.BlockSpec(memory_space=pl.ANY), pl.BlockSpec(memory_space=pl.ANY)],
        out_specs=pl.BlockSpec(memory_space=pl.ANY),
        out_shape=jax.ShapeDtypeStruct((n_rows * parts, LANES), jnp.uint32),
        scratch_shapes=[pltpu.SMEM((4 * tm,), jnp.int32), pltpu.VMEM((3, tm * parts, LANES), jnp.uint32),
                        pltpu.SemaphoreType.DMA((2,)), pltpu.SemaphoreType.DMA((3,)),
                        pltpu.SemaphoreType.DMA((3,))],
        compiler_params=_params(1, has_side_effects=True, disable_bounds_checks=True),
        name="moe_dispatch",
    )(dest, xn)

    f = w_gate.shape[3]
    nb = EXPERT_BLOCKS_PER_STEP
    step_rows = nb * tr * parts

    def expert_block(c):
        return lambda i, be, bv, nu: (layer, be[jnp.minimum(i * nb + c, nu[0] - 1)], 0, 0)

    weight_specs, weight_scratch = [], []
    for c in range(nb):
        weight_specs += [pl.BlockSpec((1, 1, d, f), expert_block(c)), pl.BlockSpec((1, 1, d, f), expert_block(c)),
                         pl.BlockSpec((1, 1, f, d), expert_block(c))]
        weight_scratch += [pltpu.VMEM((d, f), _BF16), pltpu.VMEM((d, f), _BF16), pltpu.VMEM((f, d), _BF16)]
    y = pl.pallas_call(
        _expert_kernel,
        grid_spec=pltpu.PrefetchScalarGridSpec(
            num_scalar_prefetch=3,
            grid=(n_blocks // nb,),
            in_specs=[pl.BlockSpec((step_rows, LANES),
                                   lambda i, be, bv, nu: (jnp.minimum(i, (nu[0] - 1) // nb), 0))] + weight_specs,
            out_specs=pl.BlockSpec((step_rows, LANES), lambda i, be, bv, nu: (i, 0)),
            scratch_shapes=weight_scratch),
        out_shape=jax.ShapeDtypeStruct((n_rows * parts, LANES), jnp.uint32),
        compiler_params=_params(1),
        name="moe_experts",
    )(block_e, block_valid, n_used, xs, *([w_gate, w_up, w_down] * nb))

    return dest, y


def _combine_operands(dest, rf, h, y, d):
    tm = TOKEN_TILE
    parts = d // (2 * LANES)
    in_specs = [pl.BlockSpec(memory_space=pl.ANY),
                pl.BlockSpec((1, 8, tm), lambda i: (i, 0, 0)),
                pl.BlockSpec((tm, d), lambda i: (i, 0)),
                pl.BlockSpec(memory_space=pl.ANY)]
    scratch = [pltpu.SMEM((4 * tm,), jnp.int32),
               pltpu.VMEM((2 * tm * parts, LANES), jnp.uint32), pltpu.VMEM((2 * tm * parts, LANES), jnp.uint32),
               pltpu.SemaphoreType.DMA((2,)), pltpu.SemaphoreType.DMA((2,))]
    return (dest, rf, h, y), in_specs, scratch


def kernel(x, meta_tokens, a_norm, a_w, a_scale, b_norm, b_w_dq, b_q_norm, b_w_uq, b_w_o, kv_norm, w_dkv,
           kv_lat_norm, w_uk, w_uv, ffn_norm, router_g, router_g_bias, router_e, router_e_bias, w_gate, w_up,
           w_down, final_norm):
    bsz, seq, d = x.shape
    tm = TOKEN_TILE
    assert seq % tm == 0 and seq % ATTN_Q_TILE == 0 and ATTN_Q_TILE == 2 * ATTN_K_TILE and tm % ATTN_K_TILE == 0
    kt = tm // ATTN_K_TILE
    assert d % (LANES * len(POOL_WINDOWS)) == 0 and N_META == max(POOL_WINDOWS) and N_META <= tm
    parts = d // (2 * LANES)
    n_tok = bsz * seq
    n_real_tiles = n_tok // tm
    tiles_per_batch = seq // tm
    n_tiles = n_real_tiles + 1
    gd = d // len(POOL_WINDOWS)
    row = lambda v: v.reshape(1, -1).astype(_F32)

    x2 = x.reshape(n_tok, d)
    meta_pad = jnp.concatenate([meta_tokens, jnp.zeros((tm - N_META, d), x.dtype)], axis=0)

    r = np.arange(tm)[:, None]
    cidx = np.arange(tm)[None, :]
    pb = min(POOL_BLOCK, tm)
    pm = np.stack([((r[:pb] - cidx[:, :pb] >= 0) & (r[:pb] - cidx[:, :pb] < w)) for w in POOL_WINDOWS]).astype(np.float32)
    hc = np.arange(N_META)[None, :]
    ph = np.stack([(r[:N_META] + N_META - hc < w) for w in POOL_WINDOWS]).astype(np.float32)
    triu = (r < cidx).astype(np.float32)
    pm, ph, triu = (jnp.asarray(a, dtype=_BF16) for a in (pm, ph, triu))

    wrt0, br0 = _router_operands(router_g[0], router_g_bias[0], router_e[0], router_e_bias[0])
    route_shapes, route_specs = _route_out(n_tiles, tm)
    tile_or_last = lambda i: (jnp.minimum(i, n_real_tiles - 1), 0)
    halo_blocks = tm // N_META
    h1, xn1, ri1, rf1, cnt1 = pl.pallas_call(
        functools.partial(_mixer_kernel, n_real_tiles=n_real_tiles, tiles_per_batch=tiles_per_batch),
        grid=(n_tiles,),
        in_specs=[pl.BlockSpec((tm, d), tile_or_last),
                  pl.BlockSpec((N_META, d), lambda i: (jnp.clip(i * halo_blocks - 1, 0, n_tok // N_META - 1), 0)),
                  _full((tm, d)), _full((N_META, d)), _full((1, d)),
                  _full((len(POOL_WINDOWS), gd, gd)), _full((1, d)), _full((1, d)),
                  _full((ROUTER_ROWS, d)), _full((ROUTER_ROWS, 1)),
                  _full(pm.shape), _full(ph.shape), _full((tm, tm))],
        out_specs=[pl.BlockSpec((tm, d), lambda i: (i, 0)),
                   pl.BlockSpec((tm * parts, LANES), lambda i: (i, 0))] + route_specs,
        out_shape=[jax.ShapeDtypeStruct((n_tiles * tm, d), _F32),
                   jax.ShapeDtypeStruct((n_tiles * tm * parts, LANES), jnp.uint32)] + route_shapes,
        scratch_shapes=[pltpu.VMEM((N_EXPERTS, 1), _F32), pltpu.VMEM((ROUTER_SPLIT_ROWS, d), _BF16)],
        compiler_params=_params(1),
        name="pool_mixer_router",
    )(x2, x2, meta_pad, meta_tokens, row(a_norm[0]), a_w[0].astype(_BF16), row(a_scale[0]), row(ffn_norm[0]),
      wrt0, br0, pm, ph, triu)

    dest1, y1 = _moe_experts(xn1, ri1, cnt1, w_gate, w_up, w_down, 0, n_tiles, n_real_tiles, d)

    pos = jnp.concatenate([jnp.arange(seq, dtype=_F32) + N_META, jnp.arange(tm, dtype=_F32)])
    inv_freq = ROPE_THETA ** (-jnp.arange(0, QK_ROPE_DIM, 2, dtype=_F32) / QK_ROPE_DIM)
    ang = pos[:, None] * inv_freq[None, :]
    cos_t = jnp.tile(jnp.cos(ang), (1, 4))
    sin_t = jnp.tile(jnp.concatenate([-jnp.sin(ang), jnp.sin(ang)], axis=1), (1, 2))

    wdkv = jnp.concatenate([w_dkv, w_dkv[:, KV_LORA_RANK:]], axis=1).astype(_BF16)
    wuk = w_uk.reshape(KV_LORA_RANK, N_HEADS * QK_NOPE_DIM).astype(_BF16)
    wuv = w_uv.reshape(KV_LORA_RANK, N_HEADS * V_HEAD_DIM).astype(_BF16)
    wuq = b_w_uq[0]
    q_rank = wuq.shape[0]
    wuq_rope = wuq[:, :, QK_NOPE_DIM:]
    wuq = jnp.concatenate([wuq[:, :, :QK_NOPE_DIM].reshape(q_rank, -1),
                           jnp.concatenate([wuq_rope, wuq_rope], axis=2).reshape(q_rank, -1)], axis=1).astype(_BF16)
    wdq = b_w_dq[0].astype(_BF16)
    pos_tile = lambda i: (jnp.where(i < n_real_tiles, i % tiles_per_batch, tiles_per_batch), 0)
    head_tile = lambda i: (0, i, 0)
    operands, in_specs, scratch = _combine_operands(dest1, rf1, h1, y1, d)
    h2, q, k, v = pl.pallas_call(
        functools.partial(_combine_proj_kernel, tm=tm, n_tiles=n_tiles),
        grid=(n_tiles,),
        in_specs=in_specs + [
            pl.BlockSpec((tm, 128), pos_tile), pl.BlockSpec((tm, 128), pos_tile),
            _full((1, d)), _full(wdkv.shape), _full((1, KV_LORA_RANK)), _full(wuk.shape), _full(wuv.shape),
            _full((1, d)), _full(wdq.shape), _full((1, q_rank)), _full(wuq.shape)],
        out_specs=[pl.BlockSpec((tm, d), lambda i: (i, 0)),
                   pl.BlockSpec((N_HEADS, tm, QK_DIM), head_tile),
                   pl.BlockSpec((N_HEADS, tm, QK_DIM), head_tile),
                   pl.BlockSpec((N_HEADS, kt, V_HEAD_DIM, ATTN_K_TILE), lambda i: (0, i, 0, 0))],
        out_shape=[jax.ShapeDtypeStruct((n_tiles * tm, d), _F32),
                   jax.ShapeDtypeStruct((N_HEADS, n_tiles * tm, QK_DIM), _BF16),
                   jax.ShapeDtypeStruct((N_HEADS, n_tiles * tm, QK_DIM), _BF16),
                   jax.ShapeDtypeStruct((N_HEADS, n_tiles * kt, V_HEAD_DIM, ATTN_K_TILE), _BF16)],
        scratch_shapes=scratch,
        compiler_params=_params(1, disable_bounds_checks=True),
        name="moe_combine_latent_qkv",
    )(*operands, cos_t, sin_t, row(kv_norm), wdkv, row(kv_lat_norm), wuk, wuv,
      row(b_norm[0]), wdq, row(b_q_norm[0]), wuq)

    tq, tk, hp = ATTN_Q_TILE, ATTN_K_TILE, ATTN_HEADS_PER_STEP
    n_q = seq // tq
    meta_block = n_tok // N_META
    o = pl.pallas_call(
        functools.partial(_attn_kernel, tq=tq, tk=tk),
        grid=(bsz, N_HEADS // hp, n_q),
        in_specs=[pl.BlockSpec((hp, tq, QK_DIM), lambda b, hg, j: (hg, b * n_q + j, 0)),
                  pl.BlockSpec((hp, seq, QK_DIM), lambda b, hg, j: (hg, b, 0)),
                  pl.BlockSpec((hp, seq // tk, V_HEAD_DIM, tk), lambda b, hg, j: (hg, b, 0, 0)),
                  pl.BlockSpec((hp, N_META, QK_DIM), lambda b, hg, j: (hg, meta_block, 0)),
                  pl.BlockSpec((hp, 1, V_HEAD_DIM, tk), lambda b, hg, j: (hg, n_real_tiles * kt, 0, 0))],
        out_specs=pl.BlockSpec((tq, hp * V_HEAD_DIM), lambda b, hg, j: (b * n_q + j, hg)),
        out_shape=jax.ShapeDtypeStruct((n_tok, N_HEADS * V_HEAD_DIM), _BF16),
        scratch_shapes=[pltpu.VMEM((hp, tk, tq), _F32), pltpu.VMEM((hp, tk, tq), _F32),
                        pltpu.VMEM((hp, V_HEAD_DIM, tq), _F32)],
        compiler_params=_params(3),
        name="causal_attention",
    )(q, k, v, k, v)

    wrt1, br1 = _router_operands(router_g[1], router_g_bias[1], router_e[1], router_e_bias[1])
    route_shapes, route_specs = _route_out(n_real_tiles, tm)
    h3, xn2, ri2, rf2, cnt2 = pl.pallas_call(
        _oproj_kernel,
        grid=(n_real_tiles,),
        in_specs=[pl.BlockSpec((tm, N_HEADS * V_HEAD_DIM), lambda i: (i, 0)),
                  pl.BlockSpec((tm, d), lambda i: (i, 0)),
                  _full((N_HEADS * V_HEAD_DIM, d)), _full((1, d)),
                  _full((ROUTER_ROWS, d)), _full((ROUTER_ROWS, 1)), _full((tm, tm))],
        out_specs=[pl.BlockSpec((tm, d), lambda i: (i, 0)),
                   pl.BlockSpec((tm * parts, LANES), lambda i: (i, 0))] + route_specs,
        out_shape=[jax.ShapeDtypeStruct((n_tok, d), _F32),
                   jax.ShapeDtypeStruct((n_tok * parts, LANES), jnp.uint32)] + route_shapes,
        scratch_shapes=[pltpu.VMEM((N_EXPERTS, 1), _F32), pltpu.VMEM((ROUTER_SPLIT_ROWS, d), _BF16)],
        compiler_params=_params(1),
        name="attn_out_router",
    )(o, h2, b_w_o[0].astype(_BF16), row(ffn_norm[1]), wrt1, br1, triu)

    dest2, y2 = _moe_experts(xn2, ri2, cnt2, w_gate, w_up, w_down, 1, n_real_tiles, n_real_tiles, d)
    operands, in_specs, scratch = _combine_operands(dest2, rf2, h3, y2, d)
    out = pl.pallas_call(
        functools.partial(_combine_norm_kernel, tm=tm, n_tiles=n_real_tiles),
        grid=(n_real_tiles,),
        in_specs=in_specs + [_full((1, d))],
        out_specs=pl.BlockSpec((tm, d), lambda i: (i, 0)),
        out_shape=jax.ShapeDtypeStruct((n_tok, d), _F32),
        scratch_shapes=scratch,
        compiler_params=_params(1, disable_bounds_checks=True),
        name="moe_combine_final_norm",
    )(*operands, row(final_norm))
    return out.reshape(bsz, seq, d)
```

```python
import functools

import numpy as np
import jax
import jax.numpy as jnp
from jax import lax
from jax.experimental import pallas as pl
from jax.experimental.pallas import tpu as pltpu

N_META = 16
POOL_WINDOWS = (2, 4, 8, 16)
N_HEADS = 8
QK_NOPE_DIM = 128
QK_ROPE_DIM = 64
QK_DIM = QK_NOPE_DIM + QK_ROPE_DIM
V_HEAD_DIM = 128
KV_LORA_RANK = 256
ROPE_THETA = 10000.0
ATTN_SCALE = QK_DIM ** -0.5
Q_SCALE = ATTN_SCALE * 1.4426950408889634
N_EXPERT_GROUPS = 4
EXPERTS_PER_GROUP = 8
N_EXPERTS = N_EXPERT_GROUPS * EXPERTS_PER_GROUP
RMS_EPS = 1e-6
NEG_INF = -1e30

TOKEN_TILE = 512
POOL_BLOCK = 256
EXPERT_ROWS = 256
EXPERT_BLOCKS_PER_STEP = 2
ATTN_Q_TILE = 512
ATTN_K_TILE = ATTN_Q_TILE // 2
ATTN_HEADS_PER_STEP = 4
DMA_UNROLL = 8
DMA_THREADS = 2
ROUTER_ROWS = 8 + N_EXPERTS
ROUTER_LO_ROW = 48
ROUTER_SPLIT_ROWS = 2 * ROUTER_LO_ROW
VMEM_LIMIT_BYTES = 48 * 1024 * 1024

LANES = 128
_F32 = jnp.float32
_BF16 = jnp.bfloat16
_NT_DIMS = (((1,), (1,)), ((), ()))


def _params(n_grid_dims=1, **kw):
    return pltpu.CompilerParams(dimension_semantics=("arbitrary",) * n_grid_dims,
                                vmem_limit_bytes=VMEM_LIMIT_BYTES, **kw)


def _rms(x, g):
    ms = jnp.mean(x * x, axis=-1, keepdims=True)
    return x * lax.rsqrt(ms + RMS_EPS) * g


def _split_bf16(x):
    hi = x.astype(_BF16)
    lo = (x - hi.astype(_F32)).astype(_BF16)
    return hi, lo


def _dot(a, b):
    return jnp.dot(a, b, preferred_element_type=_F32)


def _store_token_tiles(ref, x):
    n, d = x.shape
    parts = d // LANES
    for s in range(parts):
        ref[pl.ds(s, n, stride=parts), :] = x[:, s * LANES:(s + 1) * LANES]


def _load_token_tiles(ref, first_token, n, parts):
    return jnp.concatenate([ref[pl.ds(first_token * parts + s, n, stride=parts), :] for s in range(parts)], axis=1)


_HIGH_HALF = 0xFFFF0000


def _pack_rows(x):
    n, d = x.shape
    words = []
    for s in range(d // (2 * LANES)):
        lo = x[:, 2 * s * LANES:(2 * s + 1) * LANES].astype(_BF16).astype(_F32)
        hi = x[:, (2 * s + 1) * LANES:(2 * s + 2) * LANES].astype(_BF16).astype(_F32)
        words.append(lax.shift_right_logical(lax.bitcast_convert_type(lo, jnp.uint32), jnp.uint32(16))
                     | (lax.bitcast_convert_type(hi, jnp.uint32) & jnp.uint32(_HIGH_HALF)))
    return jnp.concatenate(words, axis=1)


def _unpack_rows(w):
    cols = []
    for s in range(w.shape[1] // LANES):
        ws = w[:, s * LANES:(s + 1) * LANES]
        cols.append(lax.bitcast_convert_type(lax.shift_left(ws, jnp.uint32(16)), _F32))
        cols.append(lax.bitcast_convert_type(ws & jnp.uint32(_HIGH_HALF), _F32))
    return jnp.concatenate(cols, axis=1)


def _route_init(wrt_ref, wsplit_ref, base_ref):
    base_ref[...] = jnp.zeros(base_ref.shape, base_ref.dtype)
    w_hi, w_lo = _split_bf16(wrt_ref[...])
    wsplit_ref[...] = jnp.zeros(wsplit_ref.shape, wsplit_ref.dtype)
    wsplit_ref[0:ROUTER_ROWS, :] = w_hi
    wsplit_ref[ROUTER_LO_ROW:ROUTER_LO_ROW + ROUTER_ROWS, :] = w_lo


def _route(xn, valid, wsplit_ref, br_ref, triu_ref, base_ref, ri_ref, rf_ref, cnt_ref):
    tm = xn.shape[0]
    x_hi, x_lo = _split_bf16(xn)
    by_hi = lax.dot_general(wsplit_ref[...], x_hi, _NT_DIMS, preferred_element_type=_F32)
    by_lo = lax.dot_general(wsplit_ref[0:ROUTER_LO_ROW, :], x_lo, _NT_DIMS, preferred_element_type=_F32)
    logits = (by_hi[0:ROUTER_ROWS] + by_hi[ROUTER_LO_ROW:ROUTER_LO_ROW + ROUTER_ROWS]
              + by_lo[0:ROUTER_ROWS] + br_ref[...])
    lg = logits[0:N_EXPERT_GROUPS]
    eg = jnp.exp(lg - jnp.max(lg, axis=0, keepdims=True))
    pg = eg / jnp.sum(eg, axis=0, keepdims=True)
    w_g = jnp.max(pg, axis=0, keepdims=True)
    ig = lax.broadcasted_iota(jnp.int32, pg.shape, 0).astype(_F32)
    g_sel = jnp.min(jnp.where(pg == w_g, ig, float(N_EXPERT_GROUPS)), axis=0, keepdims=True)

    sel = logits[8:8 + EXPERTS_PER_GROUP]
    for g in range(1, N_EXPERT_GROUPS):
        sel = jnp.where(g_sel == float(g), logits[8 + g * EXPERTS_PER_GROUP:8 + (g + 1) * EXPERTS_PER_GROUP], sel)
    ie = lax.broadcasted_iota(jnp.int32, sel.shape, 0).astype(_F32)
    v1 = jnp.max(sel, axis=0, keepdims=True)
    i1 = jnp.min(jnp.where(sel == v1, ie, float(EXPERTS_PER_GROUP)), axis=0, keepdims=True)
    rest = jnp.where(ie == i1, -jnp.inf, sel)
    v2 = jnp.max(rest, axis=0, keepdims=True)
    i2 = jnp.min(jnp.where(rest == v2, ie, float(EXPERTS_PER_GROUP)), axis=0, keepdims=True)
    e2 = jnp.exp(v2 - v1)
    den = 1.0 + e2
    validf = valid.astype(_F32)
    gate0 = w_g * (1.0 / den) * validf
    gate1 = w_g * (e2 / den) * validf
    eid0 = g_sel * float(EXPERTS_PER_GROUP) + i1
    eid1 = g_sel * float(EXPERTS_PER_GROUP) + i2

    iall = lax.broadcasted_iota(jnp.int32, (N_EXPERTS, tm), 0).astype(_F32)
    oh0 = jnp.where(iall == eid0, validf, 0.0)
    oh1 = jnp.where(iall == eid1, validf, 0.0)
    both = oh0 + oh1
    before = _dot(both.astype(_BF16), triu_ref[...]) + base_ref[...]
    rank0 = jnp.sum(oh0 * before, axis=0, keepdims=True)
    rank1 = jnp.sum(oh1 * before, axis=0, keepdims=True)
    base_ref[...] = base_ref[...] + jnp.sum(both, axis=1, keepdims=True)

    ri_ref[...] = jnp.zeros(ri_ref.shape, ri_ref.dtype)
    rf_ref[...] = jnp.zeros(rf_ref.shape, rf_ref.dtype)
    ri_ref[0, 0:1, :] = eid0.astype(jnp.int32)
    ri_ref[0, 1:2, :] = eid1.astype(jnp.int32)
    ri_ref[0, 2:3, :] = rank0.astype(jnp.int32)
    ri_ref[0, 3:4, :] = rank1.astype(jnp.int32)
    rf_ref[0, 0:1, :] = gate0
    rf_ref[0, 1:2, :] = gate1
    cnt_ref[...] = jnp.broadcast_to(base_ref[...], cnt_ref.shape).astype(jnp.int32)


def _mixer_kernel(x_ref, xh_ref, mp_ref, meta_ref, anorm_ref, aw_ref, ascale_ref, fnorm_ref,
                  wrt_ref, br_ref, pm_ref, ph_ref, triu_ref,
                  h_ref, xn_ref, ri_ref, rf_ref, cnt_ref, base_ref, wsplit_ref, *, n_real_tiles, tiles_per_batch):
    i = pl.program_id(0)
    tm = x_ref.shape[0]
    gd = x_ref.shape[1] // len(POOL_WINDOWS)
    is_meta = i == n_real_tiles
    first = (i % tiles_per_batch) == 0

    @pl.when(i == 0)
    def _():
        _route_init(wrt_ref, wsplit_ref, base_ref)

    h = jnp.where(is_meta, mp_ref[...], x_ref[...])
    halo = jnp.where(is_meta, 0.0, jnp.where(first, meta_ref[...], xh_ref[...]))
    hn = _rms(h, anorm_ref[...])
    hh = _rms(halo, anorm_ref[...])
    hn_hi, hn_lo = _split_bf16(hn)
    hh_hi, hh_lo = _split_bf16(hh)
    pb = pm_ref.shape[1]
    for c in range(tm // pb):
        rows = slice(c * pb, (c + 1) * pb)
        row = c * pb + lax.broadcasted_iota(jnp.int32, (pb, 1), 0)
        for g, w in enumerate(POOL_WINDOWS):
            sl = slice(g * gd, (g + 1) * gd)
            win = _dot(pm_ref[g], hn_hi[rows, sl]) + _dot(pm_ref[g], hn_lo[rows, sl])
            if c == 0:
                before_hi, before_lo = hh_hi[:, sl], hh_lo[:, sl]
            else:
                before_hi, before_lo = hn_hi[c * pb - N_META:c * pb, sl], hn_lo[c * pb - N_META:c * pb, sl]
            top = win[:N_META] + _dot(ph_ref[g], before_hi) + _dot(ph_ref[g], before_lo)
            win = jnp.concatenate([top, win[N_META:]], axis=0)
            cnt = jnp.where(is_meta, jnp.minimum(row + 1, w), w).astype(_F32)
            pooled = win * (1.0 / cnt) - hn[rows, sl]
            mix = _dot(pooled.astype(_BF16), aw_ref[g])
            h_ref[rows, sl] = h[rows, sl] + mix * ascale_ref[:, sl]

    xn = _rms(h_ref[...], fnorm_ref[...])
    _store_token_tiles(xn_ref, _pack_rows(xn))
    lane = lax.broadcasted_iota(jnp.int32, (1, tm), 1)
    valid = jnp.logical_or(jnp.logical_not(is_meta), lane < N_META)
    _route(xn, valid, wsplit_ref, br_ref, triu_ref, base_ref, ri_ref, rf_ref, cnt_ref)


def _index_copy(dest_hbm, idx_smem, isem, tile, n):
    return pltpu.make_async_copy(dest_hbm.at[pl.ds(pl.multiple_of(tile * n, n), n)],
                                 idx_smem.at[pl.ds(pl.multiple_of((tile % 2) * n, n), n)], isem.at[tile % 2])


def _dispatch_kernel(dest_hbm, xn_hbm, xs_hbm, idx_smem, xbuf, isem, lsem, ssem, *, tm, n_tiles, n_real_tiles):
    i = pl.program_id(0)
    parts = xbuf.shape[1] // tm
    buf = i % 3

    def idx_copy(tile):
        return _index_copy(dest_hbm, idx_smem, isem, tile, 2 * tm)

    def load(tile):
        rows = tm * parts
        return pltpu.make_async_copy(xn_hbm.at[pl.ds(pl.multiple_of(tile * rows, rows), rows), :],
                                     xbuf.at[tile % 3], lsem.at[tile % 3])

    def wait_scatters(tile):
        def wait_rows(n_tok):
            for k in range(2):
                pltpu.make_async_copy(xbuf.at[tile % 3, pl.ds(0, n_tok * parts), :],
                                      xs_hbm.at[pl.ds(0, n_tok * parts), :], ssem.at[tile % 3]).wait()

        @pl.when(tile < n_real_tiles)
        def _():
            wait_rows(tm)

        @pl.when(tile >= n_real_tiles)
        def _():
            wait_rows(N_META)

    def scatter(n_tok):
        def issue(g, c):
            for u in range(min(DMA_UNROLL, n_tok)):
                t = g * min(DMA_UNROLL, n_tok) + u
                src = xbuf.at[buf, pl.ds(pl.multiple_of(t * parts, parts), parts), :]
                for k in range(2):
                    d = idx_smem[(i % 2) * (2 * tm) + k * tm + t]
                    pltpu.make_async_copy(src, xs_hbm.at[pl.ds(pl.multiple_of(d * parts, parts), parts), :],
                                          ssem.at[buf]).start(priority=(2 * u + k) % DMA_THREADS)
            return c

        lax.fori_loop(0, n_tok // min(DMA_UNROLL, n_tok), issue, 0)

    @pl.when(i == 0)
    def _():
        idx_copy(0).start()
        load(0).start()

    @pl.when(i >= 2)
    def _():
        wait_scatters(i - 2)

    @pl.when(i + 1 < n_tiles)
    def _():
        idx_copy(i + 1).start()
        load(i + 1).start()

    idx_copy(i).wait()
    load(i).wait()

    @pl.when(i < n_real_tiles)
    def _():
        scatter(tm)

    @pl.when(i >= n_real_tiles)
    def _():
        scatter(N_META)

    @pl.when(i == n_tiles - 1)
    def _():
        if n_tiles >= 2:
            wait_scatters(i - 1)
        wait_scatters(i)


def _expert_kernel(be_ref, bv_ref, nu_ref, xs_ref, *refs):
    nb = EXPERT_BLOCKS_PER_STEP
    w_refs, y_ref, wb_refs = refs[:3 * nb], refs[3 * nb], refs[3 * nb + 1:]
    i = pl.program_id(0)
    n_used = nu_ref[0]
    parts = wb_refs[0].shape[0] // (2 * LANES)
    tr = xs_ref.shape[0] // (nb * parts)

    for c in range(nb):
        blk = i * nb + c
        changed = be_ref[blk] != be_ref[jnp.maximum(blk - nb, 0)]

        @pl.when(jnp.logical_or(i == 0, jnp.logical_and(blk < n_used, changed)))
        def _(c=c):
            for w_ref, wb_ref in zip(w_refs[3 * c:3 * c + 3], wb_refs[3 * c:3 * c + 3]):
                wb_ref[...] = w_ref[0, 0].astype(_BF16)

    @pl.when(i * nb < n_used)
    def _():
        rows = lax.broadcasted_iota(jnp.int32, (tr, 1), 0)
        for c in range(nb):
            blk = i * nb + c
            wgb_ref, wub_ref, wdb_ref = wb_refs[3 * c:3 * c + 3]
            x = jnp.where(rows < bv_ref[blk], _unpack_rows(_load_token_tiles(xs_ref, c * tr, tr, parts)), 0.0)
            x = x.astype(_BF16)
            g = _dot(x, wgb_ref[...])
            u = _dot(x, wub_ref[...])
            a = g * (1.0 / (1.0 + jnp.exp(-g))) * u
            y = jnp.where(blk < n_used, _dot(a.astype(_BF16), wdb_ref[...]), 0.0)
            _store_token_tiles(y_ref.at[pl.ds(c * tr * parts, tr * parts), :], _pack_rows(y))

    @pl.when(i * nb >= n_used)
    def _():
        y_ref[...] = jnp.zeros(y_ref.shape, y_ref.dtype)


def _combined_tile(dest_hbm, rf_ref, h_ref, y_hbm, idx_smem, ybufs, isem, rsem, *, tm, n_tiles, slot):
    i = pl.program_id(0)
    parts = ybufs[0].shape[0] // (2 * tm)

    def idx_copy(tile):
        return _index_copy(dest_hbm, idx_smem, isem, tile, 2 * tm)

    def row_copy(idx_slot, buf_slot, t, k, u):
        d = idx_smem[idx_slot * (2 * tm) + k * tm + t]
        row = (k * tm + t) * parts
        if not isinstance(t, int):
            row = pl.multiple_of(row, parts)
        return pltpu.make_async_copy(
            y_hbm.at[pl.ds(pl.multiple_of(d * parts, parts), parts), :],
            ybufs[buf_slot].at[pl.ds(row, parts), :],
            rsem.at[buf_slot]).start(priority=(2 * u + k) % DMA_THREADS)

    def wait_rows(buf_slot):
        pltpu.make_async_copy(y_hbm.at[pl.ds(0, 2 * tm * parts), :], ybufs[buf_slot], rsem.at[buf_slot]).wait()

    if slot == 0:
        @pl.when(i == 0)
        def _():
            idx_copy(0).start()
            idx_copy(0).wait()

            def issue(g, c):
                for u in range(DMA_UNROLL):
                    for k in range(2):
                        row_copy(0, 0, g * DMA_UNROLL + u, k, u)
                return c

            lax.fori_loop(0, tm // DMA_UNROLL, issue, 0)
            if n_tiles >= 2:
                idx_copy(1).start()

    @pl.when(i + 1 < n_tiles)
    def _():
        idx_copy(i + 1).wait()

    @pl.when(i + 2 < n_tiles)
    def _():
        idx_copy(i + 2).start()

    nxt_idx_slot = jnp.minimum(i + 1, n_tiles - 1) % 2

    def prefetch(part, n_parts):
        for t in range(part * tm // n_parts, (part + 1) * tm // n_parts):
            for k in range(2):
                row_copy(nxt_idx_slot, 1 - slot, t, k, t)

    gates = rf_ref[0]
    gt = jnp.concatenate([gates, jnp.zeros((LANES - gates.shape[0], tm), _F32)], axis=0).T
    wait_rows(slot)
    yb = ybufs[slot]

    def combined(r0, n):
        return h_ref[r0:r0 + n, :] + (_unpack_rows(_load_token_tiles(yb, r0, n, parts)) * gt[r0:r0 + n, 0:1]
                                      + _unpack_rows(_load_token_tiles(yb, tm + r0, n, parts)) * gt[r0:r0 + n, 1:2])

    def drain():
        @pl.when(i == n_tiles - 1)
        def _():
            wait_rows(1 - slot)

    return combined, prefetch, drain


def _for_each_parity(body):
    for slot in range(2):
        @pl.when(pl.program_id(0) % 2 == slot)
        def _(slot=slot):
            body(slot)


def _combine_norm_kernel(dest_hbm, rf_ref, h_ref, y_hbm, fnorm_ref, out_ref,
                         idx_smem, ybuf0, ybuf1, isem, rsem, *, tm, n_tiles):
    def body(slot):
        combined, prefetch, drain = _combined_tile(dest_hbm, rf_ref, h_ref, y_hbm, idx_smem, (ybuf0, ybuf1),
                                                   isem, rsem, tm=tm, n_tiles=n_tiles, slot=slot)
        n_chunks = 4
        n = tm // n_chunks
        for c in range(n_chunks):
            prefetch(c, n_chunks)
            out_ref[c * n:(c + 1) * n, :] = _rms(combined(c * n, n), fnorm_ref[...])
        drain()

    _for_each_parity(body)


def _combine_proj_kernel(dest_hbm, rf_ref, h_ref, y_hbm, cos_ref, sin_ref, kvn_ref, wdkv_ref, kvlat_ref,
                         wuk_ref, wuv_ref, bnorm_ref, wdq_ref, qnorm_ref, wuq_ref,
                         h_out_ref, q_ref, k_ref, v_ref, idx_smem, ybuf0, ybuf1, isem, rsem, *, tm, n_tiles):
    def body(slot):
        combined, prefetch, drain = _combined_tile(dest_hbm, rf_ref, h_ref, y_hbm, idx_smem, (ybuf0, ybuf1),
                                                   isem, rsem, tm=tm, n_tiles=n_tiles, slot=slot)
        n = v_ref.shape[3]
        n_chunks = tm // n
        for c in range(n_chunks):
            prefetch(c, n_chunks)
            h = combined(c * n, n)
            h_out_ref[c * n:(c + 1) * n, :] = h
            _project(h, c, cos_ref, sin_ref, kvn_ref, wdkv_ref, kvlat_ref, wuk_ref, wuv_ref,
                     bnorm_ref, wdq_ref, qnorm_ref, wuq_ref, q_ref, k_ref, v_ref)
        drain()

    _for_each_parity(body)


def _rope128(x, cos_t, sin_t):
    lane = lax.broadcasted_iota(jnp.int32, (1, 128), 1)
    first_half = (lane % QK_ROPE_DIM) < (QK_ROPE_DIM // 2)
    swapped = jnp.where(first_half, pltpu.roll(x, 128 - QK_ROPE_DIM // 2, axis=1),
                        pltpu.roll(x, QK_ROPE_DIM // 2, axis=1))
    return x * cos_t + swapped * sin_t


def _project(h, chunk, cos_ref, sin_ref, kvn_ref, wdkv_ref, kvlat_ref, wuk_ref, wuv_ref,
             bnorm_ref, wdq_ref, qnorm_ref, wuq_ref, q_ref, k_ref, v_ref):
    n = h.shape[0]
    rows = slice(chunk * n, (chunk + 1) * n)
    cos_t = cos_ref[rows, :]
    sin_t = sin_ref[rows, :]
    c = _dot(_rms(h, kvn_ref[...]).astype(_BF16), wdkv_ref[...])
    ckv = _rms(c[:, :KV_LORA_RANK], kvlat_ref[...]).astype(_BF16)
    kr = _rope128(c[:, KV_LORA_RANK:KV_LORA_RANK + 128], cos_t, sin_t)[:, :QK_ROPE_DIM].astype(_BF16)
    kn = _dot(ckv, wuk_ref[...])
    vt = lax.dot_general(wuv_ref[...], ckv, _NT_DIMS, preferred_element_type=_F32)
    cq = _rms(_dot(_rms(h, bnorm_ref[...]).astype(_BF16), wdq_ref[...]), qnorm_ref[...])
    q = _dot((cq * Q_SCALE).astype(_BF16), wuq_ref[...])
    rope0 = N_HEADS * QK_NOPE_DIM
    for hd in range(N_HEADS):
        k_ref[hd, rows, 0:QK_NOPE_DIM] = kn[:, hd * QK_NOPE_DIM:(hd + 1) * QK_NOPE_DIM].astype(_BF16)
        k_ref[hd, rows, QK_NOPE_DIM:QK_DIM] = kr
        v_ref[hd, chunk] = vt[hd * V_HEAD_DIM:(hd + 1) * V_HEAD_DIM, :].astype(_BF16)
        q_ref[hd, rows, 0:QK_NOPE_DIM] = q[:, hd * QK_NOPE_DIM:(hd + 1) * QK_NOPE_DIM].astype(_BF16)
    for pair in range(N_HEADS // 2):
        qr = _rope128(q[:, rope0 + pair * LANES:rope0 + (pair + 1) * LANES], cos_t, sin_t)
        q_ref[2 * pair, rows, QK_NOPE_DIM:QK_DIM] = qr[:, :QK_ROPE_DIM].astype(_BF16)
        q_ref[2 * pair + 1, rows, QK_NOPE_DIM:QK_DIM] = qr[:, QK_ROPE_DIM:].astype(_BF16)


def _attn_kernel(q_ref, k_ref, vt_ref, km_ref, vmt_ref, o_ref, sa_ref, sb_ref, acc_ref, *, tq, tk):
    j = pl.program_id(2)
    heads = range(q_ref.shape[0])
    vd = acc_ref.shape[1]

    def scores(kb, s_ref):
        start = pl.multiple_of(kb * tk, tk)
        for hd in heads:
            s_ref[hd] = lax.dot_general(k_ref[hd, pl.ds(start, tk), :], q_ref[hd], _NT_DIMS,
                                        preferred_element_type=_F32)

    def update(kb, s_ref, carry, masked):
        out = []
        for hd in heads:
            m, l = carry[hd]
            s = s_ref[hd]
            if masked:
                keys = kb * tk + lax.broadcasted_iota(jnp.int32, (tk, tq), 0)
                qpos = j * tq + lax.broadcasted_iota(jnp.int32, (tk, tq), 1)
                s = jnp.where(keys <= qpos, s, NEG_INF)
            m_new = jnp.maximum(m, jnp.max(s, axis=0, keepdims=True))
            alpha = jnp.exp2(m - m_new)
            p = jnp.exp2(s - m_new)
            l = alpha * l + jnp.sum(p, axis=0, keepdims=True)
            acc_ref[hd] = acc_ref[hd] * alpha + _dot(vt_ref[hd, kb], p.astype(_BF16))
            out.append((m_new, l))
        return tuple(out)

    scores(0, sa_ref)
    carry = []
    for hd in heads:
        s0 = lax.dot_general(km_ref[hd], q_ref[hd], _NT_DIMS, preferred_element_type=_F32)
        m = jnp.max(s0, axis=0, keepdims=True)
        p0 = jnp.exp2(s0 - m)
        carry.append((m, jnp.sum(p0, axis=0, keepdims=True)))
        acc_ref[hd] = _dot(vmt_ref[hd, 0, :, 0:N_META], p0.astype(_BF16))
    carry = tuple(carry)

    def pair(kp, carry):
        scores(2 * kp + 1, sb_ref)
        carry = update(2 * kp, sa_ref, carry, False)
        scores(2 * kp + 2, sa_ref)
        return update(2 * kp + 1, sb_ref, carry, False)

    carry = lax.fori_loop(0, j, pair, carry)
    scores(2 * j + 1, sb_ref)
    carry = update(2 * j, sa_ref, carry, True)
    carry = update(2 * j + 1, sb_ref, carry, True)
    for hd in heads:
        o_ref[:, hd * vd:(hd + 1) * vd] = (acc_ref[hd] / carry[hd][1]).T.astype(o_ref.dtype)


def _oproj_kernel(o_ref, h_ref, wo_ref, fnorm_ref, wrt_ref, br_ref, triu_ref,
                  h_out_ref, xn_ref, ri_ref, rf_ref, cnt_ref, base_ref, wsplit_ref):
    i = pl.program_id(0)

    @pl.when(i == 0)
    def _():
        _route_init(wrt_ref, wsplit_ref, base_ref)

    h = h_ref[...] + _dot(o_ref[...], wo_ref[...])
    h_out_ref[...] = h
    xn = _rms(h, fnorm_ref[...])
    _store_token_tiles(xn_ref, _pack_rows(xn))
    valid = lax.broadcasted_iota(jnp.int32, (1, h.shape[0]), 1) >= 0
    _route(xn, valid, wsplit_ref, br_ref, triu_ref, base_ref, ri_ref, rf_ref, cnt_ref)


def _full(shape):
    nd = len(shape)
    return pl.BlockSpec(shape, lambda *_: (0,) * nd)


def _router_operands(router_g, router_g_bias, router_e, router_e_bias):
    d = router_g.shape[0]
    wrt = jnp.concatenate([router_g.T, jnp.zeros((8 - N_EXPERT_GROUPS, d), _F32), router_e.T], axis=0)
    br = jnp.concatenate([router_g_bias, jnp.zeros((8 - N_EXPERT_GROUPS,), _F32), router_e_bias])[:, None]
    return wrt.astype(_F32), br.astype(_F32)


def _route_out(n_tiles, tm):
    shapes = [jax.ShapeDtypeStruct((n_tiles, 8, tm), jnp.int32),
              jax.ShapeDtypeStruct((n_tiles, 8, tm), _F32),
              jax.ShapeDtypeStruct((N_EXPERTS, 128), jnp.int32)]
    specs = [pl.BlockSpec((1, 8, tm), lambda i: (i, 0, 0)),
             pl.BlockSpec((1, 8, tm), lambda i: (i, 0, 0)),
             pl.BlockSpec((N_EXPERTS, 128), lambda i: (0, 0))]
    return shapes, specs


def _moe_experts(xn, ri, counts, w_gate, w_up, w_down, layer, n_tiles, n_real_tiles, d):
    tm, tr = TOKEN_TILE, EXPERT_ROWS
    parts = d // (2 * LANES)
    n_valid = n_real_tiles * tm + (n_tiles - n_real_tiles) * N_META
    n_blocks = -(-(2 * n_valid + N_EXPERTS * (tr - 1)) // tr)
    n_blocks = -(-n_blocks // EXPERT_BLOCKS_PER_STEP) * EXPERT_BLOCKS_PER_STEP
    n_rows = n_blocks * tr

    counts = counts[:, 0]
    padded = (counts + tr - 1) // tr * tr
    pends = jnp.cumsum(padded)
    pstarts = pends - padded
    n_used = (pends[-1] // tr).astype(jnp.int32).reshape(1)
    blk0 = jnp.arange(n_blocks, dtype=jnp.int32) * tr
    block_e = jnp.minimum(jnp.sum(blk0[:, None] >= pends[None, :], axis=1), N_EXPERTS - 1).astype(jnp.int32)
    experts = jnp.arange(N_EXPERTS, dtype=jnp.int32)
    block_oh = block_e[:, None] == experts[None, :]
    block_cnt = jnp.sum(jnp.where(block_oh, counts[None, :], 0), axis=1)
    block_start = jnp.sum(jnp.where(block_oh, pstarts[None, :], 0), axis=1)
    block_valid = jnp.clip(block_cnt - (blk0 - block_start), 0, tr).astype(jnp.int32)
    eid = ri[:, 0:2, :]
    slot0 = jnp.sum(jnp.where(eid[..., None] == experts, pstarts, 0), axis=-1)
    dest = (slot0 + ri[:, 2:4, :]).astype(jnp.int32).reshape(n_tiles * 2 * tm)

    xs = pl.pallas_call(
        functools.partial(_dispatch_kernel, tm=tm, n_tiles=n_tiles, n_real_tiles=n_real_tiles),
        grid=(n_tiles,),
        in_specs=[pl.BlockSpec(memory_space=pl.ANY), pl.BlockSpec(memory_space=pl.ANY)],
        out_specs=pl.BlockSpec(memory_space=pl.ANY),
        out_shape=jax.ShapeDtypeStruct((n_rows * parts, LANES), jnp.uint32),
        scratch_shapes=[pltpu.SMEM((4 * tm,), jnp.int32), pltpu.VMEM((3, tm * parts, LANES), jnp.uint32),
                        pltpu.SemaphoreType.DMA((2,)), pltpu.SemaphoreType.DMA((3,)),
                        pltpu.SemaphoreType.DMA((3,))],
        compiler_params=_params(1, has_side_effects=True, disable_bounds_checks=True),
        name="moe_dispatch",
    )(dest, xn)

    f = w_gate.shape[3]
    nb = EXPERT_BLOCKS_PER_STEP
    step_rows = nb * tr * parts

    def expert_block(c):
        return lambda i, be, bv, nu: (layer, be[jnp.minimum(i * nb + c, nu[0] - 1)], 0, 0)

    weight_specs, weight_scratch = [], []
    for c in range(nb):
        weight_specs += [pl.BlockSpec((1, 1, d, f), expert_block(c)), pl.BlockSpec((1, 1, d, f), expert_block(c)),
                         pl.BlockSpec((1, 1, f, d), expert_block(c))]
        weight_scratch += [pltpu.VMEM((d, f), _BF16), pltpu.VMEM((d, f), _BF16), pltpu.VMEM((f, d), _BF16)]
    y = pl.pallas_call(
        _expert_kernel,
        grid_spec=pltpu.PrefetchScalarGridSpec(
            num_scalar_prefetch=3,
            grid=(n_blocks // nb,),
            in_specs=[pl.BlockSpec((step_rows, LANES),
                                   lambda i, be, bv, nu: (jnp.minimum(i, (nu[0] - 1) // nb), 0))] + weight_specs,
            out_specs=pl.BlockSpec((step_rows, LANES), lambda i, be, bv, nu: (i, 0)),
            scratch_shapes=weight_scratch),
        out_shape=jax.ShapeDtypeStruct((n_rows * parts, LANES), jnp.uint32),
        compiler_params=_params(1),
        name="moe_experts",
    )(block_e, block_valid, n_used, xs, *([w_gate, w_up, w_down] * nb))

    return dest, y


def _combine_operands(dest, rf, h, y, d):
    tm = TOKEN_TILE
    parts = d // (2 * LANES)
    in_specs = [pl.BlockSpec(memory_space=pl.ANY),
                pl.BlockSpec((1, 8, tm), lambda i: (i, 0, 0)),
                pl.BlockSpec((tm, d), lambda i: (i, 0)),
                pl.BlockSpec(memory_space=pl.ANY)]
    scratch = [pltpu.SMEM((4 * tm,), jnp.int32),
               pltpu.VMEM((2 * tm * parts, LANES), jnp.uint32), pltpu.VMEM((2 * tm * parts, LANES), jnp.uint32),
               pltpu.SemaphoreType.DMA((2,)), pltpu.SemaphoreType.DMA((2,))]
    return (dest, rf, h, y), in_specs, scratch


def kernel(x, meta_tokens, a_norm, a_w, a_scale, b_norm, b_w_dq, b_q_norm, b_w_uq, b_w_o, kv_norm, w_dkv,
           kv_lat_norm, w_uk, w_uv, ffn_norm, router_g, router_g_bias, router_e, router_e_bias, w_gate, w_up,
           w_down, final_norm):
    bsz, seq, d = x.shape
    tm = TOKEN_TILE
    assert seq % tm == 0 and seq % ATTN_Q_TILE == 0 and ATTN_Q_TILE == 2 * ATTN_K_TILE and tm % ATTN_K_TILE == 0
    kt = tm // ATTN_K_TILE
    assert d % (LANES * len(POOL_WINDOWS)) == 0 and N_META == max(POOL_WINDOWS) and N_META <= tm
    parts = d // (2 * LANES)
    n_tok = bsz * seq
    n_real_tiles = n_tok // tm
    tiles_per_batch = seq // tm
    n_tiles = n_real_tiles + 1
    gd = d // len(POOL_WINDOWS)
    row = lambda v: v.reshape(1, -1).astype(_F32)

    x2 = x.reshape(n_tok, d)
    meta_pad = jnp.concatenate([meta_tokens, jnp.zeros((tm - N_META, d), x.dtype)], axis=0)

    r = np.arange(tm)[:, None]
    cidx = np.arange(tm)[None, :]
    pb = min(POOL_BLOCK, tm)
    pm = np.stack([((r[:pb] - cidx[:, :pb] >= 0) & (r[:pb] - cidx[:, :pb] < w)) for w in POOL_WINDOWS]).astype(np.float32)
    hc = np.arange(N_META)[None, :]
    ph = np.stack([(r[:N_META] + N_META - hc < w) for w in POOL_WINDOWS]).astype(np.float32)
    triu = (r < cidx).astype(np.float32)
    pm, ph, triu = (jnp.asarray(a, dtype=_BF16) for a in (pm, ph, triu))

    wrt0, br0 = _router_operands(router_g[0], router_g_bias[0], router_e[0], router_e_bias[0])
    route_shapes, route_specs = _route_out(n_tiles, tm)
    tile_or_last = lambda i: (jnp.minimum(i, n_real_tiles - 1), 0)
    halo_blocks = tm // N_META
    h1, xn1, ri1, rf1, cnt1 = pl.pallas_call(
        functools.partial(_mixer_kernel, n_real_tiles=n_real_tiles, tiles_per_batch=tiles_per_batch),
        grid=(n_tiles,),
        in_specs=[pl.BlockSpec((tm, d), tile_or_last),
                  pl.BlockSpec((N_META, d), lambda i: (jnp.clip(i * halo_blocks - 1, 0, n_tok // N_META - 1), 0)),
                  _full((tm, d)), _full((N_META, d)), _full((1, d)),
                  _full((len(POOL_WINDOWS), gd, gd)), _full((1, d)), _full((1, d)),
                  _full((ROUTER_ROWS, d)), _full((ROUTER_ROWS, 1)),
                  _full(pm.shape), _full(ph.shape), _full((tm, tm))],
        out_specs=[pl.BlockSpec((tm, d), lambda i: (i, 0)),
                   pl.BlockSpec((tm * parts, LANES), lambda i: (i, 0))] + route_specs,
        out_shape=[jax.ShapeDtypeStruct((n_tiles * tm, d), _F32),
                   jax.ShapeDtypeStruct((n_tiles * tm * parts, LANES), jnp.uint32)] + route_shapes,
        scratch_shapes=[pltpu.VMEM((N_EXPERTS, 1), _F32), pltpu.VMEM((ROUTER_SPLIT_ROWS, d), _BF16)],
        compiler_params=_params(1),
        name="pool_mixer_router",
    )(x2, x2, meta_pad, meta_tokens, row(a_norm[0]), a_w[0].astype(_BF16), row(a_scale[0]), row(ffn_norm[0]),
      wrt0, br0, pm, ph, triu)

    dest1, y1 = _moe_experts(xn1, ri1, cnt1, w_gate, w_up, w_down, 0, n_tiles, n_real_tiles, d)

    pos = jnp.concatenate([jnp.arange(seq, dtype=_F32) + N_META, jnp.arange(tm, dtype=_F32)])
    inv_freq = ROPE_THETA ** (-jnp.arange(0, QK_ROPE_DIM, 2, dtype=_F32) / QK_ROPE_DIM)
    ang = pos[:, None] * inv_freq[None, :]
    cos_t = jnp.tile(jnp.cos(ang), (1, 4))
    sin_t = jnp.tile(jnp.concatenate([-jnp.sin(ang), jnp.sin(ang)], axis=1), (1, 2))

    wdkv = jnp.concatenate([w_dkv, w_dkv[:, KV_LORA_RANK:]], axis=1).astype(_BF16)
    wuk = w_uk.reshape(KV_LORA_RANK, N_HEADS * QK_NOPE_DIM).astype(_BF16)
    wuv = w_uv.reshape(KV_LORA_RANK, N_HEADS * V_HEAD_DIM).T.astype(_BF16)
    wuq = b_w_uq[0]
    q_rank = wuq.shape[0]
    wuq = jnp.concatenate([wuq[:, :, :QK_NOPE_DIM].reshape(q_rank, -1),
                           wuq[:, :, QK_NOPE_DIM:].reshape(q_rank, -1)], axis=1).astype(_BF16)
    wdq = b_w_dq[0].astype(_BF16)
    pos_tile = lambda i: (jnp.where(i < n_real_tiles, i % tiles_per_batch, tiles_per_batch), 0)
    head_tile = lambda i: (0, i, 0)
    operands, in_specs, scratch = _combine_operands(dest1, rf1, h1, y1, d)
    h2, q, k, v = pl.pallas_call(
        functools.partial(_combine_proj_kernel, tm=tm, n_tiles=n_tiles),
        grid=(n_tiles,),
        in_specs=in_specs + [
            pl.BlockSpec((tm, 128), pos_tile), pl.BlockSpec((tm, 128), pos_tile),
            _full((1, d)), _full(wdkv.shape), _full((1, KV_LORA_RANK)), _full(wuk.shape), _full(wuv.shape),
            _full((1, d)), _full(wdq.shape), _full((1, q_rank)), _full(wuq.shape)],
        out_specs=[pl.BlockSpec((tm, d), lambda i: (i, 0)),
                   pl.BlockSpec((N_HEADS, tm, QK_DIM), head_tile),
                   pl.BlockSpec((N_HEADS, tm, QK_DIM), head_tile),
                   pl.BlockSpec((N_HEADS, kt, V_HEAD_DIM, ATTN_K_TILE), lambda i: (0, i, 0, 0))],
        out_shape=[jax.ShapeDtypeStruct((n_tiles * tm, d), _F32),
                   jax.ShapeDtypeStruct((N_HEADS, n_tiles * tm, QK_DIM), _BF16),
                   jax.ShapeDtypeStruct((N_HEADS, n_tiles * tm, QK_DIM), _BF16),
                   jax.ShapeDtypeStruct((N_HEADS, n_tiles * kt, V_HEAD_DIM, ATTN_K_TILE), _BF16)],
        scratch_shapes=scratch,
        compiler_params=_params(1, disable_bounds_checks=True),
        name="moe_combine_latent_qkv",
    )(*operands, cos_t, sin_t, row(kv_norm), wdkv, row(kv_lat_norm), wuk, wuv,
      row(b_norm[0]), wdq, row(b_q_norm[0]), wuq)

    tq, tk, hp = ATTN_Q_TILE, ATTN_K_TILE, ATTN_HEADS_PER_STEP
    n_q = seq // tq
    meta_block = n_tok // N_META
    o = pl.pallas_call(
        functools.partial(_attn_kernel, tq=tq, tk=tk),
        grid=(bsz, N_HEADS // hp, n_q),
        in_specs=[pl.BlockSpec((hp, tq, QK_DIM), lambda b, hg, j: (hg, b * n_q + j, 0)),
                  pl.BlockSpec((hp, seq, QK_DIM), lambda b, hg, j: (hg, b, 0)),
                  pl.BlockSpec((hp, seq // tk, V_HEAD_DIM, tk), lambda b, hg, j: (hg, b, 0, 0)),
                  pl.BlockSpec((hp, N_META, QK_DIM), lambda b, hg, j: (hg, meta_block, 0)),
                  pl.BlockSpec((hp, 1, V_HEAD_DIM, tk), lambda b, hg, j: (hg, n_real_tiles * kt, 0, 0))],
        out_specs=pl.BlockSpec((tq, hp * V_HEAD_DIM), lambda b, hg, j: (b * n_q + j, hg)),
        out_shape=jax.ShapeDtypeStruct((n_tok, N_HEADS * V_HEAD_DIM), _BF16),
        scratch_shapes=[pltpu.VMEM((hp, tk, tq), _F32), pltpu.VMEM((hp, tk, tq), _F32),
                        pltpu.VMEM((hp, V_HEAD_DIM, tq), _F32)],
        compiler_params=_params(3),
        name="causal_attention",
    )(q, k, v, k, v)

    wrt1, br1 = _router_operands(router_g[1], router_g_bias[1], router_e[1], router_e_bias[1])
    route_shapes, route_specs = _route_out(n_real_tiles, tm)
    h3, xn2, ri2, rf2, cnt2 = pl.pallas_call(
        _oproj_kernel,
        grid=(n_real_tiles,),
        in_specs=[pl.BlockSpec((tm, N_HEADS * V_HEAD_DIM), lambda i: (i, 0)),
                  pl.BlockSpec((tm, d), lambda i: (i, 0)),
                  _full((N_HEADS * V_HEAD_DIM, d)), _full((1, d)),
                  _full((ROUTER_ROWS, d)), _full((ROUTER_ROWS, 1)), _full((tm, tm))],
        out_specs=[pl.BlockSpec((tm, d), lambda i: (i, 0)),
                   pl.BlockSpec((tm * parts, LANES), lambda i: (i, 0))] + route_specs,
        out_shape=[jax.ShapeDtypeStruct((n_tok, d), _F32),
                   jax.ShapeDtypeStruct((n_tok * parts, LANES), jnp.uint32)] + route_shapes,
        scratch_shapes=[pltpu.VMEM((N_EXPERTS, 1), _F32), pltpu.VMEM((ROUTER_SPLIT_ROWS, d), _BF16)],
        compiler_params=_params(1),
        name="attn_out_router",
    )(o, h2, b_w_o[0].astype(_BF16), row(ffn_norm[1]), wrt1, br1, triu)

    dest2, y2 = _moe_experts(xn2, ri2, cnt2, w_gate, w_up, w_down, 1, n_real_tiles, n_real_tiles, d)
    operands, in_specs, scratch = _combine_operands(dest2, rf2, h3, y2, d)
    out = pl.pallas_call(
        functools.partial(_combine_norm_kernel, tm=tm, n_tiles=n_real_tiles),
        grid=(n_real_tiles,),
        in_specs=in_specs + [_full((1, d))],
        out_specs=pl.BlockSpec((tm, d), lambda i: (i, 0)),
        out_shape=jax.ShapeDtypeStruct((n_tok, d), _F32),
        scratch_shapes=scratch,
        compiler_params=_params(1, disable_bounds_checks=True),
        name="moe_combine_final_norm",
    )(*operands, row(final_norm))
    return out.reshape(bsz, seq, d)
```

```python
import functools

import numpy as np
import jax
import jax.numpy as jnp
from jax import lax
from jax.experimental import pallas as pl
from jax.experimental.pallas import tpu as pltpu

N_META = 16
POOL_WINDOWS = (2, 4, 8, 16)
N_HEADS = 8
QK_NOPE_DIM = 128
QK_ROPE_DIM = 64
QK_DIM = QK_NOPE_DIM + QK_ROPE_DIM
V_HEAD_DIM = 128
KV_LORA_RANK = 256
ROPE_THETA = 10000.0
ATTN_SCALE = QK_DIM ** -0.5
Q_SCALE = ATTN_SCALE * 1.4426950408889634
N_EXPERT_GROUPS = 4
EXPERTS_PER_GROUP = 8
N_EXPERTS = N_EXPERT_GROUPS * EXPERTS_PER_GROUP
RMS_EPS = 1e-6
NEG_INF = -1e30

TOKEN_TILE = 512
POOL_BLOCK = 256
EXPERT_ROWS = 256
EXPERT_BLOCKS_PER_STEP = 2
ATTN_Q_TILE = 512
ATTN_K_TILE = ATTN_Q_TILE // 2
ATTN_HEADS_PER_STEP = 4
DMA_UNROLL = 8
DMA_THREADS = 2
ROUTER_ROWS = 8 + N_EXPERTS
ROUTER_LO_ROW = 48
ROUTER_SPLIT_ROWS = 2 * ROUTER_LO_ROW
VMEM_LIMIT_BYTES = 48 * 1024 * 1024

LANES = 128
_F32 = jnp.float32
_BF16 = jnp.bfloat16
_NT_DIMS = (((1,), (1,)), ((), ()))


def _params(n_grid_dims=1, **kw):
    return pltpu.CompilerParams(dimension_semantics=("arbitrary",) * n_grid_dims,
                                vmem_limit_bytes=VMEM_LIMIT_BYTES, **kw)


def _rms(x, g):
    ms = jnp.mean(x * x, axis=-1, keepdims=True)
    return x * lax.rsqrt(ms + RMS_EPS) * g


def _split_bf16(x):
    hi = x.astype(_BF16)
    lo = (x - hi.astype(_F32)).astype(_BF16)
    return hi, lo


def _dot(a, b):
    return jnp.dot(a, b, preferred_element_type=_F32)


def _store_token_tiles(ref, x):
    n, d = x.shape
    parts = d // LANES
    for s in range(parts):
        ref[pl.ds(s, n, stride=parts), :] = x[:, s * LANES:(s + 1) * LANES]


def _load_token_tiles(ref, first_token, n, parts):
    return jnp.concatenate([ref[pl.ds(first_token * parts + s, n, stride=parts), :] for s in range(parts)], axis=1)


_HIGH_HALF = 0xFFFF0000


def _pack_rows(x):
    n, d = x.shape
    words = []
    for s in range(d // (2 * LANES)):
        lo = x[:, 2 * s * LANES:(2 * s + 1) * LANES].astype(_BF16).astype(_F32)
        hi = x[:, (2 * s + 1) * LANES:(2 * s + 2) * LANES].astype(_BF16).astype(_F32)
        words.append(lax.shift_right_logical(lax.bitcast_convert_type(lo, jnp.uint32), jnp.uint32(16))
                     | (lax.bitcast_convert_type(hi, jnp.uint32) & jnp.uint32(_HIGH_HALF)))
    return jnp.concatenate(words, axis=1)


def _unpack_rows(w):
    cols = []
    for s in range(w.shape[1] // LANES):
        ws = w[:, s * LANES:(s + 1) * LANES]
        cols.append(lax.bitcast_convert_type(lax.shift_left(ws, jnp.uint32(16)), _F32))
        cols.append(lax.bitcast_convert_type(ws & jnp.uint32(_HIGH_HALF), _F32))
    return jnp.concatenate(cols, axis=1)


def _route_init(wrt_ref, wsplit_ref, base_ref):
    base_ref[...] = jnp.zeros(base_ref.shape, base_ref.dtype)
    w_hi, w_lo = _split_bf16(wrt_ref[...])
    wsplit_ref[...] = jnp.zeros(wsplit_ref.shape, wsplit_ref.dtype)
    wsplit_ref[0:ROUTER_ROWS, :] = w_hi
    wsplit_ref[ROUTER_LO_ROW:ROUTER_LO_ROW + ROUTER_ROWS, :] = w_lo


def _route(xn, valid, wsplit_ref, br_ref, triu_ref, base_ref, ri_ref, rf_ref, cnt_ref):
    tm = xn.shape[0]
    x_hi, x_lo = _split_bf16(xn)
    by_hi = lax.dot_general(wsplit_ref[...], x_hi, _NT_DIMS, preferred_element_type=_F32)
    by_lo = lax.dot_general(wsplit_ref[0:ROUTER_LO_ROW, :], x_lo, _NT_DIMS, preferred_element_type=_F32)
    logits = (by_hi[0:ROUTER_ROWS] + by_hi[ROUTER_LO_ROW:ROUTER_LO_ROW + ROUTER_ROWS]
              + by_lo[0:ROUTER_ROWS] + br_ref[...])
    lg = logits[0:N_EXPERT_GROUPS]
    eg = jnp.exp(lg - jnp.max(lg, axis=0, keepdims=True))
    pg = eg / jnp.sum(eg, axis=0, keepdims=True)
    w_g = jnp.max(pg, axis=0, keepdims=True)
    ig = lax.broadcasted_iota(jnp.int32, pg.shape, 0).astype(_F32)
    g_sel = jnp.min(jnp.where(pg == w_g, ig, float(N_EXPERT_GROUPS)), axis=0, keepdims=True)

    sel = logits[8:8 + EXPERTS_PER_GROUP]
    for g in range(1, N_EXPERT_GROUPS):
        sel = jnp.where(g_sel == float(g), logits[8 + g * EXPERTS_PER_GROUP:8 + (g + 1) * EXPERTS_PER_GROUP], sel)
    ie = lax.broadcasted_iota(jnp.int32, sel.shape, 0).astype(_F32)
    v1 = jnp.max(sel, axis=0, keepdims=True)
    i1 = jnp.min(jnp.where(sel == v1, ie, float(EXPERTS_PER_GROUP)), axis=0, keepdims=True)
    rest = jnp.where(ie == i1, -jnp.inf, sel)
    v2 = jnp.max(rest, axis=0, keepdims=True)
    i2 = jnp.min(jnp.where(rest == v2, ie, float(EXPERTS_PER_GROUP)), axis=0, keepdims=True)
    e2 = jnp.exp(v2 - v1)
    den = 1.0 + e2
    validf = valid.astype(_F32)
    gate0 = w_g * (1.0 / den) * validf
    gate1 = w_g * (e2 / den) * validf
    eid0 = g_sel * float(EXPERTS_PER_GROUP) + i1
    eid1 = g_sel * float(EXPERTS_PER_GROUP) + i2

    iall = lax.broadcasted_iota(jnp.int32, (N_EXPERTS, tm), 0).astype(_F32)
    oh0 = jnp.where(iall == eid0, validf, 0.0)
    oh1 = jnp.where(iall == eid1, validf, 0.0)
    both = oh0 + oh1
    before = _dot(both.astype(_BF16), triu_ref[...]) + base_ref[...]
    rank0 = jnp.sum(oh0 * before, axis=0, keepdims=True)
    rank1 = jnp.sum(oh1 * before, axis=0, keepdims=True)
    base_ref[...] = base_ref[...] + jnp.sum(both, axis=1, keepdims=True)

    ri_ref[...] = jnp.zeros(ri_ref.shape, ri_ref.dtype)
    rf_ref[...] = jnp.zeros(rf_ref.shape, rf_ref.dtype)
    ri_ref[0, 0:1, :] = eid0.astype(jnp.int32)
    ri_ref[0, 1:2, :] = eid1.astype(jnp.int32)
    ri_ref[0, 2:3, :] = rank0.astype(jnp.int32)
    ri_ref[0, 3:4, :] = rank1.astype(jnp.int32)
    rf_ref[0, 0:1, :] = gate0
    rf_ref[0, 1:2, :] = gate1
    cnt_ref[...] = jnp.broadcast_to(base_ref[...], cnt_ref.shape).astype(jnp.int32)


def _mixer_kernel(x_ref, xh_ref, mp_ref, meta_ref, anorm_ref, aw_ref, ascale_ref, fnorm_ref,
                  wrt_ref, br_ref, pm_ref, ph_ref, triu_ref,
                  h_ref, xn_ref, ri_ref, rf_ref, cnt_ref, base_ref, wsplit_ref, *, n_real_tiles, tiles_per_batch):
    i = pl.program_id(0)
    tm = x_ref.shape[0]
    gd = x_ref.shape[1] // len(POOL_WINDOWS)
    is_meta = i == n_real_tiles
    first = (i % tiles_per_batch) == 0

    @pl.when(i == 0)
    def _():
        _route_init(wrt_ref, wsplit_ref, base_ref)

    h = jnp.where(is_meta, mp_ref[...], x_ref[...])
    halo = jnp.where(is_meta, 0.0, jnp.where(first, meta_ref[...], xh_ref[...]))
    hn = _rms(h, anorm_ref[...])
    hh = _rms(halo, anorm_ref[...])
    hn_hi, hn_lo = _split_bf16(hn)
    hh_hi, hh_lo = _split_bf16(hh)
    pb = pm_ref.shape[1]
    for c in range(tm // pb):
        rows = slice(c * pb, (c + 1) * pb)
        row = c * pb + lax.broadcasted_iota(jnp.int32, (pb, 1), 0)
        for g, w in enumerate(POOL_WINDOWS):
            sl = slice(g * gd, (g + 1) * gd)
            win = _dot(pm_ref[g], hn_hi[rows, sl]) + _dot(pm_ref[g], hn_lo[rows, sl])
            if c == 0:
                before_hi, before_lo = hh_hi[:, sl], hh_lo[:, sl]
            else:
                before_hi, before_lo = hn_hi[c * pb - N_META:c * pb, sl], hn_lo[c * pb - N_META:c * pb, sl]
            top = win[:N_META] + _dot(ph_ref[g], before_hi) + _dot(ph_ref[g], before_lo)
            win = jnp.concatenate([top, win[N_META:]], axis=0)
            cnt = jnp.where(is_meta, jnp.minimum(row + 1, w), w).astype(_F32)
            pooled = win * (1.0 / cnt) - hn[rows, sl]
            mix = _dot(pooled.astype(_BF16), aw_ref[g])
            h_ref[rows, sl] = h[rows, sl] + mix * ascale_ref[:, sl]

    xn = _rms(h_ref[...], fnorm_ref[...])
    _store_token_tiles(xn_ref, _pack_rows(xn))
    lane = lax.broadcasted_iota(jnp.int32, (1, tm), 1)
    valid = jnp.logical_or(jnp.logical_not(is_meta), lane < N_META)
    _route(xn, valid, wsplit_ref, br_ref, triu_ref, base_ref, ri_ref, rf_ref, cnt_ref)


def _index_copy(dest_hbm, idx_smem, isem, tile, n):
    return pltpu.make_async_copy(dest_hbm.at[pl.ds(pl.multiple_of(tile * n, n), n)],
                                 idx_smem.at[pl.ds(pl.multiple_of((tile % 2) * n, n), n)], isem.at[tile % 2])


def _dispatch_kernel(pad_start_ref, pad_len_ref, tail_ref, dest_hbm, xn_hbm, xs_hbm,
                     idx_smem, xbuf, zbuf, isem, lsem, ssem, zsem, *, tm, tr, n_tiles, n_real_tiles, n_blocks):
    i = pl.program_id(0)
    parts = xbuf.shape[1] // tm
    buf = i % 3

    def zero_fill(wait):
        def zero_rows(first_row, n_rows):
            cp = pltpu.make_async_copy(
                zbuf.at[pl.ds(0, n_rows * parts), :],
                xs_hbm.at[pl.ds(pl.multiple_of(first_row * parts, parts), n_rows * parts), :], zsem)
            if wait:
                cp.wait()
            else:
                cp.start()

        def per_expert(e, c):
            first, length = pad_start_ref[e], pad_len_ref[e]
            for bit in range((tr - 1).bit_length()):
                @pl.when((length >> bit) % 2 == 1)
                def _(bit=bit):
                    zero_rows(first + ((length >> (bit + 1)) << (bit + 1)), 1 << bit)
            return c

        def per_block(blk, c):
            zero_rows(blk * tr, tr)
            return c

        lax.fori_loop(0, N_EXPERTS, per_expert, 0)
        lax.fori_loop(tail_ref[0] // tr, n_blocks, per_block, 0)

    def idx_copy(tile):
        return _index_copy(dest_hbm, idx_smem, isem, tile, 2 * tm)

    def load(tile):
        rows = tm * parts
        return pltpu.make_async_copy(xn_hbm.at[pl.ds(pl.multiple_of(tile * rows, rows), rows), :],
                                     xbuf.at[tile % 3], lsem.at[tile % 3])

    def wait_scatters(tile):
        def wait_rows(n_tok):
            for k in range(2):
                pltpu.make_async_copy(xbuf.at[tile % 3, pl.ds(0, n_tok * parts), :],
                                      xs_hbm.at[pl.ds(0, n_tok * parts), :], ssem.at[tile % 3]).wait()

        @pl.when(tile < n_real_tiles)
        def _():
            wait_rows(tm)

        @pl.when(tile >= n_real_tiles)
        def _():
            wait_rows(N_META)

    def scatter(n_tok):
        def issue(g, c):
            for u in range(min(DMA_UNROLL, n_tok)):
                t = g * min(DMA_UNROLL, n_tok) + u
                src = xbuf.at[buf, pl.ds(pl.multiple_of(t * parts, parts), parts), :]
                for k in range(2):
                    d = idx_smem[(i % 2) * (2 * tm) + k * tm + t]
                    pltpu.make_async_copy(src, xs_hbm.at[pl.ds(pl.multiple_of(d * parts, parts), parts), :],
                                          ssem.at[buf]).start(priority=(2 * u + k) % DMA_THREADS)
            return c

        lax.fori_loop(0, n_tok // min(DMA_UNROLL, n_tok), issue, 0)

    @pl.when(i == 0)
    def _():
        idx_copy(0).start()
        load(0).start()
        zbuf[...] = jnp.zeros(zbuf.shape, zbuf.dtype)
        zero_fill(wait=False)

    @pl.when(i >= 2)
    def _():
        wait_scatters(i - 2)

    @pl.when(i + 1 < n_tiles)
    def _():
        idx_copy(i + 1).start()
        load(i + 1).start()

    idx_copy(i).wait()
    load(i).wait()

    @pl.when(i < n_real_tiles)
    def _():
        scatter(tm)

    @pl.when(i >= n_real_tiles)
    def _():
        scatter(N_META)

    @pl.when(i == n_tiles - 1)
    def _():
        if n_tiles >= 2:
            wait_scatters(i - 1)
        wait_scatters(i)
        zero_fill(wait=True)


def _expert_kernel(be_ref, bv_ref, nu_ref, xs_ref, *refs):
    nb = EXPERT_BLOCKS_PER_STEP
    w_refs, y_ref, wb_refs = refs[:3 * nb], refs[3 * nb], refs[3 * nb + 1:]
    i = pl.program_id(0)
    n_used = nu_ref[0]
    parts = wb_refs[0].shape[0] // (2 * LANES)
    tr = xs_ref.shape[0] // (nb * parts)

    for c in range(nb):
        blk = i * nb + c
        changed = be_ref[blk] != be_ref[jnp.maximum(blk - nb, 0)]

        @pl.when(jnp.logical_or(i == 0, jnp.logical_and(blk < n_used, changed)))
        def _(c=c):
            for w_ref, wb_ref in zip(w_refs[3 * c:3 * c + 3], wb_refs[3 * c:3 * c + 3]):
                wb_ref[...] = w_ref[0, 0].astype(_BF16)

    @pl.when(i * nb < n_used)
    def _():
        rows = lax.broadcasted_iota(jnp.int32, (tr, 1), 0)
        for c in range(nb):
            blk = i * nb + c
            wgb_ref, wub_ref, wdb_ref = wb_refs[3 * c:3 * c + 3]
            x = jnp.where(rows < bv_ref[blk], _unpack_rows(_load_token_tiles(xs_ref, c * tr, tr, parts)), 0.0)
            x = x.astype(_BF16)
            g = _dot(x, wgb_ref[...])
            u = _dot(x, wub_ref[...])
            a = g * (1.0 / (1.0 + jnp.exp(-g))) * u
            y = jnp.where(blk < n_used, _dot(a.astype(_BF16), wdb_ref[...]), 0.0)
            _store_token_tiles(y_ref.at[pl.ds(c * tr * parts, tr * parts), :], _pack_rows(y))

    @pl.when(i * nb >= n_used)
    def _():
        y_ref[...] = jnp.zeros(y_ref.shape, y_ref.dtype)


def _combined_tile(dest_hbm, rf_ref, h_ref, y_hbm, idx_smem, ybufs, isem, rsem, *, tm, n_tiles, slot):
    i = pl.program_id(0)
    parts = ybufs[0].shape[0] // (2 * tm)

    def idx_copy(tile):
        return _index_copy(dest_hbm, idx_smem, isem, tile, 2 * tm)

    def row_copy(idx_slot, buf_slot, t, k, u):
        d = idx_smem[idx_slot * (2 * tm) + k * tm + t]
        row = (k * tm + t) * parts
        if not isinstance(t, int):
            row = pl.multiple_of(row, parts)
        return pltpu.make_async_copy(
            y_hbm.at[pl.ds(pl.multiple_of(d * parts, parts), parts), :],
            ybufs[buf_slot].at[pl.ds(row, parts), :],
            rsem.at[buf_slot]).start(priority=(2 * u + k) % DMA_THREADS)

    def wait_rows(buf_slot):
        pltpu.make_async_copy(y_hbm.at[pl.ds(0, 2 * tm * parts), :], ybufs[buf_slot], rsem.at[buf_slot]).wait()

    if slot == 0:
        @pl.when(i == 0)
        def _():
            idx_copy(0).start()
            idx_copy(0).wait()

            def issue(g, c):
                for u in range(DMA_UNROLL):
                    for k in range(2):
                        row_copy(0, 0, g * DMA_UNROLL + u, k, u)
                return c

            lax.fori_loop(0, tm // DMA_UNROLL, issue, 0)
            if n_tiles >= 2:
                idx_copy(1).start()

    @pl.when(i + 1 < n_tiles)
    def _():
        idx_copy(i + 1).wait()

    @pl.when(i + 2 < n_tiles)
    def _():
        idx_copy(i + 2).start()

    nxt_idx_slot = jnp.minimum(i + 1, n_tiles - 1) % 2

    def prefetch(part, n_parts):
        for t in range(part * tm // n_parts, (part + 1) * tm // n_parts):
            for k in range(2):
                row_copy(nxt_idx_slot, 1 - slot, t, k, t)

    gates = rf_ref[0]
    gt = jnp.concatenate([gates, jnp.zeros((LANES - gates.shape[0], tm), _F32)], axis=0).T
    wait_rows(slot)
    yb = ybufs[slot]

    def combined(r0, n):
        return h_ref[r0:r0 + n, :] + (_unpack_rows(_load_token_tiles(yb, r0, n, parts)) * gt[r0:r0 + n, 0:1]
                                      + _unpack_rows(_load_token_tiles(yb, tm + r0, n, parts)) * gt[r0:r0 + n, 1:2])

    def drain():
        @pl.when(i == n_tiles - 1)
        def _():
            wait_rows(1 - slot)

    return combined, prefetch, drain


def _for_each_parity(body):
    for slot in range(2):
        @pl.when(pl.program_id(0) % 2 == slot)
        def _(slot=slot):
            body(slot)


def _combine_norm_kernel(dest_hbm, rf_ref, h_ref, y_hbm, fnorm_ref, out_ref,
                         idx_smem, ybuf0, ybuf1, isem, rsem, *, tm, n_tiles):
    def body(slot):
        combined, prefetch, drain = _combined_tile(dest_hbm, rf_ref, h_ref, y_hbm, idx_smem, (ybuf0, ybuf1),
                                                   isem, rsem, tm=tm, n_tiles=n_tiles, slot=slot)
        n_chunks = 4
        n = tm // n_chunks
        for c in range(n_chunks):
            prefetch(c, n_chunks)
            out_ref[c * n:(c + 1) * n, :] = _rms(combined(c * n, n), fnorm_ref[...])
        drain()

    _for_each_parity(body)


def _combine_proj_kernel(dest_hbm, rf_ref, h_ref, y_hbm, cos_ref, sin_ref, kvn_ref, wdkv_ref, kvlat_ref,
                         wuk_ref, wuv_ref, bnorm_ref, wdq_ref, qnorm_ref, wuq_ref,
                         h_out_ref, q_ref, k_ref, v_ref, idx_smem, ybuf0, ybuf1, isem, rsem, *, tm, n_tiles):
    def body(slot):
        combined, prefetch, drain = _combined_tile(dest_hbm, rf_ref, h_ref, y_hbm, idx_smem, (ybuf0, ybuf1),
                                                   isem, rsem, tm=tm, n_tiles=n_tiles, slot=slot)
        n = v_ref.shape[3]
        n_chunks = tm // n
        for c in range(n_chunks):
            prefetch(c, n_chunks)
            h = combined(c * n, n)
            h_out_ref[c * n:(c + 1) * n, :] = h
            _project(h, c, cos_ref, sin_ref, kvn_ref, wdkv_ref, kvlat_ref, wuk_ref, wuv_ref,
                     bnorm_ref, wdq_ref, qnorm_ref, wuq_ref, q_ref, k_ref, v_ref)
        drain()

    _for_each_parity(body)


def _rope128(x, cos_t, sin_t):
    lane = lax.broadcasted_iota(jnp.int32, (1, 128), 1)
    first_half = (lane % QK_ROPE_DIM) < (QK_ROPE_DIM // 2)
    swapped = jnp.where(first_half, pltpu.roll(x, 128 - QK_ROPE_DIM // 2, axis=1),
                        pltpu.roll(x, QK_ROPE_DIM // 2, axis=1))
    return x * cos_t + swapped * sin_t


def _project(h, chunk, cos_ref, sin_ref, kvn_ref, wdkv_ref, kvlat_ref, wuk_ref, wuv_ref,
             bnorm_ref, wdq_ref, qnorm_ref, wuq_ref, q_ref, k_ref, v_ref):
    n = h.shape[0]
    rows = slice(chunk * n, (chunk + 1) * n)
    cos_t = cos_ref[rows, :]
    sin_t = sin_ref[rows, :]
    c = _dot(_rms(h, kvn_ref[...]).astype(_BF16), wdkv_ref[...])
    ckv = _rms(c[:, :KV_LORA_RANK], kvlat_ref[...]).astype(_BF16)
    kr = _rope128(c[:, KV_LORA_RANK:KV_LORA_RANK + 128], cos_t, sin_t)[:, :QK_ROPE_DIM].astype(_BF16)
    kn = _dot(ckv, wuk_ref[...])
    vt = lax.dot_general(wuv_ref[...], ckv, _NT_DIMS, preferred_element_type=_F32)
    cq = _rms(_dot(_rms(h, bnorm_ref[...]).astype(_BF16), wdq_ref[...]), qnorm_ref[...])
    q = _dot((cq * Q_SCALE).astype(_BF16), wuq_ref[...])
    rope0 = N_HEADS * QK_NOPE_DIM
    for hd in range(N_HEADS):
        k_ref[hd, rows, 0:QK_NOPE_DIM] = kn[:, hd * QK_NOPE_DIM:(hd + 1) * QK_NOPE_DIM].astype(_BF16)
        k_ref[hd, rows, QK_NOPE_DIM:QK_DIM] = kr
        v_ref[hd, chunk] = vt[hd * V_HEAD_DIM:(hd + 1) * V_HEAD_DIM, :].astype(_BF16)
        q_ref[hd, rows, 0:QK_NOPE_DIM] = q[:, hd * QK_NOPE_DIM:(hd + 1) * QK_NOPE_DIM].astype(_BF16)
    for pair in range(N_HEADS // 2):
        qr = _rope128(q[:, rope0 + pair * LANES:rope0 + (pair + 1) * LANES], cos_t, sin_t)
        q_ref[2 * pair, rows, QK_NOPE_DIM:QK_DIM] = qr[:, :QK_ROPE_DIM].astype(_BF16)
        q_ref[2 * pair + 1, rows, QK_NOPE_DIM:QK_DIM] = qr[:, QK_ROPE_DIM:].astype(_BF16)


def _attn_kernel(q_ref, k_ref, vt_ref, km_ref, vmt_ref, o_ref, sa_ref, sb_ref, acc_ref, *, tq, tk):
    j = pl.program_id(2)
    heads = range(q_ref.shape[0])
    vd = acc_ref.shape[1]

    def scores(kb, s_ref):
        start = pl.multiple_of(kb * tk, tk)
        for hd in heads:
            s_ref[hd] = lax.dot_general(k_ref[hd, pl.ds(start, tk), :], q_ref[hd], _NT_DIMS,
                                        preferred_element_type=_F32)

    def update(kb, s_ref, carry, masked):
        out = []
        for hd in heads:
            m, l = carry[hd]
            s = s_ref[hd]
            if masked:
                keys = kb * tk + lax.broadcasted_iota(jnp.int32, (tk, tq), 0)
                qpos = j * tq + lax.broadcasted_iota(jnp.int32, (tk, tq), 1)
                s = jnp.where(keys <= qpos, s, NEG_INF)
            m_new = jnp.maximum(m, jnp.max(s, axis=0, keepdims=True))
            alpha = jnp.exp2(m - m_new)
            p = jnp.exp2(s - m_new)
            l = alpha * l + jnp.sum(p, axis=0, keepdims=True)
            acc_ref[hd] = acc_ref[hd] * alpha + _dot(vt_ref[hd, kb], p.astype(_BF16))
            out.append((m_new, l))
        return tuple(out)

    scores(0, sa_ref)
    carry = []
    for hd in heads:
        s0 = lax.dot_general(km_ref[hd], q_ref[hd], _NT_DIMS, preferred_element_type=_F32)
        m = jnp.max(s0, axis=0, keepdims=True)
        p0 = jnp.exp2(s0 - m)
        carry.append((m, jnp.sum(p0, axis=0, keepdims=True)))
        acc_ref[hd] = _dot(vmt_ref[hd, 0, :, 0:N_META], p0.astype(_BF16))
    carry = tuple(carry)

    def pair(kp, carry):
        scores(2 * kp + 1, sb_ref)
        carry = update(2 * kp, sa_ref, carry, False)
        scores(2 * kp + 2, sa_ref)
        return update(2 * kp + 1, sb_ref, carry, False)

    carry = lax.fori_loop(0, j, pair, carry)
    scores(2 * j + 1, sb_ref)
    carry = update(2 * j, sa_ref, carry, True)
    carry = update(2 * j + 1, sb_ref, carry, True)
    for hd in heads:
        o_ref[:, hd * vd:(hd + 1) * vd] = (acc_ref[hd] / carry[hd][1]).T.astype(o_ref.dtype)


def _oproj_kernel(o_ref, h_ref, wo_ref, fnorm_ref, wrt_ref, br_ref, triu_ref,
                  h_out_ref, xn_ref, ri_ref, rf_ref, cnt_ref, base_ref, wsplit_ref):
    i = pl.program_id(0)

    @pl.when(i == 0)
    def _():
        _route_init(wrt_ref, wsplit_ref, base_ref)

    h = h_ref[...] + _dot(o_ref[...], wo_ref[...])
    h_out_ref[...] = h
    xn = _rms(h, fnorm_ref[...])
    _store_token_tiles(xn_ref, _pack_rows(xn))
    valid = lax.broadcasted_iota(jnp.int32, (1, h.shape[0]), 1) >= 0
    _route(xn, valid, wsplit_ref, br_ref, triu_ref, base_ref, ri_ref, rf_ref, cnt_ref)


def _full(shape):
    nd = len(shape)
    return pl.BlockSpec(shape, lambda *_: (0,) * nd)


def _router_operands(router_g, router_g_bias, router_e, router_e_bias):
    d = router_g.shape[0]
    wrt = jnp.concatenate([router_g.T, jnp.zeros((8 - N_EXPERT_GROUPS, d), _F32), router_e.T], axis=0)
    br = jnp.concatenate([router_g_bias, jnp.zeros((8 - N_EXPERT_GROUPS,), _F32), router_e_bias])[:, None]
    return wrt.astype(_F32), br.astype(_F32)


def _route_out(n_tiles, tm):
    shapes = [jax.ShapeDtypeStruct((n_tiles, 8, tm), jnp.int32),
              jax.ShapeDtypeStruct((n_tiles, 8, tm), _F32),
              jax.ShapeDtypeStruct((N_EXPERTS, 128), jnp.int32)]
    specs = [pl.BlockSpec((1, 8, tm), lambda i: (i, 0, 0)),
             pl.BlockSpec((1, 8, tm), lambda i: (i, 0, 0)),
             pl.BlockSpec((N_EXPERTS, 128), lambda i: (0, 0))]
    return shapes, specs


def _moe_experts(xn, ri, counts, w_gate, w_up, w_down, layer, n_tiles, n_real_tiles, d):
    tm, tr = TOKEN_TILE, EXPERT_ROWS
    parts = d // (2 * LANES)
    n_valid = n_real_tiles * tm + (n_tiles - n_real_tiles) * N_META
    n_blocks = -(-(2 * n_valid + N_EXPERTS * (tr - 1)) // tr)
    n_blocks = -(-n_blocks // EXPERT_BLOCKS_PER_STEP) * EXPERT_BLOCKS_PER_STEP
    n_rows = n_blocks * tr

    counts = counts[:, 0]
    padded = (counts + tr - 1) // tr * tr
    pends = jnp.cumsum(padded)
    pstarts = pends - padded
    n_used = (pends[-1] // tr).astype(jnp.int32).reshape(1)
    blk0 = jnp.arange(n_blocks, dtype=jnp.int32) * tr
    block_e = jnp.minimum(jnp.sum(blk0[:, None] >= pends[None, :], axis=1), N_EXPERTS - 1).astype(jnp.int32)
    experts = jnp.arange(N_EXPERTS, dtype=jnp.int32)
    block_oh = block_e[:, None] == experts[None, :]
    block_cnt = jnp.sum(jnp.where(block_oh, counts[None, :], 0), axis=1)
    block_start = jnp.sum(jnp.where(block_oh, pstarts[None, :], 0), axis=1)
    block_valid = jnp.clip(block_cnt - (blk0 - block_start), 0, tr).astype(jnp.int32)
    eid = ri[:, 0:2, :]
    slot0 = jnp.sum(jnp.where(eid[..., None] == experts, pstarts, 0), axis=-1)
    dest = (slot0 + ri[:, 2:4, :]).astype(jnp.int32).reshape(n_tiles * 2 * tm)

    pad_start = (pstarts + counts).astype(jnp.int32)
    pad_len = (padded - counts).astype(jnp.int32)
    xs = pl.pallas_call(
        functools.partial(_dispatch_kernel, tm=tm, tr=tr, n_tiles=n_tiles, n_real_tiles=n_real_tiles,
                          n_blocks=n_blocks),
        grid_spec=pltpu.PrefetchScalarGridSpec(
            num_scalar_prefetch=3,
            grid=(n_tiles,),
            in_specs=[pl.BlockSpec(memory_space=pl.ANY), pl.BlockSpec(memory_space=pl.ANY)],
            out_specs=pl.BlockSpec(memory_space=pl.ANY),
            scratch_shapes=[pltpu.SMEM((4 * tm,), jnp.int32), pltpu.VMEM((3, tm * parts, LANES), jnp.uint32),
                            pltpu.VMEM((tr * parts, LANES), jnp.uint32),
                            pltpu.SemaphoreType.DMA((2,)), pltpu.SemaphoreType.DMA((3,)),
                            pltpu.SemaphoreType.DMA((3,)), pltpu.SemaphoreType.DMA]),
        out_shape=jax.ShapeDtypeStruct((n_rows * parts, LANES), jnp.uint32),
        compiler_params=_params(1, has_side_effects=True, disable_bounds_checks=True),
        name="moe_dispatch",
    )(pad_start, pad_len, pends[-1:].astype(jnp.int32), dest, xn)

    f = w_gate.shape[3]
    nb = EXPERT_BLOCKS_PER_STEP
    step_rows = nb * tr * parts

    def expert_block(c):
        return lambda i, be, bv, nu: (layer, be[jnp.minimum(i * nb + c, nu[0] - 1)], 0, 0)

    weight_specs, weight_scratch = [], []
    for c in range(nb):
        weight_specs += [pl.BlockSpec((1, 1, d, f), expert_block(c)), pl.BlockSpec((1, 1, d, f), expert_block(c)),
                         pl.BlockSpec((1, 1, f, d), expert_block(c))]
        weight_scratch += [pltpu.VMEM((d, f), _BF16), pltpu.VMEM((d, f), _BF16), pltpu.VMEM((f, d), _BF16)]
    y = pl.pallas_call(
        _expert_kernel,
        grid_spec=pltpu.PrefetchScalarGridSpec(
            num_scalar_prefetch=3,
            grid=(n_blocks // nb,),
            in_specs=[pl.BlockSpec((step_rows, LANES),
                                   lambda i, be, bv, nu: (jnp.minimum(i, (nu[0] - 1) // nb), 0))] + weight_specs,
            out_specs=pl.BlockSpec((step_rows, LANES), lambda i, be, bv, nu: (i, 0)),
            scratch_shapes=weight_scratch),
        out_shape=jax.ShapeDtypeStruct((n_rows * parts, LANES), jnp.uint32),
        compiler_params=_params(1),
        name="moe_experts",
    )(block_e, block_valid, n_used, xs, *([w_gate, w_up, w_down] * nb))

    return dest, y


def _combine_operands(dest, rf, h, y, d):
    tm = TOKEN_TILE
    parts = d // (2 * LANES)
    in_specs = [pl.BlockSpec(memory_space=pl.ANY),
                pl.BlockSpec((1, 8, tm), lambda i: (i, 0, 0)),
                pl.BlockSpec((tm, d), lambda i: (i, 0)),
                pl.BlockSpec(memory_space=pl.ANY)]
    scratch = [pltpu.SMEM((4 * tm,), jnp.int32),
               pltpu.VMEM((2 * tm * parts, LANES), jnp.uint32), pltpu.VMEM((2 * tm * parts, LANES), jnp.uint32),
               pltpu.SemaphoreType.DMA((2,)), pltpu.SemaphoreType.DMA((2,))]
    return (dest, rf, h, y), in_specs, scratch


def kernel(x, meta_tokens, a_norm, a_w, a_scale, b_norm, b_w_dq, b_q_norm, b_w_uq, b_w_o, kv_norm, w_dkv,
           kv_lat_norm, w_uk, w_uv, ffn_norm, router_g, router_g_bias, router_e, router_e_bias, w_gate, w_up,
           w_down, final_norm):
    bsz, seq, d = x.shape
    tm = TOKEN_TILE
    assert seq % tm == 0 and seq % ATTN_Q_TILE == 0 and ATTN_Q_TILE == 2 * ATTN_K_TILE and tm % ATTN_K_TILE == 0
    kt = tm // ATTN_K_TILE
    assert d % (LANES * len(POOL_WINDOWS)) == 0 and N_META == max(POOL_WINDOWS) and N_META <= tm
    parts = d // (2 * LANES)
    n_tok = bsz * seq
    n_real_tiles = n_tok // tm
    tiles_per_batch = seq // tm
    n_tiles = n_real_tiles + 1
    gd = d // len(POOL_WINDOWS)
    row = lambda v: v.reshape(1, -1).astype(_F32)

    x2 = x.reshape(n_tok, d)
    meta_pad = jnp.concatenate([meta_tokens, jnp.zeros((tm - N_META, d), x.dtype)], axis=0)

    r = np.arange(tm)[:, None]
    cidx = np.arange(tm)[None, :]
    pb = min(POOL_BLOCK, tm)
    pm = np.stack([((r[:pb] - cidx[:, :pb] >= 0) & (r[:pb] - cidx[:, :pb] < w)) for w in POOL_WINDOWS]).astype(np.float32)
    hc = np.arange(N_META)[None, :]
    ph = np.stack([(r[:N_META] + N_META - hc < w) for w in POOL_WINDOWS]).astype(np.float32)
    triu = (r < cidx).astype(np.float32)
    pm, ph, triu = (jnp.asarray(a, dtype=_BF16) for a in (pm, ph, triu))

    wrt0, br0 = _router_operands(router_g[0], router_g_bias[0], router_e[0], router_e_bias[0])
    route_shapes, route_specs = _route_out(n_tiles, tm)
    tile_or_last = lambda i: (jnp.minimum(i, n_real_tiles - 1), 0)
    halo_blocks = tm // N_META
    h1, xn1, ri1, rf1, cnt1 = pl.pallas_call(
        functools.partial(_mixer_kernel, n_real_tiles=n_real_tiles, tiles_per_batch=tiles_per_batch),
        grid=(n_tiles,),
        in_specs=[pl.BlockSpec((tm, d), tile_or_last),
                  pl.BlockSpec((N_META, d), lambda i: (jnp.clip(i * halo_blocks - 1, 0, n_tok // N_META - 1), 0)),
                  _full((tm, d)), _full((N_META, d)), _full((1, d)),
                  _full((len(POOL_WINDOWS), gd, gd)), _full((1, d)), _full((1, d)),
                  _full((ROUTER_ROWS, d)), _full((ROUTER_ROWS, 1)),
                  _full(pm.shape), _full(ph.shape), _full((tm, tm))],
        out_specs=[pl.BlockSpec((tm, d), lambda i: (i, 0)),
                   pl.BlockSpec((tm * parts, LANES), lambda i: (i, 0))] + route_specs,
        out_shape=[jax.ShapeDtypeStruct((n_tiles * tm, d), _F32),
                   jax.ShapeDtypeStruct((n_tiles * tm * parts, LANES), jnp.uint32)] + route_shapes,
        scratch_shapes=[pltpu.VMEM((N_EXPERTS, 1), _F32), pltpu.VMEM((ROUTER_SPLIT_ROWS, d), _BF16)],
        compiler_params=_params(1),
        name="pool_mixer_router",
    )(x2, x2, meta_pad, meta_tokens, row(a_norm[0]), a_w[0].astype(_BF16), row(a_scale[0]), row(ffn_norm[0]),
      wrt0, br0, pm, ph, triu)

    dest1, y1 = _moe_experts(xn1, ri1, cnt1, w_gate, w_up, w_down, 0, n_tiles, n_real_tiles, d)

    pos = jnp.concatenate([jnp.arange(seq, dtype=_F32) + N_META, jnp.arange(tm, dtype=_F32)])
    inv_freq = ROPE_THETA ** (-jnp.arange(0, QK_ROPE_DIM, 2, dtype=_F32) / QK_ROPE_DIM)
    ang = pos[:, None] * inv_freq[None, :]
    cos_t = jnp.tile(jnp.cos(ang), (1, 4))
    sin_t = jnp.tile(jnp.concatenate([-jnp.sin(ang), jnp.sin(ang)], axis=1), (1, 2))

    wdkv = jnp.concatenate([w_dkv, w_dkv[:, KV_LORA_RANK:]], axis=1).astype(_BF16)
    wuk = w_uk.reshape(KV_LORA_RANK, N_HEADS * QK_NOPE_DIM).astype(_BF16)
    wuv = w_uv.reshape(KV_LORA_RANK, N_HEADS * V_HEAD_DIM).T.astype(_BF16)
    wuq = b_w_uq[0]
    q_rank = wuq.shape[0]
    wuq = jnp.concatenate([wuq[:, :, :QK_NOPE_DIM].reshape(q_rank, -1),
                           wuq[:, :, QK_NOPE_DIM:].reshape(q_rank, -1)], axis=1).astype(_BF16)
    wdq = b_w_dq[0].astype(_BF16)
    pos_tile = lambda i: (jnp.where(i < n_real_tiles, i % tiles_per_batch, tiles_per_batch), 0)
    head_tile = lambda i: (0, i, 0)
    operands, in_specs, scratch = _combine_operands(dest1, rf1, h1, y1, d)
    h2, q, k, v = pl.pallas_call(
        functools.partial(_combine_proj_kernel, tm=tm, n_tiles=n_tiles),
        grid=(n_tiles,),
        in_specs=in_specs + [
            pl.BlockSpec((tm, 128), pos_tile), pl.BlockSpec((tm, 128), pos_tile),
            _full((1, d)), _full(wdkv.shape), _full((1, KV_LORA_RANK)), _full(wuk.shape), _full(wuv.shape),
            _full((1, d)), _full(wdq.shape), _full((1, q_rank)), _full(wuq.shape)],
        out_specs=[pl.BlockSpec((tm, d), lambda i: (i, 0)),
                   pl.BlockSpec((N_HEADS, tm, QK_DIM), head_tile),
                   pl.BlockSpec((N_HEADS, tm, QK_DIM), head_tile),
                   pl.BlockSpec((N_HEADS, kt, V_HEAD_DIM, ATTN_K_TILE), lambda i: (0, i, 0, 0))],
        out_shape=[jax.ShapeDtypeStruct((n_tiles * tm, d), _F32),
                   jax.ShapeDtypeStruct((N_HEADS, n_tiles * tm, QK_DIM), _BF16),
                   jax.ShapeDtypeStruct((N_HEADS, n_tiles * tm, QK_DIM), _BF16),
                   jax.ShapeDtypeStruct((N_HEADS, n_tiles * kt, V_HEAD_DIM, ATTN_K_TILE), _BF16)],
        scratch_shapes=scratch,
        compiler_params=_params(1, disable_bounds_checks=True),
        name="moe_combine_latent_qkv",
    )(*operands, cos_t, sin_t, row(kv_norm), wdkv, row(kv_lat_norm), wuk, wuv,
      row(b_norm[0]), wdq, row(b_q_norm[0]), wuq)

    tq, tk, hp = ATTN_Q_TILE, ATTN_K_TILE, ATTN_HEADS_PER_STEP
    n_q = seq // tq
    meta_block = n_tok // N_META
    o = pl.pallas_call(
        functools.partial(_attn_kernel, tq=tq, tk=tk),
        grid=(bsz, N_HEADS // hp, n_q),
        in_specs=[pl.BlockSpec((hp, tq, QK_DIM), lambda b, hg, j: (hg, b * n_q + j, 0)),
                  pl.BlockSpec((hp, seq, QK_DIM), lambda b, hg, j: (hg, b, 0)),
                  pl.BlockSpec((hp, seq // tk, V_HEAD_DIM, tk), lambda b, hg, j: (hg, b, 0, 0)),
                  pl.BlockSpec((hp, N_META, QK_DIM), lambda b, hg, j: (hg, meta_block, 0)),
                  pl.BlockSpec((hp, 1, V_HEAD_DIM, tk), lambda b, hg, j: (hg, n_real_tiles * kt, 0, 0))],
        out_specs=pl.BlockSpec((tq, hp * V_HEAD_DIM), lambda b, hg, j: (b * n_q + j, hg)),
        out_shape=jax.ShapeDtypeStruct((n_tok, N_HEADS * V_HEAD_DIM), _BF16),
        scratch_shapes=[pltpu.VMEM((hp, tk, tq), _F32), pltpu.VMEM((hp, tk, tq), _F32),
                        pltpu.VMEM((hp, V_HEAD_DIM, tq), _F32)],
        compiler_params=_params(3),
        name="causal_attention",
    )(q, k, v, k, v)

    wrt1, br1 = _router_operands(router_g[1], router_g_bias[1], router_e[1], router_e_bias[1])
    route_shapes, route_specs = _route_out(n_real_tiles, tm)
    h3, xn2, ri2, rf2, cnt2 = pl.pallas_call(
        _oproj_kernel,
        grid=(n_real_tiles,),
        in_specs=[pl.BlockSpec((tm, N_HEADS * V_HEAD_DIM), lambda i: (i, 0)),
                  pl.BlockSpec((tm, d), lambda i: (i, 0)),
                  _full((N_HEADS * V_HEAD_DIM, d)), _full((1, d)),
                  _full((ROUTER_ROWS, d)), _full((ROUTER_ROWS, 1)), _full((tm, tm))],
        out_specs=[pl.BlockSpec((tm, d), lambda i: (i, 0)),
                   pl.BlockSpec((tm * parts, LANES), lambda i: (i, 0))] + route_specs,
        out_shape=[jax.ShapeDtypeStruct((n_tok, d), _F32),
                   jax.ShapeDtypeStruct((n_tok * parts, LANES), jnp.uint32)] + route_shapes,
        scratch_shapes=[pltpu.VMEM((N_EXPERTS, 1), _F32), pltpu.VMEM((ROUTER_SPLIT_ROWS, d), _BF16)],
        compiler_params=_params(1),
        name="attn_out_router",
    )(o, h2, b_w_o[0].astype(_BF16), row(ffn_norm[1]), wrt1, br1, triu)

    dest2, y2 = _moe_experts(xn2, ri2, cnt2, w_gate, w_up, w_down, 1, n_real_tiles, n_real_tiles, d)
    operands, in_specs, scratch = _combine_operands(dest2, rf2, h3, y2, d)
    out = pl.pallas_call(
        functools.partial(_combine_norm_kernel, tm=tm, n_tiles=n_real_tiles),
        grid=(n_real_tiles,),
        in_specs=in_specs + [_full((1, d))],
        out_specs=pl.BlockSpec((tm, d), lambda i: (i, 0)),
        out_shape=jax.ShapeDtypeStruct((n_tok, d), _F32),
        scratch_shapes=scratch,
        compiler_params=_params(1, disable_bounds_checks=True),
        name="moe_combine_final_norm",
    )(*operands, row(final_norm))
    return out.reshape(bsz, seq, d)
```

```python
import functools

import numpy as np
import jax
import jax.numpy as jnp
from jax import lax
from jax.experimental import pallas as pl
from jax.experimental.pallas import tpu as pltpu

N_META = 16
POOL_WINDOWS = (2, 4, 8, 16)
N_HEADS = 8
QK_NOPE_DIM = 128
QK_ROPE_DIM = 64
QK_DIM = QK_NOPE_DIM + QK_ROPE_DIM
V_HEAD_DIM = 128
KV_LORA_RANK = 256
ROPE_THETA = 10000.0
ATTN_SCALE = QK_DIM ** -0.5
Q_SCALE = ATTN_SCALE * 1.4426950408889634
N_EXPERT_GROUPS = 4
EXPERTS_PER_GROUP = 8
N_EXPERTS = N_EXPERT_GROUPS * EXPERTS_PER_GROUP
RMS_EPS = 1e-6
NEG_INF = -1e30

TOKEN_TILE = 512
POOL_BLOCK = 256
EXPERT_ROWS = 256
EXPERT_BLOCKS_PER_STEP = 2
ATTN_Q_TILE = 512
ATTN_K_TILE = ATTN_Q_TILE // 2
ATTN_HEADS_PER_STEP = 4
DMA_UNROLL = 8
DMA_THREADS = 2
ROUTER_ROWS = 8 + N_EXPERTS
ROUTER_LO_ROW = 48
ROUTER_SPLIT_ROWS = 2 * ROUTER_LO_ROW
VMEM_LIMIT_BYTES = 48 * 1024 * 1024

LANES = 128
_F32 = jnp.float32
_BF16 = jnp.bfloat16
_NT_DIMS = (((1,), (1,)), ((), ()))


def _params(n_grid_dims=1, **kw):
    return pltpu.CompilerParams(dimension_semantics=("arbitrary",) * n_grid_dims,
                                vmem_limit_bytes=VMEM_LIMIT_BYTES, **kw)


def _rms(x, g):
    ms = jnp.mean(x * x, axis=-1, keepdims=True)
    return x * lax.rsqrt(ms + RMS_EPS) * g


def _split_bf16(x):
    hi = x.astype(_BF16)
    lo = (x - hi.astype(_F32)).astype(_BF16)
    return hi, lo


def _dot(a, b):
    return jnp.dot(a, b, preferred_element_type=_F32)


def _store_token_tiles(ref, x):
    n, d = x.shape
    parts = d // LANES
    for s in range(parts):
        ref[pl.ds(s, n, stride=parts), :] = x[:, s * LANES:(s + 1) * LANES]


def _load_token_tiles(ref, first_token, n, parts):
    return jnp.concatenate([ref[pl.ds(first_token * parts + s, n, stride=parts), :] for s in range(parts)], axis=1)


_HIGH_HALF = 0xFFFF0000


def _pack_rows(x):
    n, d = x.shape
    words = []
    for s in range(d // (2 * LANES)):
        lo = x[:, 2 * s * LANES:(2 * s + 1) * LANES].astype(_BF16).astype(_F32)
        hi = x[:, (2 * s + 1) * LANES:(2 * s + 2) * LANES].astype(_BF16).astype(_F32)
        words.append(lax.shift_right_logical(lax.bitcast_convert_type(lo, jnp.uint32), jnp.uint32(16))
                     | (lax.bitcast_convert_type(hi, jnp.uint32) & jnp.uint32(_HIGH_HALF)))
    return jnp.concatenate(words, axis=1)


def _unpack_rows(w):
    cols = []
    for s in range(w.shape[1] // LANES):
        ws = w[:, s * LANES:(s + 1) * LANES]
        cols.append(lax.bitcast_convert_type(lax.shift_left(ws, jnp.uint32(16)), _F32))
        cols.append(lax.bitcast_convert_type(ws & jnp.uint32(_HIGH_HALF), _F32))
    return jnp.concatenate(cols, axis=1)


def _route_init(wrt_ref, wsplit_ref, base_ref):
    base_ref[...] = jnp.zeros(base_ref.shape, base_ref.dtype)
    w_hi, w_lo = _split_bf16(wrt_ref[...])
    wsplit_ref[...] = jnp.zeros(wsplit_ref.shape, wsplit_ref.dtype)
    wsplit_ref[0:ROUTER_ROWS, :] = w_hi
    wsplit_ref[ROUTER_LO_ROW:ROUTER_LO_ROW + ROUTER_ROWS, :] = w_lo


def _route(xn, valid, wsplit_ref, br_ref, triu_ref, base_ref, ri_ref, rf_ref, cnt_ref):
    tm = xn.shape[0]
    x_hi, x_lo = _split_bf16(xn)
    by_hi = lax.dot_general(wsplit_ref[...], x_hi, _NT_DIMS, preferred_element_type=_F32)
    by_lo = lax.dot_general(wsplit_ref[0:ROUTER_LO_ROW, :], x_lo, _NT_DIMS, preferred_element_type=_F32)
    logits = (by_hi[0:ROUTER_ROWS] + by_hi[ROUTER_LO_ROW:ROUTER_LO_ROW + ROUTER_ROWS]
              + by_lo[0:ROUTER_ROWS] + br_ref[...])
    lg = logits[0:N_EXPERT_GROUPS]
    eg = jnp.exp(lg - jnp.max(lg, axis=0, keepdims=True))
    pg = eg / jnp.sum(eg, axis=0, keepdims=True)
    w_g = jnp.max(pg, axis=0, keepdims=True)
    ig = lax.broadcasted_iota(jnp.int32, pg.shape, 0).astype(_F32)
    g_sel = jnp.min(jnp.where(pg == w_g, ig, float(N_EXPERT_GROUPS)), axis=0, keepdims=True)

    sel = logits[8:8 + EXPERTS_PER_GROUP]
    for g in range(1, N_EXPERT_GROUPS):
        sel = jnp.where(g_sel == float(g), logits[8 + g * EXPERTS_PER_GROUP:8 + (g + 1) * EXPERTS_PER_GROUP], sel)
    ie = lax.broadcasted_iota(jnp.int32, sel.shape, 0).astype(_F32)
    v1 = jnp.max(sel, axis=0, keepdims=True)
    i1 = jnp.min(jnp.where(sel == v1, ie, float(EXPERTS_PER_GROUP)), axis=0, keepdims=True)
    rest = jnp.where(ie == i1, -jnp.inf, sel)
    v2 = jnp.max(rest, axis=0, keepdims=True)
    i2 = jnp.min(jnp.where(rest == v2, ie, float(EXPERTS_PER_GROUP)), axis=0, keepdims=True)
    e2 = jnp.exp(v2 - v1)
    den = 1.0 + e2
    validf = valid.astype(_F32)
    gate0 = w_g * (1.0 / den) * validf
    gate1 = w_g * (e2 / den) * validf
    eid0 = g_sel * float(EXPERTS_PER_GROUP) + i1
    eid1 = g_sel * float(EXPERTS_PER_GROUP) + i2

    iall = lax.broadcasted_iota(jnp.int32, (N_EXPERTS, tm), 0).astype(_F32)
    oh0 = jnp.where(iall == eid0, validf, 0.0)
    oh1 = jnp.where(iall == eid1, validf, 0.0)
    both = oh0 + oh1
    before = _dot(both.astype(_BF16), triu_ref[...]) + base_ref[...]
    rank0 = jnp.sum(oh0 * before, axis=0, keepdims=True)
    rank1 = jnp.sum(oh1 * before, axis=0, keepdims=True)
    base_ref[...] = base_ref[...] + jnp.sum(both, axis=1, keepdims=True)

    ri_ref[...] = jnp.zeros(ri_ref.shape, ri_ref.dtype)
    rf_ref[...] = jnp.zeros(rf_ref.shape, rf_ref.dtype)
    ri_ref[0, 0:1, :] = eid0.astype(jnp.int32)
    ri_ref[0, 1:2, :] = eid1.astype(jnp.int32)
    ri_ref[0, 2:3, :] = rank0.astype(jnp.int32)
    ri_ref[0, 3:4, :] = rank1.astype(jnp.int32)
    rf_ref[0, 0:1, :] = gate0
    rf_ref[0, 1:2, :] = gate1
    cnt_ref[...] = jnp.broadcast_to(base_ref[...], cnt_ref.shape).astype(jnp.int32)


def _mixer_kernel(x_ref, xh_ref, mp_ref, meta_ref, anorm_ref, aw_ref, ascale_ref, fnorm_ref,
                  wrt_ref, br_ref, pm_ref, ph_ref, triu_ref,
                  h_ref, xn_ref, ri_ref, rf_ref, cnt_ref, base_ref, wsplit_ref, *, n_real_tiles, tiles_per_batch):
    i = pl.program_id(0)
    tm = x_ref.shape[0]
    gd = x_ref.shape[1] // len(POOL_WINDOWS)
    is_meta = i == n_real_tiles
    first = (i % tiles_per_batch) == 0

    @pl.when(i == 0)
    def _():
        _route_init(wrt_ref, wsplit_ref, base_ref)

    h = jnp.where(is_meta, mp_ref[...], x_ref[...])
    halo = jnp.where(is_meta, 0.0, jnp.where(first, meta_ref[...], xh_ref[...]))
    hn = _rms(h, anorm_ref[...])
    hh = _rms(halo, anorm_ref[...])
    hn_hi, hn_lo = _split_bf16(hn)
    hh_hi, hh_lo = _split_bf16(hh)
    pb = pm_ref.shape[1]
    for c in range(tm // pb):
        rows = slice(c * pb, (c + 1) * pb)
        row = c * pb + lax.broadcasted_iota(jnp.int32, (pb, 1), 0)
        for g, w in enumerate(POOL_WINDOWS):
            sl = slice(g * gd, (g + 1) * gd)
            win = _dot(pm_ref[g], hn_hi[rows, sl]) + _dot(pm_ref[g], hn_lo[rows, sl])
            if c == 0:
                before_hi, before_lo = hh_hi[:, sl], hh_lo[:, sl]
            else:
                before_hi, before_lo = hn_hi[c * pb - N_META:c * pb, sl], hn_lo[c * pb - N_META:c * pb, sl]
            top = win[:N_META] + _dot(ph_ref[g], before_hi) + _dot(ph_ref[g], before_lo)
            win = jnp.concatenate([top, win[N_META:]], axis=0)
            cnt = jnp.where(is_meta, jnp.minimum(row + 1, w), w).astype(_F32)
            pooled = win * (1.0 / cnt) - hn[rows, sl]
            mix = _dot(pooled.astype(_BF16), aw_ref[g])
            h_ref[rows, sl] = h[rows, sl] + mix * ascale_ref[:, sl]

    xn = _rms(h_ref[...], fnorm_ref[...])
    _store_token_tiles(xn_ref, _pack_rows(xn))
    lane = lax.broadcasted_iota(jnp.int32, (1, tm), 1)
    valid = jnp.logical_or(jnp.logical_not(is_meta), lane < N_META)
    _route(xn, valid, wsplit_ref, br_ref, triu_ref, base_ref, ri_ref, rf_ref, cnt_ref)


def _index_copy(dest_hbm, idx_smem, isem, tile, n):
    return pltpu.make_async_copy(dest_hbm.at[pl.ds(pl.multiple_of(tile * n, n), n)],
                                 idx_smem.at[pl.ds(pl.multiple_of((tile % 2) * n, n), n)], isem.at[tile % 2])


def _dispatch_kernel(pad_start_ref, pad_len_ref, tail_ref, dest_hbm, xn_hbm, xs_hbm,
                     idx_smem, xbuf, zbuf, isem, lsem, ssem, zsem, *, tm, tr, n_tiles, n_real_tiles, n_blocks):
    i = pl.program_id(0)
    parts = xbuf.shape[1] // tm
    buf = i % 3

    def zero_fill(wait):
        def zero_rows(first_row, n_rows):
            cp = pltpu.make_async_copy(
                zbuf.at[pl.ds(0, n_rows * parts), :],
                xs_hbm.at[pl.ds(pl.multiple_of(first_row * parts, parts), n_rows * parts), :], zsem)
            if wait:
                cp.wait()
            else:
                cp.start()

        def per_expert(e, c):
            first, length = pad_start_ref[e], pad_len_ref[e]
            for bit in range((tr - 1).bit_length()):
                @pl.when((length >> bit) % 2 == 1)
                def _(bit=bit):
                    zero_rows(first + ((length >> (bit + 1)) << (bit + 1)), 1 << bit)
            return c

        def per_block(blk, c):
            zero_rows(blk * tr, tr)
            return c

        lax.fori_loop(0, N_EXPERTS, per_expert, 0)
        lax.fori_loop(tail_ref[0] // tr, n_blocks, per_block, 0)

    def idx_copy(tile):
        return _index_copy(dest_hbm, idx_smem, isem, tile, 2 * tm)

    def load(tile):
        rows = tm * parts
        return pltpu.make_async_copy(xn_hbm.at[pl.ds(pl.multiple_of(tile * rows, rows), rows), :],
                                     xbuf.at[tile % 3], lsem.at[tile % 3])

    def wait_scatters(tile):
        def wait_rows(n_tok):
            for k in range(2):
                pltpu.make_async_copy(xbuf.at[tile % 3, pl.ds(0, n_tok * parts), :],
                                      xs_hbm.at[pl.ds(0, n_tok * parts), :], ssem.at[tile % 3]).wait()

        @pl.when(tile < n_real_tiles)
        def _():
            wait_rows(tm)

        @pl.when(tile >= n_real_tiles)
        def _():
            wait_rows(N_META)

    def scatter(n_tok):
        def issue(g, c):
            for u in range(min(DMA_UNROLL, n_tok)):
                t = g * min(DMA_UNROLL, n_tok) + u
                src = xbuf.at[buf, pl.ds(pl.multiple_of(t * parts, parts), parts), :]
                for k in range(2):
                    d = idx_smem[(i % 2) * (2 * tm) + k * tm + t]
                    pltpu.make_async_copy(src, xs_hbm.at[pl.ds(pl.multiple_of(d * parts, parts), parts), :],
                                          ssem.at[buf]).start(priority=(2 * u + k) % DMA_THREADS)
            return c

        lax.fori_loop(0, n_tok // min(DMA_UNROLL, n_tok), issue, 0)

    @pl.when(i == 0)
    def _():
        idx_copy(0).start()
        load(0).start()
        zbuf[...] = jnp.zeros(zbuf.shape, zbuf.dtype)
        zero_fill(wait=False)

    @pl.when(i >= 2)
    def _():
        wait_scatters(i - 2)

    @pl.when(i + 1 < n_tiles)
    def _():
        idx_copy(i + 1).start()
        load(i + 1).start()

    idx_copy(i).wait()
    load(i).wait()

    @pl.when(i < n_real_tiles)
    def _():
        scatter(tm)

    @pl.when(i >= n_real_tiles)
    def _():
        scatter(N_META)

    @pl.when(i == n_tiles - 1)
    def _():
        if n_tiles >= 2:
            wait_scatters(i - 1)
        wait_scatters(i)
        zero_fill(wait=True)


def _expert_kernel(be_ref, bv_ref, nu_ref, xs_ref, *refs):
    nb = EXPERT_BLOCKS_PER_STEP
    w_refs, y_ref, wb_refs = refs[:3 * nb], refs[3 * nb], refs[3 * nb + 1:]
    i = pl.program_id(0)
    n_used = nu_ref[0]
    parts = wb_refs[0].shape[0] // (2 * LANES)
    tr = xs_ref.shape[0] // (nb * parts)

    for c in range(nb):
        blk = i * nb + c
        changed = be_ref[blk] != be_ref[jnp.maximum(blk - nb, 0)]

        @pl.when(jnp.logical_or(i == 0, jnp.logical_and(blk < n_used, changed)))
        def _(c=c):
            for w_ref, wb_ref in zip(w_refs[3 * c:3 * c + 3], wb_refs[3 * c:3 * c + 3]):
                wb_ref[...] = w_ref[0, 0].astype(_BF16)

    @pl.when(i * nb < n_used)
    def _():
        rows = lax.broadcasted_iota(jnp.int32, (tr, 1), 0)
        for c in range(nb):
            blk = i * nb + c
            wgb_ref, wub_ref, wdb_ref = wb_refs[3 * c:3 * c + 3]
            x = jnp.where(rows < bv_ref[blk], _unpack_rows(_load_token_tiles(xs_ref, c * tr, tr, parts)), 0.0)
            x = x.astype(_BF16)
            g = _dot(x, wgb_ref[...])
            u = _dot(x, wub_ref[...])
            a = g * (1.0 / (1.0 + jnp.exp(-g))) * u
            y = jnp.where(blk < n_used, _dot(a.astype(_BF16), wdb_ref[...]), 0.0)
            _store_token_tiles(y_ref.at[pl.ds(c * tr * parts, tr * parts), :], _pack_rows(y))

    @pl.when(i * nb >= n_used)
    def _():
        y_ref[...] = jnp.zeros(y_ref.shape, y_ref.dtype)


def _combined_tile(dest_hbm, rf_ref, h_ref, y_hbm, idx_smem, ybufs, isem, rsem, *, tm, n_tiles, slot):
    i = pl.program_id(0)
    parts = ybufs[0].shape[0] // (2 * tm)

    def idx_copy(tile):
        return _index_copy(dest_hbm, idx_smem, isem, tile, 2 * tm)

    def row_copy(idx_slot, buf_slot, t, k, u):
        d = idx_smem[idx_slot * (2 * tm) + k * tm + t]
        row = (k * tm + t) * parts
        if not isinstance(t, int):
            row = pl.multiple_of(row, parts)
        return pltpu.make_async_copy(
            y_hbm.at[pl.ds(pl.multiple_of(d * parts, parts), parts), :],
            ybufs[buf_slot].at[pl.ds(row, parts), :],
            rsem.at[buf_slot]).start(priority=(2 * u + k) % DMA_THREADS)

    def wait_rows(buf_slot):
        pltpu.make_async_copy(y_hbm.at[pl.ds(0, 2 * tm * parts), :], ybufs[buf_slot], rsem.at[buf_slot]).wait()

    if slot == 0:
        @pl.when(i == 0)
        def _():
            idx_copy(0).start()
            idx_copy(0).wait()

            def issue(g, c):
                for u in range(DMA_UNROLL):
                    for k in range(2):
                        row_copy(0, 0, g * DMA_UNROLL + u, k, u)
                return c

            lax.fori_loop(0, tm // DMA_UNROLL, issue, 0)
            if n_tiles >= 2:
                idx_copy(1).start()

    @pl.when(i + 1 < n_tiles)
    def _():
        idx_copy(i + 1).wait()

    @pl.when(i + 2 < n_tiles)
    def _():
        idx_copy(i + 2).start()

    nxt_idx_slot = jnp.minimum(i + 1, n_tiles - 1) % 2

    def prefetch(part, n_parts):
        for t in range(part * tm // n_parts, (part + 1) * tm // n_parts):
            for k in range(2):
                row_copy(nxt_idx_slot, 1 - slot, t, k, t)

    gates = rf_ref[0]
    gt = jnp.concatenate([gates, jnp.zeros((LANES - gates.shape[0], tm), _F32)], axis=0).T
    wait_rows(slot)
    yb = ybufs[slot]

    def combined(r0, n):
        return h_ref[r0:r0 + n, :] + (_unpack_rows(_load_token_tiles(yb, r0, n, parts)) * gt[r0:r0 + n, 0:1]
                                      + _unpack_rows(_load_token_tiles(yb, tm + r0, n, parts)) * gt[r0:r0 + n, 1:2])

    def drain():
        @pl.when(i == n_tiles - 1)
        def _():
            wait_rows(1 - slot)

    return combined, prefetch, drain


def _for_each_parity(body):
    for slot in range(2):
        @pl.when(pl.program_id(0) % 2 == slot)
        def _(slot=slot):
            body(slot)


def _combine_norm_kernel(dest_hbm, rf_ref, h_ref, y_hbm, fnorm_ref, out_ref,
                         idx_smem, ybuf0, ybuf1, isem, rsem, *, tm, n_tiles):
    def body(slot):
        combined, prefetch, drain = _combined_tile(dest_hbm, rf_ref, h_ref, y_hbm, idx_smem, (ybuf0, ybuf1),
                                                   isem, rsem, tm=tm, n_tiles=n_tiles, slot=slot)
        n_chunks = 4
        n = tm // n_chunks
        for c in range(n_chunks):
            prefetch(c, n_chunks)
            out_ref[c * n:(c + 1) * n, :] = _rms(combined(c * n, n), fnorm_ref[...])
        drain()

    _for_each_parity(body)


def _combine_proj_kernel(dest_hbm, rf_ref, h_ref, y_hbm, cos_ref, sin_ref, kvn_ref, wdkv_ref, kvlat_ref,
                         wuk_ref, wuv_ref, bnorm_ref, wdq_ref, qnorm_ref, wuq_ref,
                         h_out_ref, q_ref, k_ref, v_ref, idx_smem, ybuf0, ybuf1, isem, rsem, *, tm, n_tiles):
    def body(slot):
        combined, prefetch, drain = _combined_tile(dest_hbm, rf_ref, h_ref, y_hbm, idx_smem, (ybuf0, ybuf1),
                                                   isem, rsem, tm=tm, n_tiles=n_tiles, slot=slot)
        n = v_ref.shape[3]
        n_chunks = tm // n
        for c in range(n_chunks):
            prefetch(c, n_chunks)
            h = combined(c * n, n)
            h_out_ref[c * n:(c + 1) * n, :] = h
            _project(h, c, cos_ref, sin_ref, kvn_ref, wdkv_ref, kvlat_ref, wuk_ref, wuv_ref,
                     bnorm_ref, wdq_ref, qnorm_ref, wuq_ref, q_ref, k_ref, v_ref)
        drain()

    _for_each_parity(body)


def _rope128(x, cos_t, sin_t):
    lane = lax.broadcasted_iota(jnp.int32, (1, 128), 1)
    first_half = (lane % QK_ROPE_DIM) < (QK_ROPE_DIM // 2)
    swapped = jnp.where(first_half, pltpu.roll(x, 128 - QK_ROPE_DIM // 2, axis=1),
                        pltpu.roll(x, QK_ROPE_DIM // 2, axis=1))
    return x * cos_t + swapped * sin_t


def _project(h, chunk, cos_ref, sin_ref, kvn_ref, wdkv_ref, kvlat_ref, wuk_ref, wuv_ref,
             bnorm_ref, wdq_ref, qnorm_ref, wuq_ref, q_ref, k_ref, v_ref):
    n = h.shape[0]
    rows = slice(chunk * n, (chunk + 1) * n)
    cos_t = cos_ref[rows, :]
    sin_t = sin_ref[rows, :]
    c = _dot(_rms(h, kvn_ref[...]).astype(_BF16), wdkv_ref[...])
    ckv = _rms(c[:, :KV_LORA_RANK], kvlat_ref[...]).astype(_BF16)
    kr = _rope128(c[:, KV_LORA_RANK:KV_LORA_RANK + 128], cos_t, sin_t)[:, :QK_ROPE_DIM].astype(_BF16)
    kn = _dot(ckv, wuk_ref[...])
    vt = lax.dot_general(wuv_ref[...], ckv, _NT_DIMS, preferred_element_type=_F32)
    cq = _rms(_dot(_rms(h, bnorm_ref[...]).astype(_BF16), wdq_ref[...]), qnorm_ref[...])
    q = _dot((cq * Q_SCALE).astype(_BF16), wuq_ref[...])
    rope0 = N_HEADS * QK_NOPE_DIM
    for hd in range(N_HEADS):
        k_ref[hd, rows, 0:QK_NOPE_DIM] = kn[:, hd * QK_NOPE_DIM:(hd + 1) * QK_NOPE_DIM].astype(_BF16)
        k_ref[hd, rows, QK_NOPE_DIM:QK_DIM] = kr
        v_ref[hd, chunk] = vt[hd * V_HEAD_DIM:(hd + 1) * V_HEAD_DIM, :].astype(_BF16)
        q_ref[hd, rows, 0:QK_NOPE_DIM] = q[:, hd * QK_NOPE_DIM:(hd + 1) * QK_NOPE_DIM].astype(_BF16)
    for pair in range(N_HEADS // 2):
        qr = _rope128(q[:, rope0 + pair * LANES:rope0 + (pair + 1) * LANES], cos_t, sin_t)
        q_ref[2 * pair, rows, QK_NOPE_DIM:QK_DIM] = qr[:, :QK_ROPE_DIM].astype(_BF16)
        q_ref[2 * pair + 1, rows, QK_NOPE_DIM:QK_DIM] = qr[:, QK_ROPE_DIM:].astype(_BF16)


def _attn_kernel(q_ref, k_ref, vt_ref, km_ref, vmt_ref, o_ref, sa_ref, sb_ref, acc_ref, *, tq, tk):
    j = pl.program_id(2)
    heads = range(q_ref.shape[0])
    vd = acc_ref.shape[1]

    def scores(kb, s_ref):
        start = pl.multiple_of(kb * tk, tk)
        for hd in heads:
            s_ref[hd] = lax.dot_general(k_ref[hd, pl.ds(start, tk), :], q_ref[hd], _NT_DIMS,
                                        preferred_element_type=_F32)

    def update(kb, s_ref, carry, diag_block):
        out = []
        if diag_block is not None:
            visible = (diag_block * tk + lax.broadcasted_iota(jnp.int32, (tk, tq), 0)
                       <= lax.broadcasted_iota(jnp.int32, (tk, tq), 1))
        for hd in heads:
            m, l = carry[hd]
            s = s_ref[hd]
            if diag_block is not None:
                s = jnp.where(visible, s, NEG_INF)
            m_new = jnp.maximum(m, jnp.max(s, axis=0, keepdims=True))
            alpha = jnp.exp2(m - m_new)
            p = jnp.exp2(s - m_new)
            l = alpha * l + jnp.sum(p, axis=0, keepdims=True)
            acc_ref[hd] = acc_ref[hd] * alpha + _dot(vt_ref[hd, kb], p.astype(_BF16))
            out.append((m_new, l))
        return tuple(out)

    s0 = []
    for hd in heads:
        both = lax.dot_general(jnp.concatenate([k_ref[hd, 0:tk, :], km_ref[hd]], axis=0), q_ref[hd], _NT_DIMS,
                               preferred_element_type=_F32)
        sa_ref[hd] = both[0:tk]
        s0.append(both[tk:])
    carry, p0 = [], []
    for hd in heads:
        m = jnp.max(s0[hd], axis=0, keepdims=True)
        p = jnp.exp2(s0[hd] - m)
        carry.append((m, jnp.sum(p, axis=0, keepdims=True)))
        p0.append(p.astype(_BF16))
    for hd in heads:
        acc_ref[hd] = _dot(vmt_ref[hd, 0, :, 0:N_META], p0[hd])
    carry = tuple(carry)

    def pair(kp, carry):
        scores(2 * kp + 1, sb_ref)
        carry = update(2 * kp, sa_ref, carry, None)
        scores(2 * kp + 2, sa_ref)
        return update(2 * kp + 1, sb_ref, carry, None)

    carry = lax.fori_loop(0, j, pair, carry)
    scores(2 * j + 1, sb_ref)
    carry = update(2 * j, sa_ref, carry, 0)
    carry = update(2 * j + 1, sb_ref, carry, 1)
    for hd in heads:
        o_ref[:, hd * vd:(hd + 1) * vd] = (acc_ref[hd] / carry[hd][1]).T.astype(o_ref.dtype)


def _oproj_kernel(o_ref, h_ref, wo_ref, fnorm_ref, wrt_ref, br_ref, triu_ref,
                  h_out_ref, xn_ref, ri_ref, rf_ref, cnt_ref, base_ref, wsplit_ref):
    i = pl.program_id(0)

    @pl.when(i == 0)
    def _():
        _route_init(wrt_ref, wsplit_ref, base_ref)

    h = h_ref[...] + _dot(o_ref[...], wo_ref[...])
    h_out_ref[...] = h
    xn = _rms(h, fnorm_ref[...])
    _store_token_tiles(xn_ref, _pack_rows(xn))
    valid = lax.broadcasted_iota(jnp.int32, (1, h.shape[0]), 1) >= 0
    _route(xn, valid, wsplit_ref, br_ref, triu_ref, base_ref, ri_ref, rf_ref, cnt_ref)


def _full(shape):
    nd = len(shape)
    return pl.BlockSpec(shape, lambda *_: (0,) * nd)


def _router_operands(router_g, router_g_bias, router_e, router_e_bias):
    d = router_g.shape[0]
    wrt = jnp.concatenate([router_g.T, jnp.zeros((8 - N_EXPERT_GROUPS, d), _F32), router_e.T], axis=0)
    br = jnp.concatenate([router_g_bias, jnp.zeros((8 - N_EXPERT_GROUPS,), _F32), router_e_bias])[:, None]
    return wrt.astype(_F32), br.astype(_F32)


def _route_out(n_tiles, tm):
    shapes = [jax.ShapeDtypeStruct((n_tiles, 8, tm), jnp.int32),
              jax.ShapeDtypeStruct((n_tiles, 8, tm), _F32),
              jax.ShapeDtypeStruct((N_EXPERTS, 128), jnp.int32)]
    specs = [pl.BlockSpec((1, 8, tm), lambda i: (i, 0, 0)),
             pl.BlockSpec((1, 8, tm), lambda i: (i, 0, 0)),
             pl.BlockSpec((N_EXPERTS, 128), lambda i: (0, 0))]
    return shapes, specs


def _moe_experts(xn, ri, counts, w_gate, w_up, w_down, layer, n_tiles, n_real_tiles, d):
    tm, tr = TOKEN_TILE, EXPERT_ROWS
    parts = d // (2 * LANES)
    n_valid = n_real_tiles * tm + (n_tiles - n_real_tiles) * N_META
    n_blocks = -(-(2 * n_valid + N_EXPERTS * (tr - 1)) // tr)
    n_blocks = -(-n_blocks // EXPERT_BLOCKS_PER_STEP) * EXPERT_BLOCKS_PER_STEP
    n_rows = n_blocks * tr

    counts = counts[:, 0]
    padded = (counts + tr - 1) // tr * tr
    pends = jnp.cumsum(padded)
    pstarts = pends - padded
    n_used = (pends[-1] // tr).astype(jnp.int32).reshape(1)
    blk0 = jnp.arange(n_blocks, dtype=jnp.int32) * tr
    block_e = jnp.minimum(jnp.sum(blk0[:, None] >= pends[None, :], axis=1), N_EXPERTS - 1).astype(jnp.int32)
    experts = jnp.arange(N_EXPERTS, dtype=jnp.int32)
    block_oh = block_e[:, None] == experts[None, :]
    block_cnt = jnp.sum(jnp.where(block_oh, counts[None, :], 0), axis=1)
    block_start = jnp.sum(jnp.where(block_oh, pstarts[None, :], 0), axis=1)
    block_valid = jnp.clip(block_cnt - (blk0 - block_start), 0, tr).astype(jnp.int32)
    eid = ri[:, 0:2, :]
    slot0 = jnp.sum(jnp.where(eid[..., None] == experts, pstarts, 0), axis=-1)
    dest = (slot0 + ri[:, 2:4, :]).astype(jnp.int32).reshape(n_tiles * 2 * tm)

    pad_start = (pstarts + counts).astype(jnp.int32)
    pad_len = (padded - counts).astype(jnp.int32)
    xs = pl.pallas_call(
        functools.partial(_dispatch_kernel, tm=tm, tr=tr, n_tiles=n_tiles, n_real_tiles=n_real_tiles,
                          n_blocks=n_blocks),
        grid_spec=pltpu.PrefetchScalarGridSpec(
            num_scalar_prefetch=3,
            grid=(n_tiles,),
            in_specs=[pl.BlockSpec(memory_space=pl.ANY), pl.BlockSpec(memory_space=pl.ANY)],
            out_specs=pl.BlockSpec(memory_space=pl.ANY),
            scratch_shapes=[pltpu.SMEM((4 * tm,), jnp.int32), pltpu.VMEM((3, tm * parts, LANES), jnp.uint32),
                            pltpu.VMEM((tr * parts, LANES), jnp.uint32),
                            pltpu.SemaphoreType.DMA((2,)), pltpu.SemaphoreType.DMA((3,)),
                            pltpu.SemaphoreType.DMA((3,)), pltpu.SemaphoreType.DMA]),
        out_shape=jax.ShapeDtypeStruct((n_rows * parts, LANES), jnp.uint32),
        compiler_params=_params(1, has_side_effects=True, disable_bounds_checks=True),
        name="moe_dispatch",
    )(pad_start, pad_len, pends[-1:].astype(jnp.int32), dest, xn)

    f = w_gate.shape[3]
    nb = EXPERT_BLOCKS_PER_STEP
    step_rows = nb * tr * parts

    def expert_block(c):
        return lambda i, be, bv, nu: (layer, be[jnp.minimum(i * nb + c, nu[0] - 1)], 0, 0)

    weight_specs, weight_scratch = [], []
    for c in range(nb):
        weight_specs += [pl.BlockSpec((1, 1, d, f), expert_block(c)), pl.BlockSpec((1, 1, d, f), expert_block(c)),
                         pl.BlockSpec((1, 1, f, d), expert_block(c))]
        weight_scratch += [pltpu.VMEM((d, f), _BF16), pltpu.VMEM((d, f), _BF16), pltpu.VMEM((f, d), _BF16)]
    y = pl.pallas_call(
        _expert_kernel,
        grid_spec=pltpu.PrefetchScalarGridSpec(
            num_scalar_prefetch=3,
            grid=(n_blocks // nb,),
            in_specs=[pl.BlockSpec((step_rows, LANES),
                                   lambda i, be, bv, nu: (jnp.minimum(i, (nu[0] - 1) // nb), 0))] + weight_specs,
            out_specs=pl.BlockSpec((step_rows, LANES), lambda i, be, bv, nu: (i, 0)),
            scratch_shapes=weight_scratch),
        out_shape=jax.ShapeDtypeStruct((n_rows * parts, LANES), jnp.uint32),
        compiler_params=_params(1),
        name="moe_experts",
    )(block_e, block_valid, n_used, xs, *([w_gate, w_up, w_down] * nb))

    return dest, y


def _combine_operands(dest, rf, h, y, d):
    tm = TOKEN_TILE
    parts = d // (2 * LANES)
    in_specs = [pl.BlockSpec(memory_space=pl.ANY),
                pl.BlockSpec((1, 8, tm), lambda i: (i, 0, 0)),
                pl.BlockSpec((tm, d), lambda i: (i, 0)),
                pl.BlockSpec(memory_space=pl.ANY)]
    scratch = [pltpu.SMEM((4 * tm,), jnp.int32),
               pltpu.VMEM((2 * tm * parts, LANES), jnp.uint32), pltpu.VMEM((2 * tm * parts, LANES), jnp.uint32),
               pltpu.SemaphoreType.DMA((2,)), pltpu.SemaphoreType.DMA((2,))]
    return (dest, rf, h, y), in_specs, scratch


def kernel(x, meta_tokens, a_norm, a_w, a_scale, b_norm, b_w_dq, b_q_norm, b_w_uq, b_w_o, kv_norm, w_dkv,
           kv_lat_norm, w_uk, w_uv, ffn_norm, router_g, router_g_bias, router_e, router_e_bias, w_gate, w_up,
           w_down, final_norm):
    bsz, seq, d = x.shape
    tm = TOKEN_TILE
    assert seq % tm == 0 and seq % ATTN_Q_TILE == 0 and ATTN_Q_TILE == 2 * ATTN_K_TILE and tm % ATTN_K_TILE == 0
    kt = tm // ATTN_K_TILE
    assert d % (LANES * len(POOL_WINDOWS)) == 0 and N_META == max(POOL_WINDOWS) and N_META <= tm
    parts = d // (2 * LANES)
    n_tok = bsz * seq
    n_real_tiles = n_tok // tm
    tiles_per_batch = seq // tm
    n_tiles = n_real_tiles + 1
    gd = d // len(POOL_WINDOWS)
    row = lambda v: v.reshape(1, -1).astype(_F32)

    x2 = x.reshape(n_tok, d)
    meta_pad = jnp.concatenate([meta_tokens, jnp.zeros((tm - N_META, d), x.dtype)], axis=0)

    r = np.arange(tm)[:, None]
    cidx = np.arange(tm)[None, :]
    pb = min(POOL_BLOCK, tm)
    pm = np.stack([((r[:pb] - cidx[:, :pb] >= 0) & (r[:pb] - cidx[:, :pb] < w)) for w in POOL_WINDOWS]).astype(np.float32)
    hc = np.arange(N_META)[None, :]
    ph = np.stack([(r[:N_META] + N_META - hc < w) for w in POOL_WINDOWS]).astype(np.float32)
    triu = (r < cidx).astype(np.float32)
    pm, ph, triu = (jnp.asarray(a, dtype=_BF16) for a in (pm, ph, triu))

    wrt0, br0 = _router_operands(router_g[0], router_g_bias[0], router_e[0], router_e_bias[0])
    route_shapes, route_specs = _route_out(n_tiles, tm)
    tile_or_last = lambda i: (jnp.minimum(i, n_real_tiles - 1), 0)
    halo_blocks = tm // N_META
    h1, xn1, ri1, rf1, cnt1 = pl.pallas_call(
        functools.partial(_mixer_kernel, n_real_tiles=n_real_tiles, tiles_per_batch=tiles_per_batch),
        grid=(n_tiles,),
        in_specs=[pl.BlockSpec((tm, d), tile_or_last),
                  pl.BlockSpec((N_META, d), lambda i: (jnp.clip(i * halo_blocks - 1, 0, n_tok // N_META - 1), 0)),
                  _full((tm, d)), _full((N_META, d)), _full((1, d)),
                  _full((len(POOL_WINDOWS), gd, gd)), _full((1, d)), _full((1, d)),
                  _full((ROUTER_ROWS, d)), _full((ROUTER_ROWS, 1)),
                  _full(pm.shape), _full(ph.shape), _full((tm, tm))],
        out_specs=[pl.BlockSpec((tm, d), lambda i: (i, 0)),
                   pl.BlockSpec((tm * parts, LANES), lambda i: (i, 0))] + route_specs,
        out_shape=[jax.ShapeDtypeStruct((n_tiles * tm, d), _F32),
                   jax.ShapeDtypeStruct((n_tiles * tm * parts, LANES), jnp.uint32)] + route_shapes,
        scratch_shapes=[pltpu.VMEM((N_EXPERTS, 1), _F32), pltpu.VMEM((ROUTER_SPLIT_ROWS, d), _BF16)],
        compiler_params=_params(1),
        name="pool_mixer_router",
    )(x2, x2, meta_pad, meta_tokens, row(a_norm[0]), a_w[0].astype(_BF16), row(a_scale[0]), row(ffn_norm[0]),
      wrt0, br0, pm, ph, triu)

    dest1, y1 = _moe_experts(xn1, ri1, cnt1, w_gate, w_up, w_down, 0, n_tiles, n_real_tiles, d)

    pos = jnp.concatenate([jnp.arange(seq, dtype=_F32) + N_META, jnp.arange(tm, dtype=_F32)])
    inv_freq = ROPE_THETA ** (-jnp.arange(0, QK_ROPE_DIM, 2, dtype=_F32) / QK_ROPE_DIM)
    ang = pos[:, None] * inv_freq[None, :]
    cos_t = jnp.tile(jnp.cos(ang), (1, 4))
    sin_t = jnp.tile(jnp.concatenate([-jnp.sin(ang), jnp.sin(ang)], axis=1), (1, 2))

    wdkv = jnp.concatenate([w_dkv, w_dkv[:, KV_LORA_RANK:]], axis=1).astype(_BF16)
    wuk = w_uk.reshape(KV_LORA_RANK, N_HEADS * QK_NOPE_DIM).astype(_BF16)
    wuv = w_uv.reshape(KV_LORA_RANK, N_HEADS * V_HEAD_DIM).T.astype(_BF16)
    wuq = b_w_uq[0]
    q_rank = wuq.shape[0]
    wuq = jnp.concatenate([wuq[:, :, :QK_NOPE_DIM].reshape(q_rank, -1),
                           wuq[:, :, QK_NOPE_DIM:].reshape(q_rank, -1)], axis=1).astype(_BF16)
    wdq = b_w_dq[0].astype(_BF16)
    pos_tile = lambda i: (jnp.where(i < n_real_tiles, i % tiles_per_batch, tiles_per_batch), 0)
    head_tile = lambda i: (0, i, 0)
    operands, in_specs, scratch = _combine_operands(dest1, rf1, h1, y1, d)
    h2, q, k, v = pl.pallas_call(
        functools.partial(_combine_proj_kernel, tm=tm, n_tiles=n_tiles),
        grid=(n_tiles,),
        in_specs=in_specs + [
            pl.BlockSpec((tm, 128), pos_tile), pl.BlockSpec((tm, 128), pos_tile),
            _full((1, d)), _full(wdkv.shape), _full((1, KV_LORA_RANK)), _full(wuk.shape), _full(wuv.shape),
            _full((1, d)), _full(wdq.shape), _full((1, q_rank)), _full(wuq.shape)],
        out_specs=[pl.BlockSpec((tm, d), lambda i: (i, 0)),
                   pl.BlockSpec((N_HEADS, tm, QK_DIM), head_tile),
                   pl.BlockSpec((N_HEADS, tm, QK_DIM), head_tile),
                   pl.BlockSpec((N_HEADS, kt, V_HEAD_DIM, ATTN_K_TILE), lambda i: (0, i, 0, 0))],
        out_shape=[jax.ShapeDtypeStruct((n_tiles * tm, d), _F32),
                   jax.ShapeDtypeStruct((N_HEADS, n_tiles * tm, QK_DIM), _BF16),
                   jax.ShapeDtypeStruct((N_HEADS, n_tiles * tm, QK_DIM), _BF16),
                   jax.ShapeDtypeStruct((N_HEADS, n_tiles * kt, V_HEAD_DIM, ATTN_K_TILE), _BF16)],
        scratch_shapes=scratch,
        compiler_params=_params(1, disable_bounds_checks=True),
        name="moe_combine_latent_qkv",
    )(*operands, cos_t, sin_t, row(kv_norm), wdkv, row(kv_lat_norm), wuk, wuv,
      row(b_norm[0]), wdq, row(b_q_norm[0]), wuq)

    tq, tk, hp = ATTN_Q_TILE, ATTN_K_TILE, ATTN_HEADS_PER_STEP
    n_q = seq // tq
    meta_block = n_tok // N_META
    o = pl.pallas_call(
        functools.partial(_attn_kernel, tq=tq, tk=tk),
        grid=(bsz, N_HEADS // hp, n_q),
        in_specs=[pl.BlockSpec((hp, tq, QK_DIM), lambda b, hg, j: (hg, b * n_q + j, 0)),
                  pl.BlockSpec((hp, seq, QK_DIM), lambda b, hg, j: (hg, b, 0)),
                  pl.BlockSpec((hp, seq // tk, V_HEAD_DIM, tk), lambda b, hg, j: (hg, b, 0, 0)),
                  pl.BlockSpec((hp, N_META, QK_DIM), lambda b, hg, j: (hg, meta_block, 0)),
                  pl.BlockSpec((hp, 1, V_HEAD_DIM, tk), lambda b, hg, j: (hg, n_real_tiles * kt, 0, 0))],
        out_specs=pl.BlockSpec((tq, hp * V_HEAD_DIM), lambda b, hg, j: (b * n_q + j, hg)),
        out_shape=jax.ShapeDtypeStruct((n_tok, N_HEADS * V_HEAD_DIM), _BF16),
        scratch_shapes=[pltpu.VMEM((hp, tk, tq), _F32), pltpu.VMEM((hp, tk, tq), _F32),
                        pltpu.VMEM((hp, V_HEAD_DIM, tq), _F32)],
        compiler_params=_params(3),
        name="causal_attention",
    )(q, k, v, k, v)

    wrt1, br1 = _router_operands(router_g[1], router_g_bias[1], router_e[1], router_e_bias[1])
    route_shapes, route_specs = _route_out(n_real_tiles, tm)
    h3, xn2, ri2, rf2, cnt2 = pl.pallas_call(
        _oproj_kernel,
        grid=(n_real_tiles,),
        in_specs=[pl.BlockSpec((tm, N_HEADS * V_HEAD_DIM), lambda i: (i, 0)),
                  pl.BlockSpec((tm, d), lambda i: (i, 0)),
                  _full((N_HEADS * V_HEAD_DIM, d)), _full((1, d)),
                  _full((ROUTER_ROWS, d)), _full((ROUTER_ROWS, 1)), _full((tm, tm))],
        out_specs=[pl.BlockSpec((tm, d), lambda i: (i, 0)),
                   pl.BlockSpec((tm * parts, LANES), lambda i: (i, 0))] + route_specs,
        out_shape=[jax.ShapeDtypeStruct((n_tok, d), _F32),
                   jax.ShapeDtypeStruct((n_tok * parts, LANES), jnp.uint32)] + route_shapes,
        scratch_shapes=[pltpu.VMEM((N_EXPERTS, 1), _F32), pltpu.VMEM((ROUTER_SPLIT_ROWS, d), _BF16)],
        compiler_params=_params(1),
        name="attn_out_router",
    )(o, h2, b_w_o[0].astype(_BF16), row(ffn_norm[1]), wrt1, br1, triu)

    dest2, y2 = _moe_experts(xn2, ri2, cnt2, w_gate, w_up, w_down, 1, n_real_tiles, n_real_tiles, d)
    operands, in_specs, scratch = _combine_operands(dest2, rf2, h3, y2, d)
    out = pl.pallas_call(
        functools.partial(_combine_norm_kernel, tm=tm, n_tiles=n_real_tiles),
        grid=(n_real_tiles,),
        in_specs=in_specs + [_full((1, d))],
        out_specs=pl.BlockSpec((tm, d), lambda i: (i, 0)),
        out_shape=jax.ShapeDtypeStruct((n_tok, d), _F32),
        scratch_shapes=scratch,
        compiler_params=_params(1, disable_bounds_checks=True),
        name="moe_combine_final_norm",
    )(*operands, row(final_norm))
    return out.reshape(bsz, seq, d)
```

```python
import functools

import numpy as np
import jax
import jax.numpy as jnp
from jax import lax
from jax.experimental import pallas as pl
from jax.experimental.pallas import tpu as pltpu

N_META = 16
POOL_WINDOWS = (2, 4, 8, 16)
N_HEADS = 8
QK_NOPE_DIM = 128
QK_ROPE_DIM = 64
QK_DIM = QK_NOPE_DIM + QK_ROPE_DIM
V_HEAD_DIM = 128
KV_LORA_RANK = 256
ROPE_THETA = 10000.0
ATTN_SCALE = QK_DIM ** -0.5
Q_SCALE = ATTN_SCALE * 1.4426950408889634
N_EXPERT_GROUPS = 4
EXPERTS_PER_GROUP = 8
N_EXPERTS = N_EXPERT_GROUPS * EXPERTS_PER_GROUP
RMS_EPS = 1e-6
NEG_INF = -1e30

TOKEN_TILE = 512
POOL_BLOCK = 256
EXPERT_ROWS = 256
EXPERT_BLOCKS_PER_STEP = 2
ATTN_Q_TILE = 512
ATTN_K_TILE = ATTN_Q_TILE // 2
ATTN_HEADS_PER_STEP = 4
DMA_UNROLL = 8
DMA_THREADS = 2
ROUTER_ROWS = 8 + N_EXPERTS
ROUTER_LO_ROW = 48
ROUTER_SPLIT_ROWS = 2 * ROUTER_LO_ROW
VMEM_LIMIT_BYTES = 48 * 1024 * 1024

LANES = 128
_F32 = jnp.float32
_BF16 = jnp.bfloat16
_NT_DIMS = (((1,), (1,)), ((), ()))


def _params(n_grid_dims=1, **kw):
    return pltpu.CompilerParams(dimension_semantics=("arbitrary",) * n_grid_dims,
                                vmem_limit_bytes=VMEM_LIMIT_BYTES, **kw)


def _rms(x, g):
    ms = jnp.mean(x * x, axis=-1, keepdims=True)
    return x * lax.rsqrt(ms + RMS_EPS) * g


def _split_bf16(x):
    hi = x.astype(_BF16)
    lo = (x - hi.astype(_F32)).astype(_BF16)
    return hi, lo


def _dot(a, b):
    return jnp.dot(a, b, preferred_element_type=_F32)


def _store_token_tiles(ref, x):
    n, d = x.shape
    parts = d // LANES
    for s in range(parts):
        ref[pl.ds(s, n, stride=parts), :] = x[:, s * LANES:(s + 1) * LANES]


def _load_token_tiles(ref, first_token, n, parts):
    return jnp.concatenate([ref[pl.ds(first_token * parts + s, n, stride=parts), :] for s in range(parts)], axis=1)


_HIGH_HALF = 0xFFFF0000


def _pack_rows(x):
    n, d = x.shape
    words = []
    for s in range(d // (2 * LANES)):
        lo = x[:, 2 * s * LANES:(2 * s + 1) * LANES].astype(_BF16).astype(_F32)
        hi = x[:, (2 * s + 1) * LANES:(2 * s + 2) * LANES].astype(_BF16).astype(_F32)
        words.append(lax.shift_right_logical(lax.bitcast_convert_type(lo, jnp.uint32), jnp.uint32(16))
                     | (lax.bitcast_convert_type(hi, jnp.uint32) & jnp.uint32(_HIGH_HALF)))
    return jnp.concatenate(words, axis=1)


def _unpack_rows(w):
    cols = []
    for s in range(w.shape[1] // LANES):
        ws = w[:, s * LANES:(s + 1) * LANES]
        cols.append(lax.bitcast_convert_type(lax.shift_left(ws, jnp.uint32(16)), _F32))
        cols.append(lax.bitcast_convert_type(ws & jnp.uint32(_HIGH_HALF), _F32))
    return jnp.concatenate(cols, axis=1)


def _route_init(wrt_ref, wsplit_ref, base_ref):
    base_ref[...] = jnp.zeros(base_ref.shape, base_ref.dtype)
    w_hi, w_lo = _split_bf16(wrt_ref[...])
    wsplit_ref[...] = jnp.zeros(wsplit_ref.shape, wsplit_ref.dtype)
    wsplit_ref[0:ROUTER_ROWS, :] = w_hi
    wsplit_ref[ROUTER_LO_ROW:ROUTER_LO_ROW + ROUTER_ROWS, :] = w_lo


def _route(xn, valid, wsplit_ref, br_ref, triu_ref, base_ref, ri_ref, rf_ref, cnt_ref):
    tm = xn.shape[0]
    x_hi, x_lo = _split_bf16(xn)
    by_hi = lax.dot_general(wsplit_ref[...], x_hi, _NT_DIMS, preferred_element_type=_F32)
    by_lo = lax.dot_general(wsplit_ref[0:ROUTER_LO_ROW, :], x_lo, _NT_DIMS, preferred_element_type=_F32)
    logits = (by_hi[0:ROUTER_ROWS] + by_hi[ROUTER_LO_ROW:ROUTER_LO_ROW + ROUTER_ROWS]
              + by_lo[0:ROUTER_ROWS] + br_ref[...])
    lg = logits[0:N_EXPERT_GROUPS]
    eg = jnp.exp(lg - jnp.max(lg, axis=0, keepdims=True))
    pg = eg / jnp.sum(eg, axis=0, keepdims=True)
    w_g = jnp.max(pg, axis=0, keepdims=True)
    ig = lax.broadcasted_iota(jnp.int32, pg.shape, 0).astype(_F32)
    g_sel = jnp.min(jnp.where(pg == w_g, ig, float(N_EXPERT_GROUPS)), axis=0, keepdims=True)

    sel = logits[8:8 + EXPERTS_PER_GROUP]
    for g in range(1, N_EXPERT_GROUPS):
        sel = jnp.where(g_sel == float(g), logits[8 + g * EXPERTS_PER_GROUP:8 + (g + 1) * EXPERTS_PER_GROUP], sel)
    ie = lax.broadcasted_iota(jnp.int32, sel.shape, 0).astype(_F32)
    v1 = jnp.max(sel, axis=0, keepdims=True)
    i1 = jnp.min(jnp.where(sel == v1, ie, float(EXPERTS_PER_GROUP)), axis=0, keepdims=True)
    rest = jnp.where(ie == i1, -jnp.inf, sel)
    v2 = jnp.max(rest, axis=0, keepdims=True)
    i2 = jnp.min(jnp.where(rest == v2, ie, float(EXPERTS_PER_GROUP)), axis=0, keepdims=True)
    e2 = jnp.exp(v2 - v1)
    den = 1.0 + e2
    validf = valid.astype(_F32)
    gate0 = w_g * (1.0 / den) * validf
    gate1 = w_g * (e2 / den) * validf
    eid0 = g_sel * float(EXPERTS_PER_GROUP) + i1
    eid1 = g_sel * float(EXPERTS_PER_GROUP) + i2

    iall = lax.broadcasted_iota(jnp.int32, (N_EXPERTS, tm), 0).astype(_F32)
    oh0 = jnp.where(iall == eid0, validf, 0.0)
    oh1 = jnp.where(iall == eid1, validf, 0.0)
    both = oh0 + oh1
    before = _dot(both.astype(_BF16), triu_ref[...]) + base_ref[...]
    rank0 = jnp.sum(oh0 * before, axis=0, keepdims=True)
    rank1 = jnp.sum(oh1 * before, axis=0, keepdims=True)
    base_ref[...] = base_ref[...] + jnp.sum(both, axis=1, keepdims=True)

    ri_ref[...] = jnp.zeros(ri_ref.shape, ri_ref.dtype)
    rf_ref[...] = jnp.zeros(rf_ref.shape, rf_ref.dtype)
    ri_ref[0, 0:1, :] = eid0.astype(jnp.int32)
    ri_ref[0, 1:2, :] = eid1.astype(jnp.int32)
    ri_ref[0, 2:3, :] = rank0.astype(jnp.int32)
    ri_ref[0, 3:4, :] = rank1.astype(jnp.int32)
    rf_ref[0, 0:1, :] = gate0
    rf_ref[0, 1:2, :] = gate1
    cnt_ref[...] = jnp.broadcast_to(base_ref[...], cnt_ref.shape).astype(jnp.int32)


def _mixer_kernel(x_ref, xh_ref, mp_ref, meta_ref, anorm_ref, aw_ref, ascale_ref, fnorm_ref,
                  wrt_ref, br_ref, pm_ref, ph_ref, triu_ref,
                  h_ref, xn_ref, ri_ref, rf_ref, cnt_ref, base_ref, wsplit_ref, *, n_real_tiles, tiles_per_batch):
    i = pl.program_id(0)
    tm = x_ref.shape[0]
    gd = x_ref.shape[1] // len(POOL_WINDOWS)
    is_meta = i == n_real_tiles
    first = (i % tiles_per_batch) == 0

    @pl.when(i == 0)
    def _():
        _route_init(wrt_ref, wsplit_ref, base_ref)

    h = jnp.where(is_meta, mp_ref[...], x_ref[...])
    halo = jnp.where(is_meta, 0.0, jnp.where(first, meta_ref[...], xh_ref[...]))
    hn = _rms(h, anorm_ref[...])
    hh = _rms(halo, anorm_ref[...])
    hn_hi, hn_lo = _split_bf16(hn)
    hh_hi, hh_lo = _split_bf16(hh)
    pb = pm_ref.shape[1]
    for c in range(tm // pb):
        rows = slice(c * pb, (c + 1) * pb)
        row = c * pb + lax.broadcasted_iota(jnp.int32, (pb, 1), 0)
        for g, w in enumerate(POOL_WINDOWS):
            sl = slice(g * gd, (g + 1) * gd)
            win = _dot(pm_ref[g], hn_hi[rows, sl]) + _dot(pm_ref[g], hn_lo[rows, sl])
            if c == 0:
                before_hi, before_lo = hh_hi[:, sl], hh_lo[:, sl]
            else:
                before_hi, before_lo = hn_hi[c * pb - N_META:c * pb, sl], hn_lo[c * pb - N_META:c * pb, sl]
            top = win[:N_META] + _dot(ph_ref[g], before_hi) + _dot(ph_ref[g], before_lo)
            win = jnp.concatenate([top, win[N_META:]], axis=0)
            cnt = jnp.where(is_meta, jnp.minimum(row + 1, w), w).astype(_F32)
            pooled = win * (1.0 / cnt) - hn[rows, sl]
            mix = _dot(pooled.astype(_BF16), aw_ref[g])
            h_ref[rows, sl] = h[rows, sl] + mix * ascale_ref[:, sl]

    xn = _rms(h_ref[...], fnorm_ref[...])
    _store_token_tiles(xn_ref, _pack_rows(xn))
    lane = lax.broadcasted_iota(jnp.int32, (1, tm), 1)
    valid = jnp.logical_or(jnp.logical_not(is_meta), lane < N_META)
    _route(xn, valid, wsplit_ref, br_ref, triu_ref, base_ref, ri_ref, rf_ref, cnt_ref)


def _index_copy(dest_hbm, idx_smem, isem, tile, n):
    return pltpu.make_async_copy(dest_hbm.at[pl.ds(pl.multiple_of(tile * n, n), n)],
                                 idx_smem.at[pl.ds(pl.multiple_of((tile % 2) * n, n), n)], isem.at[tile % 2])


def _dispatch_kernel(pad_start_ref, pad_len_ref, tail_ref, dest_hbm, xn_hbm, xs_hbm,
                     idx_smem, xbuf, zbuf, isem, lsem, ssem, zsem, *, tm, tr, n_tiles, n_real_tiles, n_blocks):
    i = pl.program_id(0)
    parts = xbuf.shape[1] // tm
    buf = i % 3

    def zero_fill(wait):
        def zero_rows(first_row, n_rows):
            cp = pltpu.make_async_copy(
                zbuf.at[pl.ds(0, n_rows * parts), :],
                xs_hbm.at[pl.ds(pl.multiple_of(first_row * parts, parts), n_rows * parts), :], zsem)
            if wait:
                cp.wait()
            else:
                cp.start()

        def per_expert(e, c):
            first, length = pad_start_ref[e], pad_len_ref[e]
            for bit in range((tr - 1).bit_length()):
                @pl.when((length >> bit) % 2 == 1)
                def _(bit=bit):
                    zero_rows(first + ((length >> (bit + 1)) << (bit + 1)), 1 << bit)
            return c

        def per_block(blk, c):
            zero_rows(blk * tr, tr)
            return c

        lax.fori_loop(0, N_EXPERTS, per_expert, 0)
        lax.fori_loop(tail_ref[0] // tr, n_blocks, per_block, 0)

    def idx_copy(tile):
        return _index_copy(dest_hbm, idx_smem, isem, tile, 2 * tm)

    def load(tile):
        rows = tm * parts
        return pltpu.make_async_copy(xn_hbm.at[pl.ds(pl.multiple_of(tile * rows, rows), rows), :],
                                     xbuf.at[tile % 3], lsem.at[tile % 3])

    def wait_scatters(tile):
        def wait_rows(n_tok):
            for k in range(2):
                pltpu.make_async_copy(xbuf.at[tile % 3, pl.ds(0, n_tok * parts), :],
                                      xs_hbm.at[pl.ds(0, n_tok * parts), :], ssem.at[tile % 3]).wait()

        @pl.when(tile < n_real_tiles)
        def _():
            wait_rows(tm)

        @pl.when(tile >= n_real_tiles)
        def _():
            wait_rows(N_META)

    def scatter(n_tok):
        def issue(g, c):
            for u in range(min(DMA_UNROLL, n_tok)):
                t = g * min(DMA_UNROLL, n_tok) + u
                src = xbuf.at[buf, pl.ds(pl.multiple_of(t * parts, parts), parts), :]
                for k in range(2):
                    d = idx_smem[(i % 2) * (2 * tm) + k * tm + t]
                    pltpu.make_async_copy(src, xs_hbm.at[pl.ds(pl.multiple_of(d * parts, parts), parts), :],
                                          ssem.at[buf]).start(priority=(2 * u + k) % DMA_THREADS)
            return c

        lax.fori_loop(0, n_tok // min(DMA_UNROLL, n_tok), issue, 0)

    @pl.when(i == 0)
    def _():
        idx_copy(0).start()
        load(0).start()
        zbuf[...] = jnp.zeros(zbuf.shape, zbuf.dtype)
        zero_fill(wait=False)

    @pl.when(i >= 2)
    def _():
        wait_scatters(i - 2)

    @pl.when(i + 1 < n_tiles)
    def _():
        idx_copy(i + 1).start()
        load(i + 1).start()

    idx_copy(i).wait()
    load(i).wait()

    @pl.when(i < n_real_tiles)
    def _():
        scatter(tm)

    @pl.when(i >= n_real_tiles)
    def _():
        scatter(N_META)

    @pl.when(i == n_tiles - 1)
    def _():
        if n_tiles >= 2:
            wait_scatters(i - 1)
        wait_scatters(i)
        zero_fill(wait=True)


def _expert_kernel(be_ref, bv_ref, nu_ref, xs_ref, *refs):
    nb = EXPERT_BLOCKS_PER_STEP
    w_refs, y_ref, wb_refs = refs[:3 * nb], refs[3 * nb], refs[3 * nb + 1:]
    i = pl.program_id(0)
    n_used = nu_ref[0]
    parts = wb_refs[0].shape[0] // (2 * LANES)
    tr = xs_ref.shape[0] // (nb * parts)

    for c in range(nb):
        blk = i * nb + c
        changed = be_ref[blk] != be_ref[jnp.maximum(blk - nb, 0)]

        @pl.when(jnp.logical_or(i == 0, jnp.logical_and(blk < n_used, changed)))
        def _(c=c):
            for w_ref, wb_ref in zip(w_refs[3 * c:3 * c + 3], wb_refs[3 * c:3 * c + 3]):
                wb_ref[...] = w_ref[0, 0].astype(_BF16)

    @pl.when(i * nb < n_used)
    def _():
        rows = lax.broadcasted_iota(jnp.int32, (tr, 1), 0)
        xs, gs, us, ys = [], [], [], []
        for c in range(nb):
            x = jnp.where(rows < bv_ref[i * nb + c], _unpack_rows(_load_token_tiles(xs_ref, c * tr, tr, parts)), 0.0)
            xs.append(x.astype(_BF16))
        for c in range(nb):
            gs.append(_dot(xs[c], wb_refs[3 * c][...]))
            us.append(_dot(xs[c], wb_refs[3 * c + 1][...]))
        for c in range(nb):
            a = gs[c] * (1.0 / (1.0 + jnp.exp(-gs[c]))) * us[c]
            ys.append(_dot(a.astype(_BF16), wb_refs[3 * c + 2][...]))
        for c in range(nb):
            y = jnp.where(i * nb + c < n_used, ys[c], 0.0)
            _store_token_tiles(y_ref.at[pl.ds(c * tr * parts, tr * parts), :], _pack_rows(y))

    @pl.when(i * nb >= n_used)
    def _():
        y_ref[...] = jnp.zeros(y_ref.shape, y_ref.dtype)


def _combined_tile(dest_hbm, rf_ref, h_ref, y_hbm, idx_smem, ybufs, isem, rsem, *, tm, n_tiles, slot):
    i = pl.program_id(0)
    parts = ybufs[0].shape[0] // (2 * tm)

    def idx_copy(tile):
        return _index_copy(dest_hbm, idx_smem, isem, tile, 2 * tm)

    def row_copy(idx_slot, buf_slot, t, k, u):
        d = idx_smem[idx_slot * (2 * tm) + k * tm + t]
        row = (k * tm + t) * parts
        if not isinstance(t, int):
            row = pl.multiple_of(row, parts)
        return pltpu.make_async_copy(
            y_hbm.at[pl.ds(pl.multiple_of(d * parts, parts), parts), :],
            ybufs[buf_slot].at[pl.ds(row, parts), :],
            rsem.at[buf_slot]).start(priority=(2 * u + k) % DMA_THREADS)

    def wait_rows(buf_slot):
        pltpu.make_async_copy(y_hbm.at[pl.ds(0, 2 * tm * parts), :], ybufs[buf_slot], rsem.at[buf_slot]).wait()

    if slot == 0:
        @pl.when(i == 0)
        def _():
            idx_copy(0).start()
            idx_copy(0).wait()

            def issue(g, c):
                for u in range(DMA_UNROLL):
                    for k in range(2):
                        row_copy(0, 0, g * DMA_UNROLL + u, k, u)
                return c

            lax.fori_loop(0, tm // DMA_UNROLL, issue, 0)
            if n_tiles >= 2:
                idx_copy(1).start()

    @pl.when(i + 1 < n_tiles)
    def _():
        idx_copy(i + 1).wait()

    @pl.when(i + 2 < n_tiles)
    def _():
        idx_copy(i + 2).start()

    nxt_idx_slot = jnp.minimum(i + 1, n_tiles - 1) % 2

    def prefetch(part, n_parts):
        for t in range(part * tm // n_parts, (part + 1) * tm // n_parts):
            for k in range(2):
                row_copy(nxt_idx_slot, 1 - slot, t, k, t)

    gates = rf_ref[0]
    gt = jnp.concatenate([gates, jnp.zeros((LANES - gates.shape[0], tm), _F32)], axis=0).T
    wait_rows(slot)
    yb = ybufs[slot]

    def combined(r0, n):
        return h_ref[r0:r0 + n, :] + (_unpack_rows(_load_token_tiles(yb, r0, n, parts)) * gt[r0:r0 + n, 0:1]
                                      + _unpack_rows(_load_token_tiles(yb, tm + r0, n, parts)) * gt[r0:r0 + n, 1:2])

    def drain():
        @pl.when(i == n_tiles - 1)
        def _():
            wait_rows(1 - slot)

    return combined, prefetch, drain


def _for_each_parity(body):
    for slot in range(2):
        @pl.when(pl.program_id(0) % 2 == slot)
        def _(slot=slot):
            body(slot)


def _combine_norm_kernel(dest_hbm, rf_ref, h_ref, y_hbm, fnorm_ref, out_ref,
                         idx_smem, ybuf0, ybuf1, isem, rsem, *, tm, n_tiles):
    def body(slot):
        combined, prefetch, drain = _combined_tile(dest_hbm, rf_ref, h_ref, y_hbm, idx_smem, (ybuf0, ybuf1),
                                                   isem, rsem, tm=tm, n_tiles=n_tiles, slot=slot)
        n_chunks = 4
        n = tm // n_chunks
        for c in range(n_chunks):
            prefetch(c, n_chunks)
            out_ref[c * n:(c + 1) * n, :] = _rms(combined(c * n, n), fnorm_ref[...])
        drain()

    _for_each_parity(body)


def _combine_proj_kernel(dest_hbm, rf_ref, h_ref, y_hbm, cos_ref, sin_ref, kvn_ref, wdkv_ref, kvlat_ref,
                         wuk_ref, wuv_ref, bnorm_ref, wdq_ref, qnorm_ref, wuq_ref,
                         h_out_ref, q_ref, k_ref, v_ref, idx_smem, ybuf0, ybuf1, isem, rsem, *, tm, n_tiles):
    def body(slot):
        combined, prefetch, drain = _combined_tile(dest_hbm, rf_ref, h_ref, y_hbm, idx_smem, (ybuf0, ybuf1),
                                                   isem, rsem, tm=tm, n_tiles=n_tiles, slot=slot)
        n = v_ref.shape[3]
        n_chunks = tm // n
        hs = []
        for c in range(n_chunks):
            prefetch(c, n_chunks)
            hs.append(combined(c * n, n))
            h_out_ref[c * n:(c + 1) * n, :] = hs[c]
        _project(hs, cos_ref, sin_ref, kvn_ref, wdkv_ref, kvlat_ref, wuk_ref, wuv_ref,
                 bnorm_ref, wdq_ref, qnorm_ref, wuq_ref, q_ref, k_ref, v_ref)
        drain()

    _for_each_parity(body)


def _rope128(x, cos_t, sin_t):
    lane = lax.broadcasted_iota(jnp.int32, (1, 128), 1)
    first_half = (lane % QK_ROPE_DIM) < (QK_ROPE_DIM // 2)
    swapped = jnp.where(first_half, pltpu.roll(x, 128 - QK_ROPE_DIM // 2, axis=1),
                        pltpu.roll(x, QK_ROPE_DIM // 2, axis=1))
    return x * cos_t + swapped * sin_t


def _project(hs, cos_ref, sin_ref, kvn_ref, wdkv_ref, kvlat_ref, wuk_ref, wuv_ref,
             bnorm_ref, wdq_ref, qnorm_ref, wuq_ref, q_ref, k_ref, v_ref):
    n = hs[0].shape[0]
    chunks = range(len(hs))
    c_kv = [_dot(_rms(h, kvn_ref[...]).astype(_BF16), wdkv_ref[...]) for h in hs]
    c_q = [_dot(_rms(h, bnorm_ref[...]).astype(_BF16), wdq_ref[...]) for h in hs]
    ckv = [_rms(c[:, :KV_LORA_RANK], kvlat_ref[...]).astype(_BF16) for c in c_kv]
    cq = [(_rms(c, qnorm_ref[...]) * Q_SCALE).astype(_BF16) for c in c_q]
    kn = [_dot(ckv[c], wuk_ref[...]) for c in chunks]
    vt = [lax.dot_general(wuv_ref[...], ckv[c], _NT_DIMS, preferred_element_type=_F32) for c in chunks]
    q = [_dot(cq[c], wuq_ref[...]) for c in chunks]
    rope0 = N_HEADS * QK_NOPE_DIM
    for c in chunks:
        rows = slice(c * n, (c + 1) * n)
        cos_t = cos_ref[rows, :]
        sin_t = sin_ref[rows, :]
        kr = _rope128(c_kv[c][:, KV_LORA_RANK:KV_LORA_RANK + 128], cos_t, sin_t)[:, :QK_ROPE_DIM].astype(_BF16)
        for hd in range(N_HEADS):
            k_ref[hd, rows, 0:QK_NOPE_DIM] = kn[c][:, hd * QK_NOPE_DIM:(hd + 1) * QK_NOPE_DIM].astype(_BF16)
            k_ref[hd, rows, QK_NOPE_DIM:QK_DIM] = kr
            v_ref[hd, c] = vt[c][hd * V_HEAD_DIM:(hd + 1) * V_HEAD_DIM, :].astype(_BF16)
            q_ref[hd, rows, 0:QK_NOPE_DIM] = q[c][:, hd * QK_NOPE_DIM:(hd + 1) * QK_NOPE_DIM].astype(_BF16)
        for pair in range(N_HEADS // 2):
            qr = _rope128(q[c][:, rope0 + pair * LANES:rope0 + (pair + 1) * LANES], cos_t, sin_t)
            q_ref[2 * pair, rows, QK_NOPE_DIM:QK_DIM] = qr[:, :QK_ROPE_DIM].astype(_BF16)
            q_ref[2 * pair + 1, rows, QK_NOPE_DIM:QK_DIM] = qr[:, QK_ROPE_DIM:].astype(_BF16)


def _attn_kernel(q_ref, k_ref, vt_ref, km_ref, vmt_ref, o_ref, sa_ref, sb_ref, acc_ref, *, tq, tk):
    j = pl.program_id(2)
    heads = range(q_ref.shape[0])
    vd = acc_ref.shape[1]

    def scores(kb, s_ref):
        start = pl.multiple_of(kb * tk, tk)
        for hd in heads:
            s_ref[hd] = lax.dot_general(k_ref[hd, pl.ds(start, tk), :], q_ref[hd], _NT_DIMS,
                                        preferred_element_type=_F32)

    def update(kb, s_ref, carry, diag_block):
        out = []
        if diag_block is not None:
            visible = (diag_block * tk + lax.broadcasted_iota(jnp.int32, (tk, tq), 0)
                       <= lax.broadcasted_iota(jnp.int32, (tk, tq), 1))
        for hd in heads:
            m, l = carry[hd]
            s = s_ref[hd]
            if diag_block is not None:
                s = jnp.where(visible, s, NEG_INF)
            m_new = jnp.maximum(m, jnp.max(s, axis=0, keepdims=True))
            alpha = jnp.exp2(m - m_new)
            p = jnp.exp2(s - m_new)
            l = alpha * l + jnp.sum(p, axis=0, keepdims=True)
            acc_ref[hd] = acc_ref[hd] * alpha + _dot(vt_ref[hd, kb], p.astype(_BF16))
            out.append((m_new, l))
        return tuple(out)

    s0 = []
    for hd in heads:
        both = lax.dot_general(jnp.concatenate([k_ref[hd, 0:tk, :], km_ref[hd]], axis=0), q_ref[hd], _NT_DIMS,
                               preferred_element_type=_F32)
        sa_ref[hd] = both[0:tk]
        s0.append(both[tk:])
    carry, p0 = [], []
    for hd in heads:
        m = jnp.max(s0[hd], axis=0, keepdims=True)
        p = jnp.exp2(s0[hd] - m)
        carry.append((m, jnp.sum(p, axis=0, keepdims=True)))
        p0.append(p.astype(_BF16))
    for hd in heads:
        acc_ref[hd] = _dot(vmt_ref[hd, 0, :, 0:N_META], p0[hd])
    carry = tuple(carry)

    def pair(kp, carry):
        scores(2 * kp + 1, sb_ref)
        carry = update(2 * kp, sa_ref, carry, None)
        scores(2 * kp + 2, sa_ref)
        return update(2 * kp + 1, sb_ref, carry, None)

    carry = lax.fori_loop(0, j, pair, carry)
    scores(2 * j + 1, sb_ref)
    carry = update(2 * j, sa_ref, carry, 0)
    carry = update(2 * j + 1, sb_ref, carry, 1)
    for hd in heads:
        o_ref[:, hd * vd:(hd + 1) * vd] = (acc_ref[hd] / carry[hd][1]).T.astype(o_ref.dtype)


def _oproj_kernel(o_ref, h_ref, wo_ref, fnorm_ref, wrt_ref, br_ref, triu_ref,
                  h_out_ref, xn_ref, ri_ref, rf_ref, cnt_ref, base_ref, wsplit_ref):
    i = pl.program_id(0)

    @pl.when(i == 0)
    def _():
        _route_init(wrt_ref, wsplit_ref, base_ref)

    h = h_ref[...] + _dot(o_ref[...], wo_ref[...])
    h_out_ref[...] = h
    xn = _rms(h, fnorm_ref[...])
    _store_token_tiles(xn_ref, _pack_rows(xn))
    valid = lax.broadcasted_iota(jnp.int32, (1, h.shape[0]), 1) >= 0
    _route(xn, valid, wsplit_ref, br_ref, triu_ref, base_ref, ri_ref, rf_ref, cnt_ref)


def _full(shape):
    nd = len(shape)
    return pl.BlockSpec(shape, lambda *_: (0,) * nd)


def _router_operands(router_g, router_g_bias, router_e, router_e_bias):
    d = router_g.shape[0]
    wrt = jnp.concatenate([router_g.T, jnp.zeros((8 - N_EXPERT_GROUPS, d), _F32), router_e.T], axis=0)
    br = jnp.concatenate([router_g_bias, jnp.zeros((8 - N_EXPERT_GROUPS,), _F32), router_e_bias])[:, None]
    return wrt.astype(_F32), br.astype(_F32)


def _route_out(n_tiles, tm):
    shapes = [jax.ShapeDtypeStruct((n_tiles, 8, tm), jnp.int32),
              jax.ShapeDtypeStruct((n_tiles, 8, tm), _F32),
              jax.ShapeDtypeStruct((N_EXPERTS, 128), jnp.int32)]
    specs = [pl.BlockSpec((1, 8, tm), lambda i: (i, 0, 0)),
             pl.BlockSpec((1, 8, tm), lambda i: (i, 0, 0)),
             pl.BlockSpec((N_EXPERTS, 128), lambda i: (0, 0))]
    return shapes, specs


def _moe_experts(xn, ri, counts, w_gate, w_up, w_down, layer, n_tiles, n_real_tiles, d):
    tm, tr = TOKEN_TILE, EXPERT_ROWS
    parts = d // (2 * LANES)
    n_valid = n_real_tiles * tm + (n_tiles - n_real_tiles) * N_META
    n_blocks = -(-(2 * n_valid + N_EXPERTS * (tr - 1)) // tr)
    n_blocks = -(-n_blocks // EXPERT_BLOCKS_PER_STEP) * EXPERT_BLOCKS_PER_STEP
    n_rows = n_blocks * tr

    counts = counts[:, 0]
    padded = (counts + tr - 1) // tr * tr
    pends = jnp.cumsum(padded)
    pstarts = pends - padded
    n_used = (pends[-1] // tr).astype(jnp.int32).reshape(1)
    blk0 = jnp.arange(n_blocks, dtype=jnp.int32) * tr
    block_e = jnp.minimum(jnp.sum(blk0[:, None] >= pends[None, :], axis=1), N_EXPERTS - 1).astype(jnp.int32)
    experts = jnp.arange(N_EXPERTS, dtype=jnp.int32)
    block_oh = block_e[:, None] == experts[None, :]
    block_cnt = jnp.sum(jnp.where(block_oh, counts[None, :], 0), axis=1)
    block_start = jnp.sum(jnp.where(block_oh, pstarts[None, :], 0), axis=1)
    block_valid = jnp.clip(block_cnt - (blk0 - block_start), 0, tr).astype(jnp.int32)
    eid = ri[:, 0:2, :]
    slot0 = jnp.sum(jnp.where(eid[..., None] == experts, pstarts, 0), axis=-1)
    dest = (slot0 + ri[:, 2:4, :]).astype(jnp.int32).reshape(n_tiles * 2 * tm)

    pad_start = (pstarts + counts).astype(jnp.int32)
    pad_len = (padded - counts).astype(jnp.int32)
    xs = pl.pallas_call(
        functools.partial(_dispatch_kernel, tm=tm, tr=tr, n_tiles=n_tiles, n_real_tiles=n_real_tiles,
                          n_blocks=n_blocks),
        grid_spec=pltpu.PrefetchScalarGridSpec(
            num_scalar_prefetch=3,
            grid=(n_tiles,),
            in_specs=[pl.BlockSpec(memory_space=pl.ANY), pl.BlockSpec(memory_space=pl.ANY)],
            out_specs=pl.BlockSpec(memory_space=pl.ANY),
            scratch_shapes=[pltpu.SMEM((4 * tm,), jnp.int32), pltpu.VMEM((3, tm * parts, LANES), jnp.uint32),
                            pltpu.VMEM((tr * parts, LANES), jnp.uint32),
                            pltpu.SemaphoreType.DMA((2,)), pltpu.SemaphoreType.DMA((3,)),
                            pltpu.SemaphoreType.DMA((3,)), pltpu.SemaphoreType.DMA]),
        out_shape=jax.ShapeDtypeStruct((n_rows * parts, LANES), jnp.uint32),
        compiler_params=_params(1, has_side_effects=True, disable_bounds_checks=True),
        name="moe_dispatch",
    )(pad_start, pad_len, pends[-1:].astype(jnp.int32), dest, xn)

    f = w_gate.shape[3]
    nb = EXPERT_BLOCKS_PER_STEP
    step_rows = nb * tr * parts

    def expert_block(c):
        return lambda i, be, bv, nu: (layer, be[jnp.minimum(i * nb + c, nu[0] - 1)], 0, 0)

    weight_specs, weight_scratch = [], []
    for c in range(nb):
        weight_specs += [pl.BlockSpec((1, 1, d, f), expert_block(c)), pl.BlockSpec((1, 1, d, f), expert_block(c)),
                         pl.BlockSpec((1, 1, f, d), expert_block(c))]
        weight_scratch += [pltpu.VMEM((d, f), _BF16), pltpu.VMEM((d, f), _BF16), pltpu.VMEM((f, d), _BF16)]
    y = pl.pallas_call(
        _expert_kernel,
        grid_spec=pltpu.PrefetchScalarGridSpec(
            num_scalar_prefetch=3,
            grid=(n_blocks // nb,),
            in_specs=[pl.BlockSpec((step_rows, LANES),
                                   lambda i, be, bv, nu: (jnp.minimum(i, (nu[0] - 1) // nb), 0))] + weight_specs,
            out_specs=pl.BlockSpec((step_rows, LANES), lambda i, be, bv, nu: (i, 0)),
            scratch_shapes=weight_scratch),
        out_shape=jax.ShapeDtypeStruct((n_rows * parts, LANES), jnp.uint32),
        compiler_params=_params(1),
        name="moe_experts",
    )(block_e, block_valid, n_used, xs, *([w_gate, w_up, w_down] * nb))

    return dest, y


def _combine_operands(dest, rf, h, y, d):
    tm = TOKEN_TILE
    parts = d // (2 * LANES)
    in_specs = [pl.BlockSpec(memory_space=pl.ANY),
                pl.BlockSpec((1, 8, tm), lambda i: (i, 0, 0)),
                pl.BlockSpec((tm, d), lambda i: (i, 0)),
                pl.BlockSpec(memory_space=pl.ANY)]
    scratch = [pltpu.SMEM((4 * tm,), jnp.int32),
               pltpu.VMEM((2 * tm * parts, LANES), jnp.uint32), pltpu.VMEM((2 * tm * parts, LANES), jnp.uint32),
               pltpu.SemaphoreType.DMA((2,)), pltpu.SemaphoreType.DMA((2,))]
    return (dest, rf, h, y), in_specs, scratch


def kernel(x, meta_tokens, a_norm, a_w, a_scale, b_norm, b_w_dq, b_q_norm, b_w_uq, b_w_o, kv_norm, w_dkv,
           kv_lat_norm, w_uk, w_uv, ffn_norm, router_g, router_g_bias, router_e, router_e_bias, w_gate, w_up,
           w_down, final_norm):
    bsz, seq, d = x.shape
    tm = TOKEN_TILE
    assert seq % tm == 0 and seq % ATTN_Q_TILE == 0 and ATTN_Q_TILE == 2 * ATTN_K_TILE and tm % ATTN_K_TILE == 0
    kt = tm // ATTN_K_TILE
    assert d % (LANES * len(POOL_WINDOWS)) == 0 and N_META == max(POOL_WINDOWS) and N_META <= tm
    parts = d // (2 * LANES)
    n_tok = bsz * seq
    n_real_tiles = n_tok // tm
    tiles_per_batch = seq // tm
    n_tiles = n_real_tiles + 1
    gd = d // len(POOL_WINDOWS)
    row = lambda v: v.reshape(1, -1).astype(_F32)

    x2 = x.reshape(n_tok, d)
    meta_pad = jnp.concatenate([meta_tokens, jnp.zeros((tm - N_META, d), x.dtype)], axis=0)

    r = np.arange(tm)[:, None]
    cidx = np.arange(tm)[None, :]
    pb = min(POOL_BLOCK, tm)
    pm = np.stack([((r[:pb] - cidx[:, :pb] >= 0) & (r[:pb] - cidx[:, :pb] < w)) for w in POOL_WINDOWS]).astype(np.float32)
    hc = np.arange(N_META)[None, :]
    ph = np.stack([(r[:N_META] + N_META - hc < w) for w in POOL_WINDOWS]).astype(np.float32)
    triu = (r < cidx).astype(np.float32)
    pm, ph, triu = (jnp.asarray(a, dtype=_BF16) for a in (pm, ph, triu))

    wrt0, br0 = _router_operands(router_g[0], router_g_bias[0], router_e[0], router_e_bias[0])
    route_shapes, route_specs = _route_out(n_tiles, tm)
    tile_or_last = lambda i: (jnp.minimum(i, n_real_tiles - 1), 0)
    halo_blocks = tm // N_META
    h1, xn1, ri1, rf1, cnt1 = pl.pallas_call(
        functools.partial(_mixer_kernel, n_real_tiles=n_real_tiles, tiles_per_batch=tiles_per_batch),
        grid=(n_tiles,),
        in_specs=[pl.BlockSpec((tm, d), tile_or_last),
                  pl.BlockSpec((N_META, d), lambda i: (jnp.clip(i * halo_blocks - 1, 0, n_tok // N_META - 1), 0)),
                  _full((tm, d)), _full((N_META, d)), _full((1, d)),
                  _full((len(POOL_WINDOWS), gd, gd)), _full((1, d)), _full((1, d)),
                  _full((ROUTER_ROWS, d)), _full((ROUTER_ROWS, 1)),
                  _full(pm.shape), _full(ph.shape), _full((tm, tm))],
        out_specs=[pl.BlockSpec((tm, d), lambda i: (i, 0)),
                   pl.BlockSpec((tm * parts, LANES), lambda i: (i, 0))] + route_specs,
        out_shape=[jax.ShapeDtypeStruct((n_tiles * tm, d), _F32),
                   jax.ShapeDtypeStruct((n_tiles * tm * parts, LANES), jnp.uint32)] + route_shapes,
        scratch_shapes=[pltpu.VMEM((N_EXPERTS, 1), _F32), pltpu.VMEM((ROUTER_SPLIT_ROWS, d), _BF16)],
        compiler_params=_params(1),
        name="pool_mixer_router",
    )(x2, x2, meta_pad, meta_tokens, row(a_norm[0]), a_w[0].astype(_BF16), row(a_scale[0]), row(ffn_norm[0]),
      wrt0, br0, pm, ph, triu)

    dest1, y1 = _moe_experts(xn1, ri1, cnt1, w_gate, w_up, w_down, 0, n_tiles, n_real_tiles, d)

    pos = jnp.concatenate([jnp.arange(seq, dtype=_F32) + N_META, jnp.arange(tm, dtype=_F32)])
    inv_freq = ROPE_THETA ** (-jnp.arange(0, QK_ROPE_DIM, 2, dtype=_F32) / QK_ROPE_DIM)
    ang = pos[:, None] * inv_freq[None, :]
    cos_t = jnp.tile(jnp.cos(ang), (1, 4))
    sin_t = jnp.tile(jnp.concatenate([-jnp.sin(ang), jnp.sin(ang)], axis=1), (1, 2))

    wdkv = jnp.concatenate([w_dkv, w_dkv[:, KV_LORA_RANK:]], axis=1).astype(_BF16)
    wuk = w_uk.reshape(KV_LORA_RANK, N_HEADS * QK_NOPE_DIM).astype(_BF16)
    wuv = w_uv.reshape(KV_LORA_RANK, N_HEADS * V_HEAD_DIM).T.astype(_BF16)
    wuq = b_w_uq[0]
    q_rank = wuq.shape[0]
    wuq = jnp.concatenate([wuq[:, :, :QK_NOPE_DIM].reshape(q_rank, -1),
                           wuq[:, :, QK_NOPE_DIM:].reshape(q_rank, -1)], axis=1).astype(_BF16)
    wdq = b_w_dq[0].astype(_BF16)
    pos_tile = lambda i: (jnp.where(i < n_real_tiles, i % tiles_per_batch, tiles_per_batch), 0)
    head_tile = lambda i: (0, i, 0)
    operands, in_specs, scratch = _combine_operands(dest1, rf1, h1, y1, d)
    h2, q, k, v = pl.pallas_call(
        functools.partial(_combine_proj_kernel, tm=tm, n_tiles=n_tiles),
        grid=(n_tiles,),
        in_specs=in_specs + [
            pl.BlockSpec((tm, 128), pos_tile), pl.BlockSpec((tm, 128), pos_tile),
            _full((1, d)), _full(wdkv.shape), _full((1, KV_LORA_RANK)), _full(wuk.shape), _full(wuv.shape),
            _full((1, d)), _full(wdq.shape), _full((1, q_rank)), _full(wuq.shape)],
        out_specs=[pl.BlockSpec((tm, d), lambda i: (i, 0)),
                   pl.BlockSpec((N_HEADS, tm, QK_DIM), head_tile),
                   pl.BlockSpec((N_HEADS, tm, QK_DIM), head_tile),
                   pl.BlockSpec((N_HEADS, kt, V_HEAD_DIM, ATTN_K_TILE), lambda i: (0, i, 0, 0))],
        out_shape=[jax.ShapeDtypeStruct((n_tiles * tm, d), _F32),
                   jax.ShapeDtypeStruct((N_HEADS, n_tiles * tm, QK_DIM), _BF16),
                   jax.ShapeDtypeStruct((N_HEADS, n_tiles * tm, QK_DIM), _BF16),
                   jax.ShapeDtypeStruct((N_HEADS, n_tiles * kt, V_HEAD_DIM, ATTN_K_TILE), _BF16)],
        scratch_shapes=scratch,
        compiler_params=_params(1, disable_bounds_checks=True),
        name="moe_combine_latent_qkv",
    )(*operands, cos_t, sin_t, row(kv_norm), wdkv, row(kv_lat_norm), wuk, wuv,
      row(b_norm[0]), wdq, row(b_q_norm[0]), wuq)

    tq, tk, hp = ATTN_Q_TILE, ATTN_K_TILE, ATTN_HEADS_PER_STEP
    n_q = seq // tq
    meta_block = n_tok // N_META
    o = pl.pallas_call(
        functools.partial(_attn_kernel, tq=tq, tk=tk),
        grid=(bsz, N_HEADS // hp, n_q),
        in_specs=[pl.BlockSpec((hp, tq, QK_DIM), lambda b, hg, j: (hg, b * n_q + j, 0)),
                  pl.BlockSpec((hp, seq, QK_DIM), lambda b, hg, j: (hg, b, 0)),
                  pl.BlockSpec((hp, seq // tk, V_HEAD_DIM, tk), lambda b, hg, j: (hg, b, 0, 0)),
                  pl.BlockSpec((hp, N_META, QK_DIM), lambda b, hg, j: (hg, meta_block, 0)),
                  pl.BlockSpec((hp, 1, V_HEAD_DIM, tk), lambda b, hg, j: (hg, n_real_tiles * kt, 0, 0))],
        out_specs=pl.BlockSpec((tq, hp * V_HEAD_DIM), lambda b, hg, j: (b * n_q + j, hg)),
        out_shape=jax.ShapeDtypeStruct((n_tok, N_HEADS * V_HEAD_DIM), _BF16),
        scratch_shapes=[pltpu.VMEM((hp, tk, tq), _F32), pltpu.VMEM((hp, tk, tq), _F32),
                        pltpu.VMEM((hp, V_HEAD_DIM, tq), _F32)],
        compiler_params=_params(3),
        name="causal_attention",
    )(q, k, v, k, v)

    wrt1, br1 = _router_operands(router_g[1], router_g_bias[1], router_e[1], router_e_bias[1])
    route_shapes, route_specs = _route_out(n_real_tiles, tm)
    h3, xn2, ri2, rf2, cnt2 = pl.pallas_call(
        _oproj_kernel,
        grid=(n_real_tiles,),
        in_specs=[pl.BlockSpec((tm, N_HEADS * V_HEAD_DIM), lambda i: (i, 0)),
                  pl.BlockSpec((tm, d), lambda i: (i, 0)),
                  _full((N_HEADS * V_HEAD_DIM, d)), _full((1, d)),
                  _full((ROUTER_ROWS, d)), _full((ROUTER_ROWS, 1)), _full((tm, tm))],
        out_specs=[pl.BlockSpec((tm, d), lambda i: (i, 0)),
                   pl.BlockSpec((tm * parts, LANES), lambda i: (i, 0))] + route_specs,
        out_shape=[jax.ShapeDtypeStruct((n_tok, d), _F32),
                   jax.ShapeDtypeStruct((n_tok * parts, LANES), jnp.uint32)] + route_shapes,
        scratch_shapes=[pltpu.VMEM((N_EXPERTS, 1), _F32), pltpu.VMEM((ROUTER_SPLIT_ROWS, d), _BF16)],
        compiler_params=_params(1),
        name="attn_out_router",
    )(o, h2, b_w_o[0].astype(_BF16), row(ffn_norm[1]), wrt1, br1, triu)

    dest2, y2 = _moe_experts(xn2, ri2, cnt2, w_gate, w_up, w_down, 1, n_real_tiles, n_real_tiles, d)
    operands, in_specs, scratch = _combine_operands(dest2, rf2, h3, y2, d)
    out = pl.pallas_call(
        functools.partial(_combine_norm_kernel, tm=tm, n_tiles=n_real_tiles),
        grid=(n_real_tiles,),
        in_specs=in_specs + [_full((1, d))],
        out_specs=pl.BlockSpec((tm, d), lambda i: (i, 0)),
        out_shape=jax.ShapeDtypeStruct((n_tok, d), _F32),
        scratch_shapes=scratch,
        compiler_params=_params(1, disable_bounds_checks=True),
        name="moe_combine_final_norm",
    )(*operands, row(final_norm))
    return out.reshape(bsz, seq, d)
```

```python
import functools

import numpy as np
import jax
import jax.numpy as jnp
from jax import lax
from jax.experimental import pallas as pl
from jax.experimental.pallas import tpu as pltpu

N_META = 16
POOL_WINDOWS = (2, 4, 8, 16)
N_HEADS = 8
QK_NOPE_DIM = 128
QK_ROPE_DIM = 64
QK_DIM = QK_NOPE_DIM + QK_ROPE_DIM
V_HEAD_DIM = 128
KV_LORA_RANK = 256
ROPE_THETA = 10000.0
ATTN_SCALE = QK_DIM ** -0.5
Q_SCALE = ATTN_SCALE * 1.4426950408889634
N_EXPERT_GROUPS = 4
EXPERTS_PER_GROUP = 8
N_EXPERTS = N_EXPERT_GROUPS * EXPERTS_PER_GROUP
RMS_EPS = 1e-6
NEG_INF = -1e30

TOKEN_TILE = 512
POOL_BLOCK = 256
EXPERT_ROWS = 256
EXPERT_BLOCKS_PER_STEP = 2
ATTN_Q_TILE = 512
ATTN_K_TILE = ATTN_Q_TILE // 2
ATTN_HEADS_PER_STEP = 4
DMA_UNROLL = 8
DMA_THREADS = 2
ROUTER_ROWS = 8 + N_EXPERTS
ROUTER_LO_ROW = 48
ROUTER_SPLIT_ROWS = 2 * ROUTER_LO_ROW
VMEM_LIMIT_BYTES = 48 * 1024 * 1024

LANES = 128
_F32 = jnp.float32
_BF16 = jnp.bfloat16
_NT_DIMS = (((1,), (1,)), ((), ()))


def _params(n_grid_dims=1, **kw):
    return pltpu.CompilerParams(dimension_semantics=("arbitrary",) * n_grid_dims,
                                vmem_limit_bytes=VMEM_LIMIT_BYTES, **kw)


def _rms(x, g):
    ms = jnp.mean(x * x, axis=-1, keepdims=True)
    return x * lax.rsqrt(ms + RMS_EPS) * g


def _split_bf16(x):
    hi = x.astype(_BF16)
    lo = (x - hi.astype(_F32)).astype(_BF16)
    return hi, lo


def _dot(a, b):
    return jnp.dot(a, b, preferred_element_type=_F32)


def _store_token_tiles(ref, x):
    n, d = x.shape
    parts = d // LANES
    for s in range(parts):
        ref[pl.ds(s, n, stride=parts), :] = x[:, s * LANES:(s + 1) * LANES]


def _load_token_tiles(ref, first_token, n, parts):
    return jnp.concatenate([ref[pl.ds(first_token * parts + s, n, stride=parts), :] for s in range(parts)], axis=1)


_HIGH_HALF = 0xFFFF0000


def _pack_rows(x):
    n, d = x.shape
    words = []
    for s in range(d // (2 * LANES)):
        lo = x[:, 2 * s * LANES:(2 * s + 1) * LANES].astype(_BF16).astype(_F32)
        hi = x[:, (2 * s + 1) * LANES:(2 * s + 2) * LANES].astype(_BF16).astype(_F32)
        words.append(lax.shift_right_logical(lax.bitcast_convert_type(lo, jnp.uint32), jnp.uint32(16))
                     | (lax.bitcast_convert_type(hi, jnp.uint32) & jnp.uint32(_HIGH_HALF)))
    return jnp.concatenate(words, axis=1)


def _unpack_rows(w):
    cols = []
    for s in range(w.shape[1] // LANES):
        ws = w[:, s * LANES:(s + 1) * LANES]
        cols.append(lax.bitcast_convert_type(lax.shift_left(ws, jnp.uint32(16)), _F32))
        cols.append(lax.bitcast_convert_type(ws & jnp.uint32(_HIGH_HALF), _F32))
    return jnp.concatenate(cols, axis=1)


def _route_init(wrt_ref, wsplit_ref, base_ref):
    base_ref[...] = jnp.zeros(base_ref.shape, base_ref.dtype)
    w_hi, w_lo = _split_bf16(wrt_ref[...])
    wsplit_ref[...] = jnp.zeros(wsplit_ref.shape, wsplit_ref.dtype)
    wsplit_ref[0:ROUTER_ROWS, :] = w_hi
    wsplit_ref[ROUTER_LO_ROW:ROUTER_LO_ROW + ROUTER_ROWS, :] = w_lo


def _route(xn, valid, wsplit_ref, br_ref, triu_ref, base_ref, ri_ref, rf_ref, cnt_ref):
    tm = xn.shape[0]
    x_hi, x_lo = _split_bf16(xn)
    by_hi = lax.dot_general(wsplit_ref[...], x_hi, _NT_DIMS, preferred_element_type=_F32)
    by_lo = lax.dot_general(wsplit_ref[0:ROUTER_LO_ROW, :], x_lo, _NT_DIMS, preferred_element_type=_F32)
    logits = (by_hi[0:ROUTER_ROWS] + by_hi[ROUTER_LO_ROW:ROUTER_LO_ROW + ROUTER_ROWS]
              + by_lo[0:ROUTER_ROWS] + br_ref[...])
    lg = logits[0:N_EXPERT_GROUPS]
    eg = jnp.exp(lg - jnp.max(lg, axis=0, keepdims=True))
    pg = eg / jnp.sum(eg, axis=0, keepdims=True)
    w_g = jnp.max(pg, axis=0, keepdims=True)
    ig = lax.broadcasted_iota(jnp.int32, pg.shape, 0).astype(_F32)
    g_sel = jnp.min(jnp.where(pg == w_g, ig, float(N_EXPERT_GROUPS)), axis=0, keepdims=True)

    sel = logits[8:8 + EXPERTS_PER_GROUP]
    for g in range(1, N_EXPERT_GROUPS):
        sel = jnp.where(g_sel == float(g), logits[8 + g * EXPERTS_PER_GROUP:8 + (g + 1) * EXPERTS_PER_GROUP], sel)
    ie = lax.broadcasted_iota(jnp.int32, sel.shape, 0).astype(_F32)
    v1 = jnp.max(sel, axis=0, keepdims=True)
    i1 = jnp.min(jnp.where(sel == v1, ie, float(EXPERTS_PER_GROUP)), axis=0, keepdims=True)
    rest = jnp.where(ie == i1, -jnp.inf, sel)
    v2 = jnp.max(rest, axis=0, keepdims=True)
    i2 = jnp.min(jnp.where(rest == v2, ie, float(EXPERTS_PER_GROUP)), axis=0, keepdims=True)
    e2 = jnp.exp(v2 - v1)
    den = 1.0 + e2
    validf = valid.astype(_F32)
    gate0 = w_g * (1.0 / den) * validf
    gate1 = w_g * (e2 / den) * validf
    eid0 = g_sel * float(EXPERTS_PER_GROUP) + i1
    eid1 = g_sel * float(EXPERTS_PER_GROUP) + i2

    iall = lax.broadcasted_iota(jnp.int32, (N_EXPERTS, tm), 0).astype(_F32)
    oh0 = jnp.where(iall == eid0, validf, 0.0)
    oh1 = jnp.where(iall == eid1, validf, 0.0)
    both = oh0 + oh1
    before = _dot(both.astype(_BF16), triu_ref[...]) + base_ref[...]
    rank0 = jnp.sum(oh0 * before, axis=0, keepdims=True)
    rank1 = jnp.sum(oh1 * before, axis=0, keepdims=True)
    base_ref[...] = base_ref[...] + jnp.sum(both, axis=1, keepdims=True)

    ri_ref[...] = jnp.zeros(ri_ref.shape, ri_ref.dtype)
    rf_ref[...] = jnp.zeros(rf_ref.shape, rf_ref.dtype)
    ri_ref[0, 0:1, :] = eid0.astype(jnp.int32)
    ri_ref[0, 1:2, :] = eid1.astype(jnp.int32)
    ri_ref[0, 2:3, :] = rank0.astype(jnp.int32)
    ri_ref[0, 3:4, :] = rank1.astype(jnp.int32)
    rf_ref[0, 0:1, :] = gate0
    rf_ref[0, 1:2, :] = gate1
    cnt_ref[...] = jnp.broadcast_to(base_ref[...], cnt_ref.shape).astype(jnp.int32)


def _mixer_kernel(x_ref, xh_ref, mp_ref, meta_ref, anorm_ref, aw_ref, ascale_ref, fnorm_ref,
                  wrt_ref, br_ref, pm_ref, ph_ref, triu_ref,
                  h_ref, xn_ref, ri_ref, rf_ref, cnt_ref, base_ref, wsplit_ref, *, n_real_tiles, tiles_per_batch):
    i = pl.program_id(0)
    tm = x_ref.shape[0]
    gd = x_ref.shape[1] // len(POOL_WINDOWS)
    is_meta = i == n_real_tiles
    first = (i % tiles_per_batch) == 0

    @pl.when(i == 0)
    def _():
        _route_init(wrt_ref, wsplit_ref, base_ref)

    h = jnp.where(is_meta, mp_ref[...], x_ref[...])
    halo = jnp.where(is_meta, 0.0, jnp.where(first, meta_ref[...], xh_ref[...]))
    hn = _rms(h, anorm_ref[...])
    hh = _rms(halo, anorm_ref[...])
    hn_hi, hn_lo = _split_bf16(hn)
    hh_hi, hh_lo = _split_bf16(hh)
    pb = pm_ref.shape[1]
    wins = {}
    for c in range(tm // pb):
        rows = slice(c * pb, (c + 1) * pb)
        for g in range(len(POOL_WINDOWS)):
            sl = slice(g * gd, (g + 1) * gd)
            win = _dot(pm_ref[g], hn_hi[rows, sl]) + _dot(pm_ref[g], hn_lo[rows, sl])
            if c == 0:
                before_hi, before_lo = hh_hi[:, sl], hh_lo[:, sl]
            else:
                before_hi, before_lo = hn_hi[c * pb - N_META:c * pb, sl], hn_lo[c * pb - N_META:c * pb, sl]
            top = win[:N_META] + _dot(ph_ref[g], before_hi) + _dot(ph_ref[g], before_lo)
            wins[c, g] = jnp.concatenate([top, win[N_META:]], axis=0)
    for c in range(tm // pb):
        rows = slice(c * pb, (c + 1) * pb)
        row = c * pb + lax.broadcasted_iota(jnp.int32, (pb, 1), 0)
        for g, w in enumerate(POOL_WINDOWS):
            sl = slice(g * gd, (g + 1) * gd)
            cnt = jnp.where(is_meta, jnp.minimum(row + 1, w), w).astype(_F32)
            pooled = wins[c, g] * (1.0 / cnt) - hn[rows, sl]
            mix = _dot(pooled.astype(_BF16), aw_ref[g])
            h_ref[rows, sl] = h[rows, sl] + mix * ascale_ref[:, sl]

    xn = _rms(h_ref[...], fnorm_ref[...])
    _store_token_tiles(xn_ref, _pack_rows(xn))
    lane = lax.broadcasted_iota(jnp.int32, (1, tm), 1)
    valid = jnp.logical_or(jnp.logical_not(is_meta), lane < N_META)
    _route(xn, valid, wsplit_ref, br_ref, triu_ref, base_ref, ri_ref, rf_ref, cnt_ref)


def _index_copy(dest_hbm, idx_smem, isem, tile, n):
    return pltpu.make_async_copy(dest_hbm.at[pl.ds(pl.multiple_of(tile * n, n), n)],
                                 idx_smem.at[pl.ds(pl.multiple_of((tile % 2) * n, n), n)], isem.at[tile % 2])


def _dispatch_kernel(pad_start_ref, pad_len_ref, tail_ref, dest_hbm, xn_hbm, xs_hbm,
                     idx_smem, xbuf, zbuf, isem, lsem, ssem, zsem, *, tm, tr, n_tiles, n_real_tiles, n_blocks):
    i = pl.program_id(0)
    parts = xbuf.shape[1] // tm
    buf = i % 3

    def zero_fill(wait):
        def zero_rows(first_row, n_rows):
            cp = pltpu.make_async_copy(
                zbuf.at[pl.ds(0, n_rows * parts), :],
                xs_hbm.at[pl.ds(pl.multiple_of(first_row * parts, parts), n_rows * parts), :], zsem)
            if wait:
                cp.wait()
            else:
                cp.start()

        def per_expert(e, c):
            first, length = pad_start_ref[e], pad_len_ref[e]
            for bit in range((tr - 1).bit_length()):
                @pl.when((length >> bit) % 2 == 1)
                def _(bit=bit):
                    zero_rows(first + ((length >> (bit + 1)) << (bit + 1)), 1 << bit)
            return c

        def per_block(blk, c):
            zero_rows(blk * tr, tr)
            return c

        lax.fori_loop(0, N_EXPERTS, per_expert, 0)
        lax.fori_loop(tail_ref[0] // tr, n_blocks, per_block, 0)

    def idx_copy(tile):
        return _index_copy(dest_hbm, idx_smem, isem, tile, 2 * tm)

    def load(tile):
        rows = tm * parts
        return pltpu.make_async_copy(xn_hbm.at[pl.ds(pl.multiple_of(tile * rows, rows), rows), :],
                                     xbuf.at[tile % 3], lsem.at[tile % 3])

    def wait_scatters(tile):
        def wait_rows(n_tok):
            for k in range(2):
                pltpu.make_async_copy(xbuf.at[tile % 3, pl.ds(0, n_tok * parts), :],
                                      xs_hbm.at[pl.ds(0, n_tok * parts), :], ssem.at[tile % 3]).wait()

        @pl.when(tile < n_real_tiles)
        def _():
            wait_rows(tm)

        @pl.when(tile >= n_real_tiles)
        def _():
            wait_rows(N_META)

    def scatter(n_tok):
        def issue(g, c):
            for u in range(min(DMA_UNROLL, n_tok)):
                t = g * min(DMA_UNROLL, n_tok) + u
                src = xbuf.at[buf, pl.ds(pl.multiple_of(t * parts, parts), parts), :]
                for k in range(2):
                    d = idx_smem[(i % 2) * (2 * tm) + k * tm + t]
                    pltpu.make_async_copy(src, xs_hbm.at[pl.ds(pl.multiple_of(d * parts, parts), parts), :],
                                          ssem.at[buf]).start(priority=(2 * u + k) % DMA_THREADS)
            return c

        lax.fori_loop(0, n_tok // min(DMA_UNROLL, n_tok), issue, 0)

    @pl.when(i == 0)
    def _():
        idx_copy(0).start()
        load(0).start()
        zbuf[...] = jnp.zeros(zbuf.shape, zbuf.dtype)
        zero_fill(wait=False)

    @pl.when(i >= 2)
    def _():
        wait_scatters(i - 2)

    @pl.when(i + 1 < n_tiles)
    def _():
        idx_copy(i + 1).start()
        load(i + 1).start()

    idx_copy(i).wait()
    load(i).wait()

    @pl.when(i < n_real_tiles)
    def _():
        scatter(tm)

    @pl.when(i >= n_real_tiles)
    def _():
        scatter(N_META)

    @pl.when(i == n_tiles - 1)
    def _():
        if n_tiles >= 2:
            wait_scatters(i - 1)
        wait_scatters(i)
        zero_fill(wait=True)


def _expert_kernel(be_ref, bv_ref, nu_ref, xs_ref, *refs):
    nb = EXPERT_BLOCKS_PER_STEP
    w_refs, y_ref, wb_refs = refs[:3 * nb], refs[3 * nb], refs[3 * nb + 1:]
    i = pl.program_id(0)
    n_used = nu_ref[0]
    parts = wb_refs[0].shape[0] // (2 * LANES)
    tr = xs_ref.shape[0] // (nb * parts)

    for c in range(nb):
        blk = i * nb + c
        changed = be_ref[blk] != be_ref[jnp.maximum(blk - nb, 0)]

        @pl.when(jnp.logical_or(i == 0, jnp.logical_and(blk < n_used, changed)))
        def _(c=c):
            for w_ref, wb_ref in zip(w_refs[3 * c:3 * c + 3], wb_refs[3 * c:3 * c + 3]):
                wb_ref[...] = w_ref[0, 0].astype(_BF16)

    @pl.when(i * nb < n_used)
    def _():
        rows = lax.broadcasted_iota(jnp.int32, (tr, 1), 0)
        xs, gs, us, ys = [], [], [], []
        for c in range(nb):
            x = jnp.where(rows < bv_ref[i * nb + c], _unpack_rows(_load_token_tiles(xs_ref, c * tr, tr, parts)), 0.0)
            xs.append(x.astype(_BF16))
        for c in range(nb):
            gs.append(_dot(xs[c], wb_refs[3 * c][...]))
            us.append(_dot(xs[c], wb_refs[3 * c + 1][...]))
        for c in range(nb):
            a = gs[c] * (1.0 / (1.0 + jnp.exp(-gs[c]))) * us[c]
            ys.append(_dot(a.astype(_BF16), wb_refs[3 * c + 2][...]))
        for c in range(nb):
            y = jnp.where(i * nb + c < n_used, ys[c], 0.0)
            _store_token_tiles(y_ref.at[pl.ds(c * tr * parts, tr * parts), :], _pack_rows(y))

    @pl.when(i * nb >= n_used)
    def _():
        y_ref[...] = jnp.zeros(y_ref.shape, y_ref.dtype)


def _combined_tile(dest_hbm, rf_ref, h_ref, y_hbm, idx_smem, ybufs, isem, rsem, *, tm, n_tiles, slot):
    i = pl.program_id(0)
    parts = ybufs[0].shape[0] // (2 * tm)

    def idx_copy(tile):
        return _index_copy(dest_hbm, idx_smem, isem, tile, 2 * tm)

    def row_copy(idx_slot, buf_slot, t, k, u):
        d = idx_smem[idx_slot * (2 * tm) + k * tm + t]
        row = (k * tm + t) * parts
        if not isinstance(t, int):
            row = pl.multiple_of(row, parts)
        return pltpu.make_async_copy(
            y_hbm.at[pl.ds(pl.multiple_of(d * parts, parts), parts), :],
            ybufs[buf_slot].at[pl.ds(row, parts), :],
            rsem.at[buf_slot]).start(priority=(2 * u + k) % DMA_THREADS)

    def wait_rows(buf_slot):
        pltpu.make_async_copy(y_hbm.at[pl.ds(0, 2 * tm * parts), :], ybufs[buf_slot], rsem.at[buf_slot]).wait()

    if slot == 0:
        @pl.when(i == 0)
        def _():
            idx_copy(0).start()
            idx_copy(0).wait()

            def issue(g, c):
                for u in range(DMA_UNROLL):
                    for k in range(2):
                        row_copy(0, 0, g * DMA_UNROLL + u, k, u)
                return c

            lax.fori_loop(0, tm // DMA_UNROLL, issue, 0)
            if n_tiles >= 2:
                idx_copy(1).start()

    @pl.when(i + 1 < n_tiles)
    def _():
        idx_copy(i + 1).wait()

    @pl.when(i + 2 < n_tiles)
    def _():
        idx_copy(i + 2).start()

    nxt_idx_slot = jnp.minimum(i + 1, n_tiles - 1) % 2

    def prefetch(part, n_parts):
        for t in range(part * tm // n_parts, (part + 1) * tm // n_parts):
            for k in range(2):
                row_copy(nxt_idx_slot, 1 - slot, t, k, t)

    gates = rf_ref[0]
    gt = jnp.concatenate([gates, jnp.zeros((LANES - gates.shape[0], tm), _F32)], axis=0).T
    wait_rows(slot)
    yb = ybufs[slot]

    def combined(r0, n):
        return h_ref[r0:r0 + n, :] + (_unpack_rows(_load_token_tiles(yb, r0, n, parts)) * gt[r0:r0 + n, 0:1]
                                      + _unpack_rows(_load_token_tiles(yb, tm + r0, n, parts)) * gt[r0:r0 + n, 1:2])

    def drain():
        @pl.when(i == n_tiles - 1)
        def _():
            wait_rows(1 - slot)

    return combined, prefetch, drain


def _for_each_parity(body):
    for slot in range(2):
        @pl.when(pl.program_id(0) % 2 == slot)
        def _(slot=slot):
            body(slot)


def _combine_norm_kernel(dest_hbm, rf_ref, h_ref, y_hbm, fnorm_ref, out_ref,
                         idx_smem, ybuf0, ybuf1, isem, rsem, *, tm, n_tiles):
    def body(slot):
        combined, prefetch, drain = _combined_tile(dest_hbm, rf_ref, h_ref, y_hbm, idx_smem, (ybuf0, ybuf1),
                                                   isem, rsem, tm=tm, n_tiles=n_tiles, slot=slot)
        n_chunks = 4
        n = tm // n_chunks
        for c in range(n_chunks):
            prefetch(c, n_chunks)
            out_ref[c * n:(c + 1) * n, :] = _rms(combined(c * n, n), fnorm_ref[...])
        drain()

    _for_each_parity(body)


def _combine_proj_kernel(dest_hbm, rf_ref, h_ref, y_hbm, cos_ref, sin_ref, kvn_ref, wdkv_ref, kvlat_ref,
                         wuk_ref, wuv_ref, bnorm_ref, wdq_ref, qnorm_ref, wuq_ref,
                         h_out_ref, q_ref, k_ref, v_ref, idx_smem, ybuf0, ybuf1, isem, rsem, *, tm, n_tiles):
    def body(slot):
        combined, prefetch, drain = _combined_tile(dest_hbm, rf_ref, h_ref, y_hbm, idx_smem, (ybuf0, ybuf1),
                                                   isem, rsem, tm=tm, n_tiles=n_tiles, slot=slot)
        n = v_ref.shape[3]
        n_chunks = tm // n
        hs = []
        for c in range(n_chunks):
            prefetch(c, n_chunks)
            hs.append(combined(c * n, n))
            h_out_ref[c * n:(c + 1) * n, :] = hs[c]
        _project(hs, cos_ref, sin_ref, kvn_ref, wdkv_ref, kvlat_ref, wuk_ref, wuv_ref,
                 bnorm_ref, wdq_ref, qnorm_ref, wuq_ref, q_ref, k_ref, v_ref)
        drain()

    _for_each_parity(body)


def _rope128(x, cos_t, sin_t):
    lane = lax.broadcasted_iota(jnp.int32, (1, 128), 1)
    first_half = (lane % QK_ROPE_DIM) < (QK_ROPE_DIM // 2)
    swapped = jnp.where(first_half, pltpu.roll(x, 128 - QK_ROPE_DIM // 2, axis=1),
                        pltpu.roll(x, QK_ROPE_DIM // 2, axis=1))
    return x * cos_t + swapped * sin_t


def _project(hs, cos_ref, sin_ref, kvn_ref, wdkv_ref, kvlat_ref, wuk_ref, wuv_ref,
             bnorm_ref, wdq_ref, qnorm_ref, wuq_ref, q_ref, k_ref, v_ref):
    n = hs[0].shape[0]
    chunks = range(len(hs))
    c_kv = [_dot(_rms(h, kvn_ref[...]).astype(_BF16), wdkv_ref[...]) for h in hs]
    c_q = [_dot(_rms(h, bnorm_ref[...]).astype(_BF16), wdq_ref[...]) for h in hs]
    ckv = [_rms(c[:, :KV_LORA_RANK], kvlat_ref[...]).astype(_BF16) for c in c_kv]
    cq = [(_rms(c, qnorm_ref[...]) * Q_SCALE).astype(_BF16) for c in c_q]
    kn = [_dot(ckv[c], wuk_ref[...]) for c in chunks]
    vt = [lax.dot_general(wuv_ref[...], ckv[c], _NT_DIMS, preferred_element_type=_F32) for c in chunks]
    q = [_dot(cq[c], wuq_ref[...]) for c in chunks]
    rope0 = N_HEADS * QK_NOPE_DIM
    for c in chunks:
        rows = slice(c * n, (c + 1) * n)
        cos_t = cos_ref[rows, :]
        sin_t = sin_ref[rows, :]
        kr = _rope128(c_kv[c][:, KV_LORA_RANK:KV_LORA_RANK + 128], cos_t, sin_t)[:, :QK_ROPE_DIM].astype(_BF16)
        for hd in range(N_HEADS):
            k_ref[hd, rows, 0:QK_NOPE_DIM] = kn[c][:, hd * QK_NOPE_DIM:(hd + 1) * QK_NOPE_DIM].astype(_BF16)
            k_ref[hd, rows, QK_NOPE_DIM:QK_DIM] = kr
            v_ref[hd, c] = vt[c][hd * V_HEAD_DIM:(hd + 1) * V_HEAD_DIM, :].astype(_BF16)
            q_ref[hd, rows, 0:QK_NOPE_DIM] = q[c][:, hd * QK_NOPE_DIM:(hd + 1) * QK_NOPE_DIM].astype(_BF16)
        for pair in range(N_HEADS // 2):
            qr = _rope128(q[c][:, rope0 + pair * LANES:rope0 + (pair + 1) * LANES], cos_t, sin_t)
            q_ref[2 * pair, rows, QK_NOPE_DIM:QK_DIM] = qr[:, :QK_ROPE_DIM].astype(_BF16)
            q_ref[2 * pair + 1, rows, QK_NOPE_DIM:QK_DIM] = qr[:, QK_ROPE_DIM:].astype(_BF16)


def _attn_kernel(q_ref, k_ref, vt_ref, km_ref, vmt_ref, o_ref, sa_ref, sb_ref, acc_ref, *, tq, tk):
    j = pl.program_id(2)
    heads = range(q_ref.shape[0])
    vd = acc_ref.shape[1]

    def scores(kb, s_ref):
        start = pl.multiple_of(kb * tk, tk)
        for hd in heads:
            s_ref[hd] = lax.dot_general(k_ref[hd, pl.ds(start, tk), :], q_ref[hd], _NT_DIMS,
                                        preferred_element_type=_F32)

    def update(kb, s_ref, carry, diag_block):
        out = []
        if diag_block is not None:
            visible = (diag_block * tk + lax.broadcasted_iota(jnp.int32, (tk, tq), 0)
                       <= lax.broadcasted_iota(jnp.int32, (tk, tq), 1))
        for hd in heads:
            m, l = carry[hd]
            s = s_ref[hd]
            if diag_block is not None:
                s = jnp.where(visible, s, NEG_INF)
            m_new = jnp.maximum(m, jnp.max(s, axis=0, keepdims=True))
            alpha = jnp.exp2(m - m_new)
            p = jnp.exp2(s - m_new)
            l = alpha * l + jnp.sum(p, axis=0, keepdims=True)
            acc_ref[hd] = acc_ref[hd] * alpha + _dot(vt_ref[hd, kb], p.astype(_BF16))
            out.append((m_new, l))
        return tuple(out)

    s0 = []
    for hd in heads:
        both = lax.dot_general(jnp.concatenate([k_ref[hd, 0:tk, :], km_ref[hd]], axis=0), q_ref[hd], _NT_DIMS,
                               preferred_element_type=_F32)
        sa_ref[hd] = both[0:tk]
        s0.append(both[tk:])
    carry, p0 = [], []
    for hd in heads:
        m = jnp.max(s0[hd], axis=0, keepdims=True)
        p = jnp.exp2(s0[hd] - m)
        carry.append((m, jnp.sum(p, axis=0, keepdims=True)))
        p0.append(p.astype(_BF16))
    for hd in heads:
        acc_ref[hd] = _dot(vmt_ref[hd, 0, :, 0:N_META], p0[hd])
    carry = tuple(carry)

    def pair(kp, carry):
        scores(2 * kp + 1, sb_ref)
        carry = update(2 * kp, sa_ref, carry, None)
        scores(2 * kp + 2, sa_ref)
        return update(2 * kp + 1, sb_ref, carry, None)

    carry = lax.fori_loop(0, j, pair, carry)
    scores(2 * j + 1, sb_ref)
    carry = update(2 * j, sa_ref, carry, 0)
    carry = update(2 * j + 1, sb_ref, carry, 1)
    for hd in heads:
        o_ref[:, hd * vd:(hd + 1) * vd] = (acc_ref[hd] / carry[hd][1]).T.astype(o_ref.dtype)


def _oproj_kernel(o_ref, h_ref, wo_ref, fnorm_ref, wrt_ref, br_ref, triu_ref,
                  h_out_ref, xn_ref, ri_ref, rf_ref, cnt_ref, base_ref, wsplit_ref):
    i = pl.program_id(0)

    @pl.when(i == 0)
    def _():
        _route_init(wrt_ref, wsplit_ref, base_ref)

    h = h_ref[...] + _dot(o_ref[...], wo_ref[...])
    h_out_ref[...] = h
    xn = _rms(h, fnorm_ref[...])
    _store_token_tiles(xn_ref, _pack_rows(xn))
    valid = lax.broadcasted_iota(jnp.int32, (1, h.shape[0]), 1) >= 0
    _route(xn, valid, wsplit_ref, br_ref, triu_ref, base_ref, ri_ref, rf_ref, cnt_ref)


def _full(shape):
    nd = len(shape)
    return pl.BlockSpec(shape, lambda *_: (0,) * nd)


def _router_operands(router_g, router_g_bias, router_e, router_e_bias):
    d = router_g.shape[0]
    wrt = jnp.concatenate([router_g.T, jnp.zeros((8 - N_EXPERT_GROUPS, d), _F32), router_e.T], axis=0)
    br = jnp.concatenate([router_g_bias, jnp.zeros((8 - N_EXPERT_GROUPS,), _F32), router_e_bias])[:, None]
    return wrt.astype(_F32), br.astype(_F32)


def _route_out(n_tiles, tm):
    shapes = [jax.ShapeDtypeStruct((n_tiles, 8, tm), jnp.int32),
              jax.ShapeDtypeStruct((n_tiles, 8, tm), _F32),
              jax.ShapeDtypeStruct((N_EXPERTS, 128), jnp.int32)]
    specs = [pl.BlockSpec((1, 8, tm), lambda i: (i, 0, 0)),
             pl.BlockSpec((1, 8, tm), lambda i: (i, 0, 0)),
             pl.BlockSpec((N_EXPERTS, 128), lambda i: (0, 0))]
    return shapes, specs


def _moe_experts(xn, ri, counts, w_gate, w_up, w_down, layer, n_tiles, n_real_tiles, d):
    tm, tr = TOKEN_TILE, EXPERT_ROWS
    parts = d // (2 * LANES)
    n_valid = n_real_tiles * tm + (n_tiles - n_real_tiles) * N_META
    n_blocks = -(-(2 * n_valid + N_EXPERTS * (tr - 1)) // tr)
    n_blocks = -(-n_blocks // EXPERT_BLOCKS_PER_STEP) * EXPERT_BLOCKS_PER_STEP
    n_rows = n_blocks * tr

    counts = counts[:, 0]
    padded = (counts + tr - 1) // tr * tr
    pends = jnp.cumsum(padded)
    pstarts = pends - padded
    n_used = (pends[-1] // tr).astype(jnp.int32).reshape(1)
    blk0 = jnp.arange(n_blocks, dtype=jnp.int32) * tr
    block_e = jnp.minimum(jnp.sum(blk0[:, None] >= pends[None, :], axis=1), N_EXPERTS - 1).astype(jnp.int32)
    experts = jnp.arange(N_EXPERTS, dtype=jnp.int32)
    block_oh = block_e[:, None] == experts[None, :]
    block_cnt = jnp.sum(jnp.where(block_oh, counts[None, :], 0), axis=1)
    block_start = jnp.sum(jnp.where(block_oh, pstarts[None, :], 0), axis=1)
    block_valid = jnp.clip(block_cnt - (blk0 - block_start), 0, tr).astype(jnp.int32)
    eid = ri[:, 0:2, :]
    slot0 = jnp.sum(jnp.where(eid[..., None] == experts, pstarts, 0), axis=-1)
    dest = (slot0 + ri[:, 2:4, :]).astype(jnp.int32).reshape(n_tiles * 2 * tm)

    pad_start = (pstarts + counts).astype(jnp.int32)
    pad_len = (padded - counts).astype(jnp.int32)
    xs = pl.pallas_call(
        functools.partial(_dispatch_kernel, tm=tm, tr=tr, n_tiles=n_tiles, n_real_tiles=n_real_tiles,
                          n_blocks=n_blocks),
        grid_spec=pltpu.PrefetchScalarGridSpec(
            num_scalar_prefetch=3,
            grid=(n_tiles,),
            in_specs=[pl.BlockSpec(memory_space=pl.ANY), pl.BlockSpec(memory_space=pl.ANY)],
            out_specs=pl.BlockSpec(memory_space=pl.ANY),
            scratch_shapes=[pltpu.SMEM((4 * tm,), jnp.int32), pltpu.VMEM((3, tm * parts, LANES), jnp.uint32),
                            pltpu.VMEM((tr * parts, LANES), jnp.uint32),
                            pltpu.SemaphoreType.DMA((2,)), pltpu.SemaphoreType.DMA((3,)),
                            pltpu.SemaphoreType.DMA((3,)), pltpu.SemaphoreType.DMA]),
        out_shape=jax.ShapeDtypeStruct((n_rows * parts, LANES), jnp.uint32),
        compiler_params=_params(1, has_side_effects=True, disable_bounds_checks=True),
        name="moe_dispatch",
    )(pad_start, pad_len, pends[-1:].astype(jnp.int32), dest, xn)

    f = w_gate.shape[3]
    nb = EXPERT_BLOCKS_PER_STEP
    step_rows = nb * tr * parts

    def expert_block(c):
        return lambda i, be, bv, nu: (layer, be[jnp.minimum(i * nb + c, nu[0] - 1)], 0, 0)

    weight_specs, weight_scratch = [], []
    for c in range(nb):
        weight_specs += [pl.BlockSpec((1, 1, d, f), expert_block(c)), pl.BlockSpec((1, 1, d, f), expert_block(c)),
                         pl.BlockSpec((1, 1, f, d), expert_block(c))]
        weight_scratch += [pltpu.VMEM((d, f), _BF16), pltpu.VMEM((d, f), _BF16), pltpu.VMEM((f, d), _BF16)]
    y = pl.pallas_call(
        _expert_kernel,
        grid_spec=pltpu.PrefetchScalarGridSpec(
            num_scalar_prefetch=3,
            grid=(n_blocks // nb,),
            in_specs=[pl.BlockSpec((step_rows, LANES),
                                   lambda i, be, bv, nu: (jnp.minimum(i, (nu[0] - 1) // nb), 0))] + weight_specs,
            out_specs=pl.BlockSpec((step_rows, LANES), lambda i, be, bv, nu: (i, 0)),
            scratch_shapes=weight_scratch),
        out_shape=jax.ShapeDtypeStruct((n_rows * parts, LANES), jnp.uint32),
        compiler_params=_params(1),
        name="moe_experts",
    )(block_e, block_valid, n_used, xs, *([w_gate, w_up, w_down] * nb))

    return dest, y


def _combine_operands(dest, rf, h, y, d):
    tm = TOKEN_TILE
    parts = d // (2 * LANES)
    in_specs = [pl.BlockSpec(memory_space=pl.ANY),
                pl.BlockSpec((1, 8, tm), lambda i: (i, 0, 0)),
                pl.BlockSpec((tm, d), lambda i: (i, 0)),
                pl.BlockSpec(memory_space=pl.ANY)]
    scratch = [pltpu.SMEM((4 * tm,), jnp.int32),
               pltpu.VMEM((2 * tm * parts, LANES), jnp.uint32), pltpu.VMEM((2 * tm * parts, LANES), jnp.uint32),
               pltpu.SemaphoreType.DMA((2,)), pltpu.SemaphoreType.DMA((2,))]
    return (dest, rf, h, y), in_specs, scratch


def kernel(x, meta_tokens, a_norm, a_w, a_scale, b_norm, b_w_dq, b_q_norm, b_w_uq, b_w_o, kv_norm, w_dkv,
           kv_lat_norm, w_uk, w_uv, ffn_norm, router_g, router_g_bias, router_e, router_e_bias, w_gate, w_up,
           w_down, final_norm):
    bsz, seq, d = x.shape
    tm = TOKEN_TILE
    assert seq % tm == 0 and seq % ATTN_Q_TILE == 0 and ATTN_Q_TILE == 2 * ATTN_K_TILE and tm % ATTN_K_TILE == 0
    kt = tm // ATTN_K_TILE
    assert d % (LANES * len(POOL_WINDOWS)) == 0 and N_META == max(POOL_WINDOWS) and N_META <= tm
    parts = d // (2 * LANES)
    n_tok = bsz * seq
    n_real_tiles = n_tok // tm
    tiles_per_batch = seq // tm
    n_tiles = n_real_tiles + 1
    gd = d // len(POOL_WINDOWS)
    row = lambda v: v.reshape(1, -1).astype(_F32)

    x2 = x.reshape(n_tok, d)
    meta_pad = jnp.concatenate([meta_tokens, jnp.zeros((tm - N_META, d), x.dtype)], axis=0)

    r = np.arange(tm)[:, None]
    cidx = np.arange(tm)[None, :]
    pb = min(POOL_BLOCK, tm)
    pm = np.stack([((r[:pb] - cidx[:, :pb] >= 0) & (r[:pb] - cidx[:, :pb] < w)) for w in POOL_WINDOWS]).astype(np.float32)
    hc = np.arange(N_META)[None, :]
    ph = np.stack([(r[:N_META] + N_META - hc < w) for w in POOL_WINDOWS]).astype(np.float32)
    triu = (r < cidx).astype(np.float32)
    pm, ph, triu = (jnp.asarray(a, dtype=_BF16) for a in (pm, ph, triu))

    wrt0, br0 = _router_operands(router_g[0], router_g_bias[0], router_e[0], router_e_bias[0])
    route_shapes, route_specs = _route_out(n_tiles, tm)
    tile_or_last = lambda i: (jnp.minimum(i, n_real_tiles - 1), 0)
    halo_blocks = tm // N_META
    h1, xn1, ri1, rf1, cnt1 = pl.pallas_call(
        functools.partial(_mixer_kernel, n_real_tiles=n_real_tiles, tiles_per_batch=tiles_per_batch),
        grid=(n_tiles,),
        in_specs=[pl.BlockSpec((tm, d), tile_or_last),
                  pl.BlockSpec((N_META, d), lambda i: (jnp.clip(i * halo_blocks - 1, 0, n_tok // N_META - 1), 0)),
                  _full((tm, d)), _full((N_META, d)), _full((1, d)),
                  _full((len(POOL_WINDOWS), gd, gd)), _full((1, d)), _full((1, d)),
                  _full((ROUTER_ROWS, d)), _full((ROUTER_ROWS, 1)),
                  _full(pm.shape), _full(ph.shape), _full((tm, tm))],
        out_specs=[pl.BlockSpec((tm, d), lambda i: (i, 0)),
                   pl.BlockSpec((tm * parts, LANES), lambda i: (i, 0))] + route_specs,
        out_shape=[jax.ShapeDtypeStruct((n_tiles * tm, d), _F32),
                   jax.ShapeDtypeStruct((n_tiles * tm * parts, LANES), jnp.uint32)] + route_shapes,
        scratch_shapes=[pltpu.VMEM((N_EXPERTS, 1), _F32), pltpu.VMEM((ROUTER_SPLIT_ROWS, d), _BF16)],
        compiler_params=_params(1),
        name="pool_mixer_router",
    )(x2, x2, meta_pad, meta_tokens, row(a_norm[0]), a_w[0].astype(_BF16), row(a_scale[0]), row(ffn_norm[0]),
      wrt0, br0, pm, ph, triu)

    dest1, y1 = _moe_experts(xn1, ri1, cnt1, w_gate, w_up, w_down, 0, n_tiles, n_real_tiles, d)

    pos = jnp.concatenate([jnp.arange(seq, dtype=_F32) + N_META, jnp.arange(tm, dtype=_F32)])
    inv_freq = ROPE_THETA ** (-jnp.arange(0, QK_ROPE_DIM, 2, dtype=_F32) / QK_ROPE_DIM)
    ang = pos[:, None] * inv_freq[None, :]
    cos_t = jnp.tile(jnp.cos(ang), (1, 4))
    sin_t = jnp.tile(jnp.concatenate([-jnp.sin(ang), jnp.sin(ang)], axis=1), (1, 2))

    wdkv = jnp.concatenate([w_dkv, w_dkv[:, KV_LORA_RANK:]], axis=1).astype(_BF16)
    wuk = w_uk.reshape(KV_LORA_RANK, N_HEADS * QK_NOPE_DIM).astype(_BF16)
    wuv = w_uv.reshape(KV_LORA_RANK, N_HEADS * V_HEAD_DIM).T.astype(_BF16)
    wuq = b_w_uq[0]
    q_rank = wuq.shape[0]
    wuq = jnp.concatenate([wuq[:, :, :QK_NOPE_DIM].reshape(q_rank, -1),
                           wuq[:, :, QK_NOPE_DIM:].reshape(q_rank, -1)], axis=1).astype(_BF16)
    wdq = b_w_dq[0].astype(_BF16)
    pos_tile = lambda i: (jnp.where(i < n_real_tiles, i % tiles_per_batch, tiles_per_batch), 0)
    head_tile = lambda i: (0, i, 0)
    operands, in_specs, scratch = _combine_operands(dest1, rf1, h1, y1, d)
    h2, q, k, v = pl.pallas_call(
        functools.partial(_combine_proj_kernel, tm=tm, n_tiles=n_tiles),
        grid=(n_tiles,),
        in_specs=in_specs + [
            pl.BlockSpec((tm, 128), pos_tile), pl.BlockSpec((tm, 128), pos_tile),
            _full((1, d)), _full(wdkv.shape), _full((1, KV_LORA_RANK)), _full(wuk.shape), _full(wuv.shape),
            _full((1, d)), _full(wdq.shape), _full((1, q_rank)), _full(wuq.shape)],
        out_specs=[pl.BlockSpec((tm, d), lambda i: (i, 0)),
                   pl.BlockSpec((N_HEADS, tm, QK_DIM), head_tile),
                   pl.BlockSpec((N_HEADS, tm, QK_DIM), head_tile),
                   pl.BlockSpec((N_HEADS, kt, V_HEAD_DIM, ATTN_K_TILE), lambda i: (0, i, 0, 0))],
        out_shape=[jax.ShapeDtypeStruct((n_tiles * tm, d), _F32),
                   jax.ShapeDtypeStruct((N_HEADS, n_tiles * tm, QK_DIM), _BF16),
                   jax.ShapeDtypeStruct((N_HEADS, n_tiles * tm, QK_DIM), _BF16),
                   jax.ShapeDtypeStruct((N_HEADS, n_tiles * kt, V_HEAD_DIM, ATTN_K_TILE), _BF16)],
        scratch_shapes=scratch,
        compiler_params=_params(1, disable_bounds_checks=True),
        name="moe_combine_latent_qkv",
    )(*operands, cos_t, sin_t, row(kv_norm), wdkv, row(kv_lat_norm), wuk, wuv,
      row(b_norm[0]), wdq, row(b_q_norm[0]), wuq)

    tq, tk, hp = ATTN_Q_TILE, ATTN_K_TILE, ATTN_HEADS_PER_STEP
    n_q = seq // tq
    meta_block = n_tok // N_META
    o = pl.pallas_call(
        functools.partial(_attn_kernel, tq=tq, tk=tk),
        grid=(bsz, N_HEADS // hp, n_q),
        in_specs=[pl.BlockSpec((hp, tq, QK_DIM), lambda b, hg, j: (hg, b * n_q + j, 0)),
                  pl.BlockSpec((hp, seq, QK_DIM), lambda b, hg, j: (hg, b, 0)),
                  pl.BlockSpec((hp, seq // tk, V_HEAD_DIM, tk), lambda b, hg, j: (hg, b, 0, 0)),
                  pl.BlockSpec((hp, N_META, QK_DIM), lambda b, hg, j: (hg, meta_block, 0)),
                  pl.BlockSpec((hp, 1, V_HEAD_DIM, tk), lambda b, hg, j: (hg, n_real_tiles * kt, 0, 0))],
        out_specs=pl.BlockSpec((tq, hp * V_HEAD_DIM), lambda b, hg, j: (b * n_q + j, hg)),
        out_shape=jax.ShapeDtypeStruct((n_tok, N_HEADS * V_HEAD_DIM), _BF16),
        scratch_shapes=[pltpu.VMEM((hp, tk, tq), _F32), pltpu.VMEM((hp, tk, tq), _F32),
                        pltpu.VMEM((hp, V_HEAD_DIM, tq), _F32)],
        compiler_params=_params(3),
        name="causal_attention",
    )(q, k, v, k, v)

    wrt1, br1 = _router_operands(router_g[1], router_g_bias[1], router_e[1], router_e_bias[1])
    route_shapes, route_specs = _route_out(n_real_tiles, tm)
    h3, xn2, ri2, rf2, cnt2 = pl.pallas_call(
        _oproj_kernel,
        grid=(n_real_tiles,),
        in_specs=[pl.BlockSpec((tm, N_HEADS * V_HEAD_DIM), lambda i: (i, 0)),
                  pl.BlockSpec((tm, d), lambda i: (i, 0)),
                  _full((N_HEADS * V_HEAD_DIM, d)), _full((1, d)),
                  _full((ROUTER_ROWS, d)), _full((ROUTER_ROWS, 1)), _full((tm, tm))],
        out_specs=[pl.BlockSpec((tm, d), lambda i: (i, 0)),
                   pl.BlockSpec((tm * parts, LANES), lambda i: (i, 0))] + route_specs,
        out_shape=[jax.ShapeDtypeStruct((n_tok, d), _F32),
                   jax.ShapeDtypeStruct((n_tok * parts, LANES), jnp.uint32)] + route_shapes,
        scratch_shapes=[pltpu.VMEM((N_EXPERTS, 1), _F32), pltpu.VMEM((ROUTER_SPLIT_ROWS, d), _BF16)],
        compiler_params=_params(1),
        name="attn_out_router",
    )(o, h2, b_w_o[0].astype(_BF16), row(ffn_norm[1]), wrt1, br1, triu)

    dest2, y2 = _moe_experts(xn2, ri2, cnt2, w_gate, w_up, w_down, 1, n_real_tiles, n_real_tiles, d)
    operands, in_specs, scratch = _combine_operands(dest2, rf2, h3, y2, d)
    out = pl.pallas_call(
        functools.partial(_combine_norm_kernel, tm=tm, n_tiles=n_real_tiles),
        grid=(n_real_tiles,),
        in_specs=in_specs + [_full((1, d))],
        out_specs=pl.BlockSpec((tm, d), lambda i: (i, 0)),
        out_shape=jax.ShapeDtypeStruct((n_tok, d), _F32),
        scratch_shapes=scratch,
        compiler_params=_params(1, disable_bounds_checks=True),
        name="moe_combine_final_norm",
    )(*operands, row(final_norm))
    return out.reshape(bsz, seq, d)
```

```python
import functools

import numpy as np
import jax
import jax.numpy as jnp
from jax import lax
from jax.experimental import pallas as pl
from jax.experimental.pallas import tpu as pltpu

N_META = 16
POOL_WINDOWS = (2, 4, 8, 16)
N_HEADS = 8
QK_NOPE_DIM = 128
QK_ROPE_DIM = 64
QK_DIM = QK_NOPE_DIM + QK_ROPE_DIM
V_HEAD_DIM = 128
KV_LORA_RANK = 256
ROPE_THETA = 10000.0
ATTN_SCALE = QK_DIM ** -0.5
Q_SCALE = ATTN_SCALE * 1.4426950408889634
N_EXPERT_GROUPS = 4
EXPERTS_PER_GROUP = 8
N_EXPERTS = N_EXPERT_GROUPS * EXPERTS_PER_GROUP
RMS_EPS = 1e-6
NEG_INF = -1e30

TOKEN_TILE = 512
POOL_BLOCK = 256
EXPERT_ROWS = 256
EXPERT_BLOCKS_PER_STEP = 2
ATTN_Q_TILE = 512
ATTN_K_TILE = ATTN_Q_TILE // 2
ATTN_HEADS_PER_STEP = 4
DMA_UNROLL = 8
DMA_THREADS = 2
ROUTER_ROWS = 8 + N_EXPERTS
ROUTER_LO_ROW = 48
ROUTER_SPLIT_ROWS = 2 * ROUTER_LO_ROW
VMEM_LIMIT_BYTES = 48 * 1024 * 1024

LANES = 128
_F32 = jnp.float32
_BF16 = jnp.bfloat16
_NT_DIMS = (((1,), (1,)), ((), ()))


def _params(n_grid_dims=1, **kw):
    return pltpu.CompilerParams(dimension_semantics=("arbitrary",) * n_grid_dims,
                                vmem_limit_bytes=VMEM_LIMIT_BYTES, **kw)


def _rms(x, g):
    ms = jnp.mean(x * x, axis=-1, keepdims=True)
    return x * lax.rsqrt(ms + RMS_EPS) * g


def _split_bf16(x):
    hi = x.astype(_BF16)
    lo = (x - hi.astype(_F32)).astype(_BF16)
    return hi, lo


def _dot(a, b):
    return jnp.dot(a, b, preferred_element_type=_F32)


def _store_token_tiles(ref, x):
    n, d = x.shape
    parts = d // LANES
    for s in range(parts):
        ref[pl.ds(s, n, stride=parts), :] = x[:, s * LANES:(s + 1) * LANES]


def _load_token_tiles(ref, first_token, n, parts):
    return jnp.concatenate([ref[pl.ds(first_token * parts + s, n, stride=parts), :] for s in range(parts)], axis=1)


_HIGH_HALF = 0xFFFF0000


def _pack_rows(x):
    n, d = x.shape
    words = []
    for s in range(d // (2 * LANES)):
        lo = x[:, 2 * s * LANES:(2 * s + 1) * LANES].astype(_BF16).astype(_F32)
        hi = x[:, (2 * s + 1) * LANES:(2 * s + 2) * LANES].astype(_BF16).astype(_F32)
        words.append(lax.shift_right_logical(lax.bitcast_convert_type(lo, jnp.uint32), jnp.uint32(16))
                     | (lax.bitcast_convert_type(hi, jnp.uint32) & jnp.uint32(_HIGH_HALF)))
    return jnp.concatenate(words, axis=1)


def _unpack_rows(w):
    cols = []
    for s in range(w.shape[1] // LANES):
        ws = w[:, s * LANES:(s + 1) * LANES]
        cols.append(lax.bitcast_convert_type(lax.shift_left(ws, jnp.uint32(16)), _F32))
        cols.append(lax.bitcast_convert_type(ws & jnp.uint32(_HIGH_HALF), _F32))
    return jnp.concatenate(cols, axis=1)


def _route_init(wrt_ref, wsplit_ref, base_ref):
    base_ref[...] = jnp.zeros(base_ref.shape, base_ref.dtype)
    w_hi, w_lo = _split_bf16(wrt_ref[...])
    wsplit_ref[...] = jnp.zeros(wsplit_ref.shape, wsplit_ref.dtype)
    wsplit_ref[0:ROUTER_ROWS, :] = w_hi
    wsplit_ref[ROUTER_LO_ROW:ROUTER_LO_ROW + ROUTER_ROWS, :] = w_lo


def _route(xn, valid, wsplit_ref, br_ref, triu_ref, base_ref, ri_ref, rf_ref, cnt_ref):
    tm = xn.shape[0]
    x_hi, x_lo = _split_bf16(xn)
    by_hi = lax.dot_general(wsplit_ref[...], x_hi, _NT_DIMS, preferred_element_type=_F32)
    by_lo = lax.dot_general(wsplit_ref[0:ROUTER_LO_ROW, :], x_lo, _NT_DIMS, preferred_element_type=_F32)
    logits = (by_hi[0:ROUTER_ROWS] + by_hi[ROUTER_LO_ROW:ROUTER_LO_ROW + ROUTER_ROWS]
              + by_lo[0:ROUTER_ROWS] + br_ref[...])
    lg = logits[0:N_EXPERT_GROUPS]
    eg = jnp.exp(lg - jnp.max(lg, axis=0, keepdims=True))
    pg = eg / jnp.sum(eg, axis=0, keepdims=True)
    w_g = jnp.max(pg, axis=0, keepdims=True)
    ig = lax.broadcasted_iota(jnp.int32, pg.shape, 0).astype(_F32)
    g_sel = jnp.min(jnp.where(pg == w_g, ig, float(N_EXPERT_GROUPS)), axis=0, keepdims=True)

    sel = logits[8:8 + EXPERTS_PER_GROUP]
    for g in range(1, N_EXPERT_GROUPS):
        sel = jnp.where(g_sel == float(g), logits[8 + g * EXPERTS_PER_GROUP:8 + (g + 1) * EXPERTS_PER_GROUP], sel)
    ie = lax.broadcasted_iota(jnp.int32, sel.shape, 0).astype(_F32)
    v1 = jnp.max(sel, axis=0, keepdims=True)
    i1 = jnp.min(jnp.where(sel == v1, ie, float(EXPERTS_PER_GROUP)), axis=0, keepdims=True)
    rest = jnp.where(ie == i1, -jnp.inf, sel)
    v2 = jnp.max(rest, axis=0, keepdims=True)
    i2 = jnp.min(jnp.where(rest == v2, ie, float(EXPERTS_PER_GROUP)), axis=0, keepdims=True)
    e2 = jnp.exp(v2 - v1)
    den = 1.0 + e2
    validf = valid.astype(_F32)
    gate0 = w_g * (1.0 / den) * validf
    gate1 = w_g * (e2 / den) * validf
    eid0 = g_sel * float(EXPERTS_PER_GROUP) + i1
    eid1 = g_sel * float(EXPERTS_PER_GROUP) + i2

    iall = lax.broadcasted_iota(jnp.int32, (N_EXPERTS, tm), 0).astype(_F32)
    oh0 = jnp.where(iall == eid0, validf, 0.0)
    oh1 = jnp.where(iall == eid1, validf, 0.0)
    both = oh0 + oh1
    before = _dot(both.astype(_BF16), triu_ref[...]) + base_ref[...]
    rank0 = jnp.sum(oh0 * before, axis=0, keepdims=True)
    rank1 = jnp.sum(oh1 * before, axis=0, keepdims=True)
    base_ref[...] = base_ref[...] + jnp.sum(both, axis=1, keepdims=True)

    ri_ref[...] = jnp.zeros(ri_ref.shape, ri_ref.dtype)
    rf_ref[...] = jnp.zeros(rf_ref.shape, rf_ref.dtype)
    ri_ref[0, 0:1, :] = eid0.astype(jnp.int32)
    ri_ref[0, 1:2, :] = eid1.astype(jnp.int32)
    ri_ref[0, 2:3, :] = rank0.astype(jnp.int32)
    ri_ref[0, 3:4, :] = rank1.astype(jnp.int32)
    rf_ref[0, 0:1, :] = gate0
    rf_ref[0, 1:2, :] = gate1
    cnt_ref[...] = jnp.broadcast_to(base_ref[...], cnt_ref.shape).astype(jnp.int32)


def _mixer_kernel(x_ref, xh_ref, mp_ref, meta_ref, anorm_ref, aw_ref, ascale_ref, fnorm_ref,
                  wrt_ref, br_ref, pm_ref, ph_ref, triu_ref,
                  h_ref, xn_ref, ri_ref, rf_ref, cnt_ref, base_ref, wsplit_ref, *, n_real_tiles, tiles_per_batch):
    i = pl.program_id(0)
    tm = x_ref.shape[0]
    gd = x_ref.shape[1] // len(POOL_WINDOWS)
    is_meta = i == n_real_tiles
    first = (i % tiles_per_batch) == 0

    @pl.when(i == 0)
    def _():
        _route_init(wrt_ref, wsplit_ref, base_ref)

    h = jnp.where(is_meta, mp_ref[...], x_ref[...])
    halo = jnp.where(is_meta, 0.0, jnp.where(first, meta_ref[...], xh_ref[...]))
    hn = _rms(h, anorm_ref[...])
    hh = _rms(halo, anorm_ref[...])
    hn_hi, hn_lo = _split_bf16(hn)
    hh_hi, hh_lo = _split_bf16(hh)
    pb = pm_ref.shape[1]
    wins = {}
    for c in range(tm // pb):
        rows = slice(c * pb, (c + 1) * pb)
        for g in range(len(POOL_WINDOWS)):
            sl = slice(g * gd, (g + 1) * gd)
            win = _dot(pm_ref[g], hn_hi[rows, sl]) + _dot(pm_ref[g], hn_lo[rows, sl])
            if c == 0:
                before_hi, before_lo = hh_hi[:, sl], hh_lo[:, sl]
            else:
                before_hi, before_lo = hn_hi[c * pb - N_META:c * pb, sl], hn_lo[c * pb - N_META:c * pb, sl]
            top = win[:N_META] + _dot(ph_ref[g], before_hi) + _dot(ph_ref[g], before_lo)
            wins[c, g] = jnp.concatenate([top, win[N_META:]], axis=0)
    for c in range(tm // pb):
        rows = slice(c * pb, (c + 1) * pb)
        row = c * pb + lax.broadcasted_iota(jnp.int32, (pb, 1), 0)
        for g, w in enumerate(POOL_WINDOWS):
            sl = slice(g * gd, (g + 1) * gd)
            cnt = jnp.where(is_meta, jnp.minimum(row + 1, w), w).astype(_F32)
            pooled = wins[c, g] * (1.0 / cnt) - hn[rows, sl]
            mix = _dot(pooled.astype(_BF16), aw_ref[g])
            h_ref[rows, sl] = h[rows, sl] + mix * ascale_ref[:, sl]

    xn = _rms(h_ref[...], fnorm_ref[...])
    _store_token_tiles(xn_ref, _pack_rows(xn))
    lane = lax.broadcasted_iota(jnp.int32, (1, tm), 1)
    valid = jnp.logical_or(jnp.logical_not(is_meta), lane < N_META)
    _route(xn, valid, wsplit_ref, br_ref, triu_ref, base_ref, ri_ref, rf_ref, cnt_ref)


def _index_copy(dest_hbm, idx_smem, isem, tile, n):
    return pltpu.make_async_copy(dest_hbm.at[pl.ds(pl.multiple_of(tile * n, n), n)],
                                 idx_smem.at[pl.ds(pl.multiple_of((tile % 2) * n, n), n)], isem.at[tile % 2])


def _dispatch_kernel(pad_start_ref, pad_len_ref, tail_ref, dest_hbm, xn_hbm, xs_hbm,
                     idx_smem, xbuf, zbuf, isem, lsem, ssem, zsem, *, tm, tr, n_tiles, n_real_tiles, n_blocks):
    i = pl.program_id(0)
    parts = xbuf.shape[1] // tm
    buf = i % 3

    def zero_fill(wait):
        def zero_rows(first_row, n_rows):
            cp = pltpu.make_async_copy(
                zbuf.at[pl.ds(0, n_rows * parts), :],
                xs_hbm.at[pl.ds(pl.multiple_of(first_row * parts, parts), n_rows * parts), :], zsem)
            if wait:
                cp.wait()
            else:
                cp.start()

        def per_expert(e, c):
            first, length = pad_start_ref[e], pad_len_ref[e]
            for bit in range((tr - 1).bit_length()):
                @pl.when((length >> bit) % 2 == 1)
                def _(bit=bit):
                    zero_rows(first + ((length >> (bit + 1)) << (bit + 1)), 1 << bit)
            return c

        def per_block(blk, c):
            zero_rows(blk * tr, tr)
            return c

        lax.fori_loop(0, N_EXPERTS, per_expert, 0)
        lax.fori_loop(tail_ref[0] // tr, n_blocks, per_block, 0)

    def idx_copy(tile):
        return _index_copy(dest_hbm, idx_smem, isem, tile, 2 * tm)

    def load(tile):
        rows = tm * parts
        return pltpu.make_async_copy(xn_hbm.at[pl.ds(pl.multiple_of(tile * rows, rows), rows), :],
                                     xbuf.at[tile % 3], lsem.at[tile % 3])

    def wait_scatters(tile):
        def wait_rows(n_tok):
            for k in range(2):
                pltpu.make_async_copy(xbuf.at[tile % 3, pl.ds(0, n_tok * parts), :],
                                      xs_hbm.at[pl.ds(0, n_tok * parts), :], ssem.at[tile % 3]).wait()

        @pl.when(tile < n_real_tiles)
        def _():
            wait_rows(tm)

        @pl.when(tile >= n_real_tiles)
        def _():
            wait_rows(N_META)

    def scatter(n_tok):
        def issue(g, c):
            for u in range(min(DMA_UNROLL, n_tok)):
                t = g * min(DMA_UNROLL, n_tok) + u
                src = xbuf.at[buf, pl.ds(pl.multiple_of(t * parts, parts), parts), :]
                for k in range(2):
                    d = idx_smem[(i % 2) * (2 * tm) + k * tm + t]
                    pltpu.make_async_copy(src, xs_hbm.at[pl.ds(pl.multiple_of(d * parts, parts), parts), :],
                                          ssem.at[buf]).start(priority=(2 * u + k) % DMA_THREADS)
            return c

        lax.fori_loop(0, n_tok // min(DMA_UNROLL, n_tok), issue, 0)

    @pl.when(i == 0)
    def _():
        idx_copy(0).start()
        load(0).start()
        zbuf[...] = jnp.zeros(zbuf.shape, zbuf.dtype)
        zero_fill(wait=False)

    @pl.when(i >= 2)
    def _():
        wait_scatters(i - 2)

    @pl.when(i + 1 < n_tiles)
    def _():
        idx_copy(i + 1).start()
        load(i + 1).start()

    idx_copy(i).wait()
    load(i).wait()

    @pl.when(i < n_real_tiles)
    def _():
        scatter(tm)

    @pl.when(i >= n_real_tiles)
    def _():
        scatter(N_META)

    @pl.when(i == n_tiles - 1)
    def _():
        if n_tiles >= 2:
            wait_scatters(i - 1)
        wait_scatters(i)
        zero_fill(wait=True)


def _expert_kernel(be_ref, bv_ref, nu_ref, xs_ref, *refs):
    nb = EXPERT_BLOCKS_PER_STEP
    w_refs, y_ref, wb_refs = refs[:3 * nb], refs[3 * nb], refs[3 * nb + 1:]
    i = pl.program_id(0)
    n_used = nu_ref[0]
    parts = wb_refs[0].shape[0] // (2 * LANES)
    tr = xs_ref.shape[0] // (nb * parts)

    for c in range(nb):
        blk = i * nb + c
        changed = be_ref[blk] != be_ref[jnp.maximum(blk - nb, 0)]

        @pl.when(jnp.logical_or(i == 0, jnp.logical_and(blk < n_used, changed)))
        def _(c=c):
            for w_ref, wb_ref in zip(w_refs[3 * c:3 * c + 3], wb_refs[3 * c:3 * c + 3]):
                wb_ref[...] = w_ref[0, 0].astype(_BF16)

    @pl.when(i * nb < n_used)
    def _():
        rows = lax.broadcasted_iota(jnp.int32, (tr, 1), 0)
        xs, gs, us, ys = [], [], [], []
        for c in range(nb):
            x = jnp.where(rows < bv_ref[i * nb + c], _unpack_rows(_load_token_tiles(xs_ref, c * tr, tr, parts)), 0.0)
            xs.append(x.astype(_BF16))
        for c in range(nb):
            gs.append(_dot(xs[c], wb_refs[3 * c][...]))
            us.append(_dot(xs[c], wb_refs[3 * c + 1][...]))
        for c in range(nb):
            a = gs[c] * (1.0 / (1.0 + jnp.exp(-gs[c]))) * us[c]
            ys.append(_dot(a.astype(_BF16), wb_refs[3 * c + 2][...]))
        for c in range(nb):
            y = jnp.where(i * nb + c < n_used, ys[c], 0.0)
            _store_token_tiles(y_ref.at[pl.ds(c * tr * parts, tr * parts), :], _pack_rows(y))

    @pl.when(i * nb >= n_used)
    def _():
        y_ref[...] = jnp.zeros(y_ref.shape, y_ref.dtype)


def _combined_tile(dest_hbm, rf_ref, h_ref, y_hbm, idx_smem, ybufs, isem, rsem, *, tm, n_tiles, slot):
    i = pl.program_id(0)
    parts = ybufs[0].shape[0] // (2 * tm)

    def idx_copy(tile):
        return _index_copy(dest_hbm, idx_smem, isem, tile, 2 * tm)

    def row_copy(idx_slot, buf_slot, t, k, u):
        d = idx_smem[idx_slot * (2 * tm) + k * tm + t]
        row = (k * tm + t) * parts
        if not isinstance(t, int):
            row = pl.multiple_of(row, parts)
        return pltpu.make_async_copy(
            y_hbm.at[pl.ds(pl.multiple_of(d * parts, parts), parts), :],
            ybufs[buf_slot].at[pl.ds(row, parts), :],
            rsem.at[buf_slot]).start(priority=(2 * u + k) % DMA_THREADS)

    def wait_rows(buf_slot):
        pltpu.make_async_copy(y_hbm.at[pl.ds(0, 2 * tm * parts), :], ybufs[buf_slot], rsem.at[buf_slot]).wait()

    if slot == 0:
        @pl.when(i == 0)
        def _():
            idx_copy(0).start()
            idx_copy(0).wait()

            def issue(g, c):
                for u in range(DMA_UNROLL):
                    for k in range(2):
                        row_copy(0, 0, g * DMA_UNROLL + u, k, u)
                return c

            lax.fori_loop(0, tm // DMA_UNROLL, issue, 0)
            if n_tiles >= 2:
                idx_copy(1).start()

    @pl.when(i + 1 < n_tiles)
    def _():
        idx_copy(i + 1).wait()

    @pl.when(i + 2 < n_tiles)
    def _():
        idx_copy(i + 2).start()

    nxt_idx_slot = jnp.minimum(i + 1, n_tiles - 1) % 2

    def prefetch(part, n_parts):
        for t in range(part * tm // n_parts, (part + 1) * tm // n_parts):
            for k in range(2):
                row_copy(nxt_idx_slot, 1 - slot, t, k, t)

    gates = rf_ref[0]
    gt = jnp.concatenate([gates, jnp.zeros((LANES - gates.shape[0], tm), _F32)], axis=0).T
    wait_rows(slot)
    yb = ybufs[slot]

    def combined(r0, n):
        return h_ref[r0:r0 + n, :] + (_unpack_rows(_load_token_tiles(yb, r0, n, parts)) * gt[r0:r0 + n, 0:1]
                                      + _unpack_rows(_load_token_tiles(yb, tm + r0, n, parts)) * gt[r0:r0 + n, 1:2])

    def drain():
        @pl.when(i == n_tiles - 1)
        def _():
            wait_rows(1 - slot)

    return combined, prefetch, drain


def _for_each_parity(body):
    for slot in range(2):
        @pl.when(pl.program_id(0) % 2 == slot)
        def _(slot=slot):
            body(slot)


def _combine_norm_kernel(dest_hbm, rf_ref, h_ref, y_hbm, fnorm_ref, out_ref,
                         idx_smem, ybuf0, ybuf1, isem, rsem, *, tm, n_tiles):
    def body(slot):
        combined, prefetch, drain = _combined_tile(dest_hbm, rf_ref, h_ref, y_hbm, idx_smem, (ybuf0, ybuf1),
                                                   isem, rsem, tm=tm, n_tiles=n_tiles, slot=slot)
        n_chunks = 4
        n = tm // n_chunks
        for c in range(n_chunks):
            prefetch(c, n_chunks)
            out_ref[c * n:(c + 1) * n, :] = _rms(combined(c * n, n), fnorm_ref[...])
        drain()

    _for_each_parity(body)


def _combine_proj_kernel(dest_hbm, rf_ref, h_ref, y_hbm, cos_ref, sin_ref, kvn_ref, wdkv_ref, kvlat_ref,
                         wuk_ref, wuv_ref, bnorm_ref, wdq_ref, qnorm_ref, wuq_ref,
                         h_out_ref, q_ref, k_ref, v_ref, idx_smem, ybuf0, ybuf1, isem, rsem, *, tm, n_tiles):
    def body(slot):
        combined, prefetch, drain = _combined_tile(dest_hbm, rf_ref, h_ref, y_hbm, idx_smem, (ybuf0, ybuf1),
                                                   isem, rsem, tm=tm, n_tiles=n_tiles, slot=slot)
        n = v_ref.shape[3]
        n_chunks = tm // n
        hs = []
        for c in range(n_chunks):
            prefetch(c, n_chunks)
            hs.append(combined(c * n, n))
            h_out_ref[c * n:(c + 1) * n, :] = hs[c]
        _project(hs, cos_ref, sin_ref, kvn_ref, wdkv_ref, kvlat_ref, wuk_ref, wuv_ref,
                 bnorm_ref, wdq_ref, qnorm_ref, wuq_ref, q_ref, k_ref, v_ref)
        drain()

    _for_each_parity(body)


def _rope128(x, cos_t, sin_t):
    lane = lax.broadcasted_iota(jnp.int32, (1, 128), 1)
    first_half = (lane % QK_ROPE_DIM) < (QK_ROPE_DIM // 2)
    swapped = jnp.where(first_half, pltpu.roll(x, 128 - QK_ROPE_DIM // 2, axis=1),
                        pltpu.roll(x, QK_ROPE_DIM // 2, axis=1))
    return x * cos_t + swapped * sin_t


def _project(hs, cos_ref, sin_ref, kvn_ref, wdkv_ref, kvlat_ref, wuk_ref, wuv_ref,
             bnorm_ref, wdq_ref, qnorm_ref, wuq_ref, q_ref, k_ref, v_ref):
    n = hs[0].shape[0]
    chunks = range(len(hs))
    c_kv = [_dot(_rms(h, kvn_ref[...]).astype(_BF16), wdkv_ref[...]) for h in hs]
    c_q = [_dot(_rms(h, bnorm_ref[...]).astype(_BF16), wdq_ref[...]) for h in hs]
    ckv = [_rms(c[:, :KV_LORA_RANK], kvlat_ref[...]).astype(_BF16) for c in c_kv]
    cq = [(_rms(c, qnorm_ref[...]) * Q_SCALE).astype(_BF16) for c in c_q]
    kn = [_dot(ckv[c], wuk_ref[...]) for c in chunks]
    vt = [lax.dot_general(wuv_ref[...], ckv[c], _NT_DIMS, preferred_element_type=_F32) for c in chunks]
    q = [_dot(cq[c], wuq_ref[...]) for c in chunks]
    rope0 = N_HEADS * QK_NOPE_DIM
    for c in chunks:
        rows = slice(c * n, (c + 1) * n)
        cos_t = cos_ref[rows, :]
        sin_t = sin_ref[rows, :]
        kr = _rope128(c_kv[c][:, KV_LORA_RANK:KV_LORA_RANK + 128], cos_t, sin_t)[:, :QK_ROPE_DIM].astype(_BF16)
        for hd in range(N_HEADS):
            k_ref[hd, rows, 0:QK_NOPE_DIM] = kn[c][:, hd * QK_NOPE_DIM:(hd + 1) * QK_NOPE_DIM].astype(_BF16)
            k_ref[hd, rows, QK_NOPE_DIM:QK_DIM] = kr
            v_ref[hd, c] = vt[c][hd * V_HEAD_DIM:(hd + 1) * V_HEAD_DIM, :].astype(_BF16)
            q_ref[hd, rows, 0:QK_NOPE_DIM] = q[c][:, hd * QK_NOPE_DIM:(hd + 1) * QK_NOPE_DIM].astype(_BF16)
        for pair in range(N_HEADS // 2):
            qr = _rope128(q[c][:, rope0 + pair * LANES:rope0 + (pair + 1) * LANES], cos_t, sin_t)
            q_ref[2 * pair, rows, QK_NOPE_DIM:QK_DIM] = qr[:, :QK_ROPE_DIM].astype(_BF16)
            q_ref[2 * pair + 1, rows, QK_NOPE_DIM:QK_DIM] = qr[:, QK_ROPE_DIM:].astype(_BF16)


def _attn_kernel(q_ref, k_ref, vt_ref, km_ref, vmt_ref, o_ref, sa_ref, sb_ref, acc_ref, *, tq, tk):
    j = pl.program_id(2)
    heads = range(q_ref.shape[0])
    vd = acc_ref.shape[1]

    def scores(kb, s_ref):
        start = pl.multiple_of(kb * tk, tk)
        for hd in heads:
            s_ref[hd] = lax.dot_general(k_ref[hd, pl.ds(start, tk), :], q_ref[hd], _NT_DIMS,
                                        preferred_element_type=_F32)

    def update(kb, s_ref, carry, diag_block):
        out = []
        if diag_block is not None:
            visible = (diag_block * tk + lax.broadcasted_iota(jnp.int32, (tk, tq), 0)
                       <= lax.broadcasted_iota(jnp.int32, (tk, tq), 1))
        for hd in heads:
            m, l = carry[hd]
            s = s_ref[hd]
            if diag_block is not None:
                s = jnp.where(visible, s, NEG_INF)
            m_new = jnp.maximum(m, jnp.max(s, axis=0, keepdims=True))
            alpha = jnp.exp2(m - m_new)
            p = jnp.exp2(s - m_new)
            l = alpha * l + jnp.sum(p, axis=0, keepdims=True)
            acc_ref[hd] = acc_ref[hd] * alpha + _dot(vt_ref[hd, kb], p.astype(_BF16))
            out.append((m_new, l))
        return tuple(out)

    s0 = []
    for hd in heads:
        both = lax.dot_general(jnp.concatenate([k_ref[hd, 0:tk, :], km_ref[hd]], axis=0), q_ref[hd], _NT_DIMS,
                               preferred_element_type=_F32)
        sa_ref[hd] = both[0:tk]
        s0.append(both[tk:])
    carry, p0 = [], []
    for hd in heads:
        m = jnp.max(s0[hd], axis=0, keepdims=True)
        p = jnp.exp2(s0[hd] - m)
        carry.append((m, jnp.sum(p, axis=0, keepdims=True)))
        p0.append(p.astype(_BF16))
    for hd in heads:
        acc_ref[hd] = _dot(vmt_ref[hd, 0, :, 0:N_META], p0[hd])
    carry = tuple(carry)

    def pair(kp, carry):
        scores(2 * kp + 1, sb_ref)
        carry = update(2 * kp, sa_ref, carry, None)
        scores(2 * kp + 2, sa_ref)
        return update(2 * kp + 1, sb_ref, carry, None)

    carry = lax.fori_loop(0, j, pair, carry)
    last = 2 * j + 1
    start = pl.multiple_of(last * tk, tk)
    for hd in heads:
        sb_ref[hd, :, 0:tk] = lax.dot_general(k_ref[hd, pl.ds(start, tk), :], q_ref[hd, tk:tq, :], _NT_DIMS,
                                              preferred_element_type=_F32)
    carry = update(2 * j, sa_ref, carry, 0)
    visible = lax.broadcasted_iota(jnp.int32, (tk, tk), 0) <= lax.broadcasted_iota(jnp.int32, (tk, tk), 1)
    for hd in heads:
        m, l = carry[hd]
        s = jnp.where(visible, sb_ref[hd, :, 0:tk], NEG_INF)
        m_new = jnp.maximum(m[:, tk:], jnp.max(s, axis=0, keepdims=True))
        alpha = jnp.exp2(m[:, tk:] - m_new)
        p = jnp.exp2(s - m_new)
        l = jnp.concatenate([l[:, :tk], alpha * l[:, tk:] + jnp.sum(p, axis=0, keepdims=True)], axis=1)
        acc_ref[hd, :, tk:tq] = acc_ref[hd, :, tk:tq] * alpha + _dot(vt_ref[hd, last], p.astype(_BF16))
        o_ref[:, hd * vd:(hd + 1) * vd] = (acc_ref[hd] / l).T.astype(o_ref.dtype)


def _oproj_kernel(o_ref, h_ref, wo_ref, fnorm_ref, wrt_ref, br_ref, triu_ref,
                  h_out_ref, xn_ref, ri_ref, rf_ref, cnt_ref, base_ref, wsplit_ref):
    i = pl.program_id(0)

    @pl.when(i == 0)
    def _():
        _route_init(wrt_ref, wsplit_ref, base_ref)

    h = h_ref[...] + _dot(o_ref[...], wo_ref[...])
    h_out_ref[...] = h
    xn = _rms(h, fnorm_ref[...])
    _store_token_tiles(xn_ref, _pack_rows(xn))
    valid = lax.broadcasted_iota(jnp.int32, (1, h.shape[0]), 1) >= 0
    _route(xn, valid, wsplit_ref, br_ref, triu_ref, base_ref, ri_ref, rf_ref, cnt_ref)


def _full(shape):
    nd = len(shape)
    return pl.BlockSpec(shape, lambda *_: (0,) * nd)


def _router_operands(router_g, router_g_bias, router_e, router_e_bias):
    d = router_g.shape[0]
    wrt = jnp.concatenate([router_g.T, jnp.zeros((8 - N_EXPERT_GROUPS, d), _F32), router_e.T], axis=0)
    br = jnp.concatenate([router_g_bias, jnp.zeros((8 - N_EXPERT_GROUPS,), _F32), router_e_bias])[:, None]
    return wrt.astype(_F32), br.astype(_F32)


def _route_out(n_tiles, tm):
    shapes = [jax.ShapeDtypeStruct((n_tiles, 8, tm), jnp.int32),
              jax.ShapeDtypeStruct((n_tiles, 8, tm), _F32),
              jax.ShapeDtypeStruct((N_EXPERTS, 128), jnp.int32)]
    specs = [pl.BlockSpec((1, 8, tm), lambda i: (i, 0, 0)),
             pl.BlockSpec((1, 8, tm), lambda i: (i, 0, 0)),
             pl.BlockSpec((N_EXPERTS, 128), lambda i: (0, 0))]
    return shapes, specs


def _moe_experts(xn, ri, counts, w_gate, w_up, w_down, layer, n_tiles, n_real_tiles, d):
    tm, tr = TOKEN_TILE, EXPERT_ROWS
    parts = d // (2 * LANES)
    n_valid = n_real_tiles * tm + (n_tiles - n_real_tiles) * N_META
    n_blocks = -(-(2 * n_valid + N_EXPERTS * (tr - 1)) // tr)
    n_blocks = -(-n_blocks // EXPERT_BLOCKS_PER_STEP) * EXPERT_BLOCKS_PER_STEP
    n_rows = n_blocks * tr

    counts = counts[:, 0]
    padded = (counts + tr - 1) // tr * tr
    pends = jnp.cumsum(padded)
    pstarts = pends - padded
    n_used = (pends[-1] // tr).astype(jnp.int32).reshape(1)
    blk0 = jnp.arange(n_blocks, dtype=jnp.int32) * tr
    block_e = jnp.minimum(jnp.sum(blk0[:, None] >= pends[None, :], axis=1), N_EXPERTS - 1).astype(jnp.int32)
    experts = jnp.arange(N_EXPERTS, dtype=jnp.int32)
    block_oh = block_e[:, None] == experts[None, :]
    block_cnt = jnp.sum(jnp.where(block_oh, counts[None, :], 0), axis=1)
    block_start = jnp.sum(jnp.where(block_oh, pstarts[None, :], 0), axis=1)
    block_valid = jnp.clip(block_cnt - (blk0 - block_start), 0, tr).astype(jnp.int32)
    eid = ri[:, 0:2, :]
    slot0 = jnp.sum(jnp.where(eid[..., None] == experts, pstarts, 0), axis=-1)
    dest = (slot0 + ri[:, 2:4, :]).astype(jnp.int32).reshape(n_tiles * 2 * tm)

    pad_start = (pstarts + counts).astype(jnp.int32)
    pad_len = (padded - counts).astype(jnp.int32)
    xs = pl.pallas_call(
        functools.partial(_dispatch_kernel, tm=tm, tr=tr, n_tiles=n_tiles, n_real_tiles=n_real_tiles,
                          n_blocks=n_blocks),
        grid_spec=pltpu.PrefetchScalarGridSpec(
            num_scalar_prefetch=3,
            grid=(n_tiles,),
            in_specs=[pl.BlockSpec(memory_space=pl.ANY), pl.BlockSpec(memory_space=pl.ANY)],
            out_specs=pl.BlockSpec(memory_space=pl.ANY),
            scratch_shapes=[pltpu.SMEM((4 * tm,), jnp.int32), pltpu.VMEM((3, tm * parts, LANES), jnp.uint32),
                            pltpu.VMEM((tr * parts, LANES), jnp.uint32),
                            pltpu.SemaphoreType.DMA((2,)), pltpu.SemaphoreType.DMA((3,)),
                            pltpu.SemaphoreType.DMA((3,)), pltpu.SemaphoreType.DMA]),
        out_shape=jax.ShapeDtypeStruct((n_rows * parts, LANES), jnp.uint32),
        compiler_params=_params(1, has_side_effects=True, disable_bounds_checks=True),
        name="moe_dispatch",
    )(pad_start, pad_len, pends[-1:].astype(jnp.int32), dest, xn)

    f = w_gate.shape[3]
    nb = EXPERT_BLOCKS_PER_STEP
    step_rows = nb * tr * parts

    def expert_block(c):
        return lambda i, be, bv, nu: (layer, be[jnp.minimum(i * nb + c, nu[0] - 1)], 0, 0)

    weight_specs, weight_scratch = [], []
    for c in range(nb):
        weight_specs += [pl.BlockSpec((1, 1, d, f), expert_block(c)), pl.BlockSpec((1, 1, d, f), expert_block(c)),
                         pl.BlockSpec((1, 1, f, d), expert_block(c))]
        weight_scratch += [pltpu.VMEM((d, f), _BF16), pltpu.VMEM((d, f), _BF16), pltpu.VMEM((f, d), _BF16)]
    y = pl.pallas_call(
        _expert_kernel,
        grid_spec=pltpu.PrefetchScalarGridSpec(
            num_scalar_prefetch=3,
            grid=(n_blocks // nb,),
            in_specs=[pl.BlockSpec((step_rows, LANES),
                                   lambda i, be, bv, nu: (jnp.minimum(i, (nu[0] - 1) // nb), 0))] + weight_specs,
            out_specs=pl.BlockSpec((step_rows, LANES), lambda i, be, bv, nu: (i, 0)),
            scratch_shapes=weight_scratch),
        out_shape=jax.ShapeDtypeStruct((n_rows * parts, LANES), jnp.uint32),
        compiler_params=_params(1),
        name="moe_experts",
    )(block_e, block_valid, n_used, xs, *([w_gate, w_up, w_down] * nb))

    return dest, y


def _combine_operands(dest, rf, h, y, d):
    tm = TOKEN_TILE
    parts = d // (2 * LANES)
    in_specs = [pl.BlockSpec(memory_space=pl.ANY),
                pl.BlockSpec((1, 8, tm), lambda i: (i, 0, 0)),
                pl.BlockSpec((tm, d), lambda i: (i, 0)),
                pl.BlockSpec(memory_space=pl.ANY)]
    scratch = [pltpu.SMEM((4 * tm,), jnp.int32),
               pltpu.VMEM((2 * tm * parts, LANES), jnp.uint32), pltpu.VMEM((2 * tm * parts, LANES), jnp.uint32),
               pltpu.SemaphoreType.DMA((2,)), pltpu.SemaphoreType.DMA((2,))]
    return (dest, rf, h, y), in_specs, scratch


def kernel(x, meta_tokens, a_norm, a_w, a_scale, b_norm, b_w_dq, b_q_norm, b_w_uq, b_w_o, kv_norm, w_dkv,
           kv_lat_norm, w_uk, w_uv, ffn_norm, router_g, router_g_bias, router_e, router_e_bias, w_gate, w_up,
           w_down, final_norm):
    bsz, seq, d = x.shape
    tm = TOKEN_TILE
    assert seq % tm == 0 and seq % ATTN_Q_TILE == 0 and ATTN_Q_TILE == 2 * ATTN_K_TILE and tm % ATTN_K_TILE == 0
    kt = tm // ATTN_K_TILE
    assert d % (LANES * len(POOL_WINDOWS)) == 0 and N_META == max(POOL_WINDOWS) and N_META <= tm
    parts = d // (2 * LANES)
    n_tok = bsz * seq
    n_real_tiles = n_tok // tm
    tiles_per_batch = seq // tm
    n_tiles = n_real_tiles + 1
    gd = d // len(POOL_WINDOWS)
    row = lambda v: v.reshape(1, -1).astype(_F32)

    x2 = x.reshape(n_tok, d)
    meta_pad = jnp.concatenate([meta_tokens, jnp.zeros((tm - N_META, d), x.dtype)], axis=0)

    r = np.arange(tm)[:, None]
    cidx = np.arange(tm)[None, :]
    pb = min(POOL_BLOCK, tm)
    pm = np.stack([((r[:pb] - cidx[:, :pb] >= 0) & (r[:pb] - cidx[:, :pb] < w)) for w in POOL_WINDOWS]).astype(np.float32)
    hc = np.arange(N_META)[None, :]
    ph = np.stack([(r[:N_META] + N_META - hc < w) for w in POOL_WINDOWS]).astype(np.float32)
    triu = (r < cidx).astype(np.float32)
    pm, ph, triu = (jnp.asarray(a, dtype=_BF16) for a in (pm, ph, triu))

    wrt0, br0 = _router_operands(router_g[0], router_g_bias[0], router_e[0], router_e_bias[0])
    route_shapes, route_specs = _route_out(n_tiles, tm)
    tile_or_last = lambda i: (jnp.minimum(i, n_real_tiles - 1), 0)
    halo_blocks = tm // N_META
    h1, xn1, ri1, rf1, cnt1 = pl.pallas_call(
        functools.partial(_mixer_kernel, n_real_tiles=n_real_tiles, tiles_per_batch=tiles_per_batch),
        grid=(n_tiles,),
        in_specs=[pl.BlockSpec((tm, d), tile_or_last),
                  pl.BlockSpec((N_META, d), lambda i: (jnp.clip(i * halo_blocks - 1, 0, n_tok // N_META - 1), 0)),
                  _full((tm, d)), _full((N_META, d)), _full((1, d)),
                  _full((len(POOL_WINDOWS), gd, gd)), _full((1, d)), _full((1, d)),
                  _full((ROUTER_ROWS, d)), _full((ROUTER_ROWS, 1)),
                  _full(pm.shape), _full(ph.shape), _full((tm, tm))],
        out_specs=[pl.BlockSpec((tm, d), lambda i: (i, 0)),
                   pl.BlockSpec((tm * parts, LANES), lambda i: (i, 0))] + route_specs,
        out_shape=[jax.ShapeDtypeStruct((n_tiles * tm, d), _F32),
                   jax.ShapeDtypeStruct((n_tiles * tm * parts, LANES), jnp.uint32)] + route_shapes,
        scratch_shapes=[pltpu.VMEM((N_EXPERTS, 1), _F32), pltpu.VMEM((ROUTER_SPLIT_ROWS, d), _BF16)],
        compiler_params=_params(1),
        name="pool_mixer_router",
    )(x2, x2, meta_pad, meta_tokens, row(a_norm[0]), a_w[0].astype(_BF16), row(a_scale[0]), row(ffn_norm[0]),
      wrt0, br0, pm, ph, triu)

    dest1, y1 = _moe_experts(xn1, ri1, cnt1, w_gate, w_up, w_down, 0, n_tiles, n_real_tiles, d)

    pos = jnp.concatenate([jnp.arange(seq, dtype=_F32) + N_META, jnp.arange(tm, dtype=_F32)])
    inv_freq = ROPE_THETA ** (-jnp.arange(0, QK_ROPE_DIM, 2, dtype=_F32) / QK_ROPE_DIM)
    ang = pos[:, None] * inv_freq[None, :]
    cos_t = jnp.tile(jnp.cos(ang), (1, 4))
    sin_t = jnp.tile(jnp.concatenate([-jnp.sin(ang), jnp.sin(ang)], axis=1), (1, 2))

    wdkv = jnp.concatenate([w_dkv, w_dkv[:, KV_LORA_RANK:]], axis=1).astype(_BF16)
    wuk = w_uk.reshape(KV_LORA_RANK, N_HEADS * QK_NOPE_DIM).astype(_BF16)
    wuv = w_uv.reshape(KV_LORA_RANK, N_HEADS * V_HEAD_DIM).T.astype(_BF16)
    wuq = b_w_uq[0]
    q_rank = wuq.shape[0]
    wuq = jnp.concatenate([wuq[:, :, :QK_NOPE_DIM].reshape(q_rank, -1),
                           wuq[:, :, QK_NOPE_DIM:].reshape(q_rank, -1)], axis=1).astype(_BF16)
    wdq = b_w_dq[0].astype(_BF16)
    pos_tile = lambda i: (jnp.where(i < n_real_tiles, i % tiles_per_batch, tiles_per_batch), 0)
    head_tile = lambda i: (0, i, 0)
    operands, in_specs, scratch = _combine_operands(dest1, rf1, h1, y1, d)
    h2, q, k, v = pl.pallas_call(
        functools.partial(_combine_proj_kernel, tm=tm, n_tiles=n_tiles),
        grid=(n_tiles,),
        in_specs=in_specs + [
            pl.BlockSpec((tm, 128), pos_tile), pl.BlockSpec((tm, 128), pos_tile),
            _full((1, d)), _full(wdkv.shape), _full((1, KV_LORA_RANK)), _full(wuk.shape), _full(wuv.shape),
            _full((1, d)), _full(wdq.shape), _full((1, q_rank)), _full(wuq.shape)],
        out_specs=[pl.BlockSpec((tm, d), lambda i: (i, 0)),
                   pl.BlockSpec((N_HEADS, tm, QK_DIM), head_tile),
                   pl.BlockSpec((N_HEADS, tm, QK_DIM), head_tile),
                   pl.BlockSpec((N_HEADS, kt, V_HEAD_DIM, ATTN_K_TILE), lambda i: (0, i, 0, 0))],
        out_shape=[jax.ShapeDtypeStruct((n_tiles * tm, d), _F32),
                   jax.ShapeDtypeStruct((N_HEADS, n_tiles * tm, QK_DIM), _BF16),
                   jax.ShapeDtypeStruct((N_HEADS, n_tiles * tm, QK_DIM), _BF16),
                   jax.ShapeDtypeStruct((N_HEADS, n_tiles * kt, V_HEAD_DIM, ATTN_K_TILE), _BF16)],
        scratch_shapes=scratch,
        compiler_params=_params(1, disable_bounds_checks=True),
        name="moe_combine_latent_qkv",
    )(*operands, cos_t, sin_t, row(kv_norm), wdkv, row(kv_lat_norm), wuk, wuv,
      row(b_norm[0]), wdq, row(b_q_norm[0]), wuq)

    tq, tk, hp = ATTN_Q_TILE, ATTN_K_TILE, ATTN_HEADS_PER_STEP
    n_q = seq // tq
    meta_block = n_tok // N_META
    o = pl.pallas_call(
        functools.partial(_attn_kernel, tq=tq, tk=tk),
        grid=(bsz, N_HEADS // hp, n_q),
        in_specs=[pl.BlockSpec((hp, tq, QK_DIM), lambda b, hg, j: (hg, b * n_q + j, 0)),
                  pl.BlockSpec((hp, seq, QK_DIM), lambda b, hg, j: (hg, b, 0)),
                  pl.BlockSpec((hp, seq // tk, V_HEAD_DIM, tk), lambda b, hg, j: (hg, b, 0, 0)),
                  pl.BlockSpec((hp, N_META, QK_DIM), lambda b, hg, j: (hg, meta_block, 0)),
                  pl.BlockSpec((hp, 1, V_HEAD_DIM, tk), lambda b, hg, j: (hg, n_real_tiles * kt, 0, 0))],
        out_specs=pl.BlockSpec((tq, hp * V_HEAD_DIM), lambda b, hg, j: (b * n_q + j, hg)),
        out_shape=jax.ShapeDtypeStruct((n_tok, N_HEADS * V_HEAD_DIM), _BF16),
        scratch_shapes=[pltpu.VMEM((hp, tk, tq), _F32), pltpu.VMEM((hp, tk, tq), _F32),
                        pltpu.VMEM((hp, V_HEAD_DIM, tq), _F32)],
        compiler_params=_params(3),
        name="causal_attention",
    )(q, k, v, k, v)

    wrt1, br1 = _router_operands(router_g[1], router_g_bias[1], router_e[1], router_e_bias[1])
    route_shapes, route_specs = _route_out(n_real_tiles, tm)
    h3, xn2, ri2, rf2, cnt2 = pl.pallas_call(
        _oproj_kernel,
        grid=(n_real_tiles,),
        in_specs=[pl.BlockSpec((tm, N_HEADS * V_HEAD_DIM), lambda i: (i, 0)),
                  pl.BlockSpec((tm, d), lambda i: (i, 0)),
                  _full((N_HEADS * V_HEAD_DIM, d)), _full((1, d)),
                  _full((ROUTER_ROWS, d)), _full((ROUTER_ROWS, 1)), _full((tm, tm))],
        out_specs=[pl.BlockSpec((tm, d), lambda i: (i, 0)),
                   pl.BlockSpec((tm * parts, LANES), lambda i: (i, 0))] + route_specs,
        out_shape=[jax.ShapeDtypeStruct((n_tok, d), _F32),
                   jax.ShapeDtypeStruct((n_tok * parts, LANES), jnp.uint32)] + route_shapes,
        scratch_shapes=[pltpu.VMEM((N_EXPERTS, 1), _F32), pltpu.VMEM((ROUTER_SPLIT_ROWS, d), _BF16)],
        compiler_params=_params(1),
        name="attn_out_router",
    )(o, h2, b_w_o[0].astype(_BF16), row(ffn_norm[1]), wrt1, br1, triu)

    dest2, y2 = _moe_experts(xn2, ri2, cnt2, w_gate, w_up, w_down, 1, n_real_tiles, n_real_tiles, d)
    operands, in_specs, scratch = _combine_operands(dest2, rf2, h3, y2, d)
    out = pl.pallas_call(
        functools.partial(_combine_norm_kernel, tm=tm, n_tiles=n_real_tiles),
        grid=(n_real_tiles,),
        in_specs=in_specs + [_full((1, d))],
        out_specs=pl.BlockSpec((tm, d), lambda i: (i, 0)),
        out_shape=jax.ShapeDtypeStruct((n_tok, d), _F32),
        scratch_shapes=scratch,
        compiler_params=_params(1, disable_bounds_checks=True),
        name="moe_combine_final_norm",
    )(*operands, row(final_norm))
    return out.reshape(bsz, seq, d)
```

```python
import functools

import numpy as np
import jax
import jax.numpy as jnp
from jax import lax
from jax.experimental import pallas as pl
from jax.experimental.pallas import tpu as pltpu

N_META = 16
POOL_WINDOWS = (2, 4, 8, 16)
N_HEADS = 8
QK_NOPE_DIM = 128
QK_ROPE_DIM = 64
QK_DIM = QK_NOPE_DIM + QK_ROPE_DIM
V_HEAD_DIM = 128
KV_LORA_RANK = 256
ROPE_THETA = 10000.0
ATTN_SCALE = QK_DIM ** -0.5
Q_SCALE = ATTN_SCALE * 1.4426950408889634
N_EXPERT_GROUPS = 4
EXPERTS_PER_GROUP = 8
N_EXPERTS = N_EXPERT_GROUPS * EXPERTS_PER_GROUP
RMS_EPS = 1e-6
NEG_INF = -1e30

TOKEN_TILE = 512
POOL_BLOCK = 256
EXPERT_ROWS = 256
EXPERT_BLOCKS_PER_STEP = 4
ATTN_Q_TILE = 512
ATTN_K_TILE = ATTN_Q_TILE // 2
ATTN_HEADS_PER_STEP = 4
DMA_UNROLL = 8
DMA_THREADS = 2
ROUTER_ROWS = 8 + N_EXPERTS
ROUTER_LO_ROW = 48
ROUTER_SPLIT_ROWS = 2 * ROUTER_LO_ROW
VMEM_LIMIT_BYTES = 48 * 1024 * 1024

LANES = 128
_F32 = jnp.float32
_BF16 = jnp.bfloat16
_NT_DIMS = (((1,), (1,)), ((), ()))


def _params(n_grid_dims=1, **kw):
    return pltpu.CompilerParams(dimension_semantics=("arbitrary",) * n_grid_dims,
                                vmem_limit_bytes=VMEM_LIMIT_BYTES, **kw)


def _rms(x, g):
    ms = jnp.mean(x * x, axis=-1, keepdims=True)
    return x * lax.rsqrt(ms + RMS_EPS) * g


def _split_bf16(x):
    hi = x.astype(_BF16)
    lo = (x - hi.astype(_F32)).astype(_BF16)
    return hi, lo


def _dot(a, b):
    return jnp.dot(a, b, preferred_element_type=_F32)


def _store_token_tiles(ref, x):
    n, d = x.shape
    parts = d // LANES
    for s in range(parts):
        ref[pl.ds(s, n, stride=parts), :] = x[:, s * LANES:(s + 1) * LANES]


def _load_token_tiles(ref, first_token, n, parts):
    return jnp.concatenate([ref[pl.ds(first_token * parts + s, n, stride=parts), :] for s in range(parts)], axis=1)


_HIGH_HALF = 0xFFFF0000


def _pack_rows(x):
    n, d = x.shape
    words = []
    for s in range(d // (2 * LANES)):
        lo = x[:, 2 * s * LANES:(2 * s + 1) * LANES].astype(_BF16).astype(_F32)
        hi = x[:, (2 * s + 1) * LANES:(2 * s + 2) * LANES].astype(_BF16).astype(_F32)
        words.append(lax.shift_right_logical(lax.bitcast_convert_type(lo, jnp.uint32), jnp.uint32(16))
                     | (lax.bitcast_convert_type(hi, jnp.uint32) & jnp.uint32(_HIGH_HALF)))
    return jnp.concatenate(words, axis=1)


def _unpack_rows(w):
    cols = []
    for s in range(w.shape[1] // LANES):
        ws = w[:, s * LANES:(s + 1) * LANES]
        cols.append(lax.bitcast_convert_type(lax.shift_left(ws, jnp.uint32(16)), _F32))
        cols.append(lax.bitcast_convert_type(ws & jnp.uint32(_HIGH_HALF), _F32))
    return jnp.concatenate(cols, axis=1)


def _route_init(wrt_ref, wsplit_ref, base_ref):
    base_ref[...] = jnp.zeros(base_ref.shape, base_ref.dtype)
    w_hi, w_lo = _split_bf16(wrt_ref[...])
    wsplit_ref[...] = jnp.zeros(wsplit_ref.shape, wsplit_ref.dtype)
    wsplit_ref[0:ROUTER_ROWS, :] = w_hi
    wsplit_ref[ROUTER_LO_ROW:ROUTER_LO_ROW + ROUTER_ROWS, :] = w_lo


def _route(xn, valid, wsplit_ref, br_ref, triu_ref, base_ref, ri_ref, rf_ref, cnt_ref):
    tm = xn.shape[0]
    x_hi, x_lo = _split_bf16(xn)
    by_hi = lax.dot_general(wsplit_ref[...], x_hi, _NT_DIMS, preferred_element_type=_F32)
    by_lo = lax.dot_general(wsplit_ref[0:ROUTER_LO_ROW, :], x_lo, _NT_DIMS, preferred_element_type=_F32)
    logits = (by_hi[0:ROUTER_ROWS] + by_hi[ROUTER_LO_ROW:ROUTER_LO_ROW + ROUTER_ROWS]
              + by_lo[0:ROUTER_ROWS] + br_ref[...])
    lg = logits[0:N_EXPERT_GROUPS]
    eg = jnp.exp(lg - jnp.max(lg, axis=0, keepdims=True))
    pg = eg / jnp.sum(eg, axis=0, keepdims=True)
    w_g = jnp.max(pg, axis=0, keepdims=True)
    ig = lax.broadcasted_iota(jnp.int32, pg.shape, 0).astype(_F32)
    g_sel = jnp.min(jnp.where(pg == w_g, ig, float(N_EXPERT_GROUPS)), axis=0, keepdims=True)

    sel = logits[8:8 + EXPERTS_PER_GROUP]
    for g in range(1, N_EXPERT_GROUPS):
        sel = jnp.where(g_sel == float(g), logits[8 + g * EXPERTS_PER_GROUP:8 + (g + 1) * EXPERTS_PER_GROUP], sel)
    ie = lax.broadcasted_iota(jnp.int32, sel.shape, 0).astype(_F32)
    v1 = jnp.max(sel, axis=0, keepdims=True)
    i1 = jnp.min(jnp.where(sel == v1, ie, float(EXPERTS_PER_GROUP)), axis=0, keepdims=True)
    rest = jnp.where(ie == i1, -jnp.inf, sel)
    v2 = jnp.max(rest, axis=0, keepdims=True)
    i2 = jnp.min(jnp.where(rest == v2, ie, float(EXPERTS_PER_GROUP)), axis=0, keepdims=True)
    e2 = jnp.exp(v2 - v1)
    den = 1.0 + e2
    validf = valid.astype(_F32)
    gate0 = w_g * (1.0 / den) * validf
    gate1 = w_g * (e2 / den) * validf
    eid0 = g_sel * float(EXPERTS_PER_GROUP) + i1
    eid1 = g_sel * float(EXPERTS_PER_GROUP) + i2

    iall = lax.broadcasted_iota(jnp.int32, (N_EXPERTS, tm), 0).astype(_F32)
    oh0 = jnp.where(iall == eid0, validf, 0.0)
    oh1 = jnp.where(iall == eid1, validf, 0.0)
    both = oh0 + oh1
    before = _dot(both.astype(_BF16), triu_ref[...]) + base_ref[...]
    rank0 = jnp.sum(oh0 * before, axis=0, keepdims=True)
    rank1 = jnp.sum(oh1 * before, axis=0, keepdims=True)
    base_ref[...] = base_ref[...] + jnp.sum(both, axis=1, keepdims=True)

    ri_ref[...] = jnp.zeros(ri_ref.shape, ri_ref.dtype)
    rf_ref[...] = jnp.zeros(rf_ref.shape, rf_ref.dtype)
    ri_ref[0, 0:1, :] = eid0.astype(jnp.int32)
    ri_ref[0, 1:2, :] = eid1.astype(jnp.int32)
    ri_ref[0, 2:3, :] = rank0.astype(jnp.int32)
    ri_ref[0, 3:4, :] = rank1.astype(jnp.int32)
    rf_ref[0, 0:1, :] = gate0
    rf_ref[0, 1:2, :] = gate1
    cnt_ref[...] = jnp.broadcast_to(base_ref[...], cnt_ref.shape).astype(jnp.int32)


def _mixer_kernel(x_ref, xh_ref, mp_ref, meta_ref, anorm_ref, aw_ref, ascale_ref, fnorm_ref,
                  wrt_ref, br_ref, pm_ref, ph_ref, triu_ref,
                  h_ref, xn_ref, ri_ref, rf_ref, cnt_ref, base_ref, wsplit_ref, *, n_real_tiles, tiles_per_batch):
    i = pl.program_id(0)
    tm = x_ref.shape[0]
    gd = x_ref.shape[1] // len(POOL_WINDOWS)
    is_meta = i == n_real_tiles
    first = (i % tiles_per_batch) == 0

    @pl.when(i == 0)
    def _():
        _route_init(wrt_ref, wsplit_ref, base_ref)

    h = jnp.where(is_meta, mp_ref[...], x_ref[...])
    halo = jnp.where(is_meta, 0.0, jnp.where(first, meta_ref[...], xh_ref[...]))
    hn = _rms(h, anorm_ref[...])
    hh = _rms(halo, anorm_ref[...])
    hn_hi, hn_lo = _split_bf16(hn)
    hh_hi, hh_lo = _split_bf16(hh)
    pb = pm_ref.shape[1]
    wins = {}
    for c in range(tm // pb):
        rows = slice(c * pb, (c + 1) * pb)
        for g in range(len(POOL_WINDOWS)):
            sl = slice(g * gd, (g + 1) * gd)
            win = _dot(pm_ref[g], hn_hi[rows, sl]) + _dot(pm_ref[g], hn_lo[rows, sl])
            if c == 0:
                before_hi, before_lo = hh_hi[:, sl], hh_lo[:, sl]
            else:
                before_hi, before_lo = hn_hi[c * pb - N_META:c * pb, sl], hn_lo[c * pb - N_META:c * pb, sl]
            top = win[:N_META] + _dot(ph_ref[g], before_hi) + _dot(ph_ref[g], before_lo)
            wins[c, g] = jnp.concatenate([top, win[N_META:]], axis=0)
    for c in range(tm // pb):
        rows = slice(c * pb, (c + 1) * pb)
        row = c * pb + lax.broadcasted_iota(jnp.int32, (pb, 1), 0)
        for g, w in enumerate(POOL_WINDOWS):
            sl = slice(g * gd, (g + 1) * gd)
            cnt = jnp.where(is_meta, jnp.minimum(row + 1, w), w).astype(_F32)
            pooled = wins[c, g] * (1.0 / cnt) - hn[rows, sl]
            mix = _dot(pooled.astype(_BF16), aw_ref[g])
            h_ref[rows, sl] = h[rows, sl] + mix * ascale_ref[:, sl]

    xn = _rms(h_ref[...], fnorm_ref[...])
    _store_token_tiles(xn_ref, _pack_rows(xn))
    lane = lax.broadcasted_iota(jnp.int32, (1, tm), 1)
    valid = jnp.logical_or(jnp.logical_not(is_meta), lane < N_META)
    _route(xn, valid, wsplit_ref, br_ref, triu_ref, base_ref, ri_ref, rf_ref, cnt_ref)


def _index_copy(dest_hbm, idx_smem, isem, tile, n):
    return pltpu.make_async_copy(dest_hbm.at[pl.ds(pl.multiple_of(tile * n, n), n)],
                                 idx_smem.at[pl.ds(pl.multiple_of((tile % 2) * n, n), n)], isem.at[tile % 2])


def _dispatch_kernel(pad_start_ref, pad_len_ref, tail_ref, dest_hbm, xn_hbm, xs_hbm,
                     idx_smem, xbuf, zbuf, isem, lsem, ssem, zsem, *, tm, tr, n_tiles, n_real_tiles, n_blocks):
    i = pl.program_id(0)
    parts = xbuf.shape[1] // tm
    buf = i % 3

    def zero_fill(wait):
        def zero_rows(first_row, n_rows):
            cp = pltpu.make_async_copy(
                zbuf.at[pl.ds(0, n_rows * parts), :],
                xs_hbm.at[pl.ds(pl.multiple_of(first_row * parts, parts), n_rows * parts), :], zsem)
            if wait:
                cp.wait()
            else:
                cp.start()

        def per_expert(e, c):
            first, length = pad_start_ref[e], pad_len_ref[e]
            for bit in range((tr - 1).bit_length()):
                @pl.when((length >> bit) % 2 == 1)
                def _(bit=bit):
                    zero_rows(first + ((length >> (bit + 1)) << (bit + 1)), 1 << bit)
            return c

        def per_block(blk, c):
            zero_rows(blk * tr, tr)
            return c

        lax.fori_loop(0, N_EXPERTS, per_expert, 0)
        lax.fori_loop(tail_ref[0] // tr, n_blocks, per_block, 0)

    def idx_copy(tile):
        return _index_copy(dest_hbm, idx_smem, isem, tile, 2 * tm)

    def load(tile):
        rows = tm * parts
        return pltpu.make_async_copy(xn_hbm.at[pl.ds(pl.multiple_of(tile * rows, rows), rows), :],
                                     xbuf.at[tile % 3], lsem.at[tile % 3])

    def wait_scatters(tile):
        def wait_rows(n_tok):
            for k in range(2):
                pltpu.make_async_copy(xbuf.at[tile % 3, pl.ds(0, n_tok * parts), :],
                                      xs_hbm.at[pl.ds(0, n_tok * parts), :], ssem.at[tile % 3]).wait()

        @pl.when(tile < n_real_tiles)
        def _():
            wait_rows(tm)

        @pl.when(tile >= n_real_tiles)
        def _():
            wait_rows(N_META)

    def scatter(n_tok):
        def issue(g, c):
            for u in range(min(DMA_UNROLL, n_tok)):
                t = g * min(DMA_UNROLL, n_tok) + u
                src = xbuf.at[buf, pl.ds(pl.multiple_of(t * parts, parts), parts), :]
                for k in range(2):
                    d = idx_smem[(i % 2) * (2 * tm) + k * tm + t]
                    pltpu.make_async_copy(src, xs_hbm.at[pl.ds(pl.multiple_of(d * parts, parts), parts), :],
                                          ssem.at[buf]).start(priority=(2 * u + k) % DMA_THREADS)
            return c

        lax.fori_loop(0, n_tok // min(DMA_UNROLL, n_tok), issue, 0)

    @pl.when(i == 0)
    def _():
        idx_copy(0).start()
        load(0).start()
        zbuf[...] = jnp.zeros(zbuf.shape, zbuf.dtype)
        zero_fill(wait=False)

    @pl.when(i >= 2)
    def _():
        wait_scatters(i - 2)

    @pl.when(i + 1 < n_tiles)
    def _():
        idx_copy(i + 1).start()
        load(i + 1).start()

    idx_copy(i).wait()
    load(i).wait()

    @pl.when(i < n_real_tiles)
    def _():
        scatter(tm)

    @pl.when(i >= n_real_tiles)
    def _():
        scatter(N_META)

    @pl.when(i == n_tiles - 1)
    def _():
        if n_tiles >= 2:
            wait_scatters(i - 1)
        wait_scatters(i)
        zero_fill(wait=True)


def _expert_kernel(be_ref, bv_ref, nu_ref, xs_ref, *refs):
    nb = EXPERT_BLOCKS_PER_STEP
    w_refs, y_ref, wb_refs = refs[:3 * nb], refs[3 * nb], refs[3 * nb + 1:]
    i = pl.program_id(0)
    n_used = nu_ref[0]
    parts = wb_refs[0].shape[0] // (2 * LANES)
    tr = xs_ref.shape[0] // (nb * parts)

    for c in range(nb):
        blk = i * nb + c
        changed = be_ref[blk] != be_ref[jnp.maximum(blk - nb, 0)]

        @pl.when(jnp.logical_or(i == 0, jnp.logical_and(blk < n_used, changed)))
        def _(c=c):
            for w_ref, wb_ref in zip(w_refs[3 * c:3 * c + 3], wb_refs[3 * c:3 * c + 3]):
                wb_ref[...] = w_ref[0, 0].astype(_BF16)

    @pl.when(i * nb < n_used)
    def _():
        rows = lax.broadcasted_iota(jnp.int32, (tr, 1), 0)
        xs, gs, us, ys = [], [], [], []
        for c in range(nb):
            x = jnp.where(rows < bv_ref[i * nb + c], _unpack_rows(_load_token_tiles(xs_ref, c * tr, tr, parts)), 0.0)
            xs.append(x.astype(_BF16))
        for c in range(nb):
            gs.append(_dot(xs[c], wb_refs[3 * c][...]))
            us.append(_dot(xs[c], wb_refs[3 * c + 1][...]))
        for c in range(nb):
            a = gs[c] * (1.0 / (1.0 + jnp.exp(-gs[c]))) * us[c]
            ys.append(_dot(a.astype(_BF16), wb_refs[3 * c + 2][...]))
        for c in range(nb):
            y = jnp.where(i * nb + c < n_used, ys[c], 0.0)
            _store_token_tiles(y_ref.at[pl.ds(c * tr * parts, tr * parts), :], _pack_rows(y))

    @pl.when(i * nb >= n_used)
    def _():
        y_ref[...] = jnp.zeros(y_ref.shape, y_ref.dtype)


def _combined_tile(dest_hbm, rf_ref, h_ref, y_hbm, idx_smem, ybufs, isem, rsem, *, tm, n_tiles, slot):
    i = pl.program_id(0)
    parts = ybufs[0].shape[0] // (2 * tm)

    def idx_copy(tile):
        return _index_copy(dest_hbm, idx_smem, isem, tile, 2 * tm)

    def row_copy(idx_slot, buf_slot, t, k, u):
        d = idx_smem[idx_slot * (2 * tm) + k * tm + t]
        row = (k * tm + t) * parts
        if not isinstance(t, int):
            row = pl.multiple_of(row, parts)
        return pltpu.make_async_copy(
            y_hbm.at[pl.ds(pl.multiple_of(d * parts, parts), parts), :],
            ybufs[buf_slot].at[pl.ds(row, parts), :],
            rsem.at[buf_slot]).start(priority=(2 * u + k) % DMA_THREADS)

    def wait_rows(buf_slot):
        pltpu.make_async_copy(y_hbm.at[pl.ds(0, 2 * tm * parts), :], ybufs[buf_slot], rsem.at[buf_slot]).wait()

    if slot == 0:
        @pl.when(i == 0)
        def _():
            idx_copy(0).start()
            idx_copy(0).wait()

            def issue(g, c):
                for u in range(DMA_UNROLL):
                    for k in range(2):
                        row_copy(0, 0, g * DMA_UNROLL + u, k, u)
                return c

            lax.fori_loop(0, tm // DMA_UNROLL, issue, 0)
            if n_tiles >= 2:
                idx_copy(1).start()

    @pl.when(i + 1 < n_tiles)
    def _():
        idx_copy(i + 1).wait()

    @pl.when(i + 2 < n_tiles)
    def _():
        idx_copy(i + 2).start()

    nxt_idx_slot = jnp.minimum(i + 1, n_tiles - 1) % 2

    def prefetch(part, n_parts):
        for t in range(part * tm // n_parts, (part + 1) * tm // n_parts):
            for k in range(2):
                row_copy(nxt_idx_slot, 1 - slot, t, k, t)

    gates = rf_ref[0]
    gt = jnp.concatenate([gates, jnp.zeros((LANES - gates.shape[0], tm), _F32)], axis=0).T
    wait_rows(slot)
    yb = ybufs[slot]

    def combined(r0, n):
        return h_ref[r0:r0 + n, :] + (_unpack_rows(_load_token_tiles(yb, r0, n, parts)) * gt[r0:r0 + n, 0:1]
                                      + _unpack_rows(_load_token_tiles(yb, tm + r0, n, parts)) * gt[r0:r0 + n, 1:2])

    def drain():
        @pl.when(i == n_tiles - 1)
        def _():
            wait_rows(1 - slot)

    return combined, prefetch, drain


def _for_each_parity(body):
    for slot in range(2):
        @pl.when(pl.program_id(0) % 2 == slot)
        def _(slot=slot):
            body(slot)


def _combine_norm_kernel(dest_hbm, rf_ref, h_ref, y_hbm, fnorm_ref, out_ref,
                         idx_smem, ybuf0, ybuf1, isem, rsem, *, tm, n_tiles):
    def body(slot):
        combined, prefetch, drain = _combined_tile(dest_hbm, rf_ref, h_ref, y_hbm, idx_smem, (ybuf0, ybuf1),
                                                   isem, rsem, tm=tm, n_tiles=n_tiles, slot=slot)
        n_chunks = 4
        n = tm // n_chunks
        for c in range(n_chunks):
            prefetch(c, n_chunks)
            out_ref[c * n:(c + 1) * n, :] = _rms(combined(c * n, n), fnorm_ref[...])
        drain()

    _for_each_parity(body)


def _combine_proj_kernel(dest_hbm, rf_ref, h_ref, y_hbm, cos_ref, sin_ref, kvn_ref, wdkv_ref, kvlat_ref,
                         wuk_ref, wuv_ref, bnorm_ref, wdq_ref, qnorm_ref, wuq_ref,
                         h_out_ref, q_ref, k_ref, v_ref, idx_smem, ybuf0, ybuf1, isem, rsem, *, tm, n_tiles):
    def body(slot):
        combined, prefetch, drain = _combined_tile(dest_hbm, rf_ref, h_ref, y_hbm, idx_smem, (ybuf0, ybuf1),
                                                   isem, rsem, tm=tm, n_tiles=n_tiles, slot=slot)
        n = v_ref.shape[3]
        n_chunks = tm // n
        hs = []
        for c in range(n_chunks):
            prefetch(c, n_chunks)
            hs.append(combined(c * n, n))
            h_out_ref[c * n:(c + 1) * n, :] = hs[c]
        _project(hs, cos_ref, sin_ref, kvn_ref, wdkv_ref, kvlat_ref, wuk_ref, wuv_ref,
                 bnorm_ref, wdq_ref, qnorm_ref, wuq_ref, q_ref, k_ref, v_ref)
        drain()

    _for_each_parity(body)


def _rope128(x, cos_t, sin_t):
    lane = lax.broadcasted_iota(jnp.int32, (1, 128), 1)
    first_half = (lane % QK_ROPE_DIM) < (QK_ROPE_DIM // 2)
    swapped = jnp.where(first_half, pltpu.roll(x, 128 - QK_ROPE_DIM // 2, axis=1),
                        pltpu.roll(x, QK_ROPE_DIM // 2, axis=1))
    return x * cos_t + swapped * sin_t


def _project(hs, cos_ref, sin_ref, kvn_ref, wdkv_ref, kvlat_ref, wuk_ref, wuv_ref,
             bnorm_ref, wdq_ref, qnorm_ref, wuq_ref, q_ref, k_ref, v_ref):
    n = hs[0].shape[0]
    chunks = range(len(hs))
    c_kv = [_dot(_rms(h, kvn_ref[...]).astype(_BF16), wdkv_ref[...]) for h in hs]
    c_q = [_dot(_rms(h, bnorm_ref[...]).astype(_BF16), wdq_ref[...]) for h in hs]
    ckv = [_rms(c[:, :KV_LORA_RANK], kvlat_ref[...]).astype(_BF16) for c in c_kv]
    cq = [(_rms(c, qnorm_ref[...]) * Q_SCALE).astype(_BF16) for c in c_q]
    kn = [_dot(ckv[c], wuk_ref[...]) for c in chunks]
    vt = [lax.dot_general(wuv_ref[...], ckv[c], _NT_DIMS, preferred_element_type=_F32) for c in chunks]
    q = [_dot(cq[c], wuq_ref[...]) for c in chunks]
    rope0 = N_HEADS * QK_NOPE_DIM
    for c in chunks:
        rows = slice(c * n, (c + 1) * n)
        cos_t = cos_ref[rows, :]
        sin_t = sin_ref[rows, :]
        kr = _rope128(c_kv[c][:, KV_LORA_RANK:KV_LORA_RANK + 128], cos_t, sin_t)[:, :QK_ROPE_DIM].astype(_BF16)
        for hd in range(N_HEADS):
            k_ref[hd, rows, 0:QK_NOPE_DIM] = kn[c][:, hd * QK_NOPE_DIM:(hd + 1) * QK_NOPE_DIM].astype(_BF16)
            k_ref[hd, rows, QK_NOPE_DIM:QK_DIM] = kr
            v_ref[hd, c] = vt[c][hd * V_HEAD_DIM:(hd + 1) * V_HEAD_DIM, :].astype(_BF16)
            q_ref[hd, rows, 0:QK_NOPE_DIM] = q[c][:, hd * QK_NOPE_DIM:(hd + 1) * QK_NOPE_DIM].astype(_BF16)
        for pair in range(N_HEADS // 2):
            qr = _rope128(q[c][:, rope0 + pair * LANES:rope0 + (pair + 1) * LANES], cos_t, sin_t)
            q_ref[2 * pair, rows, QK_NOPE_DIM:QK_DIM] = qr[:, :QK_ROPE_DIM].astype(_BF16)
            q_ref[2 * pair + 1, rows, QK_NOPE_DIM:QK_DIM] = qr[:, QK_ROPE_DIM:].astype(_BF16)


def _attn_kernel(q_ref, k_ref, vt_ref, km_ref, vmt_ref, o_ref, sa_ref, sb_ref, acc_ref, *, tq, tk):
    j = pl.program_id(2)
    heads = range(q_ref.shape[0])
    vd = acc_ref.shape[1]

    def scores(kb, s_ref):
        start = pl.multiple_of(kb * tk, tk)
        for hd in heads:
            s_ref[hd] = lax.dot_general(k_ref[hd, pl.ds(start, tk), :], q_ref[hd], _NT_DIMS,
                                        preferred_element_type=_F32)

    def update(kb, s_ref, carry, diag_block):
        out = []
        if diag_block is not None:
            visible = (diag_block * tk + lax.broadcasted_iota(jnp.int32, (tk, tq), 0)
                       <= lax.broadcasted_iota(jnp.int32, (tk, tq), 1))
        for hd in heads:
            m, l = carry[hd]
            s = s_ref[hd]
            if diag_block is not None:
                s = jnp.where(visible, s, NEG_INF)
            m_new = jnp.maximum(m, jnp.max(s, axis=0, keepdims=True))
            alpha = jnp.exp2(m - m_new)
            p = jnp.exp2(s - m_new)
            l = alpha * l + jnp.sum(p, axis=0, keepdims=True)
            acc_ref[hd] = acc_ref[hd] * alpha + _dot(vt_ref[hd, kb], p.astype(_BF16))
            out.append((m_new, l))
        return tuple(out)

    s0 = []
    for hd in heads:
        both = lax.dot_general(jnp.concatenate([k_ref[hd, 0:tk, :], km_ref[hd]], axis=0), q_ref[hd], _NT_DIMS,
                               preferred_element_type=_F32)
        sa_ref[hd] = both[0:tk]
        s0.append(both[tk:])
    carry, p0 = [], []
    for hd in heads:
        m = jnp.max(s0[hd], axis=0, keepdims=True)
        p = jnp.exp2(s0[hd] - m)
        carry.append((m, jnp.sum(p, axis=0, keepdims=True)))
        p0.append(p.astype(_BF16))
    for hd in heads:
        acc_ref[hd] = _dot(vmt_ref[hd, 0, :, 0:N_META], p0[hd])
    carry = tuple(carry)

    def pair(kp, carry):
        scores(2 * kp + 1, sb_ref)
        carry = update(2 * kp, sa_ref, carry, None)
        scores(2 * kp + 2, sa_ref)
        return update(2 * kp + 1, sb_ref, carry, None)

    carry = lax.fori_loop(0, j, pair, carry)
    last = 2 * j + 1
    start = pl.multiple_of(last * tk, tk)
    for hd in heads:
        sb_ref[hd, :, 0:tk] = lax.dot_general(k_ref[hd, pl.ds(start, tk), :], q_ref[hd, tk:tq, :], _NT_DIMS,
                                              preferred_element_type=_F32)
    carry = update(2 * j, sa_ref, carry, 0)
    visible = lax.broadcasted_iota(jnp.int32, (tk, tk), 0) <= lax.broadcasted_iota(jnp.int32, (tk, tk), 1)
    for hd in heads:
        m, l = carry[hd]
        s = jnp.where(visible, sb_ref[hd, :, 0:tk], NEG_INF)
        m_new = jnp.maximum(m[:, tk:], jnp.max(s, axis=0, keepdims=True))
        alpha = jnp.exp2(m[:, tk:] - m_new)
        p = jnp.exp2(s - m_new)
        l = jnp.concatenate([l[:, :tk], alpha * l[:, tk:] + jnp.sum(p, axis=0, keepdims=True)], axis=1)
        acc_ref[hd, :, tk:tq] = acc_ref[hd, :, tk:tq] * alpha + _dot(vt_ref[hd, last], p.astype(_BF16))
        o_ref[:, hd * vd:(hd + 1) * vd] = (acc_ref[hd] / l).T.astype(o_ref.dtype)


def _oproj_kernel(o_ref, h_ref, wo_ref, fnorm_ref, wrt_ref, br_ref, triu_ref,
                  h_out_ref, xn_ref, ri_ref, rf_ref, cnt_ref, base_ref, wsplit_ref):
    i = pl.program_id(0)

    @pl.when(i == 0)
    def _():
        _route_init(wrt_ref, wsplit_ref, base_ref)

    h = h_ref[...] + _dot(o_ref[...], wo_ref[...])
    h_out_ref[...] = h
    xn = _rms(h, fnorm_ref[...])
    _store_token_tiles(xn_ref, _pack_rows(xn))
    valid = lax.broadcasted_iota(jnp.int32, (1, h.shape[0]), 1) >= 0
    _route(xn, valid, wsplit_ref, br_ref, triu_ref, base_ref, ri_ref, rf_ref, cnt_ref)


def _full(shape):
    nd = len(shape)
    return pl.BlockSpec(shape, lambda *_: (0,) * nd)


def _router_operands(router_g, router_g_bias, router_e, router_e_bias):
    d = router_g.shape[0]
    wrt = jnp.concatenate([router_g.T, jnp.zeros((8 - N_EXPERT_GROUPS, d), _F32), router_e.T], axis=0)
    br = jnp.concatenate([router_g_bias, jnp.zeros((8 - N_EXPERT_GROUPS,), _F32), router_e_bias])[:, None]
    return wrt.astype(_F32), br.astype(_F32)


def _route_out(n_tiles, tm):
    shapes = [jax.ShapeDtypeStruct((n_tiles, 8, tm), jnp.int32),
              jax.ShapeDtypeStruct((n_tiles, 8, tm), _F32),
              jax.ShapeDtypeStruct((N_EXPERTS, 128), jnp.int32)]
    specs = [pl.BlockSpec((1, 8, tm), lambda i: (i, 0, 0)),
             pl.BlockSpec((1, 8, tm), lambda i: (i, 0, 0)),
             pl.BlockSpec((N_EXPERTS, 128), lambda i: (0, 0))]
    return shapes, specs


def _moe_experts(xn, ri, counts, w_gate, w_up, w_down, layer, n_tiles, n_real_tiles, d):
    tm, tr = TOKEN_TILE, EXPERT_ROWS
    parts = d // (2 * LANES)
    n_valid = n_real_tiles * tm + (n_tiles - n_real_tiles) * N_META
    n_blocks = -(-(2 * n_valid + N_EXPERTS * (tr - 1)) // tr)
    n_blocks = -(-n_blocks // EXPERT_BLOCKS_PER_STEP) * EXPERT_BLOCKS_PER_STEP
    n_rows = n_blocks * tr

    counts = counts[:, 0]
    padded = (counts + tr - 1) // tr * tr
    pends = jnp.cumsum(padded)
    pstarts = pends - padded
    n_used = (pends[-1] // tr).astype(jnp.int32).reshape(1)
    blk0 = jnp.arange(n_blocks, dtype=jnp.int32) * tr
    block_e = jnp.minimum(jnp.sum(blk0[:, None] >= pends[None, :], axis=1), N_EXPERTS - 1).astype(jnp.int32)
    experts = jnp.arange(N_EXPERTS, dtype=jnp.int32)
    block_oh = block_e[:, None] == experts[None, :]
    block_cnt = jnp.sum(jnp.where(block_oh, counts[None, :], 0), axis=1)
    block_start = jnp.sum(jnp.where(block_oh, pstarts[None, :], 0), axis=1)
    block_valid = jnp.clip(block_cnt - (blk0 - block_start), 0, tr).astype(jnp.int32)
    eid = ri[:, 0:2, :]
    slot0 = jnp.sum(jnp.where(eid[..., None] == experts, pstarts, 0), axis=-1)
    dest = (slot0 + ri[:, 2:4, :]).astype(jnp.int32).reshape(n_tiles * 2 * tm)

    pad_start = (pstarts + counts).astype(jnp.int32)
    pad_len = (padded - counts).astype(jnp.int32)
    xs = pl.pallas_call(
        functools.partial(_dispatch_kernel, tm=tm, tr=tr, n_tiles=n_tiles, n_real_tiles=n_real_tiles,
                          n_blocks=n_blocks),
        grid_spec=pltpu.PrefetchScalarGridSpec(
            num_scalar_prefetch=3,
            grid=(n_tiles,),
            in_specs=[pl.BlockSpec(memory_space=pl.ANY), pl.BlockSpec(memory_space=pl.ANY)],
            out_specs=pl.BlockSpec(memory_space=pl.ANY),
            scratch_shapes=[pltpu.SMEM((4 * tm,), jnp.int32), pltpu.VMEM((3, tm * parts, LANES), jnp.uint32),
                            pltpu.VMEM((tr * parts, LANES), jnp.uint32),
                            pltpu.SemaphoreType.DMA((2,)), pltpu.SemaphoreType.DMA((3,)),
                            pltpu.SemaphoreType.DMA((3,)), pltpu.SemaphoreType.DMA]),
        out_shape=jax.ShapeDtypeStruct((n_rows * parts, LANES), jnp.uint32),
        compiler_params=_params(1, has_side_effects=True, disable_bounds_checks=True),
        name="moe_dispatch",
    )(pad_start, pad_len, pends[-1:].astype(jnp.int32), dest, xn)

    f = w_gate.shape[3]
    nb = EXPERT_BLOCKS_PER_STEP
    step_rows = nb * tr * parts

    def expert_block(c):
        return lambda i, be, bv, nu: (layer, be[jnp.minimum(i * nb + c, nu[0] - 1)], 0, 0)

    weight_specs, weight_scratch = [], []
    for c in range(nb):
        weight_specs += [pl.BlockSpec((1, 1, d, f), expert_block(c)), pl.BlockSpec((1, 1, d, f), expert_block(c)),
                         pl.BlockSpec((1, 1, f, d), expert_block(c))]
        weight_scratch += [pltpu.VMEM((d, f), _BF16), pltpu.VMEM((d, f), _BF16), pltpu.VMEM((f, d), _BF16)]
    y = pl.pallas_call(
        _expert_kernel,
        grid_spec=pltpu.PrefetchScalarGridSpec(
            num_scalar_prefetch=3,
            grid=(n_blocks // nb,),
            in_specs=[pl.BlockSpec((step_rows, LANES),
                                   lambda i, be, bv, nu: (jnp.minimum(i, (nu[0] - 1) // nb), 0))] + weight_specs,
            out_specs=pl.BlockSpec((step_rows, LANES), lambda i, be, bv, nu: (i, 0)),
            scratch_shapes=weight_scratch),
        out_shape=jax.ShapeDtypeStruct((n_rows * parts, LANES), jnp.uint32),
        compiler_params=_params(1),
        name="moe_experts",
    )(block_e, block_valid, n_used, xs, *([w_gate, w_up, w_down] * nb))

    return dest, y


def _combine_operands(dest, rf, h, y, d):
    tm = TOKEN_TILE
    parts = d // (2 * LANES)
    in_specs = [pl.BlockSpec(memory_space=pl.ANY),
                pl.BlockSpec((1, 8, tm), lambda i: (i, 0, 0)),
                pl.BlockSpec((tm, d), lambda i: (i, 0)),
                pl.BlockSpec(memory_space=pl.ANY)]
    scratch = [pltpu.SMEM((4 * tm,), jnp.int32),
               pltpu.VMEM((2 * tm * parts, LANES), jnp.uint32), pltpu.VMEM((2 * tm * parts, LANES), jnp.uint32),
               pltpu.SemaphoreType.DMA((2,)), pltpu.SemaphoreType.DMA((2,))]
    return (dest, rf, h, y), in_specs, scratch


def kernel(x, meta_tokens, a_norm, a_w, a_scale, b_norm, b_w_dq, b_q_norm, b_w_uq, b_w_o, kv_norm, w_dkv,
           kv_lat_norm, w_uk, w_uv, ffn_norm, router_g, router_g_bias, router_e, router_e_bias, w_gate, w_up,
           w_down, final_norm):
    bsz, seq, d = x.shape
    tm = TOKEN_TILE
    assert seq % tm == 0 and seq % ATTN_Q_TILE == 0 and ATTN_Q_TILE == 2 * ATTN_K_TILE and tm % ATTN_K_TILE == 0
    kt = tm // ATTN_K_TILE
    assert d % (LANES * len(POOL_WINDOWS)) == 0 and N_META == max(POOL_WINDOWS) and N_META <= tm
    parts = d // (2 * LANES)
    n_tok = bsz * seq
    n_real_tiles = n_tok // tm
    tiles_per_batch = seq // tm
    n_tiles = n_real_tiles + 1
    gd = d // len(POOL_WINDOWS)
    row = lambda v: v.reshape(1, -1).astype(_F32)

    x2 = x.reshape(n_tok, d)
    meta_pad = jnp.concatenate([meta_tokens, jnp.zeros((tm - N_META, d), x.dtype)], axis=0)

    r = np.arange(tm)[:, None]
    cidx = np.arange(tm)[None, :]
    pb = min(POOL_BLOCK, tm)
    pm = np.stack([((r[:pb] - cidx[:, :pb] >= 0) & (r[:pb] - cidx[:, :pb] < w)) for w in POOL_WINDOWS]).astype(np.float32)
    hc = np.arange(N_META)[None, :]
    ph = np.stack([(r[:N_META] + N_META - hc < w) for w in POOL_WINDOWS]).astype(np.float32)
    triu = (r < cidx).astype(np.float32)
    pm, ph, triu = (jnp.asarray(a, dtype=_BF16) for a in (pm, ph, triu))

    wrt0, br0 = _router_operands(router_g[0], router_g_bias[0], router_e[0], router_e_bias[0])
    route_shapes, route_specs = _route_out(n_tiles, tm)
    tile_or_last = lambda i: (jnp.minimum(i, n_real_tiles - 1), 0)
    halo_blocks = tm // N_META
    h1, xn1, ri1, rf1, cnt1 = pl.pallas_call(
        functools.partial(_mixer_kernel, n_real_tiles=n_real_tiles, tiles_per_batch=tiles_per_batch),
        grid=(n_tiles,),
        in_specs=[pl.BlockSpec((tm, d), tile_or_last),
                  pl.BlockSpec((N_META, d), lambda i: (jnp.clip(i * halo_blocks - 1, 0, n_tok // N_META - 1), 0)),
                  _full((tm, d)), _full((N_META, d)), _full((1, d)),
                  _full((len(POOL_WINDOWS), gd, gd)), _full((1, d)), _full((1, d)),
                  _full((ROUTER_ROWS, d)), _full((ROUTER_ROWS, 1)),
                  _full(pm.shape), _full(ph.shape), _full((tm, tm))],
        out_specs=[pl.BlockSpec((tm, d), lambda i: (i, 0)),
                   pl.BlockSpec((tm * parts, LANES), lambda i: (i, 0))] + route_specs,
        out_shape=[jax.ShapeDtypeStruct((n_tiles * tm, d), _F32),
                   jax.ShapeDtypeStruct((n_tiles * tm * parts, LANES), jnp.uint32)] + route_shapes,
        scratch_shapes=[pltpu.VMEM((N_EXPERTS, 1), _F32), pltpu.VMEM((ROUTER_SPLIT_ROWS, d), _BF16)],
        compiler_params=_params(1),
        name="pool_mixer_router",
    )(x2, x2, meta_pad, meta_tokens, row(a_norm[0]), a_w[0].astype(_BF16), row(a_scale[0]), row(ffn_norm[0]),
      wrt0, br0, pm, ph, triu)

    dest1, y1 = _moe_experts(xn1, ri1, cnt1, w_gate, w_up, w_down, 0, n_tiles, n_real_tiles, d)

    pos = jnp.concatenate([jnp.arange(seq, dtype=_F32) + N_META, jnp.arange(tm, dtype=_F32)])
    inv_freq = ROPE_THETA ** (-jnp.arange(0, QK_ROPE_DIM, 2, dtype=_F32) / QK_ROPE_DIM)
    ang = pos[:, None] * inv_freq[None, :]
    cos_t = jnp.tile(jnp.cos(ang), (1, 4))
    sin_t = jnp.tile(jnp.concatenate([-jnp.sin(ang), jnp.sin(ang)], axis=1), (1, 2))

    wdkv = jnp.concatenate([w_dkv, w_dkv[:, KV_LORA_RANK:]], axis=1).astype(_BF16)
    wuk = w_uk.reshape(KV_LORA_RANK, N_HEADS * QK_NOPE_DIM).astype(_BF16)
    wuv = w_uv.reshape(KV_LORA_RANK, N_HEADS * V_HEAD_DIM).T.astype(_BF16)
    wuq = b_w_uq[0]
    q_rank = wuq.shape[0]
    wuq = jnp.concatenate([wuq[:, :, :QK_NOPE_DIM].reshape(q_rank, -1),
                           wuq[:, :, QK_NOPE_DIM:].reshape(q_rank, -1)], axis=1).astype(_BF16)
    wdq = b_w_dq[0].astype(_BF16)
    pos_tile = lambda i: (jnp.where(i < n_real_tiles, i % tiles_per_batch, tiles_per_batch), 0)
    head_tile = lambda i: (0, i, 0)
    operands, in_specs, scratch = _combine_operands(dest1, rf1, h1, y1, d)
    h2, q, k, v = pl.pallas_call(
        functools.partial(_combine_proj_kernel, tm=tm, n_tiles=n_tiles),
        grid=(n_tiles,),
        in_specs=in_specs + [
            pl.BlockSpec((tm, 128), pos_tile), pl.BlockSpec((tm, 128), pos_tile),
            _full((1, d)), _full(wdkv.shape), _full((1, KV_LORA_RANK)), _full(wuk.shape), _full(wuv.shape),
            _full((1, d)), _full(wdq.shape), _full((1, q_rank)), _full(wuq.shape)],
        out_specs=[pl.BlockSpec((tm, d), lambda i: (i, 0)),
                   pl.BlockSpec((N_HEADS, tm, QK_DIM), head_tile),
                   pl.BlockSpec((N_HEADS, tm, QK_DIM), head_tile),
                   pl.BlockSpec((N_HEADS, kt, V_HEAD_DIM, ATTN_K_TILE), lambda i: (0, i, 0, 0))],
        out_shape=[jax.ShapeDtypeStruct((n_tiles * tm, d), _F32),
                   jax.ShapeDtypeStruct((N_HEADS, n_tiles * tm, QK_DIM), _BF16),
                   jax.ShapeDtypeStruct((N_HEADS, n_tiles * tm, QK_DIM), _BF16),
                   jax.ShapeDtypeStruct((N_HEADS, n_tiles * kt, V_HEAD_DIM, ATTN_K_TILE), _BF16)],
        scratch_shapes=scratch,
        compiler_params=_params(1, disable_bounds_checks=True),
        name="moe_combine_latent_qkv",
    )(*operands, cos_t, sin_t, row(kv_norm), wdkv, row(kv_lat_norm), wuk, wuv,
      row(b_norm[0]), wdq, row(b_q_norm[0]), wuq)

    tq, tk, hp = ATTN_Q_TILE, ATTN_K_TILE, ATTN_HEADS_PER_STEP
    n_q = seq // tq
    meta_block = n_tok // N_META
    o = pl.pallas_call(
        functools.partial(_attn_kernel, tq=tq, tk=tk),
        grid=(bsz, N_HEADS // hp, n_q),
        in_specs=[pl.BlockSpec((hp, tq, QK_DIM), lambda b, hg, j: (hg, b * n_q + j, 0)),
                  pl.BlockSpec((hp, seq, QK_DIM), lambda b, hg, j: (hg, b, 0)),
                  pl.BlockSpec((hp, seq // tk, V_HEAD_DIM, tk), lambda b, hg, j: (hg, b, 0, 0)),
                  pl.BlockSpec((hp, N_META, QK_DIM), lambda b, hg, j: (hg, meta_block, 0)),
                  pl.BlockSpec((hp, 1, V_HEAD_DIM, tk), lambda b, hg, j: (hg, n_real_tiles * kt, 0, 0))],
        out_specs=pl.BlockSpec((tq, hp * V_HEAD_DIM), lambda b, hg, j: (b * n_q + j, hg)),
        out_shape=jax.ShapeDtypeStruct((n_tok, N_HEADS * V_HEAD_DIM), _BF16),
        scratch_shapes=[pltpu.VMEM((hp, tk, tq), _F32), pltpu.VMEM((hp, tk, tq), _F32),
                        pltpu.VMEM((hp, V_HEAD_DIM, tq), _F32)],
        compiler_params=_params(3),
        name="causal_attention",
    )(q, k, v, k, v)

    wrt1, br1 = _router_operands(router_g[1], router_g_bias[1], router_e[1], router_e_bias[1])
    route_shapes, route_specs = _route_out(n_real_tiles, tm)
    h3, xn2, ri2, rf2, cnt2 = pl.pallas_call(
        _oproj_kernel,
        grid=(n_real_tiles,),
        in_specs=[pl.BlockSpec((tm, N_HEADS * V_HEAD_DIM), lambda i: (i, 0)),
                  pl.BlockSpec((tm, d), lambda i: (i, 0)),
                  _full((N_HEADS * V_HEAD_DIM, d)), _full((1, d)),
                  _full((ROUTER_ROWS, d)), _full((ROUTER_ROWS, 1)), _full((tm, tm))],
        out_specs=[pl.BlockSpec((tm, d), lambda i: (i, 0)),
                   pl.BlockSpec((tm * parts, LANES), lambda i: (i, 0))] + route_specs,
        out_shape=[jax.ShapeDtypeStruct((n_tok, d), _F32),
                   jax.ShapeDtypeStruct((n_tok * parts, LANES), jnp.uint32)] + route_shapes,
        scratch_shapes=[pltpu.VMEM((N_EXPERTS, 1), _F32), pltpu.VMEM((ROUTER_SPLIT_ROWS, d), _BF16)],
        compiler_params=_params(1),
        name="attn_out_router",
    )(o, h2, b_w_o[0].astype(_BF16), row(ffn_norm[1]), wrt1, br1, triu)

    dest2, y2 = _moe_experts(xn2, ri2, cnt2, w_gate, w_up, w_down, 1, n_real_tiles, n_real_tiles, d)
    operands, in_specs, scratch = _combine_operands(dest2, rf2, h3, y2, d)
    out = pl.pallas_call(
        functools.partial(_combine_norm_kernel, tm=tm, n_tiles=n_real_tiles),
        grid=(n_real_tiles,),
        in_specs=in_specs + [_full((1, d))],
        out_specs=pl.BlockSpec((tm, d), lambda i: (i, 0)),
        out_shape=jax.ShapeDtypeStruct((n_tok, d), _F32),
        scratch_shapes=scratch,
        compiler_params=_params(1, disable_bounds_checks=True),
        name="moe_combine_final_norm",
    )(*operands, row(final_norm))
    return out.reshape(bsz, seq, d)
```

```python
import functools

import numpy as np
import jax
import jax.numpy as jnp
from jax import lax
from jax.experimental import pallas as pl
from jax.experimental.pallas import tpu as pltpu

N_META = 16
POOL_WINDOWS = (2, 4, 8, 16)
N_HEADS = 8
QK_NOPE_DIM = 128
QK_ROPE_DIM = 64
QK_DIM = QK_NOPE_DIM + QK_ROPE_DIM
V_HEAD_DIM = 128
KV_LORA_RANK = 256
ROPE_THETA = 10000.0
ATTN_SCALE = QK_DIM ** -0.5
Q_SCALE = ATTN_SCALE * 1.4426950408889634
N_EXPERT_GROUPS = 4
EXPERTS_PER_GROUP = 8
N_EXPERTS = N_EXPERT_GROUPS * EXPERTS_PER_GROUP
RMS_EPS = 1e-6
NEG_INF = -1e30

TOKEN_TILE = 512
POOL_BLOCK = 256
EXPERT_ROWS = 256
EXPERT_BLOCKS_PER_STEP = 2
ATTN_Q_TILE = 512
ATTN_K_TILE = ATTN_Q_TILE // 2
ATTN_HEADS_PER_STEP = 4
DMA_UNROLL = 8
DMA_THREADS = 2
ROUTER_ROWS = 8 + N_EXPERTS
ROUTER_LO_ROW = 48
ROUTER_SPLIT_ROWS = 2 * ROUTER_LO_ROW
VMEM_LIMIT_BYTES = 48 * 1024 * 1024

LANES = 128
_F32 = jnp.float32
_BF16 = jnp.bfloat16
_NT_DIMS = (((1,), (1,)), ((), ()))


def _params(n_grid_dims=1, **kw):
    return pltpu.CompilerParams(dimension_semantics=("arbitrary",) * n_grid_dims,
                                vmem_limit_bytes=VMEM_LIMIT_BYTES, **kw)


def _rms(x, g):
    ms = jnp.mean(x * x, axis=-1, keepdims=True)
    return x * lax.rsqrt(ms + RMS_EPS) * g


def _split_bf16(x):
    hi = x.astype(_BF16)
    lo = (x - hi.astype(_F32)).astype(_BF16)
    return hi, lo


def _dot(a, b):
    return jnp.dot(a, b, preferred_element_type=_F32)


def _store_token_tiles(ref, x):
    n, d = x.shape
    parts = d // LANES
    for s in range(parts):
        ref[pl.ds(s, n, stride=parts), :] = x[:, s * LANES:(s + 1) * LANES]


def _load_token_tiles(ref, first_token, n, parts):
    return jnp.concatenate([ref[pl.ds(first_token * parts + s, n, stride=parts), :] for s in range(parts)], axis=1)


_HIGH_HALF = 0xFFFF0000


def _pack_rows(x):
    n, d = x.shape
    words = []
    for s in range(d // (2 * LANES)):
        lo = x[:, 2 * s * LANES:(2 * s + 1) * LANES].astype(_BF16).astype(_F32)
        hi = x[:, (2 * s + 1) * LANES:(2 * s + 2) * LANES].astype(_BF16).astype(_F32)
        words.append(lax.shift_right_logical(lax.bitcast_convert_type(lo, jnp.uint32), jnp.uint32(16))
                     | (lax.bitcast_convert_type(hi, jnp.uint32) & jnp.uint32(_HIGH_HALF)))
    return jnp.concatenate(words, axis=1)


def _unpack_rows(w):
    cols = []
    for s in range(w.shape[1] // LANES):
        ws = w[:, s * LANES:(s + 1) * LANES]
        cols.append(lax.bitcast_convert_type(lax.shift_left(ws, jnp.uint32(16)), _F32))
        cols.append(lax.bitcast_convert_type(ws & jnp.uint32(_HIGH_HALF), _F32))
    return jnp.concatenate(cols, axis=1)


def _route_init(wrt_ref, wsplit_ref, base_ref):
    base_ref[...] = jnp.zeros(base_ref.shape, base_ref.dtype)
    w_hi, w_lo = _split_bf16(wrt_ref[...])
    wsplit_ref[...] = jnp.zeros(wsplit_ref.shape, wsplit_ref.dtype)
    wsplit_ref[0:ROUTER_ROWS, :] = w_hi
    wsplit_ref[ROUTER_LO_ROW:ROUTER_LO_ROW + ROUTER_ROWS, :] = w_lo


def _route(xn, valid, wsplit_ref, br_ref, triu_ref, base_ref, ri_ref, rf_ref, cnt_ref):
    tm = xn.shape[0]
    x_hi, x_lo = _split_bf16(xn)
    by_hi = lax.dot_general(wsplit_ref[...], x_hi, _NT_DIMS, preferred_element_type=_F32)
    by_lo = lax.dot_general(wsplit_ref[0:ROUTER_LO_ROW, :], x_lo, _NT_DIMS, preferred_element_type=_F32)
    logits = (by_hi[0:ROUTER_ROWS] + by_hi[ROUTER_LO_ROW:ROUTER_LO_ROW + ROUTER_ROWS]
              + by_lo[0:ROUTER_ROWS] + br_ref[...])
    lg = logits[0:N_EXPERT_GROUPS]
    eg = jnp.exp(lg - jnp.max(lg, axis=0, keepdims=True))
    pg = eg / jnp.sum(eg, axis=0, keepdims=True)
    w_g = jnp.max(pg, axis=0, keepdims=True)
    ig = lax.broadcasted_iota(jnp.int32, pg.shape, 0).astype(_F32)
    g_sel = jnp.min(jnp.where(pg == w_g, ig, float(N_EXPERT_GROUPS)), axis=0, keepdims=True)

    sel = logits[8:8 + EXPERTS_PER_GROUP]
    for g in range(1, N_EXPERT_GROUPS):
        sel = jnp.where(g_sel == float(g), logits[8 + g * EXPERTS_PER_GROUP:8 + (g + 1) * EXPERTS_PER_GROUP], sel)
    ie = lax.broadcasted_iota(jnp.int32, sel.shape, 0).astype(_F32)
    v1 = jnp.max(sel, axis=0, keepdims=True)
    i1 = jnp.min(jnp.where(sel == v1, ie, float(EXPERTS_PER_GROUP)), axis=0, keepdims=True)
    rest = jnp.where(ie == i1, -jnp.inf, sel)
    v2 = jnp.max(rest, axis=0, keepdims=True)
    i2 = jnp.min(jnp.where(rest == v2, ie, float(EXPERTS_PER_GROUP)), axis=0, keepdims=True)
    e2 = jnp.exp(v2 - v1)
    den = 1.0 + e2
    validf = valid.astype(_F32)
    gate0 = w_g * (1.0 / den) * validf
    gate1 = w_g * (e2 / den) * validf
    eid0 = g_sel * float(EXPERTS_PER_GROUP) + i1
    eid1 = g_sel * float(EXPERTS_PER_GROUP) + i2

    iall = lax.broadcasted_iota(jnp.int32, (N_EXPERTS, tm), 0).astype(_F32)
    oh0 = jnp.where(iall == eid0, validf, 0.0)
    oh1 = jnp.where(iall == eid1, validf, 0.0)
    both = oh0 + oh1
    before = _dot(both.astype(_BF16), triu_ref[...]) + base_ref[...]
    rank0 = jnp.sum(oh0 * before, axis=0, keepdims=True)
    rank1 = jnp.sum(oh1 * before, axis=0, keepdims=True)
    base_ref[...] = base_ref[...] + jnp.sum(both, axis=1, keepdims=True)

    ri_ref[...] = jnp.zeros(ri_ref.shape, ri_ref.dtype)
    rf_ref[...] = jnp.zeros(rf_ref.shape, rf_ref.dtype)
    ri_ref[0, 0:1, :] = eid0.astype(jnp.int32)
    ri_ref[0, 1:2, :] = eid1.astype(jnp.int32)
    ri_ref[0, 2:3, :] = rank0.astype(jnp.int32)
    ri_ref[0, 3:4, :] = rank1.astype(jnp.int32)
    rf_ref[0, 0:1, :] = gate0
    rf_ref[0, 1:2, :] = gate1
    cnt_ref[...] = jnp.broadcast_to(base_ref[...], cnt_ref.shape).astype(jnp.int32)


def _mixer_kernel(x_ref, xh_ref, mp_ref, meta_ref, anorm_ref, aw_ref, ascale_ref, fnorm_ref,
                  wrt_ref, br_ref, pm_ref, ph_ref, triu_ref,
                  h_ref, xn_ref, ri_ref, rf_ref, cnt_ref, base_ref, wsplit_ref, *, n_real_tiles, tiles_per_batch):
    i = pl.program_id(0)
    tm = x_ref.shape[0]
    gd = x_ref.shape[1] // len(POOL_WINDOWS)
    is_meta = i == n_real_tiles
    first = (i % tiles_per_batch) == 0

    @pl.when(i == 0)
    def _():
        _route_init(wrt_ref, wsplit_ref, base_ref)

    h = jnp.where(is_meta, mp_ref[...], x_ref[...])
    halo = jnp.where(is_meta, 0.0, jnp.where(first, meta_ref[...], xh_ref[...]))
    hn = _rms(h, anorm_ref[...])
    hh = _rms(halo, anorm_ref[...])
    hn_hi, hn_lo = _split_bf16(hn)
    hh_hi, hh_lo = _split_bf16(hh)
    pb = pm_ref.shape[1]
    wins = {}
    for c in range(tm // pb):
        rows = slice(c * pb, (c + 1) * pb)
        for g in range(len(POOL_WINDOWS)):
            sl = slice(g * gd, (g + 1) * gd)
            win = _dot(pm_ref[g], hn_hi[rows, sl]) + _dot(pm_ref[g], hn_lo[rows, sl])
            if c == 0:
                before_hi, before_lo = hh_hi[:, sl], hh_lo[:, sl]
            else:
                before_hi, before_lo = hn_hi[c * pb - N_META:c * pb, sl], hn_lo[c * pb - N_META:c * pb, sl]
            top = win[:N_META] + _dot(ph_ref[g], before_hi) + _dot(ph_ref[g], before_lo)
            wins[c, g] = jnp.concatenate([top, win[N_META:]], axis=0)
    for c in range(tm // pb):
        rows = slice(c * pb, (c + 1) * pb)
        row = c * pb + lax.broadcasted_iota(jnp.int32, (pb, 1), 0)
        for g, w in enumerate(POOL_WINDOWS):
            sl = slice(g * gd, (g + 1) * gd)
            cnt = jnp.where(is_meta, jnp.minimum(row + 1, w), w).astype(_F32)
            pooled = wins[c, g] * (1.0 / cnt) - hn[rows, sl]
            mix = _dot(pooled.astype(_BF16), aw_ref[g])
            h_ref[rows, sl] = h[rows, sl] + mix * ascale_ref[:, sl]

    xn = _rms(h_ref[...], fnorm_ref[...])
    _store_token_tiles(xn_ref, _pack_rows(xn))
    lane = lax.broadcasted_iota(jnp.int32, (1, tm), 1)
    valid = jnp.logical_or(jnp.logical_not(is_meta), lane < N_META)
    _route(xn, valid, wsplit_ref, br_ref, triu_ref, base_ref, ri_ref, rf_ref, cnt_ref)


def _index_copy(dest_hbm, idx_smem, isem, tile, n):
    return pltpu.make_async_copy(dest_hbm.at[pl.ds(pl.multiple_of(tile * n, n), n)],
                                 idx_smem.at[pl.ds(pl.multiple_of((tile % 2) * n, n), n)], isem.at[tile % 2])


def _dispatch_kernel(pad_start_ref, pad_len_ref, tail_ref, dest_hbm, xn_hbm, xs_hbm,
                     idx_smem, xbuf, zbuf, isem, lsem, ssem, zsem, *, tm, tr, pad_unit, n_tiles, n_real_tiles, n_blocks):
    i = pl.program_id(0)
    parts = xbuf.shape[1] // tm
    buf = i % 3

    def zero_fill(wait):
        def zero_rows(first_row, n_rows):
            cp = pltpu.make_async_copy(
                zbuf.at[pl.ds(0, n_rows * parts), :],
                xs_hbm.at[pl.ds(pl.multiple_of(first_row * parts, parts), n_rows * parts), :], zsem)
            if wait:
                cp.wait()
            else:
                cp.start()

        def per_expert(e, c):
            first, length = pad_start_ref[e], pad_len_ref[e]
            for bit in range((pad_unit - 1).bit_length()):
                @pl.when((length >> bit) % 2 == 1)
                def _(bit=bit):
                    zero_rows(first + ((length >> (bit + 1)) << (bit + 1)), 1 << bit)
            return c

        def per_block(blk, c):
            zero_rows(blk * tr, tr)
            return c

        lax.fori_loop(0, N_EXPERTS, per_expert, 0)
        lax.fori_loop(tail_ref[0] // tr, n_blocks, per_block, 0)

    def idx_copy(tile):
        return _index_copy(dest_hbm, idx_smem, isem, tile, 2 * tm)

    def load(tile):
        rows = tm * parts
        return pltpu.make_async_copy(xn_hbm.at[pl.ds(pl.multiple_of(tile * rows, rows), rows), :],
                                     xbuf.at[tile % 3], lsem.at[tile % 3])

    def wait_scatters(tile):
        def wait_rows(n_tok):
            for k in range(2):
                pltpu.make_async_copy(xbuf.at[tile % 3, pl.ds(0, n_tok * parts), :],
                                      xs_hbm.at[pl.ds(0, n_tok * parts), :], ssem.at[tile % 3]).wait()

        @pl.when(tile < n_real_tiles)
        def _():
            wait_rows(tm)

        @pl.when(tile >= n_real_tiles)
        def _():
            wait_rows(N_META)

    def scatter(n_tok):
        def issue(g, c):
            for u in range(min(DMA_UNROLL, n_tok)):
                t = g * min(DMA_UNROLL, n_tok) + u
                src = xbuf.at[buf, pl.ds(pl.multiple_of(t * parts, parts), parts), :]
                for k in range(2):
                    d = idx_smem[(i % 2) * (2 * tm) + k * tm + t]
                    pltpu.make_async_copy(src, xs_hbm.at[pl.ds(pl.multiple_of(d * parts, parts), parts), :],
                                          ssem.at[buf]).start(priority=(2 * u + k) % DMA_THREADS)
            return c

        lax.fori_loop(0, n_tok // min(DMA_UNROLL, n_tok), issue, 0)

    @pl.when(i == 0)
    def _():
        idx_copy(0).start()
        load(0).start()
        zbuf[...] = jnp.zeros(zbuf.shape, zbuf.dtype)
        zero_fill(wait=False)

    @pl.when(i >= 2)
    def _():
        wait_scatters(i - 2)

    @pl.when(i + 1 < n_tiles)
    def _():
        idx_copy(i + 1).start()
        load(i + 1).start()

    idx_copy(i).wait()
    load(i).wait()

    @pl.when(i < n_real_tiles)
    def _():
        scatter(tm)

    @pl.when(i >= n_real_tiles)
    def _():
        scatter(N_META)

    @pl.when(i == n_tiles - 1)
    def _():
        if n_tiles >= 2:
            wait_scatters(i - 1)
        wait_scatters(i)
        zero_fill(wait=True)


def _expert_kernel(be_ref, bv_ref, nu_ref, xs_ref, *refs):
    nb = EXPERT_BLOCKS_PER_STEP
    w_refs, y_ref, wb_refs = refs[:3], refs[3], refs[4:]
    i = pl.program_id(0)
    n_used = nu_ref[0]
    parts = wb_refs[0].shape[0] // (2 * LANES)
    tr = xs_ref.shape[0] // (nb * parts)
    changed = be_ref[i * nb] != be_ref[jnp.maximum(i * nb - nb, 0)]

    @pl.when(jnp.logical_or(i == 0, jnp.logical_and(i * nb < n_used, changed)))
    def _():
        for w_ref, wb_ref in zip(w_refs, wb_refs):
            wb_ref[...] = w_ref[0, 0].astype(_BF16)

    @pl.when(i * nb < n_used)
    def _():
        rows = lax.broadcasted_iota(jnp.int32, (tr, 1), 0)
        xs, gs, us, ys = [], [], [], []
        for c in range(nb):
            x = jnp.where(rows < bv_ref[i * nb + c], _unpack_rows(_load_token_tiles(xs_ref, c * tr, tr, parts)), 0.0)
            xs.append(x.astype(_BF16))
        for c in range(nb):
            gs.append(_dot(xs[c], wb_refs[0][...]))
            us.append(_dot(xs[c], wb_refs[1][...]))
        for c in range(nb):
            a = gs[c] * (1.0 / (1.0 + jnp.exp(-gs[c]))) * us[c]
            ys.append(_dot(a.astype(_BF16), wb_refs[2][...]))
        for c in range(nb):
            y = jnp.where(i * nb + c < n_used, ys[c], 0.0)
            _store_token_tiles(y_ref.at[pl.ds(c * tr * parts, tr * parts), :], _pack_rows(y))

    @pl.when(i * nb >= n_used)
    def _():
        y_ref[...] = jnp.zeros(y_ref.shape, y_ref.dtype)


def _combined_tile(dest_hbm, rf_ref, h_ref, y_hbm, idx_smem, ybufs, isem, rsem, *, tm, n_tiles, slot):
    i = pl.program_id(0)
    parts = ybufs[0].shape[0] // (2 * tm)

    def idx_copy(tile):
        return _index_copy(dest_hbm, idx_smem, isem, tile, 2 * tm)

    def row_copy(idx_slot, buf_slot, t, k, u):
        d = idx_smem[idx_slot * (2 * tm) + k * tm + t]
        row = (k * tm + t) * parts
        if not isinstance(t, int):
            row = pl.multiple_of(row, parts)
        return pltpu.make_async_copy(
            y_hbm.at[pl.ds(pl.multiple_of(d * parts, parts), parts), :],
            ybufs[buf_slot].at[pl.ds(row, parts), :],
            rsem.at[buf_slot]).start(priority=(2 * u + k) % DMA_THREADS)

    def wait_rows(buf_slot):
        pltpu.make_async_copy(y_hbm.at[pl.ds(0, 2 * tm * parts), :], ybufs[buf_slot], rsem.at[buf_slot]).wait()

    if slot == 0:
        @pl.when(i == 0)
        def _():
            idx_copy(0).start()
            idx_copy(0).wait()

            def issue(g, c):
                for u in range(DMA_UNROLL):
                    for k in range(2):
                        row_copy(0, 0, g * DMA_UNROLL + u, k, u)
                return c

            lax.fori_loop(0, tm // DMA_UNROLL, issue, 0)
            if n_tiles >= 2:
                idx_copy(1).start()

    @pl.when(i + 1 < n_tiles)
    def _():
        idx_copy(i + 1).wait()

    @pl.when(i + 2 < n_tiles)
    def _():
        idx_copy(i + 2).start()

    nxt_idx_slot = jnp.minimum(i + 1, n_tiles - 1) % 2

    def prefetch(part, n_parts):
        for t in range(part * tm // n_parts, (part + 1) * tm // n_parts):
            for k in range(2):
                row_copy(nxt_idx_slot, 1 - slot, t, k, t)

    gates = rf_ref[0]
    gt = jnp.concatenate([gates, jnp.zeros((LANES - gates.shape[0], tm), _F32)], axis=0).T
    wait_rows(slot)
    yb = ybufs[slot]

    def combined(r0, n):
        return h_ref[r0:r0 + n, :] + (_unpack_rows(_load_token_tiles(yb, r0, n, parts)) * gt[r0:r0 + n, 0:1]
                                      + _unpack_rows(_load_token_tiles(yb, tm + r0, n, parts)) * gt[r0:r0 + n, 1:2])

    def drain():
        @pl.when(i == n_tiles - 1)
        def _():
            wait_rows(1 - slot)

    return combined, prefetch, drain


def _for_each_parity(body):
    for slot in range(2):
        @pl.when(pl.program_id(0) % 2 == slot)
        def _(slot=slot):
            body(slot)


def _combine_norm_kernel(dest_hbm, rf_ref, h_ref, y_hbm, fnorm_ref, out_ref,
                         idx_smem, ybuf0, ybuf1, isem, rsem, *, tm, n_tiles):
    def body(slot):
        combined, prefetch, drain = _combined_tile(dest_hbm, rf_ref, h_ref, y_hbm, idx_smem, (ybuf0, ybuf1),
                                                   isem, rsem, tm=tm, n_tiles=n_tiles, slot=slot)
        n_chunks = 4
        n = tm // n_chunks
        for c in range(n_chunks):
            prefetch(c, n_chunks)
            out_ref[c * n:(c + 1) * n, :] = _rms(combined(c * n, n), fnorm_ref[...])
        drain()

    _for_each_parity(body)


def _combine_proj_kernel(dest_hbm, rf_ref, h_ref, y_hbm, cos_ref, sin_ref, kvn_ref, wdkv_ref, kvlat_ref,
                         wuk_ref, wuv_ref, bnorm_ref, wdq_ref, qnorm_ref, wuq_ref,
                         h_out_ref, q_ref, k_ref, v_ref, idx_smem, ybuf0, ybuf1, isem, rsem, *, tm, n_tiles):
    def body(slot):
        combined, prefetch, drain = _combined_tile(dest_hbm, rf_ref, h_ref, y_hbm, idx_smem, (ybuf0, ybuf1),
                                                   isem, rsem, tm=tm, n_tiles=n_tiles, slot=slot)
        n = v_ref.shape[3]
        n_chunks = tm // n
        hs = []
        for c in range(n_chunks):
            prefetch(c, n_chunks)
            hs.append(combined(c * n, n))
            h_out_ref[c * n:(c + 1) * n, :] = hs[c]
        _project(hs, cos_ref, sin_ref, kvn_ref, wdkv_ref, kvlat_ref, wuk_ref, wuv_ref,
                 bnorm_ref, wdq_ref, qnorm_ref, wuq_ref, q_ref, k_ref, v_ref)
        drain()

    _for_each_parity(body)


def _rope128(x, cos_t, sin_t):
    lane = lax.broadcasted_iota(jnp.int32, (1, 128), 1)
    first_half = (lane % QK_ROPE_DIM) < (QK_ROPE_DIM // 2)
    swapped = jnp.where(first_half, pltpu.roll(x, 128 - QK_ROPE_DIM // 2, axis=1),
                        pltpu.roll(x, QK_ROPE_DIM // 2, axis=1))
    return x * cos_t + swapped * sin_t


def _project(hs, cos_ref, sin_ref, kvn_ref, wdkv_ref, kvlat_ref, wuk_ref, wuv_ref,
             bnorm_ref, wdq_ref, qnorm_ref, wuq_ref, q_ref, k_ref, v_ref):
    n = hs[0].shape[0]
    chunks = range(len(hs))
    c_kv = [_dot(_rms(h, kvn_ref[...]).astype(_BF16), wdkv_ref[...]) for h in hs]
    c_q = [_dot(_rms(h, bnorm_ref[...]).astype(_BF16), wdq_ref[...]) for h in hs]
    ckv = [_rms(c[:, :KV_LORA_RANK], kvlat_ref[...]).astype(_BF16) for c in c_kv]
    cq = [(_rms(c, qnorm_ref[...]) * Q_SCALE).astype(_BF16) for c in c_q]
    kn = [_dot(ckv[c], wuk_ref[...]) for c in chunks]
    vt = [lax.dot_general(wuv_ref[...], ckv[c], _NT_DIMS, preferred_element_type=_F32) for c in chunks]
    q = [_dot(cq[c], wuq_ref[...]) for c in chunks]
    rope0 = N_HEADS * QK_NOPE_DIM
    for c in chunks:
        rows = slice(c * n, (c + 1) * n)
        cos_t = cos_ref[rows, :]
        sin_t = sin_ref[rows, :]
        kr = _rope128(c_kv[c][:, KV_LORA_RANK:KV_LORA_RANK + 128], cos_t, sin_t)[:, :QK_ROPE_DIM].astype(_BF16)
        for hd in range(N_HEADS):
            k_ref[hd, rows, 0:QK_NOPE_DIM] = kn[c][:, hd * QK_NOPE_DIM:(hd + 1) * QK_NOPE_DIM].astype(_BF16)
            k_ref[hd, rows, QK_NOPE_DIM:QK_DIM] = kr
            v_ref[hd, c] = vt[c][hd * V_HEAD_DIM:(hd + 1) * V_HEAD_DIM, :].astype(_BF16)
            q_ref[hd, rows, 0:QK_NOPE_DIM] = q[c][:, hd * QK_NOPE_DIM:(hd + 1) * QK_NOPE_DIM].astype(_BF16)
        for pair in range(N_HEADS // 2):
            qr = _rope128(q[c][:, rope0 + pair * LANES:rope0 + (pair + 1) * LANES], cos_t, sin_t)
            q_ref[2 * pair, rows, QK_NOPE_DIM:QK_DIM] = qr[:, :QK_ROPE_DIM].astype(_BF16)
            q_ref[2 * pair + 1, rows, QK_NOPE_DIM:QK_DIM] = qr[:, QK_ROPE_DIM:].astype(_BF16)


def _attn_kernel(q_ref, k_ref, vt_ref, km_ref, vmt_ref, o_ref, sa_ref, sb_ref, acc_ref, *, tq, tk):
    j = pl.program_id(2)
    heads = range(q_ref.shape[0])
    vd = acc_ref.shape[1]

    def scores(kb, s_ref):
        start = pl.multiple_of(kb * tk, tk)
        for hd in heads:
            s_ref[hd] = lax.dot_general(k_ref[hd, pl.ds(start, tk), :], q_ref[hd], _NT_DIMS,
                                        preferred_element_type=_F32)

    def update(kb, s_ref, carry, diag_block):
        out = []
        if diag_block is not None:
            visible = (diag_block * tk + lax.broadcasted_iota(jnp.int32, (tk, tq), 0)
                       <= lax.broadcasted_iota(jnp.int32, (tk, tq), 1))
        for hd in heads:
            m, l = carry[hd]
            s = s_ref[hd]
            if diag_block is not None:
                s = jnp.where(visible, s, NEG_INF)
            m_new = jnp.maximum(m, jnp.max(s, axis=0, keepdims=True))
            alpha = jnp.exp2(m - m_new)
            p = jnp.exp2(s - m_new)
            l = alpha * l + jnp.sum(p, axis=0, keepdims=True)
            acc_ref[hd] = acc_ref[hd] * alpha + _dot(vt_ref[hd, kb], p.astype(_BF16))
            out.append((m_new, l))
        return tuple(out)

    s0 = []
    for hd in heads:
        both = lax.dot_general(jnp.concatenate([k_ref[hd, 0:tk, :], km_ref[hd]], axis=0), q_ref[hd], _NT_DIMS,
                               preferred_element_type=_F32)
        sa_ref[hd] = both[0:tk]
        s0.append(both[tk:])
    carry, p0 = [], []
    for hd in heads:
        m = jnp.max(s0[hd], axis=0, keepdims=True)
        p = jnp.exp2(s0[hd] - m)
        carry.append((m, jnp.sum(p, axis=0, keepdims=True)))
        p0.append(p.astype(_BF16))
    for hd in heads:
        acc_ref[hd] = _dot(vmt_ref[hd, 0, :, 0:N_META], p0[hd])
    carry = tuple(carry)

    def pair(kp, carry):
        scores(2 * kp + 1, sb_ref)
        carry = update(2 * kp, sa_ref, carry, None)
        scores(2 * kp + 2, sa_ref)
        return update(2 * kp + 1, sb_ref, carry, None)

    carry = lax.fori_loop(0, j, pair, carry)
    last = 2 * j + 1
    start = pl.multiple_of(last * tk, tk)
    for hd in heads:
        sb_ref[hd, :, 0:tk] = lax.dot_general(k_ref[hd, pl.ds(start, tk), :], q_ref[hd, tk:tq, :], _NT_DIMS,
                                              preferred_element_type=_F32)
    carry = update(2 * j, sa_ref, carry, 0)
    visible = lax.broadcasted_iota(jnp.int32, (tk, tk), 0) <= lax.broadcasted_iota(jnp.int32, (tk, tk), 1)
    for hd in heads:
        m, l = carry[hd]
        s = jnp.where(visible, sb_ref[hd, :, 0:tk], NEG_INF)
        m_new = jnp.maximum(m[:, tk:], jnp.max(s, axis=0, keepdims=True))
        alpha = jnp.exp2(m[:, tk:] - m_new)
        p = jnp.exp2(s - m_new)
        l = jnp.concatenate([l[:, :tk], alpha * l[:, tk:] + jnp.sum(p, axis=0, keepdims=True)], axis=1)
        acc_ref[hd, :, tk:tq] = acc_ref[hd, :, tk:tq] * alpha + _dot(vt_ref[hd, last], p.astype(_BF16))
        o_ref[:, hd * vd:(hd + 1) * vd] = (acc_ref[hd] / l).T.astype(o_ref.dtype)


def _oproj_kernel(o_ref, h_ref, wo_ref, fnorm_ref, wrt_ref, br_ref, triu_ref,
                  h_out_ref, xn_ref, ri_ref, rf_ref, cnt_ref, base_ref, wsplit_ref):
    i = pl.program_id(0)

    @pl.when(i == 0)
    def _():
        _route_init(wrt_ref, wsplit_ref, base_ref)

    h = h_ref[...] + _dot(o_ref[...], wo_ref[...])
    h_out_ref[...] = h
    xn = _rms(h, fnorm_ref[...])
    _store_token_tiles(xn_ref, _pack_rows(xn))
    valid = lax.broadcasted_iota(jnp.int32, (1, h.shape[0]), 1) >= 0
    _route(xn, valid, wsplit_ref, br_ref, triu_ref, base_ref, ri_ref, rf_ref, cnt_ref)


def _full(shape):
    nd = len(shape)
    return pl.BlockSpec(shape, lambda *_: (0,) * nd)


def _router_operands(router_g, router_g_bias, router_e, router_e_bias):
    d = router_g.shape[0]
    wrt = jnp.concatenate([router_g.T, jnp.zeros((8 - N_EXPERT_GROUPS, d), _F32), router_e.T], axis=0)
    br = jnp.concatenate([router_g_bias, jnp.zeros((8 - N_EXPERT_GROUPS,), _F32), router_e_bias])[:, None]
    return wrt.astype(_F32), br.astype(_F32)


def _route_out(n_tiles, tm):
    shapes = [jax.ShapeDtypeStruct((n_tiles, 8, tm), jnp.int32),
              jax.ShapeDtypeStruct((n_tiles, 8, tm), _F32),
              jax.ShapeDtypeStruct((N_EXPERTS, 128), jnp.int32)]
    specs = [pl.BlockSpec((1, 8, tm), lambda i: (i, 0, 0)),
             pl.BlockSpec((1, 8, tm), lambda i: (i, 0, 0)),
             pl.BlockSpec((N_EXPERTS, 128), lambda i: (0, 0))]
    return shapes, specs


def _moe_experts(xn, ri, counts, w_gate, w_up, w_down, layer, n_tiles, n_real_tiles, d):
    tm, tr = TOKEN_TILE, EXPERT_ROWS
    parts = d // (2 * LANES)
    n_valid = n_real_tiles * tm + (n_tiles - n_real_tiles) * N_META
    seg = tr * EXPERT_BLOCKS_PER_STEP
    n_blocks = -(-(2 * n_valid + N_EXPERTS * (seg - 1)) // seg) * EXPERT_BLOCKS_PER_STEP
    n_rows = n_blocks * tr

    counts = counts[:, 0]
    padded = (counts + seg - 1) // seg * seg
    pends = jnp.cumsum(padded)
    pstarts = pends - padded
    n_used = (pends[-1] // tr).astype(jnp.int32).reshape(1)
    blk0 = jnp.arange(n_blocks, dtype=jnp.int32) * tr
    block_e = jnp.minimum(jnp.sum(blk0[:, None] >= pends[None, :], axis=1), N_EXPERTS - 1).astype(jnp.int32)
    experts = jnp.arange(N_EXPERTS, dtype=jnp.int32)
    block_oh = block_e[:, None] == experts[None, :]
    block_cnt = jnp.sum(jnp.where(block_oh, counts[None, :], 0), axis=1)
    block_start = jnp.sum(jnp.where(block_oh, pstarts[None, :], 0), axis=1)
    block_valid = jnp.clip(block_cnt - (blk0 - block_start), 0, tr).astype(jnp.int32)
    eid = ri[:, 0:2, :]
    slot0 = jnp.sum(jnp.where(eid[..., None] == experts, pstarts, 0), axis=-1)
    dest = (slot0 + ri[:, 2:4, :]).astype(jnp.int32).reshape(n_tiles * 2 * tm)

    pad_start = (pstarts + counts).astype(jnp.int32)
    pad_len = (padded - counts).astype(jnp.int32)
    xs = pl.pallas_call(
        functools.partial(_dispatch_kernel, tm=tm, tr=tr, pad_unit=seg, n_tiles=n_tiles, n_real_tiles=n_real_tiles,
                          n_blocks=n_blocks),
        grid_spec=pltpu.PrefetchScalarGridSpec(
            num_scalar_prefetch=3,
            grid=(n_tiles,),
            in_specs=[pl.BlockSpec(memory_space=pl.ANY), pl.BlockSpec(memory_space=pl.ANY)],
            out_specs=pl.BlockSpec(memory_space=pl.ANY),
            scratch_shapes=[pltpu.SMEM((4 * tm,), jnp.int32), pltpu.VMEM((3, tm * parts, LANES), jnp.uint32),
                            pltpu.VMEM((tr * parts, LANES), jnp.uint32),
                            pltpu.SemaphoreType.DMA((2,)), pltpu.SemaphoreType.DMA((3,)),
                            pltpu.SemaphoreType.DMA((3,)), pltpu.SemaphoreType.DMA]),
        out_shape=jax.ShapeDtypeStruct((n_rows * parts, LANES), jnp.uint32),
        compiler_params=_params(1, has_side_effects=True, disable_bounds_checks=True),
        name="moe_dispatch",
    )(pad_start, pad_len, pends[-1:].astype(jnp.int32), dest, xn)

    f = w_gate.shape[3]
    nb = EXPERT_BLOCKS_PER_STEP
    step_rows = nb * tr * parts

    expert_block = lambda i, be, bv, nu: (layer, be[jnp.minimum(i * nb, nu[0] - 1)], 0, 0)
    weight_specs = [pl.BlockSpec((1, 1, d, f), expert_block), pl.BlockSpec((1, 1, d, f), expert_block),
                    pl.BlockSpec((1, 1, f, d), expert_block)]
    weight_scratch = [pltpu.VMEM((d, f), _BF16), pltpu.VMEM((d, f), _BF16), pltpu.VMEM((f, d), _BF16)]
    y = pl.pallas_call(
        _expert_kernel,
        grid_spec=pltpu.PrefetchScalarGridSpec(
            num_scalar_prefetch=3,
            grid=(n_blocks // nb,),
            in_specs=[pl.BlockSpec((step_rows, LANES),
                                   lambda i, be, bv, nu: (jnp.minimum(i, (nu[0] - 1) // nb), 0))] + weight_specs,
            out_specs=pl.BlockSpec((step_rows, LANES), lambda i, be, bv, nu: (i, 0)),
            scratch_shapes=weight_scratch),
        out_shape=jax.ShapeDtypeStruct((n_rows * parts, LANES), jnp.uint32),
        compiler_params=_params(1),
        name="moe_experts",
    )(block_e, block_valid, n_used, xs, w_gate, w_up, w_down)

    return dest, y


def _combine_operands(dest, rf, h, y, d):
    tm = TOKEN_TILE
    parts = d // (2 * LANES)
    in_specs = [pl.BlockSpec(memory_space=pl.ANY),
                pl.BlockSpec((1, 8, tm), lambda i: (i, 0, 0)),
                pl.BlockSpec((tm, d), lambda i: (i, 0)),
                pl.BlockSpec(memory_space=pl.ANY)]
    scratch = [pltpu.SMEM((4 * tm,), jnp.int32),
               pltpu.VMEM((2 * tm * parts, LANES), jnp.uint32), pltpu.VMEM((2 * tm * parts, LANES), jnp.uint32),
               pltpu.SemaphoreType.DMA((2,)), pltpu.SemaphoreType.DMA((2,))]
    return (dest, rf, h, y), in_specs, scratch


def kernel(x, meta_tokens, a_norm, a_w, a_scale, b_norm, b_w_dq, b_q_norm, b_w_uq, b_w_o, kv_norm, w_dkv,
           kv_lat_norm, w_uk, w_uv, ffn_norm, router_g, router_g_bias, router_e, router_e_bias, w_gate, w_up,
           w_down, final_norm):
    bsz, seq, d = x.shape
    tm = TOKEN_TILE
    assert seq % tm == 0 and seq % ATTN_Q_TILE == 0 and ATTN_Q_TILE == 2 * ATTN_K_TILE and tm % ATTN_K_TILE == 0
    kt = tm // ATTN_K_TILE
    assert d % (LANES * len(POOL_WINDOWS)) == 0 and N_META == max(POOL_WINDOWS) and N_META <= tm
    parts = d // (2 * LANES)
    n_tok = bsz * seq
    n_real_tiles = n_tok // tm
    tiles_per_batch = seq // tm
    n_tiles = n_real_tiles + 1
    gd = d // len(POOL_WINDOWS)
    row = lambda v: v.reshape(1, -1).astype(_F32)

    x2 = x.reshape(n_tok, d)
    meta_pad = jnp.concatenate([meta_tokens, jnp.zeros((tm - N_META, d), x.dtype)], axis=0)

    r = np.arange(tm)[:, None]
    cidx = np.arange(tm)[None, :]
    pb = min(POOL_BLOCK, tm)
    pm = np.stack([((r[:pb] - cidx[:, :pb] >= 0) & (r[:pb] - cidx[:, :pb] < w)) for w in POOL_WINDOWS]).astype(np.float32)
    hc = np.arange(N_META)[None, :]
    ph = np.stack([(r[:N_META] + N_META - hc < w) for w in POOL_WINDOWS]).astype(np.float32)
    triu = (r < cidx).astype(np.float32)
    pm, ph, triu = (jnp.asarray(a, dtype=_BF16) for a in (pm, ph, triu))

    wrt0, br0 = _router_operands(router_g[0], router_g_bias[0], router_e[0], router_e_bias[0])
    route_shapes, route_specs = _route_out(n_tiles, tm)
    tile_or_last = lambda i: (jnp.minimum(i, n_real_tiles - 1), 0)
    halo_blocks = tm // N_META
    h1, xn1, ri1, rf1, cnt1 = pl.pallas_call(
        functools.partial(_mixer_kernel, n_real_tiles=n_real_tiles, tiles_per_batch=tiles_per_batch),
        grid=(n_tiles,),
        in_specs=[pl.BlockSpec((tm, d), tile_or_last),
                  pl.BlockSpec((N_META, d), lambda i: (jnp.clip(i * halo_blocks - 1, 0, n_tok // N_META - 1), 0)),
                  _full((tm, d)), _full((N_META, d)), _full((1, d)),
                  _full((len(POOL_WINDOWS), gd, gd)), _full((1, d)), _full((1, d)),
                  _full((ROUTER_ROWS, d)), _full((ROUTER_ROWS, 1)),
                  _full(pm.shape), _full(ph.shape), _full((tm, tm))],
        out_specs=[pl.BlockSpec((tm, d), lambda i: (i, 0)),
                   pl.BlockSpec((tm * parts, LANES), lambda i: (i, 0))] + route_specs,
        out_shape=[jax.ShapeDtypeStruct((n_tiles * tm, d), _F32),
                   jax.ShapeDtypeStruct((n_tiles * tm * parts, LANES), jnp.uint32)] + route_shapes,
        scratch_shapes=[pltpu.VMEM((N_EXPERTS, 1), _F32), pltpu.VMEM((ROUTER_SPLIT_ROWS, d), _BF16)],
        compiler_params=_params(1),
        name="pool_mixer_router",
    )(x2, x2, meta_pad, meta_tokens, row(a_norm[0]), a_w[0].astype(_BF16), row(a_scale[0]), row(ffn_norm[0]),
      wrt0, br0, pm, ph, triu)

    dest1, y1 = _moe_experts(xn1, ri1, cnt1, w_gate, w_up, w_down, 0, n_tiles, n_real_tiles, d)

    pos = jnp.concatenate([jnp.arange(seq, dtype=_F32) + N_META, jnp.arange(tm, dtype=_F32)])
    inv_freq = ROPE_THETA ** (-jnp.arange(0, QK_ROPE_DIM, 2, dtype=_F32) / QK_ROPE_DIM)
    ang = pos[:, None] * inv_freq[None, :]
    cos_t = jnp.tile(jnp.cos(ang), (1, 4))
    sin_t = jnp.tile(jnp.concatenate([-jnp.sin(ang), jnp.sin(ang)], axis=1), (1, 2))

    wdkv = jnp.concatenate([w_dkv, w_dkv[:, KV_LORA_RANK:]], axis=1).astype(_BF16)
    wuk = w_uk.reshape(KV_LORA_RANK, N_HEADS * QK_NOPE_DIM).astype(_BF16)
    wuv = w_uv.reshape(KV_LORA_RANK, N_HEADS * V_HEAD_DIM).T.astype(_BF16)
    wuq = b_w_uq[0]
    q_rank = wuq.shape[0]
    wuq = jnp.concatenate([wuq[:, :, :QK_NOPE_DIM].reshape(q_rank, -1),
                           wuq[:, :, QK_NOPE_DIM:].reshape(q_rank, -1)], axis=1).astype(_BF16)
    wdq = b_w_dq[0].astype(_BF16)
    pos_tile = lambda i: (jnp.where(i < n_real_tiles, i % tiles_per_batch, tiles_per_batch), 0)
    head_tile = lambda i: (0, i, 0)
    operands, in_specs, scratch = _combine_operands(dest1, rf1, h1, y1, d)
    h2, q, k, v = pl.pallas_call(
        functools.partial(_combine_proj_kernel, tm=tm, n_tiles=n_tiles),
        grid=(n_tiles,),
        in_specs=in_specs + [
            pl.BlockSpec((tm, 128), pos_tile), pl.BlockSpec((tm, 128), pos_tile),
            _full((1, d)), _full(wdkv.shape), _full((1, KV_LORA_RANK)), _full(wuk.shape), _full(wuv.shape),
            _full((1, d)), _full(wdq.shape), _full((1, q_rank)), _full(wuq.shape)],
        out_specs=[pl.BlockSpec((tm, d), lambda i: (i, 0)),
                   pl.BlockSpec((N_HEADS, tm, QK_DIM), head_tile),
                   pl.BlockSpec((N_HEADS, tm, QK_DIM), head_tile),
                   pl.BlockSpec((N_HEADS, kt, V_HEAD_DIM, ATTN_K_TILE), lambda i: (0, i, 0, 0))],
        out_shape=[jax.ShapeDtypeStruct((n_tiles * tm, d), _F32),
                   jax.ShapeDtypeStruct((N_HEADS, n_tiles * tm, QK_DIM), _BF16),
                   jax.ShapeDtypeStruct((N_HEADS, n_tiles * tm, QK_DIM), _BF16),
                   jax.ShapeDtypeStruct((N_HEADS, n_tiles * kt, V_HEAD_DIM, ATTN_K_TILE), _BF16)],
        scratch_shapes=scratch,
        compiler_params=_params(1, disable_bounds_checks=True),
        name="moe_combine_latent_qkv",
    )(*operands, cos_t, sin_t, row(kv_norm), wdkv, row(kv_lat_norm), wuk, wuv,
      row(b_norm[0]), wdq, row(b_q_norm[0]), wuq)

    tq, tk, hp = ATTN_Q_TILE, ATTN_K_TILE, ATTN_HEADS_PER_STEP
    n_q = seq // tq
    meta_block = n_tok // N_META
    o = pl.pallas_call(
        functools.partial(_attn_kernel, tq=tq, tk=tk),
        grid=(bsz, N_HEADS // hp, n_q),
        in_specs=[pl.BlockSpec((hp, tq, QK_DIM), lambda b, hg, j: (hg, b * n_q + j, 0)),
                  pl.BlockSpec((hp, seq, QK_DIM), lambda b, hg, j: (hg, b, 0)),
                  pl.BlockSpec((hp, seq // tk, V_HEAD_DIM, tk), lambda b, hg, j: (hg, b, 0, 0)),
                  pl.BlockSpec((hp, N_META, QK_DIM), lambda b, hg, j: (hg, meta_block, 0)),
                  pl.BlockSpec((hp, 1, V_HEAD_DIM, tk), lambda b, hg, j: (hg, n_real_tiles * kt, 0, 0))],
        out_specs=pl.BlockSpec((tq, hp * V_HEAD_DIM), lambda b, hg, j: (b * n_q + j, hg)),
        out_shape=jax.ShapeDtypeStruct((n_tok, N_HEADS * V_HEAD_DIM), _BF16),
        scratch_shapes=[pltpu.VMEM((hp, tk, tq), _F32), pltpu.VMEM((hp, tk, tq), _F32),
                        pltpu.VMEM((hp, V_HEAD_DIM, tq), _F32)],
        compiler_params=_params(3),
        name="causal_attention",
    )(q, k, v, k, v)

    wrt1, br1 = _router_operands(router_g[1], router_g_bias[1], router_e[1], router_e_bias[1])
    route_shapes, route_specs = _route_out(n_real_tiles, tm)
    h3, xn2, ri2, rf2, cnt2 = pl.pallas_call(
        _oproj_kernel,
        grid=(n_real_tiles,),
        in_specs=[pl.BlockSpec((tm, N_HEADS * V_HEAD_DIM), lambda i: (i, 0)),
                  pl.BlockSpec((tm, d), lambda i: (i, 0)),
                  _full((N_HEADS * V_HEAD_DIM, d)), _full((1, d)),
                  _full((ROUTER_ROWS, d)), _full((ROUTER_ROWS, 1)), _full((tm, tm))],
        out_specs=[pl.BlockSpec((tm, d), lambda i: (i, 0)),
                   pl.BlockSpec((tm * parts, LANES), lambda i: (i, 0))] + route_specs,
        out_shape=[jax.ShapeDtypeStruct((n_tok, d), _F32),
                   jax.ShapeDtypeStruct((n_tok * parts, LANES), jnp.uint32)] + route_shapes,
        scratch_shapes=[pltpu.VMEM((N_EXPERTS, 1), _F32), pltpu.VMEM((ROUTER_SPLIT_ROWS, d), _BF16)],
        compiler_params=_params(1),
        name="attn_out_router",
    )(o, h2, b_w_o[0].astype(_BF16), row(ffn_norm[1]), wrt1, br1, triu)

    dest2, y2 = _moe_experts(xn2, ri2, cnt2, w_gate, w_up, w_down, 1, n_real_tiles, n_real_tiles, d)
    operands, in_specs, scratch = _combine_operands(dest2, rf2, h3, y2, d)
    out = pl.pallas_call(
        functools.partial(_combine_norm_kernel, tm=tm, n_tiles=n_real_tiles),
        grid=(n_real_tiles,),
        in_specs=in_specs + [_full((1, d))],
        out_specs=pl.BlockSpec((tm, d), lambda i: (i, 0)),
        out_shape=jax.ShapeDtypeStruct((n_tok, d), _F32),
        scratch_shapes=scratch,
        compiler_params=_params(1, disable_bounds_checks=True),
        name="moe_combine_final_norm",
    )(*operands, row(final_norm))
    return out.reshape(bsz, seq, d)
```

```python
import functools

import numpy as np
import jax
import jax.numpy as jnp
from jax import lax
from jax.experimental import pallas as pl
from jax.experimental.pallas import tpu as pltpu

N_META = 16
POOL_WINDOWS = (2, 4, 8, 16)
N_HEADS = 8
QK_NOPE_DIM = 128
QK_ROPE_DIM = 64
QK_DIM = QK_NOPE_DIM + QK_ROPE_DIM
V_HEAD_DIM = 128
KV_LORA_RANK = 256
ROPE_THETA = 10000.0
ATTN_SCALE = QK_DIM ** -0.5
Q_SCALE = ATTN_SCALE * 1.4426950408889634
N_EXPERT_GROUPS = 4
EXPERTS_PER_GROUP = 8
N_EXPERTS = N_EXPERT_GROUPS * EXPERTS_PER_GROUP
RMS_EPS = 1e-6
NEG_INF = -1e30

TOKEN_TILE = 512
POOL_BLOCK = 256
EXPERT_ROWS = 256
EXPERT_BLOCKS_PER_STEP = 2
ATTN_Q_TILE = 512
ATTN_K_TILE = ATTN_Q_TILE // 2
ATTN_HEADS_PER_STEP = 4
DMA_UNROLL = 16
DMA_THREADS = 2
ROUTER_ROWS = 8 + N_EXPERTS
ROUTER_LO_ROW = 48
ROUTER_SPLIT_ROWS = 2 * ROUTER_LO_ROW
VMEM_LIMIT_BYTES = 48 * 1024 * 1024

LANES = 128
_F32 = jnp.float32
_BF16 = jnp.bfloat16
_NT_DIMS = (((1,), (1,)), ((), ()))


def _params(n_grid_dims=1, **kw):
    return pltpu.CompilerParams(dimension_semantics=("arbitrary",) * n_grid_dims,
                                vmem_limit_bytes=VMEM_LIMIT_BYTES, **kw)


def _rms(x, g):
    ms = jnp.mean(x * x, axis=-1, keepdims=True)
    return x * lax.rsqrt(ms + RMS_EPS) * g


def _split_bf16(x):
    hi = x.astype(_BF16)
    lo = (x - hi.astype(_F32)).astype(_BF16)
    return hi, lo


def _dot(a, b):
    return jnp.dot(a, b, preferred_element_type=_F32)


def _store_token_tiles(ref, x):
    n, d = x.shape
    parts = d // LANES
    for s in range(parts):
        ref[pl.ds(s, n, stride=parts), :] = x[:, s * LANES:(s + 1) * LANES]


def _load_token_tiles(ref, first_token, n, parts):
    return jnp.concatenate([ref[pl.ds(first_token * parts + s, n, stride=parts), :] for s in range(parts)], axis=1)


_HIGH_HALF = 0xFFFF0000


def _pack_rows(x):
    n, d = x.shape
    words = []
    for s in range(d // (2 * LANES)):
        lo = x[:, 2 * s * LANES:(2 * s + 1) * LANES].astype(_BF16).astype(_F32)
        hi = x[:, (2 * s + 1) * LANES:(2 * s + 2) * LANES].astype(_BF16).astype(_F32)
        words.append(lax.shift_right_logical(lax.bitcast_convert_type(lo, jnp.uint32), jnp.uint32(16))
                     | (lax.bitcast_convert_type(hi, jnp.uint32) & jnp.uint32(_HIGH_HALF)))
    return jnp.concatenate(words, axis=1)


def _unpack_rows(w):
    cols = []
    for s in range(w.shape[1] // LANES):
        ws = w[:, s * LANES:(s + 1) * LANES]
        cols.append(lax.bitcast_convert_type(lax.shift_left(ws, jnp.uint32(16)), _F32))
        cols.append(lax.bitcast_convert_type(ws & jnp.uint32(_HIGH_HALF), _F32))
    return jnp.concatenate(cols, axis=1)


def _route_init(wrt_ref, wsplit_ref, base_ref):
    base_ref[...] = jnp.zeros(base_ref.shape, base_ref.dtype)
    w_hi, w_lo = _split_bf16(wrt_ref[...])
    wsplit_ref[...] = jnp.zeros(wsplit_ref.shape, wsplit_ref.dtype)
    wsplit_ref[0:ROUTER_ROWS, :] = w_hi
    wsplit_ref[ROUTER_LO_ROW:ROUTER_LO_ROW + ROUTER_ROWS, :] = w_lo


def _route(xn, valid, wsplit_ref, br_ref, triu_ref, base_ref, ri_ref, rf_ref, cnt_ref):
    tm = xn.shape[0]
    x_hi, x_lo = _split_bf16(xn)
    by_hi = lax.dot_general(wsplit_ref[...], x_hi, _NT_DIMS, preferred_element_type=_F32)
    by_lo = lax.dot_general(wsplit_ref[0:ROUTER_LO_ROW, :], x_lo, _NT_DIMS, preferred_element_type=_F32)
    logits = (by_hi[0:ROUTER_ROWS] + by_hi[ROUTER_LO_ROW:ROUTER_LO_ROW + ROUTER_ROWS]
              + by_lo[0:ROUTER_ROWS] + br_ref[...])
    lg = logits[0:N_EXPERT_GROUPS]
    eg = jnp.exp(lg - jnp.max(lg, axis=0, keepdims=True))
    pg = eg / jnp.sum(eg, axis=0, keepdims=True)
    w_g = jnp.max(pg, axis=0, keepdims=True)
    ig = lax.broadcasted_iota(jnp.int32, pg.shape, 0).astype(_F32)
    g_sel = jnp.min(jnp.where(pg == w_g, ig, float(N_EXPERT_GROUPS)), axis=0, keepdims=True)

    sel = logits[8:8 + EXPERTS_PER_GROUP]
    for g in range(1, N_EXPERT_GROUPS):
        sel = jnp.where(g_sel == float(g), logits[8 + g * EXPERTS_PER_GROUP:8 + (g + 1) * EXPERTS_PER_GROUP], sel)
    ie = lax.broadcasted_iota(jnp.int32, sel.shape, 0).astype(_F32)
    v1 = jnp.max(sel, axis=0, keepdims=True)
    i1 = jnp.min(jnp.where(sel == v1, ie, float(EXPERTS_PER_GROUP)), axis=0, keepdims=True)
    rest = jnp.where(ie == i1, -jnp.inf, sel)
    v2 = jnp.max(rest, axis=0, keepdims=True)
    i2 = jnp.min(jnp.where(rest == v2, ie, float(EXPERTS_PER_GROUP)), axis=0, keepdims=True)
    e2 = jnp.exp(v2 - v1)
    den = 1.0 + e2
    validf = valid.astype(_F32)
    gate0 = w_g * (1.0 / den) * validf
    gate1 = w_g * (e2 / den) * validf
    eid0 = g_sel * float(EXPERTS_PER_GROUP) + i1
    eid1 = g_sel * float(EXPERTS_PER_GROUP) + i2

    iall = lax.broadcasted_iota(jnp.int32, (N_EXPERTS, tm), 0).astype(_F32)
    oh0 = jnp.where(iall == eid0, validf, 0.0)
    oh1 = jnp.where(iall == eid1, validf, 0.0)
    both = oh0 + oh1
    before = _dot(both.astype(_BF16), triu_ref[...]) + base_ref[...]
    rank0 = jnp.sum(oh0 * before, axis=0, keepdims=True)
    rank1 = jnp.sum(oh1 * before, axis=0, keepdims=True)
    base_ref[...] = base_ref[...] + jnp.sum(both, axis=1, keepdims=True)

    ri_ref[...] = jnp.zeros(ri_ref.shape, ri_ref.dtype)
    rf_ref[...] = jnp.zeros(rf_ref.shape, rf_ref.dtype)
    ri_ref[0, 0:1, :] = eid0.astype(jnp.int32)
    ri_ref[0, 1:2, :] = eid1.astype(jnp.int32)
    ri_ref[0, 2:3, :] = rank0.astype(jnp.int32)
    ri_ref[0, 3:4, :] = rank1.astype(jnp.int32)
    rf_ref[0, 0:1, :] = gate0
    rf_ref[0, 1:2, :] = gate1
    cnt_ref[...] = jnp.broadcast_to(base_ref[...], cnt_ref.shape).astype(jnp.int32)


def _mixer_kernel(x_ref, xh_ref, mp_ref, meta_ref, anorm_ref, aw_ref, ascale_ref, fnorm_ref,
                  wrt_ref, br_ref, pm_ref, ph_ref, triu_ref,
                  h_ref, xn_ref, ri_ref, rf_ref, cnt_ref, base_ref, wsplit_ref, *, n_real_tiles, tiles_per_batch):
    i = pl.program_id(0)
    tm = x_ref.shape[0]
    gd = x_ref.shape[1] // len(POOL_WINDOWS)
    is_meta = i == n_real_tiles
    first = (i % tiles_per_batch) == 0

    @pl.when(i == 0)
    def _():
        _route_init(wrt_ref, wsplit_ref, base_ref)

    h = jnp.where(is_meta, mp_ref[...], x_ref[...])
    halo = jnp.where(is_meta, 0.0, jnp.where(first, meta_ref[...], xh_ref[...]))
    hn = _rms(h, anorm_ref[...])
    hh = _rms(halo, anorm_ref[...])
    hn_hi, hn_lo = _split_bf16(hn)
    hh_hi, hh_lo = _split_bf16(hh)
    pb = pm_ref.shape[1]
    wins = {}
    for c in range(tm // pb):
        rows = slice(c * pb, (c + 1) * pb)
        for g in range(len(POOL_WINDOWS)):
            sl = slice(g * gd, (g + 1) * gd)
            win = _dot(pm_ref[g], hn_hi[rows, sl]) + _dot(pm_ref[g], hn_lo[rows, sl])
            if c == 0:
                before_hi, before_lo = hh_hi[:, sl], hh_lo[:, sl]
            else:
                before_hi, before_lo = hn_hi[c * pb - N_META:c * pb, sl], hn_lo[c * pb - N_META:c * pb, sl]
            top = win[:N_META] + _dot(ph_ref[g], before_hi) + _dot(ph_ref[g], before_lo)
            wins[c, g] = jnp.concatenate([top, win[N_META:]], axis=0)
    for c in range(tm // pb):
        rows = slice(c * pb, (c + 1) * pb)
        row = c * pb + lax.broadcasted_iota(jnp.int32, (pb, 1), 0)
        for g, w in enumerate(POOL_WINDOWS):
            sl = slice(g * gd, (g + 1) * gd)
            cnt = jnp.where(is_meta, jnp.minimum(row + 1, w), w).astype(_F32)
            pooled = wins[c, g] * (1.0 / cnt) - hn[rows, sl]
            mix = _dot(pooled.astype(_BF16), aw_ref[g])
            h_ref[rows, sl] = h[rows, sl] + mix * ascale_ref[:, sl]

    xn = _rms(h_ref[...], fnorm_ref[...])
    _store_token_tiles(xn_ref, _pack_rows(xn))
    lane = lax.broadcasted_iota(jnp.int32, (1, tm), 1)
    valid = jnp.logical_or(jnp.logical_not(is_meta), lane < N_META)
    _route(xn, valid, wsplit_ref, br_ref, triu_ref, base_ref, ri_ref, rf_ref, cnt_ref)


def _index_copy(dest_hbm, idx_smem, isem, tile, n):
    return pltpu.make_async_copy(dest_hbm.at[pl.ds(pl.multiple_of(tile * n, n), n)],
                                 idx_smem.at[pl.ds(pl.multiple_of((tile % 2) * n, n), n)], isem.at[tile % 2])


def _dispatch_kernel(pad_start_ref, pad_len_ref, tail_ref, dest_hbm, xn_hbm, xs_hbm,
                     idx_smem, xbuf, zbuf, isem, lsem, ssem, zsem, *, tm, tr, n_tiles, n_real_tiles, n_blocks):
    i = pl.program_id(0)
    parts = xbuf.shape[1] // tm
    buf = i % 3

    def zero_fill(wait):
        def zero_rows(first_row, n_rows):
            cp = pltpu.make_async_copy(
                zbuf.at[pl.ds(0, n_rows * parts), :],
                xs_hbm.at[pl.ds(pl.multiple_of(first_row * parts, parts), n_rows * parts), :], zsem)
            if wait:
                cp.wait()
            else:
                cp.start()

        def per_expert(e, c):
            first, length = pad_start_ref[e], pad_len_ref[e]
            for bit in range((tr - 1).bit_length()):
                @pl.when((length >> bit) % 2 == 1)
                def _(bit=bit):
                    zero_rows(first + ((length >> (bit + 1)) << (bit + 1)), 1 << bit)
            return c

        def per_block(blk, c):
            zero_rows(blk * tr, tr)
            return c

        lax.fori_loop(0, N_EXPERTS, per_expert, 0)
        lax.fori_loop(tail_ref[0] // tr, n_blocks, per_block, 0)

    def idx_copy(tile):
        return _index_copy(dest_hbm, idx_smem, isem, tile, 2 * tm)

    def load(tile):
        rows = tm * parts
        return pltpu.make_async_copy(xn_hbm.at[pl.ds(pl.multiple_of(tile * rows, rows), rows), :],
                                     xbuf.at[tile % 3], lsem.at[tile % 3])

    def wait_scatters(tile):
        def wait_rows(n_tok):
            for k in range(2):
                pltpu.make_async_copy(xbuf.at[tile % 3, pl.ds(0, n_tok * parts), :],
                                      xs_hbm.at[pl.ds(0, n_tok * parts), :], ssem.at[tile % 3]).wait()

        @pl.when(tile < n_real_tiles)
        def _():
            wait_rows(tm)

        @pl.when(tile >= n_real_tiles)
        def _():
            wait_rows(N_META)

    def scatter(n_tok):
        def issue(g, c):
            for u in range(min(DMA_UNROLL, n_tok)):
                t = g * min(DMA_UNROLL, n_tok) + u
                src = xbuf.at[buf, pl.ds(pl.multiple_of(t * parts, parts), parts), :]
                for k in range(2):
                    d = idx_smem[(i % 2) * (2 * tm) + k * tm + t]
                    pltpu.make_async_copy(src, xs_hbm.at[pl.ds(pl.multiple_of(d * parts, parts), parts), :],
                                          ssem.at[buf]).start(priority=(2 * u + k) % DMA_THREADS)
            return c

        lax.fori_loop(0, n_tok // min(DMA_UNROLL, n_tok), issue, 0)

    @pl.when(i == 0)
    def _():
        idx_copy(0).start()
        load(0).start()
        zbuf[...] = jnp.zeros(zbuf.shape, zbuf.dtype)
        zero_fill(wait=False)

    @pl.when(i >= 2)
    def _():
        wait_scatters(i - 2)

    @pl.when(i + 1 < n_tiles)
    def _():
        idx_copy(i + 1).start()
        load(i + 1).start()

    idx_copy(i).wait()
    load(i).wait()

    @pl.when(i < n_real_tiles)
    def _():
        scatter(tm)

    @pl.when(i >= n_real_tiles)
    def _():
        scatter(N_META)

    @pl.when(i == n_tiles - 1)
    def _():
        if n_tiles >= 2:
            wait_scatters(i - 1)
        wait_scatters(i)
        zero_fill(wait=True)


def _expert_kernel(be_ref, bv_ref, nu_ref, xs_ref, *refs):
    nb = EXPERT_BLOCKS_PER_STEP
    w_refs, y_ref, wb_refs = refs[:3 * nb], refs[3 * nb], refs[3 * nb + 1:]
    i = pl.program_id(0)
    n_used = nu_ref[0]
    parts = wb_refs[0].shape[0] // (2 * LANES)
    tr = xs_ref.shape[0] // (nb * parts)

    for c in range(nb):
        blk = i * nb + c
        changed = be_ref[blk] != be_ref[jnp.maximum(blk - nb, 0)]

        @pl.when(jnp.logical_or(i == 0, jnp.logical_and(blk < n_used, changed)))
        def _(c=c):
            for w_ref, wb_ref in zip(w_refs[3 * c:3 * c + 3], wb_refs[3 * c:3 * c + 3]):
                wb_ref[...] = w_ref[0, 0].astype(_BF16)

    @pl.when(i * nb < n_used)
    def _():
        rows = lax.broadcasted_iota(jnp.int32, (tr, 1), 0)
        xs, gs, us, ys = [], [], [], []
        for c in range(nb):
            x = jnp.where(rows < bv_ref[i * nb + c], _unpack_rows(_load_token_tiles(xs_ref, c * tr, tr, parts)), 0.0)
            xs.append(x.astype(_BF16))
        for c in range(nb):
            gs.append(_dot(xs[c], wb_refs[3 * c][...]))
            us.append(_dot(xs[c], wb_refs[3 * c + 1][...]))
        for c in range(nb):
            a = gs[c] * (1.0 / (1.0 + jnp.exp(-gs[c]))) * us[c]
            ys.append(_dot(a.astype(_BF16), wb_refs[3 * c + 2][...]))
        for c in range(nb):
            y = jnp.where(i * nb + c < n_used, ys[c], 0.0)
            _store_token_tiles(y_ref.at[pl.ds(c * tr * parts, tr * parts), :], _pack_rows(y))

    @pl.when(i * nb >= n_used)
    def _():
        y_ref[...] = jnp.zeros(y_ref.shape, y_ref.dtype)


def _combined_tile(dest_hbm, rf_ref, h_ref, y_hbm, idx_smem, ybufs, isem, rsem, *, tm, n_tiles, slot):
    i = pl.program_id(0)
    parts = ybufs[0].shape[0] // (2 * tm)

    def idx_copy(tile):
        return _index_copy(dest_hbm, idx_smem, isem, tile, 2 * tm)

    def row_copy(idx_slot, buf_slot, t, k, u):
        d = idx_smem[idx_slot * (2 * tm) + k * tm + t]
        row = (k * tm + t) * parts
        if not isinstance(t, int):
            row = pl.multiple_of(row, parts)
        return pltpu.make_async_copy(
            y_hbm.at[pl.ds(pl.multiple_of(d * parts, parts), parts), :],
            ybufs[buf_slot].at[pl.ds(row, parts), :],
            rsem.at[buf_slot]).start(priority=(2 * u + k) % DMA_THREADS)

    def wait_rows(buf_slot):
        pltpu.make_async_copy(y_hbm.at[pl.ds(0, 2 * tm * parts), :], ybufs[buf_slot], rsem.at[buf_slot]).wait()

    if slot == 0:
        @pl.when(i == 0)
        def _():
            idx_copy(0).start()
            idx_copy(0).wait()

            def issue(g, c):
                for u in range(DMA_UNROLL):
                    for k in range(2):
                        row_copy(0, 0, g * DMA_UNROLL + u, k, u)
                return c

            lax.fori_loop(0, tm // DMA_UNROLL, issue, 0)
            if n_tiles >= 2:
                idx_copy(1).start()

    @pl.when(i + 1 < n_tiles)
    def _():
        idx_copy(i + 1).wait()

    @pl.when(i + 2 < n_tiles)
    def _():
        idx_copy(i + 2).start()

    nxt_idx_slot = jnp.minimum(i + 1, n_tiles - 1) % 2

    def prefetch(part, n_parts):
        for t in range(part * tm // n_parts, (part + 1) * tm // n_parts):
            for k in range(2):
                row_copy(nxt_idx_slot, 1 - slot, t, k, t)

    gates = rf_ref[0]
    gt = jnp.concatenate([gates, jnp.zeros((LANES - gates.shape[0], tm), _F32)], axis=0).T
    wait_rows(slot)
    yb = ybufs[slot]

    def combined(r0, n):
        return h_ref[r0:r0 + n, :] + (_unpack_rows(_load_token_tiles(yb, r0, n, parts)) * gt[r0:r0 + n, 0:1]
                                      + _unpack_rows(_load_token_tiles(yb, tm + r0, n, parts)) * gt[r0:r0 + n, 1:2])

    def drain():
        @pl.when(i == n_tiles - 1)
        def _():
            wait_rows(1 - slot)

    return combined, prefetch, drain


def _for_each_parity(body):
    for slot in range(2):
        @pl.when(pl.program_id(0) % 2 == slot)
        def _(slot=slot):
            body(slot)


def _combine_norm_kernel(dest_hbm, rf_ref, h_ref, y_hbm, fnorm_ref, out_ref,
                         idx_smem, ybuf0, ybuf1, isem, rsem, *, tm, n_tiles):
    def body(slot):
        combined, prefetch, drain = _combined_tile(dest_hbm, rf_ref, h_ref, y_hbm, idx_smem, (ybuf0, ybuf1),
                                                   isem, rsem, tm=tm, n_tiles=n_tiles, slot=slot)
        n_chunks = 4
        n = tm // n_chunks
        for c in range(n_chunks):
            prefetch(c, n_chunks)
            out_ref[c * n:(c + 1) * n, :] = _rms(combined(c * n, n), fnorm_ref[...])
        drain()

    _for_each_parity(body)


def _combine_proj_kernel(dest_hbm, rf_ref, h_ref, y_hbm, cos_ref, sin_ref, kvn_ref, wdkv_ref, kvlat_ref,
                         wuk_ref, wuv_ref, bnorm_ref, wdq_ref, qnorm_ref, wuq_ref,
                         h_out_ref, q_ref, k_ref, v_ref, idx_smem, ybuf0, ybuf1, isem, rsem, *, tm, n_tiles):
    def body(slot):
        combined, prefetch, drain = _combined_tile(dest_hbm, rf_ref, h_ref, y_hbm, idx_smem, (ybuf0, ybuf1),
                                                   isem, rsem, tm=tm, n_tiles=n_tiles, slot=slot)
        n = v_ref.shape[3]
        n_chunks = tm // n
        hs = []
        for c in range(n_chunks):
            prefetch(c, n_chunks)
            hs.append(combined(c * n, n))
            h_out_ref[c * n:(c + 1) * n, :] = hs[c]
        _project(hs, cos_ref, sin_ref, kvn_ref, wdkv_ref, kvlat_ref, wuk_ref, wuv_ref,
                 bnorm_ref, wdq_ref, qnorm_ref, wuq_ref, q_ref, k_ref, v_ref)
        drain()

    _for_each_parity(body)


def _rope128(x, cos_t, sin_t):
    lane = lax.broadcasted_iota(jnp.int32, (1, 128), 1)
    first_half = (lane % QK_ROPE_DIM) < (QK_ROPE_DIM // 2)
    swapped = jnp.where(first_half, pltpu.roll(x, 128 - QK_ROPE_DIM // 2, axis=1),
                        pltpu.roll(x, QK_ROPE_DIM // 2, axis=1))
    return x * cos_t + swapped * sin_t


def _project(hs, cos_ref, sin_ref, kvn_ref, wdkv_ref, kvlat_ref, wuk_ref, wuv_ref,
             bnorm_ref, wdq_ref, qnorm_ref, wuq_ref, q_ref, k_ref, v_ref):
    n = hs[0].shape[0]
    chunks = range(len(hs))
    c_kv = [_dot(_rms(h, kvn_ref[...]).astype(_BF16), wdkv_ref[...]) for h in hs]
    c_q = [_dot(_rms(h, bnorm_ref[...]).astype(_BF16), wdq_ref[...]) for h in hs]
    ckv = [_rms(c[:, :KV_LORA_RANK], kvlat_ref[...]).astype(_BF16) for c in c_kv]
    cq = [(_rms(c, qnorm_ref[...]) * Q_SCALE).astype(_BF16) for c in c_q]
    kn = [_dot(ckv[c], wuk_ref[...]) for c in chunks]
    vt = [lax.dot_general(wuv_ref[...], ckv[c], _NT_DIMS, preferred_element_type=_F32) for c in chunks]
    q = [_dot(cq[c], wuq_ref[...]) for c in chunks]
    rope0 = N_HEADS * QK_NOPE_DIM
    for c in chunks:
        rows = slice(c * n, (c + 1) * n)
        cos_t = cos_ref[rows, :]
        sin_t = sin_ref[rows, :]
        kr = _rope128(c_kv[c][:, KV_LORA_RANK:KV_LORA_RANK + 128], cos_t, sin_t)[:, :QK_ROPE_DIM].astype(_BF16)
        for hd in range(N_HEADS):
            k_ref[hd, rows, 0:QK_NOPE_DIM] = kn[c][:, hd * QK_NOPE_DIM:(hd + 1) * QK_NOPE_DIM].astype(_BF16)
            k_ref[hd, rows, QK_NOPE_DIM:QK_DIM] = kr
            v_ref[hd, c] = vt[c][hd * V_HEAD_DIM:(hd + 1) * V_HEAD_DIM, :].astype(_BF16)
            q_ref[hd, rows, 0:QK_NOPE_DIM] = q[c][:, hd * QK_NOPE_DIM:(hd + 1) * QK_NOPE_DIM].astype(_BF16)
        for pair in range(N_HEADS // 2):
            qr = _rope128(q[c][:, rope0 + pair * LANES:rope0 + (pair + 1) * LANES], cos_t, sin_t)
            q_ref[2 * pair, rows, QK_NOPE_DIM:QK_DIM] = qr[:, :QK_ROPE_DIM].astype(_BF16)
            q_ref[2 * pair + 1, rows, QK_NOPE_DIM:QK_DIM] = qr[:, QK_ROPE_DIM:].astype(_BF16)


def _attn_kernel(q_ref, k_ref, vt_ref, km_ref, vmt_ref, o_ref, sa_ref, sb_ref, acc_ref, *, tq, tk):
    j = pl.program_id(2)
    heads = range(q_ref.shape[0])
    vd = acc_ref.shape[1]

    def scores(kb, s_ref):
        start = pl.multiple_of(kb * tk, tk)
        for hd in heads:
            s_ref[hd] = lax.dot_general(k_ref[hd, pl.ds(start, tk), :], q_ref[hd], _NT_DIMS,
                                        preferred_element_type=_F32)

    def update(kb, s_ref, carry, diag_block):
        out = []
        if diag_block is not None:
            visible = (diag_block * tk + lax.broadcasted_iota(jnp.int32, (tk, tq), 0)
                       <= lax.broadcasted_iota(jnp.int32, (tk, tq), 1))
        for hd in heads:
            m, l = carry[hd]
            s = s_ref[hd]
            if diag_block is not None:
                s = jnp.where(visible, s, NEG_INF)
            m_new = jnp.maximum(m, jnp.max(s, axis=0, keepdims=True))
            alpha = jnp.exp2(m - m_new)
            p = jnp.exp2(s - m_new)
            l = alpha * l + jnp.sum(p, axis=0, keepdims=True)
            acc_ref[hd] = acc_ref[hd] * alpha + _dot(vt_ref[hd, kb], p.astype(_BF16))
            out.append((m_new, l))
        return tuple(out)

    s0 = []
    for hd in heads:
        both = lax.dot_general(jnp.concatenate([k_ref[hd, 0:tk, :], km_ref[hd]], axis=0), q_ref[hd], _NT_DIMS,
                               preferred_element_type=_F32)
        sa_ref[hd] = both[0:tk]
        s0.append(both[tk:])
    carry, p0 = [], []
    for hd in heads:
        m = jnp.max(s0[hd], axis=0, keepdims=True)
        p = jnp.exp2(s0[hd] - m)
        carry.append((m, jnp.sum(p, axis=0, keepdims=True)))
        p0.append(p.astype(_BF16))
    for hd in heads:
        acc_ref[hd] = _dot(vmt_ref[hd, 0, :, 0:N_META], p0[hd])
    carry = tuple(carry)

    def pair(kp, carry):
        scores(2 * kp + 1, sb_ref)
        carry = update(2 * kp, sa_ref, carry, None)
        scores(2 * kp + 2, sa_ref)
        return update(2 * kp + 1, sb_ref, carry, None)

    carry = lax.fori_loop(0, j, pair, carry)
    last = 2 * j + 1
    start = pl.multiple_of(last * tk, tk)
    for hd in heads:
        sb_ref[hd, :, 0:tk] = lax.dot_general(k_ref[hd, pl.ds(start, tk), :], q_ref[hd, tk:tq, :], _NT_DIMS,
                                              preferred_element_type=_F32)
    carry = update(2 * j, sa_ref, carry, 0)
    visible = lax.broadcasted_iota(jnp.int32, (tk, tk), 0) <= lax.broadcasted_iota(jnp.int32, (tk, tk), 1)
    for hd in heads:
        m, l = carry[hd]
        s = jnp.where(visible, sb_ref[hd, :, 0:tk], NEG_INF)
        m_new = jnp.maximum(m[:, tk:], jnp.max(s, axis=0, keepdims=True))
        alpha = jnp.exp2(m[:, tk:] - m_new)
        p = jnp.exp2(s - m_new)
        l = jnp.concatenate([l[:, :tk], alpha * l[:, tk:] + jnp.sum(p, axis=0, keepdims=True)], axis=1)
        acc_ref[hd, :, tk:tq] = acc_ref[hd, :, tk:tq] * alpha + _dot(vt_ref[hd, last], p.astype(_BF16))
        o_ref[:, hd * vd:(hd + 1) * vd] = (acc_ref[hd] / l).T.astype(o_ref.dtype)


def _oproj_kernel(o_ref, h_ref, wo_ref, fnorm_ref, wrt_ref, br_ref, triu_ref,
                  h_out_ref, xn_ref, ri_ref, rf_ref, cnt_ref, base_ref, wsplit_ref):
    i = pl.program_id(0)

    @pl.when(i == 0)
    def _():
        _route_init(wrt_ref, wsplit_ref, base_ref)

    h = h_ref[...] + _dot(o_ref[...], wo_ref[...])
    h_out_ref[...] = h
    xn = _rms(h, fnorm_ref[...])
    _store_token_tiles(xn_ref, _pack_rows(xn))
    valid = lax.broadcasted_iota(jnp.int32, (1, h.shape[0]), 1) >= 0
    _route(xn, valid, wsplit_ref, br_ref, triu_ref, base_ref, ri_ref, rf_ref, cnt_ref)


def _full(shape):
    nd = len(shape)
    return pl.BlockSpec(shape, lambda *_: (0,) * nd)


def _router_operands(router_g, router_g_bias, router_e, router_e_bias):
    d = router_g.shape[0]
    wrt = jnp.concatenate([router_g.T, jnp.zeros((8 - N_EXPERT_GROUPS, d), _F32), router_e.T], axis=0)
    br = jnp.concatenate([router_g_bias, jnp.zeros((8 - N_EXPERT_GROUPS,), _F32), router_e_bias])[:, None]
    return wrt.astype(_F32), br.astype(_F32)


def _route_out(n_tiles, tm):
    shapes = [jax.ShapeDtypeStruct((n_tiles, 8, tm), jnp.int32),
              jax.ShapeDtypeStruct((n_tiles, 8, tm), _F32),
              jax.ShapeDtypeStruct((N_EXPERTS, 128), jnp.int32)]
    specs = [pl.BlockSpec((1, 8, tm), lambda i: (i, 0, 0)),
             pl.BlockSpec((1, 8, tm), lambda i: (i, 0, 0)),
             pl.BlockSpec((N_EXPERTS, 128), lambda i: (0, 0))]
    return shapes, specs


def _moe_experts(xn, ri, counts, w_gate, w_up, w_down, layer, n_tiles, n_real_tiles, d):
    tm, tr = TOKEN_TILE, EXPERT_ROWS
    parts = d // (2 * LANES)
    n_valid = n_real_tiles * tm + (n_tiles - n_real_tiles) * N_META
    n_blocks = -(-(2 * n_valid + N_EXPERTS * (tr - 1)) // tr)
    n_blocks = -(-n_blocks // EXPERT_BLOCKS_PER_STEP) * EXPERT_BLOCKS_PER_STEP
    n_rows = n_blocks * tr

    counts = counts[:, 0]
    padded = (counts + tr - 1) // tr * tr
    pends = jnp.cumsum(padded)
    pstarts = pends - padded
    n_used = (pends[-1] // tr).astype(jnp.int32).reshape(1)
    blk0 = jnp.arange(n_blocks, dtype=jnp.int32) * tr
    block_e = jnp.minimum(jnp.sum(blk0[:, None] >= pends[None, :], axis=1), N_EXPERTS - 1).astype(jnp.int32)
    experts = jnp.arange(N_EXPERTS, dtype=jnp.int32)
    block_oh = block_e[:, None] == experts[None, :]
    block_cnt = jnp.sum(jnp.where(block_oh, counts[None, :], 0), axis=1)
    block_start = jnp.sum(jnp.where(block_oh, pstarts[None, :], 0), axis=1)
    block_valid = jnp.clip(block_cnt - (blk0 - block_start), 0, tr).astype(jnp.int32)
    eid = ri[:, 0:2, :]
    slot0 = jnp.sum(jnp.where(eid[..., None] == experts, pstarts, 0), axis=-1)
    dest = (slot0 + ri[:, 2:4, :]).astype(jnp.int32).reshape(n_tiles * 2 * tm)

    pad_start = (pstarts + counts).astype(jnp.int32)
    pad_len = (padded - counts).astype(jnp.int32)
    xs = pl.pallas_call(
        functools.partial(_dispatch_kernel, tm=tm, tr=tr, n_tiles=n_tiles, n_real_tiles=n_real_tiles,
                          n_blocks=n_blocks),
        grid_spec=pltpu.PrefetchScalarGridSpec(
            num_scalar_prefetch=3,
            grid=(n_tiles,),
            in_specs=[pl.BlockSpec(memory_space=pl.ANY), pl.BlockSpec(memory_space=pl.ANY)],
            out_specs=pl.BlockSpec(memory_space=pl.ANY),
            scratch_shapes=[pltpu.SMEM((4 * tm,), jnp.int32), pltpu.VMEM((3, tm * parts, LANES), jnp.uint32),
                            pltpu.VMEM((tr * parts, LANES), jnp.uint32),
                            pltpu.SemaphoreType.DMA((2,)), pltpu.SemaphoreType.DMA((3,)),
                            pltpu.SemaphoreType.DMA((3,)), pltpu.SemaphoreType.DMA]),
        out_shape=jax.ShapeDtypeStruct((n_rows * parts, LANES), jnp.uint32),
        compiler_params=_params(1, has_side_effects=True, disable_bounds_checks=True),
        name="moe_dispatch",
    )(pad_start, pad_len, pends[-1:].astype(jnp.int32), dest, xn)

    f = w_gate.shape[3]
    nb = EXPERT_BLOCKS_PER_STEP
    step_rows = nb * tr * parts

    def expert_block(c):
        return lambda i, be, bv, nu: (layer, be[jnp.minimum(i * nb + c, nu[0] - 1)], 0, 0)

    weight_specs, weight_scratch = [], []
    for c in range(nb):
        weight_specs += [pl.BlockSpec((1, 1, d, f), expert_block(c)), pl.BlockSpec((1, 1, d, f), expert_block(c)),
                         pl.BlockSpec((1, 1, f, d), expert_block(c))]
        weight_scratch += [pltpu.VMEM((d, f), _BF16), pltpu.VMEM((d, f), _BF16), pltpu.VMEM((f, d), _BF16)]
    y = pl.pallas_call(
        _expert_kernel,
        grid_spec=pltpu.PrefetchScalarGridSpec(
            num_scalar_prefetch=3,
            grid=(n_blocks // nb,),
            in_specs=[pl.BlockSpec((step_rows, LANES),
                                   lambda i, be, bv, nu: (jnp.minimum(i, (nu[0] - 1) // nb), 0))] + weight_specs,
            out_specs=pl.BlockSpec((step_rows, LANES), lambda i, be, bv, nu: (i, 0)),
            scratch_shapes=weight_scratch),
        out_shape=jax.ShapeDtypeStruct((n_rows * parts, LANES), jnp.uint32),
        compiler_params=_params(1),
        name="moe_experts",
    )(block_e, block_valid, n_used, xs, *([w_gate, w_up, w_down] * nb))

    return dest, y


def _combine_operands(dest, rf, h, y, d):
    tm = TOKEN_TILE
    parts = d // (2 * LANES)
    in_specs = [pl.BlockSpec(memory_space=pl.ANY),
                pl.BlockSpec((1, 8, tm), lambda i: (i, 0, 0)),
                pl.BlockSpec((tm, d), lambda i: (i, 0)),
                pl.BlockSpec(memory_space=pl.ANY)]
    scratch = [pltpu.SMEM((4 * tm,), jnp.int32),
               pltpu.VMEM((2 * tm * parts, LANES), jnp.uint32), pltpu.VMEM((2 * tm * parts, LANES), jnp.uint32),
               pltpu.SemaphoreType.DMA((2,)), pltpu.SemaphoreType.DMA((2,))]
    return (dest, rf, h, y), in_specs, scratch


def kernel(x, meta_tokens, a_norm, a_w, a_scale, b_norm, b_w_dq, b_q_norm, b_w_uq, b_w_o, kv_norm, w_dkv,
           kv_lat_norm, w_uk, w_uv, ffn_norm, router_g, router_g_bias, router_e, router_e_bias, w_gate, w_up,
           w_down, final_norm):
    bsz, seq, d = x.shape
    tm = TOKEN_TILE
    assert seq % tm == 0 and seq % ATTN_Q_TILE == 0 and ATTN_Q_TILE == 2 * ATTN_K_TILE and tm % ATTN_K_TILE == 0
    kt = tm // ATTN_K_TILE
    assert d % (LANES * len(POOL_WINDOWS)) == 0 and N_META == max(POOL_WINDOWS) and N_META <= tm
    parts = d // (2 * LANES)
    n_tok = bsz * seq
    n_real_tiles = n_tok // tm
    tiles_per_batch = seq // tm
    n_tiles = n_real_tiles + 1
    gd = d // len(POOL_WINDOWS)
    row = lambda v: v.reshape(1, -1).astype(_F32)

    x2 = x.reshape(n_tok, d)
    meta_pad = jnp.concatenate([meta_tokens, jnp.zeros((tm - N_META, d), x.dtype)], axis=0)

    r = np.arange(tm)[:, None]
    cidx = np.arange(tm)[None, :]
    pb = min(POOL_BLOCK, tm)
    pm = np.stack([((r[:pb] - cidx[:, :pb] >= 0) & (r[:pb] - cidx[:, :pb] < w)) for w in POOL_WINDOWS]).astype(np.float32)
    hc = np.arange(N_META)[None, :]
    ph = np.stack([(r[:N_META] + N_META - hc < w) for w in POOL_WINDOWS]).astype(np.float32)
    triu = (r < cidx).astype(np.float32)
    pm, ph, triu = (jnp.asarray(a, dtype=_BF16) for a in (pm, ph, triu))

    wrt0, br0 = _router_operands(router_g[0], router_g_bias[0], router_e[0], router_e_bias[0])
    route_shapes, route_specs = _route_out(n_tiles, tm)
    tile_or_last = lambda i: (jnp.minimum(i, n_real_tiles - 1), 0)
    halo_blocks = tm // N_META
    h1, xn1, ri1, rf1, cnt1 = pl.pallas_call(
        functools.partial(_mixer_kernel, n_real_tiles=n_real_tiles, tiles_per_batch=tiles_per_batch),
        grid=(n_tiles,),
        in_specs=[pl.BlockSpec((tm, d), tile_or_last),
                  pl.BlockSpec((N_META, d), lambda i: (jnp.clip(i * halo_blocks - 1, 0, n_tok // N_META - 1), 0)),
                  _full((tm, d)), _full((N_META, d)), _full((1, d)),
                  _full((len(POOL_WINDOWS), gd, gd)), _full((1, d)), _full((1, d)),
                  _full((ROUTER_ROWS, d)), _full((ROUTER_ROWS, 1)),
                  _full(pm.shape), _full(ph.shape), _full((tm, tm))],
        out_specs=[pl.BlockSpec((tm, d), lambda i: (i, 0)),
                   pl.BlockSpec((tm * parts, LANES), lambda i: (i, 0))] + route_specs,
        out_shape=[jax.ShapeDtypeStruct((n_tiles * tm, d), _F32),
                   jax.ShapeDtypeStruct((n_tiles * tm * parts, LANES), jnp.uint32)] + route_shapes,
        scratch_shapes=[pltpu.VMEM((N_EXPERTS, 1), _F32), pltpu.VMEM((ROUTER_SPLIT_ROWS, d), _BF16)],
        compiler_params=_params(1),
        name="pool_mixer_router",
    )(x2, x2, meta_pad, meta_tokens, row(a_norm[0]), a_w[0].astype(_BF16), row(a_scale[0]), row(ffn_norm[0]),
      wrt0, br0, pm, ph, triu)

    dest1, y1 = _moe_experts(xn1, ri1, cnt1, w_gate, w_up, w_down, 0, n_tiles, n_real_tiles, d)

    pos = jnp.concatenate([jnp.arange(seq, dtype=_F32) + N_META, jnp.arange(tm, dtype=_F32)])
    inv_freq = ROPE_THETA ** (-jnp.arange(0, QK_ROPE_DIM, 2, dtype=_F32) / QK_ROPE_DIM)
    ang = pos[:, None] * inv_freq[None, :]
    cos_t = jnp.tile(jnp.cos(ang), (1, 4))
    sin_t = jnp.tile(jnp.concatenate([-jnp.sin(ang), jnp.sin(ang)], axis=1), (1, 2))

    wdkv = jnp.concatenate([w_dkv, w_dkv[:, KV_LORA_RANK:]], axis=1).astype(_BF16)
    wuk = w_uk.reshape(KV_LORA_RANK, N_HEADS * QK_NOPE_DIM).astype(_BF16)
    wuv = w_uv.reshape(KV_LORA_RANK, N_HEADS * V_HEAD_DIM).T.astype(_BF16)
    wuq = b_w_uq[0]
    q_rank = wuq.shape[0]
    wuq = jnp.concatenate([wuq[:, :, :QK_NOPE_DIM].reshape(q_rank, -1),
                           wuq[:, :, QK_NOPE_DIM:].reshape(q_rank, -1)], axis=1).astype(_BF16)
    wdq = b_w_dq[0].astype(_BF16)
    pos_tile = lambda i: (jnp.where(i < n_real_tiles, i % tiles_per_batch, tiles_per_batch), 0)
    head_tile = lambda i: (0, i, 0)
    operands, in_specs, scratch = _combine_operands(dest1, rf1, h1, y1, d)
    h2, q, k, v = pl.pallas_call(
        functools.partial(_combine_proj_kernel, tm=tm, n_tiles=n_tiles),
        grid=(n_tiles,),
        in_specs=in_specs + [
            pl.BlockSpec((tm, 128), pos_tile), pl.BlockSpec((tm, 128), pos_tile),
            _full((1, d)), _full(wdkv.shape), _full((1, KV_LORA_RANK)), _full(wuk.shape), _full(wuv.shape),
            _full((1, d)), _full(wdq.shape), _full((1, q_rank)), _full(wuq.shape)],
        out_specs=[pl.BlockSpec((tm, d), lambda i: (i, 0)),
                   pl.BlockSpec((N_HEADS, tm, QK_DIM), head_tile),
                   pl.BlockSpec((N_HEADS, tm, QK_DIM), head_tile),
                   pl.BlockSpec((N_HEADS, kt, V_HEAD_DIM, ATTN_K_TILE), lambda i: (0, i, 0, 0))],
        out_shape=[jax.ShapeDtypeStruct((n_tiles * tm, d), _F32),
                   jax.ShapeDtypeStruct((N_HEADS, n_tiles * tm, QK_DIM), _BF16),
                   jax.ShapeDtypeStruct((N_HEADS, n_tiles * tm, QK_DIM), _BF16),
                   jax.ShapeDtypeStruct((N_HEADS, n_tiles * kt, V_HEAD_DIM, ATTN_K_TILE), _BF16)],
        scratch_shapes=scratch,
        compiler_params=_params(1, disable_bounds_checks=True),
        name="moe_combine_latent_qkv",
    )(*operands, cos_t, sin_t, row(kv_norm), wdkv, row(kv_lat_norm), wuk, wuv,
      row(b_norm[0]), wdq, row(b_q_norm[0]), wuq)

    tq, tk, hp = ATTN_Q_TILE, ATTN_K_TILE, ATTN_HEADS_PER_STEP
    n_q = seq // tq
    meta_block = n_tok // N_META
    o = pl.pallas_call(
        functools.partial(_attn_kernel, tq=tq, tk=tk),
        grid=(bsz, N_HEADS // hp, n_q),
        in_specs=[pl.BlockSpec((hp, tq, QK_DIM), lambda b, hg, j: (hg, b * n_q + j, 0)),
                  pl.BlockSpec((hp, seq, QK_DIM), lambda b, hg, j: (hg, b, 0)),
                  pl.BlockSpec((hp, seq // tk, V_HEAD_DIM, tk), lambda b, hg, j: (hg, b, 0, 0)),
                  pl.BlockSpec((hp, N_META, QK_DIM), lambda b, hg, j: (hg, meta_block, 0)),
                  pl.BlockSpec((hp, 1, V_HEAD_DIM, tk), lambda b, hg, j: (hg, n_real_tiles * kt, 0, 0))],
        out_specs=pl.BlockSpec((tq, hp * V_HEAD_DIM), lambda b, hg, j: (b * n_q + j, hg)),
        out_shape=jax.ShapeDtypeStruct((n_tok, N_HEADS * V_HEAD_DIM), _BF16),
        scratch_shapes=[pltpu.VMEM((hp, tk, tq), _F32), pltpu.VMEM((hp, tk, tq), _F32),
                        pltpu.VMEM((hp, V_HEAD_DIM, tq), _F32)],
        compiler_params=_params(3),
        name="causal_attention",
    )(q, k, v, k, v)

    wrt1, br1 = _router_operands(router_g[1], router_g_bias[1], router_e[1], router_e_bias[1])
    route_shapes, route_specs = _route_out(n_real_tiles, tm)
    h3, xn2, ri2, rf2, cnt2 = pl.pallas_call(
        _oproj_kernel,
        grid=(n_real_tiles,),
        in_specs=[pl.BlockSpec((tm, N_HEADS * V_HEAD_DIM), lambda i: (i, 0)),
                  pl.BlockSpec((tm, d), lambda i: (i, 0)),
                  _full((N_HEADS * V_HEAD_DIM, d)), _full((1, d)),
                  _full((ROUTER_ROWS, d)), _full((ROUTER_ROWS, 1)), _full((tm, tm))],
        out_specs=[pl.BlockSpec((tm, d), lambda i: (i, 0)),
                   pl.BlockSpec((tm * parts, LANES), lambda i: (i, 0))] + route_specs,
        out_shape=[jax.ShapeDtypeStruct((n_tok, d), _F32),
                   jax.ShapeDtypeStruct((n_tok * parts, LANES), jnp.uint32)] + route_shapes,
        scratch_shapes=[pltpu.VMEM((N_EXPERTS, 1), _F32), pltpu.VMEM((ROUTER_SPLIT_ROWS, d), _BF16)],
        compiler_params=_params(1),
        name="attn_out_router",
    )(o, h2, b_w_o[0].astype(_BF16), row(ffn_norm[1]), wrt1, br1, triu)

    dest2, y2 = _moe_experts(xn2, ri2, cnt2, w_gate, w_up, w_down, 1, n_real_tiles, n_real_tiles, d)
    operands, in_specs, scratch = _combine_operands(dest2, rf2, h3, y2, d)
    out = pl.pallas_call(
        functools.partial(_combine_norm_kernel, tm=tm, n_tiles=n_real_tiles),
        grid=(n_real_tiles,),
        in_specs=in_specs + [_full((1, d))],
        out_specs=pl.BlockSpec((tm, d), lambda i: (i, 0)),
        out_shape=jax.ShapeDtypeStruct((n_tok, d), _F32),
        scratch_shapes=scratch,
        compiler_params=_params(1, disable_bounds_checks=True),
        name="moe_combine_final_norm",
    )(*operands, row(final_norm))
    return out.reshape(bsz, seq, d)
```
